```python
import jax, jax.numpy as jnp
from jax import lax
import numpy as np

D_MODEL = 1024
BATCH = 8
SEQ = 8192
DEPTH = 1

CONV_WIDTH = 1024
CONV_SIZE = 31
HEAD_DIM = 64
HEADS_PER_GROUP = 8
DILATION_GROUPS = ((128, 1), (512, 4), (2048, 16))
N_GROUPS = len(DILATION_GROUPS)
N_ATT_HEADS = N_GROUPS * HEADS_PER_GROUP
ATT_QKV = N_ATT_HEADS * HEAD_DIM
ATT_OUT = HEADS_PER_GROUP * HEAD_DIM
ROT_DIM = HEAD_DIM // 4
ROPE_THETA = 500000.0
BLOCK = 128
MAX_POS_OFFSET = 4096
EPS = 1e-6
NEG_INF = -1e30

IN_SPLITS = (CONV_WIDTH, CONV_WIDTH, CONV_WIDTH,
             ATT_QKV, ATT_QKV, ATT_QKV, ATT_OUT,
             D_MODEL, D_MODEL)
IN_COLS = sum(IN_SPLITS)

kernel_name = "hybrid_conformer_conv_dilated_attention_gated_merge"


def _rmsnorm(x, g):
    xf = x.astype(jnp.float32)
    y = xf * lax.rsqrt(jnp.mean(xf * xf, axis=-1, keepdims=True) + EPS)
    return (y * g.astype(jnp.float32)).astype(x.dtype)


def _layernorm(x, g, b):
    xf = x.astype(jnp.float32)
    mu = jnp.mean(xf, axis=-1, keepdims=True)
    var = jnp.mean(jnp.square(xf - mu), axis=-1, keepdims=True)
    y = (xf - mu) * lax.rsqrt(var + EPS)
    return (y * g.astype(jnp.float32) + b.astype(jnp.float32)).astype(x.dtype)


def _partial_rope(t, positions):
    half = ROT_DIM // 2
    inv_freq = ROPE_THETA ** (-(jnp.arange(half, dtype=jnp.float32) * 2.0 / ROT_DIM))
    ang = positions.astype(jnp.float32)[..., None] * inv_freq
    cos = jnp.cos(ang)[:, :, None, :]
    sin = jnp.sin(ang)[:, :, None, :]
    tf = t.astype(jnp.float32)
    t1, t2 = tf[..., :half], tf[..., half:ROT_DIM]
    out = jnp.concatenate([t1 * cos - t2 * sin, t2 * cos + t1 * sin, tf[..., ROT_DIM:]], axis=-1)
    return out.astype(t.dtype)


def _dilated_window_group(q, k, v, window, dilation):
    b, s, h, e = q.shape
    L = s // dilation
    w_sub = window // dilation
    nb = -(-L // BLOCK)
    lp = nb * BLOCK

    def to_sub(t):
        return t.reshape(b, L, dilation, h, e).transpose(0, 2, 3, 1, 4)

    qs, ks, vs = to_sub(q), to_sub(k), to_sub(v)
    qs = jnp.pad(qs, ((0, 0), (0, 0), (0, 0), (0, lp - L), (0, 0)))
    ks = jnp.pad(ks, ((0, 0), (0, 0), (0, 0), (BLOCK, lp - L), (0, 0)))
    vs = jnp.pad(vs, ((0, 0), (0, 0), (0, 0), (BLOCK, lp - L), (0, 0)))
    qb = qs.reshape(b, dilation, h, nb, BLOCK, e)

    def band(t):
        prev = t[:, :, :, :lp].reshape(b, dilation, h, nb, BLOCK, e)
        cur = t[:, :, :, BLOCK:].reshape(b, dilation, h, nb, BLOCK, e)
        return jnp.concatenate([prev, cur], axis=-2)

    kb, vb = band(ks), band(vs)
    scores = jnp.einsum('bdhnqe,bdhnke->bdhnqk', qb.astype(jnp.float32),
                        kb.astype(jnp.float32)) * (e ** -0.5)
    qi = jnp.arange(BLOCK)[:, None]
    kj = jnp.arange(2 * BLOCK)[None, :]
    dist = qi + BLOCK - kj
    key_idx = jnp.arange(nb)[:, None, None] * BLOCK - BLOCK + kj[None]
    mask = (dist >= 0) & (dist <= w_sub) & (key_idx >= 0)
    scores = jnp.where(mask, scores, NEG_INF)
    m = jnp.max(scores, axis=-1)
    p = jnp.exp(scores - m[..., None])
    den = jnp.sum(p, axis=-1)
    o = jnp.einsum('bdhnqk,bdhnke->bdhnqe', p, vb.astype(jnp.float32)) / den[..., None]

    def from_sub(t):
        tail = t.shape[5:]
        t = t.reshape((b, dilation, h, lp) + tail)[:, :, :, :L]
        t = jnp.moveaxis(t, 3, 1)
        return t.reshape((b, s, h) + tail)

    return from_sub(o), from_sub(m), from_sub(den)


def _fwd_setup_inputs(seed: int = 0) -> dict:
    key = jax.random.key(seed)
    ks = jax.random.split(key, 16)
    f32 = jnp.float32
    x = jax.random.normal(ks[0], (BATCH, SEQ, D_MODEL), f32)
    c = jax.random.normal(ks[1], (BATCH, D_MODEL), f32)
    positions = (jnp.arange(SEQ, dtype=jnp.int32)[None, :]
                 + jax.random.randint(ks[2], (BATCH, 1), 0, MAX_POS_OFFSET, dtype=jnp.int32))
    norm_g = 1.0 + 0.05 * jax.random.normal(ks[3], (DEPTH, D_MODEL), f32)
    w_ada = 0.5 * D_MODEL ** -0.5 * jax.random.normal(ks[4], (DEPTH, D_MODEL, 3 * D_MODEL), f32)
    b_ada = 0.02 * jax.random.normal(ks[5], (DEPTH, 3 * D_MODEL), f32)
    w_in = D_MODEL ** -0.5 * jax.random.normal(ks[6], (DEPTH, D_MODEL, IN_COLS), f32)
    conv_w = CONV_SIZE ** -0.5 * jax.random.normal(ks[7], (DEPTH, CONV_SIZE, CONV_WIDTH), f32)
    conv_b = 0.02 * jax.random.normal(ks[8], (DEPTH, CONV_WIDTH), f32)
    conv_ln_g = 1.0 + 0.05 * jax.random.normal(ks[9], (DEPTH, CONV_WIDTH), f32)
    conv_ln_b = 0.02 * jax.random.normal(ks[10], (DEPTH, CONV_WIDTH), f32)
    w_conv_out = CONV_WIDTH ** -0.5 * jax.random.normal(ks[11], (DEPTH, CONV_WIDTH, D_MODEL), f32)
    w_att_out = ATT_OUT ** -0.5 * jax.random.normal(ks[12], (DEPTH, ATT_OUT, D_MODEL), f32)
    w_o = D_MODEL ** -0.5 * jax.random.normal(ks[13], (DEPTH, D_MODEL, D_MODEL), f32)
    final_g = 1.0 + 0.05 * jax.random.normal(ks[14], (D_MODEL,), f32)
    return {"x": x, "c": c, "positions": positions, "norm_g": norm_g,
            "w_ada": w_ada, "b_ada": b_ada, "w_in": w_in, "conv_w": conv_w,
            "conv_b": conv_b, "conv_ln_g": conv_ln_g, "conv_ln_b": conv_ln_b,
            "w_conv_out": w_conv_out, "w_att_out": w_att_out, "w_o": w_o,
            "final_g": final_g}


def _fwd_reference(x, c, positions, norm_g, w_ada, b_ada, w_in, conv_w, conv_b, conv_ln_g,
              conv_ln_b, w_conv_out, w_att_out, w_o, final_g):
    b, s, _ = x.shape
    split_idx = np.cumsum(IN_SPLITS)[:-1].tolist()
    for layer in range(DEPTH):
        mod = c @ w_ada[layer] + b_ada[layer]
        shift, scale, gate = [t[:, None, :] for t in jnp.split(mod, 3, axis=-1)]
        h = _rmsnorm(x, norm_g[layer]) * (1.0 + scale) + shift

        proj = h @ w_in[layer]
        (glu_a, glu_b, z_conv, q, k, v, z_att, g_conv, g_att) = jnp.split(proj, split_idx, axis=-1)

        u = glu_a * jax.nn.sigmoid(glu_b)
        u = lax.conv_general_dilated(
            u, conv_w[layer][:, None, :].astype(u.dtype), window_strides=(1,),
            padding=[(CONV_SIZE - 1, 0)], dimension_numbers=('NWC', 'WIO', 'NWC'),
            feature_group_count=CONV_WIDTH) + conv_b[layer]
        u = jax.nn.silu(_layernorm(u, conv_ln_g[layer], conv_ln_b[layer]))
        y_conv = (u * jax.nn.silu(z_conv)) @ w_conv_out[layer]

        q = _partial_rope(q.reshape(b, s, N_ATT_HEADS, HEAD_DIM), positions)
        k = _partial_rope(k.reshape(b, s, N_ATT_HEADS, HEAD_DIM), positions)
        v = v.reshape(b, s, N_ATT_HEADS, HEAD_DIM)
        outs, maxes, dens = [], [], []
        for gi, (window, dilation) in enumerate(DILATION_GROUPS):
            sl = slice(gi * HEADS_PER_GROUP, (gi + 1) * HEADS_PER_GROUP)
            o_g, m_g, d_g = _dilated_window_group(q[:, :, sl], k[:, :, sl], v[:, :, sl],
                                                  window, dilation)
            outs.append(o_g); maxes.append(m_g); dens.append(d_g)
        m_all = jnp.maximum(jnp.maximum(maxes[0], maxes[1]), maxes[2])
        wts = [d_g * jnp.exp(m_g - m_all) for m_g, d_g in zip(maxes, dens)]
        w_sum = wts[0] + wts[1] + wts[2]
        att = (wts[0][..., None] * outs[0] + wts[1][..., None] * outs[1]
               + wts[2][..., None] * outs[2]) / w_sum[..., None]
        att = att.reshape(b, s, ATT_OUT).astype(x.dtype)
        y_att = (att * jax.nn.silu(z_att)) @ w_att_out[layer]

        merged = jax.nn.sigmoid(g_conv) * y_conv + jax.nn.sigmoid(g_att) * y_att
        x = x + gate * (merged @ w_o[layer])
    return _rmsnorm(x, final_g)


import jax as _jax
import jax.numpy as _jnp

TWIN_FORMAT = 'train_step'
FWD_PARAMS = ['x', 'c', 'positions', 'norm_g', 'w_ada', 'b_ada', 'w_in', 'conv_w', 'conv_b', 'conv_ln_g', 'conv_ln_b', 'w_conv_out', 'w_att_out', 'w_o', 'final_g']
TWIN_WEIGHTS = ['norm_g', 'w_ada', 'b_ada', 'w_in', 'conv_w', 'conv_b', 'conv_ln_g', 'conv_ln_b', 'w_conv_out', 'w_att_out', 'w_o', 'final_g']
TWIN_DIFF_INPUT = 'x'
TWIN_INPUTS = ['x', 'c', 'positions', 'norm_g', 'w_ada', 'b_ada', 'w_in', 'conv_w', 'conv_b', 'conv_ln_g', 'conv_ln_b', 'w_conv_out', 'w_att_out', 'w_o', 'final_g', 'loss_target', 'm_norm_g', 'm_w_ada', 'm_b_ada', 'm_w_in', 'm_conv_w', 'm_conv_b', 'm_conv_ln_g', 'm_conv_ln_b', 'm_w_conv_out', 'm_w_att_out', 'm_w_o', 'm_final_g', 'v_norm_g', 'v_w_ada', 'v_b_ada', 'v_w_in', 'v_conv_w', 'v_conv_b', 'v_conv_ln_g', 'v_conv_ln_b', 'v_w_conv_out', 'v_w_att_out', 'v_w_o', 'v_final_g']
TWIN_OUTPUTS = ['loss', 'grad_x', 'grad_norm_g', 'grad_w_ada', 'grad_b_ada', 'grad_w_in', 'grad_conv_w', 'grad_conv_b', 'grad_conv_ln_g', 'grad_conv_ln_b', 'grad_w_conv_out', 'grad_w_att_out', 'grad_w_o', 'grad_final_g', 'delta_norm_g', 'delta_w_ada', 'delta_b_ada', 'delta_w_in', 'delta_conv_w', 'delta_conv_b', 'delta_conv_ln_g', 'delta_conv_ln_b', 'delta_w_conv_out', 'delta_w_att_out', 'delta_w_o', 'delta_final_g', 'new_m_norm_g', 'new_m_w_ada', 'new_m_b_ada', 'new_m_w_in', 'new_m_conv_w', 'new_m_conv_b', 'new_m_conv_ln_g', 'new_m_conv_ln_b', 'new_m_w_conv_out', 'new_m_w_att_out', 'new_m_w_o', 'new_m_final_g', 'new_v_norm_g', 'new_v_w_ada', 'new_v_b_ada', 'new_v_w_in', 'new_v_conv_w', 'new_v_conv_b', 'new_v_conv_ln_g', 'new_v_conv_ln_b', 'new_v_w_conv_out', 'new_v_w_att_out', 'new_v_w_o', 'new_v_final_g']
TWIN_LEAF_KINDS = {'loss': 'loss', 'grad_x': 'grad_x', 'grad_norm_g': 'grad_w', 'grad_w_ada': 'grad_w', 'grad_b_ada': 'grad_w', 'grad_w_in': 'grad_w', 'grad_conv_w': 'grad_w', 'grad_conv_b': 'grad_w', 'grad_conv_ln_g': 'grad_w', 'grad_conv_ln_b': 'grad_w', 'grad_w_conv_out': 'grad_w', 'grad_w_att_out': 'grad_w', 'grad_w_o': 'grad_w', 'grad_final_g': 'grad_w', 'delta_norm_g': 'delta_w', 'delta_w_ada': 'delta_w', 'delta_b_ada': 'delta_w', 'delta_w_in': 'delta_w', 'delta_conv_w': 'delta_w', 'delta_conv_b': 'delta_w', 'delta_conv_ln_g': 'delta_w', 'delta_conv_ln_b': 'delta_w', 'delta_w_conv_out': 'delta_w', 'delta_w_att_out': 'delta_w', 'delta_w_o': 'delta_w', 'delta_final_g': 'delta_w', 'new_m_norm_g': 'new_m', 'new_m_w_ada': 'new_m', 'new_m_b_ada': 'new_m', 'new_m_w_in': 'new_m', 'new_m_conv_w': 'new_m', 'new_m_conv_b': 'new_m', 'new_m_conv_ln_g': 'new_m', 'new_m_conv_ln_b': 'new_m', 'new_m_w_conv_out': 'new_m', 'new_m_w_att_out': 'new_m', 'new_m_w_o': 'new_m', 'new_m_final_g': 'new_m', 'new_v_norm_g': 'new_v', 'new_v_w_ada': 'new_v', 'new_v_b_ada': 'new_v', 'new_v_w_in': 'new_v', 'new_v_conv_w': 'new_v', 'new_v_conv_b': 'new_v', 'new_v_conv_ln_g': 'new_v', 'new_v_conv_ln_b': 'new_v', 'new_v_w_conv_out': 'new_v', 'new_v_w_att_out': 'new_v', 'new_v_w_o': 'new_v', 'new_v_final_g': 'new_v'}


def _forward(args):
    return _fwd_reference(*[args[k] for k in FWD_PARAMS])


def _output_shape():
    def fwd():
        inp = _fwd_setup_inputs(0)
        return _fwd_reference(*[inp[k] for k in FWD_PARAMS])
    out = _jax.eval_shape(fwd)
    return out.shape, out.dtype

N_MICROBATCH = 1
ADAM_LR = 0.001
ADAM_B1 = 0.9
ADAM_B2 = 0.999
ADAM_EPS = 1e-08
ADAM_WD = 0.01
ADAM_STEP = 10
PER_EXAMPLE_BATCH_AXIS = {'x': 0, 'c': 0, 'positions': 0, 'loss_target': 0}
SHARED_INPUTS = []
_WEIGHT_DTYPES = {'norm_g': _jnp.float32, 'w_ada': _jnp.float32, 'b_ada': _jnp.float32, 'w_in': _jnp.float32, 'conv_w': _jnp.float32, 'conv_b': _jnp.float32, 'conv_ln_g': _jnp.float32, 'conv_ln_b': _jnp.float32, 'w_conv_out': _jnp.float32, 'w_att_out': _jnp.float32, 'w_o': _jnp.float32, 'final_g': _jnp.float32}
MOMENT_SCALE = {'norm_g': 6.094189e-02, 'w_ada': 7.011408e-02, 'b_ada': 6.899225e-02, 'w_in': 2.211744e-02, 'conv_w': 3.385772e-02, 'conv_b': 5.395154e-02, 'conv_ln_g': 3.811942e-02, 'conv_ln_b': 3.297290e-02, 'w_conv_out': 3.170817e-02, 'w_att_out': 2.549804e-02, 'w_o': 4.078239e-02, 'final_g': 6.412782e+01}


def _to_microbatches(a, axis):
    t = _jnp.moveaxis(a, axis, 0)
    t = t.reshape((N_MICROBATCH, t.shape[0] // N_MICROBATCH) + t.shape[1:])
    return _jnp.moveaxis(t, 1, axis + 1)


def setup_inputs(seed: int = 0) -> dict:
    inp = _fwd_setup_inputs(seed)
    key = _jax.random.fold_in(_jax.random.key(seed), 7919)
    shape, _ = _output_shape()
    out = dict(inp)
    out["loss_target"] = _jax.random.normal(_jax.random.fold_in(key, 0), shape, _jnp.float32)
    for i, name in enumerate(TWIN_WEIGHTS):
        w = inp[name].astype(_jnp.float32)
        if MOMENT_SCALE is None:
            s = _jnp.sqrt(_jnp.mean(_jnp.square(w)) + 1e-30)
        else:
            s = MOMENT_SCALE[name]
        km, kv = _jax.random.split(_jax.random.fold_in(key, i + 1))
        out[name] = w
        out["m_" + name] = s * _jax.random.normal(km, w.shape, _jnp.float32)
        out["v_" + name] = (s * s) * _jax.random.uniform(kv, w.shape, _jnp.float32, 0.5, 1.5)
    if N_MICROBATCH > 1:
        for name, axis in PER_EXAMPLE_BATCH_AXIS.items():
            out[name] = _to_microbatches(out[name], axis)
    return {'x': out['x'], 'c': out['c'], 'positions': out['positions'], 'norm_g': out['norm_g'], 'w_ada': out['w_ada'], 'b_ada': out['b_ada'], 'w_in': out['w_in'], 'conv_w': out['conv_w'], 'conv_b': out['conv_b'], 'conv_ln_g': out['conv_ln_g'], 'conv_ln_b': out['conv_ln_b'], 'w_conv_out': out['w_conv_out'], 'w_att_out': out['w_att_out'], 'w_o': out['w_o'], 'final_g': out['final_g'], 'loss_target': out['loss_target'], 'm_norm_g': out['m_norm_g'], 'm_w_ada': out['m_w_ada'], 'm_b_ada': out['m_b_ada'], 'm_w_in': out['m_w_in'], 'm_conv_w': out['m_conv_w'], 'm_conv_b': out['m_conv_b'], 'm_conv_ln_g': out['m_conv_ln_g'], 'm_conv_ln_b': out['m_conv_ln_b'], 'm_w_conv_out': out['m_w_conv_out'], 'm_w_att_out': out['m_w_att_out'], 'm_w_o': out['m_w_o'], 'm_final_g': out['m_final_g'], 'v_norm_g': out['v_norm_g'], 'v_w_ada': out['v_w_ada'], 'v_b_ada': out['v_b_ada'], 'v_w_in': out['v_w_in'], 'v_conv_w': out['v_conv_w'], 'v_conv_b': out['v_conv_b'], 'v_conv_ln_g': out['v_conv_ln_g'], 'v_conv_ln_b': out['v_conv_ln_b'], 'v_w_conv_out': out['v_w_conv_out'], 'v_w_att_out': out['v_w_att_out'], 'v_w_o': out['v_w_o'], 'v_final_g': out['v_final_g']}


def _loss(weights, diff, rest, loss_target):
    with _jax.named_scope("forward"):
        args = {**rest, TWIN_DIFF_INPUT: diff, **{k: w.astype(_WEIGHT_DTYPES[k]) for k, w in weights.items()}}
        y = _forward(args)
    with _jax.named_scope("loss_head"):
        err = _jnp.square(y.astype(_jnp.float32) - loss_target)
        return 0.5 * _jnp.sum(_jnp.mean(err, axis=-1)) if err.ndim else 0.5 * err


def _adamw(w, g, m, v):
    m = ADAM_B1 * m + (1.0 - ADAM_B1) * g
    v = ADAM_B2 * v + (1.0 - ADAM_B2) * _jnp.square(g)
    m_hat = m / (1.0 - ADAM_B1 ** ADAM_STEP)
    v_hat = v / (1.0 - ADAM_B2 ** ADAM_STEP)
    delta = -ADAM_LR * (m_hat / (_jnp.sqrt(v_hat) + ADAM_EPS) + ADAM_WD * w)
    return delta, m, v


def reference(x, c, positions, norm_g, w_ada, b_ada, w_in, conv_w, conv_b, conv_ln_g, conv_ln_b, w_conv_out, w_att_out, w_o, final_g, loss_target, m_norm_g, m_w_ada, m_b_ada, m_w_in, m_conv_w, m_conv_b, m_conv_ln_g, m_conv_ln_b, m_w_conv_out, m_w_att_out, m_w_o, m_final_g, v_norm_g, v_w_ada, v_b_ada, v_w_in, v_conv_w, v_conv_b, v_conv_ln_g, v_conv_ln_b, v_w_conv_out, v_w_att_out, v_w_o, v_final_g):
    given = dict(x=x, c=c, positions=positions, norm_g=norm_g, w_ada=w_ada, b_ada=b_ada, w_in=w_in, conv_w=conv_w, conv_b=conv_b, conv_ln_g=conv_ln_g, conv_ln_b=conv_ln_b, w_conv_out=w_conv_out, w_att_out=w_att_out, w_o=w_o, final_g=final_g, loss_target=loss_target, m_norm_g=m_norm_g, m_w_ada=m_w_ada, m_b_ada=m_b_ada, m_w_in=m_w_in, m_conv_w=m_conv_w, m_conv_b=m_conv_b, m_conv_ln_g=m_conv_ln_g, m_conv_ln_b=m_conv_ln_b, m_w_conv_out=m_w_conv_out, m_w_att_out=m_w_att_out, m_w_o=m_w_o, m_final_g=m_final_g, v_norm_g=v_norm_g, v_w_ada=v_w_ada, v_b_ada=v_b_ada, v_w_in=v_w_in, v_conv_w=v_conv_w, v_conv_b=v_conv_b, v_conv_ln_g=v_conv_ln_g, v_conv_ln_b=v_conv_ln_b, v_w_conv_out=v_w_conv_out, v_w_att_out=v_w_att_out, v_w_o=v_w_o, v_final_g=v_final_g)
    weights = {n: given[n] for n in TWIN_WEIGHTS}
    shared = {n: given[n] for n in SHARED_INPUTS}
    per_example = {n: given[n] for n in ['x', 'c', 'positions']}
    grad_fn = _jax.value_and_grad(_loss, argnums=(0, 1))

    def one_microbatch(ex, loss_target):
        ex = dict(ex)
        diff = ex.pop(TWIN_DIFF_INPUT)
        return grad_fn(weights, diff, {**shared, **ex}, loss_target)

    if N_MICROBATCH == 1:
        loss, (grad_w, grad_x) = one_microbatch(per_example, given["loss_target"])
    else:
        def body(carry, xs):
            loss_sum, grad_sum = carry
            l_k, (gw_k, gx_k) = one_microbatch(xs[0], xs[1])
            with _jax.named_scope("update"):
                return (loss_sum + l_k, _jax.tree.map(_jnp.add, grad_sum, gw_k)), gx_k

        init = (_jnp.zeros((), _jnp.float32), _jax.tree.map(_jnp.zeros_like, weights))
        (loss, grad_w), grad_x = _jax.lax.scan(body, init, (per_example, given["loss_target"]))
    with _jax.named_scope("update"):
        delta_w, new_m, new_v = {}, {}, {}
        for n in TWIN_WEIGHTS:
            delta_w[n], new_m[n], new_v[n] = _adamw(weights[n], grad_w[n], given["m_" + n], given["v_" + n])
    return (loss, grad_x, *[grad_w[n] for n in TWIN_WEIGHTS], *[delta_w[n] for n in TWIN_WEIGHTS],
            *[new_m[n] for n in TWIN_WEIGHTS], *[new_v[n] for n in TWIN_WEIGHTS])
```

```python
import functools

import jax
import jax.numpy as jnp
from jax import lax
from jax.experimental import pallas as pl
from jax.experimental.pallas import tpu as pltpu

F32 = jnp.float32
BF16 = jnp.bfloat16

N_DEV = 8
D = 1024
N_COL = 10240
C_A, C_B, C_ZC, C_Q, C_K, C_V, C_ZA, C_GC, C_GA = 0, 1024, 2048, 3072, 4608, 6144, 7680, 8192, 9216
QKV = 1536
ATT = 512
HEAD = 64
BLK = 128
GROUPS = ((0, 1), (1, 4), (2, 16))
CONV_K = 31
CONV_KP = 32
HALO = 32
EPS = 1e-6
NEG_INF = -1e30
ROPE_THETA = 500000.0
SM_SCALE = HEAD ** -0.5

ADAM_LR, ADAM_B1, ADAM_B2, ADAM_EPS, ADAM_WD, ADAM_STEP = 0.001, 0.9, 0.999, 1e-08, 0.01, 10

MESH = pl.DeviceIdType.MESH
ANY = pl.BlockSpec(memory_space=pl.ANY)


def _sig(v):
    return 1.0 / (1.0 + jnp.exp(-v))


def _dsilu(v, s):
    return s * (1.0 + v * (1.0 - s))


def _full(shape):
    return pl.BlockSpec(shape, lambda *_: (0,) * len(shape))


def _rows(tm, width, col=0):
    return pl.BlockSpec((tm, width), lambda i: (i, col))


def _matmul(a, b, *, ta=False, tb=False, out_dtype=F32, tm, tn, tk, name):
    m, k = (a.shape[1], a.shape[0]) if ta else a.shape
    n = b.shape[0] if tb else b.shape[1]
    assert (b.shape[1] if tb else b.shape[0]) == k
    assert m % tm == 0 and n % tn == 0 and k % tk == 0
    nk = k // tk
    dims = (((0 if ta else 1,), (1 if tb else 0,)), ((), ()))
    use_scratch = out_dtype != F32 and nk > 1

    def body(a_ref, b_ref, o_ref, *scratch):
        p = lax.dot_general(a_ref[...], b_ref[...], dims, preferred_element_type=F32)
        if nk == 1:
            o_ref[...] = p.astype(out_dtype)
            return
        acc = scratch[0] if use_scratch else o_ref
        kk = pl.program_id(2)

        @pl.when(kk == 0)
        def _():
            acc[...] = p

        @pl.when(kk > 0)
        def _():
            acc[...] += p

        if use_scratch:
            @pl.when(kk == nk - 1)
            def _():
                o_ref[...] = acc[...].astype(out_dtype)

    a_spec = pl.BlockSpec((tk, tm), lambda i, j, kk: (kk, i)) if ta else pl.BlockSpec((tm, tk), lambda i, j, kk: (i, kk))
    b_spec = pl.BlockSpec((tn, tk), lambda i, j, kk: (j, kk)) if tb else pl.BlockSpec((tk, tn), lambda i, j, kk: (kk, j))
    return pl.pallas_call(
        body, name=name, grid=(m // tm, n // tn, nk),
        in_specs=[a_spec, b_spec],
        out_specs=pl.BlockSpec((tm, tn), lambda i, j, kk: (i, j)),
        out_shape=jax.ShapeDtypeStruct((m, n), out_dtype),
        scratch_shapes=[pltpu.VMEM((tm, tn), F32)] if use_scratch else [],
    )(a, b)


def _me_and_peers():
    x, y, c = lax.axis_index("x"), lax.axis_index("y"), lax.axis_index("c")
    me = 4 * x + 2 * y + c
    peers = []
    for k in range(1, N_DEV):
        px, py, pc = x ^ (k >> 2), y ^ ((k >> 1) & 1), c ^ (k & 1)
        peers.append(((px, py, pc), 4 * px + 2 * py + pc))
    return me, peers


def _allgather_small(v, name):
    r, c = v.shape

    def body(v_ref, out_ref, send_sems, recv_sems):
        me, peers = _me_and_peers()
        out_ref[me] = v_ref[...]
        copies = []
        for k, (dev, _) in enumerate(peers):
            cp = pltpu.make_async_remote_copy(src_ref=v_ref, dst_ref=out_ref.at[me], send_sem=send_sems.at[k],
                                              recv_sem=recv_sems.at[k], device_id=dev, device_id_type=MESH)
            cp.start()
            copies.append(cp)
        for k, (dev, idx) in enumerate(peers):
            pltpu.make_async_remote_copy(src_ref=v_ref, dst_ref=out_ref.at[idx], send_sem=send_sems.at[k],
                                         recv_sem=recv_sems.at[k], device_id=dev, device_id_type=MESH).wait_recv()
        for cp in copies:
            cp.wait_send()

    return pl.pallas_call(
        body, name=name,
        in_specs=[pl.BlockSpec(memory_space=pltpu.VMEM)],
        out_specs=pl.BlockSpec(memory_space=pltpu.VMEM),
        out_shape=jax.ShapeDtypeStruct((N_DEV, r, c), v.dtype),
        scratch_shapes=[pltpu.SemaphoreType.DMA((N_DEV - 1,)), pltpu.SemaphoreType.DMA((N_DEV - 1,))],
    )(v)


def _window(ref, kind, idx, size):
    start = pl.multiple_of(idx * size, size)
    if kind == "rows":
        return ref.at[pl.ds(start, size), :]
    return ref.at[:, pl.ds(start, size)]


_BIG = (("cols", N_COL // N_DEV), ("rows", D // N_DEV), ("cols", D // N_DEV), ("rows", D // N_DEV), ("cols", D // N_DEV))


def _gather_weights(shards):
    full_shapes = []
    for s, (kind, size) in zip(shards, _BIG):
        full_shapes.append(jax.ShapeDtypeStruct((s.shape[0] * N_DEV, s.shape[1]) if kind == "rows"
                                                else (s.shape[0], s.shape[1] * N_DEV), s.dtype))
    nt = len(shards)

    def body(*refs):
        src, dst = refs[:nt], refs[nt:2 * nt]
        send_sems, recv_sems, local_sems = refs[2 * nt:]
        me, peers = _me_and_peers()
        local, sent = [], []
        for t in range(nt):
            kind, size = _BIG[t]
            cp = pltpu.make_async_copy(src[t], _window(dst[t], kind, me, size), local_sems.at[t])
            cp.start()
            local.append(cp)
            for k, (dev, _) in enumerate(peers):
                rc = pltpu.make_async_remote_copy(src_ref=src[t], dst_ref=_window(dst[t], kind, me, size),
                                                  send_sem=send_sems.at[t, k], recv_sem=recv_sems.at[t, k],
                                                  device_id=dev, device_id_type=MESH)
                rc.start()
                sent.append(rc)
        for t in range(nt):
            kind, size = _BIG[t]
            for k, (dev, idx) in enumerate(peers):
                pltpu.make_async_remote_copy(src_ref=src[t], dst_ref=_window(dst[t], kind, idx, size),
                                             send_sem=send_sems.at[t, k], recv_sem=recv_sems.at[t, k],
                                             device_id=dev, device_id_type=MESH).wait_recv()
        for rc in sent:
            rc.wait_send()
        for cp in local:
            cp.wait()

    return pl.pallas_call(
        body, name="gather_weights",
        in_specs=[ANY] * nt, out_specs=[ANY] * nt, out_shape=full_shapes,
        scratch_shapes=[pltpu.SemaphoreType.DMA((nt, N_DEV - 1)), pltpu.SemaphoreType.DMA((nt, N_DEV - 1)),
                        pltpu.SemaphoreType.DMA((nt,))],
    )(*shards)


def _scatter_grads(grads):
    nt = len(grads)
    slot_shapes = []
    for g, (kind, size) in zip(grads, _BIG):
        shard = (size, g.shape[1]) if kind == "rows" else (g.shape[0], size)
        slot_shapes.append(jax.ShapeDtypeStruct((N_DEV,) + shard, g.dtype))

    def body(*refs):
        src, dst = refs[:nt], refs[nt:2 * nt]
        send_sems, recv_sems, local_sems = refs[2 * nt:]
        me, peers = _me_and_peers()
        local, sent = [], []
        for t in range(nt):
            kind, size = _BIG[t]
            cp = pltpu.make_async_copy(_window(src[t], kind, me, size), dst[t].at[me], local_sems.at[t])
            cp.start()
            local.append(cp)
            for k, (dev, idx) in enumerate(peers):
                rc = pltpu.make_async_remote_copy(src_ref=_window(src[t], kind, idx, size), dst_ref=dst[t].at[me],
                                                  send_sem=send_sems.at[t, k], recv_sem=recv_sems.at[t, k],
                                                  device_id=dev, device_id_type=MESH)
                rc.start()
                sent.append(rc)
        for t in range(nt):
            kind, size = _BIG[t]
            for k, (dev, idx) in enumerate(peers):
                pltpu.make_async_remote_copy(src_ref=_window(src[t], kind, me, size), dst_ref=dst[t].at[idx],
                                             send_sem=send_sems.at[t, k], recv_sem=recv_sems.at[t, k],
                                             device_id=dev, device_id_type=MESH).wait_recv()
        for rc in sent:
            rc.wait_send()
        for cp in local:
            cp.wait()

    return pl.pallas_call(
        body, name="scatter_grads",
        in_specs=[ANY] * nt, out_specs=[ANY] * nt, out_shape=slot_shapes,
        scratch_shapes=[pltpu.SemaphoreType.DMA((nt, N_DEV - 1)), pltpu.SemaphoreType.DMA((nt, N_DEV - 1)),
                        pltpu.SemaphoreType.DMA((nt,))],
    )(*grads)


def _adamw_math(w, g, m, v):
    m = ADAM_B1 * m + (1.0 - ADAM_B1) * g
    v = ADAM_B2 * v + (1.0 - ADAM_B2) * (g * g)
    m_hat = m / (1.0 - ADAM_B1 ** ADAM_STEP)
    v_hat = v / (1.0 - ADAM_B2 ** ADAM_STEP)
    delta = -ADAM_LR * (m_hat / (jnp.sqrt(v_hat) + ADAM_EPS) + ADAM_WD * w)
    return delta, m, v


def _sum_adamw(slots, w, m, v, tr, name):
    _, r, c = slots.shape
    assert r % tr == 0

    def body(s_ref, w_ref, m_ref, v_ref, g_ref, d_ref, nm_ref, nv_ref):
        g = s_ref[0]
        for j in range(1, N_DEV):
            g = g + s_ref[j]
        delta, nm, nv = _adamw_math(w_ref[...], g, m_ref[...], v_ref[...])
        g_ref[...] = g
        d_ref[...] = delta
        nm_ref[...] = nm
        nv_ref[...] = nv

    blk = pl.BlockSpec((tr, c), lambda i: (i, 0))
    return pl.pallas_call(
        body, name=name, grid=(r // tr,),
        in_specs=[pl.BlockSpec((N_DEV, tr, c), lambda i: (0, i, 0)), blk, blk, blk],
        out_specs=[blk] * 4, out_shape=[jax.ShapeDtypeStruct((r, c), F32)] * 4,
    )(slots, w, m, v)


def _adamw_small(g, w, m, v, name):
    def body(g_ref, w_ref, m_ref, v_ref, d_ref, nm_ref, nv_ref):
        delta, nm, nv = _adamw_math(w_ref[...], g_ref[...], m_ref[...], v_ref[...])
        d_ref[...] = delta
        nm_ref[...] = nm
        nv_ref[...] = nv

    spec = _full(g.shape)
    return pl.pallas_call(body, name=name, grid=(1,), in_specs=[spec] * 4, out_specs=[spec] * 3,
                          out_shape=[jax.ShapeDtypeStruct(g.shape, F32)] * 3)(g, w, m, v)


def _mod_part(c_all, w_ada_l, b_ada_l):
    n = w_ada_l.shape[1]

    def body(c_ref, w_ref, b_ref, o_ref):
        o_ref[...] = jnp.dot(c_ref[...], w_ref[...], preferred_element_type=F32,
                             precision=lax.Precision.HIGHEST) + b_ref[...]

    return pl.pallas_call(body, name="mod_part", grid=(1,),
                          in_specs=[_full(c_all.shape), _full(w_ada_l.shape), _full(b_ada_l.shape)],
                          out_specs=_full((N_DEV, n)), out_shape=jax.ShapeDtypeStruct((N_DEV, n), F32))(c_all, w_ada_l, b_ada_l)


def _w_ada_update(c_all_t, dmod_cols, w, m, v):
    def body(c_ref, dm_ref, w_ref, m_ref, v_ref, g_ref, d_ref, nm_ref, nv_ref):
        g = c_ref[:, 0:1] * dm_ref[0:1, :]
        for b in range(1, N_DEV):
            g = g + c_ref[:, b:b + 1] * dm_ref[b:b + 1, :]
        delta, nm, nv = _adamw_math(w_ref[...], g, m_ref[...], v_ref[...])
        g_ref[...] = g
        d_ref[...] = delta
        nm_ref[...] = nm
        nv_ref[...] = nv

    spec = _full(w.shape)
    return pl.pallas_call(body, name="w_ada_update", grid=(1,),
                          in_specs=[_full(c_all_t.shape), _full(dmod_cols.shape), spec, spec, spec],
                          out_specs=[spec] * 4, out_shape=[jax.ShapeDtypeStruct(w.shape, F32)] * 4)(c_all_t, dmod_cols, w, m, v)


def _cast_bf16(w, name):
    def body(w_ref, o_ref):
        o_ref[...] = w_ref[...].astype(BF16)

    return pl.pallas_call(body, name=name, grid=(1,), in_specs=[_full(w.shape)], out_specs=_full(w.shape),
                          out_shape=jax.ShapeDtypeStruct(w.shape, BF16))(w)


def _prenorm(x, mod, norm_g, tm):
    t = x.shape[0]

    def body(x_ref, mod_ref, g_ref, h_ref):
        xv = x_ref[...]
        r = lax.rsqrt(jnp.mean(xv * xv, axis=-1, keepdims=True) + EPS)
        h = (xv * r) * g_ref[...] * (1.0 + mod_ref[:, D:2 * D]) + mod_ref[:, 0:D]
        h_ref[...] = h.astype(BF16)

    return pl.pallas_call(body, name="prenorm", grid=(t // tm,),
                          in_specs=[_rows(tm, D), _full((1, 3 * D)), _full((1, D))],
                          out_specs=_rows(tm, D), out_shape=jax.ShapeDtypeStruct((t, D), BF16))(x, mod, norm_g)


def _rope_apply(t, cos, s_lo, s_hi):
    return t * cos + pltpu.roll(t, 120, 1) * s_lo + pltpu.roll(t, 8, 1) * s_hi


def _rope_fwd(proj, cos, s_lo, s_hi, tm):
    t = proj.shape[0]

    def body(q_ref, k_ref, v_ref, c_ref, lo_ref, hi_ref, o_ref):
        cs, lo, hi = c_ref[...], lo_ref[...], hi_ref[...]
        for part, ref in enumerate((q_ref, k_ref)):
            for ch in range(QKV // 128):
                sl = slice(ch * 128, (ch + 1) * 128)
                o_ref[:, part * QKV + ch * 128: part * QKV + (ch + 1) * 128] = _rope_apply(ref[:, sl], cs, lo, hi).astype(BF16)
        o_ref[:, 2 * QKV:3 * QKV] = v_ref[...].astype(BF16)

    return pl.pallas_call(
        body, name="rope_fwd", grid=(t // tm,),
        in_specs=[_rows(tm, QKV, C_Q // QKV), _rows(tm, QKV, C_K // QKV), _rows(tm, QKV, C_V // QKV),
                  _rows(tm, 128), _rows(tm, 128), _rows(tm, 128)],
        out_specs=_rows(tm, 3 * QKV), out_shape=jax.ShapeDtypeStruct((t, 3 * QKV), BF16),
    )(proj, proj, proj, cos, s_lo, s_hi)


def _conv_taps(acc_init, w_ref, buf, row0, c0, offset_of_tap):
    acc = acc_init
    for j in range(CONV_K):
        acc = acc + w_ref[j:j + 1, pl.ds(c0, 128)] * buf[row0 + offset_of_tap(j): row0 + offset_of_tap(j) + 64, pl.ds(c0, 128)]
    return acc


def _conv_fwd(proj, conv_w, conv_b, ln_g, ln_b, tm):
    t = proj.shape[0]
    hb = tm // HALO

    def body(a_ref, b_ref, z_ref, ah_ref, bh_ref, w_ref, cb_ref, lg_ref, lb_ref, u1_ref, pc_ref, ubuf):
        i = pl.program_id(0)
        u0h = ah_ref[...] * _sig(bh_ref[...])
        ubuf[0:HALO, :] = jnp.where(i > 0, u0h, 0.0)
        ubuf[HALO:HALO + tm, :] = a_ref[...] * _sig(b_ref[...])

        def col(ci, carry):
            c0 = pl.multiple_of(ci * 128, 128)
            for rc in range(tm // 64):
                init = jnp.zeros((64, 128), F32)
                acc = _conv_taps(init, w_ref, ubuf, rc * 64, c0, lambda j: HALO - (CONV_K - 1) + j)
                u1_ref[rc * 64:(rc + 1) * 64, pl.ds(c0, 128)] = acc + cb_ref[:, pl.ds(c0, 128)]
            return carry

        lax.fori_loop(0, D // 128, col, 0)
        u1 = u1_ref[...]
        mu = jnp.mean(u1, axis=-1, keepdims=True)
        xc = u1 - mu
        var = jnp.mean(xc * xc, axis=-1, keepdims=True)
        u2 = xc * lax.rsqrt(var + EPS) * lg_ref[...] + lb_ref[...]
        z = z_ref[...]
        pc_ref[...] = (u2 * _sig(u2) * (z * _sig(z))).astype(BF16)

    halo = pl.BlockSpec((HALO, D), lambda i: (jnp.maximum(i * hb - 1, 0), 0))
    halo_b = pl.BlockSpec((HALO, D), lambda i: (jnp.maximum(i * hb - 1, 0), 1))
    return pl.pallas_call(
        body, name="conv_fwd", grid=(t // tm,),
        in_specs=[_rows(tm, D, 0), _rows(tm, D, 1), _rows(tm, D, 2), halo, halo_b,
                  _full((CONV_KP, D)), _full((1, D)), _full((1, D)), _full((1, D))],
        out_specs=[_rows(tm, D), _rows(tm, D)],
        out_shape=[jax.ShapeDtypeStruct((t, D), F32), jax.ShapeDtypeStruct((t, D), BF16)],
        scratch_shapes=[pltpu.VMEM((HALO + tm, D), F32)],
    )(proj, proj, proj, proj, proj, conv_w, conv_b, ln_g, ln_b)


def _band_masks(n):
    row = lax.broadcasted_iota(jnp.int32, (BLK, BLK), 0)
    col = lax.broadcasted_iota(jnp.int32, (BLK, BLK), 1)
    return jnp.logical_and(col >= row, n > 0), col <= row


def _nt(a, b):
    return lax.dot_general(a, b, (((1,), (1,)), ((), ())), preferred_element_type=F32)


def _tn(a, b):
    return lax.dot_general(a, b, (((0,), (0,)), ((), ())), preferred_element_type=F32)


def _att_fwd(qkv, gi, dil):
    t = qkv.shape[0]
    rows = t // dil
    nb = rows // BLK
    view = qkv.reshape(rows, dil * 3 * QKV)
    cpb = 3 * QKV // 128

    def body(q_ref, kp_ref, kc_ref, vp_ref, vc_ref, acc_ref, m_ref, l_ref):
        n = pl.program_id(2)
        mask_p, mask_c = _band_masks(n)
        for hh in range(2):
            sl = slice(hh * HEAD, (hh + 1) * HEAD)
            q = q_ref[:, sl]
            sp = jnp.where(mask_p, _nt(q, kp_ref[:, sl]) * SM_SCALE, NEG_INF)
            sc = jnp.where(mask_c, _nt(q, kc_ref[:, sl]) * SM_SCALE, NEG_INF)
            mx = jnp.maximum(jnp.max(sp, axis=-1, keepdims=True), jnp.max(sc, axis=-1, keepdims=True))
            pp = jnp.exp(sp - mx)
            pc = jnp.exp(sc - mx)
            den = jnp.sum(pp, axis=-1, keepdims=True) + jnp.sum(pc, axis=-1, keepdims=True)
            acc = (jnp.dot(pp.astype(BF16), vp_ref[:, sl], preferred_element_type=F32)
                   + jnp.dot(pc.astype(BF16), vc_ref[:, sl], preferred_element_type=F32))
            acc_ref[:, sl] = acc
            m_ref[:, sl] = jnp.broadcast_to(mx, (BLK, HEAD))
            l_ref[:, sl] = jnp.broadcast_to(den, (BLK, HEAD))

    def qkv_spec(part, prev):
        def imap(r, hp, n):
            return (jnp.maximum(n - 1, 0) if prev else n, r * cpb + part * (QKV // 128) + 4 * gi + hp)
        return pl.BlockSpec((BLK, 128), imap)

    out_spec = pl.BlockSpec((BLK, 128), lambda r, hp, n: (n, r * (ATT // 128) + hp))
    outs = pl.pallas_call(
        body, name=f"att_fwd_g{gi}", grid=(dil, ATT // 128, nb),
        in_specs=[qkv_spec(0, False), qkv_spec(1, True), qkv_spec(1, False), qkv_spec(2, True), qkv_spec(2, False)],
        out_specs=[out_spec] * 3, out_shape=[jax.ShapeDtypeStruct((rows, dil * ATT), F32)] * 3,
    )(view, view, view, view, view)
    return [o.reshape(t, ATT) for o in outs]


def _att_combine(parts, proj, tm):
    t = proj.shape[0]

    def body(a0, m0, l0, a1, m1, l1, a2, m2, l2, z_ref, att_ref, m_ref, den_ref, pa_ref):
        m_all = jnp.maximum(jnp.maximum(m0[...], m1[...]), m2[...])
        w0, w1, w2 = jnp.exp(m0[...] - m_all), jnp.exp(m1[...] - m_all), jnp.exp(m2[...] - m_all)
        den = w0 * l0[...] + w1 * l1[...] + w2 * l2[...]
        att = (w0 * a0[...] + w1 * a1[...] + w2 * a2[...]) / den
        z = z_ref[...]
        att_ref[...] = att
        m_ref[...] = m_all
        den_ref[...] = den
        pa_ref[...] = (att * (z * _sig(z))).astype(BF16)

    spec = _rows(tm, ATT)
    return pl.pallas_call(
        body, name="att_combine", grid=(t // tm,),
        in_specs=[spec] * 9 + [_rows(tm, ATT, C_ZA // ATT)],
        out_specs=[spec] * 4,
        out_shape=[jax.ShapeDtypeStruct((t, ATT), F32)] * 3 + [jax.ShapeDtypeStruct((t, ATT), BF16)],
    )(*parts, proj)


def _att_bwd(qkv, datt, m_all, den, dsum, gi, dil):
    t = qkv.shape[0]
    rows = t // dil
    nb = rows // BLK
    view = qkv.reshape(rows, dil * 3 * QKV)
    cpb = 3 * QKV // 128
    wide = lambda a: a.reshape(rows, dil * ATT)

    def body(q_ref, kp_ref, kc_ref, vp_ref, vc_ref, do_ref, m_ref, den_ref, ds_ref, dq_ref, dk_ref, dv_ref, ck, cv):
        n = pl.program_id(2)

        @pl.when(n < nb)
        def _():
            mask_p, mask_c = _band_masks(n)
            for hh in range(2):
                sl = slice(hh * HEAD, (hh + 1) * HEAD)
                q, kp, kc, vp, vc, do = q_ref[:, sl], kp_ref[:, sl], kc_ref[:, sl], vp_ref[:, sl], vc_ref[:, sl], do_ref[:, sl]
                mx = m_ref[:, hh * HEAD:hh * HEAD + 1]
                inv = 1.0 / den_ref[:, hh * HEAD:hh * HEAD + 1]
                dsm = ds_ref[:, hh * HEAD:hh * HEAD + 1]
                sp = jnp.where(mask_p, _nt(q, kp) * SM_SCALE, NEG_INF)
                sc = jnp.where(mask_c, _nt(q, kc) * SM_SCALE, NEG_INF)
                pp = jnp.exp(sp - mx) * inv
                pc = jnp.exp(sc - mx) * inv
                dsp = (pp * (_nt(do, vp) - dsm) * SM_SCALE).astype(BF16)
                dsc = (pc * (_nt(do, vc) - dsm) * SM_SCALE).astype(BF16)
                dq_ref[:, sl] = (jnp.dot(dsp, kp, preferred_element_type=F32) + jnp.dot(dsc, kc, preferred_element_type=F32))
                dk_prev = _tn(dsp, q)
                dv_prev = _tn(pp.astype(BF16), do)

                @pl.when(n > 0)
                def _():
                    dk_ref[:, sl] = ck[:, sl] + dk_prev
                    dv_ref[:, sl] = (cv[:, sl] + dv_prev).astype(BF16)

                ck[:, sl] = _tn(dsc, q)
                cv[:, sl] = _tn(pc.astype(BF16), do)

        @pl.when(n == nb)
        def _():
            dk_ref[...] = ck[...]
            dv_ref[...] = cv[...].astype(BF16)

    def qkv_spec(part, prev):
        def imap(r, hp, n):
            nn = jnp.minimum(n, nb - 1)
            return (jnp.maximum(nn - 1, 0) if prev else nn, r * cpb + part * (QKV // 128) + 4 * gi + hp)
        return pl.BlockSpec((BLK, 128), imap)

    cur = pl.BlockSpec((BLK, 128), lambda r, hp, n: (jnp.minimum(n, nb - 1), r * (ATT // 128) + hp))
    lag = pl.BlockSpec((BLK, 128), lambda r, hp, n: (jnp.maximum(n - 1, 0), r * (ATT // 128) + hp))
    outs = pl.pallas_call(
        body, name=f"att_bwd_g{gi}", grid=(dil, ATT // 128, nb + 1),
        in_specs=[qkv_spec(0, False), qkv_spec(1, True), qkv_spec(1, False), qkv_spec(2, True), qkv_spec(2, False),
                  cur, cur, cur, cur],
        out_specs=[cur, lag, lag],
        out_shape=[jax.ShapeDtypeStruct((rows, dil * ATT), F32), jax.ShapeDtypeStruct((rows, dil * ATT), F32),
                   jax.ShapeDtypeStruct((rows, dil * ATT), BF16)],
        scratch_shapes=[pltpu.VMEM((BLK, 128), F32), pltpu.VMEM((BLK, 128), F32)],
    )(view, view, view, view, view, wide(datt), wide(m_all), wide(den), wide(dsum))
    return [o.reshape(t, ATT) for o in outs]


def _merge_fwd(proj, y_conv, y_att, tm):
    t = proj.shape[0]

    def body(gc_ref, ga_ref, yc_ref, ya_ref, o_ref):
        o_ref[...] = (_sig(gc_ref[...]) * yc_ref[...] + _sig(ga_ref[...]) * ya_ref[...]).astype(BF16)

    return pl.pallas_call(body, name="merge_fwd", grid=(t // tm,),
                          in_specs=[_rows(tm, D, C_GC // D), _rows(tm, D, C_GA // D), _rows(tm, D), _rows(tm, D)],
                          out_specs=_rows(tm, D), out_shape=jax.ShapeDtypeStruct((t, D), BF16))(proj, proj, y_conv, y_att)


def _acc_rows(ref, i, val):
    @pl.when(i == 0)
    def _():
        ref[...] = jnp.zeros_like(ref)

    ref[...] += jnp.sum(val, axis=0, keepdims=True)


def _loss_head(x, o, mod, final_g, target, tm):
    t = x.shape[0]

    def body(x_ref, o_ref, mod_ref, fg_ref, tg_ref, dout_ref, do_ref, sq_ref, gfg_ref, dgate_ref):
        i = pl.program_id(0)
        gate = mod_ref[:, 2 * D:3 * D]
        ov = o_ref[...]
        out = x_ref[...] + gate * ov
        r = lax.rsqrt(jnp.mean(out * out, axis=-1, keepdims=True) + EPS)
        yn = out * r
        diff = yn * fg_ref[...] - tg_ref[...]
        dy = diff * (1.0 / D)
        gy = dy * fg_ref[...]
        dout = r * (gy - yn * jnp.mean(gy * yn, axis=-1, keepdims=True))
        dout_ref[...] = dout
        do_ref[...] = (dout * gate).astype(BF16)
        _acc_rows(sq_ref, i, diff * diff)
        _acc_rows(gfg_ref, i, dy * yn)
        _acc_rows(dgate_ref, i, dout * ov)

    vec = _full((1, D))
    return pl.pallas_call(
        body, name="loss_head", grid=(t // tm,),
        in_specs=[_rows(tm, D), _rows(tm, D), _full((1, 3 * D)), vec, _rows(tm, D)],
        out_specs=[_rows(tm, D), _rows(tm, D), vec, vec, vec],
        out_shape=[jax.ShapeDtypeStruct((t, D), F32), jax.ShapeDtypeStruct((t, D), BF16)] + [jax.ShapeDtypeStruct((1, D), F32)] * 3,
    )(x, o, mod, final_g, target)


def _merge_bwd(dmerged, proj, y_conv, y_att, tm):
    t = proj.shape[0]

    def body(dm_ref, gc_ref, ga_ref, yc_ref, ya_ref, dyc_ref, dya_ref, dp_ref):
        dm = dm_ref[...]
        sc, sa = _sig(gc_ref[...]), _sig(ga_ref[...])
        dyc_ref[...] = (dm * sc).astype(BF16)
        dya_ref[...] = (dm * sa).astype(BF16)
        dp_ref[:, 0:D] = (dm * yc_ref[...] * sc * (1.0 - sc)).astype(BF16)
        dp_ref[:, D:2 * D] = (dm * ya_ref[...] * sa * (1.0 - sa)).astype(BF16)

    return pl.pallas_call(
        body, name="merge_bwd", grid=(t // tm,),
        in_specs=[_rows(tm, D), _rows(tm, D, C_GC // D), _rows(tm, D, C_GA // D), _rows(tm, D), _rows(tm, D)],
        out_specs=[_rows(tm, D), _rows(tm, D), _rows(tm, 2 * D, C_GC // (2 * D))],
        out_shape=[jax.ShapeDtypeStruct((t, D), BF16), jax.ShapeDtypeStruct((t, D), BF16), jax.ShapeDtypeStruct((t, N_COL), BF16)],
    )(dmerged, proj, proj, y_conv, y_att)


def _att_pre_bwd(dpa, proj, att, dproj, tm):
    t = proj.shape[0]

    def body(dpa_ref, z_ref, att_ref, dp_in, datt_ref, ds_ref, dp_ref):
        del dp_in
        z, dpa_v, att_v = z_ref[...], dpa_ref[...], att_ref[...]
        s = _sig(z)
        datt = dpa_v * (z * s)
        datt_ref[...] = datt.astype(BF16)
        dp_ref[...] = (dpa_v * att_v * _dsilu(z, s)).astype(BF16)
        prod = datt * att_v
        for h in range(ATT // HEAD):
            sl = slice(h * HEAD, (h + 1) * HEAD)
            ds_ref[:, sl] = jnp.broadcast_to(jnp.sum(prod[:, sl], axis=-1, keepdims=True), (tm, HEAD))

    return pl.pallas_call(
        body, name="att_pre_bwd", grid=(t // tm,),
        in_specs=[_rows(tm, ATT), _rows(tm, ATT, C_ZA // ATT), _rows(tm, ATT), ANY],
        out_specs=[_rows(tm, ATT), _rows(tm, ATT), _rows(tm, ATT, C_ZA // ATT)],
        out_shape=[jax.ShapeDtypeStruct((t, ATT), BF16), jax.ShapeDtypeStruct((t, ATT), F32), jax.ShapeDtypeStruct((t, N_COL), BF16)],
        input_output_aliases={3: 2},
    )(dpa, proj, att, dproj)


def _place_qkv(parts, tables, dproj, col_block, tm, name):
    t = dproj.shape[0]
    nt = 0 if tables is None else 3

    def body(*refs):
        p_refs, t_refs, dp_ref = refs[:3], refs[3:3 + nt], refs[-1]
        for g in range(3):
            for ch in range(ATT // 128):
                sl = slice(ch * 128, (ch + 1) * 128)
                val = p_refs[g][:, sl]
                if nt:
                    val = _rope_apply(val, t_refs[0][...], -t_refs[1][...], -t_refs[2][...])
                dp_ref[:, g * ATT + ch * 128:g * ATT + (ch + 1) * 128] = val.astype(BF16)

    return pl.pallas_call(
        body, name=name, grid=(t // tm,),
        in_specs=[_rows(tm, ATT)] * 3 + [_rows(tm, 128)] * nt + [ANY],
        out_specs=_rows(tm, QKV, col_block), out_shape=jax.ShapeDtypeStruct((t, N_COL), BF16),
        input_output_aliases={3 + nt: 0},
    )(*parts, *(tables or ()), dproj)


def _conv_bwd_rows(dpc, proj, u1, ln_g, ln_b, dproj, tm):
    t = proj.shape[0]

    def body(dpc_ref, z_ref, u1_ref, lg_ref, lb_ref, dp_in, du1_ref, dp_ref, dlg_ref, dlb_ref, dcb_ref):
        del dp_in
        i = pl.program_id(0)
        u1v = u1_ref[...]
        mu = jnp.mean(u1v, axis=-1, keepdims=True)
        xc = u1v - mu
        r = lax.rsqrt(jnp.mean(xc * xc, axis=-1, keepdims=True) + EPS)
        uhat = xc * r
        u2 = uhat * lg_ref[...] + lb_ref[...]
        s2 = _sig(u2)
        z = z_ref[...]
        sz = _sig(z)
        dpc_v = dpc_ref[...]
        dp_ref[...] = (dpc_v * (u2 * s2) * _dsilu(z, sz)).astype(BF16)
        du2 = dpc_v * (z * sz) * _dsilu(u2, s2)
        duhat = du2 * lg_ref[...]
        du1 = r * (duhat - jnp.mean(duhat, axis=-1, keepdims=True) - uhat * jnp.mean(duhat * uhat, axis=-1, keepdims=True))
        du1_ref[...] = du1
        _acc_rows(dlg_ref, i, du2 * uhat)
        _acc_rows(dlb_ref, i, du2)
        _acc_rows(dcb_ref, i, du1)

    vec = _full((1, D))
    return pl.pallas_call(
        body, name="conv_bwd_rows", grid=(t // tm,),
        in_specs=[_rows(tm, D), _rows(tm, D, C_ZC // D), _rows(tm, D), vec, vec, ANY],
        out_specs=[_rows(tm, D), _rows(tm, D, C_ZC // D), vec, vec, vec],
        out_shape=[jax.ShapeDtypeStruct((t, D), F32), jax.ShapeDtypeStruct((t, N_COL), BF16)] + [jax.ShapeDtypeStruct((1, D), F32)] * 3,
        input_output_aliases={5: 1},
    )(dpc, proj, u1, ln_g, ln_b, dproj)


def _conv_bwd_taps(du1, proj, conv_w, dproj, tm):
    t = proj.shape[0]
    hb = tm // HALO
    last = t // HALO - 1

    def body(du_ref, duh_ref, a_ref, b_ref, ah_ref, bh_ref, w_ref, dp_in, dp_ref, dw_ref, dbuf, ubuf, g0):
        del dp_in
        i = pl.program_id(0)
        a, sb = a_ref[...], _sig(b_ref[...])
        ubuf[0:HALO, :] = jnp.where(i > 0, ah_ref[...] * _sig(bh_ref[...]), 0.0)
        ubuf[HALO:HALO + tm, :] = a * sb
        dbuf[0:tm, :] = du_ref[...]
        dbuf[tm:tm + HALO, :] = jnp.where(i < pl.num_programs(0) - 1, duh_ref[...], 0.0)

        @pl.when(i == 0)
        def _():
            dw_ref[...] = jnp.zeros_like(dw_ref)

        def col(ci, carry):
            c0 = pl.multiple_of(ci * 128, 128)
            for rc in range(tm // 64):
                g0[rc * 64:(rc + 1) * 64, pl.ds(c0, 128)] = _conv_taps(
                    jnp.zeros((64, 128), F32), w_ref, dbuf, rc * 64, c0, lambda j: CONV_K - 1 - j)
            for j in range(CONV_K):
                part = jnp.zeros((8, 128), F32)
                for rc in range(tm // 64):
                    off = rc * 64 + HALO - (CONV_K - 1) + j
                    prod = dbuf[rc * 64:(rc + 1) * 64, pl.ds(c0, 128)] * ubuf[off:off + 64, pl.ds(c0, 128)]
                    part = part + jnp.sum(prod.reshape(8, 8, 128), axis=0)
                dw_ref[j:j + 1, pl.ds(c0, 128)] += jnp.sum(part, axis=0, keepdims=True)
            return carry

        lax.fori_loop(0, D // 128, col, 0)
        du0 = g0[...]
        dp_ref[:, 0:D] = (du0 * sb).astype(BF16)
        dp_ref[:, D:2 * D] = (du0 * a * sb * (1.0 - sb)).astype(BF16)

    prev = lambda col: pl.BlockSpec((HALO, D), lambda i: (jnp.maximum(i * hb - 1, 0), col))
    nxt = pl.BlockSpec((HALO, D), lambda i: (jnp.minimum((i + 1) * hb, last), 0))
    return pl.pallas_call(
        body, name="conv_bwd_taps", grid=(t // tm,),
        in_specs=[_rows(tm, D), nxt, _rows(tm, D, 0), _rows(tm, D, 1), prev(0), prev(1), _full((CONV_KP, D)), ANY],
        out_specs=[_rows(tm, 2 * D, 0), _full((CONV_KP, D))],
        out_shape=[jax.ShapeDtypeStruct((t, N_COL), BF16), jax.ShapeDtypeStruct((CONV_KP, D), F32)],
        scratch_shapes=[pltpu.VMEM((tm + HALO, D), F32), pltpu.VMEM((HALO + tm, D), F32), pltpu.VMEM((tm, D), F32)],
        input_output_aliases={7: 0},
    )(du1, du1, proj, proj, proj, proj, conv_w, dproj)


def _prenorm_bwd(dh, x, dout, mod, norm_g, tm):
    t = x.shape[0]

    def body(dh_ref, x_ref, dout_ref, mod_ref, g_ref, gx_ref, dshift_ref, dscale_ref, dg_ref):
        i = pl.program_id(0)
        xv, dhv = x_ref[...], dh_ref[...]
        r = lax.rsqrt(jnp.mean(xv * xv, axis=-1, keepdims=True) + EPS)
        xn = xv * r
        one_scale = 1.0 + mod_ref[:, D:2 * D]
        dxn = dhv * (g_ref[...] * one_scale)
        gx_ref[...] = r * (dxn - xn * jnp.mean(dxn * xn, axis=-1, keepdims=True)) + dout_ref[...]
        _acc_rows(dshift_ref, i, dhv)
        _acc_rows(dscale_ref, i, dhv * xn * g_ref[...])
        _acc_rows(dg_ref, i, dhv * xn * one_scale)

    vec = _full((1, D))
    return pl.pallas_call(
        body, name="prenorm_bwd", grid=(t // tm,),
        in_specs=[_rows(tm, D), _rows(tm, D), _rows(tm, D), _full((1, 3 * D)), vec],
        out_specs=[_rows(tm, D), vec, vec, vec],
        out_shape=[jax.ShapeDtypeStruct((t, D), F32)] + [jax.ShapeDtypeStruct((1, D), F32)] * 3,
    )(dh, x, dout, mod, norm_g)


def _sum_devices(gathered):
    w = gathered.shape[-1]

    def body(g_ref, o_ref):
        acc = g_ref[0]
        for j in range(1, N_DEV):
            acc = acc + g_ref[j]
        o_ref[...] = acc

    return pl.pallas_call(body, name="sum_devices", grid=(1,), in_specs=[_full(gathered.shape)], out_specs=_full((1, w)),
                          out_shape=jax.ShapeDtypeStruct((1, w), F32))(gathered)


def _rope_tables(positions):
    half = HEAD // 8
    t = positions.shape[-1]
    inv_freq = ROPE_THETA ** (-(jnp.arange(half, dtype=F32) * 2.0 / (2 * half)))
    ang = positions.reshape(t, 1).astype(F32) * inv_freq
    cos, sin = jnp.cos(ang), jnp.sin(ang)
    zeros = lambda n: jnp.zeros((t, n), F32)
    c64 = jnp.concatenate([cos, cos, jnp.ones((t, HEAD - 2 * half), F32)], axis=1)
    lo64 = jnp.concatenate([-sin, zeros(HEAD - half)], axis=1)
    hi64 = jnp.concatenate([zeros(half), sin, zeros(HEAD - 2 * half)], axis=1)
    return tuple(jnp.tile(a, (1, 2)) for a in (c64, lo64, hi64))


def kernel(x, c, positions, norm_g, w_ada, b_ada, w_in, conv_w, conv_b, conv_ln_g, conv_ln_b, w_conv_out, w_att_out, w_o, final_g, loss_target, m_norm_g, m_w_ada, m_b_ada, m_w_in, m_conv_w, m_conv_b, m_conv_ln_g, m_conv_ln_b, m_w_conv_out, m_w_att_out, m_w_o, m_final_g, v_norm_g, v_w_ada, v_b_ada, v_w_in, v_conv_w, v_conv_b, v_conv_ln_g, v_conv_ln_b, v_w_conv_out, v_w_att_out, v_w_o, v_final_g):
    me = 4 * lax.axis_index("x") + 2 * lax.axis_index("y") + lax.axis_index("c")
    x2, tgt = x[0], loss_target[0]
    t = x2.shape[0]
    te = 512 if t % 512 == 0 else 256
    tcv = 256
    tmm = 1024 if t % 1024 == 0 else 256
    n_ada = w_ada.shape[-1]

    pad_taps = lambda a: jnp.pad(a[0], ((0, CONV_KP - CONV_K), (0, 0)))
    shards = (_cast_bf16(w_in[0], "cast_w_in"), _cast_bf16(w_conv_out[0], "cast_w_conv_out"),
              _cast_bf16(w_att_out[0], "cast_w_att_out"), _cast_bf16(w_o[0], "cast_w_o"), pad_taps(conv_w))
    w_in_f, w_co_f, w_ao_f, w_o_f, conv_w_f = _gather_weights(shards)

    c_all = _allgather_small(c, "gather_c").reshape(N_DEV, D)
    b_ada_l = lax.dynamic_slice(b_ada, (0, me * n_ada), (1, n_ada))
    parts = _allgather_small(_mod_part(c_all, w_ada[0], b_ada_l), "gather_mod")
    mod = lax.dynamic_slice(parts, (0, me, 0), (N_DEV, 1, n_ada)).reshape(1, N_DEV * n_ada)

    h = _prenorm(x2, mod, norm_g, te)
    proj = _matmul(h, w_in_f, tm=tmm, tn=1280, tk=D, name="proj")
    u1, pc = _conv_fwd(proj, conv_w_f, conv_b, conv_ln_g, conv_ln_b, tcv)
    tables = _rope_tables(positions)
    qkv = _rope_fwd(proj, *tables, 256)
    parts_att = []
    for gi, dil in GROUPS:
        parts_att += _att_fwd(qkv, gi, dil)
    att, m_all, den, pa = _att_combine(parts_att, proj, te)
    y_conv = _matmul(pc, w_co_f, tm=tmm, tn=D, tk=D, name="y_conv")
    y_att = _matmul(pa, w_ao_f, tm=tmm, tn=D, tk=ATT, name="y_att")
    merged = _merge_fwd(proj, y_conv, y_att, te)
    o = _matmul(merged, w_o_f, tm=tmm, tn=D, tk=D, name="out_proj")
    dout, do, sq_sum, g_final, d_gate = _loss_head(x2, o, mod, final_g.reshape(1, D), tgt, te)

    dmerged = _matmul(do, w_o_f, tb=True, tm=tmm, tn=D, tk=D, name="d_merged")
    dw_o = _matmul(merged, do, ta=True, tm=D, tn=D, tk=512, name="dw_o")
    dyc, dya, dproj = _merge_bwd(dmerged, proj, y_conv, y_att, te)
    dpc = _matmul(dyc, w_co_f, tb=True, tm=tmm, tn=D, tk=D, name="d_pc")
    dw_co = _matmul(pc, dyc, ta=True, tm=D, tn=D, tk=512, name="dw_conv_out")
    dpa = _matmul(dya, w_ao_f, tb=True, tm=tmm, tn=ATT, tk=D, name="d_pa")
    dw_ao = _matmul(pa, dya, ta=True, tm=ATT, tn=D, tk=512, name="dw_att_out")
    datt, dsum, dproj = _att_pre_bwd(dpa, proj, att, dproj, te)
    dqs, dks, dvs = [], [], []
    for gi, dil in GROUPS:
        dq, dk, dv = _att_bwd(qkv, datt, m_all, den, dsum, gi, dil)
        dqs.append(dq), dks.append(dk), dvs.append(dv)
    dproj = _place_qkv(dqs, tables, dproj, C_Q // QKV, 256, "place_dq")
    dproj = _place_qkv(dks, tables, dproj, C_K // QKV, 256, "place_dk")
    dproj = _place_qkv(dvs, None, dproj, C_V // QKV, 256, "place_dv")
    du1, dproj, d_ln_g, d_ln_b, d_conv_b = _conv_bwd_rows(dpc, proj, u1, conv_ln_g, conv_ln_b, dproj, te)
    dproj, dconv_w = _conv_bwd_taps(du1, proj, conv_w_f, dproj, tcv)
    dh = _matmul(dproj, w_in_f, tb=True, tm=tmm, tn=D, tk=1024, name="d_h")
    dw_in = _matmul(h, dproj, ta=True, tm=D, tn=2048, tk=512, name="dw_in")
    grad_x, d_shift, d_scale, d_norm_g = _prenorm_bwd(dh, x2, dout, mod, norm_g, te)

    packed = jnp.concatenate([d_shift, d_scale, d_gate, d_norm_g, d_conv_b, d_ln_g, d_ln_b, g_final, sq_sum], axis=1)
    gathered = _allgather_small(packed, "gather_partials")
    total = _sum_devices(gathered)
    seg = lambda k, n=1: total[:, k * D:(k + n) * D]
    g_b_ada, g_norm_g, g_conv_b, g_ln_g, g_ln_b, g_final_g = seg(0, 3), seg(3), seg(4), seg(5), seg(6), seg(7)
    loss = (0.5 / D) * jnp.sum(seg(8))
    dmod_all = gathered[:, 0, 0:3 * D]
    dmod_cols = lax.dynamic_slice(dmod_all, (0, me * n_ada), (N_DEV, n_ada))
    g_w_ada, d_w_ada, nm_w_ada, nv_w_ada = _w_ada_update(c_all.T, dmod_cols, w_ada[0], m_w_ada[0], v_w_ada[0])

    small = {}
    for name, g, w, m, v in (("norm_g", g_norm_g, norm_g, m_norm_g, v_norm_g), ("b_ada", g_b_ada, b_ada, m_b_ada, v_b_ada),
                             ("conv_b", g_conv_b, conv_b, m_conv_b, v_conv_b), ("conv_ln_g", g_ln_g, conv_ln_g, m_conv_ln_g, v_conv_ln_g),
                             ("conv_ln_b", g_ln_b, conv_ln_b, m_conv_ln_b, v_conv_ln_b),
                             ("final_g", g_final_g, final_g.reshape(1, D), m_final_g.reshape(1, D), v_final_g.reshape(1, D))):
        small[name] = (g,) + tuple(_adamw_small(g, w, m, v, "adamw_" + name))

    slots = _scatter_grads((dw_in, dw_co, dw_ao, dw_o, dconv_w))
    big = {
        "w_in": _sum_adamw(slots[0], w_in[0], m_w_in[0], v_w_in[0], 256, "adamw_w_in"),
        "w_conv_out": _sum_adamw(slots[1], w_conv_out[0], m_w_conv_out[0], v_w_conv_out[0], 128, "adamw_w_conv_out"),
        "w_att_out": _sum_adamw(slots[2], w_att_out[0], m_w_att_out[0], v_w_att_out[0], 512, "adamw_w_att_out"),
        "w_o": _sum_adamw(slots[3], w_o[0], m_w_o[0], v_w_o[0], 128, "adamw_w_o"),
        "conv_w": [r[:CONV_K] for r in _sum_adamw(slots[4], pad_taps(conv_w), pad_taps(m_conv_w), pad_taps(v_conv_w), CONV_KP, "adamw_conv_w")],
    }
    big["w_ada"] = (g_w_ada, d_w_ada, nm_w_ada, nv_w_ada)

    order = ("norm_g", "w_ada", "b_ada", "w_in", "conv_w", "conv_b", "conv_ln_g", "conv_ln_b", "w_conv_out", "w_att_out", "w_o", "final_g")
    lead = lambda name, a: a.reshape(D) if name == "final_g" else (a[None] if name in big else a)
    result = {**small, **big}
    outs = [loss, grad_x[None]]
    for field in range(4):
        outs += [lead(name, result[name][field]) for name in order]
    return tuple(outs)
```

```python
import functools

import jax
import jax.numpy as jnp
from jax import lax
from jax.experimental import pallas as pl
from jax.experimental.pallas import tpu as pltpu

F32 = jnp.float32
BF16 = jnp.bfloat16

N_DEV = 8
D = 1024
N_COL = 10240
C_A, C_B, C_ZC, C_Q, C_K, C_V, C_ZA, C_GC, C_GA = 0, 1024, 2048, 3072, 4608, 6144, 7680, 8192, 9216
QKV = 1536
ATT = 512
HEAD = 64
BLK = 128
TILE = 2048
GROUPS = ((0, 1), (1, 4), (2, 16))
CONV_K = 31
CONV_KP = 32
HALO = 32
EPS = 1e-6
NEG_INF = -1e30
ROPE_THETA = 500000.0
SM_SCALE = HEAD ** -0.5

ADAM_LR, ADAM_B1, ADAM_B2, ADAM_EPS, ADAM_WD, ADAM_STEP = 0.001, 0.9, 0.999, 1e-08, 0.01, 10

MESH = pl.DeviceIdType.MESH
ANY = pl.BlockSpec(memory_space=pl.ANY)


def _sig(v):
    return 1.0 / (1.0 + jnp.exp(-v))


def _dsilu(v, s):
    return s * (1.0 + v * (1.0 - s))


def _full(shape):
    return pl.BlockSpec(shape, lambda *_: (0,) * len(shape))


def _rows(tm, width, col=0):
    return pl.BlockSpec((tm, width), lambda i: (i, col))


def _matmul(a, b, *, ta=False, tb=False, out_dtype=F32, tm, tn, tk, name):
    m, k = (a.shape[1], a.shape[0]) if ta else a.shape
    n = b.shape[0] if tb else b.shape[1]
    assert (b.shape[1] if tb else b.shape[0]) == k
    assert m % tm == 0 and n % tn == 0 and k % tk == 0
    nk = k // tk
    dims = (((0 if ta else 1,), (1 if tb else 0,)), ((), ()))
    use_scratch = out_dtype != F32 and nk > 1

    def body(a_ref, b_ref, o_ref, *scratch):
        p = lax.dot_general(a_ref[...], b_ref[...], dims, preferred_element_type=F32)
        if nk == 1:
            o_ref[...] = p.astype(out_dtype)
            return
        acc = scratch[0] if use_scratch else o_ref
        kk = pl.program_id(2)

        @pl.when(kk == 0)
        def _():
            acc[...] = p

        @pl.when(kk > 0)
        def _():
            acc[...] += p

        if use_scratch:
            @pl.when(kk == nk - 1)
            def _():
                o_ref[...] = acc[...].astype(out_dtype)

    a_spec = pl.BlockSpec((tk, tm), lambda i, j, kk: (kk, i)) if ta else pl.BlockSpec((tm, tk), lambda i, j, kk: (i, kk))
    b_spec = pl.BlockSpec((tn, tk), lambda i, j, kk: (j, kk)) if tb else pl.BlockSpec((tk, tn), lambda i, j, kk: (kk, j))
    return pl.pallas_call(
        body, name=name, grid=(m // tm, n // tn, nk),
        in_specs=[a_spec, b_spec],
        out_specs=pl.BlockSpec((tm, tn), lambda i, j, kk: (i, j)),
        out_shape=jax.ShapeDtypeStruct((m, n), out_dtype),
        scratch_shapes=[pltpu.VMEM((tm, tn), F32)] if use_scratch else [],
    )(a, b)


def _me_and_peers():
    x, y, c = lax.axis_index("x"), lax.axis_index("y"), lax.axis_index("c")
    me = 4 * x + 2 * y + c
    peers = []
    for k in range(1, N_DEV):
        px, py, pc = x ^ (k >> 2), y ^ ((k >> 1) & 1), c ^ (k & 1)
        peers.append(((px, py, pc), 4 * px + 2 * py + pc))
    return me, peers


def _allgather_small(v, name):
    r, c = v.shape

    def body(v_ref, out_ref, send_sems, recv_sems):
        me, peers = _me_and_peers()
        out_ref[me] = v_ref[...]
        copies = []
        for k, (dev, _) in enumerate(peers):
            cp = pltpu.make_async_remote_copy(src_ref=v_ref, dst_ref=out_ref.at[me], send_sem=send_sems.at[k],
                                              recv_sem=recv_sems.at[k], device_id=dev, device_id_type=MESH)
            cp.start()
            copies.append(cp)
        for k, (dev, idx) in enumerate(peers):
            pltpu.make_async_remote_copy(src_ref=v_ref, dst_ref=out_ref.at[idx], send_sem=send_sems.at[k],
                                         recv_sem=recv_sems.at[k], device_id=dev, device_id_type=MESH).wait_recv()
        for cp in copies:
            cp.wait_send()

    return pl.pallas_call(
        body, name=name,
        in_specs=[pl.BlockSpec(memory_space=pltpu.VMEM)],
        out_specs=pl.BlockSpec(memory_space=pltpu.VMEM),
        out_shape=jax.ShapeDtypeStruct((N_DEV, r, c), v.dtype),
        scratch_shapes=[pltpu.SemaphoreType.DMA((N_DEV - 1,)), pltpu.SemaphoreType.DMA((N_DEV - 1,))],
    )(v)


def _window(ref, kind, idx, size):
    start = pl.multiple_of(idx * size, size)
    if kind == "rows":
        return ref.at[pl.ds(start, size), :]
    return ref.at[:, pl.ds(start, size)]


_BIG = (("cols", N_COL // N_DEV), ("rows", D // N_DEV), ("cols", D // N_DEV), ("rows", D // N_DEV), ("cols", D // N_DEV))


def _gather_weights(shards):
    full_shapes = []
    for s, (kind, size) in zip(shards, _BIG):
        full_shapes.append(jax.ShapeDtypeStruct((s.shape[0] * N_DEV, s.shape[1]) if kind == "rows"
                                                else (s.shape[0], s.shape[1] * N_DEV), s.dtype))
    nt = len(shards)

    def body(*refs):
        src, dst = refs[:nt], refs[nt:2 * nt]
        send_sems, recv_sems, local_sems = refs[2 * nt:]
        me, peers = _me_and_peers()
        local, sent = [], []
        for t in range(nt):
            kind, size = _BIG[t]
            cp = pltpu.make_async_copy(src[t], _window(dst[t], kind, me, size), local_sems.at[t])
            cp.start()
            local.append(cp)
            for k, (dev, _) in enumerate(peers):
                rc = pltpu.make_async_remote_copy(src_ref=src[t], dst_ref=_window(dst[t], kind, me, size),
                                                  send_sem=send_sems.at[t, k], recv_sem=recv_sems.at[t, k],
                                                  device_id=dev, device_id_type=MESH)
                rc.start()
                sent.append(rc)
        for t in range(nt):
            kind, size = _BIG[t]
            for k, (dev, idx) in enumerate(peers):
                pltpu.make_async_remote_copy(src_ref=src[t], dst_ref=_window(dst[t], kind, idx, size),
                                             send_sem=send_sems.at[t, k], recv_sem=recv_sems.at[t, k],
                                             device_id=dev, device_id_type=MESH).wait_recv()
        for rc in sent:
            rc.wait_send()
        for cp in local:
            cp.wait()

    return pl.pallas_call(
        body, name="gather_weights",
        in_specs=[ANY] * nt, out_specs=[ANY] * nt, out_shape=full_shapes,
        scratch_shapes=[pltpu.SemaphoreType.DMA((nt, N_DEV - 1)), pltpu.SemaphoreType.DMA((nt, N_DEV - 1)),
                        pltpu.SemaphoreType.DMA((nt,))],
    )(*shards)


def _scatter_grads(grads):
    nt = len(grads)
    slot_shapes = []
    for g, (kind, size) in zip(grads, _BIG):
        shard = (size, g.shape[1]) if kind == "rows" else (g.shape[0], size)
        slot_shapes.append(jax.ShapeDtypeStruct((N_DEV,) + shard, g.dtype))

    def body(*refs):
        src, dst = refs[:nt], refs[nt:2 * nt]
        send_sems, recv_sems, local_sems = refs[2 * nt:]
        me, peers = _me_and_peers()
        local, sent = [], []
        for t in range(nt):
            kind, size = _BIG[t]
            cp = pltpu.make_async_copy(_window(src[t], kind, me, size), dst[t].at[me], local_sems.at[t])
            cp.start()
            local.append(cp)
            for k, (dev, idx) in enumerate(peers):
                rc = pltpu.make_async_remote_copy(src_ref=_window(src[t], kind, idx, size), dst_ref=dst[t].at[me],
                                                  send_sem=send_sems.at[t, k], recv_sem=recv_sems.at[t, k],
                                                  device_id=dev, device_id_type=MESH)
                rc.start()
                sent.append(rc)
        for t in range(nt):
            kind, size = _BIG[t]
            for k, (dev, idx) in enumerate(peers):
                pltpu.make_async_remote_copy(src_ref=_window(src[t], kind, me, size), dst_ref=dst[t].at[idx],
                                             send_sem=send_sems.at[t, k], recv_sem=recv_sems.at[t, k],
                                             device_id=dev, device_id_type=MESH).wait_recv()
        for rc in sent:
            rc.wait_send()
        for cp in local:
            cp.wait()

    return pl.pallas_call(
        body, name="scatter_grads",
        in_specs=[ANY] * nt, out_specs=[ANY] * nt, out_shape=slot_shapes,
        scratch_shapes=[pltpu.SemaphoreType.DMA((nt, N_DEV - 1)), pltpu.SemaphoreType.DMA((nt, N_DEV - 1)),
                        pltpu.SemaphoreType.DMA((nt,))],
    )(*grads)


def _adamw_math(w, g, m, v):
    m = ADAM_B1 * m + (1.0 - ADAM_B1) * g
    v = ADAM_B2 * v + (1.0 - ADAM_B2) * (g * g)
    m_hat = m / (1.0 - ADAM_B1 ** ADAM_STEP)
    v_hat = v / (1.0 - ADAM_B2 ** ADAM_STEP)
    delta = -ADAM_LR * (m_hat / (jnp.sqrt(v_hat) + ADAM_EPS) + ADAM_WD * w)
    return delta, m, v


def _sum_adamw(slots, w, m, v, tr, name):
    _, r, c = slots.shape
    assert r % tr == 0

    def body(s_ref, w_ref, m_ref, v_ref, g_ref, d_ref, nm_ref, nv_ref):
        g = s_ref[0]
        for j in range(1, N_DEV):
            g = g + s_ref[j]
        delta, nm, nv = _adamw_math(w_ref[...], g, m_ref[...], v_ref[...])
        g_ref[...] = g
        d_ref[...] = delta
        nm_ref[...] = nm
        nv_ref[...] = nv

    blk = pl.BlockSpec((tr, c), lambda i: (i, 0))
    return pl.pallas_call(
        body, name=name, grid=(r // tr,),
        in_specs=[pl.BlockSpec((N_DEV, tr, c), lambda i: (0, i, 0)), blk, blk, blk],
        out_specs=[blk] * 4, out_shape=[jax.ShapeDtypeStruct((r, c), F32)] * 4,
    )(slots, w, m, v)


def _adamw_small(g, w, m, v, name):
    def body(g_ref, w_ref, m_ref, v_ref, d_ref, nm_ref, nv_ref):
        delta, nm, nv = _adamw_math(w_ref[...], g_ref[...], m_ref[...], v_ref[...])
        d_ref[...] = delta
        nm_ref[...] = nm
        nv_ref[...] = nv

    spec = _full(g.shape)
    return pl.pallas_call(body, name=name, grid=(1,), in_specs=[spec] * 4, out_specs=[spec] * 3,
                          out_shape=[jax.ShapeDtypeStruct(g.shape, F32)] * 3)(g, w, m, v)


def _mod_part(c_all, w_ada_l, b_ada_l):
    n = w_ada_l.shape[1]

    def body(c_ref, w_ref, b_ref, o_ref):
        o_ref[...] = jnp.dot(c_ref[...], w_ref[...], preferred_element_type=F32,
                             precision=lax.Precision.HIGHEST) + b_ref[...]

    return pl.pallas_call(body, name="mod_part", grid=(1,),
                          in_specs=[_full(c_all.shape), _full(w_ada_l.shape), _full(b_ada_l.shape)],
                          out_specs=_full((N_DEV, n)), out_shape=jax.ShapeDtypeStruct((N_DEV, n), F32))(c_all, w_ada_l, b_ada_l)


def _w_ada_update(c_all_t, dmod_cols, w, m, v):
    def body(c_ref, dm_ref, w_ref, m_ref, v_ref, g_ref, d_ref, nm_ref, nv_ref):
        g = c_ref[:, 0:1] * dm_ref[0:1, :]
        for b in range(1, N_DEV):
            g = g + c_ref[:, b:b + 1] * dm_ref[b:b + 1, :]
        delta, nm, nv = _adamw_math(w_ref[...], g, m_ref[...], v_ref[...])
        g_ref[...] = g
        d_ref[...] = delta
        nm_ref[...] = nm
        nv_ref[...] = nv

    spec = _full(w.shape)
    return pl.pallas_call(body, name="w_ada_update", grid=(1,),
                          in_specs=[_full(c_all_t.shape), _full(dmod_cols.shape), spec, spec, spec],
                          out_specs=[spec] * 4, out_shape=[jax.ShapeDtypeStruct(w.shape, F32)] * 4)(c_all_t, dmod_cols, w, m, v)


def _cast_bf16(w, name):
    def body(w_ref, o_ref):
        o_ref[...] = w_ref[...].astype(BF16)

    return pl.pallas_call(body, name=name, grid=(1,), in_specs=[_full(w.shape)], out_specs=_full(w.shape),
                          out_shape=jax.ShapeDtypeStruct(w.shape, BF16))(w)


def _prenorm(x, mod, norm_g, tm):
    t = x.shape[0]

    def body(x_ref, mod_ref, g_ref, h_ref):
        xv = x_ref[...]
        r = lax.rsqrt(jnp.mean(xv * xv, axis=-1, keepdims=True) + EPS)
        h = (xv * r) * g_ref[...] * (1.0 + mod_ref[:, D:2 * D]) + mod_ref[:, 0:D]
        h_ref[...] = h.astype(BF16)

    return pl.pallas_call(body, name="prenorm", grid=(t // tm,),
                          in_specs=[_rows(tm, D), _full((1, 3 * D)), _full((1, D))],
                          out_specs=_rows(tm, D), out_shape=jax.ShapeDtypeStruct((t, D), BF16))(x, mod, norm_g)


def _rope_apply(t, cos, s_lo, s_hi):
    return t * cos + pltpu.roll(t, 120, 1) * s_lo + pltpu.roll(t, 8, 1) * s_hi


def _conv_taps(acc_init, w_ref, buf, row0, c0, offset_of_tap):
    acc = acc_init
    for j in range(CONV_K):
        acc = acc + w_ref[j:j + 1, pl.ds(c0, 128)] * buf[row0 + offset_of_tap(j): row0 + offset_of_tap(j) + 64, pl.ds(c0, 128)]
    return acc


def _conv_fwd(proj, conv_w, conv_b, ln_g, ln_b, tm):
    t = proj.shape[0]
    hb = tm // HALO

    def body(a_ref, b_ref, z_ref, ah_ref, bh_ref, w_ref, cb_ref, lg_ref, lb_ref, u1_ref, pc_ref, ubuf):
        i = pl.program_id(0)
        u0h = ah_ref[...] * _sig(bh_ref[...])
        ubuf[0:HALO, :] = jnp.where(i > 0, u0h, 0.0)
        ubuf[HALO:HALO + tm, :] = a_ref[...] * _sig(b_ref[...])

        def col(ci, carry):
            c0 = pl.multiple_of(ci * 128, 128)
            for rc in range(tm // 64):
                init = jnp.zeros((64, 128), F32)
                acc = _conv_taps(init, w_ref, ubuf, rc * 64, c0, lambda j: HALO - (CONV_K - 1) + j)
                u1_ref[rc * 64:(rc + 1) * 64, pl.ds(c0, 128)] = acc + cb_ref[:, pl.ds(c0, 128)]
            return carry

        lax.fori_loop(0, D // 128, col, 0)
        u1 = u1_ref[...]
        mu = jnp.mean(u1, axis=-1, keepdims=True)
        xc = u1 - mu
        var = jnp.mean(xc * xc, axis=-1, keepdims=True)
        u2 = xc * lax.rsqrt(var + EPS) * lg_ref[...] + lb_ref[...]
        z = z_ref[...]
        pc_ref[...] = (u2 * _sig(u2) * (z * _sig(z))).astype(BF16)

    halo = pl.BlockSpec((HALO, D), lambda i: (jnp.maximum(i * hb - 1, 0), 0))
    halo_b = pl.BlockSpec((HALO, D), lambda i: (jnp.maximum(i * hb - 1, 0), 1))
    return pl.pallas_call(
        body, name="conv_fwd", grid=(t // tm,),
        in_specs=[_rows(tm, D, 0), _rows(tm, D, 1), _rows(tm, D, 2), halo, halo_b,
                  _full((CONV_KP, D)), _full((1, D)), _full((1, D)), _full((1, D))],
        out_specs=[_rows(tm, D), _rows(tm, D)],
        out_shape=[jax.ShapeDtypeStruct((t, D), F32), jax.ShapeDtypeStruct((t, D), BF16)],
        scratch_shapes=[pltpu.VMEM((HALO + tm, D), F32)],
    )(proj, proj, proj, proj, proj, conv_w, conv_b, ln_g, ln_b)


def _band_masks(has_prev):
    row = lax.broadcasted_iota(jnp.int32, (BLK, BLK), 0)
    col = lax.broadcasted_iota(jnp.int32, (BLK, BLK), 1)
    return jnp.logical_and(col >= row, has_prev), col <= row


class _Dilated:
    def __init__(self, dil):
        self.dil = dil
        self.per = TILE // dil
        self.nbr = self.per // BLK

    def spread(self, dst, base, src_ref, dtype):
        for r in range(self.dil):
            rows = src_ref[pl.ds(r, self.per, stride=self.dil), :] if self.dil > 1 else src_ref[...]
            dst[pl.ds(pl.multiple_of(base + r * self.per, BLK), self.per), :] = rows.astype(dtype)

    def gather(self, dst_ref, src, base):
        for r in range(self.dil):
            rows = src[pl.ds(pl.multiple_of(base + r * self.per, BLK), self.per), :]
            if self.dil > 1:
                dst_ref[pl.ds(r, self.per, stride=self.dil), :] = rows
            else:
                dst_ref[...] = rows

    def block_rows(self, b, i, cur, prv):
        n = b % self.nbr
        row = pl.multiple_of(b * BLK, BLK)
        has_prev = jnp.logical_or(n > 0, i > 0)
        prev = jnp.where(n > 0, cur + row - BLK, jnp.where(i > 0, prv + row + (self.nbr - 1) * BLK, cur + row))
        return row, pl.multiple_of(prev, BLK), has_prev


def _slots(i):
    return pl.multiple_of((i % 2) * TILE, TILE), pl.multiple_of(((i + 1) % 2) * TILE, TILE)


def _pair(a, b):
    return jnp.concatenate([a, b], axis=1)


def _nt(a, b):
    return lax.dot_general(a, b, (((1,), (1,)), ((), ())), preferred_element_type=F32)


def _tn(a, b):
    return lax.dot_general(a, b, (((0,), (0,)), ((), ())), preferred_element_type=F32)


def _qkv_specs(gi, clamp_to=None):
    def spec(col0):
        def imap(hp, i):
            return (i if clamp_to is None else jnp.minimum(i, clamp_to), (col0 + gi * ATT) // 128 + hp)
        return pl.BlockSpec((TILE, 128), imap)
    return [spec(C_Q), spec(C_K), spec(C_V)]


def _att_fwd(proj, tables, gi, dil):
    t = proj.shape[0]
    dl = _Dilated(dil)

    def body(q_ref, k_ref, v_ref, c_ref, lo_ref, hi_ref, o_ref, lse_ref, tmp, qd, kd, vd, od, ld):
        i = pl.program_id(1)
        cur, prv = _slots(i)
        cs, lo, hi = c_ref[...], lo_ref[...], hi_ref[...]
        tmp[...] = _rope_apply(q_ref[...], cs, lo, hi) * SM_SCALE
        dl.spread(qd, 0, tmp, BF16)
        tmp[...] = _rope_apply(k_ref[...], cs, lo, hi)
        dl.spread(kd, cur, tmp, BF16)
        dl.spread(vd, cur, v_ref, BF16)

        def block(b, carry):
            row, prev, has_prev = dl.block_rows(b, i, cur, prv)
            mask_p, mask_c = _band_masks(has_prev)
            q2, kp2, kc2 = qd[pl.ds(row, BLK), :], kd[pl.ds(prev, BLK), :], kd[pl.ds(cur + row, BLK), :]
            vp2, vc2 = vd[pl.ds(prev, BLK), :], vd[pl.ds(cur + row, BLK), :]
            outs, lses = [], []
            for hh in range(2):
                sl = slice(hh * HEAD, (hh + 1) * HEAD)
                q = q2[:, sl]
                sp = jnp.where(mask_p, _nt(q, kp2[:, sl]), NEG_INF)
                sc = jnp.where(mask_c, _nt(q, kc2[:, sl]), NEG_INF)
                mx = jnp.maximum(jnp.max(sp, axis=-1, keepdims=True), jnp.max(sc, axis=-1, keepdims=True))
                pp = jnp.exp(sp - mx)
                pc = jnp.exp(sc - mx)
                den = jnp.sum(pp, axis=-1, keepdims=True) + jnp.sum(pc, axis=-1, keepdims=True)
                acc = (jnp.dot(pp.astype(BF16), vp2[:, sl], preferred_element_type=F32)
                       + jnp.dot(pc.astype(BF16), vc2[:, sl], preferred_element_type=F32))
                outs.append(acc / den)
                lses.append(jnp.broadcast_to(mx + jnp.log(den), (BLK, HEAD)))
            od[pl.ds(row, BLK), :] = _pair(*outs)
            ld[pl.ds(row, BLK), :] = _pair(*lses)
            return carry

        lax.fori_loop(0, TILE // BLK, block, 0)
        dl.gather(o_ref, od, 0)
        dl.gather(lse_ref, ld, 0)

    tab = pl.BlockSpec((TILE, 128), lambda hp, i: (i, 0))
    out_spec = pl.BlockSpec((TILE, 128), lambda hp, i: (i, hp))
    return pl.pallas_call(
        body, name=f"att_fwd_g{gi}", grid=(ATT // 128, t // TILE),
        in_specs=_qkv_specs(gi) + [tab] * 3,
        out_specs=[out_spec] * 2, out_shape=[jax.ShapeDtypeStruct((t, ATT), F32)] * 2,
        scratch_shapes=[pltpu.VMEM((TILE, 128), F32), pltpu.VMEM((TILE, 128), BF16), pltpu.VMEM((2 * TILE, 128), BF16),
                        pltpu.VMEM((2 * TILE, 128), BF16), pltpu.VMEM((TILE, 128), F32), pltpu.VMEM((TILE, 128), F32)],
    )(proj, proj, proj, *tables)


def _att_combine(parts, proj, tm):
    t = proj.shape[0]

    def body(o0, l0, o1, l1, o2, l2, z_ref, att_ref, lse_ref, pa_ref):
        m_all = jnp.maximum(jnp.maximum(l0[...], l1[...]), l2[...])
        w0, w1, w2 = jnp.exp(l0[...] - m_all), jnp.exp(l1[...] - m_all), jnp.exp(l2[...] - m_all)
        den = w0 + w1 + w2
        att = (w0 * o0[...] + w1 * o1[...] + w2 * o2[...]) / den
        z = z_ref[...]
        att_ref[...] = att
        lse_ref[...] = m_all + jnp.log(den)
        pa_ref[...] = (att * (z * _sig(z))).astype(BF16)

    spec = _rows(tm, ATT)
    return pl.pallas_call(
        body, name="att_combine", grid=(t // tm,),
        in_specs=[spec] * 6 + [_rows(tm, ATT, C_ZA // ATT)],
        out_specs=[spec] * 3,
        out_shape=[jax.ShapeDtypeStruct((t, ATT), F32)] * 2 + [jax.ShapeDtypeStruct((t, ATT), BF16)],
    )(*parts, proj)


def _att_bwd(proj, tables, datt, dsum, lse, gi, dil):
    t = proj.shape[0]
    nt = t // TILE
    dl = _Dilated(dil)

    def body(q_ref, k_ref, v_ref, c_ref, lo_ref, hi_ref, cl_ref, lol_ref, hil_ref, do_ref, ds_ref, lse_ref,
             dq_ref, dk_ref, dv_ref, tmp, qd, kd, vd, dod, dsd, lsd, dqd, dkd, dvd):
        i = pl.program_id(1)
        cur, prv = _slots(i)

        @pl.when(i < nt)
        def _():
            cs, lo, hi = c_ref[...], lo_ref[...], hi_ref[...]
            tmp[...] = _rope_apply(q_ref[...], cs, lo, hi) * SM_SCALE
            dl.spread(qd, 0, tmp, BF16)
            tmp[...] = _rope_apply(k_ref[...], cs, lo, hi)
            dl.spread(kd, cur, tmp, BF16)
            dl.spread(vd, cur, v_ref, BF16)
            dl.spread(dod, 0, do_ref, BF16)
            dl.spread(dsd, 0, ds_ref, F32)
            dl.spread(lsd, 0, lse_ref, F32)
            dkd[pl.ds(cur, TILE), :] = jnp.zeros((TILE, 128), F32)
            dvd[pl.ds(cur, TILE), :] = jnp.zeros((TILE, 128), F32)

            def block(b, carry):
                row, prev, has_prev = dl.block_rows(b, i, cur, prv)
                mask_p, mask_c = _band_masks(has_prev)
                q2, kp2, kc2 = qd[pl.ds(row, BLK), :], kd[pl.ds(prev, BLK), :], kd[pl.ds(cur + row, BLK), :]
                vp2, vc2, do2 = vd[pl.ds(prev, BLK), :], vd[pl.ds(cur + row, BLK), :], dod[pl.ds(row, BLK), :]
                ds2, ls2 = dsd[pl.ds(row, BLK), :], lsd[pl.ds(row, BLK), :]
                dqs, dkc, dkp, dvc, dvp = [], [], [], [], []
                for hh in range(2):
                    sl = slice(hh * HEAD, (hh + 1) * HEAD)
                    q, kp, kc, vp, vc, do = q2[:, sl], kp2[:, sl], kc2[:, sl], vp2[:, sl], vc2[:, sl], do2[:, sl]
                    lse = ls2[:, hh * HEAD:hh * HEAD + 1]
                    dsm = ds2[:, hh * HEAD:hh * HEAD + 1]
                    pp = jnp.exp(jnp.where(mask_p, _nt(q, kp), NEG_INF) - lse)
                    pc = jnp.exp(jnp.where(mask_c, _nt(q, kc), NEG_INF) - lse)
                    dsp = (pp * (_nt(do, vp) - dsm)).astype(BF16)
                    dsc = (pc * (_nt(do, vc) - dsm)).astype(BF16)
                    dqs.append((jnp.dot(dsp, kp, preferred_element_type=F32)
                                + jnp.dot(dsc, kc, preferred_element_type=F32)) * SM_SCALE)
                    dkp.append(_tn(dsp, q))
                    dkc.append(_tn(dsc, q))
                    dvp.append(_tn(pp.astype(BF16), do))
                    dvc.append(_tn(pc.astype(BF16), do))
                dqd[pl.ds(row, BLK), :] = _pair(*dqs)
                dkd[pl.ds(cur + row, BLK), :] += _pair(*dkc)
                dvd[pl.ds(cur + row, BLK), :] += _pair(*dvc)
                dkd[pl.ds(prev, BLK), :] += _pair(*dkp)
                dvd[pl.ds(prev, BLK), :] += _pair(*dvp)
                return carry

            lax.fori_loop(0, TILE // BLK, block, 0)
            dl.gather(tmp, dqd, 0)
            dq_ref[...] = _rope_apply(tmp[...], cs, -lo, -hi).astype(BF16)

        @pl.when(i > 0)
        def _():
            dl.gather(tmp, dkd, prv)
            dk_ref[...] = _rope_apply(tmp[...], cl_ref[...], -lol_ref[...], -hil_ref[...]).astype(BF16)
            dl.gather(tmp, dvd, prv)
            dv_ref[...] = tmp[...].astype(BF16)

    now = lambda col: pl.BlockSpec((TILE, 128), lambda hp, i: (jnp.minimum(i, nt - 1), col(hp)))
    lag = lambda col: pl.BlockSpec((TILE, 128), lambda hp, i: (jnp.maximum(i - 1, 0), col(hp)))
    first, pair = (lambda hp: 0), (lambda hp: hp)
    return pl.pallas_call(
        body, name=f"att_bwd_g{gi}", grid=(ATT // 128, nt + 1),
        in_specs=_qkv_specs(gi, nt - 1) + [now(first)] * 3 + [lag(first)] * 3 + [now(pair)] * 3,
        out_specs=[now(pair), lag(pair), lag(pair)],
        out_shape=[jax.ShapeDtypeStruct((t, ATT), BF16)] * 3,
        scratch_shapes=[pltpu.VMEM((TILE, 128), F32), pltpu.VMEM((TILE, 128), BF16), pltpu.VMEM((2 * TILE, 128), BF16),
                        pltpu.VMEM((2 * TILE, 128), BF16), pltpu.VMEM((TILE, 128), BF16), pltpu.VMEM((TILE, 128), F32),
                        pltpu.VMEM((TILE, 128), F32), pltpu.VMEM((TILE, 128), F32), pltpu.VMEM((2 * TILE, 128), F32),
                        pltpu.VMEM((2 * TILE, 128), F32)],
    )(proj, proj, proj, *tables, *tables, datt, dsum, lse)


def _merge_fwd(proj, y_conv, y_att, tm):
    t = proj.shape[0]

    def body(gc_ref, ga_ref, yc_ref, ya_ref, o_ref):
        o_ref[...] = (_sig(gc_ref[...]) * yc_ref[...] + _sig(ga_ref[...]) * ya_ref[...]).astype(BF16)

    return pl.pallas_call(body, name="merge_fwd", grid=(t // tm,),
                          in_specs=[_rows(tm, D, C_GC // D), _rows(tm, D, C_GA // D), _rows(tm, D), _rows(tm, D)],
                          out_specs=_rows(tm, D), out_shape=jax.ShapeDtypeStruct((t, D), BF16))(proj, proj, y_conv, y_att)


def _acc_rows(ref, i, val):
    @pl.when(i == 0)
    def _():
        ref[...] = jnp.zeros_like(ref)

    ref[...] += jnp.sum(val, axis=0, keepdims=True)


def _loss_head(x, o, mod, final_g, target, tm):
    t = x.shape[0]

    def body(x_ref, o_ref, mod_ref, fg_ref, tg_ref, dout_ref, do_ref, sq_ref, gfg_ref, dgate_ref):
        i = pl.program_id(0)
        gate = mod_ref[:, 2 * D:3 * D]
        ov = o_ref[...]
        out = x_ref[...] + gate * ov
        r = lax.rsqrt(jnp.mean(out * out, axis=-1, keepdims=True) + EPS)
        yn = out * r
        diff = yn * fg_ref[...] - tg_ref[...]
        dy = diff * (1.0 / D)
        gy = dy * fg_ref[...]
        dout = r * (gy - yn * jnp.mean(gy * yn, axis=-1, keepdims=True))
        dout_ref[...] = dout
        do_ref[...] = (dout * gate).astype(BF16)
        _acc_rows(sq_ref, i, diff * diff)
        _acc_rows(gfg_ref, i, dy * yn)
        _acc_rows(dgate_ref, i, dout * ov)

    vec = _full((1, D))
    return pl.pallas_call(
        body, name="loss_head", grid=(t // tm,),
        in_specs=[_rows(tm, D), _rows(tm, D), _full((1, 3 * D)), vec, _rows(tm, D)],
        out_specs=[_rows(tm, D), _rows(tm, D), vec, vec, vec],
        out_shape=[jax.ShapeDtypeStruct((t, D), F32), jax.ShapeDtypeStruct((t, D), BF16)] + [jax.ShapeDtypeStruct((1, D), F32)] * 3,
    )(x, o, mod, final_g, target)


def _merge_bwd(dmerged, proj, y_conv, y_att, tm):
    t = proj.shape[0]

    def body(dm_ref, gc_ref, ga_ref, yc_ref, ya_ref, dyc_ref, dya_ref, dp_ref):
        dm = dm_ref[...]
        sc, sa = _sig(gc_ref[...]), _sig(ga_ref[...])
        dyc_ref[...] = (dm * sc).astype(BF16)
        dya_ref[...] = (dm * sa).astype(BF16)
        dp_ref[:, 0:D] = (dm * yc_ref[...] * sc * (1.0 - sc)).astype(BF16)
        dp_ref[:, D:2 * D] = (dm * ya_ref[...] * sa * (1.0 - sa)).astype(BF16)

    return pl.pallas_call(
        body, name="merge_bwd", grid=(t // tm,),
        in_specs=[_rows(tm, D), _rows(tm, D, C_GC // D), _rows(tm, D, C_GA // D), _rows(tm, D), _rows(tm, D)],
        out_specs=[_rows(tm, D), _rows(tm, D), _rows(tm, 2 * D, C_GC // (2 * D))],
        out_shape=[jax.ShapeDtypeStruct((t, D), BF16), jax.ShapeDtypeStruct((t, D), BF16), jax.ShapeDtypeStruct((t, N_COL), BF16)],
    )(dmerged, proj, proj, y_conv, y_att)


def _att_pre_bwd(dpa, proj, att, dproj, tm):
    t = proj.shape[0]

    def body(dpa_ref, z_ref, att_ref, dp_in, datt_ref, ds_ref, dp_ref):
        del dp_in
        z, dpa_v, att_v = z_ref[...], dpa_ref[...], att_ref[...]
        s = _sig(z)
        datt = dpa_v * (z * s)
        datt_ref[...] = datt
        dp_ref[...] = (dpa_v * att_v * _dsilu(z, s)).astype(BF16)
        prod = datt * att_v
        for h in range(ATT // HEAD):
            sl = slice(h * HEAD, (h + 1) * HEAD)
            ds_ref[:, sl] = jnp.broadcast_to(jnp.sum(prod[:, sl], axis=-1, keepdims=True), (tm, HEAD))

    return pl.pallas_call(
        body, name="att_pre_bwd", grid=(t // tm,),
        in_specs=[_rows(tm, ATT), _rows(tm, ATT, C_ZA // ATT), _rows(tm, ATT), ANY],
        out_specs=[_rows(tm, ATT), _rows(tm, ATT), _rows(tm, ATT, C_ZA // ATT)],
        out_shape=[jax.ShapeDtypeStruct((t, ATT), F32), jax.ShapeDtypeStruct((t, ATT), F32), jax.ShapeDtypeStruct((t, N_COL), BF16)],
        input_output_aliases={3: 2},
    )(dpa, proj, att, dproj)


def _place_qkv(parts, dproj, col_block, tm, name):
    t = dproj.shape[0]

    def body(p0, p1, p2, dp_in, dp_ref):
        del dp_in
        for g, ref in enumerate((p0, p1, p2)):
            dp_ref[:, g * ATT:(g + 1) * ATT] = ref[...]

    return pl.pallas_call(
        body, name=name, grid=(t // tm,),
        in_specs=[_rows(tm, ATT)] * 3 + [ANY],
        out_specs=_rows(tm, QKV, col_block), out_shape=jax.ShapeDtypeStruct((t, N_COL), BF16),
        input_output_aliases={3: 0},
    )(*parts, dproj)


def _conv_bwd_rows(dpc, proj, u1, ln_g, ln_b, dproj, tm):
    t = proj.shape[0]

    def body(dpc_ref, z_ref, u1_ref, lg_ref, lb_ref, dp_in, du1_ref, dp_ref, dlg_ref, dlb_ref, dcb_ref):
        del dp_in
        i = pl.program_id(0)
        u1v = u1_ref[...]
        mu = jnp.mean(u1v, axis=-1, keepdims=True)
        xc = u1v - mu
        r = lax.rsqrt(jnp.mean(xc * xc, axis=-1, keepdims=True) + EPS)
        uhat = xc * r
        u2 = uhat * lg_ref[...] + lb_ref[...]
        s2 = _sig(u2)
        z = z_ref[...]
        sz = _sig(z)
        dpc_v = dpc_ref[...]
        dp_ref[...] = (dpc_v * (u2 * s2) * _dsilu(z, sz)).astype(BF16)
        du2 = dpc_v * (z * sz) * _dsilu(u2, s2)
        duhat = du2 * lg_ref[...]
        du1 = r * (duhat - jnp.mean(duhat, axis=-1, keepdims=True) - uhat * jnp.mean(duhat * uhat, axis=-1, keepdims=True))
        du1_ref[...] = du1
        _acc_rows(dlg_ref, i, du2 * uhat)
        _acc_rows(dlb_ref, i, du2)
        _acc_rows(dcb_ref, i, du1)

    vec = _full((1, D))
    return pl.pallas_call(
        body, name="conv_bwd_rows", grid=(t // tm,),
        in_specs=[_rows(tm, D), _rows(tm, D, C_ZC // D), _rows(tm, D), vec, vec, ANY],
        out_specs=[_rows(tm, D), _rows(tm, D, C_ZC // D), vec, vec, vec],
        out_shape=[jax.ShapeDtypeStruct((t, D), F32), jax.ShapeDtypeStruct((t, N_COL), BF16)] + [jax.ShapeDtypeStruct((1, D), F32)] * 3,
        input_output_aliases={5: 1},
    )(dpc, proj, u1, ln_g, ln_b, dproj)


def _conv_bwd_taps(du1, proj, conv_w, dproj, tm):
    t = proj.shape[0]
    hb = tm // HALO
    last = t // HALO - 1

    def body(du_ref, duh_ref, a_ref, b_ref, ah_ref, bh_ref, w_ref, dp_in, dp_ref, dw_ref, dbuf, ubuf, g0):
        del dp_in
        i = pl.program_id(0)
        a, sb = a_ref[...], _sig(b_ref[...])
        ubuf[0:HALO, :] = jnp.where(i > 0, ah_ref[...] * _sig(bh_ref[...]), 0.0)
        ubuf[HALO:HALO + tm, :] = a * sb
        dbuf[0:tm, :] = du_ref[...]
        dbuf[tm:tm + HALO, :] = jnp.where(i < pl.num_programs(0) - 1, duh_ref[...], 0.0)

        @pl.when(i == 0)
        def _():
            dw_ref[...] = jnp.zeros_like(dw_ref)

        def col(ci, carry):
            c0 = pl.multiple_of(ci * 128, 128)
            for rc in range(tm // 64):
                g0[rc * 64:(rc + 1) * 64, pl.ds(c0, 128)] = _conv_taps(
                    jnp.zeros((64, 128), F32), w_ref, dbuf, rc * 64, c0, lambda j: CONV_K - 1 - j)
            for j in range(CONV_K):
                part = jnp.zeros((8, 128), F32)
                for rc in range(tm // 64):
                    off = rc * 64 + HALO - (CONV_K - 1) + j
                    prod = dbuf[rc * 64:(rc + 1) * 64, pl.ds(c0, 128)] * ubuf[off:off + 64, pl.ds(c0, 128)]
                    part = part + jnp.sum(prod.reshape(8, 8, 128), axis=0)
                dw_ref[j:j + 1, pl.ds(c0, 128)] += jnp.sum(part, axis=0, keepdims=True)
            return carry

        lax.fori_loop(0, D // 128, col, 0)
        du0 = g0[...]
        dp_ref[:, 0:D] = (du0 * sb).astype(BF16)
        dp_ref[:, D:2 * D] = (du0 * a * sb * (1.0 - sb)).astype(BF16)

    prev = lambda col: pl.BlockSpec((HALO, D), lambda i: (jnp.maximum(i * hb - 1, 0), col))
    nxt = pl.BlockSpec((HALO, D), lambda i: (jnp.minimum((i + 1) * hb, last), 0))
    return pl.pallas_call(
        body, name="conv_bwd_taps", grid=(t // tm,),
        in_specs=[_rows(tm, D), nxt, _rows(tm, D, 0), _rows(tm, D, 1), prev(0), prev(1), _full((CONV_KP, D)), ANY],
        out_specs=[_rows(tm, 2 * D, 0), _full((CONV_KP, D))],
        out_shape=[jax.ShapeDtypeStruct((t, N_COL), BF16), jax.ShapeDtypeStruct((CONV_KP, D), F32)],
        scratch_shapes=[pltpu.VMEM((tm + HALO, D), F32), pltpu.VMEM((HALO + tm, D), F32), pltpu.VMEM((tm, D), F32)],
        input_output_aliases={7: 0},
    )(du1, du1, proj, proj, proj, proj, conv_w, dproj)


def _prenorm_bwd(dh, x, dout, mod, norm_g, tm):
    t = x.shape[0]

    def body(dh_ref, x_ref, dout_ref, mod_ref, g_ref, gx_ref, dshift_ref, dscale_ref, dg_ref):
        i = pl.program_id(0)
        xv, dhv = x_ref[...], dh_ref[...]
        r = lax.rsqrt(jnp.mean(xv * xv, axis=-1, keepdims=True) + EPS)
        xn = xv * r
        one_scale = 1.0 + mod_ref[:, D:2 * D]
        dxn = dhv * (g_ref[...] * one_scale)
        gx_ref[...] = r * (dxn - xn * jnp.mean(dxn * xn, axis=-1, keepdims=True)) + dout_ref[...]
        _acc_rows(dshift_ref, i, dhv)
        _acc_rows(dscale_ref, i, dhv * xn * g_ref[...])
        _acc_rows(dg_ref, i, dhv * xn * one_scale)

    vec = _full((1, D))
    return pl.pallas_call(
        body, name="prenorm_bwd", grid=(t // tm,),
        in_specs=[_rows(tm, D), _rows(tm, D), _rows(tm, D), _full((1, 3 * D)), vec],
        out_specs=[_rows(tm, D), vec, vec, vec],
        out_shape=[jax.ShapeDtypeStruct((t, D), F32)] + [jax.ShapeDtypeStruct((1, D), F32)] * 3,
    )(dh, x, dout, mod, norm_g)


def _sum_devices(gathered):
    w = gathered.shape[-1]

    def body(g_ref, o_ref):
        acc = g_ref[0]
        for j in range(1, N_DEV):
            acc = acc + g_ref[j]
        o_ref[...] = acc

    return pl.pallas_call(body, name="sum_devices", grid=(1,), in_specs=[_full(gathered.shape)], out_specs=_full((1, w)),
                          out_shape=jax.ShapeDtypeStruct((1, w), F32))(gathered)


def _rope_tables(positions):
    half = HEAD // 8
    t = positions.shape[-1]
    inv_freq = ROPE_THETA ** (-(jnp.arange(half, dtype=F32) * 2.0 / (2 * half)))
    ang = positions.reshape(t, 1).astype(F32) * inv_freq
    cos, sin = jnp.cos(ang), jnp.sin(ang)
    zeros = lambda n: jnp.zeros((t, n), F32)
    c64 = jnp.concatenate([cos, cos, jnp.ones((t, HEAD - 2 * half), F32)], axis=1)
    lo64 = jnp.concatenate([-sin, zeros(HEAD - half)], axis=1)
    hi64 = jnp.concatenate([zeros(half), sin, zeros(HEAD - 2 * half)], axis=1)
    return tuple(jnp.tile(a, (1, 2)) for a in (c64, lo64, hi64))


def kernel(x, c, positions, norm_g, w_ada, b_ada, w_in, conv_w, conv_b, conv_ln_g, conv_ln_b, w_conv_out, w_att_out, w_o, final_g, loss_target, m_norm_g, m_w_ada, m_b_ada, m_w_in, m_conv_w, m_conv_b, m_conv_ln_g, m_conv_ln_b, m_w_conv_out, m_w_att_out, m_w_o, m_final_g, v_norm_g, v_w_ada, v_b_ada, v_w_in, v_conv_w, v_conv_b, v_conv_ln_g, v_conv_ln_b, v_w_conv_out, v_w_att_out, v_w_o, v_final_g):
    me = 4 * lax.axis_index("x") + 2 * lax.axis_index("y") + lax.axis_index("c")
    x2, tgt = x[0], loss_target[0]
    t = x2.shape[0]
    te = 512 if t % 512 == 0 else 256
    tcv = 256
    tmm = 1024 if t % 1024 == 0 else 256
    n_ada = w_ada.shape[-1]

    pad_taps = lambda a: jnp.pad(a[0], ((0, CONV_KP - CONV_K), (0, 0)))
    shards = (_cast_bf16(w_in[0], "cast_w_in"), _cast_bf16(w_conv_out[0], "cast_w_conv_out"),
              _cast_bf16(w_att_out[0], "cast_w_att_out"), _cast_bf16(w_o[0], "cast_w_o"), pad_taps(conv_w))
    w_in_f, w_co_f, w_ao_f, w_o_f, conv_w_f = _gather_weights(shards)

    c_all = _allgather_small(c, "gather_c").reshape(N_DEV, D)
    b_ada_l = lax.dynamic_slice(b_ada, (0, me * n_ada), (1, n_ada))
    parts = _allgather_small(_mod_part(c_all, w_ada[0], b_ada_l), "gather_mod")
    mod = lax.dynamic_slice(parts, (0, me, 0), (N_DEV, 1, n_ada)).reshape(1, N_DEV * n_ada)

    h = _prenorm(x2, mod, norm_g, te)
    proj = _matmul(h, w_in_f, tm=tmm, tn=1280, tk=D, name="proj")
    u1, pc = _conv_fwd(proj, conv_w_f, conv_b, conv_ln_g, conv_ln_b, tcv)
    tables = _rope_tables(positions)
    parts_att = []
    for gi, dil in GROUPS:
        parts_att += _att_fwd(proj, tables, gi, dil)
    att, lse, pa = _att_combine(parts_att, proj, te)
    y_conv = _matmul(pc, w_co_f, tm=tmm, tn=D, tk=D, name="y_conv")
    y_att = _matmul(pa, w_ao_f, tm=tmm, tn=D, tk=ATT, name="y_att")
    merged = _merge_fwd(proj, y_conv, y_att, te)
    o = _matmul(merged, w_o_f, tm=tmm, tn=D, tk=D, name="out_proj")
    dout, do, sq_sum, g_final, d_gate = _loss_head(x2, o, mod, final_g.reshape(1, D), tgt, te)

    dmerged = _matmul(do, w_o_f, tb=True, tm=tmm, tn=D, tk=D, name="d_merged")
    dw_o = _matmul(merged, do, ta=True, tm=D, tn=D, tk=512, name="dw_o")
    dyc, dya, dproj = _merge_bwd(dmerged, proj, y_conv, y_att, te)
    dpc = _matmul(dyc, w_co_f, tb=True, tm=tmm, tn=D, tk=D, name="d_pc")
    dw_co = _matmul(pc, dyc, ta=True, tm=D, tn=D, tk=512, name="dw_conv_out")
    dpa = _matmul(dya, w_ao_f, tb=True, tm=tmm, tn=ATT, tk=D, name="d_pa")
    dw_ao = _matmul(pa, dya, ta=True, tm=ATT, tn=D, tk=512, name="dw_att_out")
    datt, dsum, dproj = _att_pre_bwd(dpa, proj, att, dproj, te)
    dqs, dks, dvs = [], [], []
    for gi, dil in GROUPS:
        dq, dk, dv = _att_bwd(proj, tables, datt, dsum, lse, gi, dil)
        dqs.append(dq), dks.append(dk), dvs.append(dv)
    dproj = _place_qkv(dqs, dproj, C_Q // QKV, te, "place_dq")
    dproj = _place_qkv(dks, dproj, C_K // QKV, te, "place_dk")
    dproj = _place_qkv(dvs, dproj, C_V // QKV, te, "place_dv")
    du1, dproj, d_ln_g, d_ln_b, d_conv_b = _conv_bwd_rows(dpc, proj, u1, conv_ln_g, conv_ln_b, dproj, te)
    dproj, dconv_w = _conv_bwd_taps(du1, proj, conv_w_f, dproj, tcv)
    dh = _matmul(dproj, w_in_f, tb=True, tm=tmm, tn=D, tk=1024, name="d_h")
    dw_in = _matmul(h, dproj, ta=True, tm=D, tn=2048, tk=512, name="dw_in")
    grad_x, d_shift, d_scale, d_norm_g = _prenorm_bwd(dh, x2, dout, mod, norm_g, te)

    packed = jnp.concatenate([d_shift, d_scale, d_gate, d_norm_g, d_conv_b, d_ln_g, d_ln_b, g_final, sq_sum], axis=1)
    gathered = _allgather_small(packed, "gather_partials")
    total = _sum_devices(gathered)
    seg = lambda k, n=1: total[:, k * D:(k + n) * D]
    g_b_ada, g_norm_g, g_conv_b, g_ln_g, g_ln_b, g_final_g = seg(0, 3), seg(3), seg(4), seg(5), seg(6), seg(7)
    loss = (0.5 / D) * jnp.sum(seg(8))
    dmod_all = gathered[:, 0, 0:3 * D]
    dmod_cols = lax.dynamic_slice(dmod_all, (0, me * n_ada), (N_DEV, n_ada))
    g_w_ada, d_w_ada, nm_w_ada, nv_w_ada = _w_ada_update(c_all.T, dmod_cols, w_ada[0], m_w_ada[0], v_w_ada[0])

    small = {}
    for name, g, w, m, v in (("norm_g", g_norm_g, norm_g, m_norm_g, v_norm_g), ("b_ada", g_b_ada, b_ada, m_b_ada, v_b_ada),
                             ("conv_b", g_conv_b, conv_b, m_conv_b, v_conv_b), ("conv_ln_g", g_ln_g, conv_ln_g, m_conv_ln_g, v_conv_ln_g),
                             ("conv_ln_b", g_ln_b, conv_ln_b, m_conv_ln_b, v_conv_ln_b),
                             ("final_g", g_final_g, final_g.reshape(1, D), m_final_g.reshape(1, D), v_final_g.reshape(1, D))):
        small[name] = (g,) + tuple(_adamw_small(g, w, m, v, "adamw_" + name))

    slots = _scatter_grads((dw_in, dw_co, dw_ao, dw_o, dconv_w))
    big = {
        "w_in": _sum_adamw(slots[0], w_in[0], m_w_in[0], v_w_in[0], 256, "adamw_w_in"),
        "w_conv_out": _sum_adamw(slots[1], w_conv_out[0], m_w_conv_out[0], v_w_conv_out[0], 128, "adamw_w_conv_out"),
        "w_att_out": _sum_adamw(slots[2], w_att_out[0], m_w_att_out[0], v_w_att_out[0], 512, "adamw_w_att_out"),
        "w_o": _sum_adamw(slots[3], w_o[0], m_w_o[0], v_w_o[0], 128, "adamw_w_o"),
        "conv_w": [r[:CONV_K] for r in _sum_adamw(slots[4], pad_taps(conv_w), pad_taps(m_conv_w), pad_taps(v_conv_w), CONV_KP, "adamw_conv_w")],
    }
    big["w_ada"] = (g_w_ada, d_w_ada, nm_w_ada, nv_w_ada)

    order = ("norm_g", "w_ada", "b_ada", "w_in", "conv_w", "conv_b", "conv_ln_g", "conv_ln_b", "w_conv_out", "w_att_out", "w_o", "final_g")
    lead = lambda name, a: a.reshape(D) if name == "final_g" else (a[None] if name in big else a)
    result = {**small, **big}
    outs = [loss, grad_x[None]]
    for field in range(4):
        outs += [lead(name, result[name][field]) for name in order]
    return tuple(outs)
```

```python
import functools

import jax
import jax.numpy as jnp
from jax import lax
from jax.experimental import pallas as pl
from jax.experimental.pallas import tpu as pltpu

F32 = jnp.float32
BF16 = jnp.bfloat16

N_DEV = 8
D = 1024
N_COL = 10240
C_A, C_B, C_ZC, C_Q, C_K, C_V, C_ZA, C_GC, C_GA = 0, 1024, 2048, 3072, 4608, 6144, 7680, 8192, 9216
QKV = 1536
ATT = 512
HEAD = 64
BLK = 128
TILE = 2048
GROUPS = ((0, 1), (1, 4), (2, 16))
CONV_K = 31
CONV_KP = 32
HALO = 32
EPS = 1e-6
NEG_INF = -1e30
ROPE_THETA = 500000.0
SM_SCALE = HEAD ** -0.5

ADAM_LR, ADAM_B1, ADAM_B2, ADAM_EPS, ADAM_WD, ADAM_STEP = 0.001, 0.9, 0.999, 1e-08, 0.01, 10

MESH = pl.DeviceIdType.MESH
ANY = pl.BlockSpec(memory_space=pl.ANY)


def _sig(v):
    return 1.0 / (1.0 + jnp.exp(-v))


def _dsilu(v, s):
    return s * (1.0 + v * (1.0 - s))


def _full(shape):
    return pl.BlockSpec(shape, lambda *_: (0,) * len(shape))


def _rows(tm, width, col=0):
    return pl.BlockSpec((tm, width), lambda i: (i, col))


def _matmul(a, b, *, ta=False, tb=False, out_dtype=F32, tm, tn, tk, name):
    m, k = (a.shape[1], a.shape[0]) if ta else a.shape
    n = b.shape[0] if tb else b.shape[1]
    assert (b.shape[1] if tb else b.shape[0]) == k
    assert m % tm == 0 and n % tn == 0 and k % tk == 0
    nk = k // tk
    dims = (((0 if ta else 1,), (1 if tb else 0,)), ((), ()))
    use_scratch = out_dtype != F32 and nk > 1

    def body(a_ref, b_ref, o_ref, *scratch):
        p = lax.dot_general(a_ref[...], b_ref[...], dims, preferred_element_type=F32)
        if nk == 1:
            o_ref[...] = p.astype(out_dtype)
            return
        acc = scratch[0] if use_scratch else o_ref
        kk = pl.program_id(2)

        @pl.when(kk == 0)
        def _():
            acc[...] = p

        @pl.when(kk > 0)
        def _():
            acc[...] += p

        if use_scratch:
            @pl.when(kk == nk - 1)
            def _():
                o_ref[...] = acc[...].astype(out_dtype)

    a_spec = pl.BlockSpec((tk, tm), lambda i, j, kk: (kk, i)) if ta else pl.BlockSpec((tm, tk), lambda i, j, kk: (i, kk))
    b_spec = pl.BlockSpec((tn, tk), lambda i, j, kk: (j, kk)) if tb else pl.BlockSpec((tk, tn), lambda i, j, kk: (kk, j))
    return pl.pallas_call(
        body, name=name, grid=(m // tm, n // tn, nk),
        in_specs=[a_spec, b_spec],
        out_specs=pl.BlockSpec((tm, tn), lambda i, j, kk: (i, j)),
        out_shape=jax.ShapeDtypeStruct((m, n), out_dtype),
        scratch_shapes=[pltpu.VMEM((tm, tn), F32)] if use_scratch else [],
    )(a, b)


def _me_and_peers():
    x, y, c = lax.axis_index("x"), lax.axis_index("y"), lax.axis_index("c")
    me = 4 * x + 2 * y + c
    peers = []
    for k in range(1, N_DEV):
        px, py, pc = x ^ (k >> 2), y ^ ((k >> 1) & 1), c ^ (k & 1)
        peers.append(((px, py, pc), 4 * px + 2 * py + pc))
    return me, peers


def _allgather_small(v, name):
    r, c = v.shape

    def body(v_ref, out_ref, send_sems, recv_sems):
        me, peers = _me_and_peers()
        out_ref[me] = v_ref[...]
        copies = []
        for k, (dev, _) in enumerate(peers):
            cp = pltpu.make_async_remote_copy(src_ref=v_ref, dst_ref=out_ref.at[me], send_sem=send_sems.at[k],
                                              recv_sem=recv_sems.at[k], device_id=dev, device_id_type=MESH)
            cp.start()
            copies.append(cp)
        for k, (dev, idx) in enumerate(peers):
            pltpu.make_async_remote_copy(src_ref=v_ref, dst_ref=out_ref.at[idx], send_sem=send_sems.at[k],
                                         recv_sem=recv_sems.at[k], device_id=dev, device_id_type=MESH).wait_recv()
        for cp in copies:
            cp.wait_send()

    return pl.pallas_call(
        body, name=name,
        in_specs=[pl.BlockSpec(memory_space=pltpu.VMEM)],
        out_specs=pl.BlockSpec(memory_space=pltpu.VMEM),
        out_shape=jax.ShapeDtypeStruct((N_DEV, r, c), v.dtype),
        scratch_shapes=[pltpu.SemaphoreType.DMA((N_DEV - 1,)), pltpu.SemaphoreType.DMA((N_DEV - 1,))],
    )(v)


def _window(ref, kind, idx, size):
    start = pl.multiple_of(idx * size, size)
    if kind == "rows":
        return ref.at[pl.ds(start, size), :]
    return ref.at[:, pl.ds(start, size)]


_BIG = (("cols", N_COL // N_DEV), ("rows", D // N_DEV), ("cols", D // N_DEV), ("rows", D // N_DEV), ("cols", D // N_DEV))


def _gather_weights(shards):
    full_shapes = []
    for s, (kind, size) in zip(shards, _BIG):
        full_shapes.append(jax.ShapeDtypeStruct((s.shape[0] * N_DEV, s.shape[1]) if kind == "rows"
                                                else (s.shape[0], s.shape[1] * N_DEV), s.dtype))
    nt = len(shards)

    def body(*refs):
        src, dst = refs[:nt], refs[nt:2 * nt]
        send_sems, recv_sems, local_sems = refs[2 * nt:]
        me, peers = _me_and_peers()
        local, sent = [], []
        for t in range(nt):
            kind, size = _BIG[t]
            cp = pltpu.make_async_copy(src[t], _window(dst[t], kind, me, size), local_sems.at[t])
            cp.start()
            local.append(cp)
            for k, (dev, _) in enumerate(peers):
                rc = pltpu.make_async_remote_copy(src_ref=src[t], dst_ref=_window(dst[t], kind, me, size),
                                                  send_sem=send_sems.at[t, k], recv_sem=recv_sems.at[t, k],
                                                  device_id=dev, device_id_type=MESH)
                rc.start()
                sent.append(rc)
        for t in range(nt):
            kind, size = _BIG[t]
            for k, (dev, idx) in enumerate(peers):
                pltpu.make_async_remote_copy(src_ref=src[t], dst_ref=_window(dst[t], kind, idx, size),
                                             send_sem=send_sems.at[t, k], recv_sem=recv_sems.at[t, k],
                                             device_id=dev, device_id_type=MESH).wait_recv()
        for rc in sent:
            rc.wait_send()
        for cp in local:
            cp.wait()

    return pl.pallas_call(
        body, name="gather_weights",
        in_specs=[ANY] * nt, out_specs=[ANY] * nt, out_shape=full_shapes,
        scratch_shapes=[pltpu.SemaphoreType.DMA((nt, N_DEV - 1)), pltpu.SemaphoreType.DMA((nt, N_DEV - 1)),
                        pltpu.SemaphoreType.DMA((nt,))],
    )(*shards)


def _scatter_grads(grads):
    nt = len(grads)
    slot_shapes = []
    for g, (kind, size) in zip(grads, _BIG):
        shard = (size, g.shape[1]) if kind == "rows" else (g.shape[0], size)
        slot_shapes.append(jax.ShapeDtypeStruct((N_DEV,) + shard, g.dtype))

    def body(*refs):
        src, dst = refs[:nt], refs[nt:2 * nt]
        send_sems, recv_sems, local_sems = refs[2 * nt:]
        me, peers = _me_and_peers()
        local, sent = [], []
        for t in range(nt):
            kind, size = _BIG[t]
            cp = pltpu.make_async_copy(_window(src[t], kind, me, size), dst[t].at[me], local_sems.at[t])
            cp.start()
            local.append(cp)
            for k, (dev, idx) in enumerate(peers):
                rc = pltpu.make_async_remote_copy(src_ref=_window(src[t], kind, idx, size), dst_ref=dst[t].at[me],
                                                  send_sem=send_sems.at[t, k], recv_sem=recv_sems.at[t, k],
                                                  device_id=dev, device_id_type=MESH)
                rc.start()
                sent.append(rc)
        for t in range(nt):
            kind, size = _BIG[t]
            for k, (dev, idx) in enumerate(peers):
                pltpu.make_async_remote_copy(src_ref=_window(src[t], kind, me, size), dst_ref=dst[t].at[idx],
                                             send_sem=send_sems.at[t, k], recv_sem=recv_sems.at[t, k],
                                             device_id=dev, device_id_type=MESH).wait_recv()
        for rc in sent:
            rc.wait_send()
        for cp in local:
            cp.wait()

    return pl.pallas_call(
        body, name="scatter_grads",
        in_specs=[ANY] * nt, out_specs=[ANY] * nt, out_shape=slot_shapes,
        scratch_shapes=[pltpu.SemaphoreType.DMA((nt, N_DEV - 1)), pltpu.SemaphoreType.DMA((nt, N_DEV - 1)),
                        pltpu.SemaphoreType.DMA((nt,))],
    )(*grads)


def _adamw_math(w, g, m, v):
    m = ADAM_B1 * m + (1.0 - ADAM_B1) * g
    v = ADAM_B2 * v + (1.0 - ADAM_B2) * (g * g)
    m_hat = m / (1.0 - ADAM_B1 ** ADAM_STEP)
    v_hat = v / (1.0 - ADAM_B2 ** ADAM_STEP)
    delta = -ADAM_LR * (m_hat / (jnp.sqrt(v_hat) + ADAM_EPS) + ADAM_WD * w)
    return delta, m, v


def _sum_adamw(slots, w, m, v, tr, name):
    _, r, c = slots.shape
    assert r % tr == 0

    def body(s_ref, w_ref, m_ref, v_ref, g_ref, d_ref, nm_ref, nv_ref):
        g = s_ref[0].astype(F32)
        for j in range(1, N_DEV):
            g = g + s_ref[j].astype(F32)
        delta, nm, nv = _adamw_math(w_ref[...], g, m_ref[...], v_ref[...])
        g_ref[...] = g
        d_ref[...] = delta
        nm_ref[...] = nm
        nv_ref[...] = nv

    blk = pl.BlockSpec((tr, c), lambda i: (i, 0))
    return pl.pallas_call(
        body, name=name, grid=(r // tr,),
        in_specs=[pl.BlockSpec((N_DEV, tr, c), lambda i: (0, i, 0)), blk, blk, blk],
        out_specs=[blk] * 4, out_shape=[jax.ShapeDtypeStruct((r, c), F32)] * 4,
    )(slots, w, m, v)


def _adamw_small(g, w, m, v, name):
    def body(g_ref, w_ref, m_ref, v_ref, d_ref, nm_ref, nv_ref):
        delta, nm, nv = _adamw_math(w_ref[...], g_ref[...], m_ref[...], v_ref[...])
        d_ref[...] = delta
        nm_ref[...] = nm
        nv_ref[...] = nv

    spec = _full(g.shape)
    return pl.pallas_call(body, name=name, grid=(1,), in_specs=[spec] * 4, out_specs=[spec] * 3,
                          out_shape=[jax.ShapeDtypeStruct(g.shape, F32)] * 3)(g, w, m, v)


def _mod_part(c_all, w_ada_l, b_ada_l):
    n = w_ada_l.shape[1]

    def body(c_ref, w_ref, b_ref, o_ref):
        o_ref[...] = jnp.dot(c_ref[...], w_ref[...], preferred_element_type=F32,
                             precision=lax.Precision.HIGHEST) + b_ref[...]

    return pl.pallas_call(body, name="mod_part", grid=(1,),
                          in_specs=[_full(c_all.shape), _full(w_ada_l.shape), _full(b_ada_l.shape)],
                          out_specs=_full((N_DEV, n)), out_shape=jax.ShapeDtypeStruct((N_DEV, n), F32))(c_all, w_ada_l, b_ada_l)


def _w_ada_update(c_all_t, dmod_cols, w, m, v):
    def body(c_ref, dm_ref, w_ref, m_ref, v_ref, g_ref, d_ref, nm_ref, nv_ref):
        g = c_ref[:, 0:1] * dm_ref[0:1, :]
        for b in range(1, N_DEV):
            g = g + c_ref[:, b:b + 1] * dm_ref[b:b + 1, :]
        delta, nm, nv = _adamw_math(w_ref[...], g, m_ref[...], v_ref[...])
        g_ref[...] = g
        d_ref[...] = delta
        nm_ref[...] = nm
        nv_ref[...] = nv

    spec = _full(w.shape)
    return pl.pallas_call(body, name="w_ada_update", grid=(1,),
                          in_specs=[_full(c_all_t.shape), _full(dmod_cols.shape), spec, spec, spec],
                          out_specs=[spec] * 4, out_shape=[jax.ShapeDtypeStruct(w.shape, F32)] * 4)(c_all_t, dmod_cols, w, m, v)


def _cast_bf16(w, name):
    def body(w_ref, o_ref):
        o_ref[...] = w_ref[...].astype(BF16)

    return pl.pallas_call(body, name=name, grid=(1,), in_specs=[_full(w.shape)], out_specs=_full(w.shape),
                          out_shape=jax.ShapeDtypeStruct(w.shape, BF16))(w)


def _prenorm(x, mod, norm_g, tm):
    t = x.shape[0]

    def body(x_ref, mod_ref, g_ref, h_ref):
        xv = x_ref[...]
        r = lax.rsqrt(jnp.mean(xv * xv, axis=-1, keepdims=True) + EPS)
        h = (xv * r) * g_ref[...] * (1.0 + mod_ref[:, D:2 * D]) + mod_ref[:, 0:D]
        h_ref[...] = h.astype(BF16)

    return pl.pallas_call(body, name="prenorm", grid=(t // tm,),
                          in_specs=[_rows(tm, D), _full((1, 3 * D)), _full((1, D))],
                          out_specs=_rows(tm, D), out_shape=jax.ShapeDtypeStruct((t, D), BF16))(x, mod, norm_g)


def _rope_apply(t, cos, s_lo, s_hi):
    return t * cos + pltpu.roll(t, 120, 1) * s_lo + pltpu.roll(t, 8, 1) * s_hi


def _conv_taps(acc_init, w_ref, buf, row0, c0, offset_of_tap):
    acc = acc_init
    for j in range(CONV_K):
        acc = acc + w_ref[j:j + 1, pl.ds(c0, 128)] * buf[row0 + offset_of_tap(j): row0 + offset_of_tap(j) + 64, pl.ds(c0, 128)]
    return acc


def _conv_fwd(proj, conv_w, conv_b, ln_g, ln_b, tm):
    t = proj.shape[0]
    hb = tm // HALO

    def body(a_ref, b_ref, z_ref, ah_ref, bh_ref, w_ref, cb_ref, lg_ref, lb_ref, u1_ref, pc_ref, ubuf):
        i = pl.program_id(0)
        u0h = ah_ref[...] * _sig(bh_ref[...])
        ubuf[0:HALO, :] = jnp.where(i > 0, u0h, 0.0)
        ubuf[HALO:HALO + tm, :] = a_ref[...] * _sig(b_ref[...])

        def col(ci, carry):
            c0 = pl.multiple_of(ci * 128, 128)
            for rc in range(tm // 64):
                init = jnp.zeros((64, 128), F32)
                acc = _conv_taps(init, w_ref, ubuf, rc * 64, c0, lambda j: HALO - (CONV_K - 1) + j)
                u1_ref[rc * 64:(rc + 1) * 64, pl.ds(c0, 128)] = acc + cb_ref[:, pl.ds(c0, 128)]
            return carry

        lax.fori_loop(0, D // 128, col, 0)
        u1 = u1_ref[...]
        mu = jnp.mean(u1, axis=-1, keepdims=True)
        xc = u1 - mu
        var = jnp.mean(xc * xc, axis=-1, keepdims=True)
        u2 = xc * lax.rsqrt(var + EPS) * lg_ref[...] + lb_ref[...]
        z = z_ref[...]
        pc_ref[...] = (u2 * _sig(u2) * (z * _sig(z))).astype(BF16)

    halo = pl.BlockSpec((HALO, D), lambda i: (jnp.maximum(i * hb - 1, 0), 0))
    halo_b = pl.BlockSpec((HALO, D), lambda i: (jnp.maximum(i * hb - 1, 0), 1))
    return pl.pallas_call(
        body, name="conv_fwd", grid=(t // tm,),
        in_specs=[_rows(tm, D, 0), _rows(tm, D, 1), _rows(tm, D, 2), halo, halo_b,
                  _full((CONV_KP, D)), _full((1, D)), _full((1, D)), _full((1, D))],
        out_specs=[_rows(tm, D), _rows(tm, D)],
        out_shape=[jax.ShapeDtypeStruct((t, D), F32), jax.ShapeDtypeStruct((t, D), BF16)],
        scratch_shapes=[pltpu.VMEM((HALO + tm, D), F32)],
    )(proj, proj, proj, proj, proj, conv_w, conv_b, ln_g, ln_b)


def _band_masks_t(has_prev):
    key = lax.broadcasted_iota(jnp.int32, (BLK, BLK), 0)
    qry = lax.broadcasted_iota(jnp.int32, (BLK, BLK), 1)
    return jnp.logical_and(key >= qry, has_prev), key <= qry


def _head_lanes(pair, hh):
    lane = lax.broadcasted_iota(jnp.int32, pair.shape, 1)
    return jnp.where((lane >= hh * HEAD) & (lane < (hh + 1) * HEAD), pair, jnp.zeros_like(pair))


def _pair_mask(has_prev):
    mask_p, mask_c = _band_masks_t(has_prev)
    both = jnp.concatenate([mask_p, mask_c], axis=0)
    return jnp.concatenate([both, both], axis=1)


def _query_pair(pair):
    return jnp.concatenate([_head_lanes(pair, 0), _head_lanes(pair, 1)], axis=0)


def _key_pair(ref, prev, cur):
    return jnp.concatenate([ref[pl.ds(prev, BLK), :], ref[pl.ds(cur, BLK), :]], axis=0)


def _own_head(both):
    return jnp.concatenate([both[0:HEAD, 0:BLK], both[HEAD:2 * HEAD, BLK:2 * BLK]], axis=0)


def _store_transposed(dst, base, src):
    for j in range(TILE // BLK):
        dst[base // BLK + j] = src[j * BLK:(j + 1) * BLK, :].T.astype(BF16)


class _Dilated:
    def __init__(self, dil):
        self.dil = dil
        self.per = TILE // dil
        self.nbr = self.per // BLK

    def spread(self, dst, base, src_ref, dtype):
        for r in range(self.dil):
            rows = src_ref[pl.ds(r, self.per, stride=self.dil), :] if self.dil > 1 else src_ref[...]
            dst[pl.ds(pl.multiple_of(base + r * self.per, BLK), self.per), :] = rows.astype(dtype)

    def gather(self, dst_ref, src, base):
        for r in range(self.dil):
            rows = src[pl.ds(pl.multiple_of(base + r * self.per, BLK), self.per), :]
            if self.dil > 1:
                dst_ref[pl.ds(r, self.per, stride=self.dil), :] = rows
            else:
                dst_ref[...] = rows

    def block_rows(self, b, i, cur, prv):
        n = b % self.nbr
        row = pl.multiple_of(b * BLK, BLK)
        has_prev = jnp.logical_or(n > 0, i > 0)
        prev = jnp.where(n > 0, cur + row - BLK, jnp.where(i > 0, prv + row + (self.nbr - 1) * BLK, cur + row))
        return row, pl.multiple_of(prev, BLK), has_prev


def _slots(i):
    return pl.multiple_of((i % 2) * TILE, TILE), pl.multiple_of(((i + 1) % 2) * TILE, TILE)


def _nt(a, b):
    return lax.dot_general(a, b, (((1,), (1,)), ((), ())), preferred_element_type=F32)


def _qkv_specs(gi, clamp_to=None):
    def spec(col0):
        def imap(hp, i):
            return (i if clamp_to is None else jnp.minimum(i, clamp_to), (col0 + gi * ATT) // 128 + hp)
        return pl.BlockSpec((TILE, 128), imap)
    return [spec(C_Q), spec(C_K), spec(C_V)]


def _att_fwd(proj, tables, gi, dil):
    t = proj.shape[0]
    dl = _Dilated(dil)

    def body(q_ref, k_ref, v_ref, c_ref, lo_ref, hi_ref, o_ref, lse_ref, tmp, qd, kd, vt, od, ld):
        i = pl.program_id(1)
        cur, prv = _slots(i)
        cs, lo, hi = c_ref[...], lo_ref[...], hi_ref[...]
        tmp[...] = _rope_apply(q_ref[...], cs, lo, hi) * SM_SCALE
        dl.spread(qd, 0, tmp, BF16)
        tmp[...] = _rope_apply(k_ref[...], cs, lo, hi)
        dl.spread(kd, cur, tmp, BF16)
        dl.spread(tmp, 0, v_ref, F32)
        _store_transposed(vt, cur, tmp)

        def block(b, carry):
            row, prev, has_prev = dl.block_rows(b, i, cur, prv)
            s = jnp.where(_pair_mask(has_prev), _nt(_key_pair(kd, prev, cur + row), _query_pair(qd[pl.ds(row, BLK), :])), NEG_INF)
            mx = jnp.max(s, axis=0, keepdims=True)
            p = jnp.exp(s - mx)
            den = jnp.sum(p, axis=0, keepdims=True)
            v_t = jnp.concatenate([vt[prev // BLK], vt[(cur + row) // BLK]], axis=1)
            acc = jnp.dot(v_t, p.astype(BF16), preferred_element_type=F32) / den
            lse = mx + jnp.log(den)
            od[pl.ds(row, BLK), :] = _own_head(acc).T
            ld[pl.ds(row, BLK), :] = _own_head(jnp.broadcast_to(lse, (2 * HEAD, 2 * BLK))).T
            return carry

        lax.fori_loop(0, TILE // BLK, block, 0, unroll=4)
        dl.gather(o_ref, od, 0)
        dl.gather(lse_ref, ld, 0)

    tab = pl.BlockSpec((TILE, 128), lambda hp, i: (i, 0))
    out_spec = pl.BlockSpec((TILE, 128), lambda hp, i: (i, hp))
    return pl.pallas_call(
        body, name=f"att_fwd_g{gi}", grid=(ATT // 128, t // TILE),
        in_specs=_qkv_specs(gi) + [tab] * 3,
        out_specs=[out_spec] * 2, out_shape=[jax.ShapeDtypeStruct((t, ATT), F32)] * 2,
        scratch_shapes=[pltpu.VMEM((TILE, 128), F32), pltpu.VMEM((TILE, 128), BF16), pltpu.VMEM((2 * TILE, 128), BF16),
                        pltpu.VMEM((2 * TILE // BLK, 128, BLK), BF16), pltpu.VMEM((TILE, 128), F32), pltpu.VMEM((TILE, 128), F32)],
    )(proj, proj, proj, *tables)


def _att_combine(parts, proj, tm):
    t = proj.shape[0]

    def body(o0, l0, o1, l1, o2, l2, z_ref, att_ref, lse_ref, pa_ref):
        m_all = jnp.maximum(jnp.maximum(l0[...], l1[...]), l2[...])
        w0, w1, w2 = jnp.exp(l0[...] - m_all), jnp.exp(l1[...] - m_all), jnp.exp(l2[...] - m_all)
        den = w0 + w1 + w2
        att = (w0 * o0[...] + w1 * o1[...] + w2 * o2[...]) / den
        z = z_ref[...]
        att_ref[...] = att
        lse_ref[...] = m_all + jnp.log(den)
        pa_ref[...] = (att * (z * _sig(z))).astype(BF16)

    spec = _rows(tm, ATT)
    return pl.pallas_call(
        body, name="att_combine", grid=(t // tm,),
        in_specs=[spec] * 6 + [_rows(tm, ATT, C_ZA // ATT)],
        out_specs=[spec] * 3,
        out_shape=[jax.ShapeDtypeStruct((t, ATT), F32)] * 2 + [jax.ShapeDtypeStruct((t, ATT), BF16)],
    )(*parts, proj)


def _att_bwd(proj, tables, datt, dsum, lse, gi, dil):
    t = proj.shape[0]
    nt = t // TILE
    dl = _Dilated(dil)

    def body(q_ref, k_ref, v_ref, c_ref, lo_ref, hi_ref, cl_ref, lol_ref, hil_ref, do_ref, ds_ref, lse_ref,
             dq_ref, dk_ref, dv_ref, tmp, qd, kd, vd, dod, dsd, lsd, dqd, dkd, dvd, kt):
        i = pl.program_id(1)
        cur, prv = _slots(i)

        @pl.when(i < nt)
        def _():
            cs, lo, hi = c_ref[...], lo_ref[...], hi_ref[...]
            tmp[...] = _rope_apply(q_ref[...], cs, lo, hi) * SM_SCALE
            dl.spread(qd, 0, tmp, BF16)
            tmp[...] = _rope_apply(k_ref[...], cs, lo, hi)
            dl.spread(kd, cur, tmp, BF16)
            dl.spread(dqd, 0, tmp, F32)
            _store_transposed(kt, cur, dqd)
            dl.spread(vd, cur, v_ref, BF16)
            dl.spread(dod, 0, do_ref, BF16)
            dl.spread(dsd, 0, ds_ref, F32)
            dl.spread(lsd, 0, lse_ref, F32)
            dkd[pl.ds(cur, TILE), :] = jnp.zeros((TILE, 128), F32)
            dvd[pl.ds(cur, TILE), :] = jnp.zeros((TILE, 128), F32)

            def block(b, carry):
                row, prev, has_prev = dl.block_rows(b, i, cur, prv)
                q_pair, do_pair = _query_pair(qd[pl.ds(row, BLK), :]), _query_pair(dod[pl.ds(row, BLK), :])
                k_pair, v_pair = _key_pair(kd, prev, cur + row), _key_pair(vd, prev, cur + row)
                ds_t, ls_t = dsd[pl.ds(row, BLK), :].T, lsd[pl.ds(row, BLK), :].T
                lse = jnp.concatenate([ls_t[0:1, :], ls_t[HEAD:HEAD + 1, :]], axis=1)
                dsm = jnp.concatenate([ds_t[0:1, :], ds_t[HEAD:HEAD + 1, :]], axis=1)
                p = jnp.exp(jnp.where(_pair_mask(has_prev), _nt(k_pair, q_pair), NEG_INF) - lse)
                ds = (p * (_nt(v_pair, do_pair) - dsm)).astype(BF16)
                k_t = jnp.concatenate([kt[prev // BLK], kt[(cur + row) // BLK]], axis=1)
                dqd[pl.ds(row, BLK), :] = _own_head(jnp.dot(k_t, ds, preferred_element_type=F32)).T * SM_SCALE
                dk = jnp.dot(ds, q_pair, preferred_element_type=F32)
                dv = jnp.dot(p.astype(BF16), do_pair, preferred_element_type=F32)
                dkd[pl.ds(cur + row, BLK), :] += dk[BLK:2 * BLK, :]
                dvd[pl.ds(cur + row, BLK), :] += dv[BLK:2 * BLK, :]
                dkd[pl.ds(prev, BLK), :] += dk[0:BLK, :]
                dvd[pl.ds(prev, BLK), :] += dv[0:BLK, :]
                return carry

            lax.fori_loop(0, TILE // BLK, block, 0, unroll=4)
            dl.gather(tmp, dqd, 0)
            dq_ref[...] = _rope_apply(tmp[...], cs, -lo, -hi).astype(BF16)

        @pl.when(i > 0)
        def _():
            dl.gather(tmp, dkd, prv)
            dk_ref[...] = _rope_apply(tmp[...], cl_ref[...], -lol_ref[...], -hil_ref[...]).astype(BF16)
            dl.gather(tmp, dvd, prv)
            dv_ref[...] = tmp[...].astype(BF16)

    now = lambda col: pl.BlockSpec((TILE, 128), lambda hp, i: (jnp.minimum(i, nt - 1), col(hp)))
    lag = lambda col: pl.BlockSpec((TILE, 128), lambda hp, i: (jnp.maximum(i - 1, 0), col(hp)))
    first, pair = (lambda hp: 0), (lambda hp: hp)
    return pl.pallas_call(
        body, name=f"att_bwd_g{gi}", grid=(ATT // 128, nt + 1),
        in_specs=_qkv_specs(gi, nt - 1) + [now(first)] * 3 + [lag(first)] * 3 + [now(pair)] * 3,
        out_specs=[now(pair), lag(pair), lag(pair)],
        out_shape=[jax.ShapeDtypeStruct((t, ATT), BF16)] * 3,
        scratch_shapes=[pltpu.VMEM((TILE, 128), F32), pltpu.VMEM((TILE, 128), BF16), pltpu.VMEM((2 * TILE, 128), BF16),
                        pltpu.VMEM((2 * TILE, 128), BF16), pltpu.VMEM((TILE, 128), BF16), pltpu.VMEM((TILE, 128), F32),
                        pltpu.VMEM((TILE, 128), F32), pltpu.VMEM((TILE, 128), F32), pltpu.VMEM((2 * TILE, 128), F32),
                        pltpu.VMEM((2 * TILE, 128), F32), pltpu.VMEM((2 * TILE // BLK, 128, BLK), BF16)],
    )(proj, proj, proj, *tables, *tables, datt, dsum, lse)


def _merge_fwd(proj, y_conv, y_att, tm):
    t = proj.shape[0]

    def body(gc_ref, ga_ref, yc_ref, ya_ref, o_ref):
        o_ref[...] = (_sig(gc_ref[...]) * yc_ref[...] + _sig(ga_ref[...]) * ya_ref[...]).astype(BF16)

    return pl.pallas_call(body, name="merge_fwd", grid=(t // tm,),
                          in_specs=[_rows(tm, D, C_GC // D), _rows(tm, D, C_GA // D), _rows(tm, D), _rows(tm, D)],
                          out_specs=_rows(tm, D), out_shape=jax.ShapeDtypeStruct((t, D), BF16))(proj, proj, y_conv, y_att)


def _acc_rows(ref, i, val):
    @pl.when(i == 0)
    def _():
        ref[...] = jnp.zeros_like(ref)

    ref[...] += jnp.sum(val, axis=0, keepdims=True)


def _loss_head(x, o, mod, final_g, target, tm):
    t = x.shape[0]

    def body(x_ref, o_ref, mod_ref, fg_ref, tg_ref, dout_ref, do_ref, sq_ref, gfg_ref, dgate_ref):
        i = pl.program_id(0)
        gate = mod_ref[:, 2 * D:3 * D]
        ov = o_ref[...]
        out = x_ref[...] + gate * ov
        r = lax.rsqrt(jnp.mean(out * out, axis=-1, keepdims=True) + EPS)
        yn = out * r
        diff = yn * fg_ref[...] - tg_ref[...]
        dy = diff * (1.0 / D)
        gy = dy * fg_ref[...]
        dout = r * (gy - yn * jnp.mean(gy * yn, axis=-1, keepdims=True))
        dout_ref[...] = dout
        do_ref[...] = (dout * gate).astype(BF16)
        _acc_rows(sq_ref, i, diff * diff)
        _acc_rows(gfg_ref, i, dy * yn)
        _acc_rows(dgate_ref, i, dout * ov)

    vec = _full((1, D))
    return pl.pallas_call(
        body, name="loss_head", grid=(t // tm,),
        in_specs=[_rows(tm, D), _rows(tm, D), _full((1, 3 * D)), vec, _rows(tm, D)],
        out_specs=[_rows(tm, D), _rows(tm, D), vec, vec, vec],
        out_shape=[jax.ShapeDtypeStruct((t, D), F32), jax.ShapeDtypeStruct((t, D), BF16)] + [jax.ShapeDtypeStruct((1, D), F32)] * 3,
    )(x, o, mod, final_g, target)


def _merge_bwd(dmerged, proj, y_conv, y_att, tm):
    t = proj.shape[0]

    def body(dm_ref, gc_ref, ga_ref, yc_ref, ya_ref, dyc_ref, dya_ref, dp_ref):
        dm = dm_ref[...]
        sc, sa = _sig(gc_ref[...]), _sig(ga_ref[...])
        dyc_ref[...] = (dm * sc).astype(BF16)
        dya_ref[...] = (dm * sa).astype(BF16)
        dp_ref[:, 0:D] = (dm * yc_ref[...] * sc * (1.0 - sc)).astype(BF16)
        dp_ref[:, D:2 * D] = (dm * ya_ref[...] * sa * (1.0 - sa)).astype(BF16)

    return pl.pallas_call(
        body, name="merge_bwd", grid=(t // tm,),
        in_specs=[_rows(tm, D), _rows(tm, D, C_GC // D), _rows(tm, D, C_GA // D), _rows(tm, D), _rows(tm, D)],
        out_specs=[_rows(tm, D), _rows(tm, D), _rows(tm, 2 * D, C_GC // (2 * D))],
        out_shape=[jax.ShapeDtypeStruct((t, D), BF16), jax.ShapeDtypeStruct((t, D), BF16), jax.ShapeDtypeStruct((t, N_COL), BF16)],
    )(dmerged, proj, proj, y_conv, y_att)


def _att_pre_bwd(dpa, proj, att, dproj, tm):
    t = proj.shape[0]

    def body(dpa_ref, z_ref, att_ref, dp_in, datt_ref, ds_ref, dp_ref):
        del dp_in
        z, dpa_v, att_v = z_ref[...], dpa_ref[...], att_ref[...]
        s = _sig(z)
        datt = dpa_v * (z * s)
        datt_ref[...] = datt
        dp_ref[...] = (dpa_v * att_v * _dsilu(z, s)).astype(BF16)
        prod = datt * att_v
        for h in range(ATT // HEAD):
            sl = slice(h * HEAD, (h + 1) * HEAD)
            ds_ref[:, sl] = jnp.broadcast_to(jnp.sum(prod[:, sl], axis=-1, keepdims=True), (tm, HEAD))

    return pl.pallas_call(
        body, name="att_pre_bwd", grid=(t // tm,),
        in_specs=[_rows(tm, ATT), _rows(tm, ATT, C_ZA // ATT), _rows(tm, ATT), ANY],
        out_specs=[_rows(tm, ATT), _rows(tm, ATT), _rows(tm, ATT, C_ZA // ATT)],
        out_shape=[jax.ShapeDtypeStruct((t, ATT), F32), jax.ShapeDtypeStruct((t, ATT), F32), jax.ShapeDtypeStruct((t, N_COL), BF16)],
        input_output_aliases={3: 2},
    )(dpa, proj, att, dproj)


def _place_qkv(parts, dproj, col_block, tm, name):
    t = dproj.shape[0]

    def body(p0, p1, p2, dp_in, dp_ref):
        del dp_in
        for g, ref in enumerate((p0, p1, p2)):
            dp_ref[:, g * ATT:(g + 1) * ATT] = ref[...]

    return pl.pallas_call(
        body, name=name, grid=(t // tm,),
        in_specs=[_rows(tm, ATT)] * 3 + [ANY],
        out_specs=_rows(tm, QKV, col_block), out_shape=jax.ShapeDtypeStruct((t, N_COL), BF16),
        input_output_aliases={3: 0},
    )(*parts, dproj)


def _conv_bwd_rows(dpc, proj, u1, ln_g, ln_b, dproj, tm):
    t = proj.shape[0]

    def body(dpc_ref, z_ref, u1_ref, lg_ref, lb_ref, dp_in, du1_ref, dp_ref, dlg_ref, dlb_ref, dcb_ref):
        del dp_in
        i = pl.program_id(0)
        u1v = u1_ref[...]
        mu = jnp.mean(u1v, axis=-1, keepdims=True)
        xc = u1v - mu
        r = lax.rsqrt(jnp.mean(xc * xc, axis=-1, keepdims=True) + EPS)
        uhat = xc * r
        u2 = uhat * lg_ref[...] + lb_ref[...]
        s2 = _sig(u2)
        z = z_ref[...]
        sz = _sig(z)
        dpc_v = dpc_ref[...]
        dp_ref[...] = (dpc_v * (u2 * s2) * _dsilu(z, sz)).astype(BF16)
        du2 = dpc_v * (z * sz) * _dsilu(u2, s2)
        duhat = du2 * lg_ref[...]
        du1 = r * (duhat - jnp.mean(duhat, axis=-1, keepdims=True) - uhat * jnp.mean(duhat * uhat, axis=-1, keepdims=True))
        du1_ref[...] = du1
        _acc_rows(dlg_ref, i, du2 * uhat)
        _acc_rows(dlb_ref, i, du2)
        _acc_rows(dcb_ref, i, du1)

    vec = _full((1, D))
    return pl.pallas_call(
        body, name="conv_bwd_rows", grid=(t // tm,),
        in_specs=[_rows(tm, D), _rows(tm, D, C_ZC // D), _rows(tm, D), vec, vec, ANY],
        out_specs=[_rows(tm, D), _rows(tm, D, C_ZC // D), vec, vec, vec],
        out_shape=[jax.ShapeDtypeStruct((t, D), F32), jax.ShapeDtypeStruct((t, N_COL), BF16)] + [jax.ShapeDtypeStruct((1, D), F32)] * 3,
        input_output_aliases={5: 1},
    )(dpc, proj, u1, ln_g, ln_b, dproj)


def _conv_bwd_taps(du1, proj, conv_w, dproj, tm):
    t = proj.shape[0]
    hb = tm // HALO
    last = t // HALO - 1

    def body(du_ref, duh_ref, a_ref, b_ref, ah_ref, bh_ref, w_ref, dp_in, dp_ref, dw_ref, dbuf, ubuf, g0):
        del dp_in
        i = pl.program_id(0)
        a, sb = a_ref[...], _sig(b_ref[...])
        ubuf[0:HALO, :] = jnp.where(i > 0, ah_ref[...] * _sig(bh_ref[...]), 0.0)
        ubuf[HALO:HALO + tm, :] = a * sb
        dbuf[0:tm, :] = du_ref[...]
        dbuf[tm:tm + HALO, :] = jnp.where(i < pl.num_programs(0) - 1, duh_ref[...], 0.0)

        @pl.when(i == 0)
        def _():
            dw_ref[...] = jnp.zeros_like(dw_ref)

        def col(ci, carry):
            c0 = pl.multiple_of(ci * 128, 128)
            for rc in range(tm // 64):
                g0[rc * 64:(rc + 1) * 64, pl.ds(c0, 128)] = _conv_taps(
                    jnp.zeros((64, 128), F32), w_ref, dbuf, rc * 64, c0, lambda j: CONV_K - 1 - j)
            for j in range(CONV_K):
                part = jnp.zeros((8, 128), F32)
                for rc in range(tm // 64):
                    off = rc * 64 + HALO - (CONV_K - 1) + j
                    prod = dbuf[rc * 64:(rc + 1) * 64, pl.ds(c0, 128)] * ubuf[off:off + 64, pl.ds(c0, 128)]
                    part = part + jnp.sum(prod.reshape(8, 8, 128), axis=0)
                dw_ref[j:j + 1, pl.ds(c0, 128)] += jnp.sum(part, axis=0, keepdims=True)
            return carry

        lax.fori_loop(0, D // 128, col, 0)
        du0 = g0[...]
        dp_ref[:, 0:D] = (du0 * sb).astype(BF16)
        dp_ref[:, D:2 * D] = (du0 * a * sb * (1.0 - sb)).astype(BF16)

    prev = lambda col: pl.BlockSpec((HALO, D), lambda i: (jnp.maximum(i * hb - 1, 0), col))
    nxt = pl.BlockSpec((HALO, D), lambda i: (jnp.minimum((i + 1) * hb, last), 0))
    return pl.pallas_call(
        body, name="conv_bwd_taps", grid=(t // tm,),
        in_specs=[_rows(tm, D), nxt, _rows(tm, D, 0), _rows(tm, D, 1), prev(0), prev(1), _full((CONV_KP, D)), ANY],
        out_specs=[_rows(tm, 2 * D, 0), _full((CONV_KP, D))],
        out_shape=[jax.ShapeDtypeStruct((t, N_COL), BF16), jax.ShapeDtypeStruct((CONV_KP, D), F32)],
        scratch_shapes=[pltpu.VMEM((tm + HALO, D), F32), pltpu.VMEM((HALO + tm, D), F32), pltpu.VMEM((tm, D), F32)],
        input_output_aliases={7: 0},
    )(du1, du1, proj, proj, proj, proj, conv_w, dproj)


def _prenorm_bwd(dh, x, dout, mod, norm_g, tm):
    t = x.shape[0]

    def body(dh_ref, x_ref, dout_ref, mod_ref, g_ref, gx_ref, dshift_ref, dscale_ref, dg_ref):
        i = pl.program_id(0)
        xv, dhv = x_ref[...], dh_ref[...]
        r = lax.rsqrt(jnp.mean(xv * xv, axis=-1, keepdims=True) + EPS)
        xn = xv * r
        one_scale = 1.0 + mod_ref[:, D:2 * D]
        dxn = dhv * (g_ref[...] * one_scale)
        gx_ref[...] = r * (dxn - xn * jnp.mean(dxn * xn, axis=-1, keepdims=True)) + dout_ref[...]
        _acc_rows(dshift_ref, i, dhv)
        _acc_rows(dscale_ref, i, dhv * xn * g_ref[...])
        _acc_rows(dg_ref, i, dhv * xn * one_scale)

    vec = _full((1, D))
    return pl.pallas_call(
        body, name="prenorm_bwd", grid=(t // tm,),
        in_specs=[_rows(tm, D), _rows(tm, D), _rows(tm, D), _full((1, 3 * D)), vec],
        out_specs=[_rows(tm, D), vec, vec, vec],
        out_shape=[jax.ShapeDtypeStruct((t, D), F32)] + [jax.ShapeDtypeStruct((1, D), F32)] * 3,
    )(dh, x, dout, mod, norm_g)


def _sum_devices(gathered):
    w = gathered.shape[-1]

    def body(g_ref, o_ref):
        acc = g_ref[0]
        for j in range(1, N_DEV):
            acc = acc + g_ref[j]
        o_ref[...] = acc

    return pl.pallas_call(body, name="sum_devices", grid=(1,), in_specs=[_full(gathered.shape)], out_specs=_full((1, w)),
                          out_shape=jax.ShapeDtypeStruct((1, w), F32))(gathered)


def _rope_tables(positions):
    half = HEAD // 8
    t = positions.shape[-1]
    inv_freq = ROPE_THETA ** (-(jnp.arange(half, dtype=F32) * 2.0 / (2 * half)))
    ang = positions.reshape(t, 1).astype(F32) * inv_freq
    cos, sin = jnp.cos(ang), jnp.sin(ang)
    zeros = lambda n: jnp.zeros((t, n), F32)
    c64 = jnp.concatenate([cos, cos, jnp.ones((t, HEAD - 2 * half), F32)], axis=1)
    lo64 = jnp.concatenate([-sin, zeros(HEAD - half)], axis=1)
    hi64 = jnp.concatenate([zeros(half), sin, zeros(HEAD - 2 * half)], axis=1)
    return tuple(jnp.tile(a, (1, 2)) for a in (c64, lo64, hi64))


def kernel(x, c, positions, norm_g, w_ada, b_ada, w_in, conv_w, conv_b, conv_ln_g, conv_ln_b, w_conv_out, w_att_out, w_o, final_g, loss_target, m_norm_g, m_w_ada, m_b_ada, m_w_in, m_conv_w, m_conv_b, m_conv_ln_g, m_conv_ln_b, m_w_conv_out, m_w_att_out, m_w_o, m_final_g, v_norm_g, v_w_ada, v_b_ada, v_w_in, v_conv_w, v_conv_b, v_conv_ln_g, v_conv_ln_b, v_w_conv_out, v_w_att_out, v_w_o, v_final_g):
    me = 4 * lax.axis_index("x") + 2 * lax.axis_index("y") + lax.axis_index("c")
    x2, tgt = x[0], loss_target[0]
    t = x2.shape[0]
    te = 512 if t % 512 == 0 else 256
    tcv = 256
    tmm = 1024 if t % 1024 == 0 else 256
    n_ada = w_ada.shape[-1]

    pad_taps = lambda a: jnp.pad(a[0], ((0, CONV_KP - CONV_K), (0, 0)))
    shards = (_cast_bf16(w_in[0], "cast_w_in"), _cast_bf16(w_conv_out[0], "cast_w_conv_out"),
              _cast_bf16(w_att_out[0], "cast_w_att_out"), _cast_bf16(w_o[0], "cast_w_o"), pad_taps(conv_w))
    w_in_f, w_co_f, w_ao_f, w_o_f, conv_w_f = _gather_weights(shards)

    c_all = _allgather_small(c, "gather_c").reshape(N_DEV, D)
    b_ada_l = lax.dynamic_slice(b_ada, (0, me * n_ada), (1, n_ada))
    parts = _allgather_small(_mod_part(c_all, w_ada[0], b_ada_l), "gather_mod")
    mod = lax.dynamic_slice(parts, (0, me, 0), (N_DEV, 1, n_ada)).reshape(1, N_DEV * n_ada)

    h = _prenorm(x2, mod, norm_g, te)
    proj = _matmul(h, w_in_f, tm=tmm, tn=1280, tk=D, name="proj")
    u1, pc = _conv_fwd(proj, conv_w_f, conv_b, conv_ln_g, conv_ln_b, tcv)
    tables = _rope_tables(positions)
    parts_att = []
    for gi, dil in GROUPS:
        parts_att += _att_fwd(proj, tables, gi, dil)
    att, lse, pa = _att_combine(parts_att, proj, te)
    y_conv = _matmul(pc, w_co_f, tm=tmm, tn=D, tk=D, name="y_conv")
    y_att = _matmul(pa, w_ao_f, tm=tmm, tn=D, tk=ATT, name="y_att")
    merged = _merge_fwd(proj, y_conv, y_att, te)
    o = _matmul(merged, w_o_f, tm=tmm, tn=D, tk=D, name="out_proj")
    dout, do, sq_sum, g_final, d_gate = _loss_head(x2, o, mod, final_g.reshape(1, D), tgt, te)

    dmerged = _matmul(do, w_o_f, tb=True, tm=tmm, tn=D, tk=D, name="d_merged")
    dw_o = _matmul(merged, do, ta=True, out_dtype=BF16, tm=D, tn=D, tk=512, name="dw_o")
    dyc, dya, dproj = _merge_bwd(dmerged, proj, y_conv, y_att, te)
    dpc = _matmul(dyc, w_co_f, tb=True, tm=tmm, tn=D, tk=D, name="d_pc")
    dw_co = _matmul(pc, dyc, ta=True, out_dtype=BF16, tm=D, tn=D, tk=512, name="dw_conv_out")
    dpa = _matmul(dya, w_ao_f, tb=True, tm=tmm, tn=ATT, tk=D, name="d_pa")
    dw_ao = _matmul(pa, dya, ta=True, out_dtype=BF16, tm=ATT, tn=D, tk=512, name="dw_att_out")
    datt, dsum, dproj = _att_pre_bwd(dpa, proj, att, dproj, te)
    dqs, dks, dvs = [], [], []
    for gi, dil in GROUPS:
        dq, dk, dv = _att_bwd(proj, tables, datt, dsum, lse, gi, dil)
        dqs.append(dq), dks.append(dk), dvs.append(dv)
    dproj = _place_qkv(dqs, dproj, C_Q // QKV, te, "place_dq")
    dproj = _place_qkv(dks, dproj, C_K // QKV, te, "place_dk")
    dproj = _place_qkv(dvs, dproj, C_V // QKV, te, "place_dv")
    du1, dproj, d_ln_g, d_ln_b, d_conv_b = _conv_bwd_rows(dpc, proj, u1, conv_ln_g, conv_ln_b, dproj, te)
    dproj, dconv_w = _conv_bwd_taps(du1, proj, conv_w_f, dproj, tcv)
    dh = _matmul(dproj, w_in_f, tb=True, tm=tmm, tn=D, tk=1024, name="d_h")
    dw_in = _matmul(h, dproj, ta=True, out_dtype=BF16, tm=D, tn=2048, tk=512, name="dw_in")
    grad_x, d_shift, d_scale, d_norm_g = _prenorm_bwd(dh, x2, dout, mod, norm_g, te)

    packed = jnp.concatenate([d_shift, d_scale, d_gate, d_norm_g, d_conv_b, d_ln_g, d_ln_b, g_final, sq_sum], axis=1)
    gathered = _allgather_small(packed, "gather_partials")
    total = _sum_devices(gathered)
    seg = lambda k, n=1: total[:, k * D:(k + n) * D]
    g_b_ada, g_norm_g, g_conv_b, g_ln_g, g_ln_b, g_final_g = seg(0, 3), seg(3), seg(4), seg(5), seg(6), seg(7)
    loss = (0.5 / D) * jnp.sum(seg(8))
    dmod_all = gathered[:, 0, 0:3 * D]
    dmod_cols = lax.dynamic_slice(dmod_all, (0, me * n_ada), (N_DEV, n_ada))
    g_w_ada, d_w_ada, nm_w_ada, nv_w_ada = _w_ada_update(c_all.T, dmod_cols, w_ada[0], m_w_ada[0], v_w_ada[0])

    small = {}
    for name, g, w, m, v in (("norm_g", g_norm_g, norm_g, m_norm_g, v_norm_g), ("b_ada", g_b_ada, b_ada, m_b_ada, v_b_ada),
                             ("conv_b", g_conv_b, conv_b, m_conv_b, v_conv_b), ("conv_ln_g", g_ln_g, conv_ln_g, m_conv_ln_g, v_conv_ln_g),
                             ("conv_ln_b", g_ln_b, conv_ln_b, m_conv_ln_b, v_conv_ln_b),
                             ("final_g", g_final_g, final_g.reshape(1, D), m_final_g.reshape(1, D), v_final_g.reshape(1, D))):
        small[name] = (g,) + tuple(_adamw_small(g, w, m, v, "adamw_" + name))

    slots = _scatter_grads((dw_in, dw_co, dw_ao, dw_o, dconv_w))
    big = {
        "w_in": _sum_adamw(slots[0], w_in[0], m_w_in[0], v_w_in[0], 256, "adamw_w_in"),
        "w_conv_out": _sum_adamw(slots[1], w_conv_out[0], m_w_conv_out[0], v_w_conv_out[0], 128, "adamw_w_conv_out"),
        "w_att_out": _sum_adamw(slots[2], w_att_out[0], m_w_att_out[0], v_w_att_out[0], 512, "adamw_w_att_out"),
        "w_o": _sum_adamw(slots[3], w_o[0], m_w_o[0], v_w_o[0], 128, "adamw_w_o"),
        "conv_w": [r[:CONV_K] for r in _sum_adamw(slots[4], pad_taps(conv_w), pad_taps(m_conv_w), pad_taps(v_conv_w), CONV_KP, "adamw_conv_w")],
    }
    big["w_ada"] = (g_w_ada, d_w_ada, nm_w_ada, nv_w_ada)

    order = ("norm_g", "w_ada", "b_ada", "w_in", "conv_w", "conv_b", "conv_ln_g", "conv_ln_b", "w_conv_out", "w_att_out", "w_o", "final_g")
    lead = lambda name, a: a.reshape(D) if name == "final_g" else (a[None] if name in big else a)
    result = {**small, **big}
    outs = [loss, grad_x[None]]
    for field in range(4):
        outs += [lead(name, result[name][field]) for name in order]
    return tuple(outs)
```

```python
import functools

import jax
import jax.numpy as jnp
from jax import lax
from jax.experimental import pallas as pl
from jax.experimental.pallas import tpu as pltpu

F32 = jnp.float32
BF16 = jnp.bfloat16

N_DEV = 8
D = 1024
N_COL = 10240
C_A, C_B, C_ZC, C_Q, C_K, C_V, C_ZA, C_GC, C_GA = 0, 1024, 2048, 3072, 4608, 6144, 7680, 8192, 9216
QKV = 1536
ATT = 512
HEAD = 64
BLK = 128
TILE = 2048
GROUPS = ((0, 1), (1, 4), (2, 16))
CONV_K = 31
CONV_KP = 32
HALO = 32
EPS = 1e-6
NEG_INF = -1e30
ROPE_THETA = 500000.0
SM_SCALE = HEAD ** -0.5

ADAM_LR, ADAM_B1, ADAM_B2, ADAM_EPS, ADAM_WD, ADAM_STEP = 0.001, 0.9, 0.999, 1e-08, 0.01, 10

MESH = pl.DeviceIdType.MESH
ANY = pl.BlockSpec(memory_space=pl.ANY)


def _sig(v):
    return 1.0 / (1.0 + jnp.exp(-v))


def _dsilu(v, s):
    return s * (1.0 + v * (1.0 - s))


def _full(shape):
    return pl.BlockSpec(shape, lambda *_: (0,) * len(shape))


def _rows(tm, width, col=0):
    return pl.BlockSpec((tm, width), lambda i: (i, col))


def _matmul(a, b, *, ta=False, tb=False, out_dtype=F32, tm, tn, tk, name):
    m, k = (a.shape[1], a.shape[0]) if ta else a.shape
    n = b.shape[0] if tb else b.shape[1]
    assert (b.shape[1] if tb else b.shape[0]) == k
    assert m % tm == 0 and n % tn == 0 and k % tk == 0
    nk = k // tk
    dims = (((0 if ta else 1,), (1 if tb else 0,)), ((), ()))
    use_scratch = out_dtype != F32 and nk > 1

    def body(a_ref, b_ref, o_ref, *scratch):
        p = lax.dot_general(a_ref[...], b_ref[...], dims, preferred_element_type=F32)
        if nk == 1:
            o_ref[...] = p.astype(out_dtype)
            return
        acc = scratch[0] if use_scratch else o_ref
        kk = pl.program_id(2)

        @pl.when(kk == 0)
        def _():
            acc[...] = p

        @pl.when(kk > 0)
        def _():
            acc[...] += p

        if use_scratch:
            @pl.when(kk == nk - 1)
            def _():
                o_ref[...] = acc[...].astype(out_dtype)

    a_spec = pl.BlockSpec((tk, tm), lambda i, j, kk: (kk, i)) if ta else pl.BlockSpec((tm, tk), lambda i, j, kk: (i, kk))
    b_spec = pl.BlockSpec((tn, tk), lambda i, j, kk: (j, kk)) if tb else pl.BlockSpec((tk, tn), lambda i, j, kk: (kk, j))
    return pl.pallas_call(
        body, name=name, grid=(m // tm, n // tn, nk),
        in_specs=[a_spec, b_spec],
        out_specs=pl.BlockSpec((tm, tn), lambda i, j, kk: (i, j)),
        out_shape=jax.ShapeDtypeStruct((m, n), out_dtype),
        scratch_shapes=[pltpu.VMEM((tm, tn), F32)] if use_scratch else [],
    )(a, b)


def _me_and_peers():
    x, y, c = lax.axis_index("x"), lax.axis_index("y"), lax.axis_index("c")
    me = 4 * x + 2 * y + c
    peers = []
    for k in range(1, N_DEV):
        px, py, pc = x ^ (k >> 2), y ^ ((k >> 1) & 1), c ^ (k & 1)
        peers.append(((px, py, pc), 4 * px + 2 * py + pc))
    return me, peers


def _allgather_small(v, name):
    r, c = v.shape

    def body(v_ref, out_ref, send_sems, recv_sems):
        me, peers = _me_and_peers()
        out_ref[me] = v_ref[...]
        copies = []
        for k, (dev, _) in enumerate(peers):
            cp = pltpu.make_async_remote_copy(src_ref=v_ref, dst_ref=out_ref.at[me], send_sem=send_sems.at[k],
                                              recv_sem=recv_sems.at[k], device_id=dev, device_id_type=MESH)
            cp.start()
            copies.append(cp)
        for k, (dev, idx) in enumerate(peers):
            pltpu.make_async_remote_copy(src_ref=v_ref, dst_ref=out_ref.at[idx], send_sem=send_sems.at[k],
                                         recv_sem=recv_sems.at[k], device_id=dev, device_id_type=MESH).wait_recv()
        for cp in copies:
            cp.wait_send()

    return pl.pallas_call(
        body, name=name,
        in_specs=[pl.BlockSpec(memory_space=pltpu.VMEM)],
        out_specs=pl.BlockSpec(memory_space=pltpu.VMEM),
        out_shape=jax.ShapeDtypeStruct((N_DEV, r, c), v.dtype),
        scratch_shapes=[pltpu.SemaphoreType.DMA((N_DEV - 1,)), pltpu.SemaphoreType.DMA((N_DEV - 1,))],
    )(v)


def _window(ref, kind, idx, size):
    start = pl.multiple_of(idx * size, size)
    if kind == "rows":
        return ref.at[pl.ds(start, size), :]
    return ref.at[:, pl.ds(start, size)]


_BIG = (("cols", N_COL // N_DEV), ("rows", D // N_DEV), ("cols", D // N_DEV), ("rows", D // N_DEV), ("cols", D // N_DEV))


GATHER_ORDER = (0, 1, 2, 4, 3, 5, 6, 7)
SCATTER_ORDER = (6, 4, 2, 7, 5, 3, 1, 0)


def _proj_gather(h, shards, order, tm):
    t = h.shape[0]
    nt = len(shards)
    n_blk = N_COL // N_DEV
    full_shapes = []
    for s, (kind, size) in zip(shards, _BIG):
        full_shapes.append(jax.ShapeDtypeStruct((s.shape[0] * N_DEV, s.shape[1]) if kind == "rows"
                                                else (s.shape[0], s.shape[1] * N_DEV), s.dtype))
    last = (N_DEV - 1, t // tm - 1)

    def body(order_ref, h_ref, *refs):
        src, proj_ref, dst = refs[:nt], refs[nt], refs[nt + 1:2 * nt + 1]
        wbuf, send_sems, recv_sems, local_sems, load_sem = refs[2 * nt + 1:]
        j, i = pl.program_id(0), pl.program_id(1)
        me, peers = _me_and_peers()

        def local(tn):
            kind, size = _BIG[tn]
            return pltpu.make_async_copy(src[tn], _window(dst[tn], kind, me, size), local_sems.at[tn])

        def remote(tn, k, block_of):
            kind, size = _BIG[tn]
            dev, idx = peers[k - 1]
            return pltpu.make_async_remote_copy(src_ref=src[tn], dst_ref=_window(dst[tn], kind, me if block_of == "mine" else idx, size),
                                                send_sem=send_sems.at[tn, k - 1], recv_sem=recv_sems.at[tn, k - 1],
                                                device_id=dev, device_id_type=MESH)

        @pl.when((j == 0) & (i == 0))
        def _():
            for tn in range(nt):
                local(tn).start()
                for k in GATHER_ORDER[1:]:
                    remote(tn, k, "mine").start()

        @pl.when(i == 0)
        def _():
            for step, k in enumerate(GATHER_ORDER):
                @pl.when(j == step)
                def _():
                    if k == 0:
                        local(0).wait()
                    else:
                        remote(0, k, "theirs").wait_recv()
            blk = pltpu.make_async_copy(_window(dst[0], "cols", order_ref[j], n_blk), wbuf, load_sem)
            blk.start()
            blk.wait()

        proj_ref[...] = jnp.dot(h_ref[...], wbuf[...], preferred_element_type=F32)

        @pl.when((j == last[0]) & (i == last[1]))
        def _():
            for tn in range(1, nt):
                local(tn).wait()
                for k in range(1, N_DEV):
                    remote(tn, k, "theirs").wait_recv()
            for tn in range(nt):
                for k in range(1, N_DEV):
                    remote(tn, k, "mine").wait_send()

    grid_spec = pltpu.PrefetchScalarGridSpec(
        num_scalar_prefetch=1, grid=(N_DEV, t // tm),
        in_specs=[pl.BlockSpec((tm, D), lambda j, i, order_ref: (i, 0))] + [ANY] * nt,
        out_specs=[pl.BlockSpec((tm, n_blk), lambda j, i, order_ref: (i, order_ref[j]))] + [ANY] * nt,
        scratch_shapes=[pltpu.VMEM((D, n_blk), BF16), pltpu.SemaphoreType.DMA((nt, N_DEV - 1)),
                        pltpu.SemaphoreType.DMA((nt, N_DEV - 1)), pltpu.SemaphoreType.DMA((nt,)), pltpu.SemaphoreType.DMA(())],
    )
    return pl.pallas_call(
        body, name="proj_gather", grid_spec=grid_spec,
        out_shape=[jax.ShapeDtypeStruct((t, N_COL), F32)] + full_shapes,
    )(order, h, *shards)


def _dw_in_scatter(h, dproj, small_grads, order, tk):
    t = h.shape[0]
    nt = 1 + len(small_grads)
    n_blk = N_COL // N_DEV
    nk = t // tk
    slot_shapes = [jax.ShapeDtypeStruct((N_DEV, D, n_blk), BF16)]
    for g, (kind, size) in zip(small_grads, _BIG[1:]):
        slot_shapes.append(jax.ShapeDtypeStruct((N_DEV,) + ((size, g.shape[1]) if kind == "rows" else (g.shape[0], size)), g.dtype))

    def body(order_ref, h_ref, dp_ref, *refs):
        src, dst = refs[:nt - 1], refs[nt - 1:2 * nt - 1]
        acc, stage, send_sems, recv_sems, local_sems = refs[2 * nt - 1:]
        j, kk = pl.program_id(0), pl.program_id(1)
        me, peers = _me_and_peers()

        def small_local(tn):
            kind, size = _BIG[tn]
            return pltpu.make_async_copy(_window(src[tn - 1], kind, me, size), dst[tn].at[me], local_sems.at[tn])

        def small_remote(tn, k, mine):
            kind, size = _BIG[tn]
            dev, idx = peers[k - 1]
            return pltpu.make_async_remote_copy(src_ref=_window(src[tn - 1], kind, idx if mine else me, size),
                                                dst_ref=dst[tn].at[me if mine else idx],
                                                send_sem=send_sems.at[tn, k - 1], recv_sem=recv_sems.at[tn, k - 1],
                                                device_id=dev, device_id_type=MESH)

        def big_remote(k, slot, mine):
            dev, idx = peers[k - 1]
            return pltpu.make_async_remote_copy(src_ref=stage.at[slot], dst_ref=dst[0].at[me if mine else idx],
                                                send_sem=send_sems.at[0, k - 1], recv_sem=recv_sems.at[0, k - 1],
                                                device_id=dev, device_id_type=MESH)

        def big_local(slot):
            return pltpu.make_async_copy(stage.at[slot], dst[0].at[me], local_sems.at[0])

        @pl.when((j == 0) & (kk == 0))
        def _():
            for tn in range(1, nt):
                small_local(tn).start()
                for k in SCATTER_ORDER[:-1]:
                    small_remote(tn, k, True).start()

        p = lax.dot_general(h_ref[...], dp_ref[...], (((0,), (0,)), ((), ())), preferred_element_type=F32)

        @pl.when(kk == 0)
        def _():
            acc[...] = p

        @pl.when(kk > 0)
        def _():
            acc[...] += p

        @pl.when(kk == nk - 1)
        def _():
            for step, k in enumerate(SCATTER_ORDER):
                @pl.when(j == step)
                def _():
                    slot = step % 2
                    if step >= 2:
                        big_remote(SCATTER_ORDER[step - 2], slot, True).wait_send()
                    stage[slot] = acc[...].astype(BF16)
                    if k == 0:
                        big_local(slot).start()
                    else:
                        big_remote(k, slot, True).start()

        @pl.when((j == N_DEV - 1) & (kk == nk - 1))
        def _():
            big_remote(SCATTER_ORDER[N_DEV - 2], (N_DEV - 2) % 2, True).wait_send()
            big_local((N_DEV - 1) % 2).wait()
            for k in range(1, N_DEV):
                big_remote(k, 0, False).wait_recv()
            for tn in range(1, nt):
                small_local(tn).wait()
                for k in range(1, N_DEV):
                    small_remote(tn, k, False).wait_recv()
                    small_remote(tn, k, True).wait_send()

    grid_spec = pltpu.PrefetchScalarGridSpec(
        num_scalar_prefetch=1, grid=(N_DEV, nk),
        in_specs=[pl.BlockSpec((tk, D), lambda j, kk, order_ref: (kk, 0)),
                  pl.BlockSpec((tk, n_blk), lambda j, kk, order_ref: (kk, order_ref[j]))] + [ANY] * (nt - 1),
        out_specs=[ANY] * nt,
        scratch_shapes=[pltpu.VMEM((D, n_blk), F32), pltpu.VMEM((2, D, n_blk), BF16), pltpu.SemaphoreType.DMA((nt, N_DEV - 1)),
                        pltpu.SemaphoreType.DMA((nt, N_DEV - 1)), pltpu.SemaphoreType.DMA((nt,))],
    )
    return pl.pallas_call(body, name="dw_in_scatter", grid_spec=grid_spec, out_shape=slot_shapes)(order, h, dproj, *small_grads)


def _adamw_math(w, g, m, v):
    m = ADAM_B1 * m + (1.0 - ADAM_B1) * g
    v = ADAM_B2 * v + (1.0 - ADAM_B2) * (g * g)
    m_hat = m / (1.0 - ADAM_B1 ** ADAM_STEP)
    v_hat = v / (1.0 - ADAM_B2 ** ADAM_STEP)
    delta = -ADAM_LR * (m_hat / (jnp.sqrt(v_hat) + ADAM_EPS) + ADAM_WD * w)
    return delta, m, v


def _sum_adamw(slots, w, m, v, tr, name):
    _, r, c = slots.shape
    assert r % tr == 0

    def body(s_ref, w_ref, m_ref, v_ref, g_ref, d_ref, nm_ref, nv_ref):
        g = s_ref[0].astype(F32)
        for j in range(1, N_DEV):
            g = g + s_ref[j].astype(F32)
        delta, nm, nv = _adamw_math(w_ref[...], g, m_ref[...], v_ref[...])
        g_ref[...] = g
        d_ref[...] = delta
        nm_ref[...] = nm
        nv_ref[...] = nv

    blk = pl.BlockSpec((tr, c), lambda i: (i, 0))
    return pl.pallas_call(
        body, name=name, grid=(r // tr,),
        in_specs=[pl.BlockSpec((N_DEV, tr, c), lambda i: (0, i, 0)), blk, blk, blk],
        out_specs=[blk] * 4, out_shape=[jax.ShapeDtypeStruct((r, c), F32)] * 4,
    )(slots, w, m, v)


def _adamw_small(g, w, m, v, name):
    def body(g_ref, w_ref, m_ref, v_ref, d_ref, nm_ref, nv_ref):
        delta, nm, nv = _adamw_math(w_ref[...], g_ref[...], m_ref[...], v_ref[...])
        d_ref[...] = delta
        nm_ref[...] = nm
        nv_ref[...] = nv

    spec = _full(g.shape)
    return pl.pallas_call(body, name=name, grid=(1,), in_specs=[spec] * 4, out_specs=[spec] * 3,
                          out_shape=[jax.ShapeDtypeStruct(g.shape, F32)] * 3)(g, w, m, v)


def _mod_part(c_all, w_ada_l, b_ada_l):
    n = w_ada_l.shape[1]

    def body(c_ref, w_ref, b_ref, o_ref):
        o_ref[...] = jnp.dot(c_ref[...], w_ref[...], preferred_element_type=F32,
                             precision=lax.Precision.HIGHEST) + b_ref[...]

    return pl.pallas_call(body, name="mod_part", grid=(1,),
                          in_specs=[_full(c_all.shape), _full(w_ada_l.shape), _full(b_ada_l.shape)],
                          out_specs=_full((N_DEV, n)), out_shape=jax.ShapeDtypeStruct((N_DEV, n), F32))(c_all, w_ada_l, b_ada_l)


def _w_ada_update(c_all_t, dmod_cols, w, m, v):
    def body(c_ref, dm_ref, w_ref, m_ref, v_ref, g_ref, d_ref, nm_ref, nv_ref):
        g = c_ref[:, 0:1] * dm_ref[0:1, :]
        for b in range(1, N_DEV):
            g = g + c_ref[:, b:b + 1] * dm_ref[b:b + 1, :]
        delta, nm, nv = _adamw_math(w_ref[...], g, m_ref[...], v_ref[...])
        g_ref[...] = g
        d_ref[...] = delta
        nm_ref[...] = nm
        nv_ref[...] = nv

    spec = _full(w.shape)
    return pl.pallas_call(body, name="w_ada_update", grid=(1,),
                          in_specs=[_full(c_all_t.shape), _full(dmod_cols.shape), spec, spec, spec],
                          out_specs=[spec] * 4, out_shape=[jax.ShapeDtypeStruct(w.shape, F32)] * 4)(c_all_t, dmod_cols, w, m, v)


def _cast_bf16(w, name):
    def body(w_ref, o_ref):
        o_ref[...] = w_ref[...].astype(BF16)

    return pl.pallas_call(body, name=name, grid=(1,), in_specs=[_full(w.shape)], out_specs=_full(w.shape),
                          out_shape=jax.ShapeDtypeStruct(w.shape, BF16))(w)


def _prenorm(x, mod, norm_g, tm):
    t = x.shape[0]

    def body(x_ref, mod_ref, g_ref, h_ref):
        xv = x_ref[...]
        r = lax.rsqrt(jnp.mean(xv * xv, axis=-1, keepdims=True) + EPS)
        h = (xv * r) * g_ref[...] * (1.0 + mod_ref[:, D:2 * D]) + mod_ref[:, 0:D]
        h_ref[...] = h.astype(BF16)

    return pl.pallas_call(body, name="prenorm", grid=(t // tm,),
                          in_specs=[_rows(tm, D), _full((1, 3 * D)), _full((1, D))],
                          out_specs=_rows(tm, D), out_shape=jax.ShapeDtypeStruct((t, D), BF16))(x, mod, norm_g)


def _rope_apply(t, cos, s_lo, s_hi):
    return t * cos + pltpu.roll(t, 120, 1) * s_lo + pltpu.roll(t, 8, 1) * s_hi


def _conv_taps(acc_init, w_ref, buf, row0, c0, offset_of_tap):
    acc = acc_init
    for j in range(CONV_K):
        acc = acc + w_ref[j:j + 1, pl.ds(c0, 128)] * buf[row0 + offset_of_tap(j): row0 + offset_of_tap(j) + 64, pl.ds(c0, 128)]
    return acc


def _conv_fwd(proj, conv_w, conv_b, ln_g, ln_b, tm):
    t = proj.shape[0]
    hb = tm // HALO

    def body(a_ref, b_ref, z_ref, ah_ref, bh_ref, w_ref, cb_ref, lg_ref, lb_ref, u1_ref, pc_ref, ubuf):
        i = pl.program_id(0)
        u0h = ah_ref[...] * _sig(bh_ref[...])
        ubuf[0:HALO, :] = jnp.where(i > 0, u0h, 0.0)
        ubuf[HALO:HALO + tm, :] = a_ref[...] * _sig(b_ref[...])

        def col(ci, carry):
            c0 = pl.multiple_of(ci * 128, 128)
            for rc in range(tm // 64):
                init = jnp.zeros((64, 128), F32)
                acc = _conv_taps(init, w_ref, ubuf, rc * 64, c0, lambda j: HALO - (CONV_K - 1) + j)
                u1_ref[rc * 64:(rc + 1) * 64, pl.ds(c0, 128)] = acc + cb_ref[:, pl.ds(c0, 128)]
            return carry

        lax.fori_loop(0, D // 128, col, 0)
        u1 = u1_ref[...]
        mu = jnp.mean(u1, axis=-1, keepdims=True)
        xc = u1 - mu
        var = jnp.mean(xc * xc, axis=-1, keepdims=True)
        u2 = xc * lax.rsqrt(var + EPS) * lg_ref[...] + lb_ref[...]
        z = z_ref[...]
        pc_ref[...] = (u2 * _sig(u2) * (z * _sig(z))).astype(BF16)

    halo = pl.BlockSpec((HALO, D), lambda i: (jnp.maximum(i * hb - 1, 0), 0))
    halo_b = pl.BlockSpec((HALO, D), lambda i: (jnp.maximum(i * hb - 1, 0), 1))
    return pl.pallas_call(
        body, name="conv_fwd", grid=(t // tm,),
        in_specs=[_rows(tm, D, 0), _rows(tm, D, 1), _rows(tm, D, 2), halo, halo_b,
                  _full((CONV_KP, D)), _full((1, D)), _full((1, D)), _full((1, D))],
        out_specs=[_rows(tm, D), _rows(tm, D)],
        out_shape=[jax.ShapeDtypeStruct((t, D), F32), jax.ShapeDtypeStruct((t, D), BF16)],
        scratch_shapes=[pltpu.VMEM((HALO + tm, D), F32)],
    )(proj, proj, proj, proj, proj, conv_w, conv_b, ln_g, ln_b)


def _band_masks_t(has_prev):
    key = lax.broadcasted_iota(jnp.int32, (BLK, BLK), 0)
    qry = lax.broadcasted_iota(jnp.int32, (BLK, BLK), 1)
    return jnp.logical_and(key >= qry, has_prev), key <= qry


def _head_lanes(pair, hh):
    lane = lax.broadcasted_iota(jnp.int32, pair.shape, 1)
    return jnp.where((lane >= hh * HEAD) & (lane < (hh + 1) * HEAD), pair, jnp.zeros_like(pair))


def _pair_mask(has_prev):
    mask_p, mask_c = _band_masks_t(has_prev)
    both = jnp.concatenate([mask_p, mask_c], axis=0)
    return jnp.concatenate([both, both], axis=1)


def _query_pair(pair):
    return jnp.concatenate([_head_lanes(pair, 0), _head_lanes(pair, 1)], axis=0)


def _key_pair(ref, prev, cur):
    return jnp.concatenate([ref[pl.ds(prev, BLK), :], ref[pl.ds(cur, BLK), :]], axis=0)


def _own_head(both):
    return jnp.concatenate([both[0:HEAD, 0:BLK], both[HEAD:2 * HEAD, BLK:2 * BLK]], axis=0)


def _store_transposed(dst, base, src):
    for j in range(TILE // BLK):
        dst[base // BLK + j] = src[j * BLK:(j + 1) * BLK, :].T.astype(BF16)


class _Dilated:
    def __init__(self, dil):
        self.dil = dil
        self.per = TILE // dil
        self.nbr = self.per // BLK

    def spread(self, dst, base, src_ref, dtype):
        for r in range(self.dil):
            rows = src_ref[pl.ds(r, self.per, stride=self.dil), :] if self.dil > 1 else src_ref[...]
            dst[pl.ds(pl.multiple_of(base + r * self.per, BLK), self.per), :] = rows.astype(dtype)

    def gather(self, dst_ref, src, base):
        for r in range(self.dil):
            rows = src[pl.ds(pl.multiple_of(base + r * self.per, BLK), self.per), :]
            if self.dil > 1:
                dst_ref[pl.ds(r, self.per, stride=self.dil), :] = rows
            else:
                dst_ref[...] = rows

    def block_rows(self, b, i, cur, prv):
        n = b % self.nbr
        row = pl.multiple_of(b * BLK, BLK)
        has_prev = jnp.logical_or(n > 0, i > 0)
        prev = jnp.where(n > 0, cur + row - BLK, jnp.where(i > 0, prv + row + (self.nbr - 1) * BLK, cur + row))
        return row, pl.multiple_of(prev, BLK), has_prev


def _slots(i):
    return pl.multiple_of((i % 2) * TILE, TILE), pl.multiple_of(((i + 1) % 2) * TILE, TILE)


def _nt(a, b):
    return lax.dot_general(a, b, (((1,), (1,)), ((), ())), preferred_element_type=F32)


def _qkv_specs(gi, clamp_to=None):
    def spec(col0):
        def imap(hp, i):
            return (i if clamp_to is None else jnp.minimum(i, clamp_to), (col0 + gi * ATT) // 128 + hp)
        return pl.BlockSpec((TILE, 128), imap)
    return [spec(C_Q), spec(C_K), spec(C_V)]


def _att_fwd(proj, tables, gi, dil):
    t = proj.shape[0]
    dl = _Dilated(dil)

    def body(q_ref, k_ref, v_ref, c_ref, lo_ref, hi_ref, o_ref, lse_ref, tmp, qd, kd, vt, od, ld):
        i = pl.program_id(1)
        cur, prv = _slots(i)
        cs, lo, hi = c_ref[...], lo_ref[...], hi_ref[...]
        tmp[...] = _rope_apply(q_ref[...], cs, lo, hi) * SM_SCALE
        dl.spread(qd, 0, tmp, BF16)
        tmp[...] = _rope_apply(k_ref[...], cs, lo, hi)
        dl.spread(kd, cur, tmp, BF16)
        dl.spread(tmp, 0, v_ref, F32)
        _store_transposed(vt, cur, tmp)

        def block(b, carry):
            row, prev, has_prev = dl.block_rows(b, i, cur, prv)
            s = jnp.where(_pair_mask(has_prev), _nt(_key_pair(kd, prev, cur + row), _query_pair(qd[pl.ds(row, BLK), :])), NEG_INF)
            mx = jnp.max(s, axis=0, keepdims=True)
            p = jnp.exp(s - mx)
            den = jnp.sum(p, axis=0, keepdims=True)
            v_t = jnp.concatenate([vt[prev // BLK], vt[(cur + row) // BLK]], axis=1)
            acc = jnp.dot(v_t, p.astype(BF16), preferred_element_type=F32) / den
            lse = mx + jnp.log(den)
            od[pl.ds(row, BLK), :] = _own_head(acc).T
            ld[pl.ds(row, BLK), :] = _own_head(jnp.broadcast_to(lse, (2 * HEAD, 2 * BLK))).T
            return carry

        lax.fori_loop(0, TILE // BLK, block, 0, unroll=4)
        dl.gather(o_ref, od, 0)
        dl.gather(lse_ref, ld, 0)

    tab = pl.BlockSpec((TILE, 128), lambda hp, i: (i, 0))
    out_spec = pl.BlockSpec((TILE, 128), lambda hp, i: (i, hp))
    return pl.pallas_call(
        body, name=f"att_fwd_g{gi}", grid=(ATT // 128, t // TILE),
        in_specs=_qkv_specs(gi) + [tab] * 3,
        out_specs=[out_spec] * 2, out_shape=[jax.ShapeDtypeStruct((t, ATT), F32)] * 2,
        scratch_shapes=[pltpu.VMEM((TILE, 128), F32), pltpu.VMEM((TILE, 128), BF16), pltpu.VMEM((2 * TILE, 128), BF16),
                        pltpu.VMEM((2 * TILE // BLK, 128, BLK), BF16), pltpu.VMEM((TILE, 128), F32), pltpu.VMEM((TILE, 128), F32)],
    )(proj, proj, proj, *tables)


def _att_combine(parts, proj, tm):
    t = proj.shape[0]

    def body(o0, l0, o1, l1, o2, l2, z_ref, att_ref, lse_ref, pa_ref):
        m_all = jnp.maximum(jnp.maximum(l0[...], l1[...]), l2[...])
        w0, w1, w2 = jnp.exp(l0[...] - m_all), jnp.exp(l1[...] - m_all), jnp.exp(l2[...] - m_all)
        den = w0 + w1 + w2
        att = (w0 * o0[...] + w1 * o1[...] + w2 * o2[...]) / den
        z = z_ref[...]
        att_ref[...] = att
        lse_ref[...] = m_all + jnp.log(den)
        pa_ref[...] = (att * (z * _sig(z))).astype(BF16)

    spec = _rows(tm, ATT)
    return pl.pallas_call(
        body, name="att_combine", grid=(t // tm,),
        in_specs=[spec] * 6 + [_rows(tm, ATT, C_ZA // ATT)],
        out_specs=[spec] * 3,
        out_shape=[jax.ShapeDtypeStruct((t, ATT), F32)] * 2 + [jax.ShapeDtypeStruct((t, ATT), BF16)],
    )(*parts, proj)


def _att_bwd(proj, tables, datt, dsum, lse, gi, dil):
    t = proj.shape[0]
    nt = t // TILE
    dl = _Dilated(dil)

    def body(q_ref, k_ref, v_ref, c_ref, lo_ref, hi_ref, cl_ref, lol_ref, hil_ref, do_ref, ds_ref, lse_ref,
             dq_ref, dk_ref, dv_ref, tmp, qd, kd, vd, dod, dsd, lsd, dqd, dkd, dvd, kt):
        i = pl.program_id(1)
        cur, prv = _slots(i)

        @pl.when(i < nt)
        def _():
            cs, lo, hi = c_ref[...], lo_ref[...], hi_ref[...]
            tmp[...] = _rope_apply(q_ref[...], cs, lo, hi) * SM_SCALE
            dl.spread(qd, 0, tmp, BF16)
            tmp[...] = _rope_apply(k_ref[...], cs, lo, hi)
            dl.spread(kd, cur, tmp, BF16)
            dl.spread(dqd, 0, tmp, F32)
            _store_transposed(kt, cur, dqd)
            dl.spread(vd, cur, v_ref, BF16)
            dl.spread(dod, 0, do_ref, BF16)
            dl.spread(dsd, 0, ds_ref, F32)
            dl.spread(lsd, 0, lse_ref, F32)
            dkd[pl.ds(cur, TILE), :] = jnp.zeros((TILE, 128), F32)
            dvd[pl.ds(cur, TILE), :] = jnp.zeros((TILE, 128), F32)

            def block(b, carry):
                row, prev, has_prev = dl.block_rows(b, i, cur, prv)
                q_pair, do_pair = _query_pair(qd[pl.ds(row, BLK), :]), _query_pair(dod[pl.ds(row, BLK), :])
                k_pair, v_pair = _key_pair(kd, prev, cur + row), _key_pair(vd, prev, cur + row)
                ds_t, ls_t = dsd[pl.ds(row, BLK), :].T, lsd[pl.ds(row, BLK), :].T
                lse = jnp.concatenate([ls_t[0:1, :], ls_t[HEAD:HEAD + 1, :]], axis=1)
                dsm = jnp.concatenate([ds_t[0:1, :], ds_t[HEAD:HEAD + 1, :]], axis=1)
                p = jnp.exp(jnp.where(_pair_mask(has_prev), _nt(k_pair, q_pair), NEG_INF) - lse)
                ds = (p * (_nt(v_pair, do_pair) - dsm)).astype(BF16)
                k_t = jnp.concatenate([kt[prev // BLK], kt[(cur + row) // BLK]], axis=1)
                dqd[pl.ds(row, BLK), :] = _own_head(jnp.dot(k_t, ds, preferred_element_type=F32)).T * SM_SCALE
                dk = jnp.dot(ds, q_pair, preferred_element_type=F32)
                dv = jnp.dot(p.astype(BF16), do_pair, preferred_element_type=F32)
                dkd[pl.ds(cur + row, BLK), :] += dk[BLK:2 * BLK, :]
                dvd[pl.ds(cur + row, BLK), :] += dv[BLK:2 * BLK, :]
                dkd[pl.ds(prev, BLK), :] += dk[0:BLK, :]
                dvd[pl.ds(prev, BLK), :] += dv[0:BLK, :]
                return carry

            lax.fori_loop(0, TILE // BLK, block, 0, unroll=4)
            dl.gather(tmp, dqd, 0)
            dq_ref[...] = _rope_apply(tmp[...], cs, -lo, -hi).astype(BF16)

        @pl.when(i > 0)
        def _():
            dl.gather(tmp, dkd, prv)
            dk_ref[...] = _rope_apply(tmp[...], cl_ref[...], -lol_ref[...], -hil_ref[...]).astype(BF16)
            dl.gather(tmp, dvd, prv)
            dv_ref[...] = tmp[...].astype(BF16)

    now = lambda col: pl.BlockSpec((TILE, 128), lambda hp, i: (jnp.minimum(i, nt - 1), col(hp)))
    lag = lambda col: pl.BlockSpec((TILE, 128), lambda hp, i: (jnp.maximum(i - 1, 0), col(hp)))
    first, pair = (lambda hp: 0), (lambda hp: hp)
    return pl.pallas_call(
        body, name=f"att_bwd_g{gi}", grid=(ATT // 128, nt + 1),
        in_specs=_qkv_specs(gi, nt - 1) + [now(first)] * 3 + [lag(first)] * 3 + [now(pair)] * 3,
        out_specs=[now(pair), lag(pair), lag(pair)],
        out_shape=[jax.ShapeDtypeStruct((t, ATT), BF16)] * 3,
        scratch_shapes=[pltpu.VMEM((TILE, 128), F32), pltpu.VMEM((TILE, 128), BF16), pltpu.VMEM((2 * TILE, 128), BF16),
                        pltpu.VMEM((2 * TILE, 128), BF16), pltpu.VMEM((TILE, 128), BF16), pltpu.VMEM((TILE, 128), F32),
                        pltpu.VMEM((TILE, 128), F32), pltpu.VMEM((TILE, 128), F32), pltpu.VMEM((2 * TILE, 128), F32),
                        pltpu.VMEM((2 * TILE, 128), F32), pltpu.VMEM((2 * TILE // BLK, 128, BLK), BF16)],
    )(proj, proj, proj, *tables, *tables, datt, dsum, lse)


def _merge_fwd(proj, y_conv, y_att, tm):
    t = proj.shape[0]

    def body(gc_ref, ga_ref, yc_ref, ya_ref, o_ref):
        o_ref[...] = (_sig(gc_ref[...]) * yc_ref[...] + _sig(ga_ref[...]) * ya_ref[...]).astype(BF16)

    return pl.pallas_call(body, name="merge_fwd", grid=(t // tm,),
                          in_specs=[_rows(tm, D, C_GC // D), _rows(tm, D, C_GA // D), _rows(tm, D), _rows(tm, D)],
                          out_specs=_rows(tm, D), out_shape=jax.ShapeDtypeStruct((t, D), BF16))(proj, proj, y_conv, y_att)


def _acc_rows(ref, i, val):
    @pl.when(i == 0)
    def _():
        ref[...] = jnp.zeros_like(ref)

    ref[...] += jnp.sum(val, axis=0, keepdims=True)


def _loss_head(x, o, mod, final_g, target, tm):
    t = x.shape[0]

    def body(x_ref, o_ref, mod_ref, fg_ref, tg_ref, dout_ref, do_ref, sq_ref, gfg_ref, dgate_ref):
        i = pl.program_id(0)
        gate = mod_ref[:, 2 * D:3 * D]
        ov = o_ref[...]
        out = x_ref[...] + gate * ov
        r = lax.rsqrt(jnp.mean(out * out, axis=-1, keepdims=True) + EPS)
        yn = out * r
        diff = yn * fg_ref[...] - tg_ref[...]
        dy = diff * (1.0 / D)
        gy = dy * fg_ref[...]
        dout = r * (gy - yn * jnp.mean(gy * yn, axis=-1, keepdims=True))
        dout_ref[...] = dout
        do_ref[...] = (dout * gate).astype(BF16)
        _acc_rows(sq_ref, i, diff * diff)
        _acc_rows(gfg_ref, i, dy * yn)
        _acc_rows(dgate_ref, i, dout * ov)

    vec = _full((1, D))
    return pl.pallas_call(
        body, name="loss_head", grid=(t // tm,),
        in_specs=[_rows(tm, D), _rows(tm, D), _full((1, 3 * D)), vec, _rows(tm, D)],
        out_specs=[_rows(tm, D), _rows(tm, D), vec, vec, vec],
        out_shape=[jax.ShapeDtypeStruct((t, D), F32), jax.ShapeDtypeStruct((t, D), BF16)] + [jax.ShapeDtypeStruct((1, D), F32)] * 3,
    )(x, o, mod, final_g, target)


def _merge_bwd(dmerged, proj, y_conv, y_att, tm):
    t = proj.shape[0]

    def body(dm_ref, gc_ref, ga_ref, yc_ref, ya_ref, dyc_ref, dya_ref, dp_ref):
        dm = dm_ref[...]
        sc, sa = _sig(gc_ref[...]), _sig(ga_ref[...])
        dyc_ref[...] = (dm * sc).astype(BF16)
        dya_ref[...] = (dm * sa).astype(BF16)
        dp_ref[:, 0:D] = (dm * yc_ref[...] * sc * (1.0 - sc)).astype(BF16)
        dp_ref[:, D:2 * D] = (dm * ya_ref[...] * sa * (1.0 - sa)).astype(BF16)

    return pl.pallas_call(
        body, name="merge_bwd", grid=(t // tm,),
        in_specs=[_rows(tm, D), _rows(tm, D, C_GC // D), _rows(tm, D, C_GA // D), _rows(tm, D), _rows(tm, D)],
        out_specs=[_rows(tm, D), _rows(tm, D), _rows(tm, 2 * D, C_GC // (2 * D))],
        out_shape=[jax.ShapeDtypeStruct((t, D), BF16), jax.ShapeDtypeStruct((t, D), BF16), jax.ShapeDtypeStruct((t, N_COL), BF16)],
    )(dmerged, proj, proj, y_conv, y_att)


def _att_pre_bwd(dpa, proj, att, dproj, tm):
    t = proj.shape[0]

    def body(dpa_ref, z_ref, att_ref, dp_in, datt_ref, ds_ref, dp_ref):
        del dp_in
        z, dpa_v, att_v = z_ref[...], dpa_ref[...], att_ref[...]
        s = _sig(z)
        datt = dpa_v * (z * s)
        datt_ref[...] = datt
        dp_ref[...] = (dpa_v * att_v * _dsilu(z, s)).astype(BF16)
        prod = datt * att_v
        for h in range(ATT // HEAD):
            sl = slice(h * HEAD, (h + 1) * HEAD)
            ds_ref[:, sl] = jnp.broadcast_to(jnp.sum(prod[:, sl], axis=-1, keepdims=True), (tm, HEAD))

    return pl.pallas_call(
        body, name="att_pre_bwd", grid=(t // tm,),
        in_specs=[_rows(tm, ATT), _rows(tm, ATT, C_ZA // ATT), _rows(tm, ATT), ANY],
        out_specs=[_rows(tm, ATT), _rows(tm, ATT), _rows(tm, ATT, C_ZA // ATT)],
        out_shape=[jax.ShapeDtypeStruct((t, ATT), F32), jax.ShapeDtypeStruct((t, ATT), F32), jax.ShapeDtypeStruct((t, N_COL), BF16)],
        input_output_aliases={3: 2},
    )(dpa, proj, att, dproj)


def _place_qkv(parts, dproj, col_block, tm, name):
    t = dproj.shape[0]

    def body(p0, p1, p2, dp_in, dp_ref):
        del dp_in
        for g, ref in enumerate((p0, p1, p2)):
            dp_ref[:, g * ATT:(g + 1) * ATT] = ref[...]

    return pl.pallas_call(
        body, name=name, grid=(t // tm,),
        in_specs=[_rows(tm, ATT)] * 3 + [ANY],
        out_specs=_rows(tm, QKV, col_block), out_shape=jax.ShapeDtypeStruct((t, N_COL), BF16),
        input_output_aliases={3: 0},
    )(*parts, dproj)


def _conv_bwd_rows(dpc, proj, u1, ln_g, ln_b, dproj, tm):
    t = proj.shape[0]

    def body(dpc_ref, z_ref, u1_ref, lg_ref, lb_ref, dp_in, du1_ref, dp_ref, dlg_ref, dlb_ref, dcb_ref):
        del dp_in
        i = pl.program_id(0)
        u1v = u1_ref[...]
        mu = jnp.mean(u1v, axis=-1, keepdims=True)
        xc = u1v - mu
        r = lax.rsqrt(jnp.mean(xc * xc, axis=-1, keepdims=True) + EPS)
        uhat = xc * r
        u2 = uhat * lg_ref[...] + lb_ref[...]
        s2 = _sig(u2)
        z = z_ref[...]
        sz = _sig(z)
        dpc_v = dpc_ref[...]
        dp_ref[...] = (dpc_v * (u2 * s2) * _dsilu(z, sz)).astype(BF16)
        du2 = dpc_v * (z * sz) * _dsilu(u2, s2)
        duhat = du2 * lg_ref[...]
        du1 = r * (duhat - jnp.mean(duhat, axis=-1, keepdims=True) - uhat * jnp.mean(duhat * uhat, axis=-1, keepdims=True))
        du1_ref[...] = du1
        _acc_rows(dlg_ref, i, du2 * uhat)
        _acc_rows(dlb_ref, i, du2)
        _acc_rows(dcb_ref, i, du1)

    vec = _full((1, D))
    return pl.pallas_call(
        body, name="conv_bwd_rows", grid=(t // tm,),
        in_specs=[_rows(tm, D), _rows(tm, D, C_ZC // D), _rows(tm, D), vec, vec, ANY],
        out_specs=[_rows(tm, D), _rows(tm, D, C_ZC // D), vec, vec, vec],
        out_shape=[jax.ShapeDtypeStruct((t, D), F32), jax.ShapeDtypeStruct((t, N_COL), BF16)] + [jax.ShapeDtypeStruct((1, D), F32)] * 3,
        input_output_aliases={5: 1},
    )(dpc, proj, u1, ln_g, ln_b, dproj)


def _conv_bwd_taps(du1, proj, conv_w, dproj, tm):
    t = proj.shape[0]
    hb = tm // HALO
    last = t // HALO - 1

    def body(du_ref, duh_ref, a_ref, b_ref, ah_ref, bh_ref, w_ref, dp_in, dp_ref, dw_ref, dbuf, ubuf, g0):
        del dp_in
        i = pl.program_id(0)
        a, sb = a_ref[...], _sig(b_ref[...])
        ubuf[0:HALO, :] = jnp.where(i > 0, ah_ref[...] * _sig(bh_ref[...]), 0.0)
        ubuf[HALO:HALO + tm, :] = a * sb
        dbuf[0:tm, :] = du_ref[...]
        dbuf[tm:tm + HALO, :] = jnp.where(i < pl.num_programs(0) - 1, duh_ref[...], 0.0)

        @pl.when(i == 0)
        def _():
            dw_ref[...] = jnp.zeros_like(dw_ref)

        def col(ci, carry):
            c0 = pl.multiple_of(ci * 128, 128)
            for rc in range(tm // 64):
                g0[rc * 64:(rc + 1) * 64, pl.ds(c0, 128)] = _conv_taps(
                    jnp.zeros((64, 128), F32), w_ref, dbuf, rc * 64, c0, lambda j: CONV_K - 1 - j)
            for j in range(CONV_K):
                part = jnp.zeros((8, 128), F32)
                for rc in range(tm // 64):
                    off = rc * 64 + HALO - (CONV_K - 1) + j
                    prod = dbuf[rc * 64:(rc + 1) * 64, pl.ds(c0, 128)] * ubuf[off:off + 64, pl.ds(c0, 128)]
                    part = part + jnp.sum(prod.reshape(8, 8, 128), axis=0)
                dw_ref[j:j + 1, pl.ds(c0, 128)] += jnp.sum(part, axis=0, keepdims=True)
            return carry

        lax.fori_loop(0, D // 128, col, 0)
        du0 = g0[...]
        dp_ref[:, 0:D] = (du0 * sb).astype(BF16)
        dp_ref[:, D:2 * D] = (du0 * a * sb * (1.0 - sb)).astype(BF16)

    prev = lambda col: pl.BlockSpec((HALO, D), lambda i: (jnp.maximum(i * hb - 1, 0), col))
    nxt = pl.BlockSpec((HALO, D), lambda i: (jnp.minimum((i + 1) * hb, last), 0))
    return pl.pallas_call(
        body, name="conv_bwd_taps", grid=(t // tm,),
        in_specs=[_rows(tm, D), nxt, _rows(tm, D, 0), _rows(tm, D, 1), prev(0), prev(1), _full((CONV_KP, D)), ANY],
        out_specs=[_rows(tm, 2 * D, 0), _full((CONV_KP, D))],
        out_shape=[jax.ShapeDtypeStruct((t, N_COL), BF16), jax.ShapeDtypeStruct((CONV_KP, D), F32)],
        scratch_shapes=[pltpu.VMEM((tm + HALO, D), F32), pltpu.VMEM((HALO + tm, D), F32), pltpu.VMEM((tm, D), F32)],
        input_output_aliases={7: 0},
    )(du1, du1, proj, proj, proj, proj, conv_w, dproj)


def _prenorm_bwd(dh, x, dout, mod, norm_g, tm):
    t = x.shape[0]

    def body(dh_ref, x_ref, dout_ref, mod_ref, g_ref, gx_ref, dshift_ref, dscale_ref, dg_ref):
        i = pl.program_id(0)
        xv, dhv = x_ref[...], dh_ref[...]
        r = lax.rsqrt(jnp.mean(xv * xv, axis=-1, keepdims=True) + EPS)
        xn = xv * r
        one_scale = 1.0 + mod_ref[:, D:2 * D]
        dxn = dhv * (g_ref[...] * one_scale)
        gx_ref[...] = r * (dxn - xn * jnp.mean(dxn * xn, axis=-1, keepdims=True)) + dout_ref[...]
        _acc_rows(dshift_ref, i, dhv)
        _acc_rows(dscale_ref, i, dhv * xn * g_ref[...])
        _acc_rows(dg_ref, i, dhv * xn * one_scale)

    vec = _full((1, D))
    return pl.pallas_call(
        body, name="prenorm_bwd", grid=(t // tm,),
        in_specs=[_rows(tm, D), _rows(tm, D), _rows(tm, D), _full((1, 3 * D)), vec],
        out_specs=[_rows(tm, D), vec, vec, vec],
        out_shape=[jax.ShapeDtypeStruct((t, D), F32)] + [jax.ShapeDtypeStruct((1, D), F32)] * 3,
    )(dh, x, dout, mod, norm_g)


def _sum_devices(gathered):
    w = gathered.shape[-1]

    def body(g_ref, o_ref):
        acc = g_ref[0]
        for j in range(1, N_DEV):
            acc = acc + g_ref[j]
        o_ref[...] = acc

    return pl.pallas_call(body, name="sum_devices", grid=(1,), in_specs=[_full(gathered.shape)], out_specs=_full((1, w)),
                          out_shape=jax.ShapeDtypeStruct((1, w), F32))(gathered)


def _rope_tables(positions):
    half = HEAD // 8
    t = positions.shape[-1]
    inv_freq = ROPE_THETA ** (-(jnp.arange(half, dtype=F32) * 2.0 / (2 * half)))
    ang = positions.reshape(t, 1).astype(F32) * inv_freq
    cos, sin = jnp.cos(ang), jnp.sin(ang)
    zeros = lambda n: jnp.zeros((t, n), F32)
    c64 = jnp.concatenate([cos, cos, jnp.ones((t, HEAD - 2 * half), F32)], axis=1)
    lo64 = jnp.concatenate([-sin, zeros(HEAD - half)], axis=1)
    hi64 = jnp.concatenate([zeros(half), sin, zeros(HEAD - 2 * half)], axis=1)
    return tuple(jnp.tile(a, (1, 2)) for a in (c64, lo64, hi64))


def kernel(x, c, positions, norm_g, w_ada, b_ada, w_in, conv_w, conv_b, conv_ln_g, conv_ln_b, w_conv_out, w_att_out, w_o, final_g, loss_target, m_norm_g, m_w_ada, m_b_ada, m_w_in, m_conv_w, m_conv_b, m_conv_ln_g, m_conv_ln_b, m_w_conv_out, m_w_att_out, m_w_o, m_final_g, v_norm_g, v_w_ada, v_b_ada, v_w_in, v_conv_w, v_conv_b, v_conv_ln_g, v_conv_ln_b, v_w_conv_out, v_w_att_out, v_w_o, v_final_g):
    me = 4 * lax.axis_index("x") + 2 * lax.axis_index("y") + lax.axis_index("c")
    x2, tgt = x[0], loss_target[0]
    t = x2.shape[0]
    te = 512 if t % 512 == 0 else 256
    tcv = 256
    tmm = 1024 if t % 1024 == 0 else 256
    n_ada = w_ada.shape[-1]

    pad_taps = lambda a: jnp.pad(a[0], ((0, CONV_KP - CONV_K), (0, 0)))
    shards = (_cast_bf16(w_in[0], "cast_w_in"), _cast_bf16(w_conv_out[0], "cast_w_conv_out"),
              _cast_bf16(w_att_out[0], "cast_w_att_out"), _cast_bf16(w_o[0], "cast_w_o"), pad_taps(conv_w))
    block_of = lambda relations: jnp.bitwise_xor(me, jnp.array(relations, jnp.int32))

    c_all = _allgather_small(c, "gather_c").reshape(N_DEV, D)
    b_ada_l = lax.dynamic_slice(b_ada, (0, me * n_ada), (1, n_ada))
    parts = _allgather_small(_mod_part(c_all, w_ada[0], b_ada_l), "gather_mod")
    mod = lax.dynamic_slice(parts, (0, me, 0), (N_DEV, 1, n_ada)).reshape(1, N_DEV * n_ada)

    h = _prenorm(x2, mod, norm_g, te)
    proj, w_in_f, w_co_f, w_ao_f, w_o_f, conv_w_f = _proj_gather(h, shards, block_of(GATHER_ORDER), tmm)
    u1, pc = _conv_fwd(proj, conv_w_f, conv_b, conv_ln_g, conv_ln_b, tcv)
    tables = _rope_tables(positions)
    parts_att = []
    for gi, dil in GROUPS:
        parts_att += _att_fwd(proj, tables, gi, dil)
    att, lse, pa = _att_combine(parts_att, proj, te)
    y_conv = _matmul(pc, w_co_f, tm=tmm, tn=D, tk=D, name="y_conv")
    y_att = _matmul(pa, w_ao_f, tm=tmm, tn=D, tk=ATT, name="y_att")
    merged = _merge_fwd(proj, y_conv, y_att, te)
    o = _matmul(merged, w_o_f, tm=tmm, tn=D, tk=D, name="out_proj")
    dout, do, sq_sum, g_final, d_gate = _loss_head(x2, o, mod, final_g.reshape(1, D), tgt, te)

    dmerged = _matmul(do, w_o_f, tb=True, tm=tmm, tn=D, tk=D, name="d_merged")
    dw_o = _matmul(merged, do, ta=True, out_dtype=BF16, tm=D, tn=D, tk=512, name="dw_o")
    dyc, dya, dproj = _merge_bwd(dmerged, proj, y_conv, y_att, te)
    dpc = _matmul(dyc, w_co_f, tb=True, tm=tmm, tn=D, tk=D, name="d_pc")
    dw_co = _matmul(pc, dyc, ta=True, out_dtype=BF16, tm=D, tn=D, tk=512, name="dw_conv_out")
    dpa = _matmul(dya, w_ao_f, tb=True, tm=tmm, tn=ATT, tk=D, name="d_pa")
    dw_ao = _matmul(pa, dya, ta=True, out_dtype=BF16, tm=ATT, tn=D, tk=512, name="dw_att_out")
    datt, dsum, dproj = _att_pre_bwd(dpa, proj, att, dproj, te)
    dqs, dks, dvs = [], [], []
    for gi, dil in GROUPS:
        dq, dk, dv = _att_bwd(proj, tables, datt, dsum, lse, gi, dil)
        dqs.append(dq), dks.append(dk), dvs.append(dv)
    dproj = _place_qkv(dqs, dproj, C_Q // QKV, te, "place_dq")
    dproj = _place_qkv(dks, dproj, C_K // QKV, te, "place_dk")
    dproj = _place_qkv(dvs, dproj, C_V // QKV, te, "place_dv")
    du1, dproj, d_ln_g, d_ln_b, d_conv_b = _conv_bwd_rows(dpc, proj, u1, conv_ln_g, conv_ln_b, dproj, te)
    dproj, dconv_w = _conv_bwd_taps(du1, proj, conv_w_f, dproj, tcv)
    dh = _matmul(dproj, w_in_f, tb=True, tm=tmm, tn=D, tk=1024, name="d_h")
    grad_x, d_shift, d_scale, d_norm_g = _prenorm_bwd(dh, x2, dout, mod, norm_g, te)

    packed = jnp.concatenate([d_shift, d_scale, d_gate, d_norm_g, d_conv_b, d_ln_g, d_ln_b, g_final, sq_sum], axis=1)
    gathered = _allgather_small(packed, "gather_partials")
    total = _sum_devices(gathered)
    seg = lambda k, n=1: total[:, k * D:(k + n) * D]
    g_b_ada, g_norm_g, g_conv_b, g_ln_g, g_ln_b, g_final_g = seg(0, 3), seg(3), seg(4), seg(5), seg(6), seg(7)
    loss = (0.5 / D) * jnp.sum(seg(8))
    dmod_all = gathered[:, 0, 0:3 * D]
    dmod_cols = lax.dynamic_slice(dmod_all, (0, me * n_ada), (N_DEV, n_ada))
    g_w_ada, d_w_ada, nm_w_ada, nv_w_ada = _w_ada_update(c_all.T, dmod_cols, w_ada[0], m_w_ada[0], v_w_ada[0])

    small = {}
    for name, g, w, m, v in (("norm_g", g_norm_g, norm_g, m_norm_g, v_norm_g), ("b_ada", g_b_ada, b_ada, m_b_ada, v_b_ada),
                             ("conv_b", g_conv_b, conv_b, m_conv_b, v_conv_b), ("conv_ln_g", g_ln_g, conv_ln_g, m_conv_ln_g, v_conv_ln_g),
                             ("conv_ln_b", g_ln_b, conv_ln_b, m_conv_ln_b, v_conv_ln_b),
                             ("final_g", g_final_g, final_g.reshape(1, D), m_final_g.reshape(1, D), v_final_g.reshape(1, D))):
        small[name] = (g,) + tuple(_adamw_small(g, w, m, v, "adamw_" + name))

    slots = _dw_in_scatter(h, dproj, (dw_co, dw_ao, dw_o, dconv_w), block_of(SCATTER_ORDER), 512)
    big = {
        "w_in": _sum_adamw(slots[0], w_in[0], m_w_in[0], v_w_in[0], 256, "adamw_w_in"),
        "w_conv_out": _sum_adamw(slots[1], w_conv_out[0], m_w_conv_out[0], v_w_conv_out[0], 128, "adamw_w_conv_out"),
        "w_att_out": _sum_adamw(slots[2], w_att_out[0], m_w_att_out[0], v_w_att_out[0], 512, "adamw_w_att_out"),
        "w_o": _sum_adamw(slots[3], w_o[0], m_w_o[0], v_w_o[0], 128, "adamw_w_o"),
        "conv_w": [r[:CONV_K] for r in _sum_adamw(slots[4], pad_taps(conv_w), pad_taps(m_conv_w), pad_taps(v_conv_w), CONV_KP, "adamw_conv_w")],
    }
    big["w_ada"] = (g_w_ada, d_w_ada, nm_w_ada, nv_w_ada)

    order = ("norm_g", "w_ada", "b_ada", "w_in", "conv_w", "conv_b", "conv_ln_g", "conv_ln_b", "w_conv_out", "w_att_out", "w_o", "final_g")
    lead = lambda name, a: a.reshape(D) if name == "final_g" else (a[None] if name in big else a)
    result = {**small, **big}
    outs = [loss, grad_x[None]]
    for field in range(4):
        outs += [lead(name, result[name][field]) for name in order]
    return tuple(outs)
```

```python
import functools

import jax
import jax.numpy as jnp
from jax import lax
from jax.experimental import pallas as pl
from jax.experimental.pallas import tpu as pltpu

F32 = jnp.float32
BF16 = jnp.bfloat16

N_DEV = 8
D = 1024
N_COL = 10240
C_A, C_B, C_ZC, C_Q, C_K, C_V, C_ZA, C_GC, C_GA = 0, 1024, 2048, 3072, 4608, 6144, 7680, 8192, 9216
QKV = 1536
ATT = 512
HEAD = 64
BLK = 128
TILE = 2048
GROUPS = ((0, 1), (1, 4), (2, 16))
CONV_K = 31
CONV_KP = 32
HALO = 32
EPS = 1e-6
NEG_INF = -1e30
ROPE_THETA = 500000.0
SM_SCALE = HEAD ** -0.5

ADAM_LR, ADAM_B1, ADAM_B2, ADAM_EPS, ADAM_WD, ADAM_STEP = 0.001, 0.9, 0.999, 1e-08, 0.01, 10

MESH = pl.DeviceIdType.MESH
ANY = pl.BlockSpec(memory_space=pl.ANY)


def _sig(v):
    return 1.0 / (1.0 + jnp.exp(-v))


def _dsilu(v, s):
    return s * (1.0 + v * (1.0 - s))


def _full(shape):
    return pl.BlockSpec(shape, lambda *_: (0,) * len(shape))


def _rows(tm, width, col=0):
    return pl.BlockSpec((tm, width), lambda i: (i, col))


def _matmul(a, b, *, ta=False, tb=False, out_dtype=F32, tm, tn, tk, name):
    m, k = (a.shape[1], a.shape[0]) if ta else a.shape
    n = b.shape[0] if tb else b.shape[1]
    assert (b.shape[1] if tb else b.shape[0]) == k
    assert m % tm == 0 and n % tn == 0 and k % tk == 0
    nk = k // tk
    dims = (((0 if ta else 1,), (1 if tb else 0,)), ((), ()))
    use_scratch = out_dtype != F32 and nk > 1

    def body(a_ref, b_ref, o_ref, *scratch):
        p = lax.dot_general(a_ref[...], b_ref[...], dims, preferred_element_type=F32)
        if nk == 1:
            o_ref[...] = p.astype(out_dtype)
            return
        acc = scratch[0] if use_scratch else o_ref
        kk = pl.program_id(2)

        @pl.when(kk == 0)
        def _():
            acc[...] = p

        @pl.when(kk > 0)
        def _():
            acc[...] += p

        if use_scratch:
            @pl.when(kk == nk - 1)
            def _():
                o_ref[...] = acc[...].astype(out_dtype)

    a_spec = pl.BlockSpec((tk, tm), lambda i, j, kk: (kk, i)) if ta else pl.BlockSpec((tm, tk), lambda i, j, kk: (i, kk))
    b_spec = pl.BlockSpec((tn, tk), lambda i, j, kk: (j, kk)) if tb else pl.BlockSpec((tk, tn), lambda i, j, kk: (kk, j))
    return pl.pallas_call(
        body, name=name, grid=(m // tm, n // tn, nk),
        in_specs=[a_spec, b_spec],
        out_specs=pl.BlockSpec((tm, tn), lambda i, j, kk: (i, j)),
        out_shape=jax.ShapeDtypeStruct((m, n), out_dtype),
        scratch_shapes=[pltpu.VMEM((tm, tn), F32)] if use_scratch else [],
    )(a, b)


def _me_and_peers():
    x, y, c = lax.axis_index("x"), lax.axis_index("y"), lax.axis_index("c")
    me = 4 * x + 2 * y + c
    peers = []
    for k in range(1, N_DEV):
        px, py, pc = x ^ (k >> 2), y ^ ((k >> 1) & 1), c ^ (k & 1)
        peers.append(((px, py, pc), 4 * px + 2 * py + pc))
    return me, peers


def _allgather_small(v, name):
    r, c = v.shape

    def body(v_ref, out_ref, send_sems, recv_sems):
        me, peers = _me_and_peers()
        out_ref[me] = v_ref[...]
        copies = []
        for k, (dev, _) in enumerate(peers):
            cp = pltpu.make_async_remote_copy(src_ref=v_ref, dst_ref=out_ref.at[me], send_sem=send_sems.at[k],
                                              recv_sem=recv_sems.at[k], device_id=dev, device_id_type=MESH)
            cp.start()
            copies.append(cp)
        for k, (dev, idx) in enumerate(peers):
            pltpu.make_async_remote_copy(src_ref=v_ref, dst_ref=out_ref.at[idx], send_sem=send_sems.at[k],
                                         recv_sem=recv_sems.at[k], device_id=dev, device_id_type=MESH).wait_recv()
        for cp in copies:
            cp.wait_send()

    return pl.pallas_call(
        body, name=name,
        in_specs=[pl.BlockSpec(memory_space=pltpu.VMEM)],
        out_specs=pl.BlockSpec(memory_space=pltpu.VMEM),
        out_shape=jax.ShapeDtypeStruct((N_DEV, r, c), v.dtype),
        scratch_shapes=[pltpu.SemaphoreType.DMA((N_DEV - 1,)), pltpu.SemaphoreType.DMA((N_DEV - 1,))],
    )(v)


def _window(ref, kind, idx, size):
    start = pl.multiple_of(idx * size, size)
    if kind == "rows":
        return ref.at[pl.ds(start, size), :]
    return ref.at[:, pl.ds(start, size)]


_BIG = (("cols", N_COL // N_DEV), ("rows", D // N_DEV), ("cols", D // N_DEV), ("rows", D // N_DEV), ("cols", D // N_DEV))


GATHER_ORDER = (0, 1, 2, 4, 3, 5, 6, 7)
SCATTER_ORDER = (6, 4, 2, 7, 5, 3, 1, 0)


def _proj_gather(h, shards, order, tm):
    t = h.shape[0]
    nt = len(shards)
    n_blk = N_COL // N_DEV
    full_shapes = []
    for s, (kind, size) in zip(shards, _BIG):
        full_shapes.append(jax.ShapeDtypeStruct((s.shape[0] * N_DEV, s.shape[1]) if kind == "rows"
                                                else (s.shape[0], s.shape[1] * N_DEV), s.dtype))
    last = (N_DEV - 1, t // tm - 1)

    def body(order_ref, h_ref, *refs):
        src, proj_ref, dst = refs[:nt], refs[nt], refs[nt + 1:2 * nt + 1]
        wbuf, send_sems, recv_sems, local_sems, load_sem = refs[2 * nt + 1:]
        j, i = pl.program_id(0), pl.program_id(1)
        me, peers = _me_and_peers()

        def local(tn):
            kind, size = _BIG[tn]
            return pltpu.make_async_copy(src[tn], _window(dst[tn], kind, me, size), local_sems.at[tn])

        def remote(tn, k, block_of):
            kind, size = _BIG[tn]
            dev, idx = peers[k - 1]
            return pltpu.make_async_remote_copy(src_ref=src[tn], dst_ref=_window(dst[tn], kind, me if block_of == "mine" else idx, size),
                                                send_sem=send_sems.at[tn, k - 1], recv_sem=recv_sems.at[tn, k - 1],
                                                device_id=dev, device_id_type=MESH)

        @pl.when((j == 0) & (i == 0))
        def _():
            for tn in range(nt):
                local(tn).start()
                for k in GATHER_ORDER[1:]:
                    remote(tn, k, "mine").start()

        @pl.when(i == 0)
        def _():
            for step, k in enumerate(GATHER_ORDER):
                @pl.when(j == step)
                def _():
                    if k == 0:
                        local(0).wait()
                    else:
                        remote(0, k, "theirs").wait_recv()
            blk = pltpu.make_async_copy(_window(dst[0], "cols", order_ref[j], n_blk), wbuf, load_sem)
            blk.start()
            blk.wait()

        proj_ref[...] = jnp.dot(h_ref[...], wbuf[...], preferred_element_type=F32)

        @pl.when((j == last[0]) & (i == last[1]))
        def _():
            for tn in range(1, nt):
                local(tn).wait()
                for k in range(1, N_DEV):
                    remote(tn, k, "theirs").wait_recv()
            for tn in range(nt):
                for k in range(1, N_DEV):
                    remote(tn, k, "mine").wait_send()

    grid_spec = pltpu.PrefetchScalarGridSpec(
        num_scalar_prefetch=1, grid=(N_DEV, t // tm),
        in_specs=[pl.BlockSpec((tm, D), lambda j, i, order_ref: (i, 0))] + [ANY] * nt,
        out_specs=[pl.BlockSpec((tm, n_blk), lambda j, i, order_ref: (i, order_ref[j]))] + [ANY] * nt,
        scratch_shapes=[pltpu.VMEM((D, n_blk), BF16), pltpu.SemaphoreType.DMA((nt, N_DEV - 1)),
                        pltpu.SemaphoreType.DMA((nt, N_DEV - 1)), pltpu.SemaphoreType.DMA((nt,)), pltpu.SemaphoreType.DMA(())],
    )
    return pl.pallas_call(
        body, name="proj_gather", grid_spec=grid_spec,
        out_shape=[jax.ShapeDtypeStruct((t, N_COL), F32)] + full_shapes,
    )(order, h, *shards)


def _dw_in_scatter(h, dproj, small_grads, order, tk):
    t = h.shape[0]
    nt = 1 + len(small_grads)
    n_blk = N_COL // N_DEV
    nk = t // tk
    slot_shapes = [jax.ShapeDtypeStruct((N_DEV, D, n_blk), BF16)]
    for g, (kind, size) in zip(small_grads, _BIG[1:]):
        slot_shapes.append(jax.ShapeDtypeStruct((N_DEV,) + ((size, g.shape[1]) if kind == "rows" else (g.shape[0], size)), g.dtype))

    def body(order_ref, h_ref, dp_ref, *refs):
        src, dst = refs[:nt - 1], refs[nt - 1:2 * nt - 1]
        acc, stage, send_sems, recv_sems, local_sems = refs[2 * nt - 1:]
        j, kk = pl.program_id(0), pl.program_id(1)
        me, peers = _me_and_peers()

        def small_local(tn):
            kind, size = _BIG[tn]
            return pltpu.make_async_copy(_window(src[tn - 1], kind, me, size), dst[tn].at[me], local_sems.at[tn])

        def small_remote(tn, k, mine):
            kind, size = _BIG[tn]
            dev, idx = peers[k - 1]
            return pltpu.make_async_remote_copy(src_ref=_window(src[tn - 1], kind, idx if mine else me, size),
                                                dst_ref=dst[tn].at[me if mine else idx],
                                                send_sem=send_sems.at[tn, k - 1], recv_sem=recv_sems.at[tn, k - 1],
                                                device_id=dev, device_id_type=MESH)

        def big_remote(k, slot, mine):
            dev, idx = peers[k - 1]
            return pltpu.make_async_remote_copy(src_ref=stage.at[slot], dst_ref=dst[0].at[me if mine else idx],
                                                send_sem=send_sems.at[0, k - 1], recv_sem=recv_sems.at[0, k - 1],
                                                device_id=dev, device_id_type=MESH)

        def big_local(slot):
            return pltpu.make_async_copy(stage.at[slot], dst[0].at[me], local_sems.at[0])

        @pl.when((j == 0) & (kk == 0))
        def _():
            for tn in range(1, nt):
                small_local(tn).start()
                for k in SCATTER_ORDER[:-1]:
                    small_remote(tn, k, True).start()

        p = lax.dot_general(h_ref[...], dp_ref[...], (((0,), (0,)), ((), ())), preferred_element_type=F32)

        @pl.when(kk == 0)
        def _():
            acc[...] = p

        @pl.when(kk > 0)
        def _():
            acc[...] += p

        @pl.when(kk == nk - 1)
        def _():
            for step, k in enumerate(SCATTER_ORDER):
                @pl.when(j == step)
                def _():
                    slot = step % 2
                    if step >= 2:
                        big_remote(SCATTER_ORDER[step - 2], slot, True).wait_send()
                    stage[slot] = acc[...].astype(BF16)
                    if k == 0:
                        big_local(slot).start()
                    else:
                        big_remote(k, slot, True).start()

        @pl.when((j == N_DEV - 1) & (kk == nk - 1))
        def _():
            big_remote(SCATTER_ORDER[N_DEV - 2], (N_DEV - 2) % 2, True).wait_send()
            big_local((N_DEV - 1) % 2).wait()
            for k in range(1, N_DEV):
                big_remote(k, 0, False).wait_recv()
            for tn in range(1, nt):
                small_local(tn).wait()
                for k in range(1, N_DEV):
                    small_remote(tn, k, False).wait_recv()
                    small_remote(tn, k, True).wait_send()

    grid_spec = pltpu.PrefetchScalarGridSpec(
        num_scalar_prefetch=1, grid=(N_DEV, nk),
        in_specs=[pl.BlockSpec((tk, D), lambda j, kk, order_ref: (kk, 0)),
                  pl.BlockSpec((tk, n_blk), lambda j, kk, order_ref: (kk, order_ref[j]))] + [ANY] * (nt - 1),
        out_specs=[ANY] * nt,
        scratch_shapes=[pltpu.VMEM((D, n_blk), F32), pltpu.VMEM((2, D, n_blk), BF16), pltpu.SemaphoreType.DMA((nt, N_DEV - 1)),
                        pltpu.SemaphoreType.DMA((nt, N_DEV - 1)), pltpu.SemaphoreType.DMA((nt,))],
    )
    return pl.pallas_call(body, name="dw_in_scatter", grid_spec=grid_spec, out_shape=slot_shapes)(order, h, dproj, *small_grads)


def _adamw_math(w, g, m, v):
    m = ADAM_B1 * m + (1.0 - ADAM_B1) * g
    v = ADAM_B2 * v + (1.0 - ADAM_B2) * (g * g)
    m_hat = m / (1.0 - ADAM_B1 ** ADAM_STEP)
    v_hat = v / (1.0 - ADAM_B2 ** ADAM_STEP)
    delta = -ADAM_LR * (m_hat / (jnp.sqrt(v_hat) + ADAM_EPS) + ADAM_WD * w)
    return delta, m, v


def _sum_adamw(slots, w, m, v, tr, name):
    _, r, c = slots.shape
    assert r % tr == 0

    def body(s_ref, w_ref, m_ref, v_ref, g_ref, d_ref, nm_ref, nv_ref):
        g = s_ref[0].astype(F32)
        for j in range(1, N_DEV):
            g = g + s_ref[j].astype(F32)
        delta, nm, nv = _adamw_math(w_ref[...], g, m_ref[...], v_ref[...])
        g_ref[...] = g
        d_ref[...] = delta
        nm_ref[...] = nm
        nv_ref[...] = nv

    blk = pl.BlockSpec((tr, c), lambda i: (i, 0))
    return pl.pallas_call(
        body, name=name, grid=(r // tr,),
        in_specs=[pl.BlockSpec((N_DEV, tr, c), lambda i: (0, i, 0)), blk, blk, blk],
        out_specs=[blk] * 4, out_shape=[jax.ShapeDtypeStruct((r, c), F32)] * 4,
    )(slots, w, m, v)


def _adamw_small(g, w, m, v, name):
    def body(g_ref, w_ref, m_ref, v_ref, d_ref, nm_ref, nv_ref):
        delta, nm, nv = _adamw_math(w_ref[...], g_ref[...], m_ref[...], v_ref[...])
        d_ref[...] = delta
        nm_ref[...] = nm
        nv_ref[...] = nv

    spec = _full(g.shape)
    return pl.pallas_call(body, name=name, grid=(1,), in_specs=[spec] * 4, out_specs=[spec] * 3,
                          out_shape=[jax.ShapeDtypeStruct(g.shape, F32)] * 3)(g, w, m, v)


def _mod_part(c_all, w_ada_l, b_ada_l):
    n = w_ada_l.shape[1]

    def body(c_ref, w_ref, b_ref, o_ref):
        o_ref[...] = jnp.dot(c_ref[...], w_ref[...], preferred_element_type=F32,
                             precision=lax.Precision.HIGHEST) + b_ref[...]

    return pl.pallas_call(body, name="mod_part", grid=(1,),
                          in_specs=[_full(c_all.shape), _full(w_ada_l.shape), _full(b_ada_l.shape)],
                          out_specs=_full((N_DEV, n)), out_shape=jax.ShapeDtypeStruct((N_DEV, n), F32))(c_all, w_ada_l, b_ada_l)


def _w_ada_update(c_all_t, dmod_cols, w, m, v):
    def body(c_ref, dm_ref, w_ref, m_ref, v_ref, g_ref, d_ref, nm_ref, nv_ref):
        g = c_ref[:, 0:1] * dm_ref[0:1, :]
        for b in range(1, N_DEV):
            g = g + c_ref[:, b:b + 1] * dm_ref[b:b + 1, :]
        delta, nm, nv = _adamw_math(w_ref[...], g, m_ref[...], v_ref[...])
        g_ref[...] = g
        d_ref[...] = delta
        nm_ref[...] = nm
        nv_ref[...] = nv

    spec = _full(w.shape)
    return pl.pallas_call(body, name="w_ada_update", grid=(1,),
                          in_specs=[_full(c_all_t.shape), _full(dmod_cols.shape), spec, spec, spec],
                          out_specs=[spec] * 4, out_shape=[jax.ShapeDtypeStruct(w.shape, F32)] * 4)(c_all_t, dmod_cols, w, m, v)


def _cast_bf16(w, name):
    def body(w_ref, o_ref):
        o_ref[...] = w_ref[...].astype(BF16)

    return pl.pallas_call(body, name=name, grid=(1,), in_specs=[_full(w.shape)], out_specs=_full(w.shape),
                          out_shape=jax.ShapeDtypeStruct(w.shape, BF16))(w)


def _prenorm(x, mod, norm_g, tm):
    t = x.shape[0]

    def body(x_ref, mod_ref, g_ref, h_ref):
        xv = x_ref[...]
        r = lax.rsqrt(jnp.mean(xv * xv, axis=-1, keepdims=True) + EPS)
        h = (xv * r) * g_ref[...] * (1.0 + mod_ref[:, D:2 * D]) + mod_ref[:, 0:D]
        h_ref[...] = h.astype(BF16)

    return pl.pallas_call(body, name="prenorm", grid=(t // tm,),
                          in_specs=[_rows(tm, D), _full((1, 3 * D)), _full((1, D))],
                          out_specs=_rows(tm, D), out_shape=jax.ShapeDtypeStruct((t, D), BF16))(x, mod, norm_g)


def _rope_apply(t, cos, s_lo, s_hi):
    return t * cos + pltpu.roll(t, 120, 1) * s_lo + pltpu.roll(t, 8, 1) * s_hi


def _shift_copies(sh, buf, c0):
    rows = buf.shape[0] - 8
    for s in range(1, 8):
        sh[s, 0:rows, :] = buf[s:s + rows, pl.ds(c0, 128)]


def _window64(buf, sh, c0, start):
    s = start % 8
    if s == 0:
        return buf[start:start + 64, pl.ds(c0, 128)]
    return sh[s, start - s:start - s + 64, :]


def _conv_taps(acc_init, w_ref, buf, sh, row0, c0, offset_of_tap):
    acc = acc_init
    for j in range(CONV_K):
        acc = acc + w_ref[j:j + 1, pl.ds(c0, 128)] * _window64(buf, sh, c0, row0 + offset_of_tap(j))
    return acc


def _conv_fwd(proj, conv_w, conv_b, ln_g, ln_b, tm):
    t = proj.shape[0]
    hb = tm // HALO

    def body(a_ref, b_ref, z_ref, ah_ref, bh_ref, w_ref, cb_ref, lg_ref, lb_ref, u1_ref, pc_ref, ubuf, sh):
        i = pl.program_id(0)
        u0h = ah_ref[...] * _sig(bh_ref[...])
        ubuf[0:HALO, :] = jnp.where(i > 0, u0h, 0.0)
        ubuf[HALO:HALO + tm, :] = a_ref[...] * _sig(b_ref[...])

        def col(ci, carry):
            c0 = pl.multiple_of(ci * 128, 128)
            _shift_copies(sh, ubuf, c0)
            for rc in range(tm // 64):
                init = jnp.zeros((64, 128), F32)
                acc = _conv_taps(init, w_ref, ubuf, sh, rc * 64, c0, lambda j: HALO - (CONV_K - 1) + j)
                u1_ref[rc * 64:(rc + 1) * 64, pl.ds(c0, 128)] = acc + cb_ref[:, pl.ds(c0, 128)]
            return carry

        lax.fori_loop(0, D // 128, col, 0)
        u1 = u1_ref[...]
        mu = jnp.mean(u1, axis=-1, keepdims=True)
        xc = u1 - mu
        var = jnp.mean(xc * xc, axis=-1, keepdims=True)
        u2 = xc * lax.rsqrt(var + EPS) * lg_ref[...] + lb_ref[...]
        z = z_ref[...]
        pc_ref[...] = (u2 * _sig(u2) * (z * _sig(z))).astype(BF16)

    halo = pl.BlockSpec((HALO, D), lambda i: (jnp.maximum(i * hb - 1, 0), 0))
    halo_b = pl.BlockSpec((HALO, D), lambda i: (jnp.maximum(i * hb - 1, 0), 1))
    return pl.pallas_call(
        body, name="conv_fwd", grid=(t // tm,),
        in_specs=[_rows(tm, D, 0), _rows(tm, D, 1), _rows(tm, D, 2), halo, halo_b,
                  _full((CONV_KP, D)), _full((1, D)), _full((1, D)), _full((1, D))],
        out_specs=[_rows(tm, D), _rows(tm, D)],
        out_shape=[jax.ShapeDtypeStruct((t, D), F32), jax.ShapeDtypeStruct((t, D), BF16)],
        scratch_shapes=[pltpu.VMEM((HALO + tm, D), F32), pltpu.VMEM((8, HALO + tm, 128), F32)],
    )(proj, proj, proj, proj, proj, conv_w, conv_b, ln_g, ln_b)


def _band_masks_t(has_prev):
    key = lax.broadcasted_iota(jnp.int32, (BLK, BLK), 0)
    qry = lax.broadcasted_iota(jnp.int32, (BLK, BLK), 1)
    return jnp.logical_and(key >= qry, has_prev), key <= qry


def _head_lanes(pair, hh):
    lane = lax.broadcasted_iota(jnp.int32, pair.shape, 1)
    return jnp.where((lane >= hh * HEAD) & (lane < (hh + 1) * HEAD), pair, jnp.zeros_like(pair))


def _pair_mask(has_prev):
    mask_p, mask_c = _band_masks_t(has_prev)
    both = jnp.concatenate([mask_p, mask_c], axis=0)
    return jnp.concatenate([both, both], axis=1)


def _query_pair(pair):
    return jnp.concatenate([_head_lanes(pair, 0), _head_lanes(pair, 1)], axis=0)


def _key_pair(ref, prev, cur):
    return jnp.concatenate([ref[pl.ds(prev, BLK), :], ref[pl.ds(cur, BLK), :]], axis=0)


def _own_head(both):
    return jnp.concatenate([both[0:HEAD, 0:BLK], both[HEAD:2 * HEAD, BLK:2 * BLK]], axis=0)


def _store_transposed(dst, base, src):
    for j in range(TILE // BLK):
        dst[base // BLK + j] = src[j * BLK:(j + 1) * BLK, :].T.astype(BF16)


class _Dilated:
    def __init__(self, dil):
        self.dil = dil
        self.per = TILE // dil
        self.nbr = self.per // BLK

    def spread(self, dst, base, src_ref, dtype):
        for r in range(self.dil):
            rows = src_ref[pl.ds(r, self.per, stride=self.dil), :] if self.dil > 1 else src_ref[...]
            dst[pl.ds(pl.multiple_of(base + r * self.per, BLK), self.per), :] = rows.astype(dtype)

    def gather(self, dst_ref, src, base):
        for r in range(self.dil):
            rows = src[pl.ds(pl.multiple_of(base + r * self.per, BLK), self.per), :]
            if self.dil > 1:
                dst_ref[pl.ds(r, self.per, stride=self.dil), :] = rows
            else:
                dst_ref[...] = rows

    def block_rows(self, b, i, cur, prv):
        n = b % self.nbr
        row = pl.multiple_of(b * BLK, BLK)
        has_prev = jnp.logical_or(n > 0, i > 0)
        prev = jnp.where(n > 0, cur + row - BLK, jnp.where(i > 0, prv + row + (self.nbr - 1) * BLK, cur + row))
        return row, pl.multiple_of(prev, BLK), has_prev


def _slots(i):
    return pl.multiple_of((i % 2) * TILE, TILE), pl.multiple_of(((i + 1) % 2) * TILE, TILE)


def _nt(a, b):
    return lax.dot_general(a, b, (((1,), (1,)), ((), ())), preferred_element_type=F32)


def _qkv_specs(gi, clamp_to=None):
    def spec(col0):
        def imap(hp, i):
            return (i if clamp_to is None else jnp.minimum(i, clamp_to), (col0 + gi * ATT) // 128 + hp)
        return pl.BlockSpec((TILE, 128), imap)
    return [spec(C_Q), spec(C_K), spec(C_V)]


def _att_fwd(proj, tables, gi, dil):
    t = proj.shape[0]
    dl = _Dilated(dil)

    def body(q_ref, k_ref, v_ref, c_ref, lo_ref, hi_ref, o_ref, lse_ref, tmp, qd, kd, vt, od, ld):
        i = pl.program_id(1)
        cur, prv = _slots(i)
        cs, lo, hi = c_ref[...], lo_ref[...], hi_ref[...]
        tmp[...] = _rope_apply(q_ref[...], cs, lo, hi) * SM_SCALE
        dl.spread(qd, 0, tmp, BF16)
        tmp[...] = _rope_apply(k_ref[...], cs, lo, hi)
        dl.spread(kd, cur, tmp, BF16)
        dl.spread(tmp, 0, v_ref, F32)
        _store_transposed(vt, cur, tmp)

        def block(b, carry):
            row, prev, has_prev = dl.block_rows(b, i, cur, prv)
            s = jnp.where(_pair_mask(has_prev), _nt(_key_pair(kd, prev, cur + row), _query_pair(qd[pl.ds(row, BLK), :])), NEG_INF)
            mx = jnp.max(s, axis=0, keepdims=True)
            p = jnp.exp(s - mx)
            den = jnp.sum(p, axis=0, keepdims=True)
            v_t = jnp.concatenate([vt[prev // BLK], vt[(cur + row) // BLK]], axis=1)
            acc = jnp.dot(v_t, p.astype(BF16), preferred_element_type=F32) / den
            lse = mx + jnp.log(den)
            od[pl.ds(row, BLK), :] = _own_head(acc).T
            ld[pl.ds(row, BLK), :] = _own_head(jnp.broadcast_to(lse, (2 * HEAD, 2 * BLK))).T
            return carry

        lax.fori_loop(0, TILE // BLK, block, 0, unroll=4)
        dl.gather(o_ref, od, 0)
        dl.gather(lse_ref, ld, 0)

    tab = pl.BlockSpec((TILE, 128), lambda hp, i: (i, 0))
    out_spec = pl.BlockSpec((TILE, 128), lambda hp, i: (i, hp))
    return pl.pallas_call(
        body, name=f"att_fwd_g{gi}", grid=(ATT // 128, t // TILE),
        in_specs=_qkv_specs(gi) + [tab] * 3,
        out_specs=[out_spec] * 2, out_shape=[jax.ShapeDtypeStruct((t, ATT), F32)] * 2,
        scratch_shapes=[pltpu.VMEM((TILE, 128), F32), pltpu.VMEM((TILE, 128), BF16), pltpu.VMEM((2 * TILE, 128), BF16),
                        pltpu.VMEM((2 * TILE // BLK, 128, BLK), BF16), pltpu.VMEM((TILE, 128), F32), pltpu.VMEM((TILE, 128), F32)],
    )(proj, proj, proj, *tables)


def _att_combine(parts, proj, tm):
    t = proj.shape[0]

    def body(o0, l0, o1, l1, o2, l2, z_ref, att_ref, lse_ref, pa_ref):
        m_all = jnp.maximum(jnp.maximum(l0[...], l1[...]), l2[...])
        w0, w1, w2 = jnp.exp(l0[...] - m_all), jnp.exp(l1[...] - m_all), jnp.exp(l2[...] - m_all)
        den = w0 + w1 + w2
        att = (w0 * o0[...] + w1 * o1[...] + w2 * o2[...]) / den
        z = z_ref[...]
        att_ref[...] = att
        lse_ref[...] = m_all + jnp.log(den)
        pa_ref[...] = (att * (z * _sig(z))).astype(BF16)

    spec = _rows(tm, ATT)
    return pl.pallas_call(
        body, name="att_combine", grid=(t // tm,),
        in_specs=[spec] * 6 + [_rows(tm, ATT, C_ZA // ATT)],
        out_specs=[spec] * 3,
        out_shape=[jax.ShapeDtypeStruct((t, ATT), F32)] * 2 + [jax.ShapeDtypeStruct((t, ATT), BF16)],
    )(*parts, proj)


def _att_bwd(proj, tables, datt, dsum, lse, gi, dil):
    t = proj.shape[0]
    nt = t // TILE
    dl = _Dilated(dil)

    def body(q_ref, k_ref, v_ref, c_ref, lo_ref, hi_ref, cl_ref, lol_ref, hil_ref, do_ref, ds_ref, lse_ref,
             dq_ref, dk_ref, dv_ref, tmp, qd, kd, vd, dod, dsd, lsd, dqd, dkd, dvd, kt):
        i = pl.program_id(1)
        cur, prv = _slots(i)

        @pl.when(i < nt)
        def _():
            cs, lo, hi = c_ref[...], lo_ref[...], hi_ref[...]
            tmp[...] = _rope_apply(q_ref[...], cs, lo, hi) * SM_SCALE
            dl.spread(qd, 0, tmp, BF16)
            tmp[...] = _rope_apply(k_ref[...], cs, lo, hi)
            dl.spread(kd, cur, tmp, BF16)
            dl.spread(dqd, 0, tmp, F32)
            _store_transposed(kt, cur, dqd)
            dl.spread(vd, cur, v_ref, BF16)
            dl.spread(dod, 0, do_ref, BF16)
            dl.spread(dsd, 0, ds_ref, F32)
            dl.spread(lsd, 0, lse_ref, F32)
            dkd[pl.ds(cur, TILE), :] = jnp.zeros((TILE, 128), F32)
            dvd[pl.ds(cur, TILE), :] = jnp.zeros((TILE, 128), F32)

            def block(b, carry):
                row, prev, has_prev = dl.block_rows(b, i, cur, prv)
                q_pair, do_pair = _query_pair(qd[pl.ds(row, BLK), :]), _query_pair(dod[pl.ds(row, BLK), :])
                k_pair, v_pair = _key_pair(kd, prev, cur + row), _key_pair(vd, prev, cur + row)
                ds_t, ls_t = dsd[pl.ds(row, BLK), :].T, lsd[pl.ds(row, BLK), :].T
                lse = jnp.concatenate([ls_t[0:1, :], ls_t[HEAD:HEAD + 1, :]], axis=1)
                dsm = jnp.concatenate([ds_t[0:1, :], ds_t[HEAD:HEAD + 1, :]], axis=1)
                p = jnp.exp(jnp.where(_pair_mask(has_prev), _nt(k_pair, q_pair), NEG_INF) - lse)
                ds = (p * (_nt(v_pair, do_pair) - dsm)).astype(BF16)
                k_t = jnp.concatenate([kt[prev // BLK], kt[(cur + row) // BLK]], axis=1)
                dqd[pl.ds(row, BLK), :] = _own_head(jnp.dot(k_t, ds, preferred_element_type=F32)).T * SM_SCALE
                dk = jnp.dot(ds, q_pair, preferred_element_type=F32)
                dv = jnp.dot(p.astype(BF16), do_pair, preferred_element_type=F32)
                dkd[pl.ds(cur + row, BLK), :] += dk[BLK:2 * BLK, :]
                dvd[pl.ds(cur + row, BLK), :] += dv[BLK:2 * BLK, :]
                dkd[pl.ds(prev, BLK), :] += dk[0:BLK, :]
                dvd[pl.ds(prev, BLK), :] += dv[0:BLK, :]
                return carry

            lax.fori_loop(0, TILE // BLK, block, 0, unroll=4)
            dl.gather(tmp, dqd, 0)
            dq_ref[...] = _rope_apply(tmp[...], cs, -lo, -hi).astype(BF16)

        @pl.when(i > 0)
        def _():
            dl.gather(tmp, dkd, prv)
            dk_ref[...] = _rope_apply(tmp[...], cl_ref[...], -lol_ref[...], -hil_ref[...]).astype(BF16)
            dl.gather(tmp, dvd, prv)
            dv_ref[...] = tmp[...].astype(BF16)

    now = lambda col: pl.BlockSpec((TILE, 128), lambda hp, i: (jnp.minimum(i, nt - 1), col(hp)))
    lag = lambda col: pl.BlockSpec((TILE, 128), lambda hp, i: (jnp.maximum(i - 1, 0), col(hp)))
    first, pair = (lambda hp: 0), (lambda hp: hp)
    return pl.pallas_call(
        body, name=f"att_bwd_g{gi}", grid=(ATT // 128, nt + 1),
        in_specs=_qkv_specs(gi, nt - 1) + [now(first)] * 3 + [lag(first)] * 3 + [now(pair)] * 3,
        out_specs=[now(pair), lag(pair), lag(pair)],
        out_shape=[jax.ShapeDtypeStruct((t, ATT), BF16)] * 3,
        scratch_shapes=[pltpu.VMEM((TILE, 128), F32), pltpu.VMEM((TILE, 128), BF16), pltpu.VMEM((2 * TILE, 128), BF16),
                        pltpu.VMEM((2 * TILE, 128), BF16), pltpu.VMEM((TILE, 128), BF16), pltpu.VMEM((TILE, 128), F32),
                        pltpu.VMEM((TILE, 128), F32), pltpu.VMEM((TILE, 128), F32), pltpu.VMEM((2 * TILE, 128), F32),
                        pltpu.VMEM((2 * TILE, 128), F32), pltpu.VMEM((2 * TILE // BLK, 128, BLK), BF16)],
    )(proj, proj, proj, *tables, *tables, datt, dsum, lse)


def _merge_fwd(proj, y_conv, y_att, tm):
    t = proj.shape[0]

    def body(gc_ref, ga_ref, yc_ref, ya_ref, o_ref):
        o_ref[...] = (_sig(gc_ref[...]) * yc_ref[...] + _sig(ga_ref[...]) * ya_ref[...]).astype(BF16)

    return pl.pallas_call(body, name="merge_fwd", grid=(t // tm,),
                          in_specs=[_rows(tm, D, C_GC // D), _rows(tm, D, C_GA // D), _rows(tm, D), _rows(tm, D)],
                          out_specs=_rows(tm, D), out_shape=jax.ShapeDtypeStruct((t, D), BF16))(proj, proj, y_conv, y_att)


def _acc_rows(ref, i, val):
    @pl.when(i == 0)
    def _():
        ref[...] = jnp.zeros_like(ref)

    ref[...] += jnp.sum(val, axis=0, keepdims=True)


def _loss_head(x, o, mod, final_g, target, tm):
    t = x.shape[0]

    def body(x_ref, o_ref, mod_ref, fg_ref, tg_ref, dout_ref, do_ref, sq_ref, gfg_ref, dgate_ref):
        i = pl.program_id(0)
        gate = mod_ref[:, 2 * D:3 * D]
        ov = o_ref[...]
        out = x_ref[...] + gate * ov
        r = lax.rsqrt(jnp.mean(out * out, axis=-1, keepdims=True) + EPS)
        yn = out * r
        diff = yn * fg_ref[...] - tg_ref[...]
        dy = diff * (1.0 / D)
        gy = dy * fg_ref[...]
        dout = r * (gy - yn * jnp.mean(gy * yn, axis=-1, keepdims=True))
        dout_ref[...] = dout
        do_ref[...] = (dout * gate).astype(BF16)
        _acc_rows(sq_ref, i, diff * diff)
        _acc_rows(gfg_ref, i, dy * yn)
        _acc_rows(dgate_ref, i, dout * ov)

    vec = _full((1, D))
    return pl.pallas_call(
        body, name="loss_head", grid=(t // tm,),
        in_specs=[_rows(tm, D), _rows(tm, D), _full((1, 3 * D)), vec, _rows(tm, D)],
        out_specs=[_rows(tm, D), _rows(tm, D), vec, vec, vec],
        out_shape=[jax.ShapeDtypeStruct((t, D), F32), jax.ShapeDtypeStruct((t, D), BF16)] + [jax.ShapeDtypeStruct((1, D), F32)] * 3,
    )(x, o, mod, final_g, target)


def _merge_bwd(dmerged, proj, y_conv, y_att, tm):
    t = proj.shape[0]

    def body(dm_ref, gc_ref, ga_ref, yc_ref, ya_ref, dyc_ref, dya_ref, dp_ref):
        dm = dm_ref[...]
        sc, sa = _sig(gc_ref[...]), _sig(ga_ref[...])
        dyc_ref[...] = (dm * sc).astype(BF16)
        dya_ref[...] = (dm * sa).astype(BF16)
        dp_ref[:, 0:D] = (dm * yc_ref[...] * sc * (1.0 - sc)).astype(BF16)
        dp_ref[:, D:2 * D] = (dm * ya_ref[...] * sa * (1.0 - sa)).astype(BF16)

    return pl.pallas_call(
        body, name="merge_bwd", grid=(t // tm,),
        in_specs=[_rows(tm, D), _rows(tm, D, C_GC // D), _rows(tm, D, C_GA // D), _rows(tm, D), _rows(tm, D)],
        out_specs=[_rows(tm, D), _rows(tm, D), _rows(tm, 2 * D, C_GC // (2 * D))],
        out_shape=[jax.ShapeDtypeStruct((t, D), BF16), jax.ShapeDtypeStruct((t, D), BF16), jax.ShapeDtypeStruct((t, N_COL), BF16)],
    )(dmerged, proj, proj, y_conv, y_att)


def _att_pre_bwd(dpa, proj, att, dproj, tm):
    t = proj.shape[0]

    def body(dpa_ref, z_ref, att_ref, dp_in, datt_ref, ds_ref, dp_ref):
        del dp_in
        z, dpa_v, att_v = z_ref[...], dpa_ref[...], att_ref[...]
        s = _sig(z)
        datt = dpa_v * (z * s)
        datt_ref[...] = datt
        dp_ref[...] = (dpa_v * att_v * _dsilu(z, s)).astype(BF16)
        prod = datt * att_v
        for h in range(ATT // HEAD):
            sl = slice(h * HEAD, (h + 1) * HEAD)
            ds_ref[:, sl] = jnp.broadcast_to(jnp.sum(prod[:, sl], axis=-1, keepdims=True), (tm, HEAD))

    return pl.pallas_call(
        body, name="att_pre_bwd", grid=(t // tm,),
        in_specs=[_rows(tm, ATT), _rows(tm, ATT, C_ZA // ATT), _rows(tm, ATT), ANY],
        out_specs=[_rows(tm, ATT), _rows(tm, ATT), _rows(tm, ATT, C_ZA // ATT)],
        out_shape=[jax.ShapeDtypeStruct((t, ATT), F32), jax.ShapeDtypeStruct((t, ATT), F32), jax.ShapeDtypeStruct((t, N_COL), BF16)],
        input_output_aliases={3: 2},
    )(dpa, proj, att, dproj)


def _place_qkv(parts, dproj, col_block, tm, name):
    t = dproj.shape[0]

    def body(p0, p1, p2, dp_in, dp_ref):
        del dp_in
        for g, ref in enumerate((p0, p1, p2)):
            dp_ref[:, g * ATT:(g + 1) * ATT] = ref[...]

    return pl.pallas_call(
        body, name=name, grid=(t // tm,),
        in_specs=[_rows(tm, ATT)] * 3 + [ANY],
        out_specs=_rows(tm, QKV, col_block), out_shape=jax.ShapeDtypeStruct((t, N_COL), BF16),
        input_output_aliases={3: 0},
    )(*parts, dproj)


def _conv_bwd_rows(dpc, proj, u1, ln_g, ln_b, dproj, tm):
    t = proj.shape[0]

    def body(dpc_ref, z_ref, u1_ref, lg_ref, lb_ref, dp_in, du1_ref, dp_ref, dlg_ref, dlb_ref, dcb_ref):
        del dp_in
        i = pl.program_id(0)
        u1v = u1_ref[...]
        mu = jnp.mean(u1v, axis=-1, keepdims=True)
        xc = u1v - mu
        r = lax.rsqrt(jnp.mean(xc * xc, axis=-1, keepdims=True) + EPS)
        uhat = xc * r
        u2 = uhat * lg_ref[...] + lb_ref[...]
        s2 = _sig(u2)
        z = z_ref[...]
        sz = _sig(z)
        dpc_v = dpc_ref[...]
        dp_ref[...] = (dpc_v * (u2 * s2) * _dsilu(z, sz)).astype(BF16)
        du2 = dpc_v * (z * sz) * _dsilu(u2, s2)
        duhat = du2 * lg_ref[...]
        du1 = r * (duhat - jnp.mean(duhat, axis=-1, keepdims=True) - uhat * jnp.mean(duhat * uhat, axis=-1, keepdims=True))
        du1_ref[...] = du1
        _acc_rows(dlg_ref, i, du2 * uhat)
        _acc_rows(dlb_ref, i, du2)
        _acc_rows(dcb_ref, i, du1)

    vec = _full((1, D))
    return pl.pallas_call(
        body, name="conv_bwd_rows", grid=(t // tm,),
        in_specs=[_rows(tm, D), _rows(tm, D, C_ZC // D), _rows(tm, D), vec, vec, ANY],
        out_specs=[_rows(tm, D), _rows(tm, D, C_ZC // D), vec, vec, vec],
        out_shape=[jax.ShapeDtypeStruct((t, D), F32), jax.ShapeDtypeStruct((t, N_COL), BF16)] + [jax.ShapeDtypeStruct((1, D), F32)] * 3,
        input_output_aliases={5: 1},
    )(dpc, proj, u1, ln_g, ln_b, dproj)


def _conv_bwd_taps(du1, proj, conv_w, dproj, tm):
    t = proj.shape[0]
    hb = tm // HALO
    last = t // HALO - 1

    def body(du_ref, duh_ref, a_ref, b_ref, ah_ref, bh_ref, w_ref, dp_in, dp_ref, dw_ref, dbuf, ubuf, g0, shd, shu):
        del dp_in
        i = pl.program_id(0)
        a, sb = a_ref[...], _sig(b_ref[...])
        ubuf[0:HALO, :] = jnp.where(i > 0, ah_ref[...] * _sig(bh_ref[...]), 0.0)
        ubuf[HALO:HALO + tm, :] = a * sb
        dbuf[0:tm, :] = du_ref[...]
        dbuf[tm:tm + HALO, :] = jnp.where(i < pl.num_programs(0) - 1, duh_ref[...], 0.0)

        @pl.when(i == 0)
        def _():
            dw_ref[...] = jnp.zeros_like(dw_ref)

        def col(ci, carry):
            c0 = pl.multiple_of(ci * 128, 128)
            _shift_copies(shd, dbuf, c0)
            _shift_copies(shu, ubuf, c0)
            for rc in range(tm // 64):
                g0[rc * 64:(rc + 1) * 64, pl.ds(c0, 128)] = _conv_taps(
                    jnp.zeros((64, 128), F32), w_ref, dbuf, shd, rc * 64, c0, lambda j: CONV_K - 1 - j)
            for j in range(CONV_K):
                part = jnp.zeros((8, 128), F32)
                for rc in range(tm // 64):
                    off = rc * 64 + HALO - (CONV_K - 1) + j
                    prod = dbuf[rc * 64:(rc + 1) * 64, pl.ds(c0, 128)] * _window64(ubuf, shu, c0, off)
                    part = part + jnp.sum(prod.reshape(8, 8, 128), axis=0)
                dw_ref[j:j + 1, pl.ds(c0, 128)] += jnp.sum(part, axis=0, keepdims=True)
            return carry

        lax.fori_loop(0, D // 128, col, 0)
        du0 = g0[...]
        dp_ref[:, 0:D] = (du0 * sb).astype(BF16)
        dp_ref[:, D:2 * D] = (du0 * a * sb * (1.0 - sb)).astype(BF16)

    prev = lambda col: pl.BlockSpec((HALO, D), lambda i: (jnp.maximum(i * hb - 1, 0), col))
    nxt = pl.BlockSpec((HALO, D), lambda i: (jnp.minimum((i + 1) * hb, last), 0))
    return pl.pallas_call(
        body, name="conv_bwd_taps", grid=(t // tm,),
        in_specs=[_rows(tm, D), nxt, _rows(tm, D, 0), _rows(tm, D, 1), prev(0), prev(1), _full((CONV_KP, D)), ANY],
        out_specs=[_rows(tm, 2 * D, 0), _full((CONV_KP, D))],
        out_shape=[jax.ShapeDtypeStruct((t, N_COL), BF16), jax.ShapeDtypeStruct((CONV_KP, D), F32)],
        scratch_shapes=[pltpu.VMEM((tm + HALO, D), F32), pltpu.VMEM((HALO + tm, D), F32), pltpu.VMEM((tm, D), F32),
                        pltpu.VMEM((8, HALO + tm, 128), F32), pltpu.VMEM((8, HALO + tm, 128), F32)],
        input_output_aliases={7: 0},
    )(du1, du1, proj, proj, proj, proj, conv_w, dproj)


def _prenorm_bwd(dh, x, dout, mod, norm_g, tm):
    t = x.shape[0]

    def body(dh_ref, x_ref, dout_ref, mod_ref, g_ref, gx_ref, dshift_ref, dscale_ref, dg_ref):
        i = pl.program_id(0)
        xv, dhv = x_ref[...], dh_ref[...]
        r = lax.rsqrt(jnp.mean(xv * xv, axis=-1, keepdims=True) + EPS)
        xn = xv * r
        one_scale = 1.0 + mod_ref[:, D:2 * D]
        dxn = dhv * (g_ref[...] * one_scale)
        gx_ref[...] = r * (dxn - xn * jnp.mean(dxn * xn, axis=-1, keepdims=True)) + dout_ref[...]
        _acc_rows(dshift_ref, i, dhv)
        _acc_rows(dscale_ref, i, dhv * xn * g_ref[...])
        _acc_rows(dg_ref, i, dhv * xn * one_scale)

    vec = _full((1, D))
    return pl.pallas_call(
        body, name="prenorm_bwd", grid=(t // tm,),
        in_specs=[_rows(tm, D), _rows(tm, D), _rows(tm, D), _full((1, 3 * D)), vec],
        out_specs=[_rows(tm, D), vec, vec, vec],
        out_shape=[jax.ShapeDtypeStruct((t, D), F32)] + [jax.ShapeDtypeStruct((1, D), F32)] * 3,
    )(dh, x, dout, mod, norm_g)


def _sum_devices(gathered):
    w = gathered.shape[-1]

    def body(g_ref, o_ref):
        acc = g_ref[0]
        for j in range(1, N_DEV):
            acc = acc + g_ref[j]
        o_ref[...] = acc

    return pl.pallas_call(body, name="sum_devices", grid=(1,), in_specs=[_full(gathered.shape)], out_specs=_full((1, w)),
                          out_shape=jax.ShapeDtypeStruct((1, w), F32))(gathered)


def _rope_tables(positions):
    half = HEAD // 8
    t = positions.shape[-1]
    inv_freq = ROPE_THETA ** (-(jnp.arange(half, dtype=F32) * 2.0 / (2 * half)))
    ang = positions.reshape(t, 1).astype(F32) * inv_freq
    cos, sin = jnp.cos(ang), jnp.sin(ang)
    zeros = lambda n: jnp.zeros((t, n), F32)
    c64 = jnp.concatenate([cos, cos, jnp.ones((t, HEAD - 2 * half), F32)], axis=1)
    lo64 = jnp.concatenate([-sin, zeros(HEAD - half)], axis=1)
    hi64 = jnp.concatenate([zeros(half), sin, zeros(HEAD - 2 * half)], axis=1)
    return tuple(jnp.tile(a, (1, 2)) for a in (c64, lo64, hi64))


def kernel(x, c, positions, norm_g, w_ada, b_ada, w_in, conv_w, conv_b, conv_ln_g, conv_ln_b, w_conv_out, w_att_out, w_o, final_g, loss_target, m_norm_g, m_w_ada, m_b_ada, m_w_in, m_conv_w, m_conv_b, m_conv_ln_g, m_conv_ln_b, m_w_conv_out, m_w_att_out, m_w_o, m_final_g, v_norm_g, v_w_ada, v_b_ada, v_w_in, v_conv_w, v_conv_b, v_conv_ln_g, v_conv_ln_b, v_w_conv_out, v_w_att_out, v_w_o, v_final_g):
    me = 4 * lax.axis_index("x") + 2 * lax.axis_index("y") + lax.axis_index("c")
    x2, tgt = x[0], loss_target[0]
    t = x2.shape[0]
    te = 512 if t % 512 == 0 else 256
    tcv = 256
    tmm = 1024 if t % 1024 == 0 else 256
    n_ada = w_ada.shape[-1]

    pad_taps = lambda a: jnp.pad(a[0], ((0, CONV_KP - CONV_K), (0, 0)))
    shards = (_cast_bf16(w_in[0], "cast_w_in"), _cast_bf16(w_conv_out[0], "cast_w_conv_out"),
              _cast_bf16(w_att_out[0], "cast_w_att_out"), _cast_bf16(w_o[0], "cast_w_o"), pad_taps(conv_w))
    block_of = lambda relations: jnp.bitwise_xor(me, jnp.array(relations, jnp.int32))

    c_all = _allgather_small(c, "gather_c").reshape(N_DEV, D)
    b_ada_l = lax.dynamic_slice(b_ada, (0, me * n_ada), (1, n_ada))
    parts = _allgather_small(_mod_part(c_all, w_ada[0], b_ada_l), "gather_mod")
    mod = lax.dynamic_slice(parts, (0, me, 0), (N_DEV, 1, n_ada)).reshape(1, N_DEV * n_ada)

    h = _prenorm(x2, mod, norm_g, te)
    proj, w_in_f, w_co_f, w_ao_f, w_o_f, conv_w_f = _proj_gather(h, shards, block_of(GATHER_ORDER), tmm)
    u1, pc = _conv_fwd(proj, conv_w_f, conv_b, conv_ln_g, conv_ln_b, tcv)
    tables = _rope_tables(positions)
    parts_att = []
    for gi, dil in GROUPS:
        parts_att += _att_fwd(proj, tables, gi, dil)
    att, lse, pa = _att_combine(parts_att, proj, te)
    y_conv = _matmul(pc, w_co_f, tm=tmm, tn=D, tk=D, name="y_conv")
    y_att = _matmul(pa, w_ao_f, tm=tmm, tn=D, tk=ATT, name="y_att")
    merged = _merge_fwd(proj, y_conv, y_att, te)
    o = _matmul(merged, w_o_f, tm=tmm, tn=D, tk=D, name="out_proj")
    dout, do, sq_sum, g_final, d_gate = _loss_head(x2, o, mod, final_g.reshape(1, D), tgt, te)

    dmerged = _matmul(do, w_o_f, tb=True, tm=tmm, tn=D, tk=D, name="d_merged")
    dw_o = _matmul(merged, do, ta=True, out_dtype=BF16, tm=D, tn=D, tk=512, name="dw_o")
    dyc, dya, dproj = _merge_bwd(dmerged, proj, y_conv, y_att, te)
    dpc = _matmul(dyc, w_co_f, tb=True, tm=tmm, tn=D, tk=D, name="d_pc")
    dw_co = _matmul(pc, dyc, ta=True, out_dtype=BF16, tm=D, tn=D, tk=512, name="dw_conv_out")
    dpa = _matmul(dya, w_ao_f, tb=True, tm=tmm, tn=ATT, tk=D, name="d_pa")
    dw_ao = _matmul(pa, dya, ta=True, out_dtype=BF16, tm=ATT, tn=D, tk=512, name="dw_att_out")
    datt, dsum, dproj = _att_pre_bwd(dpa, proj, att, dproj, te)
    dqs, dks, dvs = [], [], []
    for gi, dil in GROUPS:
        dq, dk, dv = _att_bwd(proj, tables, datt, dsum, lse, gi, dil)
        dqs.append(dq), dks.append(dk), dvs.append(dv)
    dproj = _place_qkv(dqs, dproj, C_Q // QKV, te, "place_dq")
    dproj = _place_qkv(dks, dproj, C_K // QKV, te, "place_dk")
    dproj = _place_qkv(dvs, dproj, C_V // QKV, te, "place_dv")
    du1, dproj, d_ln_g, d_ln_b, d_conv_b = _conv_bwd_rows(dpc, proj, u1, conv_ln_g, conv_ln_b, dproj, te)
    dproj, dconv_w = _conv_bwd_taps(du1, proj, conv_w_f, dproj, tcv)
    dh = _matmul(dproj, w_in_f, tb=True, tm=tmm, tn=D, tk=1024, name="d_h")
    grad_x, d_shift, d_scale, d_norm_g = _prenorm_bwd(dh, x2, dout, mod, norm_g, te)

    packed = jnp.concatenate([d_shift, d_scale, d_gate, d_norm_g, d_conv_b, d_ln_g, d_ln_b, g_final, sq_sum], axis=1)
    gathered = _allgather_small(packed, "gather_partials")
    total = _sum_devices(gathered)
    seg = lambda k, n=1: total[:, k * D:(k + n) * D]
    g_b_ada, g_norm_g, g_conv_b, g_ln_g, g_ln_b, g_final_g = seg(0, 3), seg(3), seg(4), seg(5), seg(6), seg(7)
    loss = (0.5 / D) * jnp.sum(seg(8))
    dmod_all = gathered[:, 0, 0:3 * D]
    dmod_cols = lax.dynamic_slice(dmod_all, (0, me * n_ada), (N_DEV, n_ada))
    g_w_ada, d_w_ada, nm_w_ada, nv_w_ada = _w_ada_update(c_all.T, dmod_cols, w_ada[0], m_w_ada[0], v_w_ada[0])

    small = {}
    for name, g, w, m, v in (("norm_g", g_norm_g, norm_g, m_norm_g, v_norm_g), ("b_ada", g_b_ada, b_ada, m_b_ada, v_b_ada),
                             ("conv_b", g_conv_b, conv_b, m_conv_b, v_conv_b), ("conv_ln_g", g_ln_g, conv_ln_g, m_conv_ln_g, v_conv_ln_g),
                             ("conv_ln_b", g_ln_b, conv_ln_b, m_conv_ln_b, v_conv_ln_b),
                             ("final_g", g_final_g, final_g.reshape(1, D), m_final_g.reshape(1, D), v_final_g.reshape(1, D))):
        small[name] = (g,) + tuple(_adamw_small(g, w, m, v, "adamw_" + name))

    slots = _dw_in_scatter(h, dproj, (dw_co, dw_ao, dw_o, dconv_w), block_of(SCATTER_ORDER), 512)
    big = {
        "w_in": _sum_adamw(slots[0], w_in[0], m_w_in[0], v_w_in[0], 256, "adamw_w_in"),
        "w_conv_out": _sum_adamw(slots[1], w_conv_out[0], m_w_conv_out[0], v_w_conv_out[0], 128, "adamw_w_conv_out"),
        "w_att_out": _sum_adamw(slots[2], w_att_out[0], m_w_att_out[0], v_w_att_out[0], 512, "adamw_w_att_out"),
        "w_o": _sum_adamw(slots[3], w_o[0], m_w_o[0], v_w_o[0], 128, "adamw_w_o"),
        "conv_w": [r[:CONV_K] for r in _sum_adamw(slots[4], pad_taps(conv_w), pad_taps(m_conv_w), pad_taps(v_conv_w), CONV_KP, "adamw_conv_w")],
    }
    big["w_ada"] = (g_w_ada, d_w_ada, nm_w_ada, nv_w_ada)

    order = ("norm_g", "w_ada", "b_ada", "w_in", "conv_w", "conv_b", "conv_ln_g", "conv_ln_b", "w_conv_out", "w_att_out", "w_o", "final_g")
    lead = lambda name, a: a.reshape(D) if name == "final_g" else (a[None] if name in big else a)
    result = {**small, **big}
    outs = [loss, grad_x[None]]
    for field in range(4):
        outs += [lead(name, result[name][field]) for name in order]
    return tuple(outs)
```

```python
import functools

import jax
import jax.numpy as jnp
from jax import lax
from jax.experimental import pallas as pl
from jax.experimental.pallas import tpu as pltpu

F32 = jnp.float32
BF16 = jnp.bfloat16

N_DEV = 8
D = 1024
N_COL = 10240
C_A, C_B, C_ZC, C_Q, C_K, C_V, C_ZA, C_GC, C_GA = 0, 1024, 2048, 3072, 4608, 6144, 7680, 8192, 9216
QKV = 1536
ATT = 512
HEAD = 64
BLK = 128
TILE = 2048
GROUPS = ((0, 1), (1, 4), (2, 16))
CONV_K = 31
CONV_KP = 32
HALO = 32
EPS = 1e-6
NEG_INF = -1e30
ROPE_THETA = 500000.0
SM_SCALE = HEAD ** -0.5

ADAM_LR, ADAM_B1, ADAM_B2, ADAM_EPS, ADAM_WD, ADAM_STEP = 0.001, 0.9, 0.999, 1e-08, 0.01, 10

MESH = pl.DeviceIdType.MESH
ANY = pl.BlockSpec(memory_space=pl.ANY)


def _sig(v):
    return 1.0 / (1.0 + jnp.exp(-v))


def _dsilu(v, s):
    return s * (1.0 + v * (1.0 - s))


def _full(shape):
    return pl.BlockSpec(shape, lambda *_: (0,) * len(shape))


def _rows(tm, width, col=0):
    return pl.BlockSpec((tm, width), lambda i: (i, col))


def _matmul(a, b, *, ta=False, tb=False, out_dtype=F32, tm, tn, tk, name):
    m, k = (a.shape[1], a.shape[0]) if ta else a.shape
    n = b.shape[0] if tb else b.shape[1]
    assert (b.shape[1] if tb else b.shape[0]) == k
    assert m % tm == 0 and n % tn == 0 and k % tk == 0
    nk = k // tk
    dims = (((0 if ta else 1,), (1 if tb else 0,)), ((), ()))
    use_scratch = out_dtype != F32 and nk > 1

    def body(a_ref, b_ref, o_ref, *scratch):
        p = lax.dot_general(a_ref[...], b_ref[...], dims, preferred_element_type=F32)
        if nk == 1:
            o_ref[...] = p.astype(out_dtype)
            return
        acc = scratch[0] if use_scratch else o_ref
        kk = pl.program_id(2)

        @pl.when(kk == 0)
        def _():
            acc[...] = p

        @pl.when(kk > 0)
        def _():
            acc[...] += p

        if use_scratch:
            @pl.when(kk == nk - 1)
            def _():
                o_ref[...] = acc[...].astype(out_dtype)

    a_spec = pl.BlockSpec((tk, tm), lambda i, j, kk: (kk, i)) if ta else pl.BlockSpec((tm, tk), lambda i, j, kk: (i, kk))
    b_spec = pl.BlockSpec((tn, tk), lambda i, j, kk: (j, kk)) if tb else pl.BlockSpec((tk, tn), lambda i, j, kk: (kk, j))
    return pl.pallas_call(
        body, name=name, grid=(m // tm, n // tn, nk),
        in_specs=[a_spec, b_spec],
        out_specs=pl.BlockSpec((tm, tn), lambda i, j, kk: (i, j)),
        out_shape=jax.ShapeDtypeStruct((m, n), out_dtype),
        scratch_shapes=[pltpu.VMEM((tm, tn), F32)] if use_scratch else [],
    )(a, b)


def _me_and_peers():
    x, y, c = lax.axis_index("x"), lax.axis_index("y"), lax.axis_index("c")
    me = 4 * x + 2 * y + c
    peers = []
    for k in range(1, N_DEV):
        px, py, pc = x ^ (k >> 2), y ^ ((k >> 1) & 1), c ^ (k & 1)
        peers.append(((px, py, pc), 4 * px + 2 * py + pc))
    return me, peers


def _allgather_small(v, name):
    r, c = v.shape

    def body(v_ref, out_ref, send_sems, recv_sems):
        me, peers = _me_and_peers()
        out_ref[me] = v_ref[...]
        copies = []
        for k, (dev, _) in enumerate(peers):
            cp = pltpu.make_async_remote_copy(src_ref=v_ref, dst_ref=out_ref.at[me], send_sem=send_sems.at[k],
                                              recv_sem=recv_sems.at[k], device_id=dev, device_id_type=MESH)
            cp.start()
            copies.append(cp)
        for k, (dev, idx) in enumerate(peers):
            pltpu.make_async_remote_copy(src_ref=v_ref, dst_ref=out_ref.at[idx], send_sem=send_sems.at[k],
                                         recv_sem=recv_sems.at[k], device_id=dev, device_id_type=MESH).wait_recv()
        for cp in copies:
            cp.wait_send()

    return pl.pallas_call(
        body, name=name,
        in_specs=[pl.BlockSpec(memory_space=pltpu.VMEM)],
        out_specs=pl.BlockSpec(memory_space=pltpu.VMEM),
        out_shape=jax.ShapeDtypeStruct((N_DEV, r, c), v.dtype),
        scratch_shapes=[pltpu.SemaphoreType.DMA((N_DEV - 1,)), pltpu.SemaphoreType.DMA((N_DEV - 1,))],
    )(v)


def _window(ref, kind, idx, size):
    start = pl.multiple_of(idx * size, size)
    if kind == "rows":
        return ref.at[pl.ds(start, size), :]
    return ref.at[:, pl.ds(start, size)]


_BIG = (("cols", N_COL // N_DEV), ("rows", D // N_DEV), ("cols", D // N_DEV), ("rows", D // N_DEV), ("cols", D // N_DEV))


GATHER_ORDER = (0, 1, 2, 4, 3, 5, 6, 7)
W_IN_DIRECT = (1, 2, 4, 6)
SCATTER_ORDER = (7, 5, 3, 6, 4, 2, 1, 0)
W_IN_SLOT = {0: 0, 1: 1, 2: 2, 4: 3, 6: 4}


def _proj_gather(h, shards, order, tm):
    t = h.shape[0]
    nt = len(shards)
    n_blk = N_COL // N_DEV
    full_shapes = []
    for s, (kind, size) in zip(shards, _BIG):
        full_shapes.append(jax.ShapeDtypeStruct((s.shape[0] * N_DEV, s.shape[1]) if kind == "rows"
                                                else (s.shape[0], s.shape[1] * N_DEV), s.dtype))
    last = (N_DEV - 1, t // tm - 1)

    def body(order_ref, h_ref, *refs):
        src, proj_ref, dst = refs[:nt], refs[nt], refs[nt + 1:2 * nt + 1]
        wbuf, send_sems, recv_sems, local_sems, load_sem = refs[2 * nt + 1:]
        j, i = pl.program_id(0), pl.program_id(1)
        me, peers = _me_and_peers()

        def local(tn):
            kind, size = _BIG[tn]
            return pltpu.make_async_copy(src[tn], _window(dst[tn], kind, me, size), local_sems.at[tn])

        def remote(tn, k, block_of):
            kind, size = _BIG[tn]
            dev, idx = peers[k - 1]
            return pltpu.make_async_remote_copy(src_ref=src[tn], dst_ref=_window(dst[tn], kind, me if block_of == "mine" else idx, size),
                                                send_sem=send_sems.at[tn, k - 1], recv_sem=recv_sems.at[tn, k - 1],
                                                device_id=dev, device_id_type=MESH)

        def forward(k):
            dev, idx = peers[k - 1]
            block = _window(dst[0], "cols", idx, n_blk)
            return pltpu.make_async_remote_copy(src_ref=block, dst_ref=block, send_sem=send_sems.at[0, k], recv_sem=recv_sems.at[0, k],
                                                device_id=peers[0][0], device_id_type=MESH)

        @pl.when((j == 0) & (i == 0))
        def _():
            for tn in range(nt):
                local(tn).start()
                for k in GATHER_ORDER[1:]:
                    if tn > 0 or k in W_IN_DIRECT:
                        remote(tn, k, "mine").start()

        @pl.when(i == 0)
        def _():
            for step, k in enumerate(GATHER_ORDER):
                @pl.when(j == step)
                def _():
                    if k == 0:
                        local(0).wait()
                    else:
                        remote(0, k, "theirs").wait_recv()
                        if k in W_IN_DIRECT and k > 1:
                            forward(k).start()
            blk = pltpu.make_async_copy(_window(dst[0], "cols", order_ref[j], n_blk), wbuf, load_sem)
            blk.start()
            blk.wait()

        proj_ref[...] = jnp.dot(h_ref[...], wbuf[...], preferred_element_type=F32)

        @pl.when((j == last[0]) & (i == last[1]))
        def _():
            for tn in range(1, nt):
                local(tn).wait()
                for k in range(1, N_DEV):
                    remote(tn, k, "theirs").wait_recv()
            for tn in range(nt):
                for k in range(1, N_DEV):
                    if tn > 0 or k in W_IN_DIRECT:
                        remote(tn, k, "mine").wait_send()
                    else:
                        forward(k - 1).wait_send()

    grid_spec = pltpu.PrefetchScalarGridSpec(
        num_scalar_prefetch=1, grid=(N_DEV, t // tm),
        in_specs=[pl.BlockSpec((tm, D), lambda j, i, order_ref: (i, 0))] + [ANY] * nt,
        out_specs=[pl.BlockSpec((tm, n_blk), lambda j, i, order_ref: (i, order_ref[j]))] + [ANY] * nt,
        scratch_shapes=[pltpu.VMEM((D, n_blk), BF16), pltpu.SemaphoreType.DMA((nt, N_DEV - 1)),
                        pltpu.SemaphoreType.DMA((nt, N_DEV - 1)), pltpu.SemaphoreType.DMA((nt,)), pltpu.SemaphoreType.DMA(())],
    )
    return pl.pallas_call(
        body, name="proj_gather", grid_spec=grid_spec,
        out_shape=[jax.ShapeDtypeStruct((t, N_COL), F32)] + full_shapes,
    )(order, h, *shards)


def _dw_in_scatter(h, dproj, small_grads, order, tk):
    t = h.shape[0]
    nt = 1 + len(small_grads)
    n_blk = N_COL // N_DEV
    nk = t // tk
    slot_shapes = [jax.ShapeDtypeStruct((len(W_IN_SLOT), D, n_blk), BF16)]
    for g, (kind, size) in zip(small_grads, _BIG[1:]):
        slot_shapes.append(jax.ShapeDtypeStruct((N_DEV,) + ((size, g.shape[1]) if kind == "rows" else (g.shape[0], size)), g.dtype))

    def body(order_ref, h_ref, dp_ref, *refs):
        src, dst = refs[:nt - 1], refs[nt - 1:2 * nt - 1]
        acc, stage, partner, send_sems, recv_sems, local_sems, pair_send, pair_recv = refs[2 * nt - 1:]
        j, kk = pl.program_id(0), pl.program_id(1)
        me, peers = _me_and_peers()

        def small_local(tn):
            kind, size = _BIG[tn]
            return pltpu.make_async_copy(_window(src[tn - 1], kind, me, size), dst[tn].at[me], local_sems.at[tn])

        def small_remote(tn, k, mine):
            kind, size = _BIG[tn]
            dev, idx = peers[k - 1]
            return pltpu.make_async_remote_copy(src_ref=_window(src[tn - 1], kind, idx if mine else me, size),
                                                dst_ref=dst[tn].at[me if mine else idx],
                                                send_sem=send_sems.at[tn, k - 1], recv_sem=recv_sems.at[tn, k - 1],
                                                device_id=dev, device_id_type=MESH)

        def push(step):
            k, slot = SCATTER_ORDER[step], step % 2
            if k == 0:
                return pltpu.make_async_copy(stage.at[slot], dst[0].at[W_IN_SLOT[0]], local_sems.at[0])
            if k not in W_IN_SLOT:
                p = (k - 3) // 2
                return pltpu.make_async_remote_copy(src_ref=stage.at[slot], dst_ref=partner.at[p], send_sem=pair_send.at[p],
                                                    recv_sem=pair_recv.at[p], device_id=peers[0][0], device_id_type=MESH)
            return pltpu.make_async_remote_copy(src_ref=stage.at[slot], dst_ref=dst[0].at[W_IN_SLOT[k]],
                                                send_sem=send_sems.at[0, k - 1], recv_sem=recv_sems.at[0, k - 1],
                                                device_id=peers[k - 1][0], device_id_type=MESH)

        @pl.when((j == 0) & (kk == 0))
        def _():
            for tn in range(1, nt):
                small_local(tn).start()
                for k in range(1, N_DEV):
                    small_remote(tn, k, True).start()

        p = lax.dot_general(h_ref[...], dp_ref[...], (((0,), (0,)), ((), ())), preferred_element_type=F32)

        @pl.when(kk == 0)
        def _():
            acc[...] = p

        @pl.when(kk > 0)
        def _():
            acc[...] += p

        @pl.when(kk == nk - 1)
        def _():
            for step, k in enumerate(SCATTER_ORDER):
                @pl.when(j == step)
                def _():
                    if step >= 2:
                        push(step - 2).wait_send()
                    total = acc[...]
                    if k in W_IN_SLOT and k >= 2:
                        p = k // 2 - 1
                        push(SCATTER_ORDER.index(k + 1)).wait_recv()
                        total = total + partner[p].astype(F32)
                    stage[step % 2] = total.astype(BF16)
                    push(step).start()

        @pl.when((j == N_DEV - 1) & (kk == nk - 1))
        def _():
            push(N_DEV - 2).wait_send()
            push(N_DEV - 1).wait()
            for k in (1, 2, 4, 6):
                push(SCATTER_ORDER.index(k)).wait_recv()
            for tn in range(1, nt):
                small_local(tn).wait()
                for k in range(1, N_DEV):
                    small_remote(tn, k, False).wait_recv()
                    small_remote(tn, k, True).wait_send()

    grid_spec = pltpu.PrefetchScalarGridSpec(
        num_scalar_prefetch=1, grid=(N_DEV, nk),
        in_specs=[pl.BlockSpec((tk, D), lambda j, kk, order_ref: (kk, 0)),
                  pl.BlockSpec((tk, n_blk), lambda j, kk, order_ref: (kk, order_ref[j]))] + [ANY] * (nt - 1),
        out_specs=[ANY] * nt,
        scratch_shapes=[pltpu.VMEM((D, n_blk), F32), pltpu.VMEM((2, D, n_blk), BF16), pltpu.VMEM((3, D, n_blk), BF16),
                        pltpu.SemaphoreType.DMA((nt, N_DEV - 1)), pltpu.SemaphoreType.DMA((nt, N_DEV - 1)),
                        pltpu.SemaphoreType.DMA((nt,)), pltpu.SemaphoreType.DMA((3,)), pltpu.SemaphoreType.DMA((3,))],
    )
    return pl.pallas_call(body, name="dw_in_scatter", grid_spec=grid_spec, out_shape=slot_shapes)(order, h, dproj, *small_grads)


def _adamw_math(w, g, m, v):
    m = ADAM_B1 * m + (1.0 - ADAM_B1) * g
    v = ADAM_B2 * v + (1.0 - ADAM_B2) * (g * g)
    m_hat = m / (1.0 - ADAM_B1 ** ADAM_STEP)
    v_hat = v / (1.0 - ADAM_B2 ** ADAM_STEP)
    delta = -ADAM_LR * (m_hat / (jnp.sqrt(v_hat) + ADAM_EPS) + ADAM_WD * w)
    return delta, m, v


def _sum_adamw(slots, w, m, v, tr, name):
    n_slots, r, c = slots.shape
    assert r % tr == 0

    def body(s_ref, w_ref, m_ref, v_ref, g_ref, d_ref, nm_ref, nv_ref):
        g = s_ref[0].astype(F32)
        for j in range(1, n_slots):
            g = g + s_ref[j].astype(F32)
        delta, nm, nv = _adamw_math(w_ref[...], g, m_ref[...], v_ref[...])
        g_ref[...] = g
        d_ref[...] = delta
        nm_ref[...] = nm
        nv_ref[...] = nv

    blk = pl.BlockSpec((tr, c), lambda i: (i, 0))
    return pl.pallas_call(
        body, name=name, grid=(r // tr,),
        in_specs=[pl.BlockSpec((n_slots, tr, c), lambda i: (0, i, 0)), blk, blk, blk],
        out_specs=[blk] * 4, out_shape=[jax.ShapeDtypeStruct((r, c), F32)] * 4,
    )(slots, w, m, v)


def _adamw_small(g, w, m, v, name):
    def body(g_ref, w_ref, m_ref, v_ref, d_ref, nm_ref, nv_ref):
        delta, nm, nv = _adamw_math(w_ref[...], g_ref[...], m_ref[...], v_ref[...])
        d_ref[...] = delta
        nm_ref[...] = nm
        nv_ref[...] = nv

    spec = _full(g.shape)
    return pl.pallas_call(body, name=name, grid=(1,), in_specs=[spec] * 4, out_specs=[spec] * 3,
                          out_shape=[jax.ShapeDtypeStruct(g.shape, F32)] * 3)(g, w, m, v)


def _mod_part(c_all, w_ada_l, b_ada_l):
    n = w_ada_l.shape[1]

    def body(c_ref, w_ref, b_ref, o_ref):
        o_ref[...] = jnp.dot(c_ref[...], w_ref[...], preferred_element_type=F32,
                             precision=lax.Precision.HIGHEST) + b_ref[...]

    return pl.pallas_call(body, name="mod_part", grid=(1,),
                          in_specs=[_full(c_all.shape), _full(w_ada_l.shape), _full(b_ada_l.shape)],
                          out_specs=_full((N_DEV, n)), out_shape=jax.ShapeDtypeStruct((N_DEV, n), F32))(c_all, w_ada_l, b_ada_l)


def _w_ada_update(c_all_t, dmod_cols, w, m, v):
    def body(c_ref, dm_ref, w_ref, m_ref, v_ref, g_ref, d_ref, nm_ref, nv_ref):
        g = c_ref[:, 0:1] * dm_ref[0:1, :]
        for b in range(1, N_DEV):
            g = g + c_ref[:, b:b + 1] * dm_ref[b:b + 1, :]
        delta, nm, nv = _adamw_math(w_ref[...], g, m_ref[...], v_ref[...])
        g_ref[...] = g
        d_ref[...] = delta
        nm_ref[...] = nm
        nv_ref[...] = nv

    spec = _full(w.shape)
    return pl.pallas_call(body, name="w_ada_update", grid=(1,),
                          in_specs=[_full(c_all_t.shape), _full(dmod_cols.shape), spec, spec, spec],
                          out_specs=[spec] * 4, out_shape=[jax.ShapeDtypeStruct(w.shape, F32)] * 4)(c_all_t, dmod_cols, w, m, v)


def _cast_bf16(w, name):
    def body(w_ref, o_ref):
        o_ref[...] = w_ref[...].astype(BF16)

    return pl.pallas_call(body, name=name, grid=(1,), in_specs=[_full(w.shape)], out_specs=_full(w.shape),
                          out_shape=jax.ShapeDtypeStruct(w.shape, BF16))(w)


def _prenorm(x, mod, norm_g, tm):
    t = x.shape[0]

    def body(x_ref, mod_ref, g_ref, h_ref):
        xv = x_ref[...]
        r = lax.rsqrt(jnp.mean(xv * xv, axis=-1, keepdims=True) + EPS)
        h = (xv * r) * g_ref[...] * (1.0 + mod_ref[:, D:2 * D]) + mod_ref[:, 0:D]
        h_ref[...] = h.astype(BF16)

    return pl.pallas_call(body, name="prenorm", grid=(t // tm,),
                          in_specs=[_rows(tm, D), _full((1, 3 * D)), _full((1, D))],
                          out_specs=_rows(tm, D), out_shape=jax.ShapeDtypeStruct((t, D), BF16))(x, mod, norm_g)


def _rope_apply(t, cos, s_lo, s_hi):
    return t * cos + pltpu.roll(t, 120, 1) * s_lo + pltpu.roll(t, 8, 1) * s_hi


def _shift_copies(sh, buf, c0):
    rows = buf.shape[0] - 8
    for s in range(1, 8):
        sh[s, 0:rows, :] = buf[s:s + rows, pl.ds(c0, 128)]


def _window64(buf, sh, c0, start):
    s = start % 8
    if s == 0:
        return buf[start:start + 64, pl.ds(c0, 128)]
    return sh[s, start - s:start - s + 64, :]


def _conv_taps(acc_init, w_ref, buf, sh, row0, c0, offset_of_tap):
    acc = acc_init
    for j in range(CONV_K):
        acc = acc + w_ref[j:j + 1, pl.ds(c0, 128)] * _window64(buf, sh, c0, row0 + offset_of_tap(j))
    return acc


def _conv_fwd(proj, conv_w, conv_b, ln_g, ln_b, tm):
    t = proj.shape[0]
    hb = tm // HALO

    def body(a_ref, b_ref, z_ref, ah_ref, bh_ref, w_ref, cb_ref, lg_ref, lb_ref, u1_ref, pc_ref, ubuf, sh):
        i = pl.program_id(0)
        u0h = ah_ref[...] * _sig(bh_ref[...])
        ubuf[0:HALO, :] = jnp.where(i > 0, u0h, 0.0)
        ubuf[HALO:HALO + tm, :] = a_ref[...] * _sig(b_ref[...])

        def col(ci, carry):
            c0 = pl.multiple_of(ci * 128, 128)
            _shift_copies(sh, ubuf, c0)
            for rc in range(tm // 64):
                init = jnp.zeros((64, 128), F32)
                acc = _conv_taps(init, w_ref, ubuf, sh, rc * 64, c0, lambda j: HALO - (CONV_K - 1) + j)
                u1_ref[rc * 64:(rc + 1) * 64, pl.ds(c0, 128)] = acc + cb_ref[:, pl.ds(c0, 128)]
            return carry

        lax.fori_loop(0, D // 128, col, 0)
        u1 = u1_ref[...]
        mu = jnp.mean(u1, axis=-1, keepdims=True)
        xc = u1 - mu
        var = jnp.mean(xc * xc, axis=-1, keepdims=True)
        u2 = xc * lax.rsqrt(var + EPS) * lg_ref[...] + lb_ref[...]
        z = z_ref[...]
        pc_ref[...] = (u2 * _sig(u2) * (z * _sig(z))).astype(BF16)

    halo = pl.BlockSpec((HALO, D), lambda i: (jnp.maximum(i * hb - 1, 0), 0))
    halo_b = pl.BlockSpec((HALO, D), lambda i: (jnp.maximum(i * hb - 1, 0), 1))
    return pl.pallas_call(
        body, name="conv_fwd", grid=(t // tm,),
        in_specs=[_rows(tm, D, 0), _rows(tm, D, 1), _rows(tm, D, 2), halo, halo_b,
                  _full((CONV_KP, D)), _full((1, D)), _full((1, D)), _full((1, D))],
        out_specs=[_rows(tm, D), _rows(tm, D)],
        out_shape=[jax.ShapeDtypeStruct((t, D), F32), jax.ShapeDtypeStruct((t, D), BF16)],
        scratch_shapes=[pltpu.VMEM((HALO + tm, D), F32), pltpu.VMEM((8, HALO + tm, 128), F32)],
    )(proj, proj, proj, proj, proj, conv_w, conv_b, ln_g, ln_b)


def _band_masks_t(has_prev):
    key = lax.broadcasted_iota(jnp.int32, (BLK, BLK), 0)
    qry = lax.broadcasted_iota(jnp.int32, (BLK, BLK), 1)
    return jnp.logical_and(key >= qry, has_prev), key <= qry


def _head_lanes(pair, hh):
    lane = lax.broadcasted_iota(jnp.int32, pair.shape, 1)
    return jnp.where((lane >= hh * HEAD) & (lane < (hh + 1) * HEAD), pair, jnp.zeros_like(pair))


def _pair_mask(has_prev):
    mask_p, mask_c = _band_masks_t(has_prev)
    both = jnp.concatenate([mask_p, mask_c], axis=0)
    return jnp.concatenate([both, both], axis=1)


def _query_pair(pair):
    return jnp.concatenate([_head_lanes(pair, 0), _head_lanes(pair, 1)], axis=0)


def _key_pair(ref, prev, cur):
    return jnp.concatenate([ref[pl.ds(prev, BLK), :], ref[pl.ds(cur, BLK), :]], axis=0)


def _own_head(both):
    return jnp.concatenate([both[0:HEAD, 0:BLK], both[HEAD:2 * HEAD, BLK:2 * BLK]], axis=0)


def _store_transposed(dst, base, src):
    for j in range(TILE // BLK):
        dst[base // BLK + j] = src[j * BLK:(j + 1) * BLK, :].T.astype(BF16)


class _Dilated:
    def __init__(self, dil):
        self.dil = dil
        self.per = TILE // dil
        self.nbr = self.per // BLK

    def spread(self, dst, base, src_ref, dtype):
        for r in range(self.dil):
            rows = src_ref[pl.ds(r, self.per, stride=self.dil), :] if self.dil > 1 else src_ref[...]
            dst[pl.ds(pl.multiple_of(base + r * self.per, BLK), self.per), :] = rows.astype(dtype)

    def gather(self, dst_ref, src, base):
        for r in range(self.dil):
            rows = src[pl.ds(pl.multiple_of(base + r * self.per, BLK), self.per), :]
            if self.dil > 1:
                dst_ref[pl.ds(r, self.per, stride=self.dil), :] = rows
            else:
                dst_ref[...] = rows

    def block_rows(self, b, i, cur, prv):
        n = b % self.nbr
        row = pl.multiple_of(b * BLK, BLK)
        has_prev = jnp.logical_or(n > 0, i > 0)
        prev = jnp.where(n > 0, cur + row - BLK, jnp.where(i > 0, prv + row + (self.nbr - 1) * BLK, cur + row))
        return row, pl.multiple_of(prev, BLK), has_prev


def _slots(i):
    return pl.multiple_of((i % 2) * TILE, TILE), pl.multiple_of(((i + 1) % 2) * TILE, TILE)


def _nt(a, b):
    return lax.dot_general(a, b, (((1,), (1,)), ((), ())), preferred_element_type=F32)


def _qkv_specs(gi, clamp_to=None):
    def spec(col0):
        def imap(hp, i):
            return (i if clamp_to is None else jnp.minimum(i, clamp_to), (col0 + gi * ATT) // 128 + hp)
        return pl.BlockSpec((TILE, 128), imap)
    return [spec(C_Q), spec(C_K), spec(C_V)]


def _att_fwd(proj, tables, gi, dil):
    t = proj.shape[0]
    dl = _Dilated(dil)

    def body(q_ref, k_ref, v_ref, c_ref, lo_ref, hi_ref, o_ref, lse_ref, tmp, qd, kd, vt, od, ld):
        i = pl.program_id(1)
        cur, prv = _slots(i)
        cs, lo, hi = c_ref[...], lo_ref[...], hi_ref[...]
        tmp[...] = _rope_apply(q_ref[...], cs, lo, hi) * SM_SCALE
        dl.spread(qd, 0, tmp, BF16)
        tmp[...] = _rope_apply(k_ref[...], cs, lo, hi)
        dl.spread(kd, cur, tmp, BF16)
        dl.spread(tmp, 0, v_ref, F32)
        _store_transposed(vt, cur, tmp)

        def block(b, carry):
            row, prev, has_prev = dl.block_rows(b, i, cur, prv)
            s = jnp.where(_pair_mask(has_prev), _nt(_key_pair(kd, prev, cur + row), _query_pair(qd[pl.ds(row, BLK), :])), NEG_INF)
            mx = jnp.max(s, axis=0, keepdims=True)
            p = jnp.exp(s - mx)
            den = jnp.sum(p, axis=0, keepdims=True)
            v_t = jnp.concatenate([vt[prev // BLK], vt[(cur + row) // BLK]], axis=1)
            acc = jnp.dot(v_t, p.astype(BF16), preferred_element_type=F32) / den
            lse = mx + jnp.log(den)
            od[pl.ds(row, BLK), :] = _own_head(acc).T
            ld[pl.ds(row, BLK), :] = _own_head(jnp.broadcast_to(lse, (2 * HEAD, 2 * BLK))).T
            return carry

        lax.fori_loop(0, TILE // BLK, block, 0, unroll=True)
        dl.gather(o_ref, od, 0)
        dl.gather(lse_ref, ld, 0)

    tab = pl.BlockSpec((TILE, 128), lambda hp, i: (i, 0))
    out_spec = pl.BlockSpec((TILE, 128), lambda hp, i: (i, hp))
    return pl.pallas_call(
        body, name=f"att_fwd_g{gi}", grid=(ATT // 128, t // TILE),
        in_specs=_qkv_specs(gi) + [tab] * 3,
        out_specs=[out_spec] * 2, out_shape=[jax.ShapeDtypeStruct((t, ATT), F32)] * 2,
        scratch_shapes=[pltpu.VMEM((TILE, 128), F32), pltpu.VMEM((TILE, 128), BF16), pltpu.VMEM((2 * TILE, 128), BF16),
                        pltpu.VMEM((2 * TILE // BLK, 128, BLK), BF16), pltpu.VMEM((TILE, 128), F32), pltpu.VMEM((TILE, 128), F32)],
    )(proj, proj, proj, *tables)


def _att_combine(parts, proj, tm):
    t = proj.shape[0]

    def body(o0, l0, o1, l1, o2, l2, z_ref, att_ref, lse_ref, pa_ref):
        m_all = jnp.maximum(jnp.maximum(l0[...], l1[...]), l2[...])
        w0, w1, w2 = jnp.exp(l0[...] - m_all), jnp.exp(l1[...] - m_all), jnp.exp(l2[...] - m_all)
        den = w0 + w1 + w2
        att = (w0 * o0[...] + w1 * o1[...] + w2 * o2[...]) / den
        z = z_ref[...]
        att_ref[...] = att
        lse_ref[...] = m_all + jnp.log(den)
        pa_ref[...] = (att * (z * _sig(z))).astype(BF16)

    spec = _rows(tm, ATT)
    return pl.pallas_call(
        body, name="att_combine", grid=(t // tm,),
        in_specs=[spec] * 6 + [_rows(tm, ATT, C_ZA // ATT)],
        out_specs=[spec] * 3,
        out_shape=[jax.ShapeDtypeStruct((t, ATT), F32)] * 2 + [jax.ShapeDtypeStruct((t, ATT), BF16)],
    )(*parts, proj)


def _att_bwd(proj, tables, datt, dsum, lse, gi, dil):
    t = proj.shape[0]
    nt = t // TILE
    dl = _Dilated(dil)

    def body(q_ref, k_ref, v_ref, c_ref, lo_ref, hi_ref, cl_ref, lol_ref, hil_ref, do_ref, ds_ref, lse_ref,
             dq_ref, dk_ref, dv_ref, tmp, qd, kd, vd, dod, dsd, lsd, dqd, dkd, dvd, kt):
        i = pl.program_id(1)
        cur, prv = _slots(i)

        @pl.when(i < nt)
        def _():
            cs, lo, hi = c_ref[...], lo_ref[...], hi_ref[...]
            tmp[...] = _rope_apply(q_ref[...], cs, lo, hi) * SM_SCALE
            dl.spread(qd, 0, tmp, BF16)
            tmp[...] = _rope_apply(k_ref[...], cs, lo, hi)
            dl.spread(kd, cur, tmp, BF16)
            dl.spread(dqd, 0, tmp, F32)
            _store_transposed(kt, cur, dqd)
            dl.spread(vd, cur, v_ref, BF16)
            dl.spread(dod, 0, do_ref, BF16)
            dl.spread(dsd, 0, ds_ref, F32)
            dl.spread(lsd, 0, lse_ref, F32)
            dkd[pl.ds(cur, TILE), :] = jnp.zeros((TILE, 128), F32)
            dvd[pl.ds(cur, TILE), :] = jnp.zeros((TILE, 128), F32)

            def block(b, carry):
                row, prev, has_prev = dl.block_rows(b, i, cur, prv)
                q_pair, do_pair = _query_pair(qd[pl.ds(row, BLK), :]), _query_pair(dod[pl.ds(row, BLK), :])
                k_pair, v_pair = _key_pair(kd, prev, cur + row), _key_pair(vd, prev, cur + row)
                ds_t, ls_t = dsd[pl.ds(row, BLK), :].T, lsd[pl.ds(row, BLK), :].T
                lse = jnp.concatenate([ls_t[0:1, :], ls_t[HEAD:HEAD + 1, :]], axis=1)
                dsm = jnp.concatenate([ds_t[0:1, :], ds_t[HEAD:HEAD + 1, :]], axis=1)
                p = jnp.exp(jnp.where(_pair_mask(has_prev), _nt(k_pair, q_pair), NEG_INF) - lse)
                ds = (p * (_nt(v_pair, do_pair) - dsm)).astype(BF16)
                k_t = jnp.concatenate([kt[prev // BLK], kt[(cur + row) // BLK]], axis=1)
                dqd[pl.ds(row, BLK), :] = _own_head(jnp.dot(k_t, ds, preferred_element_type=F32)).T * SM_SCALE
                dk = jnp.dot(ds, q_pair, preferred_element_type=F32)
                dv = jnp.dot(p.astype(BF16), do_pair, preferred_element_type=F32)
                dkd[pl.ds(cur + row, BLK), :] += dk[BLK:2 * BLK, :]
                dvd[pl.ds(cur + row, BLK), :] += dv[BLK:2 * BLK, :]
                dkd[pl.ds(prev, BLK), :] += dk[0:BLK, :]
                dvd[pl.ds(prev, BLK), :] += dv[0:BLK, :]
                return carry

            lax.fori_loop(0, TILE // BLK, block, 0, unroll=True)
            dl.gather(tmp, dqd, 0)
            dq_ref[...] = _rope_apply(tmp[...], cs, -lo, -hi).astype(BF16)

        @pl.when(i > 0)
        def _():
            dl.gather(tmp, dkd, prv)
            dk_ref[...] = _rope_apply(tmp[...], cl_ref[...], -lol_ref[...], -hil_ref[...]).astype(BF16)
            dl.gather(tmp, dvd, prv)
            dv_ref[...] = tmp[...].astype(BF16)

    now = lambda col: pl.BlockSpec((TILE, 128), lambda hp, i: (jnp.minimum(i, nt - 1), col(hp)))
    lag = lambda col: pl.BlockSpec((TILE, 128), lambda hp, i: (jnp.maximum(i - 1, 0), col(hp)))
    first, pair = (lambda hp: 0), (lambda hp: hp)
    return pl.pallas_call(
        body, name=f"att_bwd_g{gi}", grid=(ATT // 128, nt + 1),
        in_specs=_qkv_specs(gi, nt - 1) + [now(first)] * 3 + [lag(first)] * 3 + [now(pair)] * 3,
        out_specs=[now(pair), lag(pair), lag(pair)],
        out_shape=[jax.ShapeDtypeStruct((t, ATT), BF16)] * 3,
        scratch_shapes=[pltpu.VMEM((TILE, 128), F32), pltpu.VMEM((TILE, 128), BF16), pltpu.VMEM((2 * TILE, 128), BF16),
                        pltpu.VMEM((2 * TILE, 128), BF16), pltpu.VMEM((TILE, 128), BF16), pltpu.VMEM((TILE, 128), F32),
                        pltpu.VMEM((TILE, 128), F32), pltpu.VMEM((TILE, 128), F32), pltpu.VMEM((2 * TILE, 128), F32),
                        pltpu.VMEM((2 * TILE, 128), F32), pltpu.VMEM((2 * TILE // BLK, 128, BLK), BF16)],
    )(proj, proj, proj, *tables, *tables, datt, dsum, lse)


def _merge_fwd(proj, y_conv, y_att, tm):
    t = proj.shape[0]

    def body(gc_ref, ga_ref, yc_ref, ya_ref, o_ref):
        o_ref[...] = (_sig(gc_ref[...]) * yc_ref[...] + _sig(ga_ref[...]) * ya_ref[...]).astype(BF16)

    return pl.pallas_call(body, name="merge_fwd", grid=(t // tm,),
                          in_specs=[_rows(tm, D, C_GC // D), _rows(tm, D, C_GA // D), _rows(tm, D), _rows(tm, D)],
                          out_specs=_rows(tm, D), out_shape=jax.ShapeDtypeStruct((t, D), BF16))(proj, proj, y_conv, y_att)


def _acc_rows(ref, i, val):
    @pl.when(i == 0)
    def _():
        ref[...] = jnp.zeros_like(ref)

    ref[...] += jnp.sum(val, axis=0, keepdims=True)


def _loss_head(x, o, mod, final_g, target, tm):
    t = x.shape[0]

    def body(x_ref, o_ref, mod_ref, fg_ref, tg_ref, dout_ref, do_ref, sq_ref, gfg_ref, dgate_ref):
        i = pl.program_id(0)
        gate = mod_ref[:, 2 * D:3 * D]
        ov = o_ref[...]
        out = x_ref[...] + gate * ov
        r = lax.rsqrt(jnp.mean(out * out, axis=-1, keepdims=True) + EPS)
        yn = out * r
        diff = yn * fg_ref[...] - tg_ref[...]
        dy = diff * (1.0 / D)
        gy = dy * fg_ref[...]
        dout = r * (gy - yn * jnp.mean(gy * yn, axis=-1, keepdims=True))
        dout_ref[...] = dout
        do_ref[...] = (dout * gate).astype(BF16)
        _acc_rows(sq_ref, i, diff * diff)
        _acc_rows(gfg_ref, i, dy * yn)
        _acc_rows(dgate_ref, i, dout * ov)

    vec = _full((1, D))
    return pl.pallas_call(
        body, name="loss_head", grid=(t // tm,),
        in_specs=[_rows(tm, D), _rows(tm, D), _full((1, 3 * D)), vec, _rows(tm, D)],
        out_specs=[_rows(tm, D), _rows(tm, D), vec, vec, vec],
        out_shape=[jax.ShapeDtypeStruct((t, D), F32), jax.ShapeDtypeStruct((t, D), BF16)] + [jax.ShapeDtypeStruct((1, D), F32)] * 3,
    )(x, o, mod, final_g, target)


def _merge_bwd(dmerged, proj, y_conv, y_att, tm):
    t = proj.shape[0]

    def body(dm_ref, gc_ref, ga_ref, yc_ref, ya_ref, dyc_ref, dya_ref, dp_ref):
        dm = dm_ref[...]
        sc, sa = _sig(gc_ref[...]), _sig(ga_ref[...])
        dyc_ref[...] = (dm * sc).astype(BF16)
        dya_ref[...] = (dm * sa).astype(BF16)
        dp_ref[:, 0:D] = (dm * yc_ref[...] * sc * (1.0 - sc)).astype(BF16)
        dp_ref[:, D:2 * D] = (dm * ya_ref[...] * sa * (1.0 - sa)).astype(BF16)

    return pl.pallas_call(
        body, name="merge_bwd", grid=(t // tm,),
        in_specs=[_rows(tm, D), _rows(tm, D, C_GC // D), _rows(tm, D, C_GA // D), _rows(tm, D), _rows(tm, D)],
        out_specs=[_rows(tm, D), _rows(tm, D), _rows(tm, 2 * D, C_GC // (2 * D))],
        out_shape=[jax.ShapeDtypeStruct((t, D), BF16), jax.ShapeDtypeStruct((t, D), BF16), jax.ShapeDtypeStruct((t, N_COL), BF16)],
    )(dmerged, proj, proj, y_conv, y_att)


def _att_pre_bwd(dpa, proj, att, dproj, tm):
    t = proj.shape[0]

    def body(dpa_ref, z_ref, att_ref, dp_in, datt_ref, ds_ref, dp_ref):
        del dp_in
        z, dpa_v, att_v = z_ref[...], dpa_ref[...], att_ref[...]
        s = _sig(z)
        datt = dpa_v * (z * s)
        datt_ref[...] = datt
        dp_ref[...] = (dpa_v * att_v * _dsilu(z, s)).astype(BF16)
        prod = datt * att_v
        for h in range(ATT // HEAD):
            sl = slice(h * HEAD, (h + 1) * HEAD)
            ds_ref[:, sl] = jnp.broadcast_to(jnp.sum(prod[:, sl], axis=-1, keepdims=True), (tm, HEAD))

    return pl.pallas_call(
        body, name="att_pre_bwd", grid=(t // tm,),
        in_specs=[_rows(tm, ATT), _rows(tm, ATT, C_ZA // ATT), _rows(tm, ATT), ANY],
        out_specs=[_rows(tm, ATT), _rows(tm, ATT), _rows(tm, ATT, C_ZA // ATT)],
        out_shape=[jax.ShapeDtypeStruct((t, ATT), F32), jax.ShapeDtypeStruct((t, ATT), F32), jax.ShapeDtypeStruct((t, N_COL), BF16)],
        input_output_aliases={3: 2},
    )(dpa, proj, att, dproj)


def _place_qkv(parts, dproj, col_block, tm, name):
    t = dproj.shape[0]

    def body(p0, p1, p2, dp_in, dp_ref):
        del dp_in
        for g, ref in enumerate((p0, p1, p2)):
            dp_ref[:, g * ATT:(g + 1) * ATT] = ref[...]

    return pl.pallas_call(
        body, name=name, grid=(t // tm,),
        in_specs=[_rows(tm, ATT)] * 3 + [ANY],
        out_specs=_rows(tm, QKV, col_block), out_shape=jax.ShapeDtypeStruct((t, N_COL), BF16),
        input_output_aliases={3: 0},
    )(*parts, dproj)


def _conv_bwd_rows(dpc, proj, u1, ln_g, ln_b, dproj, tm):
    t = proj.shape[0]

    def body(dpc_ref, z_ref, u1_ref, lg_ref, lb_ref, dp_in, du1_ref, dp_ref, dlg_ref, dlb_ref, dcb_ref):
        del dp_in
        i = pl.program_id(0)
        u1v = u1_ref[...]
        mu = jnp.mean(u1v, axis=-1, keepdims=True)
        xc = u1v - mu
        r = lax.rsqrt(jnp.mean(xc * xc, axis=-1, keepdims=True) + EPS)
        uhat = xc * r
        u2 = uhat * lg_ref[...] + lb_ref[...]
        s2 = _sig(u2)
        z = z_ref[...]
        sz = _sig(z)
        dpc_v = dpc_ref[...]
        dp_ref[...] = (dpc_v * (u2 * s2) * _dsilu(z, sz)).astype(BF16)
        du2 = dpc_v * (z * sz) * _dsilu(u2, s2)
        duhat = du2 * lg_ref[...]
        du1 = r * (duhat - jnp.mean(duhat, axis=-1, keepdims=True) - uhat * jnp.mean(duhat * uhat, axis=-1, keepdims=True))
        du1_ref[...] = du1
        _acc_rows(dlg_ref, i, du2 * uhat)
        _acc_rows(dlb_ref, i, du2)
        _acc_rows(dcb_ref, i, du1)

    vec = _full((1, D))
    return pl.pallas_call(
        body, name="conv_bwd_rows", grid=(t // tm,),
        in_specs=[_rows(tm, D), _rows(tm, D, C_ZC // D), _rows(tm, D), vec, vec, ANY],
        out_specs=[_rows(tm, D), _rows(tm, D, C_ZC // D), vec, vec, vec],
        out_shape=[jax.ShapeDtypeStruct((t, D), F32), jax.ShapeDtypeStruct((t, N_COL), BF16)] + [jax.ShapeDtypeStruct((1, D), F32)] * 3,
        input_output_aliases={5: 1},
    )(dpc, proj, u1, ln_g, ln_b, dproj)


def _conv_bwd_taps(du1, proj, conv_w, dproj, tm):
    t = proj.shape[0]
    hb = tm // HALO
    last = t // HALO - 1

    def body(du_ref, duh_ref, a_ref, b_ref, ah_ref, bh_ref, w_ref, dp_in, dp_ref, dw_ref, dbuf, ubuf, g0, shd, shu):
        del dp_in
        i = pl.program_id(0)
        a, sb = a_ref[...], _sig(b_ref[...])
        ubuf[0:HALO, :] = jnp.where(i > 0, ah_ref[...] * _sig(bh_ref[...]), 0.0)
        ubuf[HALO:HALO + tm, :] = a * sb
        dbuf[0:tm, :] = du_ref[...]
        dbuf[tm:tm + HALO, :] = jnp.where(i < pl.num_programs(0) - 1, duh_ref[...], 0.0)

        @pl.when(i == 0)
        def _():
            dw_ref[...] = jnp.zeros_like(dw_ref)

        def col(ci, carry):
            c0 = pl.multiple_of(ci * 128, 128)
            _shift_copies(shd, dbuf, c0)
            _shift_copies(shu, ubuf, c0)
            for rc in range(tm // 64):
                g0[rc * 64:(rc + 1) * 64, pl.ds(c0, 128)] = _conv_taps(
                    jnp.zeros((64, 128), F32), w_ref, dbuf, shd, rc * 64, c0, lambda j: CONV_K - 1 - j)
            for j in range(CONV_K):
                part = jnp.zeros((8, 128), F32)
                for rc in range(tm // 64):
                    off = rc * 64 + HALO - (CONV_K - 1) + j
                    prod = dbuf[rc * 64:(rc + 1) * 64, pl.ds(c0, 128)] * _window64(ubuf, shu, c0, off)
                    part = part + jnp.sum(prod.reshape(8, 8, 128), axis=0)
                dw_ref[j:j + 1, pl.ds(c0, 128)] += jnp.sum(part, axis=0, keepdims=True)
            return carry

        lax.fori_loop(0, D // 128, col, 0)
        du0 = g0[...]
        dp_ref[:, 0:D] = (du0 * sb).astype(BF16)
        dp_ref[:, D:2 * D] = (du0 * a * sb * (1.0 - sb)).astype(BF16)

    prev = lambda col: pl.BlockSpec((HALO, D), lambda i: (jnp.maximum(i * hb - 1, 0), col))
    nxt = pl.BlockSpec((HALO, D), lambda i: (jnp.minimum((i + 1) * hb, last), 0))
    return pl.pallas_call(
        body, name="conv_bwd_taps", grid=(t // tm,),
        in_specs=[_rows(tm, D), nxt, _rows(tm, D, 0), _rows(tm, D, 1), prev(0), prev(1), _full((CONV_KP, D)), ANY],
        out_specs=[_rows(tm, 2 * D, 0), _full((CONV_KP, D))],
        out_shape=[jax.ShapeDtypeStruct((t, N_COL), BF16), jax.ShapeDtypeStruct((CONV_KP, D), F32)],
        scratch_shapes=[pltpu.VMEM((tm + HALO, D), F32), pltpu.VMEM((HALO + tm, D), F32), pltpu.VMEM((tm, D), F32),
                        pltpu.VMEM((8, HALO + tm, 128), F32), pltpu.VMEM((8, HALO + tm, 128), F32)],
        input_output_aliases={7: 0},
    )(du1, du1, proj, proj, proj, proj, conv_w, dproj)


def _prenorm_bwd(dh, x, dout, mod, norm_g, tm):
    t = x.shape[0]

    def body(dh_ref, x_ref, dout_ref, mod_ref, g_ref, gx_ref, dshift_ref, dscale_ref, dg_ref):
        i = pl.program_id(0)
        xv, dhv = x_ref[...], dh_ref[...]
        r = lax.rsqrt(jnp.mean(xv * xv, axis=-1, keepdims=True) + EPS)
        xn = xv * r
        one_scale = 1.0 + mod_ref[:, D:2 * D]
        dxn = dhv * (g_ref[...] * one_scale)
        gx_ref[...] = r * (dxn - xn * jnp.mean(dxn * xn, axis=-1, keepdims=True)) + dout_ref[...]
        _acc_rows(dshift_ref, i, dhv)
        _acc_rows(dscale_ref, i, dhv * xn * g_ref[...])
        _acc_rows(dg_ref, i, dhv * xn * one_scale)

    vec = _full((1, D))
    return pl.pallas_call(
        body, name="prenorm_bwd", grid=(t // tm,),
        in_specs=[_rows(tm, D), _rows(tm, D), _rows(tm, D), _full((1, 3 * D)), vec],
        out_specs=[_rows(tm, D), vec, vec, vec],
        out_shape=[jax.ShapeDtypeStruct((t, D), F32)] + [jax.ShapeDtypeStruct((1, D), F32)] * 3,
    )(dh, x, dout, mod, norm_g)


def _sum_devices(gathered):
    w = gathered.shape[-1]

    def body(g_ref, o_ref):
        acc = g_ref[0]
        for j in range(1, N_DEV):
            acc = acc + g_ref[j]
        o_ref[...] = acc

    return pl.pallas_call(body, name="sum_devices", grid=(1,), in_specs=[_full(gathered.shape)], out_specs=_full((1, w)),
                          out_shape=jax.ShapeDtypeStruct((1, w), F32))(gathered)


def _rope_tables(positions):
    half = HEAD // 8
    t = positions.shape[-1]
    inv_freq = ROPE_THETA ** (-(jnp.arange(half, dtype=F32) * 2.0 / (2 * half)))
    ang = positions.reshape(t, 1).astype(F32) * inv_freq
    cos, sin = jnp.cos(ang), jnp.sin(ang)
    zeros = lambda n: jnp.zeros((t, n), F32)
    c64 = jnp.concatenate([cos, cos, jnp.ones((t, HEAD - 2 * half), F32)], axis=1)
    lo64 = jnp.concatenate([-sin, zeros(HEAD - half)], axis=1)
    hi64 = jnp.concatenate([zeros(half), sin, zeros(HEAD - 2 * half)], axis=1)
    return tuple(jnp.tile(a, (1, 2)) for a in (c64, lo64, hi64))


def kernel(x, c, positions, norm_g, w_ada, b_ada, w_in, conv_w, conv_b, conv_ln_g, conv_ln_b, w_conv_out, w_att_out, w_o, final_g, loss_target, m_norm_g, m_w_ada, m_b_ada, m_w_in, m_conv_w, m_conv_b, m_conv_ln_g, m_conv_ln_b, m_w_conv_out, m_w_att_out, m_w_o, m_final_g, v_norm_g, v_w_ada, v_b_ada, v_w_in, v_conv_w, v_conv_b, v_conv_ln_g, v_conv_ln_b, v_w_conv_out, v_w_att_out, v_w_o, v_final_g):
    me = 4 * lax.axis_index("x") + 2 * lax.axis_index("y") + lax.axis_index("c")
    x2, tgt = x[0], loss_target[0]
    t = x2.shape[0]
    te = 512 if t % 512 == 0 else 256
    tcv = 256
    tmm = 1024 if t % 1024 == 0 else 256
    n_ada = w_ada.shape[-1]

    pad_taps = lambda a: jnp.pad(a[0], ((0, CONV_KP - CONV_K), (0, 0)))
    shards = (_cast_bf16(w_in[0], "cast_w_in"), _cast_bf16(w_conv_out[0], "cast_w_conv_out"),
              _cast_bf16(w_att_out[0], "cast_w_att_out"), _cast_bf16(w_o[0], "cast_w_o"), pad_taps(conv_w))
    block_of = lambda relations: jnp.bitwise_xor(me, jnp.array(relations, jnp.int32))

    c_all = _allgather_small(c, "gather_c").reshape(N_DEV, D)
    b_ada_l = lax.dynamic_slice(b_ada, (0, me * n_ada), (1, n_ada))
    parts = _allgather_small(_mod_part(c_all, w_ada[0], b_ada_l), "gather_mod")
    mod = lax.dynamic_slice(parts, (0, me, 0), (N_DEV, 1, n_ada)).reshape(1, N_DEV * n_ada)

    h = _prenorm(x2, mod, norm_g, te)
    proj, w_in_f, w_co_f, w_ao_f, w_o_f, conv_w_f = _proj_gather(h, shards, block_of(GATHER_ORDER), tmm)
    u1, pc = _conv_fwd(proj, conv_w_f, conv_b, conv_ln_g, conv_ln_b, tcv)
    tables = _rope_tables(positions)
    parts_att = []
    for gi, dil in GROUPS:
        parts_att += _att_fwd(proj, tables, gi, dil)
    att, lse, pa = _att_combine(parts_att, proj, te)
    y_conv = _matmul(pc, w_co_f, tm=tmm, tn=D, tk=D, name="y_conv")
    y_att = _matmul(pa, w_ao_f, tm=tmm, tn=D, tk=ATT, name="y_att")
    merged = _merge_fwd(proj, y_conv, y_att, te)
    o = _matmul(merged, w_o_f, tm=tmm, tn=D, tk=D, name="out_proj")
    dout, do, sq_sum, g_final, d_gate = _loss_head(x2, o, mod, final_g.reshape(1, D), tgt, te)

    dmerged = _matmul(do, w_o_f, tb=True, tm=tmm, tn=D, tk=D, name="d_merged")
    dw_o = _matmul(merged, do, ta=True, out_dtype=BF16, tm=D, tn=D, tk=512, name="dw_o")
    dyc, dya, dproj = _merge_bwd(dmerged, proj, y_conv, y_att, te)
    dpc = _matmul(dyc, w_co_f, tb=True, tm=tmm, tn=D, tk=D, name="d_pc")
    dw_co = _matmul(pc, dyc, ta=True, out_dtype=BF16, tm=D, tn=D, tk=512, name="dw_conv_out")
    dpa = _matmul(dya, w_ao_f, tb=True, tm=tmm, tn=ATT, tk=D, name="d_pa")
    dw_ao = _matmul(pa, dya, ta=True, out_dtype=BF16, tm=ATT, tn=D, tk=512, name="dw_att_out")
    datt, dsum, dproj = _att_pre_bwd(dpa, proj, att, dproj, te)
    dqs, dks, dvs = [], [], []
    for gi, dil in GROUPS:
        dq, dk, dv = _att_bwd(proj, tables, datt, dsum, lse, gi, dil)
        dqs.append(dq), dks.append(dk), dvs.append(dv)
    dproj = _place_qkv(dqs, dproj, C_Q // QKV, te, "place_dq")
    dproj = _place_qkv(dks, dproj, C_K // QKV, te, "place_dk")
    dproj = _place_qkv(dvs, dproj, C_V // QKV, te, "place_dv")
    du1, dproj, d_ln_g, d_ln_b, d_conv_b = _conv_bwd_rows(dpc, proj, u1, conv_ln_g, conv_ln_b, dproj, te)
    dproj, dconv_w = _conv_bwd_taps(du1, proj, conv_w_f, dproj, tcv)
    dh = _matmul(dproj, w_in_f, tb=True, tm=tmm, tn=D, tk=1024, name="d_h")
    grad_x, d_shift, d_scale, d_norm_g = _prenorm_bwd(dh, x2, dout, mod, norm_g, te)

    packed = jnp.concatenate([d_shift, d_scale, d_gate, d_norm_g, d_conv_b, d_ln_g, d_ln_b, g_final, sq_sum], axis=1)
    gathered = _allgather_small(packed, "gather_partials")
    total = _sum_devices(gathered)
    seg = lambda k, n=1: total[:, k * D:(k + n) * D]
    g_b_ada, g_norm_g, g_conv_b, g_ln_g, g_ln_b, g_final_g = seg(0, 3), seg(3), seg(4), seg(5), seg(6), seg(7)
    loss = (0.5 / D) * jnp.sum(seg(8))
    dmod_all = gathered[:, 0, 0:3 * D]
    dmod_cols = lax.dynamic_slice(dmod_all, (0, me * n_ada), (N_DEV, n_ada))
    g_w_ada, d_w_ada, nm_w_ada, nv_w_ada = _w_ada_update(c_all.T, dmod_cols, w_ada[0], m_w_ada[0], v_w_ada[0])

    small = {}
    for name, g, w, m, v in (("norm_g", g_norm_g, norm_g, m_norm_g, v_norm_g), ("b_ada", g_b_ada, b_ada, m_b_ada, v_b_ada),
                             ("conv_b", g_conv_b, conv_b, m_conv_b, v_conv_b), ("conv_ln_g", g_ln_g, conv_ln_g, m_conv_ln_g, v_conv_ln_g),
                             ("conv_ln_b", g_ln_b, conv_ln_b, m_conv_ln_b, v_conv_ln_b),
                             ("final_g", g_final_g, final_g.reshape(1, D), m_final_g.reshape(1, D), v_final_g.reshape(1, D))):
        small[name] = (g,) + tuple(_adamw_small(g, w, m, v, "adamw_" + name))

    slots = _dw_in_scatter(h, dproj, (dw_co, dw_ao, dw_o, dconv_w), block_of(SCATTER_ORDER), 512)
    big = {
        "w_in": _sum_adamw(slots[0], w_in[0], m_w_in[0], v_w_in[0], 256, "adamw_w_in"),
        "w_conv_out": _sum_adamw(slots[1], w_conv_out[0], m_w_conv_out[0], v_w_conv_out[0], 128, "adamw_w_conv_out"),
        "w_att_out": _sum_adamw(slots[2], w_att_out[0], m_w_att_out[0], v_w_att_out[0], 512, "adamw_w_att_out"),
        "w_o": _sum_adamw(slots[3], w_o[0], m_w_o[0], v_w_o[0], 128, "adamw_w_o"),
        "conv_w": [r[:CONV_K] for r in _sum_adamw(slots[4], pad_taps(conv_w), pad_taps(m_conv_w), pad_taps(v_conv_w), CONV_KP, "adamw_conv_w")],
    }
    big["w_ada"] = (g_w_ada, d_w_ada, nm_w_ada, nv_w_ada)

    order = ("norm_g", "w_ada", "b_ada", "w_in", "conv_w", "conv_b", "conv_ln_g", "conv_ln_b", "w_conv_out", "w_att_out", "w_o", "final_g")
    lead = lambda name, a: a.reshape(D) if name == "final_g" else (a[None] if name in big else a)
    result = {**small, **big}
    outs = [loss, grad_x[None]]
    for field in range(4):
        outs += [lead(name, result[name][field]) for name in order]
    return tuple(outs)
```

```python
import functools

import jax
import jax.numpy as jnp
from jax import lax
from jax.experimental import pallas as pl
from jax.experimental.pallas import tpu as pltpu

F32 = jnp.float32
BF16 = jnp.bfloat16

N_DEV = 8
D = 1024
N_COL = 10240
C_A, C_B, C_ZC, C_Q, C_K, C_V, C_ZA, C_GC, C_GA = 0, 1024, 2048, 3072, 4608, 6144, 7680, 8192, 9216
QKV = 1536
ATT = 512
HEAD = 64
BLK = 128
TILE = 2048
GROUPS = ((0, 1), (1, 4), (2, 16))
CONV_K = 31
CONV_KP = 32
HALO = 32
EPS = 1e-6
NEG_INF = -1e30
ROPE_THETA = 500000.0
SM_SCALE = HEAD ** -0.5

ADAM_LR, ADAM_B1, ADAM_B2, ADAM_EPS, ADAM_WD, ADAM_STEP = 0.001, 0.9, 0.999, 1e-08, 0.01, 10

MESH = pl.DeviceIdType.MESH
ANY = pl.BlockSpec(memory_space=pl.ANY)


def _sig(v):
    return 1.0 / (1.0 + jnp.exp(-v))


def _dsilu(v, s):
    return s * (1.0 + v * (1.0 - s))


def _full(shape):
    return pl.BlockSpec(shape, lambda *_: (0,) * len(shape))


def _rows(tm, width, col=0):
    return pl.BlockSpec((tm, width), lambda i: (i, col))


def _matmul(a, b, *, ta=False, tb=False, out_dtype=F32, tm, tn, tk, name):
    m, k = (a.shape[1], a.shape[0]) if ta else a.shape
    stacked = b.ndim == 3
    if stacked:
        assert tb and b.shape[2] == tk and b.shape[0] * tk == k
        n = b.shape[1]
    else:
        n = b.shape[0] if tb else b.shape[1]
        assert (b.shape[1] if tb else b.shape[0]) == k
    assert m % tm == 0 and n % tn == 0 and k % tk == 0
    nk = k // tk
    dims = (((0 if ta else 1,), (1 if tb else 0,)), ((), ()))
    use_scratch = out_dtype != F32 and nk > 1

    def body(a_ref, b_ref, o_ref, *scratch):
        p = lax.dot_general(a_ref[...], b_ref[...], dims, preferred_element_type=F32)
        if nk == 1:
            o_ref[...] = p.astype(out_dtype)
            return
        acc = scratch[0] if use_scratch else o_ref
        kk = pl.program_id(2)

        @pl.when(kk == 0)
        def _():
            acc[...] = p

        @pl.when(kk > 0)
        def _():
            acc[...] += p

        if use_scratch:
            @pl.when(kk == nk - 1)
            def _():
                o_ref[...] = acc[...].astype(out_dtype)

    a_spec = pl.BlockSpec((tk, tm), lambda i, j, kk: (kk, i)) if ta else pl.BlockSpec((tm, tk), lambda i, j, kk: (i, kk))
    b_spec = pl.BlockSpec((tn, tk), lambda i, j, kk: (j, kk)) if tb else pl.BlockSpec((tk, tn), lambda i, j, kk: (kk, j))
    if stacked:
        b_spec = pl.BlockSpec((None, tn, tk), lambda i, j, kk: (kk, j, 0))
    return pl.pallas_call(
        body, name=name, grid=(m // tm, n // tn, nk),
        in_specs=[a_spec, b_spec],
        out_specs=pl.BlockSpec((tm, tn), lambda i, j, kk: (i, j)),
        out_shape=jax.ShapeDtypeStruct((m, n), out_dtype),
        scratch_shapes=[pltpu.VMEM((tm, tn), F32)] if use_scratch else [],
    )(a, b)


def _me_and_peers():
    x, y, c = lax.axis_index("x"), lax.axis_index("y"), lax.axis_index("c")
    me = 4 * x + 2 * y + c
    peers = []
    for k in range(1, N_DEV):
        px, py, pc = x ^ (k >> 2), y ^ ((k >> 1) & 1), c ^ (k & 1)
        peers.append(((px, py, pc), 4 * px + 2 * py + pc))
    return me, peers


def _allgather_small(v, name):
    r, c = v.shape

    def body(v_ref, out_ref, send_sems, recv_sems):
        me, peers = _me_and_peers()
        out_ref[me] = v_ref[...]
        copies = []
        for k, (dev, _) in enumerate(peers):
            cp = pltpu.make_async_remote_copy(src_ref=v_ref, dst_ref=out_ref.at[me], send_sem=send_sems.at[k],
                                              recv_sem=recv_sems.at[k], device_id=dev, device_id_type=MESH)
            cp.start()
            copies.append(cp)
        for k, (dev, idx) in enumerate(peers):
            pltpu.make_async_remote_copy(src_ref=v_ref, dst_ref=out_ref.at[idx], send_sem=send_sems.at[k],
                                         recv_sem=recv_sems.at[k], device_id=dev, device_id_type=MESH).wait_recv()
        for cp in copies:
            cp.wait_send()

    return pl.pallas_call(
        body, name=name,
        in_specs=[pl.BlockSpec(memory_space=pltpu.VMEM)],
        out_specs=pl.BlockSpec(memory_space=pltpu.VMEM),
        out_shape=jax.ShapeDtypeStruct((N_DEV, r, c), v.dtype),
        scratch_shapes=[pltpu.SemaphoreType.DMA((N_DEV - 1,)), pltpu.SemaphoreType.DMA((N_DEV - 1,))],
    )(v)


def _window(ref, kind, idx, size):
    if kind == "block":
        return ref.at[idx]
    start = pl.multiple_of(idx * size, size)
    if kind == "rows":
        return ref.at[pl.ds(start, size), :]
    return ref.at[:, pl.ds(start, size)]


_BIG = (("cols", N_COL // N_DEV), ("rows", D // N_DEV), ("cols", D // N_DEV), ("rows", D // N_DEV), ("cols", D // N_DEV))
_GATHERED = (("block", 1),) + _BIG[1:]


GATHER_ORDER = (0, 1, 2, 4, 3, 5, 6, 7)
W_IN_DIRECT = (1, 2, 4, 6)
SCATTER_ORDER = (7, 5, 3, 6, 4, 2, 1, 0)
W_IN_SLOT = {0: 0, 1: 1, 2: 2, 4: 3, 6: 4}


def _proj_gather(h, shards, order, tm):
    t = h.shape[0]
    nt = len(shards)
    n_blk = N_COL // N_DEV
    full_shapes = []
    for s, (kind, size) in zip(shards, _GATHERED):
        full_shapes.append(jax.ShapeDtypeStruct({"block": (N_DEV,) + s.shape, "rows": (s.shape[0] * N_DEV, s.shape[1]),
                                                 "cols": (s.shape[0], s.shape[1] * N_DEV)}[kind], s.dtype))
    last = (N_DEV - 1, t // tm - 1)

    def body(order_ref, h_ref, *refs):
        src, proj_ref, dst = refs[:nt], refs[nt], refs[nt + 1:2 * nt + 1]
        wbuf, send_sems, recv_sems, local_sems, load_sem = refs[2 * nt + 1:]
        j, i = pl.program_id(0), pl.program_id(1)
        me, peers = _me_and_peers()

        def local(tn):
            kind, size = _GATHERED[tn]
            return pltpu.make_async_copy(src[tn], _window(dst[tn], kind, me, size), local_sems.at[tn])

        def remote(tn, k, block_of):
            kind, size = _GATHERED[tn]
            dev, idx = peers[k - 1]
            return pltpu.make_async_remote_copy(src_ref=src[tn], dst_ref=_window(dst[tn], kind, me if block_of == "mine" else idx, size),
                                                send_sem=send_sems.at[tn, k - 1], recv_sem=recv_sems.at[tn, k - 1],
                                                device_id=dev, device_id_type=MESH)

        def forward(k):
            dev, idx = peers[k - 1]
            block = dst[0].at[idx]
            return pltpu.make_async_remote_copy(src_ref=block, dst_ref=block, send_sem=send_sems.at[0, k], recv_sem=recv_sems.at[0, k],
                                                device_id=peers[0][0], device_id_type=MESH)

        @pl.when((j == 0) & (i == 0))
        def _():
            for tn in range(nt):
                local(tn).start()
                for k in GATHER_ORDER[1:]:
                    if tn > 0 or k in W_IN_DIRECT:
                        remote(tn, k, "mine").start()

        @pl.when(i == 0)
        def _():
            for step, k in enumerate(GATHER_ORDER):
                @pl.when(j == step)
                def _():
                    if k == 0:
                        local(0).wait()
                    else:
                        remote(0, k, "theirs").wait_recv()
                        if k in W_IN_DIRECT and k > 1:
                            forward(k).start()
            blk = pltpu.make_async_copy(dst[0].at[order_ref[j]], wbuf, load_sem)
            blk.start()
            blk.wait()

        proj_ref[...] = jnp.dot(h_ref[...], wbuf[...], preferred_element_type=F32)

        @pl.when((j == last[0]) & (i == last[1]))
        def _():
            for tn in range(1, nt):
                local(tn).wait()
                for k in range(1, N_DEV):
                    remote(tn, k, "theirs").wait_recv()
            for tn in range(nt):
                for k in range(1, N_DEV):
                    if tn > 0 or k in W_IN_DIRECT:
                        remote(tn, k, "mine").wait_send()
                    else:
                        forward(k - 1).wait_send()

    grid_spec = pltpu.PrefetchScalarGridSpec(
        num_scalar_prefetch=1, grid=(N_DEV, t // tm),
        in_specs=[pl.BlockSpec((tm, D), lambda j, i, order_ref: (i, 0))] + [ANY] * nt,
        out_specs=[pl.BlockSpec((tm, n_blk), lambda j, i, order_ref: (i, order_ref[j]))] + [ANY] * nt,
        scratch_shapes=[pltpu.VMEM((D, n_blk), BF16), pltpu.SemaphoreType.DMA((nt, N_DEV - 1)),
                        pltpu.SemaphoreType.DMA((nt, N_DEV - 1)), pltpu.SemaphoreType.DMA((nt,)), pltpu.SemaphoreType.DMA(())],
    )
    return pl.pallas_call(
        body, name="proj_gather", grid_spec=grid_spec,
        out_shape=[jax.ShapeDtypeStruct((t, N_COL), F32)] + full_shapes,
    )(order, h, *shards)


def _dw_in_scatter(h, dproj, small_grads, order, tk):
    t = h.shape[0]
    nt = 1 + len(small_grads)
    n_blk = N_COL // N_DEV
    nk = t // tk
    slot_shapes = [jax.ShapeDtypeStruct((len(W_IN_SLOT), D, n_blk), BF16)]
    for g, (kind, size) in zip(small_grads, _BIG[1:]):
        slot_shapes.append(jax.ShapeDtypeStruct((N_DEV,) + ((size, g.shape[1]) if kind == "rows" else (g.shape[0], size)), g.dtype))

    def body(order_ref, h_ref, dp_ref, *refs):
        src, dst = refs[:nt - 1], refs[nt - 1:2 * nt - 1]
        acc, stage, partner, send_sems, recv_sems, local_sems, pair_send, pair_recv = refs[2 * nt - 1:]
        j, kk = pl.program_id(0), pl.program_id(1)
        me, peers = _me_and_peers()

        def small_local(tn):
            kind, size = _BIG[tn]
            return pltpu.make_async_copy(_window(src[tn - 1], kind, me, size), dst[tn].at[me], local_sems.at[tn])

        def small_remote(tn, k, mine):
            kind, size = _BIG[tn]
            dev, idx = peers[k - 1]
            return pltpu.make_async_remote_copy(src_ref=_window(src[tn - 1], kind, idx if mine else me, size),
                                                dst_ref=dst[tn].at[me if mine else idx],
                                                send_sem=send_sems.at[tn, k - 1], recv_sem=recv_sems.at[tn, k - 1],
                                                device_id=dev, device_id_type=MESH)

        def push(step):
            k, slot = SCATTER_ORDER[step], step % 2
            if k == 0:
                return pltpu.make_async_copy(stage.at[slot], dst[0].at[W_IN_SLOT[0]], local_sems.at[0])
            if k not in W_IN_SLOT:
                p = (k - 3) // 2
                return pltpu.make_async_remote_copy(src_ref=stage.at[slot], dst_ref=partner.at[p], send_sem=pair_send.at[p],
                                                    recv_sem=pair_recv.at[p], device_id=peers[0][0], device_id_type=MESH)
            return pltpu.make_async_remote_copy(src_ref=stage.at[slot], dst_ref=dst[0].at[W_IN_SLOT[k]],
                                                send_sem=send_sems.at[0, k - 1], recv_sem=recv_sems.at[0, k - 1],
                                                device_id=peers[k - 1][0], device_id_type=MESH)

        @pl.when((j == 0) & (kk == 0))
        def _():
            for tn in range(1, nt):
                small_local(tn).start()
                for k in range(1, N_DEV):
                    small_remote(tn, k, True).start()

        p = lax.dot_general(h_ref[...], dp_ref[...], (((0,), (0,)), ((), ())), preferred_element_type=F32)

        @pl.when(kk == 0)
        def _():
            acc[...] = p

        @pl.when(kk > 0)
        def _():
            acc[...] += p

        @pl.when(kk == nk - 1)
        def _():
            for step, k in enumerate(SCATTER_ORDER):
                @pl.when(j == step)
                def _():
                    if step >= 2:
                        push(step - 2).wait_send()
                    total = acc[...]
                    if k in W_IN_SLOT and k >= 2:
                        p = k // 2 - 1
                        push(SCATTER_ORDER.index(k + 1)).wait_recv()
                        total = total + partner[p].astype(F32)
                    stage[step % 2] = total.astype(BF16)
                    push(step).start()

        @pl.when((j == N_DEV - 1) & (kk == nk - 1))
        def _():
            push(N_DEV - 2).wait_send()
            push(N_DEV - 1).wait()
            for k in (1, 2, 4, 6):
                push(SCATTER_ORDER.index(k)).wait_recv()
            for tn in range(1, nt):
                small_local(tn).wait()
                for k in range(1, N_DEV):
                    small_remote(tn, k, False).wait_recv()
                    small_remote(tn, k, True).wait_send()

    grid_spec = pltpu.PrefetchScalarGridSpec(
        num_scalar_prefetch=1, grid=(N_DEV, nk),
        in_specs=[pl.BlockSpec((tk, D), lambda j, kk, order_ref: (kk, 0)),
                  pl.BlockSpec((tk, n_blk), lambda j, kk, order_ref: (kk, order_ref[j]))] + [ANY] * (nt - 1),
        out_specs=[ANY] * nt,
        scratch_shapes=[pltpu.VMEM((D, n_blk), F32), pltpu.VMEM((2, D, n_blk), BF16), pltpu.VMEM((3, D, n_blk), BF16),
                        pltpu.SemaphoreType.DMA((nt, N_DEV - 1)), pltpu.SemaphoreType.DMA((nt, N_DEV - 1)),
                        pltpu.SemaphoreType.DMA((nt,)), pltpu.SemaphoreType.DMA((3,)), pltpu.SemaphoreType.DMA((3,))],
    )
    return pl.pallas_call(body, name="dw_in_scatter", grid_spec=grid_spec, out_shape=slot_shapes)(order, h, dproj, *small_grads)


def _adamw_math(w, g, m, v):
    m = ADAM_B1 * m + (1.0 - ADAM_B1) * g
    v = ADAM_B2 * v + (1.0 - ADAM_B2) * (g * g)
    m_hat = m / (1.0 - ADAM_B1 ** ADAM_STEP)
    v_hat = v / (1.0 - ADAM_B2 ** ADAM_STEP)
    delta = -ADAM_LR * (m_hat / (jnp.sqrt(v_hat) + ADAM_EPS) + ADAM_WD * w)
    return delta, m, v


def _sum_adamw(slots, w, m, v, tr, name):
    n_slots, r, c = slots.shape
    assert r % tr == 0

    def body(s_ref, w_ref, m_ref, v_ref, g_ref, d_ref, nm_ref, nv_ref):
        g = s_ref[0].astype(F32)
        for j in range(1, n_slots):
            g = g + s_ref[j].astype(F32)
        delta, nm, nv = _adamw_math(w_ref[...], g, m_ref[...], v_ref[...])
        g_ref[...] = g
        d_ref[...] = delta
        nm_ref[...] = nm
        nv_ref[...] = nv

    blk = pl.BlockSpec((tr, c), lambda i: (i, 0))
    return pl.pallas_call(
        body, name=name, grid=(r // tr,),
        in_specs=[pl.BlockSpec((n_slots, tr, c), lambda i: (0, i, 0)), blk, blk, blk],
        out_specs=[blk] * 4, out_shape=[jax.ShapeDtypeStruct((r, c), F32)] * 4,
    )(slots, w, m, v)


def _adamw_small(g, w, m, v, name):
    def body(g_ref, w_ref, m_ref, v_ref, d_ref, nm_ref, nv_ref):
        delta, nm, nv = _adamw_math(w_ref[...], g_ref[...], m_ref[...], v_ref[...])
        d_ref[...] = delta
        nm_ref[...] = nm
        nv_ref[...] = nv

    spec = _full(g.shape)
    return pl.pallas_call(body, name=name, grid=(1,), in_specs=[spec] * 4, out_specs=[spec] * 3,
                          out_shape=[jax.ShapeDtypeStruct(g.shape, F32)] * 3)(g, w, m, v)


def _mod_part(c_all, w_ada_l, b_ada_l):
    n = w_ada_l.shape[1]

    def body(c_ref, w_ref, b_ref, o_ref):
        o_ref[...] = jnp.dot(c_ref[...], w_ref[...], preferred_element_type=F32,
                             precision=lax.Precision.HIGHEST) + b_ref[...]

    return pl.pallas_call(body, name="mod_part", grid=(1,),
                          in_specs=[_full(c_all.shape), _full(w_ada_l.shape), _full(b_ada_l.shape)],
                          out_specs=_full((N_DEV, n)), out_shape=jax.ShapeDtypeStruct((N_DEV, n), F32))(c_all, w_ada_l, b_ada_l)


def _w_ada_update(c_all_t, dmod_cols, w, m, v):
    def body(c_ref, dm_ref, w_ref, m_ref, v_ref, g_ref, d_ref, nm_ref, nv_ref):
        g = c_ref[:, 0:1] * dm_ref[0:1, :]
        for b in range(1, N_DEV):
            g = g + c_ref[:, b:b + 1] * dm_ref[b:b + 1, :]
        delta, nm, nv = _adamw_math(w_ref[...], g, m_ref[...], v_ref[...])
        g_ref[...] = g
        d_ref[...] = delta
        nm_ref[...] = nm
        nv_ref[...] = nv

    spec = _full(w.shape)
    return pl.pallas_call(body, name="w_ada_update", grid=(1,),
                          in_specs=[_full(c_all_t.shape), _full(dmod_cols.shape), spec, spec, spec],
                          out_specs=[spec] * 4, out_shape=[jax.ShapeDtypeStruct(w.shape, F32)] * 4)(c_all_t, dmod_cols, w, m, v)


def _cast_bf16(w, name):
    def body(w_ref, o_ref):
        o_ref[...] = w_ref[...].astype(BF16)

    return pl.pallas_call(body, name=name, grid=(1,), in_specs=[_full(w.shape)], out_specs=_full(w.shape),
                          out_shape=jax.ShapeDtypeStruct(w.shape, BF16))(w)


def _prenorm(x, mod, norm_g, tm):
    t = x.shape[0]

    def body(x_ref, mod_ref, g_ref, h_ref):
        xv = x_ref[...]
        r = lax.rsqrt(jnp.mean(xv * xv, axis=-1, keepdims=True) + EPS)
        h = (xv * r) * g_ref[...] * (1.0 + mod_ref[:, D:2 * D]) + mod_ref[:, 0:D]
        h_ref[...] = h.astype(BF16)

    return pl.pallas_call(body, name="prenorm", grid=(t // tm,),
                          in_specs=[_rows(tm, D), _full((1, 3 * D)), _full((1, D))],
                          out_specs=_rows(tm, D), out_shape=jax.ShapeDtypeStruct((t, D), BF16))(x, mod, norm_g)


def _rope_apply(t, cos, s_lo, s_hi):
    return t * cos + pltpu.roll(t, 120, 1) * s_lo + pltpu.roll(t, 8, 1) * s_hi


def _shift_copies(sh, buf, c0):
    rows = buf.shape[0] - 8
    for s in range(1, 8):
        sh[s, 0:rows, :] = buf[s:s + rows, pl.ds(c0, 128)]


def _window64(buf, sh, c0, start):
    s = start % 8
    if s == 0:
        return buf[start:start + 64, pl.ds(c0, 128)]
    return sh[s, start - s:start - s + 64, :]


def _conv_taps(acc_init, w_ref, buf, sh, row0, c0, offset_of_tap):
    acc = acc_init
    for j in range(CONV_K):
        acc = acc + w_ref[j:j + 1, pl.ds(c0, 128)] * _window64(buf, sh, c0, row0 + offset_of_tap(j))
    return acc


def _conv_fwd(proj, conv_w, conv_b, ln_g, ln_b, tm):
    t = proj.shape[0]
    hb = tm // HALO

    def body(a_ref, b_ref, z_ref, ah_ref, bh_ref, w_ref, cb_ref, lg_ref, lb_ref, u1_ref, pc_ref, ubuf, sh):
        i = pl.program_id(0)
        u0h = ah_ref[...] * _sig(bh_ref[...])
        ubuf[0:HALO, :] = jnp.where(i > 0, u0h, 0.0)
        ubuf[HALO:HALO + tm, :] = a_ref[...] * _sig(b_ref[...])

        def col(ci, carry):
            c0 = pl.multiple_of(ci * 128, 128)
            _shift_copies(sh, ubuf, c0)
            for rc in range(tm // 64):
                init = jnp.zeros((64, 128), F32)
                acc = _conv_taps(init, w_ref, ubuf, sh, rc * 64, c0, lambda j: HALO - (CONV_K - 1) + j)
                u1_ref[rc * 64:(rc + 1) * 64, pl.ds(c0, 128)] = acc + cb_ref[:, pl.ds(c0, 128)]
            return carry

        lax.fori_loop(0, D // 128, col, 0)
        u1 = u1_ref[...]
        mu = jnp.mean(u1, axis=-1, keepdims=True)
        xc = u1 - mu
        var = jnp.mean(xc * xc, axis=-1, keepdims=True)
        u2 = xc * lax.rsqrt(var + EPS) * lg_ref[...] + lb_ref[...]
        z = z_ref[...]
        pc_ref[...] = (u2 * _sig(u2) * (z * _sig(z))).astype(BF16)

    halo = pl.BlockSpec((HALO, D), lambda i: (jnp.maximum(i * hb - 1, 0), 0))
    halo_b = pl.BlockSpec((HALO, D), lambda i: (jnp.maximum(i * hb - 1, 0), 1))
    return pl.pallas_call(
        body, name="conv_fwd", grid=(t // tm,),
        in_specs=[_rows(tm, D, 0), _rows(tm, D, 1), _rows(tm, D, 2), halo, halo_b,
                  _full((CONV_KP, D)), _full((1, D)), _full((1, D)), _full((1, D))],
        out_specs=[_rows(tm, D), _rows(tm, D)],
        out_shape=[jax.ShapeDtypeStruct((t, D), F32), jax.ShapeDtypeStruct((t, D), BF16)],
        scratch_shapes=[pltpu.VMEM((HALO + tm, D), F32), pltpu.VMEM((8, HALO + tm, 128), F32)],
    )(proj, proj, proj, proj, proj, conv_w, conv_b, ln_g, ln_b)


def _band_masks_t(has_prev):
    key = lax.broadcasted_iota(jnp.int32, (BLK, BLK), 0)
    qry = lax.broadcasted_iota(jnp.int32, (BLK, BLK), 1)
    return jnp.logical_and(key >= qry, has_prev), key <= qry


def _head_lanes(pair, hh):
    lane = lax.broadcasted_iota(jnp.int32, pair.shape, 1)
    return jnp.where((lane >= hh * HEAD) & (lane < (hh + 1) * HEAD), pair, jnp.zeros_like(pair))


def _pair_mask(has_prev):
    mask_p, mask_c = _band_masks_t(has_prev)
    both = jnp.concatenate([mask_p, mask_c], axis=0)
    return jnp.concatenate([both, both], axis=1)


def _query_pair(pair):
    return jnp.concatenate([_head_lanes(pair, 0), _head_lanes(pair, 1)], axis=0)


def _key_pair(ref, prev, cur):
    return jnp.concatenate([ref[pl.ds(prev, BLK), :], ref[pl.ds(cur, BLK), :]], axis=0)


def _own_head(both):
    return jnp.concatenate([both[0:HEAD, 0:BLK], both[HEAD:2 * HEAD, BLK:2 * BLK]], axis=0)


def _store_transposed(dst, base, src):
    for j in range(TILE // BLK):
        dst[base // BLK + j] = src[j * BLK:(j + 1) * BLK, :].T.astype(BF16)


class _Dilated:
    def __init__(self, dil):
        self.dil = dil
        self.per = TILE // dil
        self.nbr = self.per // BLK

    def spread(self, dst, base, src_ref, dtype):
        for r in range(self.dil):
            rows = src_ref[pl.ds(r, self.per, stride=self.dil), :] if self.dil > 1 else src_ref[...]
            dst[pl.ds(pl.multiple_of(base + r * self.per, BLK), self.per), :] = rows.astype(dtype)

    def gather(self, dst_ref, src, base):
        for r in range(self.dil):
            rows = src[pl.ds(pl.multiple_of(base + r * self.per, BLK), self.per), :]
            if self.dil > 1:
                dst_ref[pl.ds(r, self.per, stride=self.dil), :] = rows
            else:
                dst_ref[...] = rows

    def block_rows(self, b, i, cur, prv):
        n = b % self.nbr
        row = pl.multiple_of(b * BLK, BLK)
        has_prev = jnp.logical_or(n > 0, i > 0)
        prev = jnp.where(n > 0, cur + row - BLK, jnp.where(i > 0, prv + row + (self.nbr - 1) * BLK, cur + row))
        return row, pl.multiple_of(prev, BLK), has_prev


def _slots(i):
    return pl.multiple_of((i % 2) * TILE, TILE), pl.multiple_of(((i + 1) % 2) * TILE, TILE)


def _nt(a, b):
    return lax.dot_general(a, b, (((1,), (1,)), ((), ())), preferred_element_type=F32)


def _qkv_specs(gi, clamp_to=None):
    def spec(col0):
        def imap(hp, i):
            return (i if clamp_to is None else jnp.minimum(i, clamp_to), (col0 + gi * ATT) // 128 + hp)
        return pl.BlockSpec((TILE, 128), imap)
    return [spec(C_Q), spec(C_K), spec(C_V)]


def _att_fwd(proj, tables, gi, dil):
    t = proj.shape[0]
    dl = _Dilated(dil)

    def body(q_ref, k_ref, v_ref, c_ref, lo_ref, hi_ref, o_ref, lse_ref, tmp, qd, kd, vt, od, ld):
        i = pl.program_id(1)
        cur, prv = _slots(i)
        cs, lo, hi = c_ref[...], lo_ref[...], hi_ref[...]
        tmp[...] = _rope_apply(q_ref[...], cs, lo, hi) * SM_SCALE
        dl.spread(qd, 0, tmp, BF16)
        tmp[...] = _rope_apply(k_ref[...], cs, lo, hi)
        dl.spread(kd, cur, tmp, BF16)
        dl.spread(tmp, 0, v_ref, F32)
        _store_transposed(vt, cur, tmp)

        def block(b, carry):
            row, prev, has_prev = dl.block_rows(b, i, cur, prv)
            s = jnp.where(_pair_mask(has_prev), _nt(_key_pair(kd, prev, cur + row), _query_pair(qd[pl.ds(row, BLK), :])), NEG_INF)
            mx = jnp.max(s, axis=0, keepdims=True)
            p = jnp.exp(s - mx)
            den = jnp.sum(p, axis=0, keepdims=True)
            v_t = jnp.concatenate([vt[prev // BLK], vt[(cur + row) // BLK]], axis=1)
            acc = jnp.dot(v_t, p.astype(BF16), preferred_element_type=F32) / den
            lse = mx + jnp.log(den)
            od[pl.ds(row, BLK), :] = _own_head(acc).T
            ld[pl.ds(row, BLK), :] = _own_head(jnp.broadcast_to(lse, (2 * HEAD, 2 * BLK))).T
            return carry

        lax.fori_loop(0, TILE // BLK, block, 0, unroll=True)
        dl.gather(o_ref, od, 0)
        dl.gather(lse_ref, ld, 0)

    tab = pl.BlockSpec((TILE, 128), lambda hp, i: (i, 0))
    out_spec = pl.BlockSpec((TILE, 128), lambda hp, i: (i, hp))
    return pl.pallas_call(
        body, name=f"att_fwd_g{gi}", grid=(ATT // 128, t // TILE),
        in_specs=_qkv_specs(gi) + [tab] * 3,
        out_specs=[out_spec] * 2, out_shape=[jax.ShapeDtypeStruct((t, ATT), F32)] * 2,
        scratch_shapes=[pltpu.VMEM((TILE, 128), F32), pltpu.VMEM((TILE, 128), BF16), pltpu.VMEM((2 * TILE, 128), BF16),
                        pltpu.VMEM((2 * TILE // BLK, 128, BLK), BF16), pltpu.VMEM((TILE, 128), F32), pltpu.VMEM((TILE, 128), F32)],
    )(proj, proj, proj, *tables)


def _att_combine(parts, proj, tm):
    t = proj.shape[0]

    def body(o0, l0, o1, l1, o2, l2, z_ref, att_ref, lse_ref, pa_ref):
        m_all = jnp.maximum(jnp.maximum(l0[...], l1[...]), l2[...])
        w0, w1, w2 = jnp.exp(l0[...] - m_all), jnp.exp(l1[...] - m_all), jnp.exp(l2[...] - m_all)
        den = w0 + w1 + w2
        att = (w0 * o0[...] + w1 * o1[...] + w2 * o2[...]) / den
        z = z_ref[...]
        att_ref[...] = att
        lse_ref[...] = m_all + jnp.log(den)
        pa_ref[...] = (att * (z * _sig(z))).astype(BF16)

    spec = _rows(tm, ATT)
    return pl.pallas_call(
        body, name="att_combine", grid=(t // tm,),
        in_specs=[spec] * 6 + [_rows(tm, ATT, C_ZA // ATT)],
        out_specs=[spec] * 3,
        out_shape=[jax.ShapeDtypeStruct((t, ATT), F32)] * 2 + [jax.ShapeDtypeStruct((t, ATT), BF16)],
    )(*parts, proj)


def _att_bwd(proj, tables, datt, dsum, lse, gi, dil):
    t = proj.shape[0]
    nt = t // TILE
    dl = _Dilated(dil)

    def body(q_ref, k_ref, v_ref, c_ref, lo_ref, hi_ref, cl_ref, lol_ref, hil_ref, do_ref, ds_ref, lse_ref,
             dq_ref, dk_ref, dv_ref, tmp, qd, kd, vd, dod, dsd, lsd, dqd, dkd, dvd, kt):
        i = pl.program_id(1)
        cur, prv = _slots(i)

        @pl.when(i < nt)
        def _():
            cs, lo, hi = c_ref[...], lo_ref[...], hi_ref[...]
            tmp[...] = _rope_apply(q_ref[...], cs, lo, hi) * SM_SCALE
            dl.spread(qd, 0, tmp, BF16)
            tmp[...] = _rope_apply(k_ref[...], cs, lo, hi)
            dl.spread(kd, cur, tmp, BF16)
            dl.spread(dqd, 0, tmp, F32)
            _store_transposed(kt, cur, dqd)
            dl.spread(vd, cur, v_ref, BF16)
            dl.spread(dod, 0, do_ref, BF16)
            dl.spread(dsd, 0, ds_ref, F32)
            dl.spread(lsd, 0, lse_ref, F32)
            dkd[pl.ds(cur, TILE), :] = jnp.zeros((TILE, 128), F32)
            dvd[pl.ds(cur, TILE), :] = jnp.zeros((TILE, 128), F32)

            def block(b, carry):
                row, prev, has_prev = dl.block_rows(b, i, cur, prv)
                q_pair, do_pair = _query_pair(qd[pl.ds(row, BLK), :]), _query_pair(dod[pl.ds(row, BLK), :])
                k_pair, v_pair = _key_pair(kd, prev, cur + row), _key_pair(vd, prev, cur + row)
                ds_t, ls_t = dsd[pl.ds(row, BLK), :].T, lsd[pl.ds(row, BLK), :].T
                lse = jnp.concatenate([ls_t[0:1, :], ls_t[HEAD:HEAD + 1, :]], axis=1)
                dsm = jnp.concatenate([ds_t[0:1, :], ds_t[HEAD:HEAD + 1, :]], axis=1)
                p = jnp.exp(jnp.where(_pair_mask(has_prev), _nt(k_pair, q_pair), NEG_INF) - lse)
                ds = (p * (_nt(v_pair, do_pair) - dsm)).astype(BF16)
                k_t = jnp.concatenate([kt[prev // BLK], kt[(cur + row) // BLK]], axis=1)
                dqd[pl.ds(row, BLK), :] = _own_head(jnp.dot(k_t, ds, preferred_element_type=F32)).T * SM_SCALE
                dk = jnp.dot(ds, q_pair, preferred_element_type=F32)
                dv = jnp.dot(p.astype(BF16), do_pair, preferred_element_type=F32)
                dkd[pl.ds(cur + row, BLK), :] += dk[BLK:2 * BLK, :]
                dvd[pl.ds(cur + row, BLK), :] += dv[BLK:2 * BLK, :]
                dkd[pl.ds(prev, BLK), :] += dk[0:BLK, :]
                dvd[pl.ds(prev, BLK), :] += dv[0:BLK, :]
                return carry

            lax.fori_loop(0, TILE // BLK, block, 0, unroll=True)
            dl.gather(tmp, dqd, 0)
            dq_ref[...] = _rope_apply(tmp[...], cs, -lo, -hi).astype(BF16)

        @pl.when(i > 0)
        def _():
            dl.gather(tmp, dkd, prv)
            dk_ref[...] = _rope_apply(tmp[...], cl_ref[...], -lol_ref[...], -hil_ref[...]).astype(BF16)
            dl.gather(tmp, dvd, prv)
            dv_ref[...] = tmp[...].astype(BF16)

    now = lambda col: pl.BlockSpec((TILE, 128), lambda hp, i: (jnp.minimum(i, nt - 1), col(hp)))
    lag = lambda col: pl.BlockSpec((TILE, 128), lambda hp, i: (jnp.maximum(i - 1, 0), col(hp)))
    first, pair = (lambda hp: 0), (lambda hp: hp)
    return pl.pallas_call(
        body, name=f"att_bwd_g{gi}", grid=(ATT // 128, nt + 1),
        in_specs=_qkv_specs(gi, nt - 1) + [now(first)] * 3 + [lag(first)] * 3 + [now(pair)] * 3,
        out_specs=[now(pair), lag(pair), lag(pair)],
        out_shape=[jax.ShapeDtypeStruct((t, ATT), BF16)] * 3,
        scratch_shapes=[pltpu.VMEM((TILE, 128), F32), pltpu.VMEM((TILE, 128), BF16), pltpu.VMEM((2 * TILE, 128), BF16),
                        pltpu.VMEM((2 * TILE, 128), BF16), pltpu.VMEM((TILE, 128), BF16), pltpu.VMEM((TILE, 128), F32),
                        pltpu.VMEM((TILE, 128), F32), pltpu.VMEM((TILE, 128), F32), pltpu.VMEM((2 * TILE, 128), F32),
                        pltpu.VMEM((2 * TILE, 128), F32), pltpu.VMEM((2 * TILE // BLK, 128, BLK), BF16)],
    )(proj, proj, proj, *tables, *tables, datt, dsum, lse)


def _merge_fwd(proj, y_conv, y_att, tm):
    t = proj.shape[0]

    def body(gc_ref, ga_ref, yc_ref, ya_ref, o_ref):
        o_ref[...] = (_sig(gc_ref[...]) * yc_ref[...] + _sig(ga_ref[...]) * ya_ref[...]).astype(BF16)

    return pl.pallas_call(body, name="merge_fwd", grid=(t // tm,),
                          in_specs=[_rows(tm, D, C_GC // D), _rows(tm, D, C_GA // D), _rows(tm, D), _rows(tm, D)],
                          out_specs=_rows(tm, D), out_shape=jax.ShapeDtypeStruct((t, D), BF16))(proj, proj, y_conv, y_att)


def _acc_rows(ref, i, val):
    @pl.when(i == 0)
    def _():
        ref[...] = jnp.zeros_like(ref)

    ref[...] += jnp.sum(val, axis=0, keepdims=True)


def _loss_head(x, o, mod, final_g, target, tm):
    t = x.shape[0]

    def body(x_ref, o_ref, mod_ref, fg_ref, tg_ref, dout_ref, do_ref, sq_ref, gfg_ref, dgate_ref):
        i = pl.program_id(0)
        gate = mod_ref[:, 2 * D:3 * D]
        ov = o_ref[...]
        out = x_ref[...] + gate * ov
        r = lax.rsqrt(jnp.mean(out * out, axis=-1, keepdims=True) + EPS)
        yn = out * r
        diff = yn * fg_ref[...] - tg_ref[...]
        dy = diff * (1.0 / D)
        gy = dy * fg_ref[...]
        dout = r * (gy - yn * jnp.mean(gy * yn, axis=-1, keepdims=True))
        dout_ref[...] = dout
        do_ref[...] = (dout * gate).astype(BF16)
        _acc_rows(sq_ref, i, diff * diff)
        _acc_rows(gfg_ref, i, dy * yn)
        _acc_rows(dgate_ref, i, dout * ov)

    vec = _full((1, D))
    return pl.pallas_call(
        body, name="loss_head", grid=(t // tm,),
        in_specs=[_rows(tm, D), _rows(tm, D), _full((1, 3 * D)), vec, _rows(tm, D)],
        out_specs=[_rows(tm, D), _rows(tm, D), vec, vec, vec],
        out_shape=[jax.ShapeDtypeStruct((t, D), F32), jax.ShapeDtypeStruct((t, D), BF16)] + [jax.ShapeDtypeStruct((1, D), F32)] * 3,
    )(x, o, mod, final_g, target)


def _merge_bwd(dmerged, proj, y_conv, y_att, tm):
    t = proj.shape[0]

    def body(dm_ref, gc_ref, ga_ref, yc_ref, ya_ref, dyc_ref, dya_ref, dp_ref):
        dm = dm_ref[...]
        sc, sa = _sig(gc_ref[...]), _sig(ga_ref[...])
        dyc_ref[...] = (dm * sc).astype(BF16)
        dya_ref[...] = (dm * sa).astype(BF16)
        dp_ref[:, 0:D] = (dm * yc_ref[...] * sc * (1.0 - sc)).astype(BF16)
        dp_ref[:, D:2 * D] = (dm * ya_ref[...] * sa * (1.0 - sa)).astype(BF16)

    return pl.pallas_call(
        body, name="merge_bwd", grid=(t // tm,),
        in_specs=[_rows(tm, D), _rows(tm, D, C_GC // D), _rows(tm, D, C_GA // D), _rows(tm, D), _rows(tm, D)],
        out_specs=[_rows(tm, D), _rows(tm, D), _rows(tm, 2 * D, C_GC // (2 * D))],
        out_shape=[jax.ShapeDtypeStruct((t, D), BF16), jax.ShapeDtypeStruct((t, D), BF16), jax.ShapeDtypeStruct((t, N_COL), BF16)],
    )(dmerged, proj, proj, y_conv, y_att)


def _att_pre_bwd(dpa, proj, att, dproj, tm):
    t = proj.shape[0]

    def body(dpa_ref, z_ref, att_ref, dp_in, datt_ref, ds_ref, dp_ref):
        del dp_in
        z, dpa_v, att_v = z_ref[...], dpa_ref[...], att_ref[...]
        s = _sig(z)
        datt = dpa_v * (z * s)
        datt_ref[...] = datt
        dp_ref[...] = (dpa_v * att_v * _dsilu(z, s)).astype(BF16)
        prod = datt * att_v
        for h in range(ATT // HEAD):
            sl = slice(h * HEAD, (h + 1) * HEAD)
            ds_ref[:, sl] = jnp.broadcast_to(jnp.sum(prod[:, sl], axis=-1, keepdims=True), (tm, HEAD))

    return pl.pallas_call(
        body, name="att_pre_bwd", grid=(t // tm,),
        in_specs=[_rows(tm, ATT), _rows(tm, ATT, C_ZA // ATT), _rows(tm, ATT), ANY],
        out_specs=[_rows(tm, ATT), _rows(tm, ATT), _rows(tm, ATT, C_ZA // ATT)],
        out_shape=[jax.ShapeDtypeStruct((t, ATT), F32), jax.ShapeDtypeStruct((t, ATT), F32), jax.ShapeDtypeStruct((t, N_COL), BF16)],
        input_output_aliases={3: 2},
    )(dpa, proj, att, dproj)


def _place_qkv(parts, dproj, col_block, tm, name):
    t = dproj.shape[0]

    def body(p0, p1, p2, dp_in, dp_ref):
        del dp_in
        for g, ref in enumerate((p0, p1, p2)):
            dp_ref[:, g * ATT:(g + 1) * ATT] = ref[...]

    return pl.pallas_call(
        body, name=name, grid=(t // tm,),
        in_specs=[_rows(tm, ATT)] * 3 + [ANY],
        out_specs=_rows(tm, QKV, col_block), out_shape=jax.ShapeDtypeStruct((t, N_COL), BF16),
        input_output_aliases={3: 0},
    )(*parts, dproj)


def _conv_bwd_rows(dpc, proj, u1, ln_g, ln_b, dproj, tm):
    t = proj.shape[0]

    def body(dpc_ref, z_ref, u1_ref, lg_ref, lb_ref, dp_in, du1_ref, dp_ref, dlg_ref, dlb_ref, dcb_ref):
        del dp_in
        i = pl.program_id(0)
        u1v = u1_ref[...]
        mu = jnp.mean(u1v, axis=-1, keepdims=True)
        xc = u1v - mu
        r = lax.rsqrt(jnp.mean(xc * xc, axis=-1, keepdims=True) + EPS)
        uhat = xc * r
        u2 = uhat * lg_ref[...] + lb_ref[...]
        s2 = _sig(u2)
        z = z_ref[...]
        sz = _sig(z)
        dpc_v = dpc_ref[...]
        dp_ref[...] = (dpc_v * (u2 * s2) * _dsilu(z, sz)).astype(BF16)
        du2 = dpc_v * (z * sz) * _dsilu(u2, s2)
        duhat = du2 * lg_ref[...]
        du1 = r * (duhat - jnp.mean(duhat, axis=-1, keepdims=True) - uhat * jnp.mean(duhat * uhat, axis=-1, keepdims=True))
        du1_ref[...] = du1
        _acc_rows(dlg_ref, i, du2 * uhat)
        _acc_rows(dlb_ref, i, du2)
        _acc_rows(dcb_ref, i, du1)

    vec = _full((1, D))
    return pl.pallas_call(
        body, name="conv_bwd_rows", grid=(t // tm,),
        in_specs=[_rows(tm, D), _rows(tm, D, C_ZC // D), _rows(tm, D), vec, vec, ANY],
        out_specs=[_rows(tm, D), _rows(tm, D, C_ZC // D), vec, vec, vec],
        out_shape=[jax.ShapeDtypeStruct((t, D), F32), jax.ShapeDtypeStruct((t, N_COL), BF16)] + [jax.ShapeDtypeStruct((1, D), F32)] * 3,
        input_output_aliases={5: 1},
    )(dpc, proj, u1, ln_g, ln_b, dproj)


def _conv_bwd_taps(du1, proj, conv_w, dproj, tm):
    t = proj.shape[0]
    hb = tm // HALO
    last = t // HALO - 1

    def body(du_ref, duh_ref, a_ref, b_ref, ah_ref, bh_ref, w_ref, dp_in, dp_ref, dw_ref, dbuf, ubuf, g0, shd, shu):
        del dp_in
        i = pl.program_id(0)
        a, sb = a_ref[...], _sig(b_ref[...])
        ubuf[0:HALO, :] = jnp.where(i > 0, ah_ref[...] * _sig(bh_ref[...]), 0.0)
        ubuf[HALO:HALO + tm, :] = a * sb
        dbuf[0:tm, :] = du_ref[...]
        dbuf[tm:tm + HALO, :] = jnp.where(i < pl.num_programs(0) - 1, duh_ref[...], 0.0)

        @pl.when(i == 0)
        def _():
            dw_ref[...] = jnp.zeros_like(dw_ref)

        def col(ci, carry):
            c0 = pl.multiple_of(ci * 128, 128)
            _shift_copies(shd, dbuf, c0)
            _shift_copies(shu, ubuf, c0)
            for rc in range(tm // 64):
                g0[rc * 64:(rc + 1) * 64, pl.ds(c0, 128)] = _conv_taps(
                    jnp.zeros((64, 128), F32), w_ref, dbuf, shd, rc * 64, c0, lambda j: CONV_K - 1 - j)
            for j in range(CONV_K):
                part = jnp.zeros((8, 128), F32)
                for rc in range(tm // 64):
                    off = rc * 64 + HALO - (CONV_K - 1) + j
                    prod = dbuf[rc * 64:(rc + 1) * 64, pl.ds(c0, 128)] * _window64(ubuf, shu, c0, off)
                    part = part + jnp.sum(prod.reshape(8, 8, 128), axis=0)
                dw_ref[j:j + 1, pl.ds(c0, 128)] += jnp.sum(part, axis=0, keepdims=True)
            return carry

        lax.fori_loop(0, D // 128, col, 0)
        du0 = g0[...]
        dp_ref[:, 0:D] = (du0 * sb).astype(BF16)
        dp_ref[:, D:2 * D] = (du0 * a * sb * (1.0 - sb)).astype(BF16)

    prev = lambda col: pl.BlockSpec((HALO, D), lambda i: (jnp.maximum(i * hb - 1, 0), col))
    nxt = pl.BlockSpec((HALO, D), lambda i: (jnp.minimum((i + 1) * hb, last), 0))
    return pl.pallas_call(
        body, name="conv_bwd_taps", grid=(t // tm,),
        in_specs=[_rows(tm, D), nxt, _rows(tm, D, 0), _rows(tm, D, 1), prev(0), prev(1), _full((CONV_KP, D)), ANY],
        out_specs=[_rows(tm, 2 * D, 0), _full((CONV_KP, D))],
        out_shape=[jax.ShapeDtypeStruct((t, N_COL), BF16), jax.ShapeDtypeStruct((CONV_KP, D), F32)],
        scratch_shapes=[pltpu.VMEM((tm + HALO, D), F32), pltpu.VMEM((HALO + tm, D), F32), pltpu.VMEM((tm, D), F32),
                        pltpu.VMEM((8, HALO + tm, 128), F32), pltpu.VMEM((8, HALO + tm, 128), F32)],
        input_output_aliases={7: 0},
    )(du1, du1, proj, proj, proj, proj, conv_w, dproj)


def _prenorm_bwd(dh, x, dout, mod, norm_g, tm):
    t = x.shape[0]

    def body(dh_ref, x_ref, dout_ref, mod_ref, g_ref, gx_ref, dshift_ref, dscale_ref, dg_ref):
        i = pl.program_id(0)
        xv, dhv = x_ref[...], dh_ref[...]
        r = lax.rsqrt(jnp.mean(xv * xv, axis=-1, keepdims=True) + EPS)
        xn = xv * r
        one_scale = 1.0 + mod_ref[:, D:2 * D]
        dxn = dhv * (g_ref[...] * one_scale)
        gx_ref[...] = r * (dxn - xn * jnp.mean(dxn * xn, axis=-1, keepdims=True)) + dout_ref[...]
        _acc_rows(dshift_ref, i, dhv)
        _acc_rows(dscale_ref, i, dhv * xn * g_ref[...])
        _acc_rows(dg_ref, i, dhv * xn * one_scale)

    vec = _full((1, D))
    return pl.pallas_call(
        body, name="prenorm_bwd", grid=(t // tm,),
        in_specs=[_rows(tm, D), _rows(tm, D), _rows(tm, D), _full((1, 3 * D)), vec],
        out_specs=[_rows(tm, D), vec, vec, vec],
        out_shape=[jax.ShapeDtypeStruct((t, D), F32)] + [jax.ShapeDtypeStruct((1, D), F32)] * 3,
    )(dh, x, dout, mod, norm_g)


def _sum_devices(gathered):
    w = gathered.shape[-1]

    def body(g_ref, o_ref):
        acc = g_ref[0]
        for j in range(1, N_DEV):
            acc = acc + g_ref[j]
        o_ref[...] = acc

    return pl.pallas_call(body, name="sum_devices", grid=(1,), in_specs=[_full(gathered.shape)], out_specs=_full((1, w)),
                          out_shape=jax.ShapeDtypeStruct((1, w), F32))(gathered)


def _rope_tables(positions):
    half = HEAD // 8
    t = positions.shape[-1]
    inv_freq = ROPE_THETA ** (-(jnp.arange(half, dtype=F32) * 2.0 / (2 * half)))
    ang = positions.reshape(t, 1).astype(F32) * inv_freq
    cos, sin = jnp.cos(ang), jnp.sin(ang)
    zeros = lambda n: jnp.zeros((t, n), F32)
    c64 = jnp.concatenate([cos, cos, jnp.ones((t, HEAD - 2 * half), F32)], axis=1)
    lo64 = jnp.concatenate([-sin, zeros(HEAD - half)], axis=1)
    hi64 = jnp.concatenate([zeros(half), sin, zeros(HEAD - 2 * half)], axis=1)
    return tuple(jnp.tile(a, (1, 2)) for a in (c64, lo64, hi64))


def kernel(x, c, positions, norm_g, w_ada, b_ada, w_in, conv_w, conv_b, conv_ln_g, conv_ln_b, w_conv_out, w_att_out, w_o, final_g, loss_target, m_norm_g, m_w_ada, m_b_ada, m_w_in, m_conv_w, m_conv_b, m_conv_ln_g, m_conv_ln_b, m_w_conv_out, m_w_att_out, m_w_o, m_final_g, v_norm_g, v_w_ada, v_b_ada, v_w_in, v_conv_w, v_conv_b, v_conv_ln_g, v_conv_ln_b, v_w_conv_out, v_w_att_out, v_w_o, v_final_g):
    me = 4 * lax.axis_index("x") + 2 * lax.axis_index("y") + lax.axis_index("c")
    x2, tgt = x[0], loss_target[0]
    t = x2.shape[0]
    te = 512 if t % 512 == 0 else 256
    tcv = 256
    tmm = 1024 if t % 1024 == 0 else 256
    n_ada = w_ada.shape[-1]

    pad_taps = lambda a: jnp.pad(a[0], ((0, CONV_KP - CONV_K), (0, 0)))
    shards = (_cast_bf16(w_in[0], "cast_w_in"), _cast_bf16(w_conv_out[0], "cast_w_conv_out"),
              _cast_bf16(w_att_out[0], "cast_w_att_out"), _cast_bf16(w_o[0], "cast_w_o"), pad_taps(conv_w))
    block_of = lambda relations: jnp.bitwise_xor(me, jnp.array(relations, jnp.int32))

    c_all = _allgather_small(c, "gather_c").reshape(N_DEV, D)
    b_ada_l = lax.dynamic_slice(b_ada, (0, me * n_ada), (1, n_ada))
    parts = _allgather_small(_mod_part(c_all, w_ada[0], b_ada_l), "gather_mod")
    mod = lax.dynamic_slice(parts, (0, me, 0), (N_DEV, 1, n_ada)).reshape(1, N_DEV * n_ada)

    h = _prenorm(x2, mod, norm_g, te)
    proj, w_in_f, w_co_f, w_ao_f, w_o_f, conv_w_f = _proj_gather(h, shards, block_of(GATHER_ORDER), tmm)
    u1, pc = _conv_fwd(proj, conv_w_f, conv_b, conv_ln_g, conv_ln_b, tcv)
    tables = _rope_tables(positions)
    parts_att = []
    for gi, dil in GROUPS:
        parts_att += _att_fwd(proj, tables, gi, dil)
    att, lse, pa = _att_combine(parts_att, proj, te)
    y_conv = _matmul(pc, w_co_f, tm=tmm, tn=D, tk=D, name="y_conv")
    y_att = _matmul(pa, w_ao_f, tm=tmm, tn=D, tk=ATT, name="y_att")
    merged = _merge_fwd(proj, y_conv, y_att, te)
    o = _matmul(merged, w_o_f, tm=tmm, tn=D, tk=D, name="out_proj")
    dout, do, sq_sum, g_final, d_gate = _loss_head(x2, o, mod, final_g.reshape(1, D), tgt, te)

    dmerged = _matmul(do, w_o_f, tb=True, tm=tmm, tn=D, tk=D, name="d_merged")
    dw_o = _matmul(merged, do, ta=True, out_dtype=BF16, tm=D, tn=D, tk=512, name="dw_o")
    dyc, dya, dproj = _merge_bwd(dmerged, proj, y_conv, y_att, te)
    dpc = _matmul(dyc, w_co_f, tb=True, tm=tmm, tn=D, tk=D, name="d_pc")
    dw_co = _matmul(pc, dyc, ta=True, out_dtype=BF16, tm=D, tn=D, tk=512, name="dw_conv_out")
    dpa = _matmul(dya, w_ao_f, tb=True, tm=tmm, tn=ATT, tk=D, name="d_pa")
    dw_ao = _matmul(pa, dya, ta=True, out_dtype=BF16, tm=ATT, tn=D, tk=512, name="dw_att_out")
    datt, dsum, dproj = _att_pre_bwd(dpa, proj, att, dproj, te)
    dqs, dks, dvs = [], [], []
    for gi, dil in GROUPS:
        dq, dk, dv = _att_bwd(proj, tables, datt, dsum, lse, gi, dil)
        dqs.append(dq), dks.append(dk), dvs.append(dv)
    dproj = _place_qkv(dqs, dproj, C_Q // QKV, te, "place_dq")
    dproj = _place_qkv(dks, dproj, C_K // QKV, te, "place_dk")
    dproj = _place_qkv(dvs, dproj, C_V // QKV, te, "place_dv")
    du1, dproj, d_ln_g, d_ln_b, d_conv_b = _conv_bwd_rows(dpc, proj, u1, conv_ln_g, conv_ln_b, dproj, te)
    dproj, dconv_w = _conv_bwd_taps(du1, proj, conv_w_f, dproj, tcv)
    dh = _matmul(dproj, w_in_f, tb=True, tm=tmm, tn=D, tk=N_COL // N_DEV, name="d_h")
    grad_x, d_shift, d_scale, d_norm_g = _prenorm_bwd(dh, x2, dout, mod, norm_g, te)

    packed = jnp.concatenate([d_shift, d_scale, d_gate, d_norm_g, d_conv_b, d_ln_g, d_ln_b, g_final, sq_sum], axis=1)
    gathered = _allgather_small(packed, "gather_partials")
    total = _sum_devices(gathered)
    seg = lambda k, n=1: total[:, k * D:(k + n) * D]
    g_b_ada, g_norm_g, g_conv_b, g_ln_g, g_ln_b, g_final_g = seg(0, 3), seg(3), seg(4), seg(5), seg(6), seg(7)
    loss = (0.5 / D) * jnp.sum(seg(8))
    dmod_all = gathered[:, 0, 0:3 * D]
    dmod_cols = lax.dynamic_slice(dmod_all, (0, me * n_ada), (N_DEV, n_ada))
    g_w_ada, d_w_ada, nm_w_ada, nv_w_ada = _w_ada_update(c_all.T, dmod_cols, w_ada[0], m_w_ada[0], v_w_ada[0])

    small = {}
    for name, g, w, m, v in (("norm_g", g_norm_g, norm_g, m_norm_g, v_norm_g), ("b_ada", g_b_ada, b_ada, m_b_ada, v_b_ada),
                             ("conv_b", g_conv_b, conv_b, m_conv_b, v_conv_b), ("conv_ln_g", g_ln_g, conv_ln_g, m_conv_ln_g, v_conv_ln_g),
                             ("conv_ln_b", g_ln_b, conv_ln_b, m_conv_ln_b, v_conv_ln_b),
                             ("final_g", g_final_g, final_g.reshape(1, D), m_final_g.reshape(1, D), v_final_g.reshape(1, D))):
        small[name] = (g,) + tuple(_adamw_small(g, w, m, v, "adamw_" + name))

    slots = _dw_in_scatter(h, dproj, (dw_co, dw_ao, dw_o, dconv_w), block_of(SCATTER_ORDER), 512)
    big = {
        "w_in": _sum_adamw(slots[0], w_in[0], m_w_in[0], v_w_in[0], 256, "adamw_w_in"),
        "w_conv_out": _sum_adamw(slots[1], w_conv_out[0], m_w_conv_out[0], v_w_conv_out[0], 128, "adamw_w_conv_out"),
        "w_att_out": _sum_adamw(slots[2], w_att_out[0], m_w_att_out[0], v_w_att_out[0], 512, "adamw_w_att_out"),
        "w_o": _sum_adamw(slots[3], w_o[0], m_w_o[0], v_w_o[0], 128, "adamw_w_o"),
        "conv_w": [r[:CONV_K] for r in _sum_adamw(slots[4], pad_taps(conv_w), pad_taps(m_conv_w), pad_taps(v_conv_w), CONV_KP, "adamw_conv_w")],
    }
    big["w_ada"] = (g_w_ada, d_w_ada, nm_w_ada, nv_w_ada)

    order = ("norm_g", "w_ada", "b_ada", "w_in", "conv_w", "conv_b", "conv_ln_g", "conv_ln_b", "w_conv_out", "w_att_out", "w_o", "final_g")
    lead = lambda name, a: a.reshape(D) if name == "final_g" else (a[None] if name in big else a)
    result = {**small, **big}
    outs = [loss, grad_x[None]]
    for field in range(4):
        outs += [lead(name, result[name][field]) for name in order]
    return tuple(outs)
```

```python
import functools

import jax
import jax.numpy as jnp
from jax import lax
from jax.experimental import pallas as pl
from jax.experimental.pallas import tpu as pltpu

F32 = jnp.float32
BF16 = jnp.bfloat16

N_DEV = 8
D = 1024
N_COL = 10240
C_A, C_B, C_ZC, C_Q, C_K, C_V, C_ZA, C_GC, C_GA = 0, 1024, 2048, 3072, 4608, 6144, 7680, 8192, 9216
QKV = 1536
ATT = 512
HEAD = 64
BLK = 128
TILE = 2048
GROUPS = ((0, 1), (1, 4), (2, 16))
CONV_K = 31
CONV_KP = 32
HALO = 32
EPS = 1e-6
NEG_INF = -1e30
ROPE_THETA = 500000.0
SM_SCALE = HEAD ** -0.5

ADAM_LR, ADAM_B1, ADAM_B2, ADAM_EPS, ADAM_WD, ADAM_STEP = 0.001, 0.9, 0.999, 1e-08, 0.01, 10

MESH = pl.DeviceIdType.MESH
ANY = pl.BlockSpec(memory_space=pl.ANY)


def _sig(v):
    return 1.0 / (1.0 + jnp.exp(-v))


def _dsilu(v, s):
    return s * (1.0 + v * (1.0 - s))


def _full(shape):
    return pl.BlockSpec(shape, lambda *_: (0,) * len(shape))


def _rows(tm, width, col=0):
    return pl.BlockSpec((tm, width), lambda i: (i, col))


def _matmul(a, b, *, ta=False, tb=False, out_dtype=F32, tm, tn, tk, name):
    m, k = (a.shape[1], a.shape[0]) if ta else a.shape
    stacked = b.ndim == 3
    if stacked:
        assert tb and b.shape[2] == tk and b.shape[0] * tk == k
        n = b.shape[1]
    else:
        n = b.shape[0] if tb else b.shape[1]
        assert (b.shape[1] if tb else b.shape[0]) == k
    assert m % tm == 0 and n % tn == 0 and k % tk == 0
    nk = k // tk
    dims = (((0 if ta else 1,), (1 if tb else 0,)), ((), ()))
    use_scratch = out_dtype != F32 and nk > 1

    def body(a_ref, b_ref, o_ref, *scratch):
        p = lax.dot_general(a_ref[...], b_ref[...], dims, preferred_element_type=F32)
        if nk == 1:
            o_ref[...] = p.astype(out_dtype)
            return
        acc = scratch[0] if use_scratch else o_ref
        kk = pl.program_id(2)

        @pl.when(kk == 0)
        def _():
            acc[...] = p

        @pl.when(kk > 0)
        def _():
            acc[...] += p

        if use_scratch:
            @pl.when(kk == nk - 1)
            def _():
                o_ref[...] = acc[...].astype(out_dtype)

    a_spec = pl.BlockSpec((tk, tm), lambda i, j, kk: (kk, i)) if ta else pl.BlockSpec((tm, tk), lambda i, j, kk: (i, kk))
    b_spec = pl.BlockSpec((tn, tk), lambda i, j, kk: (j, kk)) if tb else pl.BlockSpec((tk, tn), lambda i, j, kk: (kk, j))
    if stacked:
        b_spec = pl.BlockSpec((None, tn, tk), lambda i, j, kk: (kk, j, 0))
    return pl.pallas_call(
        body, name=name, grid=(m // tm, n // tn, nk),
        in_specs=[a_spec, b_spec],
        out_specs=pl.BlockSpec((tm, tn), lambda i, j, kk: (i, j)),
        out_shape=jax.ShapeDtypeStruct((m, n), out_dtype),
        scratch_shapes=[pltpu.VMEM((tm, tn), F32)] if use_scratch else [],
    )(a, b)


def _me_and_peers():
    x, y, c = lax.axis_index("x"), lax.axis_index("y"), lax.axis_index("c")
    me = 4 * x + 2 * y + c
    peers = []
    for k in range(1, N_DEV):
        px, py, pc = x ^ (k >> 2), y ^ ((k >> 1) & 1), c ^ (k & 1)
        peers.append(((px, py, pc), 4 * px + 2 * py + pc))
    return me, peers


def _allgather_small(v, name):
    r, c = v.shape

    def body(v_ref, out_ref, send_sems, recv_sems):
        me, peers = _me_and_peers()
        out_ref[me] = v_ref[...]
        copies = []
        for k, (dev, _) in enumerate(peers):
            cp = pltpu.make_async_remote_copy(src_ref=v_ref, dst_ref=out_ref.at[me], send_sem=send_sems.at[k],
                                              recv_sem=recv_sems.at[k], device_id=dev, device_id_type=MESH)
            cp.start()
            copies.append(cp)
        for k, (dev, idx) in enumerate(peers):
            pltpu.make_async_remote_copy(src_ref=v_ref, dst_ref=out_ref.at[idx], send_sem=send_sems.at[k],
                                         recv_sem=recv_sems.at[k], device_id=dev, device_id_type=MESH).wait_recv()
        for cp in copies:
            cp.wait_send()

    return pl.pallas_call(
        body, name=name,
        in_specs=[pl.BlockSpec(memory_space=pltpu.VMEM)],
        out_specs=pl.BlockSpec(memory_space=pltpu.VMEM),
        out_shape=jax.ShapeDtypeStruct((N_DEV, r, c), v.dtype),
        scratch_shapes=[pltpu.SemaphoreType.DMA((N_DEV - 1,)), pltpu.SemaphoreType.DMA((N_DEV - 1,))],
    )(v)


def _window(ref, kind, idx, size):
    if kind == "block":
        return ref.at[idx]
    start = pl.multiple_of(idx * size, size)
    if kind == "rows":
        return ref.at[pl.ds(start, size), :]
    return ref.at[:, pl.ds(start, size)]


_BIG = (("cols", N_COL // N_DEV), ("rows", D // N_DEV), ("cols", D // N_DEV), ("rows", D // N_DEV), ("cols", D // N_DEV))
_GATHERED = (("block", 1),) + _BIG[1:]


GATHER_ORDER = (0, 1, 2, 4, 3, 5, 6, 7)
W_IN_DIRECT = (1, 2, 4, 6)
SCATTER_ORDER = (7, 5, 3, 6, 4, 2, 1, 0)
W_IN_SLOT = {0: 0, 1: 1, 2: 2, 4: 3, 6: 4}


def _proj_gather(h, shards, order, tm):
    t = h.shape[0]
    nt = len(shards)
    n_blk = N_COL // N_DEV
    full_shapes = []
    for s, (kind, size) in zip(shards, _GATHERED):
        full_shapes.append(jax.ShapeDtypeStruct({"block": (N_DEV,) + s.shape, "rows": (s.shape[0] * N_DEV, s.shape[1]),
                                                 "cols": (s.shape[0], s.shape[1] * N_DEV)}[kind], s.dtype))
    last = (N_DEV - 1, t // tm - 1)

    def body(order_ref, h_ref, *refs):
        src, proj_ref, dst = refs[:nt], refs[nt], refs[nt + 1:2 * nt + 1]
        w_all, send_sems, recv_sems, local_sems, keep_sems = refs[2 * nt + 1:]
        j, i = pl.program_id(0), pl.program_id(1)
        me, peers = _me_and_peers()

        def landing(tn, idx):
            kind, size = _GATHERED[tn]
            return w_all.at[idx] if tn == 0 else _window(dst[tn], kind, idx, size)

        def local(tn):
            return pltpu.make_async_copy(src[tn], landing(tn, me), local_sems.at[tn])

        def remote(tn, k, block_of):
            dev, idx = peers[k - 1]
            return pltpu.make_async_remote_copy(src_ref=src[tn], dst_ref=landing(tn, me if block_of == "mine" else idx),
                                                send_sem=send_sems.at[tn, k - 1], recv_sem=recv_sems.at[tn, k - 1],
                                                device_id=dev, device_id_type=MESH)

        def forward(k):
            block = w_all.at[peers[k - 1][1]]
            return pltpu.make_async_remote_copy(src_ref=block, dst_ref=block, send_sem=send_sems.at[0, k], recv_sem=recv_sems.at[0, k],
                                                device_id=peers[0][0], device_id_type=MESH)

        def keep(step):
            blk = order_ref[step]
            return pltpu.make_async_copy(w_all.at[blk], dst[0].at[blk], keep_sems.at[step])

        @pl.when((j == 0) & (i == 0))
        def _():
            for tn in range(nt):
                local(tn).start()
                for k in GATHER_ORDER[1:]:
                    if tn > 0 or k in W_IN_DIRECT:
                        remote(tn, k, "mine").start()

        @pl.when(i == 0)
        def _():
            for step, k in enumerate(GATHER_ORDER):
                @pl.when(j == step)
                def _():
                    if k == 0:
                        local(0).wait()
                    else:
                        remote(0, k, "theirs").wait_recv()
                        if k in W_IN_DIRECT and k > 1:
                            forward(k).start()
                    keep(step).start()

        proj_ref[...] = jnp.dot(h_ref[...], w_all[order_ref[j]], preferred_element_type=F32)

        @pl.when((j == last[0]) & (i == last[1]))
        def _():
            for step in range(N_DEV):
                keep(step).wait()
            for tn in range(1, nt):
                local(tn).wait()
                for k in range(1, N_DEV):
                    remote(tn, k, "theirs").wait_recv()
            for tn in range(nt):
                for k in range(1, N_DEV):
                    if tn > 0 or k in W_IN_DIRECT:
                        remote(tn, k, "mine").wait_send()
                    else:
                        forward(k - 1).wait_send()

    grid_spec = pltpu.PrefetchScalarGridSpec(
        num_scalar_prefetch=1, grid=(N_DEV, t // tm),
        in_specs=[pl.BlockSpec((tm, D), lambda j, i, order_ref: (i, 0))] + [ANY] * nt,
        out_specs=[pl.BlockSpec((tm, n_blk), lambda j, i, order_ref: (i, order_ref[j]))] + [ANY] * nt,
        scratch_shapes=[pltpu.VMEM((N_DEV, D, n_blk), BF16), pltpu.SemaphoreType.DMA((nt, N_DEV - 1)),
                        pltpu.SemaphoreType.DMA((nt, N_DEV - 1)), pltpu.SemaphoreType.DMA((nt,)), pltpu.SemaphoreType.DMA((N_DEV,))],
    )
    return pl.pallas_call(
        body, name="proj_gather", grid_spec=grid_spec,
        out_shape=[jax.ShapeDtypeStruct((t, N_COL), F32)] + full_shapes,
    )(order, h, *shards)


def _dw_in_scatter(h, dproj, small_grads, order, tk):
    t = h.shape[0]
    nt = 1 + len(small_grads)
    n_blk = N_COL // N_DEV
    nk = t // tk
    slot_shapes = [jax.ShapeDtypeStruct((len(W_IN_SLOT), D, n_blk), BF16)]
    for g, (kind, size) in zip(small_grads, _BIG[1:]):
        slot_shapes.append(jax.ShapeDtypeStruct((N_DEV,) + ((size, g.shape[1]) if kind == "rows" else (g.shape[0], size)), g.dtype))

    def body(order_ref, h_ref, dp_ref, *refs):
        src, dst = refs[:nt - 1], refs[nt - 1:2 * nt - 1]
        acc, stage, partner, send_sems, recv_sems, local_sems, pair_send, pair_recv = refs[2 * nt - 1:]
        j, kk = pl.program_id(0), pl.program_id(1)
        me, peers = _me_and_peers()

        def small_local(tn):
            kind, size = _BIG[tn]
            return pltpu.make_async_copy(_window(src[tn - 1], kind, me, size), dst[tn].at[me], local_sems.at[tn])

        def small_remote(tn, k, mine):
            kind, size = _BIG[tn]
            dev, idx = peers[k - 1]
            return pltpu.make_async_remote_copy(src_ref=_window(src[tn - 1], kind, idx if mine else me, size),
                                                dst_ref=dst[tn].at[me if mine else idx],
                                                send_sem=send_sems.at[tn, k - 1], recv_sem=recv_sems.at[tn, k - 1],
                                                device_id=dev, device_id_type=MESH)

        def push(step):
            k, slot = SCATTER_ORDER[step], step % 2
            if k == 0:
                return pltpu.make_async_copy(stage.at[slot], dst[0].at[W_IN_SLOT[0]], local_sems.at[0])
            if k not in W_IN_SLOT:
                p = (k - 3) // 2
                return pltpu.make_async_remote_copy(src_ref=stage.at[slot], dst_ref=partner.at[p], send_sem=pair_send.at[p],
                                                    recv_sem=pair_recv.at[p], device_id=peers[0][0], device_id_type=MESH)
            return pltpu.make_async_remote_copy(src_ref=stage.at[slot], dst_ref=dst[0].at[W_IN_SLOT[k]],
                                                send_sem=send_sems.at[0, k - 1], recv_sem=recv_sems.at[0, k - 1],
                                                device_id=peers[k - 1][0], device_id_type=MESH)

        @pl.when((j == 0) & (kk == 0))
        def _():
            for tn in range(1, nt):
                small_local(tn).start()
                for k in range(1, N_DEV):
                    small_remote(tn, k, True).start()

        p = lax.dot_general(h_ref[...], dp_ref[...], (((0,), (0,)), ((), ())), preferred_element_type=F32)

        @pl.when(kk == 0)
        def _():
            acc[...] = p

        @pl.when(kk > 0)
        def _():
            acc[...] += p

        @pl.when(kk == nk - 1)
        def _():
            for step, k in enumerate(SCATTER_ORDER):
                @pl.when(j == step)
                def _():
                    if step >= 2:
                        push(step - 2).wait_send()
                    total = acc[...]
                    if k in W_IN_SLOT and k >= 2:
                        p = k // 2 - 1
                        push(SCATTER_ORDER.index(k + 1)).wait_recv()
                        total = total + partner[p].astype(F32)
                    stage[step % 2] = total.astype(BF16)
                    push(step).start()

        @pl.when((j == N_DEV - 1) & (kk == nk - 1))
        def _():
            push(N_DEV - 2).wait_send()
            push(N_DEV - 1).wait()
            for k in (1, 2, 4, 6):
                push(SCATTER_ORDER.index(k)).wait_recv()
            for tn in range(1, nt):
                small_local(tn).wait()
                for k in range(1, N_DEV):
                    small_remote(tn, k, False).wait_recv()
                    small_remote(tn, k, True).wait_send()

    grid_spec = pltpu.PrefetchScalarGridSpec(
        num_scalar_prefetch=1, grid=(N_DEV, nk),
        in_specs=[pl.BlockSpec((tk, D), lambda j, kk, order_ref: (kk, 0)),
                  pl.BlockSpec((tk, n_blk), lambda j, kk, order_ref: (kk, order_ref[j]))] + [ANY] * (nt - 1),
        out_specs=[ANY] * nt,
        scratch_shapes=[pltpu.VMEM((D, n_blk), F32), pltpu.VMEM((2, D, n_blk), BF16), pltpu.VMEM((3, D, n_blk), BF16),
                        pltpu.SemaphoreType.DMA((nt, N_DEV - 1)), pltpu.SemaphoreType.DMA((nt, N_DEV - 1)),
                        pltpu.SemaphoreType.DMA((nt,)), pltpu.SemaphoreType.DMA((3,)), pltpu.SemaphoreType.DMA((3,))],
    )
    return pl.pallas_call(body, name="dw_in_scatter", grid_spec=grid_spec, out_shape=slot_shapes)(order, h, dproj, *small_grads)


def _adamw_math(w, g, m, v):
    m = ADAM_B1 * m + (1.0 - ADAM_B1) * g
    v = ADAM_B2 * v + (1.0 - ADAM_B2) * (g * g)
    m_hat = m / (1.0 - ADAM_B1 ** ADAM_STEP)
    v_hat = v / (1.0 - ADAM_B2 ** ADAM_STEP)
    delta = -ADAM_LR * (m_hat / (jnp.sqrt(v_hat) + ADAM_EPS) + ADAM_WD * w)
    return delta, m, v


def _sum_adamw(slots, w, m, v, tr, name):
    n_slots, r, c = slots.shape
    assert r % tr == 0

    def body(s_ref, w_ref, m_ref, v_ref, g_ref, d_ref, nm_ref, nv_ref):
        g = s_ref[0].astype(F32)
        for j in range(1, n_slots):
            g = g + s_ref[j].astype(F32)
        delta, nm, nv = _adamw_math(w_ref[...], g, m_ref[...], v_ref[...])
        g_ref[...] = g
        d_ref[...] = delta
        nm_ref[...] = nm
        nv_ref[...] = nv

    blk = pl.BlockSpec((tr, c), lambda i: (i, 0))
    return pl.pallas_call(
        body, name=name, grid=(r // tr,),
        in_specs=[pl.BlockSpec((n_slots, tr, c), lambda i: (0, i, 0)), blk, blk, blk],
        out_specs=[blk] * 4, out_shape=[jax.ShapeDtypeStruct((r, c), F32)] * 4,
    )(slots, w, m, v)


def _adamw_small(g, w, m, v, name):
    def body(g_ref, w_ref, m_ref, v_ref, d_ref, nm_ref, nv_ref):
        delta, nm, nv = _adamw_math(w_ref[...], g_ref[...], m_ref[...], v_ref[...])
        d_ref[...] = delta
        nm_ref[...] = nm
        nv_ref[...] = nv

    spec = _full(g.shape)
    return pl.pallas_call(body, name=name, grid=(1,), in_specs=[spec] * 4, out_specs=[spec] * 3,
                          out_shape=[jax.ShapeDtypeStruct(g.shape, F32)] * 3)(g, w, m, v)


def _mod_part(c_all, w_ada_l, b_ada_l):
    n = w_ada_l.shape[1]

    def body(c_ref, w_ref, b_ref, o_ref):
        o_ref[...] = jnp.dot(c_ref[...], w_ref[...], preferred_element_type=F32,
                             precision=lax.Precision.HIGHEST) + b_ref[...]

    return pl.pallas_call(body, name="mod_part", grid=(1,),
                          in_specs=[_full(c_all.shape), _full(w_ada_l.shape), _full(b_ada_l.shape)],
                          out_specs=_full((N_DEV, n)), out_shape=jax.ShapeDtypeStruct((N_DEV, n), F32))(c_all, w_ada_l, b_ada_l)


def _w_ada_update(c_all_t, dmod_cols, w, m, v):
    def body(c_ref, dm_ref, w_ref, m_ref, v_ref, g_ref, d_ref, nm_ref, nv_ref):
        g = c_ref[:, 0:1] * dm_ref[0:1, :]
        for b in range(1, N_DEV):
            g = g + c_ref[:, b:b + 1] * dm_ref[b:b + 1, :]
        delta, nm, nv = _adamw_math(w_ref[...], g, m_ref[...], v_ref[...])
        g_ref[...] = g
        d_ref[...] = delta
        nm_ref[...] = nm
        nv_ref[...] = nv

    spec = _full(w.shape)
    return pl.pallas_call(body, name="w_ada_update", grid=(1,),
                          in_specs=[_full(c_all_t.shape), _full(dmod_cols.shape), spec, spec, spec],
                          out_specs=[spec] * 4, out_shape=[jax.ShapeDtypeStruct(w.shape, F32)] * 4)(c_all_t, dmod_cols, w, m, v)


def _cast_bf16(w, name):
    def body(w_ref, o_ref):
        o_ref[...] = w_ref[...].astype(BF16)

    return pl.pallas_call(body, name=name, grid=(1,), in_specs=[_full(w.shape)], out_specs=_full(w.shape),
                          out_shape=jax.ShapeDtypeStruct(w.shape, BF16))(w)


def _prenorm(x, mod, norm_g, tm):
    t = x.shape[0]

    def body(x_ref, mod_ref, g_ref, h_ref):
        xv = x_ref[...]
        r = lax.rsqrt(jnp.mean(xv * xv, axis=-1, keepdims=True) + EPS)
        h = (xv * r) * g_ref[...] * (1.0 + mod_ref[:, D:2 * D]) + mod_ref[:, 0:D]
        h_ref[...] = h.astype(BF16)

    return pl.pallas_call(body, name="prenorm", grid=(t // tm,),
                          in_specs=[_rows(tm, D), _full((1, 3 * D)), _full((1, D))],
                          out_specs=_rows(tm, D), out_shape=jax.ShapeDtypeStruct((t, D), BF16))(x, mod, norm_g)


def _rope_apply(t, cos, s_lo, s_hi):
    return t * cos + pltpu.roll(t, 120, 1) * s_lo + pltpu.roll(t, 8, 1) * s_hi


def _shift_copies(sh, buf, c0):
    rows = buf.shape[0] - 8
    for s in range(1, 8):
        sh[s, 0:rows, :] = buf[s:s + rows, pl.ds(c0, 128)]


def _window64(buf, sh, c0, start):
    s = start % 8
    if s == 0:
        return buf[start:start + 64, pl.ds(c0, 128)]
    return sh[s, start - s:start - s + 64, :]


def _conv_taps(acc_init, w_ref, buf, sh, row0, c0, offset_of_tap):
    acc = acc_init
    for j in range(CONV_K):
        acc = acc + w_ref[j:j + 1, pl.ds(c0, 128)] * _window64(buf, sh, c0, row0 + offset_of_tap(j))
    return acc


def _conv_fwd(proj, conv_w, conv_b, ln_g, ln_b, tm):
    t = proj.shape[0]
    hb = tm // HALO

    def body(a_ref, b_ref, z_ref, ah_ref, bh_ref, w_ref, cb_ref, lg_ref, lb_ref, u1_ref, pc_ref, ubuf, sh):
        i = pl.program_id(0)
        u0h = ah_ref[...] * _sig(bh_ref[...])
        ubuf[0:HALO, :] = jnp.where(i > 0, u0h, 0.0)
        ubuf[HALO:HALO + tm, :] = a_ref[...] * _sig(b_ref[...])

        def col(ci, carry):
            c0 = pl.multiple_of(ci * 128, 128)
            _shift_copies(sh, ubuf, c0)
            for rc in range(tm // 64):
                init = jnp.zeros((64, 128), F32)
                acc = _conv_taps(init, w_ref, ubuf, sh, rc * 64, c0, lambda j: HALO - (CONV_K - 1) + j)
                u1_ref[rc * 64:(rc + 1) * 64, pl.ds(c0, 128)] = acc + cb_ref[:, pl.ds(c0, 128)]
            return carry

        lax.fori_loop(0, D // 128, col, 0)
        u1 = u1_ref[...]
        mu = jnp.mean(u1, axis=-1, keepdims=True)
        xc = u1 - mu
        var = jnp.mean(xc * xc, axis=-1, keepdims=True)
        u2 = xc * lax.rsqrt(var + EPS) * lg_ref[...] + lb_ref[...]
        z = z_ref[...]
        pc_ref[...] = (u2 * _sig(u2) * (z * _sig(z))).astype(BF16)

    halo = pl.BlockSpec((HALO, D), lambda i: (jnp.maximum(i * hb - 1, 0), 0))
    halo_b = pl.BlockSpec((HALO, D), lambda i: (jnp.maximum(i * hb - 1, 0), 1))
    return pl.pallas_call(
        body, name="conv_fwd", grid=(t // tm,),
        in_specs=[_rows(tm, D, 0), _rows(tm, D, 1), _rows(tm, D, 2), halo, halo_b,
                  _full((CONV_KP, D)), _full((1, D)), _full((1, D)), _full((1, D))],
        out_specs=[_rows(tm, D), _rows(tm, D)],
        out_shape=[jax.ShapeDtypeStruct((t, D), F32), jax.ShapeDtypeStruct((t, D), BF16)],
        scratch_shapes=[pltpu.VMEM((HALO + tm, D), F32), pltpu.VMEM((8, HALO + tm, 128), F32)],
    )(proj, proj, proj, proj, proj, conv_w, conv_b, ln_g, ln_b)


def _band_masks_t(has_prev):
    key = lax.broadcasted_iota(jnp.int32, (BLK, BLK), 0)
    qry = lax.broadcasted_iota(jnp.int32, (BLK, BLK), 1)
    return jnp.logical_and(key >= qry, has_prev), key <= qry


def _head_lanes(pair, hh):
    lane = lax.broadcasted_iota(jnp.int32, pair.shape, 1)
    return jnp.where((lane >= hh * HEAD) & (lane < (hh + 1) * HEAD), pair, jnp.zeros_like(pair))


def _pair_mask(has_prev):
    mask_p, mask_c = _band_masks_t(has_prev)
    both = jnp.concatenate([mask_p, mask_c], axis=0)
    return jnp.concatenate([both, both], axis=1)


def _query_pair(pair):
    return jnp.concatenate([_head_lanes(pair, 0), _head_lanes(pair, 1)], axis=0)


def _key_pair(ref, prev, cur):
    return jnp.concatenate([ref[pl.ds(prev, BLK), :], ref[pl.ds(cur, BLK), :]], axis=0)


def _own_head(both):
    return jnp.concatenate([both[0:HEAD, 0:BLK], both[HEAD:2 * HEAD, BLK:2 * BLK]], axis=0)


def _store_transposed(dst, base, src):
    for j in range(TILE // BLK):
        dst[base // BLK + j] = src[j * BLK:(j + 1) * BLK, :].T.astype(BF16)


class _Dilated:
    def __init__(self, dil):
        self.dil = dil
        self.per = TILE // dil
        self.nbr = self.per // BLK

    def spread(self, dst, base, src_ref, dtype):
        for r in range(self.dil):
            rows = src_ref[pl.ds(r, self.per, stride=self.dil), :] if self.dil > 1 else src_ref[...]
            dst[pl.ds(pl.multiple_of(base + r * self.per, BLK), self.per), :] = rows.astype(dtype)

    def gather(self, dst_ref, src, base):
        for r in range(self.dil):
            rows = src[pl.ds(pl.multiple_of(base + r * self.per, BLK), self.per), :]
            if self.dil > 1:
                dst_ref[pl.ds(r, self.per, stride=self.dil), :] = rows
            else:
                dst_ref[...] = rows

    def block_rows(self, b, i, cur, prv):
        n = b % self.nbr
        row = pl.multiple_of(b * BLK, BLK)
        has_prev = jnp.logical_or(n > 0, i > 0)
        prev = jnp.where(n > 0, cur + row - BLK, jnp.where(i > 0, prv + row + (self.nbr - 1) * BLK, cur + row))
        return row, pl.multiple_of(prev, BLK), has_prev


def _slots(i):
    return pl.multiple_of((i % 2) * TILE, TILE), pl.multiple_of(((i + 1) % 2) * TILE, TILE)


def _nt(a, b):
    return lax.dot_general(a, b, (((1,), (1,)), ((), ())), preferred_element_type=F32)


def _qkv_specs(gi, clamp_to=None):
    def spec(col0):
        def imap(hp, i):
            return (i if clamp_to is None else jnp.minimum(i, clamp_to), (col0 + gi * ATT) // 128 + hp)
        return pl.BlockSpec((TILE, 128), imap)
    return [spec(C_Q), spec(C_K), spec(C_V)]


def _att_fwd(proj, tables, gi, dil):
    t = proj.shape[0]
    dl = _Dilated(dil)

    def body(q_ref, k_ref, v_ref, c_ref, lo_ref, hi_ref, o_ref, lse_ref, tmp, qd, kd, vt, od, ld):
        i = pl.program_id(1)
        cur, prv = _slots(i)
        cs, lo, hi = c_ref[...], lo_ref[...], hi_ref[...]
        tmp[...] = _rope_apply(q_ref[...], cs, lo, hi) * SM_SCALE
        dl.spread(qd, 0, tmp, BF16)
        tmp[...] = _rope_apply(k_ref[...], cs, lo, hi)
        dl.spread(kd, cur, tmp, BF16)
        dl.spread(tmp, 0, v_ref, F32)
        _store_transposed(vt, cur, tmp)

        def block(b, carry):
            row, prev, has_prev = dl.block_rows(b, i, cur, prv)
            s = jnp.where(_pair_mask(has_prev), _nt(_key_pair(kd, prev, cur + row), _query_pair(qd[pl.ds(row, BLK), :])), NEG_INF)
            mx = jnp.max(s, axis=0, keepdims=True)
            p = jnp.exp(s - mx)
            den = jnp.sum(p, axis=0, keepdims=True)
            v_t = jnp.concatenate([vt[prev // BLK], vt[(cur + row) // BLK]], axis=1)
            acc = jnp.dot(v_t, p.astype(BF16), preferred_element_type=F32) / den
            lse = mx + jnp.log(den)
            od[pl.ds(row, BLK), :] = _own_head(acc).T
            ld[pl.ds(row, BLK), :] = _own_head(jnp.broadcast_to(lse, (2 * HEAD, 2 * BLK))).T
            return carry

        lax.fori_loop(0, TILE // BLK, block, 0, unroll=True)
        dl.gather(o_ref, od, 0)
        dl.gather(lse_ref, ld, 0)

    tab = pl.BlockSpec((TILE, 128), lambda hp, i: (i, 0))
    out_spec = pl.BlockSpec((TILE, 128), lambda hp, i: (i, hp))
    return pl.pallas_call(
        body, name=f"att_fwd_g{gi}", grid=(ATT // 128, t // TILE),
        in_specs=_qkv_specs(gi) + [tab] * 3,
        out_specs=[out_spec] * 2, out_shape=[jax.ShapeDtypeStruct((t, ATT), F32)] * 2,
        scratch_shapes=[pltpu.VMEM((TILE, 128), F32), pltpu.VMEM((TILE, 128), BF16), pltpu.VMEM((2 * TILE, 128), BF16),
                        pltpu.VMEM((2 * TILE // BLK, 128, BLK), BF16), pltpu.VMEM((TILE, 128), F32), pltpu.VMEM((TILE, 128), F32)],
    )(proj, proj, proj, *tables)


def _att_combine(parts, proj, tm):
    t = proj.shape[0]

    def body(o0, l0, o1, l1, o2, l2, z_ref, att_ref, lse_ref, pa_ref):
        m_all = jnp.maximum(jnp.maximum(l0[...], l1[...]), l2[...])
        w0, w1, w2 = jnp.exp(l0[...] - m_all), jnp.exp(l1[...] - m_all), jnp.exp(l2[...] - m_all)
        den = w0 + w1 + w2
        att = (w0 * o0[...] + w1 * o1[...] + w2 * o2[...]) / den
        z = z_ref[...]
        att_ref[...] = att
        lse_ref[...] = m_all + jnp.log(den)
        pa_ref[...] = (att * (z * _sig(z))).astype(BF16)

    spec = _rows(tm, ATT)
    return pl.pallas_call(
        body, name="att_combine", grid=(t // tm,),
        in_specs=[spec] * 6 + [_rows(tm, ATT, C_ZA // ATT)],
        out_specs=[spec] * 3,
        out_shape=[jax.ShapeDtypeStruct((t, ATT), F32)] * 2 + [jax.ShapeDtypeStruct((t, ATT), BF16)],
    )(*parts, proj)


def _att_bwd(proj, tables, datt, dsum, lse, gi, dil):
    t = proj.shape[0]
    nt = t // TILE
    dl = _Dilated(dil)

    def body(q_ref, k_ref, v_ref, c_ref, lo_ref, hi_ref, cl_ref, lol_ref, hil_ref, do_ref, ds_ref, lse_ref,
             dq_ref, dk_ref, dv_ref, tmp, qd, kd, vd, dod, dsd, lsd, dqd, dkd, dvd, kt):
        i = pl.program_id(1)
        cur, prv = _slots(i)

        @pl.when(i < nt)
        def _():
            cs, lo, hi = c_ref[...], lo_ref[...], hi_ref[...]
            tmp[...] = _rope_apply(q_ref[...], cs, lo, hi) * SM_SCALE
            dl.spread(qd, 0, tmp, BF16)
            tmp[...] = _rope_apply(k_ref[...], cs, lo, hi)
            dl.spread(kd, cur, tmp, BF16)
            dl.spread(dqd, 0, tmp, F32)
            _store_transposed(kt, cur, dqd)
            dl.spread(vd, cur, v_ref, BF16)
            dl.spread(dod, 0, do_ref, BF16)
            dl.spread(dsd, 0, ds_ref, F32)
            dl.spread(lsd, 0, lse_ref, F32)
            dkd[pl.ds(cur, TILE), :] = jnp.zeros((TILE, 128), F32)
            dvd[pl.ds(cur, TILE), :] = jnp.zeros((TILE, 128), F32)

            def block(b, carry):
                row, prev, has_prev = dl.block_rows(b, i, cur, prv)
                q_pair, do_pair = _query_pair(qd[pl.ds(row, BLK), :]), _query_pair(dod[pl.ds(row, BLK), :])
                k_pair, v_pair = _key_pair(kd, prev, cur + row), _key_pair(vd, prev, cur + row)
                ds_t, ls_t = dsd[pl.ds(row, BLK), :].T, lsd[pl.ds(row, BLK), :].T
                lse = jnp.concatenate([ls_t[0:1, :], ls_t[HEAD:HEAD + 1, :]], axis=1)
                dsm = jnp.concatenate([ds_t[0:1, :], ds_t[HEAD:HEAD + 1, :]], axis=1)
                p = jnp.exp(jnp.where(_pair_mask(has_prev), _nt(k_pair, q_pair), NEG_INF) - lse)
                ds = (p * (_nt(v_pair, do_pair) - dsm)).astype(BF16)
                k_t = jnp.concatenate([kt[prev // BLK], kt[(cur + row) // BLK]], axis=1)
                dqd[pl.ds(row, BLK), :] = _own_head(jnp.dot(k_t, ds, preferred_element_type=F32)).T * SM_SCALE
                dk = jnp.dot(ds, q_pair, preferred_element_type=F32)
                dv = jnp.dot(p.astype(BF16), do_pair, preferred_element_type=F32)
                dkd[pl.ds(cur + row, BLK), :] += dk[BLK:2 * BLK, :]
                dvd[pl.ds(cur + row, BLK), :] += dv[BLK:2 * BLK, :]
                dkd[pl.ds(prev, BLK), :] += dk[0:BLK, :]
                dvd[pl.ds(prev, BLK), :] += dv[0:BLK, :]
                return carry

            lax.fori_loop(0, TILE // BLK, block, 0, unroll=True)
            dl.gather(tmp, dqd, 0)
            dq_ref[...] = _rope_apply(tmp[...], cs, -lo, -hi).astype(BF16)

        @pl.when(i > 0)
        def _():
            dl.gather(tmp, dkd, prv)
            dk_ref[...] = _rope_apply(tmp[...], cl_ref[...], -lol_ref[...], -hil_ref[...]).astype(BF16)
            dl.gather(tmp, dvd, prv)
            dv_ref[...] = tmp[...].astype(BF16)

    now = lambda col: pl.BlockSpec((TILE, 128), lambda hp, i: (jnp.minimum(i, nt - 1), col(hp)))
    lag = lambda col: pl.BlockSpec((TILE, 128), lambda hp, i: (jnp.maximum(i - 1, 0), col(hp)))
    first, pair = (lambda hp: 0), (lambda hp: hp)
    return pl.pallas_call(
        body, name=f"att_bwd_g{gi}", grid=(ATT // 128, nt + 1),
        in_specs=_qkv_specs(gi, nt - 1) + [now(first)] * 3 + [lag(first)] * 3 + [now(pair)] * 3,
        out_specs=[now(pair), lag(pair), lag(pair)],
        out_shape=[jax.ShapeDtypeStruct((t, ATT), BF16)] * 3,
        scratch_shapes=[pltpu.VMEM((TILE, 128), F32), pltpu.VMEM((TILE, 128), BF16), pltpu.VMEM((2 * TILE, 128), BF16),
                        pltpu.VMEM((2 * TILE, 128), BF16), pltpu.VMEM((TILE, 128), BF16), pltpu.VMEM((TILE, 128), F32),
                        pltpu.VMEM((TILE, 128), F32), pltpu.VMEM((TILE, 128), F32), pltpu.VMEM((2 * TILE, 128), F32),
                        pltpu.VMEM((2 * TILE, 128), F32), pltpu.VMEM((2 * TILE // BLK, 128, BLK), BF16)],
    )(proj, proj, proj, *tables, *tables, datt, dsum, lse)


def _merge_fwd(proj, y_conv, y_att, tm):
    t = proj.shape[0]

    def body(gc_ref, ga_ref, yc_ref, ya_ref, o_ref):
        o_ref[...] = (_sig(gc_ref[...]) * yc_ref[...] + _sig(ga_ref[...]) * ya_ref[...]).astype(BF16)

    return pl.pallas_call(body, name="merge_fwd", grid=(t // tm,),
                          in_specs=[_rows(tm, D, C_GC // D), _rows(tm, D, C_GA // D), _rows(tm, D), _rows(tm, D)],
                          out_specs=_rows(tm, D), out_shape=jax.ShapeDtypeStruct((t, D), BF16))(proj, proj, y_conv, y_att)


def _acc_rows(ref, i, val):
    @pl.when(i == 0)
    def _():
        ref[...] = jnp.zeros_like(ref)

    ref[...] += jnp.sum(val, axis=0, keepdims=True)


def _loss_head(x, o, mod, final_g, target, tm):
    t = x.shape[0]

    def body(x_ref, o_ref, mod_ref, fg_ref, tg_ref, dout_ref, do_ref, sq_ref, gfg_ref, dgate_ref):
        i = pl.program_id(0)
        gate = mod_ref[:, 2 * D:3 * D]
        ov = o_ref[...]
        out = x_ref[...] + gate * ov
        r = lax.rsqrt(jnp.mean(out * out, axis=-1, keepdims=True) + EPS)
        yn = out * r
        diff = yn * fg_ref[...] - tg_ref[...]
        dy = diff * (1.0 / D)
        gy = dy * fg_ref[...]
        dout = r * (gy - yn * jnp.mean(gy * yn, axis=-1, keepdims=True))
        dout_ref[...] = dout
        do_ref[...] = (dout * gate).astype(BF16)
        _acc_rows(sq_ref, i, diff * diff)
        _acc_rows(gfg_ref, i, dy * yn)
        _acc_rows(dgate_ref, i, dout * ov)

    vec = _full((1, D))
    return pl.pallas_call(
        body, name="loss_head", grid=(t // tm,),
        in_specs=[_rows(tm, D), _rows(tm, D), _full((1, 3 * D)), vec, _rows(tm, D)],
        out_specs=[_rows(tm, D), _rows(tm, D), vec, vec, vec],
        out_shape=[jax.ShapeDtypeStruct((t, D), F32), jax.ShapeDtypeStruct((t, D), BF16)] + [jax.ShapeDtypeStruct((1, D), F32)] * 3,
    )(x, o, mod, final_g, target)


def _merge_bwd(dmerged, proj, y_conv, y_att, tm):
    t = proj.shape[0]

    def body(dm_ref, gc_ref, ga_ref, yc_ref, ya_ref, dyc_ref, dya_ref, dp_ref):
        dm = dm_ref[...]
        sc, sa = _sig(gc_ref[...]), _sig(ga_ref[...])
        dyc_ref[...] = (dm * sc).astype(BF16)
        dya_ref[...] = (dm * sa).astype(BF16)
        dp_ref[:, 0:D] = (dm * yc_ref[...] * sc * (1.0 - sc)).astype(BF16)
        dp_ref[:, D:2 * D] = (dm * ya_ref[...] * sa * (1.0 - sa)).astype(BF16)

    return pl.pallas_call(
        body, name="merge_bwd", grid=(t // tm,),
        in_specs=[_rows(tm, D), _rows(tm, D, C_GC // D), _rows(tm, D, C_GA // D), _rows(tm, D), _rows(tm, D)],
        out_specs=[_rows(tm, D), _rows(tm, D), _rows(tm, 2 * D, C_GC // (2 * D))],
        out_shape=[jax.ShapeDtypeStruct((t, D), BF16), jax.ShapeDtypeStruct((t, D), BF16), jax.ShapeDtypeStruct((t, N_COL), BF16)],
    )(dmerged, proj, proj, y_conv, y_att)


def _att_pre_bwd(dpa, proj, att, dproj, tm):
    t = proj.shape[0]

    def body(dpa_ref, z_ref, att_ref, dp_in, datt_ref, ds_ref, dp_ref):
        del dp_in
        z, dpa_v, att_v = z_ref[...], dpa_ref[...], att_ref[...]
        s = _sig(z)
        datt = dpa_v * (z * s)
        datt_ref[...] = datt
        dp_ref[...] = (dpa_v * att_v * _dsilu(z, s)).astype(BF16)
        prod = datt * att_v
        for h in range(ATT // HEAD):
            sl = slice(h * HEAD, (h + 1) * HEAD)
            ds_ref[:, sl] = jnp.broadcast_to(jnp.sum(prod[:, sl], axis=-1, keepdims=True), (tm, HEAD))

    return pl.pallas_call(
        body, name="att_pre_bwd", grid=(t // tm,),
        in_specs=[_rows(tm, ATT), _rows(tm, ATT, C_ZA // ATT), _rows(tm, ATT), ANY],
        out_specs=[_rows(tm, ATT), _rows(tm, ATT), _rows(tm, ATT, C_ZA // ATT)],
        out_shape=[jax.ShapeDtypeStruct((t, ATT), F32), jax.ShapeDtypeStruct((t, ATT), F32), jax.ShapeDtypeStruct((t, N_COL), BF16)],
        input_output_aliases={3: 2},
    )(dpa, proj, att, dproj)


def _place_qkv(parts, dproj, col_block, tm, name):
    t = dproj.shape[0]

    def body(p0, p1, p2, dp_in, dp_ref):
        del dp_in
        for g, ref in enumerate((p0, p1, p2)):
            dp_ref[:, g * ATT:(g + 1) * ATT] = ref[...]

    return pl.pallas_call(
        body, name=name, grid=(t // tm,),
        in_specs=[_rows(tm, ATT)] * 3 + [ANY],
        out_specs=_rows(tm, QKV, col_block), out_shape=jax.ShapeDtypeStruct((t, N_COL), BF16),
        input_output_aliases={3: 0},
    )(*parts, dproj)


def _conv_bwd_rows(dpc, proj, u1, ln_g, ln_b, dproj, tm):
    t = proj.shape[0]

    def body(dpc_ref, z_ref, u1_ref, lg_ref, lb_ref, dp_in, du1_ref, dp_ref, dlg_ref, dlb_ref, dcb_ref):
        del dp_in
        i = pl.program_id(0)
        u1v = u1_ref[...]
        mu = jnp.mean(u1v, axis=-1, keepdims=True)
        xc = u1v - mu
        r = lax.rsqrt(jnp.mean(xc * xc, axis=-1, keepdims=True) + EPS)
        uhat = xc * r
        u2 = uhat * lg_ref[...] + lb_ref[...]
        s2 = _sig(u2)
        z = z_ref[...]
        sz = _sig(z)
        dpc_v = dpc_ref[...]
        dp_ref[...] = (dpc_v * (u2 * s2) * _dsilu(z, sz)).astype(BF16)
        du2 = dpc_v * (z * sz) * _dsilu(u2, s2)
        duhat = du2 * lg_ref[...]
        du1 = r * (duhat - jnp.mean(duhat, axis=-1, keepdims=True) - uhat * jnp.mean(duhat * uhat, axis=-1, keepdims=True))
        du1_ref[...] = du1
        _acc_rows(dlg_ref, i, du2 * uhat)
        _acc_rows(dlb_ref, i, du2)
        _acc_rows(dcb_ref, i, du1)

    vec = _full((1, D))
    return pl.pallas_call(
        body, name="conv_bwd_rows", grid=(t // tm,),
        in_specs=[_rows(tm, D), _rows(tm, D, C_ZC // D), _rows(tm, D), vec, vec, ANY],
        out_specs=[_rows(tm, D), _rows(tm, D, C_ZC // D), vec, vec, vec],
        out_shape=[jax.ShapeDtypeStruct((t, D), F32), jax.ShapeDtypeStruct((t, N_COL), BF16)] + [jax.ShapeDtypeStruct((1, D), F32)] * 3,
        input_output_aliases={5: 1},
    )(dpc, proj, u1, ln_g, ln_b, dproj)


def _conv_bwd_taps(du1, proj, conv_w, dproj, tm):
    t = proj.shape[0]
    hb = tm // HALO
    last = t // HALO - 1

    def body(du_ref, duh_ref, a_ref, b_ref, ah_ref, bh_ref, w_ref, dp_in, dp_ref, dw_ref, dbuf, ubuf, g0, shd, shu):
        del dp_in
        i = pl.program_id(0)
        a, sb = a_ref[...], _sig(b_ref[...])
        ubuf[0:HALO, :] = jnp.where(i > 0, ah_ref[...] * _sig(bh_ref[...]), 0.0)
        ubuf[HALO:HALO + tm, :] = a * sb
        dbuf[0:tm, :] = du_ref[...]
        dbuf[tm:tm + HALO, :] = jnp.where(i < pl.num_programs(0) - 1, duh_ref[...], 0.0)

        @pl.when(i == 0)
        def _():
            dw_ref[...] = jnp.zeros_like(dw_ref)

        def col(ci, carry):
            c0 = pl.multiple_of(ci * 128, 128)
            _shift_copies(shd, dbuf, c0)
            _shift_copies(shu, ubuf, c0)
            for rc in range(tm // 64):
                g0[rc * 64:(rc + 1) * 64, pl.ds(c0, 128)] = _conv_taps(
                    jnp.zeros((64, 128), F32), w_ref, dbuf, shd, rc * 64, c0, lambda j: CONV_K - 1 - j)
            for j in range(CONV_K):
                part = jnp.zeros((8, 128), F32)
                for rc in range(tm // 64):
                    off = rc * 64 + HALO - (CONV_K - 1) + j
                    prod = dbuf[rc * 64:(rc + 1) * 64, pl.ds(c0, 128)] * _window64(ubuf, shu, c0, off)
                    part = part + jnp.sum(prod.reshape(8, 8, 128), axis=0)
                dw_ref[j:j + 1, pl.ds(c0, 128)] += jnp.sum(part, axis=0, keepdims=True)
            return carry

        lax.fori_loop(0, D // 128, col, 0)
        du0 = g0[...]
        dp_ref[:, 0:D] = (du0 * sb).astype(BF16)
        dp_ref[:, D:2 * D] = (du0 * a * sb * (1.0 - sb)).astype(BF16)

    prev = lambda col: pl.BlockSpec((HALO, D), lambda i: (jnp.maximum(i * hb - 1, 0), col))
    nxt = pl.BlockSpec((HALO, D), lambda i: (jnp.minimum((i + 1) * hb, last), 0))
    return pl.pallas_call(
        body, name="conv_bwd_taps", grid=(t // tm,),
        in_specs=[_rows(tm, D), nxt, _rows(tm, D, 0), _rows(tm, D, 1), prev(0), prev(1), _full((CONV_KP, D)), ANY],
        out_specs=[_rows(tm, 2 * D, 0), _full((CONV_KP, D))],
        out_shape=[jax.ShapeDtypeStruct((t, N_COL), BF16), jax.ShapeDtypeStruct((CONV_KP, D), F32)],
        scratch_shapes=[pltpu.VMEM((tm + HALO, D), F32), pltpu.VMEM((HALO + tm, D), F32), pltpu.VMEM((tm, D), F32),
                        pltpu.VMEM((8, HALO + tm, 128), F32), pltpu.VMEM((8, HALO + tm, 128), F32)],
        input_output_aliases={7: 0},
    )(du1, du1, proj, proj, proj, proj, conv_w, dproj)


def _prenorm_bwd(dh, x, dout, mod, norm_g, tm):
    t = x.shape[0]

    def body(dh_ref, x_ref, dout_ref, mod_ref, g_ref, gx_ref, dshift_ref, dscale_ref, dg_ref):
        i = pl.program_id(0)
        xv, dhv = x_ref[...], dh_ref[...]
        r = lax.rsqrt(jnp.mean(xv * xv, axis=-1, keepdims=True) + EPS)
        xn = xv * r
        one_scale = 1.0 + mod_ref[:, D:2 * D]
        dxn = dhv * (g_ref[...] * one_scale)
        gx_ref[...] = r * (dxn - xn * jnp.mean(dxn * xn, axis=-1, keepdims=True)) + dout_ref[...]
        _acc_rows(dshift_ref, i, dhv)
        _acc_rows(dscale_ref, i, dhv * xn * g_ref[...])
        _acc_rows(dg_ref, i, dhv * xn * one_scale)

    vec = _full((1, D))
    return pl.pallas_call(
        body, name="prenorm_bwd", grid=(t // tm,),
        in_specs=[_rows(tm, D), _rows(tm, D), _rows(tm, D), _full((1, 3 * D)), vec],
        out_specs=[_rows(tm, D), vec, vec, vec],
        out_shape=[jax.ShapeDtypeStruct((t, D), F32)] + [jax.ShapeDtypeStruct((1, D), F32)] * 3,
    )(dh, x, dout, mod, norm_g)


def _sum_devices(gathered):
    w = gathered.shape[-1]

    def body(g_ref, o_ref):
        acc = g_ref[0]
        for j in range(1, N_DEV):
            acc = acc + g_ref[j]
        o_ref[...] = acc

    return pl.pallas_call(body, name="sum_devices", grid=(1,), in_specs=[_full(gathered.shape)], out_specs=_full((1, w)),
                          out_shape=jax.ShapeDtypeStruct((1, w), F32))(gathered)


def _rope_tables(positions):
    half = HEAD // 8
    t = positions.shape[-1]
    inv_freq = ROPE_THETA ** (-(jnp.arange(half, dtype=F32) * 2.0 / (2 * half)))
    ang = positions.reshape(t, 1).astype(F32) * inv_freq
    cos, sin = jnp.cos(ang), jnp.sin(ang)
    zeros = lambda n: jnp.zeros((t, n), F32)
    c64 = jnp.concatenate([cos, cos, jnp.ones((t, HEAD - 2 * half), F32)], axis=1)
    lo64 = jnp.concatenate([-sin, zeros(HEAD - half)], axis=1)
    hi64 = jnp.concatenate([zeros(half), sin, zeros(HEAD - 2 * half)], axis=1)
    return tuple(jnp.tile(a, (1, 2)) for a in (c64, lo64, hi64))


def kernel(x, c, positions, norm_g, w_ada, b_ada, w_in, conv_w, conv_b, conv_ln_g, conv_ln_b, w_conv_out, w_att_out, w_o, final_g, loss_target, m_norm_g, m_w_ada, m_b_ada, m_w_in, m_conv_w, m_conv_b, m_conv_ln_g, m_conv_ln_b, m_w_conv_out, m_w_att_out, m_w_o, m_final_g, v_norm_g, v_w_ada, v_b_ada, v_w_in, v_conv_w, v_conv_b, v_conv_ln_g, v_conv_ln_b, v_w_conv_out, v_w_att_out, v_w_o, v_final_g):
    me = 4 * lax.axis_index("x") + 2 * lax.axis_index("y") + lax.axis_index("c")
    x2, tgt = x[0], loss_target[0]
    t = x2.shape[0]
    te = 512 if t % 512 == 0 else 256
    tcv = 256
    tmm = 1024 if t % 1024 == 0 else 256
    n_ada = w_ada.shape[-1]

    pad_taps = lambda a: jnp.pad(a[0], ((0, CONV_KP - CONV_K), (0, 0)))
    shards = (_cast_bf16(w_in[0], "cast_w_in"), _cast_bf16(w_conv_out[0], "cast_w_conv_out"),
              _cast_bf16(w_att_out[0], "cast_w_att_out"), _cast_bf16(w_o[0], "cast_w_o"), pad_taps(conv_w))
    block_of = lambda relations: jnp.bitwise_xor(me, jnp.array(relations, jnp.int32))

    c_all = _allgather_small(c, "gather_c").reshape(N_DEV, D)
    b_ada_l = lax.dynamic_slice(b_ada, (0, me * n_ada), (1, n_ada))
    parts = _allgather_small(_mod_part(c_all, w_ada[0], b_ada_l), "gather_mod")
    mod = lax.dynamic_slice(parts, (0, me, 0), (N_DEV, 1, n_ada)).reshape(1, N_DEV * n_ada)

    h = _prenorm(x2, mod, norm_g, te)
    proj, w_in_f, w_co_f, w_ao_f, w_o_f, conv_w_f = _proj_gather(h, shards, block_of(GATHER_ORDER), tmm)
    u1, pc = _conv_fwd(proj, conv_w_f, conv_b, conv_ln_g, conv_ln_b, tcv)
    tables = _rope_tables(positions)
    parts_att = []
    for gi, dil in GROUPS:
        parts_att += _att_fwd(proj, tables, gi, dil)
    att, lse, pa = _att_combine(parts_att, proj, te)
    y_conv = _matmul(pc, w_co_f, tm=tmm, tn=D, tk=D, name="y_conv")
    y_att = _matmul(pa, w_ao_f, tm=tmm, tn=D, tk=ATT, name="y_att")
    merged = _merge_fwd(proj, y_conv, y_att, te)
    o = _matmul(merged, w_o_f, tm=tmm, tn=D, tk=D, name="out_proj")
    dout, do, sq_sum, g_final, d_gate = _loss_head(x2, o, mod, final_g.reshape(1, D), tgt, te)

    dmerged = _matmul(do, w_o_f, tb=True, tm=tmm, tn=D, tk=D, name="d_merged")
    dw_o = _matmul(merged, do, ta=True, out_dtype=BF16, tm=D, tn=D, tk=512, name="dw_o")
    dyc, dya, dproj = _merge_bwd(dmerged, proj, y_conv, y_att, te)
    dpc = _matmul(dyc, w_co_f, tb=True, tm=tmm, tn=D, tk=D, name="d_pc")
    dw_co = _matmul(pc, dyc, ta=True, out_dtype=BF16, tm=D, tn=D, tk=512, name="dw_conv_out")
    dpa = _matmul(dya, w_ao_f, tb=True, tm=tmm, tn=ATT, tk=D, name="d_pa")
    dw_ao = _matmul(pa, dya, ta=True, out_dtype=BF16, tm=ATT, tn=D, tk=512, name="dw_att_out")
    datt, dsum, dproj = _att_pre_bwd(dpa, proj, att, dproj, te)
    dqs, dks, dvs = [], [], []
    for gi, dil in GROUPS:
        dq, dk, dv = _att_bwd(proj, tables, datt, dsum, lse, gi, dil)
        dqs.append(dq), dks.append(dk), dvs.append(dv)
    dproj = _place_qkv(dqs, dproj, C_Q // QKV, te, "place_dq")
    dproj = _place_qkv(dks, dproj, C_K // QKV, te, "place_dk")
    dproj = _place_qkv(dvs, dproj, C_V // QKV, te, "place_dv")
    du1, dproj, d_ln_g, d_ln_b, d_conv_b = _conv_bwd_rows(dpc, proj, u1, conv_ln_g, conv_ln_b, dproj, te)
    dproj, dconv_w = _conv_bwd_taps(du1, proj, conv_w_f, dproj, tcv)
    dh = _matmul(dproj, w_in_f, tb=True, tm=tmm, tn=D, tk=N_COL // N_DEV, name="d_h")
    grad_x, d_shift, d_scale, d_norm_g = _prenorm_bwd(dh, x2, dout, mod, norm_g, te)

    packed = jnp.concatenate([d_shift, d_scale, d_gate, d_norm_g, d_conv_b, d_ln_g, d_ln_b, g_final, sq_sum], axis=1)
    gathered = _allgather_small(packed, "gather_partials")
    total = _sum_devices(gathered)
    seg = lambda k, n=1: total[:, k * D:(k + n) * D]
    g_b_ada, g_norm_g, g_conv_b, g_ln_g, g_ln_b, g_final_g = seg(0, 3), seg(3), seg(4), seg(5), seg(6), seg(7)
    loss = (0.5 / D) * jnp.sum(seg(8))
    dmod_all = gathered[:, 0, 0:3 * D]
    dmod_cols = lax.dynamic_slice(dmod_all, (0, me * n_ada), (N_DEV, n_ada))
    g_w_ada, d_w_ada, nm_w_ada, nv_w_ada = _w_ada_update(c_all.T, dmod_cols, w_ada[0], m_w_ada[0], v_w_ada[0])

    small = {}
    for name, g, w, m, v in (("norm_g", g_norm_g, norm_g, m_norm_g, v_norm_g), ("b_ada", g_b_ada, b_ada, m_b_ada, v_b_ada),
                             ("conv_b", g_conv_b, conv_b, m_conv_b, v_conv_b), ("conv_ln_g", g_ln_g, conv_ln_g, m_conv_ln_g, v_conv_ln_g),
                             ("conv_ln_b", g_ln_b, conv_ln_b, m_conv_ln_b, v_conv_ln_b),
                             ("final_g", g_final_g, final_g.reshape(1, D), m_final_g.reshape(1, D), v_final_g.reshape(1, D))):
        small[name] = (g,) + tuple(_adamw_small(g, w, m, v, "adamw_" + name))

    slots = _dw_in_scatter(h, dproj, (dw_co, dw_ao, dw_o, dconv_w), block_of(SCATTER_ORDER), 512)
    big = {
        "w_in": _sum_adamw(slots[0], w_in[0], m_w_in[0], v_w_in[0], 256, "adamw_w_in"),
        "w_conv_out": _sum_adamw(slots[1], w_conv_out[0], m_w_conv_out[0], v_w_conv_out[0], 128, "adamw_w_conv_out"),
        "w_att_out": _sum_adamw(slots[2], w_att_out[0], m_w_att_out[0], v_w_att_out[0], 512, "adamw_w_att_out"),
        "w_o": _sum_adamw(slots[3], w_o[0], m_w_o[0], v_w_o[0], 128, "adamw_w_o"),
        "conv_w": [r[:CONV_K] for r in _sum_adamw(slots[4], pad_taps(conv_w), pad_taps(m_conv_w), pad_taps(v_conv_w), CONV_KP, "adamw_conv_w")],
    }
    big["w_ada"] = (g_w_ada, d_w_ada, nm_w_ada, nv_w_ada)

    order = ("norm_g", "w_ada", "b_ada", "w_in", "conv_w", "conv_b", "conv_ln_g", "conv_ln_b", "w_conv_out", "w_att_out", "w_o", "final_g")
    lead = lambda name, a: a.reshape(D) if name == "final_g" else (a[None] if name in big else a)
    result = {**small, **big}
    outs = [loss, grad_x[None]]
    for field in range(4):
        outs += [lead(name, result[name][field]) for name in order]
    return tuple(outs)
```

```python
import functools

import jax
import jax.numpy as jnp
from jax import lax
from jax.experimental import pallas as pl
from jax.experimental.pallas import tpu as pltpu

F32 = jnp.float32
BF16 = jnp.bfloat16

N_DEV = 8
D = 1024
N_COL = 10240
C_A, C_B, C_ZC, C_Q, C_K, C_V, C_ZA, C_GC, C_GA = 0, 1024, 2048, 3072, 4608, 6144, 7680, 8192, 9216
QKV = 1536
ATT = 512
HEAD = 64
BLK = 128
TILE = 2048
GROUPS = ((0, 1), (1, 4), (2, 16))
CONV_K = 31
CONV_KP = 32
HALO = 32
EPS = 1e-6
NEG_INF = -1e30
ROPE_THETA = 500000.0
SM_SCALE = HEAD ** -0.5

ADAM_LR, ADAM_B1, ADAM_B2, ADAM_EPS, ADAM_WD, ADAM_STEP = 0.001, 0.9, 0.999, 1e-08, 0.01, 10

MESH = pl.DeviceIdType.MESH
ANY = pl.BlockSpec(memory_space=pl.ANY)


def _sig(v):
    return 1.0 / (1.0 + jnp.exp(-v))


def _dsilu(v, s):
    return s * (1.0 + v * (1.0 - s))


def _full(shape):
    return pl.BlockSpec(shape, lambda *_: (0,) * len(shape))


def _rows(tm, width, col=0):
    return pl.BlockSpec((tm, width), lambda i: (i, col))


def _matmul(a, b, *, ta=False, tb=False, out_dtype=F32, tm, tn, tk, name):
    m, k = (a.shape[1], a.shape[0]) if ta else a.shape
    stacked = b.ndim == 3
    if stacked:
        assert tb and b.shape[2] == tk and b.shape[0] * tk == k
        n = b.shape[1]
    else:
        n = b.shape[0] if tb else b.shape[1]
        assert (b.shape[1] if tb else b.shape[0]) == k
    assert m % tm == 0 and n % tn == 0 and k % tk == 0
    nk = k // tk
    dims = (((0 if ta else 1,), (1 if tb else 0,)), ((), ()))
    use_scratch = out_dtype != F32 and nk > 1

    def body(a_ref, b_ref, o_ref, *scratch):
        p = lax.dot_general(a_ref[...], b_ref[...], dims, preferred_element_type=F32)
        if nk == 1:
            o_ref[...] = p.astype(out_dtype)
            return
        acc = scratch[0] if use_scratch else o_ref
        kk = pl.program_id(2)

        @pl.when(kk == 0)
        def _():
            acc[...] = p

        @pl.when(kk > 0)
        def _():
            acc[...] += p

        if use_scratch:
            @pl.when(kk == nk - 1)
            def _():
                o_ref[...] = acc[...].astype(out_dtype)

    a_spec = pl.BlockSpec((tk, tm), lambda i, j, kk: (kk, i)) if ta else pl.BlockSpec((tm, tk), lambda i, j, kk: (i, kk))
    b_spec = pl.BlockSpec((tn, tk), lambda i, j, kk: (j, kk)) if tb else pl.BlockSpec((tk, tn), lambda i, j, kk: (kk, j))
    if stacked:
        b_spec = pl.BlockSpec((None, tn, tk), lambda i, j, kk: (kk, j, 0))
    return pl.pallas_call(
        body, name=name, grid=(m // tm, n // tn, nk),
        in_specs=[a_spec, b_spec],
        out_specs=pl.BlockSpec((tm, tn), lambda i, j, kk: (i, j)),
        out_shape=jax.ShapeDtypeStruct((m, n), out_dtype),
        scratch_shapes=[pltpu.VMEM((tm, tn), F32)] if use_scratch else [],
    )(a, b)


def _me_and_peers():
    x, y, c = lax.axis_index("x"), lax.axis_index("y"), lax.axis_index("c")
    me = 4 * x + 2 * y + c
    peers = []
    for k in range(1, N_DEV):
        px, py, pc = x ^ (k >> 2), y ^ ((k >> 1) & 1), c ^ (k & 1)
        peers.append(((px, py, pc), 4 * px + 2 * py + pc))
    return me, peers


def _allgather_small(v, name):
    r, c = v.shape

    def body(v_ref, out_ref, send_sems, recv_sems):
        me, peers = _me_and_peers()
        out_ref[me] = v_ref[...]
        copies = []
        for k, (dev, _) in enumerate(peers):
            cp = pltpu.make_async_remote_copy(src_ref=v_ref, dst_ref=out_ref.at[me], send_sem=send_sems.at[k],
                                              recv_sem=recv_sems.at[k], device_id=dev, device_id_type=MESH)
            cp.start()
            copies.append(cp)
        for k, (dev, idx) in enumerate(peers):
            pltpu.make_async_remote_copy(src_ref=v_ref, dst_ref=out_ref.at[idx], send_sem=send_sems.at[k],
                                         recv_sem=recv_sems.at[k], device_id=dev, device_id_type=MESH).wait_recv()
        for cp in copies:
            cp.wait_send()

    return pl.pallas_call(
        body, name=name,
        in_specs=[pl.BlockSpec(memory_space=pltpu.VMEM)],
        out_specs=pl.BlockSpec(memory_space=pltpu.VMEM),
        out_shape=jax.ShapeDtypeStruct((N_DEV, r, c), v.dtype),
        scratch_shapes=[pltpu.SemaphoreType.DMA((N_DEV - 1,)), pltpu.SemaphoreType.DMA((N_DEV - 1,))],
    )(v)


def _window(ref, kind, idx, size):
    if kind == "block":
        return ref.at[idx]
    start = pl.multiple_of(idx * size, size)
    if kind == "rows":
        return ref.at[pl.ds(start, size), :]
    return ref.at[:, pl.ds(start, size)]


_BIG = (("cols", N_COL // N_DEV), ("rows", D // N_DEV), ("cols", D // N_DEV), ("rows", D // N_DEV), ("cols", D // N_DEV))
_GATHERED = (("block", 1),) + _BIG[1:]


GATHER_ORDER = (0, 1, 2, 4, 3, 5, 6, 7)
W_IN_DIRECT = (1, 2, 4, 6)
SCATTER_ORDER = (7, 6, 5, 4, 3, 2, 1, 0)
W_IN_SLOT = {0: 0, 1: 1, 2: 2, 4: 3, 6: 4}


def _proj_gather(h, shards, order, tm):
    t = h.shape[0]
    nt = len(shards)
    n_blk = N_COL // N_DEV
    full_shapes = []
    for s, (kind, size) in zip(shards, _GATHERED):
        full_shapes.append(jax.ShapeDtypeStruct({"block": (N_DEV,) + s.shape, "rows": (s.shape[0] * N_DEV, s.shape[1]),
                                                 "cols": (s.shape[0], s.shape[1] * N_DEV)}[kind], s.dtype))
    last = (N_DEV - 1, t // tm - 1)

    def body(order_ref, h_ref, *refs):
        src, proj_ref, dst = refs[:nt], refs[nt], refs[nt + 1:2 * nt + 1]
        w_all, send_sems, recv_sems, local_sems, keep_sems = refs[2 * nt + 1:]
        j, i = pl.program_id(0), pl.program_id(1)
        me, peers = _me_and_peers()

        def landing(tn, idx):
            kind, size = _GATHERED[tn]
            return w_all.at[idx] if tn == 0 else _window(dst[tn], kind, idx, size)

        def local(tn):
            return pltpu.make_async_copy(src[tn], landing(tn, me), local_sems.at[tn])

        def remote(tn, k, block_of):
            dev, idx = peers[k - 1]
            return pltpu.make_async_remote_copy(src_ref=src[tn], dst_ref=landing(tn, me if block_of == "mine" else idx),
                                                send_sem=send_sems.at[tn, k - 1], recv_sem=recv_sems.at[tn, k - 1],
                                                device_id=dev, device_id_type=MESH)

        def forward(k):
            block = w_all.at[peers[k - 1][1]]
            return pltpu.make_async_remote_copy(src_ref=block, dst_ref=block, send_sem=send_sems.at[0, k], recv_sem=recv_sems.at[0, k],
                                                device_id=peers[0][0], device_id_type=MESH)

        def keep(step):
            blk = order_ref[step]
            return pltpu.make_async_copy(w_all.at[blk], dst[0].at[blk], keep_sems.at[step])

        @pl.when((j == 0) & (i == 0))
        def _():
            for tn in range(nt):
                local(tn).start()
                for k in GATHER_ORDER[1:]:
                    if tn > 0 or k in W_IN_DIRECT:
                        remote(tn, k, "mine").start()

        @pl.when(i == 0)
        def _():
            for step, k in enumerate(GATHER_ORDER):
                @pl.when(j == step)
                def _():
                    if k == 0:
                        local(0).wait()
                    else:
                        remote(0, k, "theirs").wait_recv()
                        if k in W_IN_DIRECT and k > 1:
                            forward(k).start()
                    keep(step).start()

        proj_ref[...] = jnp.dot(h_ref[...], w_all[order_ref[j]], preferred_element_type=F32)

        @pl.when((j == last[0]) & (i == last[1]))
        def _():
            for step in range(N_DEV):
                keep(step).wait()
            for tn in range(1, nt):
                local(tn).wait()
                for k in range(1, N_DEV):
                    remote(tn, k, "theirs").wait_recv()
            for tn in range(nt):
                for k in range(1, N_DEV):
                    if tn > 0 or k in W_IN_DIRECT:
                        remote(tn, k, "mine").wait_send()
                    else:
                        forward(k - 1).wait_send()

    grid_spec = pltpu.PrefetchScalarGridSpec(
        num_scalar_prefetch=1, grid=(N_DEV, t // tm),
        in_specs=[pl.BlockSpec((tm, D), lambda j, i, order_ref: (i, 0))] + [ANY] * nt,
        out_specs=[pl.BlockSpec((tm, n_blk), lambda j, i, order_ref: (i, order_ref[j]))] + [ANY] * nt,
        scratch_shapes=[pltpu.VMEM((N_DEV, D, n_blk), BF16), pltpu.SemaphoreType.DMA((nt, N_DEV - 1)),
                        pltpu.SemaphoreType.DMA((nt, N_DEV - 1)), pltpu.SemaphoreType.DMA((nt,)), pltpu.SemaphoreType.DMA((N_DEV,))],
    )
    return pl.pallas_call(
        body, name="proj_gather", grid_spec=grid_spec,
        out_shape=[jax.ShapeDtypeStruct((t, N_COL), F32)] + full_shapes,
    )(order, h, *shards)


def _dw_in_scatter(h, dproj, small_grads, order, tk):
    t = h.shape[0]
    nt = 1 + len(small_grads)
    n_blk = N_COL // N_DEV
    nk = t // tk
    slot_shapes = [jax.ShapeDtypeStruct((len(W_IN_SLOT), D, n_blk), BF16)]
    for g, (kind, size) in zip(small_grads, _BIG[1:]):
        slot_shapes.append(jax.ShapeDtypeStruct((N_DEV,) + ((size, g.shape[1]) if kind == "rows" else (g.shape[0], size)), g.dtype))

    def body(order_ref, h_ref, dp_ref, *refs):
        src, dst = refs[:nt - 1], refs[nt - 1:2 * nt - 1]
        acc, stage, partner, send_sems, recv_sems, local_sems, pair_send, pair_recv = refs[2 * nt - 1:]
        j, kk = pl.program_id(0), pl.program_id(1)
        me, peers = _me_and_peers()

        def small_local(tn):
            kind, size = _BIG[tn]
            return pltpu.make_async_copy(_window(src[tn - 1], kind, me, size), dst[tn].at[me], local_sems.at[tn])

        def small_remote(tn, k, mine):
            kind, size = _BIG[tn]
            dev, idx = peers[k - 1]
            return pltpu.make_async_remote_copy(src_ref=_window(src[tn - 1], kind, idx if mine else me, size),
                                                dst_ref=dst[tn].at[me if mine else idx],
                                                send_sem=send_sems.at[tn, k - 1], recv_sem=recv_sems.at[tn, k - 1],
                                                device_id=dev, device_id_type=MESH)

        def push(step):
            k, slot = SCATTER_ORDER[step], step % 2
            if k == 0:
                return pltpu.make_async_copy(stage.at[slot], dst[0].at[W_IN_SLOT[0]], local_sems.at[0])
            if k not in W_IN_SLOT:
                p = (k - 3) // 2
                return pltpu.make_async_remote_copy(src_ref=stage.at[slot], dst_ref=partner.at[p], send_sem=pair_send.at[p],
                                                    recv_sem=pair_recv.at[p], device_id=peers[0][0], device_id_type=MESH)
            return pltpu.make_async_remote_copy(src_ref=stage.at[slot], dst_ref=dst[0].at[W_IN_SLOT[k]],
                                                send_sem=send_sems.at[0, k - 1], recv_sem=recv_sems.at[0, k - 1],
                                                device_id=peers[k - 1][0], device_id_type=MESH)

        @pl.when((j == 0) & (kk == 0))
        def _():
            for tn in range(1, nt):
                small_local(tn).start()
                for k in range(1, N_DEV):
                    small_remote(tn, k, True).start()

        p = lax.dot_general(h_ref[...], dp_ref[...], (((0,), (0,)), ((), ())), preferred_element_type=F32)

        @pl.when(kk == 0)
        def _():
            acc[...] = p

        @pl.when(kk > 0)
        def _():
            acc[...] += p

        @pl.when(kk == nk - 1)
        def _():
            for step, k in enumerate(SCATTER_ORDER):
                @pl.when(j == step)
                def _():
                    if step >= 2:
                        push(step - 2).wait_send()
                    total = acc[...]
                    if k in W_IN_SLOT and k >= 2:
                        p = k // 2 - 1
                        push(SCATTER_ORDER.index(k + 1)).wait_recv()
                        total = total + partner[p].astype(F32)
                    stage[step % 2] = total.astype(BF16)
                    push(step).start()

        @pl.when((j == N_DEV - 1) & (kk == nk - 1))
        def _():
            push(N_DEV - 2).wait_send()
            push(N_DEV - 1).wait()
            for k in (1, 2, 4, 6):
                push(SCATTER_ORDER.index(k)).wait_recv()
            for tn in range(1, nt):
                small_local(tn).wait()
                for k in range(1, N_DEV):
                    small_remote(tn, k, False).wait_recv()
                    small_remote(tn, k, True).wait_send()

    grid_spec = pltpu.PrefetchScalarGridSpec(
        num_scalar_prefetch=1, grid=(N_DEV, nk),
        in_specs=[pl.BlockSpec((tk, D), lambda j, kk, order_ref: (kk, 0)),
                  pl.BlockSpec((tk, n_blk), lambda j, kk, order_ref: (kk, order_ref[j]))] + [ANY] * (nt - 1),
        out_specs=[ANY] * nt,
        scratch_shapes=[pltpu.VMEM((D, n_blk), F32), pltpu.VMEM((2, D, n_blk), BF16), pltpu.VMEM((3, D, n_blk), BF16),
                        pltpu.SemaphoreType.DMA((nt, N_DEV - 1)), pltpu.SemaphoreType.DMA((nt, N_DEV - 1)),
                        pltpu.SemaphoreType.DMA((nt,)), pltpu.SemaphoreType.DMA((3,)), pltpu.SemaphoreType.DMA((3,))],
    )
    return pl.pallas_call(body, name="dw_in_scatter", grid_spec=grid_spec, out_shape=slot_shapes)(order, h, dproj, *small_grads)


def _adamw_math(w, g, m, v):
    m = ADAM_B1 * m + (1.0 - ADAM_B1) * g
    v = ADAM_B2 * v + (1.0 - ADAM_B2) * (g * g)
    m_hat = m / (1.0 - ADAM_B1 ** ADAM_STEP)
    v_hat = v / (1.0 - ADAM_B2 ** ADAM_STEP)
    delta = -ADAM_LR * (m_hat / (jnp.sqrt(v_hat) + ADAM_EPS) + ADAM_WD * w)
    return delta, m, v


def _sum_adamw(slots, w, m, v, tr, name):
    n_slots, r, c = slots.shape
    assert r % tr == 0

    def body(s_ref, w_ref, m_ref, v_ref, g_ref, d_ref, nm_ref, nv_ref):
        g = s_ref[0].astype(F32)
        for j in range(1, n_slots):
            g = g + s_ref[j].astype(F32)
        delta, nm, nv = _adamw_math(w_ref[...], g, m_ref[...], v_ref[...])
        g_ref[...] = g
        d_ref[...] = delta
        nm_ref[...] = nm
        nv_ref[...] = nv

    blk = pl.BlockSpec((tr, c), lambda i: (i, 0))
    return pl.pallas_call(
        body, name=name, grid=(r // tr,),
        in_specs=[pl.BlockSpec((n_slots, tr, c), lambda i: (0, i, 0)), blk, blk, blk],
        out_specs=[blk] * 4, out_shape=[jax.ShapeDtypeStruct((r, c), F32)] * 4,
    )(slots, w, m, v)


def _adamw_small(g, w, m, v, name):
    def body(g_ref, w_ref, m_ref, v_ref, d_ref, nm_ref, nv_ref):
        delta, nm, nv = _adamw_math(w_ref[...], g_ref[...], m_ref[...], v_ref[...])
        d_ref[...] = delta
        nm_ref[...] = nm
        nv_ref[...] = nv

    spec = _full(g.shape)
    return pl.pallas_call(body, name=name, grid=(1,), in_specs=[spec] * 4, out_specs=[spec] * 3,
                          out_shape=[jax.ShapeDtypeStruct(g.shape, F32)] * 3)(g, w, m, v)


def _mod_part(c_all, w_ada_l, b_ada_l):
    n = w_ada_l.shape[1]

    def body(c_ref, w_ref, b_ref, o_ref):
        o_ref[...] = jnp.dot(c_ref[...], w_ref[...], preferred_element_type=F32,
                             precision=lax.Precision.HIGHEST) + b_ref[...]

    return pl.pallas_call(body, name="mod_part", grid=(1,),
                          in_specs=[_full(c_all.shape), _full(w_ada_l.shape), _full(b_ada_l.shape)],
                          out_specs=_full((N_DEV, n)), out_shape=jax.ShapeDtypeStruct((N_DEV, n), F32))(c_all, w_ada_l, b_ada_l)


def _w_ada_update(c_all_t, dmod_cols, w, m, v):
    def body(c_ref, dm_ref, w_ref, m_ref, v_ref, g_ref, d_ref, nm_ref, nv_ref):
        g = c_ref[:, 0:1] * dm_ref[0:1, :]
        for b in range(1, N_DEV):
            g = g + c_ref[:, b:b + 1] * dm_ref[b:b + 1, :]
        delta, nm, nv = _adamw_math(w_ref[...], g, m_ref[...], v_ref[...])
        g_ref[...] = g
        d_ref[...] = delta
        nm_ref[...] = nm
        nv_ref[...] = nv

    spec = _full(w.shape)
    return pl.pallas_call(body, name="w_ada_update", grid=(1,),
                          in_specs=[_full(c_all_t.shape), _full(dmod_cols.shape), spec, spec, spec],
                          out_specs=[spec] * 4, out_shape=[jax.ShapeDtypeStruct(w.shape, F32)] * 4)(c_all_t, dmod_cols, w, m, v)


def _cast_bf16(w, name):
    def body(w_ref, o_ref):
        o_ref[...] = w_ref[...].astype(BF16)

    return pl.pallas_call(body, name=name, grid=(1,), in_specs=[_full(w.shape)], out_specs=_full(w.shape),
                          out_shape=jax.ShapeDtypeStruct(w.shape, BF16))(w)


def _prenorm(x, mod, norm_g, tm):
    t = x.shape[0]

    def body(x_ref, mod_ref, g_ref, h_ref):
        xv = x_ref[...]
        r = lax.rsqrt(jnp.mean(xv * xv, axis=-1, keepdims=True) + EPS)
        h = (xv * r) * g_ref[...] * (1.0 + mod_ref[:, D:2 * D]) + mod_ref[:, 0:D]
        h_ref[...] = h.astype(BF16)

    return pl.pallas_call(body, name="prenorm", grid=(t // tm,),
                          in_specs=[_rows(tm, D), _full((1, 3 * D)), _full((1, D))],
                          out_specs=_rows(tm, D), out_shape=jax.ShapeDtypeStruct((t, D), BF16))(x, mod, norm_g)


def _rope_apply(t, cos, s_lo, s_hi):
    return t * cos + pltpu.roll(t, 120, 1) * s_lo + pltpu.roll(t, 8, 1) * s_hi


def _shift_copies(sh, buf, c0):
    rows = buf.shape[0] - 8
    for s in range(1, 8):
        sh[s, 0:rows, :] = buf[s:s + rows, pl.ds(c0, 128)]


def _window64(buf, sh, c0, start):
    s = start % 8
    if s == 0:
        return buf[start:start + 64, pl.ds(c0, 128)]
    return sh[s, start - s:start - s + 64, :]


def _conv_taps(acc_init, w_ref, buf, sh, row0, c0, offset_of_tap):
    acc = acc_init
    for j in range(CONV_K):
        acc = acc + w_ref[j:j + 1, pl.ds(c0, 128)] * _window64(buf, sh, c0, row0 + offset_of_tap(j))
    return acc


def _conv_fwd(proj, conv_w, conv_b, ln_g, ln_b, tm):
    t = proj.shape[0]
    hb = tm // HALO

    def body(a_ref, b_ref, z_ref, ah_ref, bh_ref, w_ref, cb_ref, lg_ref, lb_ref, u1_ref, pc_ref, ubuf, sh):
        i = pl.program_id(0)
        u0h = ah_ref[...] * _sig(bh_ref[...])
        ubuf[0:HALO, :] = jnp.where(i > 0, u0h, 0.0)
        ubuf[HALO:HALO + tm, :] = a_ref[...] * _sig(b_ref[...])

        def col(ci, carry):
            c0 = pl.multiple_of(ci * 128, 128)
            _shift_copies(sh, ubuf, c0)
            for rc in range(tm // 64):
                init = jnp.zeros((64, 128), F32)
                acc = _conv_taps(init, w_ref, ubuf, sh, rc * 64, c0, lambda j: HALO - (CONV_K - 1) + j)
                u1_ref[rc * 64:(rc + 1) * 64, pl.ds(c0, 128)] = acc + cb_ref[:, pl.ds(c0, 128)]
            return carry

        lax.fori_loop(0, D // 128, col, 0)
        u1 = u1_ref[...]
        mu = jnp.mean(u1, axis=-1, keepdims=True)
        xc = u1 - mu
        var = jnp.mean(xc * xc, axis=-1, keepdims=True)
        u2 = xc * lax.rsqrt(var + EPS) * lg_ref[...] + lb_ref[...]
        z = z_ref[...]
        pc_ref[...] = (u2 * _sig(u2) * (z * _sig(z))).astype(BF16)

    halo = pl.BlockSpec((HALO, D), lambda i: (jnp.maximum(i * hb - 1, 0), 0))
    halo_b = pl.BlockSpec((HALO, D), lambda i: (jnp.maximum(i * hb - 1, 0), 1))
    return pl.pallas_call(
        body, name="conv_fwd", grid=(t // tm,),
        in_specs=[_rows(tm, D, 0), _rows(tm, D, 1), _rows(tm, D, 2), halo, halo_b,
                  _full((CONV_KP, D)), _full((1, D)), _full((1, D)), _full((1, D))],
        out_specs=[_rows(tm, D), _rows(tm, D)],
        out_shape=[jax.ShapeDtypeStruct((t, D), F32), jax.ShapeDtypeStruct((t, D), BF16)],
        scratch_shapes=[pltpu.VMEM((HALO + tm, D), F32), pltpu.VMEM((8, HALO + tm, 128), F32)],
    )(proj, proj, proj, proj, proj, conv_w, conv_b, ln_g, ln_b)


def _band_masks_t(has_prev):
    key = lax.broadcasted_iota(jnp.int32, (BLK, BLK), 0)
    qry = lax.broadcasted_iota(jnp.int32, (BLK, BLK), 1)
    return jnp.logical_and(key >= qry, has_prev), key <= qry


def _head_lanes(pair, hh):
    lane = lax.broadcasted_iota(jnp.int32, pair.shape, 1)
    return jnp.where((lane >= hh * HEAD) & (lane < (hh + 1) * HEAD), pair, jnp.zeros_like(pair))


def _pair_mask(has_prev):
    mask_p, mask_c = _band_masks_t(has_prev)
    both = jnp.concatenate([mask_p, mask_c], axis=0)
    return jnp.concatenate([both, both], axis=1)


def _query_pair(pair):
    return jnp.concatenate([_head_lanes(pair, 0), _head_lanes(pair, 1)], axis=0)


def _key_pair(ref, prev, cur):
    return jnp.concatenate([ref[pl.ds(prev, BLK), :], ref[pl.ds(cur, BLK), :]], axis=0)


def _own_head(both):
    return jnp.concatenate([both[0:HEAD, 0:BLK], both[HEAD:2 * HEAD, BLK:2 * BLK]], axis=0)


def _store_transposed(dst, base, src):
    for j in range(TILE // BLK):
        dst[base // BLK + j] = src[j * BLK:(j + 1) * BLK, :].T.astype(BF16)


class _Dilated:
    def __init__(self, dil):
        self.dil = dil
        self.per = TILE // dil
        self.nbr = self.per // BLK

    def spread(self, dst, base, src_ref, dtype):
        for r in range(self.dil):
            rows = src_ref[pl.ds(r, self.per, stride=self.dil), :] if self.dil > 1 else src_ref[...]
            dst[pl.ds(pl.multiple_of(base + r * self.per, BLK), self.per), :] = rows.astype(dtype)

    def gather(self, dst_ref, src, base):
        for r in range(self.dil):
            rows = src[pl.ds(pl.multiple_of(base + r * self.per, BLK), self.per), :]
            if self.dil > 1:
                dst_ref[pl.ds(r, self.per, stride=self.dil), :] = rows
            else:
                dst_ref[...] = rows

    def block_rows(self, b, i, cur, prv):
        n = b % self.nbr
        row = pl.multiple_of(b * BLK, BLK)
        has_prev = jnp.logical_or(n > 0, i > 0)
        prev = jnp.where(n > 0, cur + row - BLK, jnp.where(i > 0, prv + row + (self.nbr - 1) * BLK, cur + row))
        return row, pl.multiple_of(prev, BLK), has_prev


def _slots(i):
    return pl.multiple_of((i % 2) * TILE, TILE), pl.multiple_of(((i + 1) % 2) * TILE, TILE)


def _nt(a, b):
    return lax.dot_general(a, b, (((1,), (1,)), ((), ())), preferred_element_type=F32)


def _qkv_specs(gi, clamp_to=None):
    def spec(col0):
        def imap(hp, i):
            return (i if clamp_to is None else jnp.minimum(i, clamp_to), (col0 + gi * ATT) // 128 + hp)
        return pl.BlockSpec((TILE, 128), imap)
    return [spec(C_Q), spec(C_K), spec(C_V)]


def _att_fwd(proj, tables, gi, dil):
    t = proj.shape[0]
    dl = _Dilated(dil)

    def body(q_ref, k_ref, v_ref, c_ref, lo_ref, hi_ref, o_ref, lse_ref, tmp, qd, kd, vt, od, ld):
        i = pl.program_id(1)
        cur, prv = _slots(i)
        cs, lo, hi = c_ref[...], lo_ref[...], hi_ref[...]
        tmp[...] = _rope_apply(q_ref[...], cs, lo, hi) * SM_SCALE
        dl.spread(qd, 0, tmp, BF16)
        tmp[...] = _rope_apply(k_ref[...], cs, lo, hi)
        dl.spread(kd, cur, tmp, BF16)
        dl.spread(tmp, 0, v_ref, F32)
        _store_transposed(vt, cur, tmp)

        def block(b, carry):
            row, prev, has_prev = dl.block_rows(b, i, cur, prv)
            s = jnp.where(_pair_mask(has_prev), _nt(_key_pair(kd, prev, cur + row), _query_pair(qd[pl.ds(row, BLK), :])), NEG_INF)
            mx = jnp.max(s, axis=0, keepdims=True)
            p = jnp.exp(s - mx)
            den = jnp.sum(p, axis=0, keepdims=True)
            v_t = jnp.concatenate([vt[prev // BLK], vt[(cur + row) // BLK]], axis=1)
            acc = jnp.dot(v_t, p.astype(BF16), preferred_element_type=F32) / den
            lse = mx + jnp.log(den)
            od[pl.ds(row, BLK), :] = _own_head(acc).T
            ld[pl.ds(row, BLK), :] = _own_head(jnp.broadcast_to(lse, (2 * HEAD, 2 * BLK))).T
            return carry

        lax.fori_loop(0, TILE // BLK, block, 0, unroll=True)
        dl.gather(o_ref, od, 0)
        dl.gather(lse_ref, ld, 0)

    tab = pl.BlockSpec((TILE, 128), lambda hp, i: (i, 0))
    out_spec = pl.BlockSpec((TILE, 128), lambda hp, i: (i, hp))
    return pl.pallas_call(
        body, name=f"att_fwd_g{gi}", grid=(ATT // 128, t // TILE),
        in_specs=_qkv_specs(gi) + [tab] * 3,
        out_specs=[out_spec] * 2, out_shape=[jax.ShapeDtypeStruct((t, ATT), F32)] * 2,
        scratch_shapes=[pltpu.VMEM((TILE, 128), F32), pltpu.VMEM((TILE, 128), BF16), pltpu.VMEM((2 * TILE, 128), BF16),
                        pltpu.VMEM((2 * TILE // BLK, 128, BLK), BF16), pltpu.VMEM((TILE, 128), F32), pltpu.VMEM((TILE, 128), F32)],
    )(proj, proj, proj, *tables)


def _att_combine(parts, proj, tm):
    t = proj.shape[0]

    def body(o0, l0, o1, l1, o2, l2, z_ref, att_ref, lse_ref, pa_ref):
        m_all = jnp.maximum(jnp.maximum(l0[...], l1[...]), l2[...])
        w0, w1, w2 = jnp.exp(l0[...] - m_all), jnp.exp(l1[...] - m_all), jnp.exp(l2[...] - m_all)
        den = w0 + w1 + w2
        att = (w0 * o0[...] + w1 * o1[...] + w2 * o2[...]) / den
        z = z_ref[...]
        att_ref[...] = att
        lse_ref[...] = m_all + jnp.log(den)
        pa_ref[...] = (att * (z * _sig(z))).astype(BF16)

    spec = _rows(tm, ATT)
    return pl.pallas_call(
        body, name="att_combine", grid=(t // tm,),
        in_specs=[spec] * 6 + [_rows(tm, ATT, C_ZA // ATT)],
        out_specs=[spec] * 3,
        out_shape=[jax.ShapeDtypeStruct((t, ATT), F32)] * 2 + [jax.ShapeDtypeStruct((t, ATT), BF16)],
    )(*parts, proj)


def _att_bwd(proj, tables, datt, dsum, lse, gi, dil):
    t = proj.shape[0]
    nt = t // TILE
    dl = _Dilated(dil)

    def body(q_ref, k_ref, v_ref, c_ref, lo_ref, hi_ref, cl_ref, lol_ref, hil_ref, do_ref, ds_ref, lse_ref,
             dq_ref, dk_ref, dv_ref, tmp, qd, kd, vd, dod, dsd, lsd, dqd, dkd, dvd, kt):
        i = pl.program_id(1)
        cur, prv = _slots(i)

        @pl.when(i < nt)
        def _():
            cs, lo, hi = c_ref[...], lo_ref[...], hi_ref[...]
            tmp[...] = _rope_apply(q_ref[...], cs, lo, hi) * SM_SCALE
            dl.spread(qd, 0, tmp, BF16)
            tmp[...] = _rope_apply(k_ref[...], cs, lo, hi)
            dl.spread(kd, cur, tmp, BF16)
            dl.spread(dqd, 0, tmp, F32)
            _store_transposed(kt, cur, dqd)
            dl.spread(vd, cur, v_ref, BF16)
            dl.spread(dod, 0, do_ref, BF16)
            dl.spread(dsd, 0, ds_ref, F32)
            dl.spread(lsd, 0, lse_ref, F32)
            dkd[pl.ds(cur, TILE), :] = jnp.zeros((TILE, 128), F32)
            dvd[pl.ds(cur, TILE), :] = jnp.zeros((TILE, 128), F32)

            def block(b, carry):
                row, prev, has_prev = dl.block_rows(b, i, cur, prv)
                q_pair, do_pair = _query_pair(qd[pl.ds(row, BLK), :]), _query_pair(dod[pl.ds(row, BLK), :])
                k_pair, v_pair = _key_pair(kd, prev, cur + row), _key_pair(vd, prev, cur + row)
                ds_t, ls_t = dsd[pl.ds(row, BLK), :].T, lsd[pl.ds(row, BLK), :].T
                lse = jnp.concatenate([ls_t[0:1, :], ls_t[HEAD:HEAD + 1, :]], axis=1)
                dsm = jnp.concatenate([ds_t[0:1, :], ds_t[HEAD:HEAD + 1, :]], axis=1)
                p = jnp.exp(jnp.where(_pair_mask(has_prev), _nt(k_pair, q_pair), NEG_INF) - lse)
                ds = (p * (_nt(v_pair, do_pair) - dsm)).astype(BF16)
                k_t = jnp.concatenate([kt[prev // BLK], kt[(cur + row) // BLK]], axis=1)
                dqd[pl.ds(row, BLK), :] = _own_head(jnp.dot(k_t, ds, preferred_element_type=F32)).T * SM_SCALE
                dk = jnp.dot(ds, q_pair, preferred_element_type=F32)
                dv = jnp.dot(p.astype(BF16), do_pair, preferred_element_type=F32)
                dkd[pl.ds(cur + row, BLK), :] += dk[BLK:2 * BLK, :]
                dvd[pl.ds(cur + row, BLK), :] += dv[BLK:2 * BLK, :]
                dkd[pl.ds(prev, BLK), :] += dk[0:BLK, :]
                dvd[pl.ds(prev, BLK), :] += dv[0:BLK, :]
                return carry

            lax.fori_loop(0, TILE // BLK, block, 0, unroll=True)
            dl.gather(tmp, dqd, 0)
            dq_ref[...] = _rope_apply(tmp[...], cs, -lo, -hi).astype(BF16)

        @pl.when(i > 0)
        def _():
            dl.gather(tmp, dkd, prv)
            dk_ref[...] = _rope_apply(tmp[...], cl_ref[...], -lol_ref[...], -hil_ref[...]).astype(BF16)
            dl.gather(tmp, dvd, prv)
            dv_ref[...] = tmp[...].astype(BF16)

    now = lambda col: pl.BlockSpec((TILE, 128), lambda hp, i: (jnp.minimum(i, nt - 1), col(hp)))
    lag = lambda col: pl.BlockSpec((TILE, 128), lambda hp, i: (jnp.maximum(i - 1, 0), col(hp)))
    first, pair = (lambda hp: 0), (lambda hp: hp)
    return pl.pallas_call(
        body, name=f"att_bwd_g{gi}", grid=(ATT // 128, nt + 1),
        in_specs=_qkv_specs(gi, nt - 1) + [now(first)] * 3 + [lag(first)] * 3 + [now(pair)] * 3,
        out_specs=[now(pair), lag(pair), lag(pair)],
        out_shape=[jax.ShapeDtypeStruct((t, ATT), BF16)] * 3,
        scratch_shapes=[pltpu.VMEM((TILE, 128), F32), pltpu.VMEM((TILE, 128), BF16), pltpu.VMEM((2 * TILE, 128), BF16),
                        pltpu.VMEM((2 * TILE, 128), BF16), pltpu.VMEM((TILE, 128), BF16), pltpu.VMEM((TILE, 128), F32),
                        pltpu.VMEM((TILE, 128), F32), pltpu.VMEM((TILE, 128), F32), pltpu.VMEM((2 * TILE, 128), F32),
                        pltpu.VMEM((2 * TILE, 128), F32), pltpu.VMEM((2 * TILE // BLK, 128, BLK), BF16)],
    )(proj, proj, proj, *tables, *tables, datt, dsum, lse)


def _merge_head(pc, pa, proj, x, mod, final_g, target, w_co, w_ao, w_o, tm):
    t = x.shape[0]

    def body(pc_ref, pa_ref, gc_ref, ga_ref, x_ref, mod_ref, fg_ref, tg_ref, wco_ref, wao_ref, wo_ref,
             merged_ref, do_ref, dyc_ref, dya_ref, dout_ref, dpc_ref, dpa_ref, dp_ref, sq_ref, gfg_ref, dgate_ref):
        i = pl.program_id(0)
        yc = jnp.dot(pc_ref[...], wco_ref[...], preferred_element_type=F32)
        ya = jnp.dot(pa_ref[...], wao_ref[...], preferred_element_type=F32)
        sc, sa = _sig(gc_ref[...]), _sig(ga_ref[...])
        merged = (sc * yc + sa * ya).astype(BF16)
        merged_ref[...] = merged
        ov = jnp.dot(merged, wo_ref[...], preferred_element_type=F32)
        gate = mod_ref[:, 2 * D:3 * D]
        out = x_ref[...] + gate * ov
        r = lax.rsqrt(jnp.mean(out * out, axis=-1, keepdims=True) + EPS)
        yn = out * r
        diff = yn * fg_ref[...] - tg_ref[...]
        dy = diff * (1.0 / D)
        gy = dy * fg_ref[...]
        dout = r * (gy - yn * jnp.mean(gy * yn, axis=-1, keepdims=True))
        dout_ref[...] = dout
        do = (dout * gate).astype(BF16)
        do_ref[...] = do
        _acc_rows(sq_ref, i, diff * diff)
        _acc_rows(gfg_ref, i, dy * yn)
        _acc_rows(dgate_ref, i, dout * ov)
        dm = _nt(do, wo_ref[...])
        dyc = (dm * sc).astype(BF16)
        dya = (dm * sa).astype(BF16)
        dyc_ref[...] = dyc
        dya_ref[...] = dya
        dp_ref[:, 0:D] = (dm * yc * sc * (1.0 - sc)).astype(BF16)
        dp_ref[:, D:2 * D] = (dm * ya * sa * (1.0 - sa)).astype(BF16)
        dpc_ref[...] = _nt(dyc, wco_ref[...])
        dpa_ref[...] = _nt(dya, wao_ref[...])

    vec = _full((1, D))
    bf = lambda w: jax.ShapeDtypeStruct((t, w), BF16)
    f32 = lambda w: jax.ShapeDtypeStruct((t, w), F32)
    return pl.pallas_call(
        body, name="merge_head", grid=(t // tm,),
        in_specs=[_rows(tm, D), _rows(tm, ATT), _rows(tm, D, C_GC // D), _rows(tm, D, C_GA // D), _rows(tm, D),
                  _full((1, 3 * D)), vec, _rows(tm, D), _full((D, D)), _full((ATT, D)), _full((D, D))],
        out_specs=[_rows(tm, D)] * 5 + [_rows(tm, D), _rows(tm, ATT), _rows(tm, 2 * D, C_GC // (2 * D)), vec, vec, vec],
        out_shape=[bf(D), bf(D), bf(D), bf(D), f32(D), f32(D), f32(ATT), bf(N_COL)] + [jax.ShapeDtypeStruct((1, D), F32)] * 3,
    )(pc, pa, proj, proj, x, mod, final_g, target, w_co, w_ao, w_o)


def _acc_rows(ref, i, val):
    @pl.when(i == 0)
    def _():
        ref[...] = jnp.zeros_like(ref)

    ref[...] += jnp.sum(val, axis=0, keepdims=True)


def _att_pre_bwd(dpa, proj, att, dproj, tm):
    t = proj.shape[0]

    def body(dpa_ref, z_ref, att_ref, dp_in, datt_ref, ds_ref, dp_ref):
        del dp_in
        z, dpa_v, att_v = z_ref[...], dpa_ref[...], att_ref[...]
        s = _sig(z)
        datt = dpa_v * (z * s)
        datt_ref[...] = datt
        dp_ref[...] = (dpa_v * att_v * _dsilu(z, s)).astype(BF16)
        prod = datt * att_v
        for h in range(ATT // HEAD):
            sl = slice(h * HEAD, (h + 1) * HEAD)
            ds_ref[:, sl] = jnp.broadcast_to(jnp.sum(prod[:, sl], axis=-1, keepdims=True), (tm, HEAD))

    return pl.pallas_call(
        body, name="att_pre_bwd", grid=(t // tm,),
        in_specs=[_rows(tm, ATT), _rows(tm, ATT, C_ZA // ATT), _rows(tm, ATT), ANY],
        out_specs=[_rows(tm, ATT), _rows(tm, ATT), _rows(tm, ATT, C_ZA // ATT)],
        out_shape=[jax.ShapeDtypeStruct((t, ATT), F32), jax.ShapeDtypeStruct((t, ATT), F32), jax.ShapeDtypeStruct((t, N_COL), BF16)],
        input_output_aliases={3: 2},
    )(dpa, proj, att, dproj)


def _place_qkv(parts, dproj, col_block, tm, name):
    t = dproj.shape[0]

    def body(p0, p1, p2, dp_in, dp_ref):
        del dp_in
        for g, ref in enumerate((p0, p1, p2)):
            dp_ref[:, g * ATT:(g + 1) * ATT] = ref[...]

    return pl.pallas_call(
        body, name=name, grid=(t // tm,),
        in_specs=[_rows(tm, ATT)] * 3 + [ANY],
        out_specs=_rows(tm, QKV, col_block), out_shape=jax.ShapeDtypeStruct((t, N_COL), BF16),
        input_output_aliases={3: 0},
    )(*parts, dproj)


def _conv_bwd_rows(dpc, proj, u1, ln_g, ln_b, dproj, tm):
    t = proj.shape[0]

    def body(dpc_ref, z_ref, u1_ref, lg_ref, lb_ref, dp_in, du1_ref, dp_ref, dlg_ref, dlb_ref, dcb_ref):
        del dp_in
        i = pl.program_id(0)
        u1v = u1_ref[...]
        mu = jnp.mean(u1v, axis=-1, keepdims=True)
        xc = u1v - mu
        r = lax.rsqrt(jnp.mean(xc * xc, axis=-1, keepdims=True) + EPS)
        uhat = xc * r
        u2 = uhat * lg_ref[...] + lb_ref[...]
        s2 = _sig(u2)
        z = z_ref[...]
        sz = _sig(z)
        dpc_v = dpc_ref[...]
        dp_ref[...] = (dpc_v * (u2 * s2) * _dsilu(z, sz)).astype(BF16)
        du2 = dpc_v * (z * sz) * _dsilu(u2, s2)
        duhat = du2 * lg_ref[...]
        du1 = r * (duhat - jnp.mean(duhat, axis=-1, keepdims=True) - uhat * jnp.mean(duhat * uhat, axis=-1, keepdims=True))
        du1_ref[...] = du1
        _acc_rows(dlg_ref, i, du2 * uhat)
        _acc_rows(dlb_ref, i, du2)
        _acc_rows(dcb_ref, i, du1)

    vec = _full((1, D))
    return pl.pallas_call(
        body, name="conv_bwd_rows", grid=(t // tm,),
        in_specs=[_rows(tm, D), _rows(tm, D, C_ZC // D), _rows(tm, D), vec, vec, ANY],
        out_specs=[_rows(tm, D), _rows(tm, D, C_ZC // D), vec, vec, vec],
        out_shape=[jax.ShapeDtypeStruct((t, D), F32), jax.ShapeDtypeStruct((t, N_COL), BF16)] + [jax.ShapeDtypeStruct((1, D), F32)] * 3,
        input_output_aliases={5: 1},
    )(dpc, proj, u1, ln_g, ln_b, dproj)


def _conv_bwd_taps(du1, proj, conv_w, dproj, tm):
    t = proj.shape[0]
    hb = tm // HALO
    last = t // HALO - 1

    def body(du_ref, duh_ref, a_ref, b_ref, ah_ref, bh_ref, w_ref, dp_in, dp_ref, dw_ref, dbuf, ubuf, g0, shd, shu):
        del dp_in
        i = pl.program_id(0)
        a, sb = a_ref[...], _sig(b_ref[...])
        ubuf[0:HALO, :] = jnp.where(i > 0, ah_ref[...] * _sig(bh_ref[...]), 0.0)
        ubuf[HALO:HALO + tm, :] = a * sb
        dbuf[0:tm, :] = du_ref[...]
        dbuf[tm:tm + HALO, :] = jnp.where(i < pl.num_programs(0) - 1, duh_ref[...], 0.0)

        @pl.when(i == 0)
        def _():
            dw_ref[...] = jnp.zeros_like(dw_ref)

        def col(ci, carry):
            c0 = pl.multiple_of(ci * 128, 128)
            _shift_copies(shd, dbuf, c0)
            _shift_copies(shu, ubuf, c0)
            for rc in range(tm // 64):
                g0[rc * 64:(rc + 1) * 64, pl.ds(c0, 128)] = _conv_taps(
                    jnp.zeros((64, 128), F32), w_ref, dbuf, shd, rc * 64, c0, lambda j: CONV_K - 1 - j)
            for j in range(CONV_K):
                part = jnp.zeros((8, 128), F32)
                for rc in range(tm // 64):
                    off = rc * 64 + HALO - (CONV_K - 1) + j
                    prod = dbuf[rc * 64:(rc + 1) * 64, pl.ds(c0, 128)] * _window64(ubuf, shu, c0, off)
                    part = part + jnp.sum(prod.reshape(8, 8, 128), axis=0)
                dw_ref[j:j + 1, pl.ds(c0, 128)] += jnp.sum(part, axis=0, keepdims=True)
            return carry

        lax.fori_loop(0, D // 128, col, 0)
        du0 = g0[...]
        dp_ref[:, 0:D] = (du0 * sb).astype(BF16)
        dp_ref[:, D:2 * D] = (du0 * a * sb * (1.0 - sb)).astype(BF16)

    prev = lambda col: pl.BlockSpec((HALO, D), lambda i: (jnp.maximum(i * hb - 1, 0), col))
    nxt = pl.BlockSpec((HALO, D), lambda i: (jnp.minimum((i + 1) * hb, last), 0))
    return pl.pallas_call(
        body, name="conv_bwd_taps", grid=(t // tm,),
        in_specs=[_rows(tm, D), nxt, _rows(tm, D, 0), _rows(tm, D, 1), prev(0), prev(1), _full((CONV_KP, D)), ANY],
        out_specs=[_rows(tm, 2 * D, 0), _full((CONV_KP, D))],
        out_shape=[jax.ShapeDtypeStruct((t, N_COL), BF16), jax.ShapeDtypeStruct((CONV_KP, D), F32)],
        scratch_shapes=[pltpu.VMEM((tm + HALO, D), F32), pltpu.VMEM((HALO + tm, D), F32), pltpu.VMEM((tm, D), F32),
                        pltpu.VMEM((8, HALO + tm, 128), F32), pltpu.VMEM((8, HALO + tm, 128), F32)],
        input_output_aliases={7: 0},
    )(du1, du1, proj, proj, proj, proj, conv_w, dproj)


def _prenorm_bwd(dh, x, dout, mod, norm_g, tm):
    t = x.shape[0]

    def body(dh_ref, x_ref, dout_ref, mod_ref, g_ref, gx_ref, dshift_ref, dscale_ref, dg_ref):
        i = pl.program_id(0)
        xv, dhv = x_ref[...], dh_ref[...]
        r = lax.rsqrt(jnp.mean(xv * xv, axis=-1, keepdims=True) + EPS)
        xn = xv * r
        one_scale = 1.0 + mod_ref[:, D:2 * D]
        dxn = dhv * (g_ref[...] * one_scale)
        gx_ref[...] = r * (dxn - xn * jnp.mean(dxn * xn, axis=-1, keepdims=True)) + dout_ref[...]
        _acc_rows(dshift_ref, i, dhv)
        _acc_rows(dscale_ref, i, dhv * xn * g_ref[...])
        _acc_rows(dg_ref, i, dhv * xn * one_scale)

    vec = _full((1, D))
    return pl.pallas_call(
        body, name="prenorm_bwd", grid=(t // tm,),
        in_specs=[_rows(tm, D), _rows(tm, D), _rows(tm, D), _full((1, 3 * D)), vec],
        out_specs=[_rows(tm, D), vec, vec, vec],
        out_shape=[jax.ShapeDtypeStruct((t, D), F32)] + [jax.ShapeDtypeStruct((1, D), F32)] * 3,
    )(dh, x, dout, mod, norm_g)


def _sum_devices(gathered):
    w = gathered.shape[-1]

    def body(g_ref, o_ref):
        acc = g_ref[0]
        for j in range(1, N_DEV):
            acc = acc + g_ref[j]
        o_ref[...] = acc

    return pl.pallas_call(body, name="sum_devices", grid=(1,), in_specs=[_full(gathered.shape)], out_specs=_full((1, w)),
                          out_shape=jax.ShapeDtypeStruct((1, w), F32))(gathered)


def _rope_tables(positions):
    half = HEAD // 8
    t = positions.shape[-1]
    inv_freq = ROPE_THETA ** (-(jnp.arange(half, dtype=F32) * 2.0 / (2 * half)))
    ang = positions.reshape(t, 1).astype(F32) * inv_freq
    cos, sin = jnp.cos(ang), jnp.sin(ang)
    zeros = lambda n: jnp.zeros((t, n), F32)
    c64 = jnp.concatenate([cos, cos, jnp.ones((t, HEAD - 2 * half), F32)], axis=1)
    lo64 = jnp.concatenate([-sin, zeros(HEAD - half)], axis=1)
    hi64 = jnp.concatenate([zeros(half), sin, zeros(HEAD - 2 * half)], axis=1)
    return tuple(jnp.tile(a, (1, 2)) for a in (c64, lo64, hi64))


def kernel(x, c, positions, norm_g, w_ada, b_ada, w_in, conv_w, conv_b, conv_ln_g, conv_ln_b, w_conv_out, w_att_out, w_o, final_g, loss_target, m_norm_g, m_w_ada, m_b_ada, m_w_in, m_conv_w, m_conv_b, m_conv_ln_g, m_conv_ln_b, m_w_conv_out, m_w_att_out, m_w_o, m_final_g, v_norm_g, v_w_ada, v_b_ada, v_w_in, v_conv_w, v_conv_b, v_conv_ln_g, v_conv_ln_b, v_w_conv_out, v_w_att_out, v_w_o, v_final_g):
    me = 4 * lax.axis_index("x") + 2 * lax.axis_index("y") + lax.axis_index("c")
    x2, tgt = x[0], loss_target[0]
    t = x2.shape[0]
    te = 512 if t % 512 == 0 else 256
    tcv = 256
    tmm = 1024 if t % 1024 == 0 else 256
    n_ada = w_ada.shape[-1]

    pad_taps = lambda a: jnp.pad(a[0], ((0, CONV_KP - CONV_K), (0, 0)))
    shards = (_cast_bf16(w_in[0], "cast_w_in"), _cast_bf16(w_conv_out[0], "cast_w_conv_out"),
              _cast_bf16(w_att_out[0], "cast_w_att_out"), _cast_bf16(w_o[0], "cast_w_o"), pad_taps(conv_w))
    block_of = lambda relations: jnp.bitwise_xor(me, jnp.array(relations, jnp.int32))

    c_all = _allgather_small(c, "gather_c").reshape(N_DEV, D)
    b_ada_l = lax.dynamic_slice(b_ada, (0, me * n_ada), (1, n_ada))
    parts = _allgather_small(_mod_part(c_all, w_ada[0], b_ada_l), "gather_mod")
    mod = lax.dynamic_slice(parts, (0, me, 0), (N_DEV, 1, n_ada)).reshape(1, N_DEV * n_ada)

    h = _prenorm(x2, mod, norm_g, te)
    proj, w_in_f, w_co_f, w_ao_f, w_o_f, conv_w_f = _proj_gather(h, shards, block_of(GATHER_ORDER), tmm)
    u1, pc = _conv_fwd(proj, conv_w_f, conv_b, conv_ln_g, conv_ln_b, tcv)
    tables = _rope_tables(positions)
    parts_att = []
    for gi, dil in GROUPS:
        parts_att += _att_fwd(proj, tables, gi, dil)
    att, lse, pa = _att_combine(parts_att, proj, te)

    merged, do, dyc, dya, dout, dpc, dpa, dproj, sq_sum, g_final, d_gate = _merge_head(
        pc, pa, proj, x2, mod, final_g.reshape(1, D), tgt, w_co_f, w_ao_f, w_o_f, tcv)

    dw_o = _matmul(merged, do, ta=True, out_dtype=BF16, tm=D, tn=D, tk=512, name="dw_o")
    dw_co = _matmul(pc, dyc, ta=True, out_dtype=BF16, tm=D, tn=D, tk=512, name="dw_conv_out")
    dw_ao = _matmul(pa, dya, ta=True, out_dtype=BF16, tm=ATT, tn=D, tk=512, name="dw_att_out")
    datt, dsum, dproj = _att_pre_bwd(dpa, proj, att, dproj, te)
    dqs, dks, dvs = [], [], []
    for gi, dil in GROUPS:
        dq, dk, dv = _att_bwd(proj, tables, datt, dsum, lse, gi, dil)
        dqs.append(dq), dks.append(dk), dvs.append(dv)
    dproj = _place_qkv(dqs, dproj, C_Q // QKV, te, "place_dq")
    dproj = _place_qkv(dks, dproj, C_K // QKV, te, "place_dk")
    dproj = _place_qkv(dvs, dproj, C_V // QKV, te, "place_dv")
    du1, dproj, d_ln_g, d_ln_b, d_conv_b = _conv_bwd_rows(dpc, proj, u1, conv_ln_g, conv_ln_b, dproj, te)
    dproj, dconv_w = _conv_bwd_taps(du1, proj, conv_w_f, dproj, tcv)
    dh = _matmul(dproj, w_in_f, tb=True, tm=tmm, tn=D, tk=N_COL // N_DEV, name="d_h")
    grad_x, d_shift, d_scale, d_norm_g = _prenorm_bwd(dh, x2, dout, mod, norm_g, te)

    packed = jnp.concatenate([d_shift, d_scale, d_gate, d_norm_g, d_conv_b, d_ln_g, d_ln_b, g_final, sq_sum], axis=1)
    gathered = _allgather_small(packed, "gather_partials")
    total = _sum_devices(gathered)
    seg = lambda k, n=1: total[:, k * D:(k + n) * D]
    g_b_ada, g_norm_g, g_conv_b, g_ln_g, g_ln_b, g_final_g = seg(0, 3), seg(3), seg(4), seg(5), seg(6), seg(7)
    loss = (0.5 / D) * jnp.sum(seg(8))
    dmod_all = gathered[:, 0, 0:3 * D]
    dmod_cols = lax.dynamic_slice(dmod_all, (0, me * n_ada), (N_DEV, n_ada))
    g_w_ada, d_w_ada, nm_w_ada, nv_w_ada = _w_ada_update(c_all.T, dmod_cols, w_ada[0], m_w_ada[0], v_w_ada[0])

    small = {}
    for name, g, w, m, v in (("norm_g", g_norm_g, norm_g, m_norm_g, v_norm_g), ("b_ada", g_b_ada, b_ada, m_b_ada, v_b_ada),
                             ("conv_b", g_conv_b, conv_b, m_conv_b, v_conv_b), ("conv_ln_g", g_ln_g, conv_ln_g, m_conv_ln_g, v_conv_ln_g),
                             ("conv_ln_b", g_ln_b, conv_ln_b, m_conv_ln_b, v_conv_ln_b),
                             ("final_g", g_final_g, final_g.reshape(1, D), m_final_g.reshape(1, D), v_final_g.reshape(1, D))):
        small[name] = (g,) + tuple(_adamw_small(g, w, m, v, "adamw_" + name))

    slots = _dw_in_scatter(h, dproj, (dw_co, dw_ao, dw_o, dconv_w), block_of(SCATTER_ORDER), 512)
    big = {
        "w_in": _sum_adamw(slots[0], w_in[0], m_w_in[0], v_w_in[0], 256, "adamw_w_in"),
        "w_conv_out": _sum_adamw(slots[1], w_conv_out[0], m_w_conv_out[0], v_w_conv_out[0], 128, "adamw_w_conv_out"),
        "w_att_out": _sum_adamw(slots[2], w_att_out[0], m_w_att_out[0], v_w_att_out[0], 512, "adamw_w_att_out"),
        "w_o": _sum_adamw(slots[3], w_o[0], m_w_o[0], v_w_o[0], 128, "adamw_w_o"),
        "conv_w": [r[:CONV_K] for r in _sum_adamw(slots[4], pad_taps(conv_w), pad_taps(m_conv_w), pad_taps(v_conv_w), CONV_KP, "adamw_conv_w")],
    }
    big["w_ada"] = (g_w_ada, d_w_ada, nm_w_ada, nv_w_ada)

    order = ("norm_g", "w_ada", "b_ada", "w_in", "conv_w", "conv_b", "conv_ln_g", "conv_ln_b", "w_conv_out", "w_att_out", "w_o", "final_g")
    lead = lambda name, a: a.reshape(D) if name == "final_g" else (a[None] if name in big else a)
    result = {**small, **big}
    outs = [loss, grad_x[None]]
    for field in range(4):
        outs += [lead(name, result[name][field]) for name in order]
    return tuple(outs)
```

```python
import functools

import jax
import jax.numpy as jnp
from jax import lax
from jax.experimental import pallas as pl
from jax.experimental.pallas import tpu as pltpu

F32 = jnp.float32
BF16 = jnp.bfloat16

N_DEV = 8
D = 1024
N_COL = 10240
C_A, C_B, C_ZC, C_Q, C_K, C_V, C_ZA, C_GC, C_GA = 0, 1024, 2048, 3072, 4608, 6144, 7680, 8192, 9216
QKV = 1536
ATT = 512
HEAD = 64
BLK = 128
TILE = 2048
GROUPS = ((0, 1), (1, 4), (2, 16))
CONV_K = 31
CONV_KP = 32
HALO = 32
EPS = 1e-6
NEG_INF = -1e30
ROPE_THETA = 500000.0
SM_SCALE = HEAD ** -0.5

ADAM_LR, ADAM_B1, ADAM_B2, ADAM_EPS, ADAM_WD, ADAM_STEP = 0.001, 0.9, 0.999, 1e-08, 0.01, 10

MESH = pl.DeviceIdType.MESH
ANY = pl.BlockSpec(memory_space=pl.ANY)


def _sig(v):
    return 1.0 / (1.0 + jnp.exp(-v))


def _dsilu(v, s):
    return s * (1.0 + v * (1.0 - s))


def _full(shape):
    return pl.BlockSpec(shape, lambda *_: (0,) * len(shape))


def _rows(tm, width, col=0):
    return pl.BlockSpec((tm, width), lambda i: (i, col))


def _matmul(a, b, *, ta=False, tb=False, out_dtype=F32, tm, tn, tk, name):
    m, k = (a.shape[1], a.shape[0]) if ta else a.shape
    stacked = b.ndim == 3
    if stacked:
        assert tb and b.shape[2] == tk and b.shape[0] * tk == k
        n = b.shape[1]
    else:
        n = b.shape[0] if tb else b.shape[1]
        assert (b.shape[1] if tb else b.shape[0]) == k
    assert m % tm == 0 and n % tn == 0 and k % tk == 0
    nk = k // tk
    dims = (((0 if ta else 1,), (1 if tb else 0,)), ((), ()))
    use_scratch = out_dtype != F32 and nk > 1

    def body(a_ref, b_ref, o_ref, *scratch):
        p = lax.dot_general(a_ref[...], b_ref[...], dims, preferred_element_type=F32)
        if nk == 1:
            o_ref[...] = p.astype(out_dtype)
            return
        acc = scratch[0] if use_scratch else o_ref
        kk = pl.program_id(2)

        @pl.when(kk == 0)
        def _():
            acc[...] = p

        @pl.when(kk > 0)
        def _():
            acc[...] += p

        if use_scratch:
            @pl.when(kk == nk - 1)
            def _():
                o_ref[...] = acc[...].astype(out_dtype)

    a_spec = pl.BlockSpec((tk, tm), lambda i, j, kk: (kk, i)) if ta else pl.BlockSpec((tm, tk), lambda i, j, kk: (i, kk))
    b_spec = pl.BlockSpec((tn, tk), lambda i, j, kk: (j, kk)) if tb else pl.BlockSpec((tk, tn), lambda i, j, kk: (kk, j))
    if stacked:
        b_spec = pl.BlockSpec((None, tn, tk), lambda i, j, kk: (kk, j, 0))
    return pl.pallas_call(
        body, name=name, grid=(m // tm, n // tn, nk),
        in_specs=[a_spec, b_spec],
        out_specs=pl.BlockSpec((tm, tn), lambda i, j, kk: (i, j)),
        out_shape=jax.ShapeDtypeStruct((m, n), out_dtype),
        scratch_shapes=[pltpu.VMEM((tm, tn), F32)] if use_scratch else [],
    )(a, b)


def _me_and_peers():
    x, y, c = lax.axis_index("x"), lax.axis_index("y"), lax.axis_index("c")
    me = 4 * x + 2 * y + c
    peers = []
    for k in range(1, N_DEV):
        px, py, pc = x ^ (k >> 2), y ^ ((k >> 1) & 1), c ^ (k & 1)
        peers.append(((px, py, pc), 4 * px + 2 * py + pc))
    return me, peers


def _allgather_small(v, name):
    r, c = v.shape

    def body(v_ref, out_ref, send_sems, recv_sems):
        me, peers = _me_and_peers()
        out_ref[me] = v_ref[...]
        copies = []
        for k, (dev, _) in enumerate(peers):
            cp = pltpu.make_async_remote_copy(src_ref=v_ref, dst_ref=out_ref.at[me], send_sem=send_sems.at[k],
                                              recv_sem=recv_sems.at[k], device_id=dev, device_id_type=MESH)
            cp.start()
            copies.append(cp)
        for k, (dev, idx) in enumerate(peers):
            pltpu.make_async_remote_copy(src_ref=v_ref, dst_ref=out_ref.at[idx], send_sem=send_sems.at[k],
                                         recv_sem=recv_sems.at[k], device_id=dev, device_id_type=MESH).wait_recv()
        for cp in copies:
            cp.wait_send()

    return pl.pallas_call(
        body, name=name,
        in_specs=[pl.BlockSpec(memory_space=pltpu.VMEM)],
        out_specs=pl.BlockSpec(memory_space=pltpu.VMEM),
        out_shape=jax.ShapeDtypeStruct((N_DEV, r, c), v.dtype),
        scratch_shapes=[pltpu.SemaphoreType.DMA((N_DEV - 1,)), pltpu.SemaphoreType.DMA((N_DEV - 1,))],
    )(v)


def _window(ref, kind, idx, size):
    if kind == "block":
        return ref.at[idx]
    start = pl.multiple_of(idx * size, size)
    if kind == "rows":
        return ref.at[pl.ds(start, size), :]
    return ref.at[:, pl.ds(start, size)]


_BIG = (("cols", N_COL // N_DEV), ("rows", D // N_DEV), ("cols", D // N_DEV), ("rows", D // N_DEV), ("cols", D // N_DEV))
_GATHERED = (("block", 1),) + _BIG[1:]


GATHER_ORDER = (0, 1, 2, 4, 3, 5, 6, 7)
W_IN_DIRECT = (1, 2, 4, 6)
SCATTER_ORDER = (7, 6, 5, 4, 3, 2, 1, 0)
W_IN_SLOT = {0: 0, 1: 1, 2: 2, 4: 3, 6: 4}


def _proj_gather(h, shards, order, tm):
    t = h.shape[0]
    nt = len(shards)
    n_blk = N_COL // N_DEV
    full_shapes = []
    for s, (kind, size) in zip(shards, _GATHERED):
        full_shapes.append(jax.ShapeDtypeStruct({"block": (N_DEV,) + s.shape, "rows": (s.shape[0] * N_DEV, s.shape[1]),
                                                 "cols": (s.shape[0], s.shape[1] * N_DEV)}[kind], s.dtype))
    last = (N_DEV - 1, t // tm - 1)

    def body(order_ref, h_ref, *refs):
        src, proj_ref, dst = refs[:nt], refs[nt], refs[nt + 1:2 * nt + 1]
        w_all, send_sems, recv_sems, local_sems, keep_sems = refs[2 * nt + 1:]
        j, i = pl.program_id(0), pl.program_id(1)
        me, peers = _me_and_peers()

        def landing(tn, idx):
            kind, size = _GATHERED[tn]
            return w_all.at[idx] if tn == 0 else _window(dst[tn], kind, idx, size)

        def local(tn):
            return pltpu.make_async_copy(src[tn], landing(tn, me), local_sems.at[tn])

        def remote(tn, k, block_of):
            dev, idx = peers[k - 1]
            return pltpu.make_async_remote_copy(src_ref=src[tn], dst_ref=landing(tn, me if block_of == "mine" else idx),
                                                send_sem=send_sems.at[tn, k - 1], recv_sem=recv_sems.at[tn, k - 1],
                                                device_id=dev, device_id_type=MESH)

        def forward(k):
            block = w_all.at[peers[k - 1][1]]
            return pltpu.make_async_remote_copy(src_ref=block, dst_ref=block, send_sem=send_sems.at[0, k], recv_sem=recv_sems.at[0, k],
                                                device_id=peers[0][0], device_id_type=MESH)

        def keep(step):
            blk = order_ref[step]
            return pltpu.make_async_copy(w_all.at[blk], dst[0].at[blk], keep_sems.at[step])

        @pl.when((j == 0) & (i == 0))
        def _():
            for tn in range(nt):
                local(tn).start()
                for k in GATHER_ORDER[1:]:
                    if tn > 0 or k in W_IN_DIRECT:
                        remote(tn, k, "mine").start()

        @pl.when(i == 0)
        def _():
            for step, k in enumerate(GATHER_ORDER):
                @pl.when(j == step)
                def _():
                    if k == 0:
                        local(0).wait()
                    else:
                        remote(0, k, "theirs").wait_recv()
                        if k in W_IN_DIRECT and k > 1:
                            forward(k).start()
                    keep(step).start()

        proj_ref[...] = jnp.dot(h_ref[...], w_all[order_ref[j]], preferred_element_type=F32)

        @pl.when((j == last[0]) & (i == last[1]))
        def _():
            for step in range(N_DEV):
                keep(step).wait()
            for tn in range(1, nt):
                local(tn).wait()
                for k in range(1, N_DEV):
                    remote(tn, k, "theirs").wait_recv()
            for tn in range(nt):
                for k in range(1, N_DEV):
                    if tn > 0 or k in W_IN_DIRECT:
                        remote(tn, k, "mine").wait_send()
                    else:
                        forward(k - 1).wait_send()

    grid_spec = pltpu.PrefetchScalarGridSpec(
        num_scalar_prefetch=1, grid=(N_DEV, t // tm),
        in_specs=[pl.BlockSpec((tm, D), lambda j, i, order_ref: (i, 0))] + [ANY] * nt,
        out_specs=[pl.BlockSpec((tm, n_blk), lambda j, i, order_ref: (i, order_ref[j]))] + [ANY] * nt,
        scratch_shapes=[pltpu.VMEM((N_DEV, D, n_blk), BF16), pltpu.SemaphoreType.DMA((nt, N_DEV - 1)),
                        pltpu.SemaphoreType.DMA((nt, N_DEV - 1)), pltpu.SemaphoreType.DMA((nt,)), pltpu.SemaphoreType.DMA((N_DEV,))],
    )
    return pl.pallas_call(
        body, name="proj_gather", grid_spec=grid_spec,
        out_shape=[jax.ShapeDtypeStruct((t, N_COL), F32)] + full_shapes,
    )(order, h, *shards)


def _dw_in_scatter(h, dproj, small_grads, order, tk):
    t = h.shape[0]
    nt = 1 + len(small_grads)
    n_blk = N_COL // N_DEV
    nk = t // tk
    slot_shapes = [jax.ShapeDtypeStruct((len(W_IN_SLOT), D, n_blk), BF16)]
    for g, (kind, size) in zip(small_grads, _BIG[1:]):
        slot_shapes.append(jax.ShapeDtypeStruct((N_DEV,) + ((size, g.shape[1]) if kind == "rows" else (g.shape[0], size)), g.dtype))

    def body(order_ref, h_ref, dp_ref, *refs):
        src, dst = refs[:nt - 1], refs[nt - 1:2 * nt - 1]
        acc, stage, partner, send_sems, recv_sems, local_sems, pair_send, pair_recv = refs[2 * nt - 1:]
        j, kk = pl.program_id(0), pl.program_id(1)
        me, peers = _me_and_peers()

        def small_local(tn):
            kind, size = _BIG[tn]
            return pltpu.make_async_copy(_window(src[tn - 1], kind, me, size), dst[tn].at[me], local_sems.at[tn])

        def small_remote(tn, k, mine):
            kind, size = _BIG[tn]
            dev, idx = peers[k - 1]
            return pltpu.make_async_remote_copy(src_ref=_window(src[tn - 1], kind, idx if mine else me, size),
                                                dst_ref=dst[tn].at[me if mine else idx],
                                                send_sem=send_sems.at[tn, k - 1], recv_sem=recv_sems.at[tn, k - 1],
                                                device_id=dev, device_id_type=MESH)

        def push(step):
            k, slot = SCATTER_ORDER[step], step % 2
            if k == 0:
                return pltpu.make_async_copy(stage.at[slot], dst[0].at[W_IN_SLOT[0]], local_sems.at[0])
            if k not in W_IN_SLOT:
                p = (k - 3) // 2
                return pltpu.make_async_remote_copy(src_ref=stage.at[slot], dst_ref=partner.at[p], send_sem=pair_send.at[p],
                                                    recv_sem=pair_recv.at[p], device_id=peers[0][0], device_id_type=MESH)
            return pltpu.make_async_remote_copy(src_ref=stage.at[slot], dst_ref=dst[0].at[W_IN_SLOT[k]],
                                                send_sem=send_sems.at[0, k - 1], recv_sem=recv_sems.at[0, k - 1],
                                                device_id=peers[k - 1][0], device_id_type=MESH)

        @pl.when((j == 0) & (kk == 0))
        def _():
            for tn in range(1, nt):
                small_local(tn).start()
                for k in range(1, N_DEV):
                    small_remote(tn, k, True).start()

        p = lax.dot_general(h_ref[...], dp_ref[...], (((0,), (0,)), ((), ())), preferred_element_type=F32)

        @pl.when(kk == 0)
        def _():
            acc[...] = p

        @pl.when(kk > 0)
        def _():
            acc[...] += p

        @pl.when(kk == nk - 1)
        def _():
            for step, k in enumerate(SCATTER_ORDER):
                @pl.when(j == step)
                def _():
                    if step >= 2:
                        push(step - 2).wait_send()
                    total = acc[...]
                    if k in W_IN_SLOT and k >= 2:
                        p = k // 2 - 1
                        push(SCATTER_ORDER.index(k + 1)).wait_recv()
                        total = total + partner[p].astype(F32)
                    stage[step % 2] = total.astype(BF16)
                    push(step).start()

        @pl.when((j == N_DEV - 1) & (kk == nk - 1))
        def _():
            push(N_DEV - 2).wait_send()
            push(N_DEV - 1).wait()
            for k in (1, 2, 4, 6):
                push(SCATTER_ORDER.index(k)).wait_recv()
            for tn in range(1, nt):
                small_local(tn).wait()
                for k in range(1, N_DEV):
                    small_remote(tn, k, False).wait_recv()
                    small_remote(tn, k, True).wait_send()

    grid_spec = pltpu.PrefetchScalarGridSpec(
        num_scalar_prefetch=1, grid=(N_DEV, nk),
        in_specs=[pl.BlockSpec((tk, D), lambda j, kk, order_ref: (kk, 0)),
                  pl.BlockSpec((tk, n_blk), lambda j, kk, order_ref: (kk, order_ref[j]))] + [ANY] * (nt - 1),
        out_specs=[ANY] * nt,
        scratch_shapes=[pltpu.VMEM((D, n_blk), F32), pltpu.VMEM((2, D, n_blk), BF16), pltpu.VMEM((3, D, n_blk), BF16),
                        pltpu.SemaphoreType.DMA((nt, N_DEV - 1)), pltpu.SemaphoreType.DMA((nt, N_DEV - 1)),
                        pltpu.SemaphoreType.DMA((nt,)), pltpu.SemaphoreType.DMA((3,)), pltpu.SemaphoreType.DMA((3,))],
    )
    return pl.pallas_call(body, name="dw_in_scatter", grid_spec=grid_spec, out_shape=slot_shapes)(order, h, dproj, *small_grads)


def _adamw_math(w, g, m, v):
    m = ADAM_B1 * m + (1.0 - ADAM_B1) * g
    v = ADAM_B2 * v + (1.0 - ADAM_B2) * (g * g)
    m_hat = m / (1.0 - ADAM_B1 ** ADAM_STEP)
    v_hat = v / (1.0 - ADAM_B2 ** ADAM_STEP)
    delta = -ADAM_LR * (m_hat / (jnp.sqrt(v_hat) + ADAM_EPS) + ADAM_WD * w)
    return delta, m, v


def _sum_adamw(slots, w, m, v, tr, name):
    n_slots, r, c = slots.shape
    assert r % tr == 0

    def body(s_ref, w_ref, m_ref, v_ref, g_ref, d_ref, nm_ref, nv_ref):
        g = s_ref[0].astype(F32)
        for j in range(1, n_slots):
            g = g + s_ref[j].astype(F32)
        delta, nm, nv = _adamw_math(w_ref[...], g, m_ref[...], v_ref[...])
        g_ref[...] = g
        d_ref[...] = delta
        nm_ref[...] = nm
        nv_ref[...] = nv

    blk = pl.BlockSpec((tr, c), lambda i: (i, 0))
    return pl.pallas_call(
        body, name=name, grid=(r // tr,),
        in_specs=[pl.BlockSpec((n_slots, tr, c), lambda i: (0, i, 0)), blk, blk, blk],
        out_specs=[blk] * 4, out_shape=[jax.ShapeDtypeStruct((r, c), F32)] * 4,
    )(slots, w, m, v)


def _adamw_small(g, w, m, v, name):
    def body(g_ref, w_ref, m_ref, v_ref, d_ref, nm_ref, nv_ref):
        delta, nm, nv = _adamw_math(w_ref[...], g_ref[...], m_ref[...], v_ref[...])
        d_ref[...] = delta
        nm_ref[...] = nm
        nv_ref[...] = nv

    spec = _full(g.shape)
    return pl.pallas_call(body, name=name, grid=(1,), in_specs=[spec] * 4, out_specs=[spec] * 3,
                          out_shape=[jax.ShapeDtypeStruct(g.shape, F32)] * 3)(g, w, m, v)


def _mod_part(c_all, w_ada_l, b_ada_l):
    n = w_ada_l.shape[1]

    def body(c_ref, w_ref, b_ref, o_ref):
        o_ref[...] = jnp.dot(c_ref[...], w_ref[...], preferred_element_type=F32,
                             precision=lax.Precision.HIGHEST) + b_ref[...]

    return pl.pallas_call(body, name="mod_part", grid=(1,),
                          in_specs=[_full(c_all.shape), _full(w_ada_l.shape), _full(b_ada_l.shape)],
                          out_specs=_full((N_DEV, n)), out_shape=jax.ShapeDtypeStruct((N_DEV, n), F32))(c_all, w_ada_l, b_ada_l)


def _w_ada_update(c_all_t, dmod_cols, w, m, v):
    def body(c_ref, dm_ref, w_ref, m_ref, v_ref, g_ref, d_ref, nm_ref, nv_ref):
        g = c_ref[:, 0:1] * dm_ref[0:1, :]
        for b in range(1, N_DEV):
            g = g + c_ref[:, b:b + 1] * dm_ref[b:b + 1, :]
        delta, nm, nv = _adamw_math(w_ref[...], g, m_ref[...], v_ref[...])
        g_ref[...] = g
        d_ref[...] = delta
        nm_ref[...] = nm
        nv_ref[...] = nv

    spec = _full(w.shape)
    return pl.pallas_call(body, name="w_ada_update", grid=(1,),
                          in_specs=[_full(c_all_t.shape), _full(dmod_cols.shape), spec, spec, spec],
                          out_specs=[spec] * 4, out_shape=[jax.ShapeDtypeStruct(w.shape, F32)] * 4)(c_all_t, dmod_cols, w, m, v)


def _cast_bf16(w, name):
    def body(w_ref, o_ref):
        o_ref[...] = w_ref[...].astype(BF16)

    return pl.pallas_call(body, name=name, grid=(1,), in_specs=[_full(w.shape)], out_specs=_full(w.shape),
                          out_shape=jax.ShapeDtypeStruct(w.shape, BF16))(w)


def _prenorm(x, mod, norm_g, tm):
    t = x.shape[0]

    def body(x_ref, mod_ref, g_ref, h_ref):
        xv = x_ref[...]
        r = lax.rsqrt(jnp.mean(xv * xv, axis=-1, keepdims=True) + EPS)
        h = (xv * r) * g_ref[...] * (1.0 + mod_ref[:, D:2 * D]) + mod_ref[:, 0:D]
        h_ref[...] = h.astype(BF16)

    return pl.pallas_call(body, name="prenorm", grid=(t // tm,),
                          in_specs=[_rows(tm, D), _full((1, 3 * D)), _full((1, D))],
                          out_specs=_rows(tm, D), out_shape=jax.ShapeDtypeStruct((t, D), BF16))(x, mod, norm_g)


def _rope_apply(t, cos, s_lo, s_hi):
    return t * cos + pltpu.roll(t, 120, 1) * s_lo + pltpu.roll(t, 8, 1) * s_hi


def _shift_copies(sh, buf, c0):
    rows = buf.shape[0] - 8
    for s in range(1, 8):
        sh[s, 0:rows, :] = buf[s:s + rows, pl.ds(c0, 128)]


def _window64(buf, sh, c0, start):
    s = start % 8
    if s == 0:
        return buf[start:start + 64, pl.ds(c0, 128)]
    return sh[s, start - s:start - s + 64, :]


def _conv_taps(acc_init, w_ref, buf, sh, row0, c0, offset_of_tap):
    acc = acc_init
    for j in range(CONV_K):
        acc = acc + w_ref[j:j + 1, pl.ds(c0, 128)] * _window64(buf, sh, c0, row0 + offset_of_tap(j))
    return acc


def _conv_fwd(proj, conv_w, conv_b, ln_g, ln_b, tm):
    t = proj.shape[0]
    hb = tm // HALO

    def body(a_ref, b_ref, z_ref, ah_ref, bh_ref, w_ref, cb_ref, lg_ref, lb_ref, u1_ref, pc_ref, ubuf, sh):
        i = pl.program_id(0)
        u0h = ah_ref[...] * _sig(bh_ref[...])
        ubuf[0:HALO, :] = jnp.where(i > 0, u0h, 0.0)
        ubuf[HALO:HALO + tm, :] = a_ref[...] * _sig(b_ref[...])

        def col(ci, carry):
            c0 = pl.multiple_of(ci * 128, 128)
            _shift_copies(sh, ubuf, c0)
            for rc in range(tm // 64):
                init = jnp.zeros((64, 128), F32)
                acc = _conv_taps(init, w_ref, ubuf, sh, rc * 64, c0, lambda j: HALO - (CONV_K - 1) + j)
                u1_ref[rc * 64:(rc + 1) * 64, pl.ds(c0, 128)] = acc + cb_ref[:, pl.ds(c0, 128)]
            return carry

        lax.fori_loop(0, D // 128, col, 0)
        u1 = u1_ref[...]
        mu = jnp.mean(u1, axis=-1, keepdims=True)
        xc = u1 - mu
        var = jnp.mean(xc * xc, axis=-1, keepdims=True)
        u2 = xc * lax.rsqrt(var + EPS) * lg_ref[...] + lb_ref[...]
        z = z_ref[...]
        pc_ref[...] = (u2 * _sig(u2) * (z * _sig(z))).astype(BF16)

    halo = pl.BlockSpec((HALO, D), lambda i: (jnp.maximum(i * hb - 1, 0), 0))
    halo_b = pl.BlockSpec((HALO, D), lambda i: (jnp.maximum(i * hb - 1, 0), 1))
    return pl.pallas_call(
        body, name="conv_fwd", grid=(t // tm,),
        in_specs=[_rows(tm, D, 0), _rows(tm, D, 1), _rows(tm, D, 2), halo, halo_b,
                  _full((CONV_KP, D)), _full((1, D)), _full((1, D)), _full((1, D))],
        out_specs=[_rows(tm, D), _rows(tm, D)],
        out_shape=[jax.ShapeDtypeStruct((t, D), F32), jax.ShapeDtypeStruct((t, D), BF16)],
        scratch_shapes=[pltpu.VMEM((HALO + tm, D), F32), pltpu.VMEM((8, HALO + tm, 128), F32)],
    )(proj, proj, proj, proj, proj, conv_w, conv_b, ln_g, ln_b)


def _band_masks_t(has_prev):
    key = lax.broadcasted_iota(jnp.int32, (BLK, BLK), 0)
    qry = lax.broadcasted_iota(jnp.int32, (BLK, BLK), 1)
    return jnp.logical_and(key >= qry, has_prev), key <= qry


def _head_lanes(pair, hh):
    lane = lax.broadcasted_iota(jnp.int32, pair.shape, 1)
    return jnp.where((lane >= hh * HEAD) & (lane < (hh + 1) * HEAD), pair, jnp.zeros_like(pair))


def _pair_mask(has_prev):
    mask_p, mask_c = _band_masks_t(has_prev)
    both = jnp.concatenate([mask_p, mask_c], axis=0)
    return jnp.concatenate([both, both], axis=1)


def _query_pair(pair):
    return jnp.concatenate([_head_lanes(pair, 0), _head_lanes(pair, 1)], axis=0)


def _key_pair(ref, prev, cur):
    return jnp.concatenate([ref[pl.ds(prev, BLK), :], ref[pl.ds(cur, BLK), :]], axis=0)


def _own_head(both):
    return jnp.concatenate([both[0:HEAD, 0:BLK], both[HEAD:2 * HEAD, BLK:2 * BLK]], axis=0)


def _store_transposed(dst, base, src):
    for j in range(TILE // BLK):
        dst[base // BLK + j] = src[j * BLK:(j + 1) * BLK, :].T.astype(BF16)


class _Dilated:
    def __init__(self, dil):
        self.dil = dil
        self.per = TILE // dil
        self.nbr = self.per // BLK

    def spread(self, dst, base, src_ref, dtype):
        for r in range(self.dil):
            rows = src_ref[pl.ds(r, self.per, stride=self.dil), :] if self.dil > 1 else src_ref[...]
            dst[pl.ds(pl.multiple_of(base + r * self.per, BLK), self.per), :] = rows.astype(dtype)

    def gather(self, dst_ref, src, base):
        for r in range(self.dil):
            rows = src[pl.ds(pl.multiple_of(base + r * self.per, BLK), self.per), :]
            if self.dil > 1:
                dst_ref[pl.ds(r, self.per, stride=self.dil), :] = rows
            else:
                dst_ref[...] = rows

    def block_rows(self, b, i, cur, prv):
        n = b % self.nbr
        row = pl.multiple_of(b * BLK, BLK)
        has_prev = jnp.logical_or(n > 0, i > 0)
        prev = jnp.where(n > 0, cur + row - BLK, jnp.where(i > 0, prv + row + (self.nbr - 1) * BLK, cur + row))
        return row, pl.multiple_of(prev, BLK), has_prev


def _slots(i):
    return pl.multiple_of((i % 2) * TILE, TILE), pl.multiple_of(((i + 1) % 2) * TILE, TILE)


def _nt(a, b):
    return lax.dot_general(a, b, (((1,), (1,)), ((), ())), preferred_element_type=F32)


def _qkv_specs(gi, clamp_to=None):
    def spec(col0):
        def imap(hp, i):
            return (i if clamp_to is None else jnp.minimum(i, clamp_to), (col0 + gi * ATT) // 128 + hp)
        return pl.BlockSpec((TILE, 128), imap)
    return [spec(C_Q), spec(C_K), spec(C_V)]


def _att_fwd(proj, tables, gi, dil):
    t = proj.shape[0]
    dl = _Dilated(dil)

    def body(q_ref, k_ref, v_ref, c_ref, lo_ref, hi_ref, o_ref, lse_ref, tmp, qd, kd, vt, od, ld):
        i = pl.program_id(1)
        cur, prv = _slots(i)
        cs, lo, hi = c_ref[...], lo_ref[...], hi_ref[...]
        tmp[...] = _rope_apply(q_ref[...], cs, lo, hi) * SM_SCALE
        dl.spread(qd, 0, tmp, BF16)
        tmp[...] = _rope_apply(k_ref[...], cs, lo, hi)
        dl.spread(kd, cur, tmp, BF16)
        dl.spread(tmp, 0, v_ref, F32)
        _store_transposed(vt, cur, tmp)

        def block(b, carry):
            row, prev, has_prev = dl.block_rows(b, i, cur, prv)
            s = jnp.where(_pair_mask(has_prev), _nt(_key_pair(kd, prev, cur + row), _query_pair(qd[pl.ds(row, BLK), :])), NEG_INF)
            mx = jnp.max(s, axis=0, keepdims=True)
            p = jnp.exp(s - mx)
            den = jnp.sum(p, axis=0, keepdims=True)
            v_t = jnp.concatenate([vt[prev // BLK], vt[(cur + row) // BLK]], axis=1)
            acc = jnp.dot(v_t, p.astype(BF16), preferred_element_type=F32) / den
            lse = mx + jnp.log(den)
            od[pl.ds(row, BLK), :] = _own_head(acc).T
            ld[pl.ds(row, BLK), :] = _own_head(jnp.broadcast_to(lse, (2 * HEAD, 2 * BLK))).T
            return carry

        lax.fori_loop(0, TILE // BLK, block, 0, unroll=True)
        dl.gather(o_ref, od, 0)
        dl.gather(lse_ref, ld, 0)

    tab = pl.BlockSpec((TILE, 128), lambda hp, i: (i, 0))
    out_spec = pl.BlockSpec((TILE, 128), lambda hp, i: (i, hp))
    return pl.pallas_call(
        body, name=f"att_fwd_g{gi}", grid=(ATT // 128, t // TILE),
        in_specs=_qkv_specs(gi) + [tab] * 3,
        out_specs=[out_spec] * 2, out_shape=[jax.ShapeDtypeStruct((t, ATT), F32)] * 2,
        scratch_shapes=[pltpu.VMEM((TILE, 128), F32), pltpu.VMEM((TILE, 128), BF16), pltpu.VMEM((2 * TILE, 128), BF16),
                        pltpu.VMEM((2 * TILE // BLK, 128, BLK), BF16), pltpu.VMEM((TILE, 128), F32), pltpu.VMEM((TILE, 128), F32)],
    )(proj, proj, proj, *tables)


def _att_combine(parts, proj, tm):
    t = proj.shape[0]

    def body(o0, l0, o1, l1, o2, l2, z_ref, att_ref, lse_ref, pa_ref):
        m_all = jnp.maximum(jnp.maximum(l0[...], l1[...]), l2[...])
        w0, w1, w2 = jnp.exp(l0[...] - m_all), jnp.exp(l1[...] - m_all), jnp.exp(l2[...] - m_all)
        den = w0 + w1 + w2
        att = (w0 * o0[...] + w1 * o1[...] + w2 * o2[...]) / den
        z = z_ref[...]
        att_ref[...] = att
        lse_ref[...] = m_all + jnp.log(den)
        pa_ref[...] = (att * (z * _sig(z))).astype(BF16)

    spec = _rows(tm, ATT)
    return pl.pallas_call(
        body, name="att_combine", grid=(t // tm,),
        in_specs=[spec] * 6 + [_rows(tm, ATT, C_ZA // ATT)],
        out_specs=[spec] * 3,
        out_shape=[jax.ShapeDtypeStruct((t, ATT), F32)] * 2 + [jax.ShapeDtypeStruct((t, ATT), BF16)],
    )(*parts, proj)


def _att_bwd(proj, tables, datt, dsum, lse, gi, dil):
    t = proj.shape[0]
    nt = t // TILE
    dl = _Dilated(dil)

    def body(q_ref, k_ref, v_ref, c_ref, lo_ref, hi_ref, cl_ref, lol_ref, hil_ref, do_ref, ds_ref, lse_ref,
             dq_ref, dk_ref, dv_ref, tmp, qd, kd, vd, dod, dsd, lsd, dqd, dkd, dvd, kt):
        i = pl.program_id(1)
        cur, prv = _slots(i)

        @pl.when(i < nt)
        def _():
            cs, lo, hi = c_ref[...], lo_ref[...], hi_ref[...]
            tmp[...] = _rope_apply(q_ref[...], cs, lo, hi) * SM_SCALE
            dl.spread(qd, 0, tmp, BF16)
            tmp[...] = _rope_apply(k_ref[...], cs, lo, hi)
            dl.spread(kd, cur, tmp, BF16)
            dl.spread(dqd, 0, tmp, F32)
            _store_transposed(kt, cur, dqd)
            dl.spread(vd, cur, v_ref, BF16)
            dl.spread(dod, 0, do_ref, BF16)
            dl.spread(dsd, 0, ds_ref, F32)
            dl.spread(lsd, 0, lse_ref, F32)
            dkd[pl.ds(cur, TILE), :] = jnp.zeros((TILE, 128), F32)
            dvd[pl.ds(cur, TILE), :] = jnp.zeros((TILE, 128), F32)

            def block(b, carry):
                row, prev, has_prev = dl.block_rows(b, i, cur, prv)
                q_pair, do_pair = _query_pair(qd[pl.ds(row, BLK), :]), _query_pair(dod[pl.ds(row, BLK), :])
                k_pair, v_pair = _key_pair(kd, prev, cur + row), _key_pair(vd, prev, cur + row)
                ds_t, ls_t = dsd[pl.ds(row, BLK), :].T, lsd[pl.ds(row, BLK), :].T
                lse = jnp.concatenate([ls_t[0:1, :], ls_t[HEAD:HEAD + 1, :]], axis=1)
                dsm = jnp.concatenate([ds_t[0:1, :], ds_t[HEAD:HEAD + 1, :]], axis=1)
                p = jnp.exp(jnp.where(_pair_mask(has_prev), _nt(k_pair, q_pair), NEG_INF) - lse)
                ds = (p * (_nt(v_pair, do_pair) - dsm)).astype(BF16)
                k_t = jnp.concatenate([kt[prev // BLK], kt[(cur + row) // BLK]], axis=1)
                dqd[pl.ds(row, BLK), :] = _own_head(jnp.dot(k_t, ds, preferred_element_type=F32)).T * SM_SCALE
                dk = jnp.dot(ds, q_pair, preferred_element_type=F32)
                dv = jnp.dot(p.astype(BF16), do_pair, preferred_element_type=F32)
                dkd[pl.ds(cur + row, BLK), :] += dk[BLK:2 * BLK, :]
                dvd[pl.ds(cur + row, BLK), :] += dv[BLK:2 * BLK, :]
                dkd[pl.ds(prev, BLK), :] += dk[0:BLK, :]
                dvd[pl.ds(prev, BLK), :] += dv[0:BLK, :]
                return carry

            lax.fori_loop(0, TILE // BLK, block, 0, unroll=True)
            dl.gather(tmp, dqd, 0)
            dq_ref[...] = _rope_apply(tmp[...], cs, -lo, -hi).astype(BF16)

        @pl.when(i > 0)
        def _():
            dl.gather(tmp, dkd, prv)
            dk_ref[...] = _rope_apply(tmp[...], cl_ref[...], -lol_ref[...], -hil_ref[...]).astype(BF16)
            dl.gather(tmp, dvd, prv)
            dv_ref[...] = tmp[...].astype(BF16)

    now = lambda col: pl.BlockSpec((TILE, 128), lambda hp, i: (jnp.minimum(i, nt - 1), col(hp)))
    lag = lambda col: pl.BlockSpec((TILE, 128), lambda hp, i: (jnp.maximum(i - 1, 0), col(hp)))
    first, pair = (lambda hp: 0), (lambda hp: hp)
    return pl.pallas_call(
        body, name=f"att_bwd_g{gi}", grid=(ATT // 128, nt + 1),
        in_specs=_qkv_specs(gi, nt - 1) + [now(first)] * 3 + [lag(first)] * 3 + [now(pair)] * 3,
        out_specs=[now(pair), lag(pair), lag(pair)],
        out_shape=[jax.ShapeDtypeStruct((t, ATT), BF16)] * 3,
        scratch_shapes=[pltpu.VMEM((TILE, 128), F32), pltpu.VMEM((TILE, 128), BF16), pltpu.VMEM((2 * TILE, 128), BF16),
                        pltpu.VMEM((2 * TILE, 128), BF16), pltpu.VMEM((TILE, 128), BF16), pltpu.VMEM((TILE, 128), F32),
                        pltpu.VMEM((TILE, 128), F32), pltpu.VMEM((TILE, 128), F32), pltpu.VMEM((2 * TILE, 128), F32),
                        pltpu.VMEM((2 * TILE, 128), F32), pltpu.VMEM((2 * TILE // BLK, 128, BLK), BF16)],
    )(proj, proj, proj, *tables, *tables, datt, dsum, lse)


def _merge_head(pc, pa, proj, x, mod, final_g, target, w_co, w_ao, w_o, u1, ln_g, ln_b, att, tm):
    t = x.shape[0]

    def body(pc_ref, pa_ref, gc_ref, ga_ref, x_ref, mod_ref, fg_ref, tg_ref, wco_ref, wao_ref, wo_ref,
             zc_ref, u1_ref, lg_ref, lb_ref, att_ref, za_ref,
             merged_ref, do_ref, dyc_ref, dya_ref, dout_ref, du1_ref, dzc_ref, datt_ref, ds_ref, dp_ref,
             sq_ref, gfg_ref, dgate_ref, dlg_ref, dlb_ref, dcb_ref):
        i = pl.program_id(0)
        yc = jnp.dot(pc_ref[...], wco_ref[...], preferred_element_type=F32)
        ya = jnp.dot(pa_ref[...], wao_ref[...], preferred_element_type=F32)
        sc, sa = _sig(gc_ref[...]), _sig(ga_ref[...])
        merged = (sc * yc + sa * ya).astype(BF16)
        merged_ref[...] = merged
        ov = jnp.dot(merged, wo_ref[...], preferred_element_type=F32)
        gate = mod_ref[:, 2 * D:3 * D]
        out = x_ref[...] + gate * ov
        r = lax.rsqrt(jnp.mean(out * out, axis=-1, keepdims=True) + EPS)
        yn = out * r
        diff = yn * fg_ref[...] - tg_ref[...]
        dy = diff * (1.0 / D)
        gy = dy * fg_ref[...]
        dout = r * (gy - yn * jnp.mean(gy * yn, axis=-1, keepdims=True))
        dout_ref[...] = dout
        do = (dout * gate).astype(BF16)
        do_ref[...] = do
        _acc_rows(sq_ref, i, diff * diff)
        _acc_rows(gfg_ref, i, dy * yn)
        _acc_rows(dgate_ref, i, dout * ov)
        dm = _nt(do, wo_ref[...])
        dyc = (dm * sc).astype(BF16)
        dya = (dm * sa).astype(BF16)
        dyc_ref[...] = dyc
        dya_ref[...] = dya
        dp_ref[:, ATT:ATT + D] = (dm * yc * sc * (1.0 - sc)).astype(BF16)
        dp_ref[:, ATT + D:ATT + 2 * D] = (dm * ya * sa * (1.0 - sa)).astype(BF16)

        dpc = _nt(dyc, wco_ref[...])
        u1v = u1_ref[...]
        xc = u1v - jnp.mean(u1v, axis=-1, keepdims=True)
        rs = lax.rsqrt(jnp.mean(xc * xc, axis=-1, keepdims=True) + EPS)
        uhat = xc * rs
        u2 = uhat * lg_ref[...] + lb_ref[...]
        s2 = _sig(u2)
        zc = zc_ref[...]
        szc = _sig(zc)
        dzc_ref[...] = (dpc * (u2 * s2) * _dsilu(zc, szc)).astype(BF16)
        du2 = dpc * (zc * szc) * _dsilu(u2, s2)
        duhat = du2 * lg_ref[...]
        du1 = rs * (duhat - jnp.mean(duhat, axis=-1, keepdims=True) - uhat * jnp.mean(duhat * uhat, axis=-1, keepdims=True))
        du1_ref[...] = du1
        _acc_rows(dlg_ref, i, du2 * uhat)
        _acc_rows(dlb_ref, i, du2)
        _acc_rows(dcb_ref, i, du1)

        dpa = _nt(dya, wao_ref[...])
        za, att_v = za_ref[...], att_ref[...]
        sza = _sig(za)
        datt = dpa * (za * sza)
        datt_ref[...] = datt
        dp_ref[:, 0:ATT] = (dpa * att_v * _dsilu(za, sza)).astype(BF16)
        prod = datt * att_v
        for hd in range(ATT // HEAD):
            sl = slice(hd * HEAD, (hd + 1) * HEAD)
            ds_ref[:, sl] = jnp.broadcast_to(jnp.sum(prod[:, sl], axis=-1, keepdims=True), (tm, HEAD))

    vec = _full((1, D))
    bf = lambda w: jax.ShapeDtypeStruct((t, w), BF16)
    f32 = lambda w: jax.ShapeDtypeStruct((t, w), F32)
    tail = ATT + 2 * D
    return pl.pallas_call(
        body, name="merge_head", grid=(t // tm,),
        in_specs=[_rows(tm, D), _rows(tm, ATT), _rows(tm, D, C_GC // D), _rows(tm, D, C_GA // D), _rows(tm, D),
                  _full((1, 3 * D)), vec, _rows(tm, D), _full((D, D)), _full((ATT, D)), _full((D, D)),
                  _rows(tm, D, C_ZC // D), _rows(tm, D), vec, vec, _rows(tm, ATT), _rows(tm, ATT, C_ZA // ATT)],
        out_specs=[_rows(tm, D)] * 7 + [_rows(tm, ATT), _rows(tm, ATT), _rows(tm, tail, C_ZA // tail)] + [vec] * 6,
        out_shape=[bf(D), bf(D), bf(D), bf(D), f32(D), f32(D), bf(D), f32(ATT), f32(ATT), bf(N_COL)]
        + [jax.ShapeDtypeStruct((1, D), F32)] * 6,
    )(pc, pa, proj, proj, x, mod, final_g, target, w_co, w_ao, w_o, proj, u1, ln_g, ln_b, att, proj)


def _acc_rows(ref, i, val):
    @pl.when(i == 0)
    def _():
        ref[...] = jnp.zeros_like(ref)

    ref[...] += jnp.sum(val, axis=0, keepdims=True)


def _conv_bwd_taps(du1, proj, conv_w, dzc, dqkv, dproj, tm):
    t = proj.shape[0]
    hb = tm // HALO
    last = t // HALO - 1

    def body(du_ref, duh_ref, a_ref, b_ref, ah_ref, bh_ref, w_ref, dzc_ref, *rest):
        qkv_refs, (dp_in, dp_ref, dw_ref, dbuf, ubuf, g0, shd, shu) = rest[:9], rest[9:]
        del dp_in
        dp_ref[:, C_ZC:C_ZC + D] = dzc_ref[...]
        for n, ref in enumerate(qkv_refs):
            dp_ref[:, C_Q + n * ATT:C_Q + (n + 1) * ATT] = ref[...]
        i = pl.program_id(0)
        a, sb = a_ref[...], _sig(b_ref[...])
        ubuf[0:HALO, :] = jnp.where(i > 0, ah_ref[...] * _sig(bh_ref[...]), 0.0)
        ubuf[HALO:HALO + tm, :] = a * sb
        dbuf[0:tm, :] = du_ref[...]
        dbuf[tm:tm + HALO, :] = jnp.where(i < pl.num_programs(0) - 1, duh_ref[...], 0.0)

        @pl.when(i == 0)
        def _():
            dw_ref[...] = jnp.zeros_like(dw_ref)

        def col(ci, carry):
            c0 = pl.multiple_of(ci * 128, 128)
            _shift_copies(shd, dbuf, c0)
            _shift_copies(shu, ubuf, c0)
            for rc in range(tm // 64):
                g0[rc * 64:(rc + 1) * 64, pl.ds(c0, 128)] = _conv_taps(
                    jnp.zeros((64, 128), F32), w_ref, dbuf, shd, rc * 64, c0, lambda j: CONV_K - 1 - j)
            for j in range(CONV_K):
                part = jnp.zeros((8, 128), F32)
                for rc in range(tm // 64):
                    off = rc * 64 + HALO - (CONV_K - 1) + j
                    prod = dbuf[rc * 64:(rc + 1) * 64, pl.ds(c0, 128)] * _window64(ubuf, shu, c0, off)
                    part = part + jnp.sum(prod.reshape(8, 8, 128), axis=0)
                dw_ref[j:j + 1, pl.ds(c0, 128)] += jnp.sum(part, axis=0, keepdims=True)
            return carry

        lax.fori_loop(0, D // 128, col, 0)
        du0 = g0[...]
        dp_ref[:, 0:D] = (du0 * sb).astype(BF16)
        dp_ref[:, D:2 * D] = (du0 * a * sb * (1.0 - sb)).astype(BF16)

    prev = lambda col: pl.BlockSpec((HALO, D), lambda i: (jnp.maximum(i * hb - 1, 0), col))
    nxt = pl.BlockSpec((HALO, D), lambda i: (jnp.minimum((i + 1) * hb, last), 0))
    return pl.pallas_call(
        body, name="conv_bwd_taps", grid=(t // tm,),
        in_specs=[_rows(tm, D), nxt, _rows(tm, D, 0), _rows(tm, D, 1), prev(0), prev(1), _full((CONV_KP, D)),
                  _rows(tm, D)] + [_rows(tm, ATT)] * 9 + [ANY],
        out_specs=[_rows(tm, C_ZA, 0), _full((CONV_KP, D))],
        out_shape=[jax.ShapeDtypeStruct((t, N_COL), BF16), jax.ShapeDtypeStruct((CONV_KP, D), F32)],
        scratch_shapes=[pltpu.VMEM((tm + HALO, D), F32), pltpu.VMEM((HALO + tm, D), F32), pltpu.VMEM((tm, D), F32),
                        pltpu.VMEM((8, HALO + tm, 128), F32), pltpu.VMEM((8, HALO + tm, 128), F32)],
        input_output_aliases={17: 0},
    )(du1, du1, proj, proj, proj, proj, conv_w, dzc, *dqkv, dproj)


def _prenorm_bwd(dh, x, dout, mod, norm_g, tm):
    t = x.shape[0]

    def body(dh_ref, x_ref, dout_ref, mod_ref, g_ref, gx_ref, dshift_ref, dscale_ref, dg_ref):
        i = pl.program_id(0)
        xv, dhv = x_ref[...], dh_ref[...]
        r = lax.rsqrt(jnp.mean(xv * xv, axis=-1, keepdims=True) + EPS)
        xn = xv * r
        one_scale = 1.0 + mod_ref[:, D:2 * D]
        dxn = dhv * (g_ref[...] * one_scale)
        gx_ref[...] = r * (dxn - xn * jnp.mean(dxn * xn, axis=-1, keepdims=True)) + dout_ref[...]
        _acc_rows(dshift_ref, i, dhv)
        _acc_rows(dscale_ref, i, dhv * xn * g_ref[...])
        _acc_rows(dg_ref, i, dhv * xn * one_scale)

    vec = _full((1, D))
    return pl.pallas_call(
        body, name="prenorm_bwd", grid=(t // tm,),
        in_specs=[_rows(tm, D), _rows(tm, D), _rows(tm, D), _full((1, 3 * D)), vec],
        out_specs=[_rows(tm, D), vec, vec, vec],
        out_shape=[jax.ShapeDtypeStruct((t, D), F32)] + [jax.ShapeDtypeStruct((1, D), F32)] * 3,
    )(dh, x, dout, mod, norm_g)


def _sum_devices(gathered):
    w = gathered.shape[-1]

    def body(g_ref, o_ref):
        acc = g_ref[0]
        for j in range(1, N_DEV):
            acc = acc + g_ref[j]
        o_ref[...] = acc

    return pl.pallas_call(body, name="sum_devices", grid=(1,), in_specs=[_full(gathered.shape)], out_specs=_full((1, w)),
                          out_shape=jax.ShapeDtypeStruct((1, w), F32))(gathered)


def _rope_tables(positions):
    half = HEAD // 8
    t = positions.shape[-1]
    inv_freq = ROPE_THETA ** (-(jnp.arange(half, dtype=F32) * 2.0 / (2 * half)))
    ang = positions.reshape(t, 1).astype(F32) * inv_freq
    cos, sin = jnp.cos(ang), jnp.sin(ang)
    zeros = lambda n: jnp.zeros((t, n), F32)
    c64 = jnp.concatenate([cos, cos, jnp.ones((t, HEAD - 2 * half), F32)], axis=1)
    lo64 = jnp.concatenate([-sin, zeros(HEAD - half)], axis=1)
    hi64 = jnp.concatenate([zeros(half), sin, zeros(HEAD - 2 * half)], axis=1)
    return tuple(jnp.tile(a, (1, 2)) for a in (c64, lo64, hi64))


def kernel(x, c, positions, norm_g, w_ada, b_ada, w_in, conv_w, conv_b, conv_ln_g, conv_ln_b, w_conv_out, w_att_out, w_o, final_g, loss_target, m_norm_g, m_w_ada, m_b_ada, m_w_in, m_conv_w, m_conv_b, m_conv_ln_g, m_conv_ln_b, m_w_conv_out, m_w_att_out, m_w_o, m_final_g, v_norm_g, v_w_ada, v_b_ada, v_w_in, v_conv_w, v_conv_b, v_conv_ln_g, v_conv_ln_b, v_w_conv_out, v_w_att_out, v_w_o, v_final_g):
    me = 4 * lax.axis_index("x") + 2 * lax.axis_index("y") + lax.axis_index("c")
    x2, tgt = x[0], loss_target[0]
    t = x2.shape[0]
    te = 512 if t % 512 == 0 else 256
    tcv = 256
    tmm = 1024 if t % 1024 == 0 else 256
    n_ada = w_ada.shape[-1]

    pad_taps = lambda a: jnp.pad(a[0], ((0, CONV_KP - CONV_K), (0, 0)))
    shards = (_cast_bf16(w_in[0], "cast_w_in"), _cast_bf16(w_conv_out[0], "cast_w_conv_out"),
              _cast_bf16(w_att_out[0], "cast_w_att_out"), _cast_bf16(w_o[0], "cast_w_o"), pad_taps(conv_w))
    block_of = lambda relations: jnp.bitwise_xor(me, jnp.array(relations, jnp.int32))

    c_all = _allgather_small(c, "gather_c").reshape(N_DEV, D)
    b_ada_l = lax.dynamic_slice(b_ada, (0, me * n_ada), (1, n_ada))
    parts = _allgather_small(_mod_part(c_all, w_ada[0], b_ada_l), "gather_mod")
    mod = lax.dynamic_slice(parts, (0, me, 0), (N_DEV, 1, n_ada)).reshape(1, N_DEV * n_ada)

    h = _prenorm(x2, mod, norm_g, te)
    proj, w_in_f, w_co_f, w_ao_f, w_o_f, conv_w_f = _proj_gather(h, shards, block_of(GATHER_ORDER), tmm)
    u1, pc = _conv_fwd(proj, conv_w_f, conv_b, conv_ln_g, conv_ln_b, tcv)
    tables = _rope_tables(positions)
    parts_att = []
    for gi, dil in GROUPS:
        parts_att += _att_fwd(proj, tables, gi, dil)
    att, lse, pa = _att_combine(parts_att, proj, te)

    (merged, do, dyc, dya, dout, du1, dzc, datt, dsum, dproj,
     sq_sum, g_final, d_gate, d_ln_g, d_ln_b, d_conv_b) = _merge_head(
        pc, pa, proj, x2, mod, final_g.reshape(1, D), tgt, w_co_f, w_ao_f, w_o_f, u1, conv_ln_g, conv_ln_b, att, tcv)

    dw_o = _matmul(merged, do, ta=True, out_dtype=BF16, tm=D, tn=D, tk=512, name="dw_o")
    dw_co = _matmul(pc, dyc, ta=True, out_dtype=BF16, tm=D, tn=D, tk=512, name="dw_conv_out")
    dw_ao = _matmul(pa, dya, ta=True, out_dtype=BF16, tm=ATT, tn=D, tk=512, name="dw_att_out")
    dqs, dks, dvs = [], [], []
    for gi, dil in GROUPS:
        dq, dk, dv = _att_bwd(proj, tables, datt, dsum, lse, gi, dil)
        dqs.append(dq), dks.append(dk), dvs.append(dv)
    dproj, dconv_w = _conv_bwd_taps(du1, proj, conv_w_f, dzc, dqs + dks + dvs, dproj, tcv)
    dh = _matmul(dproj, w_in_f, tb=True, tm=tmm, tn=D, tk=N_COL // N_DEV, name="d_h")
    grad_x, d_shift, d_scale, d_norm_g = _prenorm_bwd(dh, x2, dout, mod, norm_g, te)

    packed = jnp.concatenate([d_shift, d_scale, d_gate, d_norm_g, d_conv_b, d_ln_g, d_ln_b, g_final, sq_sum], axis=1)
    gathered = _allgather_small(packed, "gather_partials")
    total = _sum_devices(gathered)
    seg = lambda k, n=1: total[:, k * D:(k + n) * D]
    g_b_ada, g_norm_g, g_conv_b, g_ln_g, g_ln_b, g_final_g = seg(0, 3), seg(3), seg(4), seg(5), seg(6), seg(7)
    loss = (0.5 / D) * jnp.sum(seg(8))
    dmod_all = gathered[:, 0, 0:3 * D]
    dmod_cols = lax.dynamic_slice(dmod_all, (0, me * n_ada), (N_DEV, n_ada))
    g_w_ada, d_w_ada, nm_w_ada, nv_w_ada = _w_ada_update(c_all.T, dmod_cols, w_ada[0], m_w_ada[0], v_w_ada[0])

    small = {}
    for name, g, w, m, v in (("norm_g", g_norm_g, norm_g, m_norm_g, v_norm_g), ("b_ada", g_b_ada, b_ada, m_b_ada, v_b_ada),
                             ("conv_b", g_conv_b, conv_b, m_conv_b, v_conv_b), ("conv_ln_g", g_ln_g, conv_ln_g, m_conv_ln_g, v_conv_ln_g),
                             ("conv_ln_b", g_ln_b, conv_ln_b, m_conv_ln_b, v_conv_ln_b),
                             ("final_g", g_final_g, final_g.reshape(1, D), m_final_g.reshape(1, D), v_final_g.reshape(1, D))):
        small[name] = (g,) + tuple(_adamw_small(g, w, m, v, "adamw_" + name))

    slots = _dw_in_scatter(h, dproj, (dw_co, dw_ao, dw_o, dconv_w), block_of(SCATTER_ORDER), 512)
    big = {
        "w_in": _sum_adamw(slots[0], w_in[0], m_w_in[0], v_w_in[0], 256, "adamw_w_in"),
        "w_conv_out": _sum_adamw(slots[1], w_conv_out[0], m_w_conv_out[0], v_w_conv_out[0], 128, "adamw_w_conv_out"),
        "w_att_out": _sum_adamw(slots[2], w_att_out[0], m_w_att_out[0], v_w_att_out[0], 512, "adamw_w_att_out"),
        "w_o": _sum_adamw(slots[3], w_o[0], m_w_o[0], v_w_o[0], 128, "adamw_w_o"),
        "conv_w": [r[:CONV_K] for r in _sum_adamw(slots[4], pad_taps(conv_w), pad_taps(m_conv_w), pad_taps(v_conv_w), CONV_KP, "adamw_conv_w")],
    }
    big["w_ada"] = (g_w_ada, d_w_ada, nm_w_ada, nv_w_ada)

    order = ("norm_g", "w_ada", "b_ada", "w_in", "conv_w", "conv_b", "conv_ln_g", "conv_ln_b", "w_conv_out", "w_att_out", "w_o", "final_g")
    lead = lambda name, a: a.reshape(D) if name == "final_g" else (a[None] if name in big else a)
    result = {**small, **big}
    outs = [loss, grad_x[None]]
    for field in range(4):
        outs += [lead(name, result[name][field]) for name in order]
    return tuple(outs)
```

```python
import functools

import jax
import jax.numpy as jnp
from jax import lax
from jax.experimental import pallas as pl
from jax.experimental.pallas import tpu as pltpu

F32 = jnp.float32
BF16 = jnp.bfloat16

N_DEV = 8
D = 1024
N_COL = 10240
C_A, C_B, C_ZC, C_Q, C_K, C_V, C_ZA, C_GC, C_GA = 0, 1024, 2048, 3072, 4608, 6144, 7680, 8192, 9216
QKV = 1536
ATT = 512
HEAD = 64
BLK = 128
TILE = 2048
GROUPS = ((0, 1), (1, 4), (2, 16))
CONV_K = 31
CONV_KP = 32
HALO = 32
EPS = 1e-6
NEG_INF = -1e30
ROPE_THETA = 500000.0
SM_SCALE = HEAD ** -0.5

ADAM_LR, ADAM_B1, ADAM_B2, ADAM_EPS, ADAM_WD, ADAM_STEP = 0.001, 0.9, 0.999, 1e-08, 0.01, 10

MESH = pl.DeviceIdType.MESH
ANY = pl.BlockSpec(memory_space=pl.ANY)


def _sig(v):
    return 1.0 / (1.0 + jnp.exp(-v))


def _dsilu(v, s):
    return s * (1.0 + v * (1.0 - s))


def _full(shape):
    return pl.BlockSpec(shape, lambda *_: (0,) * len(shape))


def _rows(tm, width, col=0):
    return pl.BlockSpec((tm, width), lambda i: (i, col))


def _matmul(a, b, *, ta=False, tb=False, out_dtype=F32, tm, tn, tk, name):
    m, k = (a.shape[1], a.shape[0]) if ta else a.shape
    stacked = b.ndim == 3
    if stacked:
        assert tb and b.shape[2] == tk and b.shape[0] * tk == k
        n = b.shape[1]
    else:
        n = b.shape[0] if tb else b.shape[1]
        assert (b.shape[1] if tb else b.shape[0]) == k
    assert m % tm == 0 and n % tn == 0 and k % tk == 0
    nk = k // tk
    dims = (((0 if ta else 1,), (1 if tb else 0,)), ((), ()))
    use_scratch = out_dtype != F32 and nk > 1

    def body(a_ref, b_ref, o_ref, *scratch):
        p = lax.dot_general(a_ref[...], b_ref[...], dims, preferred_element_type=F32)
        if nk == 1:
            o_ref[...] = p.astype(out_dtype)
            return
        acc = scratch[0] if use_scratch else o_ref
        kk = pl.program_id(2)

        @pl.when(kk == 0)
        def _():
            acc[...] = p

        @pl.when(kk > 0)
        def _():
            acc[...] += p

        if use_scratch:
            @pl.when(kk == nk - 1)
            def _():
                o_ref[...] = acc[...].astype(out_dtype)

    a_spec = pl.BlockSpec((tk, tm), lambda i, j, kk: (kk, i)) if ta else pl.BlockSpec((tm, tk), lambda i, j, kk: (i, kk))
    b_spec = pl.BlockSpec((tn, tk), lambda i, j, kk: (j, kk)) if tb else pl.BlockSpec((tk, tn), lambda i, j, kk: (kk, j))
    if stacked:
        b_spec = pl.BlockSpec((None, tn, tk), lambda i, j, kk: (kk, j, 0))
    return pl.pallas_call(
        body, name=name, grid=(m // tm, n // tn, nk),
        in_specs=[a_spec, b_spec],
        out_specs=pl.BlockSpec((tm, tn), lambda i, j, kk: (i, j)),
        out_shape=jax.ShapeDtypeStruct((m, n), out_dtype),
        scratch_shapes=[pltpu.VMEM((tm, tn), F32)] if use_scratch else [],
    )(a, b)


def _me_and_peers():
    x, y, c = lax.axis_index("x"), lax.axis_index("y"), lax.axis_index("c")
    me = 4 * x + 2 * y + c
    peers = []
    for k in range(1, N_DEV):
        px, py, pc = x ^ (k >> 2), y ^ ((k >> 1) & 1), c ^ (k & 1)
        peers.append(((px, py, pc), 4 * px + 2 * py + pc))
    return me, peers


def _allgather_small(v, name):
    r, c = v.shape

    def body(v_ref, out_ref, send_sems, recv_sems):
        me, peers = _me_and_peers()
        out_ref[me] = v_ref[...]
        copies = []
        for k, (dev, _) in enumerate(peers):
            cp = pltpu.make_async_remote_copy(src_ref=v_ref, dst_ref=out_ref.at[me], send_sem=send_sems.at[k],
                                              recv_sem=recv_sems.at[k], device_id=dev, device_id_type=MESH)
            cp.start()
            copies.append(cp)
        for k, (dev, idx) in enumerate(peers):
            pltpu.make_async_remote_copy(src_ref=v_ref, dst_ref=out_ref.at[idx], send_sem=send_sems.at[k],
                                         recv_sem=recv_sems.at[k], device_id=dev, device_id_type=MESH).wait_recv()
        for cp in copies:
            cp.wait_send()

    return pl.pallas_call(
        body, name=name,
        in_specs=[pl.BlockSpec(memory_space=pltpu.VMEM)],
        out_specs=pl.BlockSpec(memory_space=pltpu.VMEM),
        out_shape=jax.ShapeDtypeStruct((N_DEV, r, c), v.dtype),
        scratch_shapes=[pltpu.SemaphoreType.DMA((N_DEV - 1,)), pltpu.SemaphoreType.DMA((N_DEV - 1,))],
    )(v)


def _window(ref, kind, idx, size):
    if kind == "block":
        return ref.at[idx]
    start = pl.multiple_of(idx * size, size)
    if kind == "rows":
        return ref.at[pl.ds(start, size), :]
    return ref.at[:, pl.ds(start, size)]


_BIG = (("cols", N_COL // N_DEV), ("rows", D // N_DEV), ("cols", D // N_DEV), ("rows", D // N_DEV), ("cols", D // N_DEV))
_GATHERED = (("block", 1),) + _BIG[1:]


GATHER_ORDER = (0, 1, 2, 4, 3, 5, 6, 7)
W_IN_DIRECT = (1, 2, 4, 6)
SCATTER_ORDER = (7, 6, 5, 4, 3, 2, 1, 0)
W_IN_SLOT = {0: 0, 1: 1, 2: 2, 4: 3, 6: 4}


def _proj_gather(h, shards, order, tm):
    t = h.shape[0]
    nt = len(shards)
    n_blk = N_COL // N_DEV
    full_shapes = []
    for s, (kind, size) in zip(shards, _GATHERED):
        full_shapes.append(jax.ShapeDtypeStruct({"block": (N_DEV,) + s.shape, "rows": (s.shape[0] * N_DEV, s.shape[1]),
                                                 "cols": (s.shape[0], s.shape[1] * N_DEV)}[kind], s.dtype))
    last = (N_DEV - 1, t // tm - 1)

    def body(order_ref, h_ref, *refs):
        src, proj_ref, dst = refs[:nt], refs[nt], refs[nt + 1:2 * nt + 1]
        w_all, send_sems, recv_sems, local_sems, keep_sems = refs[2 * nt + 1:]
        j, i = pl.program_id(0), pl.program_id(1)
        me, peers = _me_and_peers()

        def landing(tn, idx):
            kind, size = _GATHERED[tn]
            return w_all.at[idx] if tn == 0 else _window(dst[tn], kind, idx, size)

        def local(tn):
            return pltpu.make_async_copy(src[tn], landing(tn, me), local_sems.at[tn])

        def remote(tn, k, block_of):
            dev, idx = peers[k - 1]
            return pltpu.make_async_remote_copy(src_ref=src[tn], dst_ref=landing(tn, me if block_of == "mine" else idx),
                                                send_sem=send_sems.at[tn, k - 1], recv_sem=recv_sems.at[tn, k - 1],
                                                device_id=dev, device_id_type=MESH)

        def forward(k):
            block = w_all.at[peers[k - 1][1]]
            return pltpu.make_async_remote_copy(src_ref=block, dst_ref=block, send_sem=send_sems.at[0, k], recv_sem=recv_sems.at[0, k],
                                                device_id=peers[0][0], device_id_type=MESH)

        def keep(step):
            blk = order_ref[step]
            return pltpu.make_async_copy(w_all.at[blk], dst[0].at[blk], keep_sems.at[step])

        @pl.when((j == 0) & (i == 0))
        def _():
            for tn in range(nt):
                local(tn).start()
                for k in GATHER_ORDER[1:]:
                    if tn > 0 or k in W_IN_DIRECT:
                        remote(tn, k, "mine").start()

        @pl.when(i == 0)
        def _():
            for step, k in enumerate(GATHER_ORDER):
                @pl.when(j == step)
                def _():
                    if k == 0:
                        local(0).wait()
                    else:
                        remote(0, k, "theirs").wait_recv()
                        if k in W_IN_DIRECT and k > 1:
                            forward(k).start()
                    keep(step).start()

        proj_ref[...] = jnp.dot(h_ref[...], w_all[order_ref[j]], preferred_element_type=F32)

        @pl.when((j == last[0]) & (i == last[1]))
        def _():
            for step in range(N_DEV):
                keep(step).wait()
            for tn in range(1, nt):
                local(tn).wait()
                for k in range(1, N_DEV):
                    remote(tn, k, "theirs").wait_recv()
            for tn in range(nt):
                for k in range(1, N_DEV):
                    if tn > 0 or k in W_IN_DIRECT:
                        remote(tn, k, "mine").wait_send()
                    else:
                        forward(k - 1).wait_send()

    grid_spec = pltpu.PrefetchScalarGridSpec(
        num_scalar_prefetch=1, grid=(N_DEV, t // tm),
        in_specs=[pl.BlockSpec((tm, D), lambda j, i, order_ref: (i, 0))] + [ANY] * nt,
        out_specs=[pl.BlockSpec((tm, n_blk), lambda j, i, order_ref: (i, order_ref[j]))] + [ANY] * nt,
        scratch_shapes=[pltpu.VMEM((N_DEV, D, n_blk), BF16), pltpu.SemaphoreType.DMA((nt, N_DEV - 1)),
                        pltpu.SemaphoreType.DMA((nt, N_DEV - 1)), pltpu.SemaphoreType.DMA((nt,)), pltpu.SemaphoreType.DMA((N_DEV,))],
    )
    return pl.pallas_call(
        body, name="proj_gather", grid_spec=grid_spec,
        out_shape=[jax.ShapeDtypeStruct((t, N_COL), F32)] + full_shapes,
    )(order, h, *shards)


def _dw_in_scatter(ht, dproj, small_grads, order, tk):
    t = ht.shape[1]
    nt = 1 + len(small_grads)
    n_blk = N_COL // N_DEV
    nk = t // tk
    slot_shapes = [jax.ShapeDtypeStruct((len(W_IN_SLOT), D, n_blk), BF16)]
    for g, (kind, size) in zip(small_grads, _BIG[1:]):
        slot_shapes.append(jax.ShapeDtypeStruct((N_DEV,) + ((size, g.shape[1]) if kind == "rows" else (g.shape[0], size)), g.dtype))

    def body(order_ref, h_ref, dp_ref, *refs):
        src, dst = refs[:nt - 1], refs[nt - 1:2 * nt - 1]
        acc, stage, partner, send_sems, recv_sems, local_sems, pair_send, pair_recv = refs[2 * nt - 1:]
        j, kk = pl.program_id(0), pl.program_id(1)
        me, peers = _me_and_peers()

        def small_local(tn):
            kind, size = _BIG[tn]
            return pltpu.make_async_copy(_window(src[tn - 1], kind, me, size), dst[tn].at[me], local_sems.at[tn])

        def small_remote(tn, k, mine):
            kind, size = _BIG[tn]
            dev, idx = peers[k - 1]
            return pltpu.make_async_remote_copy(src_ref=_window(src[tn - 1], kind, idx if mine else me, size),
                                                dst_ref=dst[tn].at[me if mine else idx],
                                                send_sem=send_sems.at[tn, k - 1], recv_sem=recv_sems.at[tn, k - 1],
                                                device_id=dev, device_id_type=MESH)

        def push(step):
            k, slot = SCATTER_ORDER[step], step % 2
            if k == 0:
                return pltpu.make_async_copy(stage.at[slot], dst[0].at[W_IN_SLOT[0]], local_sems.at[0])
            if k not in W_IN_SLOT:
                p = (k - 3) // 2
                return pltpu.make_async_remote_copy(src_ref=stage.at[slot], dst_ref=partner.at[p], send_sem=pair_send.at[p],
                                                    recv_sem=pair_recv.at[p], device_id=peers[0][0], device_id_type=MESH)
            return pltpu.make_async_remote_copy(src_ref=stage.at[slot], dst_ref=dst[0].at[W_IN_SLOT[k]],
                                                send_sem=send_sems.at[0, k - 1], recv_sem=recv_sems.at[0, k - 1],
                                                device_id=peers[k - 1][0], device_id_type=MESH)

        @pl.when((j == 0) & (kk == 0))
        def _():
            for tn in range(1, nt):
                small_local(tn).start()
                for k in range(1, N_DEV):
                    small_remote(tn, k, True).start()

        p = jnp.dot(h_ref[...], dp_ref[...], preferred_element_type=F32)

        @pl.when(kk == 0)
        def _():
            acc[...] = p

        @pl.when(kk > 0)
        def _():
            acc[...] += p

        @pl.when(kk == nk - 1)
        def _():
            for step, k in enumerate(SCATTER_ORDER):
                @pl.when(j == step)
                def _():
                    if step >= 2:
                        push(step - 2).wait_send()
                    total = acc[...]
                    if k in W_IN_SLOT and k >= 2:
                        p = k // 2 - 1
                        push(SCATTER_ORDER.index(k + 1)).wait_recv()
                        total = total + partner[p].astype(F32)
                    stage[step % 2] = total.astype(BF16)
                    push(step).start()

        @pl.when((j == N_DEV - 1) & (kk == nk - 1))
        def _():
            push(N_DEV - 2).wait_send()
            push(N_DEV - 1).wait()
            for k in (1, 2, 4, 6):
                push(SCATTER_ORDER.index(k)).wait_recv()
            for tn in range(1, nt):
                small_local(tn).wait()
                for k in range(1, N_DEV):
                    small_remote(tn, k, False).wait_recv()
                    small_remote(tn, k, True).wait_send()

    grid_spec = pltpu.PrefetchScalarGridSpec(
        num_scalar_prefetch=1, grid=(N_DEV, nk),
        in_specs=[pl.BlockSpec((D, tk), lambda j, kk, order_ref: (0, kk)),
                  pl.BlockSpec((tk, n_blk), lambda j, kk, order_ref: (kk, order_ref[j]))] + [ANY] * (nt - 1),
        out_specs=[ANY] * nt,
        scratch_shapes=[pltpu.VMEM((D, n_blk), F32), pltpu.VMEM((2, D, n_blk), BF16), pltpu.VMEM((3, D, n_blk), BF16),
                        pltpu.SemaphoreType.DMA((nt, N_DEV - 1)), pltpu.SemaphoreType.DMA((nt, N_DEV - 1)),
                        pltpu.SemaphoreType.DMA((nt,)), pltpu.SemaphoreType.DMA((3,)), pltpu.SemaphoreType.DMA((3,))],
    )
    return pl.pallas_call(body, name="dw_in_scatter", grid_spec=grid_spec, out_shape=slot_shapes)(order, ht, dproj, *small_grads)


def _adamw_math(w, g, m, v):
    m = ADAM_B1 * m + (1.0 - ADAM_B1) * g
    v = ADAM_B2 * v + (1.0 - ADAM_B2) * (g * g)
    m_hat = m / (1.0 - ADAM_B1 ** ADAM_STEP)
    v_hat = v / (1.0 - ADAM_B2 ** ADAM_STEP)
    delta = -ADAM_LR * (m_hat / (jnp.sqrt(v_hat) + ADAM_EPS) + ADAM_WD * w)
    return delta, m, v


def _sum_adamw(slots, w, m, v, tr, name):
    n_slots, r, c = slots.shape
    assert r % tr == 0

    def body(s_ref, w_ref, m_ref, v_ref, g_ref, d_ref, nm_ref, nv_ref):
        g = s_ref[0].astype(F32)
        for j in range(1, n_slots):
            g = g + s_ref[j].astype(F32)
        delta, nm, nv = _adamw_math(w_ref[...], g, m_ref[...], v_ref[...])
        g_ref[...] = g
        d_ref[...] = delta
        nm_ref[...] = nm
        nv_ref[...] = nv

    blk = pl.BlockSpec((tr, c), lambda i: (i, 0))
    return pl.pallas_call(
        body, name=name, grid=(r // tr,),
        in_specs=[pl.BlockSpec((n_slots, tr, c), lambda i: (0, i, 0)), blk, blk, blk],
        out_specs=[blk] * 4, out_shape=[jax.ShapeDtypeStruct((r, c), F32)] * 4,
    )(slots, w, m, v)


def _adamw_small(g, w, m, v, name):
    def body(g_ref, w_ref, m_ref, v_ref, d_ref, nm_ref, nv_ref):
        delta, nm, nv = _adamw_math(w_ref[...], g_ref[...], m_ref[...], v_ref[...])
        d_ref[...] = delta
        nm_ref[...] = nm
        nv_ref[...] = nv

    spec = _full(g.shape)
    return pl.pallas_call(body, name=name, grid=(1,), in_specs=[spec] * 4, out_specs=[spec] * 3,
                          out_shape=[jax.ShapeDtypeStruct(g.shape, F32)] * 3)(g, w, m, v)


def _mod_part(c_all, w_ada_l, b_ada_l):
    n = w_ada_l.shape[1]

    def body(c_ref, w_ref, b_ref, o_ref):
        o_ref[...] = jnp.dot(c_ref[...], w_ref[...], preferred_element_type=F32,
                             precision=lax.Precision.HIGHEST) + b_ref[...]

    return pl.pallas_call(body, name="mod_part", grid=(1,),
                          in_specs=[_full(c_all.shape), _full(w_ada_l.shape), _full(b_ada_l.shape)],
                          out_specs=_full((N_DEV, n)), out_shape=jax.ShapeDtypeStruct((N_DEV, n), F32))(c_all, w_ada_l, b_ada_l)


def _w_ada_update(c_all_t, dmod_cols, w, m, v):
    def body(c_ref, dm_ref, w_ref, m_ref, v_ref, g_ref, d_ref, nm_ref, nv_ref):
        g = c_ref[:, 0:1] * dm_ref[0:1, :]
        for b in range(1, N_DEV):
            g = g + c_ref[:, b:b + 1] * dm_ref[b:b + 1, :]
        delta, nm, nv = _adamw_math(w_ref[...], g, m_ref[...], v_ref[...])
        g_ref[...] = g
        d_ref[...] = delta
        nm_ref[...] = nm
        nv_ref[...] = nv

    spec = _full(w.shape)
    return pl.pallas_call(body, name="w_ada_update", grid=(1,),
                          in_specs=[_full(c_all_t.shape), _full(dmod_cols.shape), spec, spec, spec],
                          out_specs=[spec] * 4, out_shape=[jax.ShapeDtypeStruct(w.shape, F32)] * 4)(c_all_t, dmod_cols, w, m, v)


def _cast_bf16(w, name):
    def body(w_ref, o_ref):
        o_ref[...] = w_ref[...].astype(BF16)

    return pl.pallas_call(body, name=name, grid=(1,), in_specs=[_full(w.shape)], out_specs=_full(w.shape),
                          out_shape=jax.ShapeDtypeStruct(w.shape, BF16))(w)


def _prenorm(x, mod, norm_g, tm):
    t = x.shape[0]

    def body(x_ref, mod_ref, g_ref, h_ref, ht_ref):
        xv = x_ref[...]
        r = lax.rsqrt(jnp.mean(xv * xv, axis=-1, keepdims=True) + EPS)
        h = (xv * r) * g_ref[...] * (1.0 + mod_ref[:, D:2 * D]) + mod_ref[:, 0:D]
        h_ref[...] = h.astype(BF16)
        ht_ref[...] = h.T.astype(BF16)

    return pl.pallas_call(body, name="prenorm", grid=(t // tm,),
                          in_specs=[_rows(tm, D), _full((1, 3 * D)), _full((1, D))],
                          out_specs=[_rows(tm, D), pl.BlockSpec((D, tm), lambda i: (0, i))],
                          out_shape=[jax.ShapeDtypeStruct((t, D), BF16), jax.ShapeDtypeStruct((D, t), BF16)])(x, mod, norm_g)


def _rope_apply(t, cos, s_lo, s_hi):
    return t * cos + pltpu.roll(t, 120, 1) * s_lo + pltpu.roll(t, 8, 1) * s_hi


def _shift_copies(sh, buf, c0):
    rows = buf.shape[0] - 8
    for s in range(1, 8):
        sh[s, 0:rows, :] = buf[s:s + rows, pl.ds(c0, 128)]


def _window64(buf, sh, c0, start):
    s = start % 8
    if s == 0:
        return buf[start:start + 64, pl.ds(c0, 128)]
    return sh[s, start - s:start - s + 64, :]


def _conv_taps(acc_init, w_ref, buf, sh, row0, c0, offset_of_tap):
    acc = acc_init
    for j in range(CONV_K):
        acc = acc + w_ref[j:j + 1, pl.ds(c0, 128)] * _window64(buf, sh, c0, row0 + offset_of_tap(j))
    return acc


def _conv_fwd(proj, conv_w, conv_b, ln_g, ln_b, tm):
    t = proj.shape[0]
    hb = tm // HALO

    def body(a_ref, b_ref, z_ref, ah_ref, bh_ref, w_ref, cb_ref, lg_ref, lb_ref, u1_ref, pc_ref, ubuf, sh):
        i = pl.program_id(0)
        u0h = ah_ref[...] * _sig(bh_ref[...])
        ubuf[0:HALO, :] = jnp.where(i > 0, u0h, 0.0)
        ubuf[HALO:HALO + tm, :] = a_ref[...] * _sig(b_ref[...])

        def col(ci, carry):
            c0 = pl.multiple_of(ci * 128, 128)
            _shift_copies(sh, ubuf, c0)
            for rc in range(tm // 64):
                init = jnp.zeros((64, 128), F32)
                acc = _conv_taps(init, w_ref, ubuf, sh, rc * 64, c0, lambda j: HALO - (CONV_K - 1) + j)
                u1_ref[rc * 64:(rc + 1) * 64, pl.ds(c0, 128)] = acc + cb_ref[:, pl.ds(c0, 128)]
            return carry

        lax.fori_loop(0, D // 128, col, 0)
        u1 = u1_ref[...]
        mu = jnp.mean(u1, axis=-1, keepdims=True)
        xc = u1 - mu
        var = jnp.mean(xc * xc, axis=-1, keepdims=True)
        u2 = xc * lax.rsqrt(var + EPS) * lg_ref[...] + lb_ref[...]
        z = z_ref[...]
        pc_ref[...] = (u2 * _sig(u2) * (z * _sig(z))).astype(BF16)

    halo = pl.BlockSpec((HALO, D), lambda i: (jnp.maximum(i * hb - 1, 0), 0))
    halo_b = pl.BlockSpec((HALO, D), lambda i: (jnp.maximum(i * hb - 1, 0), 1))
    return pl.pallas_call(
        body, name="conv_fwd", grid=(t // tm,),
        in_specs=[_rows(tm, D, 0), _rows(tm, D, 1), _rows(tm, D, 2), halo, halo_b,
                  _full((CONV_KP, D)), _full((1, D)), _full((1, D)), _full((1, D))],
        out_specs=[_rows(tm, D), _rows(tm, D)],
        out_shape=[jax.ShapeDtypeStruct((t, D), F32), jax.ShapeDtypeStruct((t, D), BF16)],
        scratch_shapes=[pltpu.VMEM((HALO + tm, D), F32), pltpu.VMEM((8, HALO + tm, 128), F32)],
    )(proj, proj, proj, proj, proj, conv_w, conv_b, ln_g, ln_b)


def _band_masks_t(has_prev):
    key = lax.broadcasted_iota(jnp.int32, (BLK, BLK), 0)
    qry = lax.broadcasted_iota(jnp.int32, (BLK, BLK), 1)
    return jnp.logical_and(key >= qry, has_prev), key <= qry


def _head_lanes(pair, hh):
    lane = lax.broadcasted_iota(jnp.int32, pair.shape, 1)
    return jnp.where((lane >= hh * HEAD) & (lane < (hh + 1) * HEAD), pair, jnp.zeros_like(pair))


def _pair_mask(has_prev):
    mask_p, mask_c = _band_masks_t(has_prev)
    both = jnp.concatenate([mask_p, mask_c], axis=0)
    return jnp.concatenate([both, both], axis=1)


def _query_pair(pair):
    return jnp.concatenate([_head_lanes(pair, 0), _head_lanes(pair, 1)], axis=0)


def _key_pair(ref, prev, cur):
    return jnp.concatenate([ref[pl.ds(prev, BLK), :], ref[pl.ds(cur, BLK), :]], axis=0)


def _own_head(both):
    return jnp.concatenate([both[0:HEAD, 0:BLK], both[HEAD:2 * HEAD, BLK:2 * BLK]], axis=0)


def _store_transposed(dst, base, src):
    for j in range(TILE // BLK):
        dst[base // BLK + j] = src[j * BLK:(j + 1) * BLK, :].T.astype(BF16)


class _Dilated:
    def __init__(self, dil):
        self.dil = dil
        self.per = TILE // dil
        self.nbr = self.per // BLK

    def spread(self, dst, base, src_ref, dtype):
        for r in range(self.dil):
            rows = src_ref[pl.ds(r, self.per, stride=self.dil), :] if self.dil > 1 else src_ref[...]
            dst[pl.ds(pl.multiple_of(base + r * self.per, BLK), self.per), :] = rows.astype(dtype)

    def gather(self, dst_ref, src, base):
        for r in range(self.dil):
            rows = src[pl.ds(pl.multiple_of(base + r * self.per, BLK), self.per), :]
            if self.dil > 1:
                dst_ref[pl.ds(r, self.per, stride=self.dil), :] = rows
            else:
                dst_ref[...] = rows

    def block_rows(self, b, i, cur, prv):
        n = b % self.nbr
        row = pl.multiple_of(b * BLK, BLK)
        has_prev = jnp.logical_or(n > 0, i > 0)
        prev = jnp.where(n > 0, cur + row - BLK, jnp.where(i > 0, prv + row + (self.nbr - 1) * BLK, cur + row))
        return row, pl.multiple_of(prev, BLK), has_prev


def _slots(i):
    return pl.multiple_of((i % 2) * TILE, TILE), pl.multiple_of(((i + 1) % 2) * TILE, TILE)


def _nt(a, b):
    return lax.dot_general(a, b, (((1,), (1,)), ((), ())), preferred_element_type=F32)


def _qkv_specs(gi, clamp_to=None):
    def spec(col0):
        def imap(hp, i):
            return (i if clamp_to is None else jnp.minimum(i, clamp_to), (col0 + gi * ATT) // 128 + hp)
        return pl.BlockSpec((TILE, 128), imap)
    return [spec(C_Q), spec(C_K), spec(C_V)]


def _att_fwd(proj, tables, gi, dil):
    t = proj.shape[0]
    dl = _Dilated(dil)

    def body(q_ref, k_ref, v_ref, c_ref, lo_ref, hi_ref, o_ref, lse_ref, tmp, qd, kd, vt, od, ld):
        i = pl.program_id(1)
        cur, prv = _slots(i)
        cs, lo, hi = c_ref[...], lo_ref[...], hi_ref[...]
        tmp[...] = _rope_apply(q_ref[...], cs, lo, hi) * SM_SCALE
        dl.spread(qd, 0, tmp, BF16)
        tmp[...] = _rope_apply(k_ref[...], cs, lo, hi)
        dl.spread(kd, cur, tmp, BF16)
        dl.spread(tmp, 0, v_ref, F32)
        _store_transposed(vt, cur, tmp)

        def block(b, carry):
            row, prev, has_prev = dl.block_rows(b, i, cur, prv)
            s = jnp.where(_pair_mask(has_prev), _nt(_key_pair(kd, prev, cur + row), _query_pair(qd[pl.ds(row, BLK), :])), NEG_INF)
            mx = jnp.max(s, axis=0, keepdims=True)
            p = jnp.exp(s - mx)
            den = jnp.sum(p, axis=0, keepdims=True)
            v_t = jnp.concatenate([vt[prev // BLK], vt[(cur + row) // BLK]], axis=1)
            acc = jnp.dot(v_t, p.astype(BF16), preferred_element_type=F32) / den
            lse = mx + jnp.log(den)
            od[pl.ds(row, BLK), :] = _own_head(acc).T
            ld[pl.ds(row, BLK), :] = _own_head(jnp.broadcast_to(lse, (2 * HEAD, 2 * BLK))).T
            return carry

        lax.fori_loop(0, TILE // BLK, block, 0, unroll=True)
        dl.gather(o_ref, od, 0)
        dl.gather(lse_ref, ld, 0)

    tab = pl.BlockSpec((TILE, 128), lambda hp, i: (i, 0))
    out_spec = pl.BlockSpec((TILE, 128), lambda hp, i: (i, hp))
    return pl.pallas_call(
        body, name=f"att_fwd_g{gi}", grid=(ATT // 128, t // TILE),
        in_specs=_qkv_specs(gi) + [tab] * 3,
        out_specs=[out_spec] * 2, out_shape=[jax.ShapeDtypeStruct((t, ATT), F32)] * 2,
        scratch_shapes=[pltpu.VMEM((TILE, 128), F32), pltpu.VMEM((TILE, 128), BF16), pltpu.VMEM((2 * TILE, 128), BF16),
                        pltpu.VMEM((2 * TILE // BLK, 128, BLK), BF16), pltpu.VMEM((TILE, 128), F32), pltpu.VMEM((TILE, 128), F32)],
    )(proj, proj, proj, *tables)


def _att_combine(parts, proj, tm):
    t = proj.shape[0]

    def body(o0, l0, o1, l1, o2, l2, z_ref, att_ref, lse_ref, pa_ref):
        m_all = jnp.maximum(jnp.maximum(l0[...], l1[...]), l2[...])
        w0, w1, w2 = jnp.exp(l0[...] - m_all), jnp.exp(l1[...] - m_all), jnp.exp(l2[...] - m_all)
        den = w0 + w1 + w2
        att = (w0 * o0[...] + w1 * o1[...] + w2 * o2[...]) / den
        z = z_ref[...]
        att_ref[...] = att
        lse_ref[...] = m_all + jnp.log(den)
        pa_ref[...] = (att * (z * _sig(z))).astype(BF16)

    spec = _rows(tm, ATT)
    return pl.pallas_call(
        body, name="att_combine", grid=(t // tm,),
        in_specs=[spec] * 6 + [_rows(tm, ATT, C_ZA // ATT)],
        out_specs=[spec] * 3,
        out_shape=[jax.ShapeDtypeStruct((t, ATT), F32)] * 2 + [jax.ShapeDtypeStruct((t, ATT), BF16)],
    )(*parts, proj)


def _att_bwd(proj, tables, datt, dsum, lse, gi, dil):
    t = proj.shape[0]
    nt = t // TILE
    dl = _Dilated(dil)

    def body(q_ref, k_ref, v_ref, c_ref, lo_ref, hi_ref, cl_ref, lol_ref, hil_ref, do_ref, ds_ref, lse_ref,
             dq_ref, dk_ref, dv_ref, tmp, qd, kd, vd, dod, dsd, lsd, dqd, dkd, dvd, kt):
        i = pl.program_id(1)
        cur, prv = _slots(i)

        @pl.when(i < nt)
        def _():
            cs, lo, hi = c_ref[...], lo_ref[...], hi_ref[...]
            tmp[...] = _rope_apply(q_ref[...], cs, lo, hi) * SM_SCALE
            dl.spread(qd, 0, tmp, BF16)
            tmp[...] = _rope_apply(k_ref[...], cs, lo, hi)
            dl.spread(kd, cur, tmp, BF16)
            dl.spread(dqd, 0, tmp, F32)
            _store_transposed(kt, cur, dqd)
            dl.spread(vd, cur, v_ref, BF16)
            dl.spread(dod, 0, do_ref, BF16)
            dl.spread(dsd, 0, ds_ref, F32)
            dl.spread(lsd, 0, lse_ref, F32)
            dkd[pl.ds(cur, TILE), :] = jnp.zeros((TILE, 128), F32)
            dvd[pl.ds(cur, TILE), :] = jnp.zeros((TILE, 128), F32)

            def block(b, carry):
                row, prev, has_prev = dl.block_rows(b, i, cur, prv)
                q_pair, do_pair = _query_pair(qd[pl.ds(row, BLK), :]), _query_pair(dod[pl.ds(row, BLK), :])
                k_pair, v_pair = _key_pair(kd, prev, cur + row), _key_pair(vd, prev, cur + row)
                ds_t, ls_t = dsd[pl.ds(row, BLK), :].T, lsd[pl.ds(row, BLK), :].T
                lse = jnp.concatenate([ls_t[0:1, :], ls_t[HEAD:HEAD + 1, :]], axis=1)
                dsm = jnp.concatenate([ds_t[0:1, :], ds_t[HEAD:HEAD + 1, :]], axis=1)
                p = jnp.exp(jnp.where(_pair_mask(has_prev), _nt(k_pair, q_pair), NEG_INF) - lse)
                ds = (p * (_nt(v_pair, do_pair) - dsm)).astype(BF16)
                k_t = jnp.concatenate([kt[prev // BLK], kt[(cur + row) // BLK]], axis=1)
                dqd[pl.ds(row, BLK), :] = _own_head(jnp.dot(k_t, ds, preferred_element_type=F32)).T * SM_SCALE
                dk = jnp.dot(ds, q_pair, preferred_element_type=F32)
                dv = jnp.dot(p.astype(BF16), do_pair, preferred_element_type=F32)
                dkd[pl.ds(cur + row, BLK), :] += dk[BLK:2 * BLK, :]
                dvd[pl.ds(cur + row, BLK), :] += dv[BLK:2 * BLK, :]
                dkd[pl.ds(prev, BLK), :] += dk[0:BLK, :]
                dvd[pl.ds(prev, BLK), :] += dv[0:BLK, :]
                return carry

            lax.fori_loop(0, TILE // BLK, block, 0, unroll=True)
            dl.gather(tmp, dqd, 0)
            dq_ref[...] = _rope_apply(tmp[...], cs, -lo, -hi).astype(BF16)

        @pl.when(i > 0)
        def _():
            dl.gather(tmp, dkd, prv)
            dk_ref[...] = _rope_apply(tmp[...], cl_ref[...], -lol_ref[...], -hil_ref[...]).astype(BF16)
            dl.gather(tmp, dvd, prv)
            dv_ref[...] = tmp[...].astype(BF16)

    now = lambda col: pl.BlockSpec((TILE, 128), lambda hp, i: (jnp.minimum(i, nt - 1), col(hp)))
    lag = lambda col: pl.BlockSpec((TILE, 128), lambda hp, i: (jnp.maximum(i - 1, 0), col(hp)))
    first, pair = (lambda hp: 0), (lambda hp: hp)
    return pl.pallas_call(
        body, name=f"att_bwd_g{gi}", grid=(ATT // 128, nt + 1),
        in_specs=_qkv_specs(gi, nt - 1) + [now(first)] * 3 + [lag(first)] * 3 + [now(pair)] * 3,
        out_specs=[now(pair), lag(pair), lag(pair)],
        out_shape=[jax.ShapeDtypeStruct((t, ATT), BF16)] * 3,
        scratch_shapes=[pltpu.VMEM((TILE, 128), F32), pltpu.VMEM((TILE, 128), BF16), pltpu.VMEM((2 * TILE, 128), BF16),
                        pltpu.VMEM((2 * TILE, 128), BF16), pltpu.VMEM((TILE, 128), BF16), pltpu.VMEM((TILE, 128), F32),
                        pltpu.VMEM((TILE, 128), F32), pltpu.VMEM((TILE, 128), F32), pltpu.VMEM((2 * TILE, 128), F32),
                        pltpu.VMEM((2 * TILE, 128), F32), pltpu.VMEM((2 * TILE // BLK, 128, BLK), BF16)],
    )(proj, proj, proj, *tables, *tables, datt, dsum, lse)


def _merge_head(pc, pa, proj, x, mod, final_g, target, w_co, w_ao, w_o, u1, ln_g, ln_b, att, tm):
    t = x.shape[0]

    def body(pc_ref, pa_ref, gc_ref, ga_ref, x_ref, mod_ref, fg_ref, tg_ref, wco_ref, wao_ref, wo_ref,
             zc_ref, u1_ref, lg_ref, lb_ref, att_ref, za_ref,
             merged_ref, do_ref, dyc_ref, dya_ref, dout_ref, du1_ref, dzc_ref, datt_ref, ds_ref, dp_ref,
             sq_ref, gfg_ref, dgate_ref, dlg_ref, dlb_ref, dcb_ref):
        i = pl.program_id(0)
        yc = jnp.dot(pc_ref[...], wco_ref[...], preferred_element_type=F32)
        ya = jnp.dot(pa_ref[...], wao_ref[...], preferred_element_type=F32)
        sc, sa = _sig(gc_ref[...]), _sig(ga_ref[...])
        merged = (sc * yc + sa * ya).astype(BF16)
        merged_ref[...] = merged
        ov = jnp.dot(merged, wo_ref[...], preferred_element_type=F32)
        gate = mod_ref[:, 2 * D:3 * D]
        out = x_ref[...] + gate * ov
        r = lax.rsqrt(jnp.mean(out * out, axis=-1, keepdims=True) + EPS)
        yn = out * r
        diff = yn * fg_ref[...] - tg_ref[...]
        dy = diff * (1.0 / D)
        gy = dy * fg_ref[...]
        dout = r * (gy - yn * jnp.mean(gy * yn, axis=-1, keepdims=True))
        dout_ref[...] = dout
        do = (dout * gate).astype(BF16)
        do_ref[...] = do
        _acc_rows(sq_ref, i, diff * diff)
        _acc_rows(gfg_ref, i, dy * yn)
        _acc_rows(dgate_ref, i, dout * ov)
        dm = _nt(do, wo_ref[...])
        dyc = (dm * sc).astype(BF16)
        dya = (dm * sa).astype(BF16)
        dyc_ref[...] = dyc
        dya_ref[...] = dya
        dp_ref[:, ATT:ATT + D] = (dm * yc * sc * (1.0 - sc)).astype(BF16)
        dp_ref[:, ATT + D:ATT + 2 * D] = (dm * ya * sa * (1.0 - sa)).astype(BF16)

        dpc = _nt(dyc, wco_ref[...])
        u1v = u1_ref[...]
        xc = u1v - jnp.mean(u1v, axis=-1, keepdims=True)
        rs = lax.rsqrt(jnp.mean(xc * xc, axis=-1, keepdims=True) + EPS)
        uhat = xc * rs
        u2 = uhat * lg_ref[...] + lb_ref[...]
        s2 = _sig(u2)
        zc = zc_ref[...]
        szc = _sig(zc)
        dzc_ref[...] = (dpc * (u2 * s2) * _dsilu(zc, szc)).astype(BF16)
        du2 = dpc * (zc * szc) * _dsilu(u2, s2)
        duhat = du2 * lg_ref[...]
        du1 = rs * (duhat - jnp.mean(duhat, axis=-1, keepdims=True) - uhat * jnp.mean(duhat * uhat, axis=-1, keepdims=True))
        du1_ref[...] = du1
        _acc_rows(dlg_ref, i, du2 * uhat)
        _acc_rows(dlb_ref, i, du2)
        _acc_rows(dcb_ref, i, du1)

        dpa = _nt(dya, wao_ref[...])
        za, att_v = za_ref[...], att_ref[...]
        sza = _sig(za)
        datt = dpa * (za * sza)
        datt_ref[...] = datt
        dp_ref[:, 0:ATT] = (dpa * att_v * _dsilu(za, sza)).astype(BF16)
        prod = datt * att_v
        for hd in range(ATT // HEAD):
            sl = slice(hd * HEAD, (hd + 1) * HEAD)
            ds_ref[:, sl] = jnp.broadcast_to(jnp.sum(prod[:, sl], axis=-1, keepdims=True), (tm, HEAD))

    vec = _full((1, D))
    bf = lambda w: jax.ShapeDtypeStruct((t, w), BF16)
    f32 = lambda w: jax.ShapeDtypeStruct((t, w), F32)
    tail = ATT + 2 * D
    return pl.pallas_call(
        body, name="merge_head", grid=(t // tm,),
        in_specs=[_rows(tm, D), _rows(tm, ATT), _rows(tm, D, C_GC // D), _rows(tm, D, C_GA // D), _rows(tm, D),
                  _full((1, 3 * D)), vec, _rows(tm, D), _full((D, D)), _full((ATT, D)), _full((D, D)),
                  _rows(tm, D, C_ZC // D), _rows(tm, D), vec, vec, _rows(tm, ATT), _rows(tm, ATT, C_ZA // ATT)],
        out_specs=[_rows(tm, D)] * 7 + [_rows(tm, ATT), _rows(tm, ATT), _rows(tm, tail, C_ZA // tail)] + [vec] * 6,
        out_shape=[bf(D), bf(D), bf(D), bf(D), f32(D), f32(D), bf(D), f32(ATT), f32(ATT), bf(N_COL)]
        + [jax.ShapeDtypeStruct((1, D), F32)] * 6,
    )(pc, pa, proj, proj, x, mod, final_g, target, w_co, w_ao, w_o, proj, u1, ln_g, ln_b, att, proj)


def _acc_rows(ref, i, val):
    @pl.when(i == 0)
    def _():
        ref[...] = jnp.zeros_like(ref)

    ref[...] += jnp.sum(val, axis=0, keepdims=True)


def _conv_bwd_taps(du1, proj, conv_w, dzc, dqkv, dproj, tm):
    t = proj.shape[0]
    hb = tm // HALO
    last = t // HALO - 1

    def body(du_ref, duh_ref, a_ref, b_ref, ah_ref, bh_ref, w_ref, dzc_ref, *rest):
        qkv_refs, (dp_in, dp_ref, dw_ref, dbuf, ubuf, g0, shd, shu) = rest[:9], rest[9:]
        del dp_in
        dp_ref[:, C_ZC:C_ZC + D] = dzc_ref[...]
        for n, ref in enumerate(qkv_refs):
            dp_ref[:, C_Q + n * ATT:C_Q + (n + 1) * ATT] = ref[...]
        i = pl.program_id(0)
        a, sb = a_ref[...], _sig(b_ref[...])
        ubuf[0:HALO, :] = jnp.where(i > 0, ah_ref[...] * _sig(bh_ref[...]), 0.0)
        ubuf[HALO:HALO + tm, :] = a * sb
        dbuf[0:tm, :] = du_ref[...]
        dbuf[tm:tm + HALO, :] = jnp.where(i < pl.num_programs(0) - 1, duh_ref[...], 0.0)

        @pl.when(i == 0)
        def _():
            dw_ref[...] = jnp.zeros_like(dw_ref)

        def col(ci, carry):
            c0 = pl.multiple_of(ci * 128, 128)
            _shift_copies(shd, dbuf, c0)
            _shift_copies(shu, ubuf, c0)
            for rc in range(tm // 64):
                g0[rc * 64:(rc + 1) * 64, pl.ds(c0, 128)] = _conv_taps(
                    jnp.zeros((64, 128), F32), w_ref, dbuf, shd, rc * 64, c0, lambda j: CONV_K - 1 - j)
            for j in range(CONV_K):
                part = jnp.zeros((8, 128), F32)
                for rc in range(tm // 64):
                    off = rc * 64 + HALO - (CONV_K - 1) + j
                    prod = dbuf[rc * 64:(rc + 1) * 64, pl.ds(c0, 128)] * _window64(ubuf, shu, c0, off)
                    part = part + jnp.sum(prod.reshape(8, 8, 128), axis=0)
                dw_ref[j:j + 1, pl.ds(c0, 128)] += jnp.sum(part, axis=0, keepdims=True)
            return carry

        lax.fori_loop(0, D // 128, col, 0)
        du0 = g0[...]
        dp_ref[:, 0:D] = (du0 * sb).astype(BF16)
        dp_ref[:, D:2 * D] = (du0 * a * sb * (1.0 - sb)).astype(BF16)

    prev = lambda col: pl.BlockSpec((HALO, D), lambda i: (jnp.maximum(i * hb - 1, 0), col))
    nxt = pl.BlockSpec((HALO, D), lambda i: (jnp.minimum((i + 1) * hb, last), 0))
    return pl.pallas_call(
        body, name="conv_bwd_taps", grid=(t // tm,),
        in_specs=[_rows(tm, D), nxt, _rows(tm, D, 0), _rows(tm, D, 1), prev(0), prev(1), _full((CONV_KP, D)),
                  _rows(tm, D)] + [_rows(tm, ATT)] * 9 + [ANY],
        out_specs=[_rows(tm, C_ZA, 0), _full((CONV_KP, D))],
        out_shape=[jax.ShapeDtypeStruct((t, N_COL), BF16), jax.ShapeDtypeStruct((CONV_KP, D), F32)],
        scratch_shapes=[pltpu.VMEM((tm + HALO, D), F32), pltpu.VMEM((HALO + tm, D), F32), pltpu.VMEM((tm, D), F32),
                        pltpu.VMEM((8, HALO + tm, 128), F32), pltpu.VMEM((8, HALO + tm, 128), F32)],
        input_output_aliases={17: 0},
    )(du1, du1, proj, proj, proj, proj, conv_w, dzc, *dqkv, dproj)


def _dh_prenorm_bwd(dproj, w_in_blocks, x, dout, mod, norm_g, tm):
    t = x.shape[0]
    nk, _, tk = w_in_blocks.shape

    def body(dp_ref, w_ref, x_ref, dout_ref, mod_ref, g_ref, gx_ref, dshift_ref, dscale_ref, dg_ref, acc):
        i, kk = pl.program_id(0), pl.program_id(1)
        p = _nt(dp_ref[...], w_ref[...])

        @pl.when(kk == 0)
        def _():
            acc[...] = p

        @pl.when(kk > 0)
        def _():
            acc[...] += p

        @pl.when(kk == nk - 1)
        def _():
            xv, dhv = x_ref[...], acc[...]
            r = lax.rsqrt(jnp.mean(xv * xv, axis=-1, keepdims=True) + EPS)
            xn = xv * r
            one_scale = 1.0 + mod_ref[:, D:2 * D]
            dxn = dhv * (g_ref[...] * one_scale)
            gx_ref[...] = r * (dxn - xn * jnp.mean(dxn * xn, axis=-1, keepdims=True)) + dout_ref[...]
            _acc_rows(dshift_ref, i, dhv)
            _acc_rows(dscale_ref, i, dhv * xn * g_ref[...])
            _acc_rows(dg_ref, i, dhv * xn * one_scale)

    row = pl.BlockSpec((tm, D), lambda i, kk: (i, 0))
    vec = pl.BlockSpec((1, D), lambda i, kk: (0, 0))
    return pl.pallas_call(
        body, name="dh_prenorm_bwd", grid=(t // tm, nk),
        in_specs=[pl.BlockSpec((tm, tk), lambda i, kk: (i, kk)), pl.BlockSpec((None, D, tk), lambda i, kk: (kk, 0, 0)),
                  row, row, pl.BlockSpec((1, 3 * D), lambda i, kk: (0, 0)), vec],
        out_specs=[row, vec, vec, vec],
        out_shape=[jax.ShapeDtypeStruct((t, D), F32)] + [jax.ShapeDtypeStruct((1, D), F32)] * 3,
        scratch_shapes=[pltpu.VMEM((tm, D), F32)],
    )(dproj, w_in_blocks, x, dout, mod, norm_g)


def _sum_devices(gathered):
    w = gathered.shape[-1]

    def body(g_ref, o_ref):
        acc = g_ref[0]
        for j in range(1, N_DEV):
            acc = acc + g_ref[j]
        o_ref[...] = acc

    return pl.pallas_call(body, name="sum_devices", grid=(1,), in_specs=[_full(gathered.shape)], out_specs=_full((1, w)),
                          out_shape=jax.ShapeDtypeStruct((1, w), F32))(gathered)


def _rope_tables(positions):
    half = HEAD // 8
    t = positions.shape[-1]
    inv_freq = ROPE_THETA ** (-(jnp.arange(half, dtype=F32) * 2.0 / (2 * half)))
    ang = positions.reshape(t, 1).astype(F32) * inv_freq
    cos, sin = jnp.cos(ang), jnp.sin(ang)
    zeros = lambda n: jnp.zeros((t, n), F32)
    c64 = jnp.concatenate([cos, cos, jnp.ones((t, HEAD - 2 * half), F32)], axis=1)
    lo64 = jnp.concatenate([-sin, zeros(HEAD - half)], axis=1)
    hi64 = jnp.concatenate([zeros(half), sin, zeros(HEAD - 2 * half)], axis=1)
    return tuple(jnp.tile(a, (1, 2)) for a in (c64, lo64, hi64))


def kernel(x, c, positions, norm_g, w_ada, b_ada, w_in, conv_w, conv_b, conv_ln_g, conv_ln_b, w_conv_out, w_att_out, w_o, final_g, loss_target, m_norm_g, m_w_ada, m_b_ada, m_w_in, m_conv_w, m_conv_b, m_conv_ln_g, m_conv_ln_b, m_w_conv_out, m_w_att_out, m_w_o, m_final_g, v_norm_g, v_w_ada, v_b_ada, v_w_in, v_conv_w, v_conv_b, v_conv_ln_g, v_conv_ln_b, v_w_conv_out, v_w_att_out, v_w_o, v_final_g):
    me = 4 * lax.axis_index("x") + 2 * lax.axis_index("y") + lax.axis_index("c")
    x2, tgt = x[0], loss_target[0]
    t = x2.shape[0]
    te = 512 if t % 512 == 0 else 256
    tcv = 256
    tmm = 1024 if t % 1024 == 0 else 256
    n_ada = w_ada.shape[-1]

    pad_taps = lambda a: jnp.pad(a[0], ((0, CONV_KP - CONV_K), (0, 0)))
    shards = (_cast_bf16(w_in[0], "cast_w_in"), _cast_bf16(w_conv_out[0], "cast_w_conv_out"),
              _cast_bf16(w_att_out[0], "cast_w_att_out"), _cast_bf16(w_o[0], "cast_w_o"), pad_taps(conv_w))
    block_of = lambda relations: jnp.bitwise_xor(me, jnp.array(relations, jnp.int32))

    c_all = _allgather_small(c, "gather_c").reshape(N_DEV, D)
    b_ada_l = lax.dynamic_slice(b_ada, (0, me * n_ada), (1, n_ada))
    parts = _allgather_small(_mod_part(c_all, w_ada[0], b_ada_l), "gather_mod")
    mod = lax.dynamic_slice(parts, (0, me, 0), (N_DEV, 1, n_ada)).reshape(1, N_DEV * n_ada)

    h, ht = _prenorm(x2, mod, norm_g, te)
    proj, w_in_f, w_co_f, w_ao_f, w_o_f, conv_w_f = _proj_gather(h, shards, block_of(GATHER_ORDER), tmm)
    u1, pc = _conv_fwd(proj, conv_w_f, conv_b, conv_ln_g, conv_ln_b, tcv)
    tables = _rope_tables(positions)
    parts_att = []
    for gi, dil in GROUPS:
        parts_att += _att_fwd(proj, tables, gi, dil)
    att, lse, pa = _att_combine(parts_att, proj, te)

    (merged, do, dyc, dya, dout, du1, dzc, datt, dsum, dproj,
     sq_sum, g_final, d_gate, d_ln_g, d_ln_b, d_conv_b) = _merge_head(
        pc, pa, proj, x2, mod, final_g.reshape(1, D), tgt, w_co_f, w_ao_f, w_o_f, u1, conv_ln_g, conv_ln_b, att, tcv)

    dw_o = _matmul(merged, do, ta=True, out_dtype=BF16, tm=D, tn=D, tk=512, name="dw_o")
    dw_co = _matmul(pc, dyc, ta=True, out_dtype=BF16, tm=D, tn=D, tk=512, name="dw_conv_out")
    dw_ao = _matmul(pa, dya, ta=True, out_dtype=BF16, tm=ATT, tn=D, tk=512, name="dw_att_out")
    dqs, dks, dvs = [], [], []
    for gi, dil in GROUPS:
        dq, dk, dv = _att_bwd(proj, tables, datt, dsum, lse, gi, dil)
        dqs.append(dq), dks.append(dk), dvs.append(dv)
    dproj, dconv_w = _conv_bwd_taps(du1, proj, conv_w_f, dzc, dqs + dks + dvs, dproj, tcv)
    grad_x, d_shift, d_scale, d_norm_g = _dh_prenorm_bwd(dproj, w_in_f, x2, dout, mod, norm_g, tmm)

    packed = jnp.concatenate([d_shift, d_scale, d_gate, d_norm_g, d_conv_b, d_ln_g, d_ln_b, g_final, sq_sum], axis=1)
    gathered = _allgather_small(packed, "gather_partials")
    total = _sum_devices(gathered)
    seg = lambda k, n=1: total[:, k * D:(k + n) * D]
    g_b_ada, g_norm_g, g_conv_b, g_ln_g, g_ln_b, g_final_g = seg(0, 3), seg(3), seg(4), seg(5), seg(6), seg(7)
    loss = (0.5 / D) * jnp.sum(seg(8))
    dmod_all = gathered[:, 0, 0:3 * D]
    dmod_cols = lax.dynamic_slice(dmod_all, (0, me * n_ada), (N_DEV, n_ada))
    g_w_ada, d_w_ada, nm_w_ada, nv_w_ada = _w_ada_update(c_all.T, dmod_cols, w_ada[0], m_w_ada[0], v_w_ada[0])

    small = {}
    for name, g, w, m, v in (("norm_g", g_norm_g, norm_g, m_norm_g, v_norm_g), ("b_ada", g_b_ada, b_ada, m_b_ada, v_b_ada),
                             ("conv_b", g_conv_b, conv_b, m_conv_b, v_conv_b), ("conv_ln_g", g_ln_g, conv_ln_g, m_conv_ln_g, v_conv_ln_g),
                             ("conv_ln_b", g_ln_b, conv_ln_b, m_conv_ln_b, v_conv_ln_b),
                             ("final_g", g_final_g, final_g.reshape(1, D), m_final_g.reshape(1, D), v_final_g.reshape(1, D))):
        small[name] = (g,) + tuple(_adamw_small(g, w, m, v, "adamw_" + name))

    slots = _dw_in_scatter(ht, dproj, (dw_co, dw_ao, dw_o, dconv_w), block_of(SCATTER_ORDER), 1024)
    big = {
        "w_in": _sum_adamw(slots[0], w_in[0], m_w_in[0], v_w_in[0], 256, "adamw_w_in"),
        "w_conv_out": _sum_adamw(slots[1], w_conv_out[0], m_w_conv_out[0], v_w_conv_out[0], 128, "adamw_w_conv_out"),
        "w_att_out": _sum_adamw(slots[2], w_att_out[0], m_w_att_out[0], v_w_att_out[0], 512, "adamw_w_att_out"),
        "w_o": _sum_adamw(slots[3], w_o[0], m_w_o[0], v_w_o[0], 128, "adamw_w_o"),
        "conv_w": [r[:CONV_K] for r in _sum_adamw(slots[4], pad_taps(conv_w), pad_taps(m_conv_w), pad_taps(v_conv_w), CONV_KP, "adamw_conv_w")],
    }
    big["w_ada"] = (g_w_ada, d_w_ada, nm_w_ada, nv_w_ada)

    order = ("norm_g", "w_ada", "b_ada", "w_in", "conv_w", "conv_b", "conv_ln_g", "conv_ln_b", "w_conv_out", "w_att_out", "w_o", "final_g")
    lead = lambda name, a: a.reshape(D) if name == "final_g" else (a[None] if name in big else a)
    result = {**small, **big}
    outs = [loss, grad_x[None]]
    for field in range(4):
        outs += [lead(name, result[name][field]) for name in order]
    return tuple(outs)
```

```python
import functools

import jax
import jax.numpy as jnp
from jax import lax
from jax.experimental import pallas as pl
from jax.experimental.pallas import tpu as pltpu

F32 = jnp.float32
BF16 = jnp.bfloat16

N_DEV = 8
D = 1024
N_COL = 10240
C_A, C_B, C_ZC, C_Q, C_K, C_V, C_ZA, C_GC, C_GA = 0, 1024, 2048, 3072, 4608, 6144, 7680, 8192, 9216
QKV = 1536
ATT = 512
HEAD = 64
BLK = 128
TILE = 2048
GROUPS = ((0, 1), (1, 4), (2, 16))
CONV_K = 31
CONV_KP = 32
HALO = 32
EPS = 1e-6
NEG_INF = -1e30
ROPE_THETA = 500000.0
SM_SCALE = HEAD ** -0.5

ADAM_LR, ADAM_B1, ADAM_B2, ADAM_EPS, ADAM_WD, ADAM_STEP = 0.001, 0.9, 0.999, 1e-08, 0.01, 10

MESH = pl.DeviceIdType.MESH
ANY = pl.BlockSpec(memory_space=pl.ANY)


def _sig(v):
    return 1.0 / (1.0 + jnp.exp(-v))


def _dsilu(v, s):
    return s * (1.0 + v * (1.0 - s))


def _full(shape):
    return pl.BlockSpec(shape, lambda *_: (0,) * len(shape))


def _rows(tm, width, col=0):
    return pl.BlockSpec((tm, width), lambda i: (i, col))


def _matmul(a, b, *, ta=False, tb=False, out_dtype=F32, tm, tn, tk, name):
    m, k = (a.shape[1], a.shape[0]) if ta else a.shape
    n = b.shape[0] if tb else b.shape[1]
    assert (b.shape[1] if tb else b.shape[0]) == k
    assert m % tm == 0 and n % tn == 0 and k % tk == 0
    nk = k // tk
    dims = (((0 if ta else 1,), (1 if tb else 0,)), ((), ()))
    use_scratch = out_dtype != F32 and nk > 1

    def body(a_ref, b_ref, o_ref, *scratch):
        p = lax.dot_general(a_ref[...], b_ref[...], dims, preferred_element_type=F32)
        if nk == 1:
            o_ref[...] = p.astype(out_dtype)
            return
        acc = scratch[0] if use_scratch else o_ref
        kk = pl.program_id(2)

        @pl.when(kk == 0)
        def _():
            acc[...] = p

        @pl.when(kk > 0)
        def _():
            acc[...] += p

        if use_scratch:
            @pl.when(kk == nk - 1)
            def _():
                o_ref[...] = acc[...].astype(out_dtype)

    a_spec = pl.BlockSpec((tk, tm), lambda i, j, kk: (kk, i)) if ta else pl.BlockSpec((tm, tk), lambda i, j, kk: (i, kk))
    b_spec = pl.BlockSpec((tn, tk), lambda i, j, kk: (j, kk)) if tb else pl.BlockSpec((tk, tn), lambda i, j, kk: (kk, j))
    return pl.pallas_call(
        body, name=name, grid=(m // tm, n // tn, nk),
        in_specs=[a_spec, b_spec],
        out_specs=pl.BlockSpec((tm, tn), lambda i, j, kk: (i, j)),
        out_shape=jax.ShapeDtypeStruct((m, n), out_dtype),
        scratch_shapes=[pltpu.VMEM((tm, tn), F32)] if use_scratch else [],
    )(a, b)


def _me_and_peers():
    x, y, c = lax.axis_index("x"), lax.axis_index("y"), lax.axis_index("c")
    me = 4 * x + 2 * y + c
    peers = []
    for k in range(1, N_DEV):
        px, py, pc = x ^ (k >> 2), y ^ ((k >> 1) & 1), c ^ (k & 1)
        peers.append(((px, py, pc), 4 * px + 2 * py + pc))
    return me, peers


def _allgather_small(v, name):
    r, c = v.shape

    def body(v_ref, out_ref, send_sems, recv_sems):
        me, peers = _me_and_peers()
        out_ref[me] = v_ref[...]
        copies = []
        for k, (dev, _) in enumerate(peers):
            cp = pltpu.make_async_remote_copy(src_ref=v_ref, dst_ref=out_ref.at[me], send_sem=send_sems.at[k],
                                              recv_sem=recv_sems.at[k], device_id=dev, device_id_type=MESH)
            cp.start()
            copies.append(cp)
        for k, (dev, idx) in enumerate(peers):
            pltpu.make_async_remote_copy(src_ref=v_ref, dst_ref=out_ref.at[idx], send_sem=send_sems.at[k],
                                         recv_sem=recv_sems.at[k], device_id=dev, device_id_type=MESH).wait_recv()
        for cp in copies:
            cp.wait_send()

    return pl.pallas_call(
        body, name=name,
        in_specs=[pl.BlockSpec(memory_space=pltpu.VMEM)],
        out_specs=pl.BlockSpec(memory_space=pltpu.VMEM),
        out_shape=jax.ShapeDtypeStruct((N_DEV, r, c), v.dtype),
        scratch_shapes=[pltpu.SemaphoreType.DMA((N_DEV - 1,)), pltpu.SemaphoreType.DMA((N_DEV - 1,))],
    )(v)


def _window(ref, kind, idx, size):
    if kind == "block":
        return ref.at[idx]
    start = pl.multiple_of(idx * size, size)
    if kind == "rows":
        return ref.at[pl.ds(start, size), :]
    return ref.at[:, pl.ds(start, size)]


_BIG = (("cols", N_COL // N_DEV), ("rows", D // N_DEV), ("cols", D // N_DEV), ("rows", D // N_DEV), ("cols", D // N_DEV))
_GATHERED = (("block", 1),) + _BIG[1:]


GATHER_ORDER = (0, 1, 2, 4, 3, 5, 6, 7)
W_IN_DIRECT = (1, 2, 4, 6)
SCATTER_ORDER = (7, 6, 5, 4, 3, 2, 1, 0)
W_IN_SLOT = {0: 0, 1: 1, 2: 2, 4: 3, 6: 4}


def _proj_gather(h, shards, order, tm):
    t = h.shape[0]
    nt = len(shards)
    n_blk = N_COL // N_DEV
    full_shapes = []
    for s, (kind, size) in zip(shards, _GATHERED):
        full_shapes.append(jax.ShapeDtypeStruct({"block": (N_DEV,) + s.shape, "rows": (s.shape[0] * N_DEV, s.shape[1]),
                                                 "cols": (s.shape[0], s.shape[1] * N_DEV)}[kind], s.dtype))
    last = (N_DEV - 1, t // tm - 1)

    def body(order_ref, h_ref, *refs):
        src, proj_ref, dst = refs[:nt], refs[nt], refs[nt + 1:2 * nt + 1]
        w_all, send_sems, recv_sems, local_sems, keep_sems = refs[2 * nt + 1:]
        j, i = pl.program_id(0), pl.program_id(1)
        me, peers = _me_and_peers()

        def landing(tn, idx):
            kind, size = _GATHERED[tn]
            return w_all.at[idx] if tn == 0 else _window(dst[tn], kind, idx, size)

        def local(tn):
            return pltpu.make_async_copy(src[tn], landing(tn, me), local_sems.at[tn])

        def remote(tn, k, block_of):
            dev, idx = peers[k - 1]
            return pltpu.make_async_remote_copy(src_ref=src[tn], dst_ref=landing(tn, me if block_of == "mine" else idx),
                                                send_sem=send_sems.at[tn, k - 1], recv_sem=recv_sems.at[tn, k - 1],
                                                device_id=dev, device_id_type=MESH)

        def forward(k):
            block = w_all.at[peers[k - 1][1]]
            return pltpu.make_async_remote_copy(src_ref=block, dst_ref=block, send_sem=send_sems.at[0, k], recv_sem=recv_sems.at[0, k],
                                                device_id=peers[0][0], device_id_type=MESH)

        def keep(step):
            blk = order_ref[step]
            return pltpu.make_async_copy(w_all.at[blk], dst[0].at[blk], keep_sems.at[step])

        @pl.when((j == 0) & (i == 0))
        def _():
            for tn in range(nt):
                local(tn).start()
                for k in GATHER_ORDER[1:]:
                    if tn > 0 or k in W_IN_DIRECT:
                        remote(tn, k, "mine").start()

        @pl.when(i == 0)
        def _():
            for step, k in enumerate(GATHER_ORDER):
                @pl.when(j == step)
                def _():
                    if k == 0:
                        local(0).wait()
                    else:
                        remote(0, k, "theirs").wait_recv()
                        if k in W_IN_DIRECT and k > 1:
                            forward(k).start()
                    keep(step).start()

        proj_ref[...] = jnp.dot(h_ref[...], w_all[order_ref[j]], preferred_element_type=F32)

        @pl.when((j == last[0]) & (i == last[1]))
        def _():
            for step in range(N_DEV):
                keep(step).wait()
            for tn in range(1, nt):
                local(tn).wait()
                for k in range(1, N_DEV):
                    remote(tn, k, "theirs").wait_recv()
            for tn in range(nt):
                for k in range(1, N_DEV):
                    if tn > 0 or k in W_IN_DIRECT:
                        remote(tn, k, "mine").wait_send()
                    else:
                        forward(k - 1).wait_send()

    grid_spec = pltpu.PrefetchScalarGridSpec(
        num_scalar_prefetch=1, grid=(N_DEV, t // tm),
        in_specs=[pl.BlockSpec((tm, D), lambda j, i, order_ref: (i, 0))] + [ANY] * nt,
        out_specs=[pl.BlockSpec((tm, n_blk), lambda j, i, order_ref: (i, order_ref[j]))] + [ANY] * nt,
        scratch_shapes=[pltpu.VMEM((N_DEV, D, n_blk), BF16), pltpu.SemaphoreType.DMA((nt, N_DEV - 1)),
                        pltpu.SemaphoreType.DMA((nt, N_DEV - 1)), pltpu.SemaphoreType.DMA((nt,)), pltpu.SemaphoreType.DMA((N_DEV,))],
    )
    return pl.pallas_call(
        body, name="proj_gather", grid_spec=grid_spec,
        out_shape=[jax.ShapeDtypeStruct((t, N_COL), F32)] + full_shapes,
    )(order, h, *shards)


def _dw_in_scatter(ht, dproj, small_grads, order, tk):
    t = ht.shape[1]
    nt = 1 + len(small_grads)
    n_blk = N_COL // N_DEV
    nk = t // tk
    slot_shapes = [jax.ShapeDtypeStruct((len(W_IN_SLOT), D, n_blk), BF16)]
    for g, (kind, size) in zip(small_grads, _BIG[1:]):
        slot_shapes.append(jax.ShapeDtypeStruct((N_DEV,) + ((size, g.shape[1]) if kind == "rows" else (g.shape[0], size)), g.dtype))

    def body(order_ref, h_ref, dp_ref, *refs):
        src, dst = refs[:nt - 1], refs[nt - 1:2 * nt - 1]
        acc, stage, partner, send_sems, recv_sems, local_sems, pair_send, pair_recv = refs[2 * nt - 1:]
        j, kk = pl.program_id(0), pl.program_id(1)
        me, peers = _me_and_peers()

        def small_local(tn):
            kind, size = _BIG[tn]
            return pltpu.make_async_copy(_window(src[tn - 1], kind, me, size), dst[tn].at[me], local_sems.at[tn])

        def small_remote(tn, k, mine):
            kind, size = _BIG[tn]
            dev, idx = peers[k - 1]
            return pltpu.make_async_remote_copy(src_ref=_window(src[tn - 1], kind, idx if mine else me, size),
                                                dst_ref=dst[tn].at[me if mine else idx],
                                                send_sem=send_sems.at[tn, k - 1], recv_sem=recv_sems.at[tn, k - 1],
                                                device_id=dev, device_id_type=MESH)

        def push(step):
            k, slot = SCATTER_ORDER[step], step % 2
            if k == 0:
                return pltpu.make_async_copy(stage.at[slot], dst[0].at[W_IN_SLOT[0]], local_sems.at[0])
            if k not in W_IN_SLOT:
                p = (k - 3) // 2
                return pltpu.make_async_remote_copy(src_ref=stage.at[slot], dst_ref=partner.at[p], send_sem=pair_send.at[p],
                                                    recv_sem=pair_recv.at[p], device_id=peers[0][0], device_id_type=MESH)
            return pltpu.make_async_remote_copy(src_ref=stage.at[slot], dst_ref=dst[0].at[W_IN_SLOT[k]],
                                                send_sem=send_sems.at[0, k - 1], recv_sem=recv_sems.at[0, k - 1],
                                                device_id=peers[k - 1][0], device_id_type=MESH)

        @pl.when((j == 0) & (kk == 0))
        def _():
            for tn in range(1, nt):
                small_local(tn).start()
                for k in range(1, N_DEV):
                    small_remote(tn, k, True).start()

        p = jnp.dot(h_ref[...], dp_ref[...], preferred_element_type=F32)

        @pl.when(kk == 0)
        def _():
            acc[...] = p

        @pl.when(kk > 0)
        def _():
            acc[...] += p

        @pl.when(kk == nk - 1)
        def _():
            for step, k in enumerate(SCATTER_ORDER):
                @pl.when(j == step)
                def _():
                    if step >= 2:
                        push(step - 2).wait_send()
                    total = acc[...]
                    if k in W_IN_SLOT and k >= 2:
                        p = k // 2 - 1
                        push(SCATTER_ORDER.index(k + 1)).wait_recv()
                        total = total + partner[p].astype(F32)
                    stage[step % 2] = total.astype(BF16)
                    push(step).start()

        @pl.when((j == N_DEV - 1) & (kk == nk - 1))
        def _():
            push(N_DEV - 2).wait_send()
            push(N_DEV - 1).wait()
            for k in (1, 2, 4, 6):
                push(SCATTER_ORDER.index(k)).wait_recv()
            for tn in range(1, nt):
                small_local(tn).wait()
                for k in range(1, N_DEV):
                    small_remote(tn, k, False).wait_recv()
                    small_remote(tn, k, True).wait_send()

    grid_spec = pltpu.PrefetchScalarGridSpec(
        num_scalar_prefetch=1, grid=(N_DEV, nk),
        in_specs=[pl.BlockSpec((D, tk), lambda j, kk, order_ref: (0, kk)),
                  pl.BlockSpec((tk, n_blk), lambda j, kk, order_ref: (kk, order_ref[j]))] + [ANY] * (nt - 1),
        out_specs=[ANY] * nt,
        scratch_shapes=[pltpu.VMEM((D, n_blk), F32), pltpu.VMEM((2, D, n_blk), BF16), pltpu.VMEM((3, D, n_blk), BF16),
                        pltpu.SemaphoreType.DMA((nt, N_DEV - 1)), pltpu.SemaphoreType.DMA((nt, N_DEV - 1)),
                        pltpu.SemaphoreType.DMA((nt,)), pltpu.SemaphoreType.DMA((3,)), pltpu.SemaphoreType.DMA((3,))],
    )
    return pl.pallas_call(body, name="dw_in_scatter", grid_spec=grid_spec, out_shape=slot_shapes)(order, ht, dproj, *small_grads)


def _adamw_math(w, g, m, v):
    m = ADAM_B1 * m + (1.0 - ADAM_B1) * g
    v = ADAM_B2 * v + (1.0 - ADAM_B2) * (g * g)
    m_hat = m / (1.0 - ADAM_B1 ** ADAM_STEP)
    v_hat = v / (1.0 - ADAM_B2 ** ADAM_STEP)
    delta = -ADAM_LR * (m_hat / (jnp.sqrt(v_hat) + ADAM_EPS) + ADAM_WD * w)
    return delta, m, v


def _sum_adamw(slots, w, m, v, tr, name):
    n_slots, r, c = slots.shape
    assert r % tr == 0

    def body(s_ref, w_ref, m_ref, v_ref, g_ref, d_ref, nm_ref, nv_ref):
        g = s_ref[0].astype(F32)
        for j in range(1, n_slots):
            g = g + s_ref[j].astype(F32)
        delta, nm, nv = _adamw_math(w_ref[...], g, m_ref[...], v_ref[...])
        g_ref[...] = g
        d_ref[...] = delta
        nm_ref[...] = nm
        nv_ref[...] = nv

    blk = pl.BlockSpec((tr, c), lambda i: (i, 0))
    return pl.pallas_call(
        body, name=name, grid=(r // tr,),
        in_specs=[pl.BlockSpec((n_slots, tr, c), lambda i: (0, i, 0)), blk, blk, blk],
        out_specs=[blk] * 4, out_shape=[jax.ShapeDtypeStruct((r, c), F32)] * 4,
    )(slots, w, m, v)


def _adamw_small(g, w, m, v, name):
    def body(g_ref, w_ref, m_ref, v_ref, d_ref, nm_ref, nv_ref):
        delta, nm, nv = _adamw_math(w_ref[...], g_ref[...], m_ref[...], v_ref[...])
        d_ref[...] = delta
        nm_ref[...] = nm
        nv_ref[...] = nv

    spec = _full(g.shape)
    return pl.pallas_call(body, name=name, grid=(1,), in_specs=[spec] * 4, out_specs=[spec] * 3,
                          out_shape=[jax.ShapeDtypeStruct(g.shape, F32)] * 3)(g, w, m, v)


def _mod_part(c_all, w_ada_l, b_ada_l):
    n = w_ada_l.shape[1]

    def body(c_ref, w_ref, b_ref, o_ref):
        o_ref[...] = jnp.dot(c_ref[...], w_ref[...], preferred_element_type=F32,
                             precision=lax.Precision.HIGHEST) + b_ref[...]

    return pl.pallas_call(body, name="mod_part", grid=(1,),
                          in_specs=[_full(c_all.shape), _full(w_ada_l.shape), _full(b_ada_l.shape)],
                          out_specs=_full((N_DEV, n)), out_shape=jax.ShapeDtypeStruct((N_DEV, n), F32))(c_all, w_ada_l, b_ada_l)


def _w_ada_update(c_all_t, dmod_cols, w, m, v):
    def body(c_ref, dm_ref, w_ref, m_ref, v_ref, g_ref, d_ref, nm_ref, nv_ref):
        g = c_ref[:, 0:1] * dm_ref[0:1, :]
        for b in range(1, N_DEV):
            g = g + c_ref[:, b:b + 1] * dm_ref[b:b + 1, :]
        delta, nm, nv = _adamw_math(w_ref[...], g, m_ref[...], v_ref[...])
        g_ref[...] = g
        d_ref[...] = delta
        nm_ref[...] = nm
        nv_ref[...] = nv

    spec = _full(w.shape)
    return pl.pallas_call(body, name="w_ada_update", grid=(1,),
                          in_specs=[_full(c_all_t.shape), _full(dmod_cols.shape), spec, spec, spec],
                          out_specs=[spec] * 4, out_shape=[jax.ShapeDtypeStruct(w.shape, F32)] * 4)(c_all_t, dmod_cols, w, m, v)


def _cast_bf16(w, name):
    def body(w_ref, o_ref):
        o_ref[...] = w_ref[...].astype(BF16)

    return pl.pallas_call(body, name=name, grid=(1,), in_specs=[_full(w.shape)], out_specs=_full(w.shape),
                          out_shape=jax.ShapeDtypeStruct(w.shape, BF16))(w)


def _prenorm(x, mod, norm_g, tm):
    t = x.shape[0]

    def body(x_ref, mod_ref, g_ref, h_ref, ht_ref):
        xv = x_ref[...]
        r = lax.rsqrt(jnp.mean(xv * xv, axis=-1, keepdims=True) + EPS)
        h = (xv * r) * g_ref[...] * (1.0 + mod_ref[:, D:2 * D]) + mod_ref[:, 0:D]
        h_ref[...] = h.astype(BF16)
        ht_ref[...] = h.T.astype(BF16)

    return pl.pallas_call(body, name="prenorm", grid=(t // tm,),
                          in_specs=[_rows(tm, D), _full((1, 3 * D)), _full((1, D))],
                          out_specs=[_rows(tm, D), pl.BlockSpec((D, tm), lambda i: (0, i))],
                          out_shape=[jax.ShapeDtypeStruct((t, D), BF16), jax.ShapeDtypeStruct((D, t), BF16)])(x, mod, norm_g)


def _rope_apply(t, cos, s_lo, s_hi):
    return t * cos + pltpu.roll(t, 120, 1) * s_lo + pltpu.roll(t, 8, 1) * s_hi


def _shift_copies(sh, buf, c0):
    rows = buf.shape[0] - 8
    for s in range(1, 8):
        sh[s, 0:rows, :] = buf[s:s + rows, pl.ds(c0, 128)]


def _window64(buf, sh, c0, start):
    s = start % 8
    if s == 0:
        return buf[start:start + 64, pl.ds(c0, 128)]
    return sh[s, start - s:start - s + 64, :]


def _conv_taps(acc_init, w_ref, buf, sh, row0, c0, offset_of_tap):
    acc = acc_init
    for j in range(CONV_K):
        acc = acc + w_ref[j:j + 1, pl.ds(c0, 128)] * _window64(buf, sh, c0, row0 + offset_of_tap(j))
    return acc


def _conv_fwd(proj, conv_w, conv_b, ln_g, ln_b, tm):
    t = proj.shape[0]
    hb = tm // HALO

    def body(a_ref, b_ref, z_ref, ah_ref, bh_ref, w_ref, cb_ref, lg_ref, lb_ref, u1_ref, pc_ref, ubuf, sh):
        i = pl.program_id(0)
        u0h = ah_ref[...] * _sig(bh_ref[...])
        ubuf[0:HALO, :] = jnp.where(i > 0, u0h, 0.0)
        ubuf[HALO:HALO + tm, :] = a_ref[...] * _sig(b_ref[...])

        def col(ci, carry):
            c0 = pl.multiple_of(ci * 128, 128)
            _shift_copies(sh, ubuf, c0)
            for rc in range(tm // 64):
                init = jnp.zeros((64, 128), F32)
                acc = _conv_taps(init, w_ref, ubuf, sh, rc * 64, c0, lambda j: HALO - (CONV_K - 1) + j)
                u1_ref[rc * 64:(rc + 1) * 64, pl.ds(c0, 128)] = acc + cb_ref[:, pl.ds(c0, 128)]
            return carry

        lax.fori_loop(0, D // 128, col, 0)
        u1 = u1_ref[...]
        mu = jnp.mean(u1, axis=-1, keepdims=True)
        xc = u1 - mu
        var = jnp.mean(xc * xc, axis=-1, keepdims=True)
        u2 = xc * lax.rsqrt(var + EPS) * lg_ref[...] + lb_ref[...]
        z = z_ref[...]
        pc_ref[...] = (u2 * _sig(u2) * (z * _sig(z))).astype(BF16)

    halo = pl.BlockSpec((HALO, D), lambda i: (jnp.maximum(i * hb - 1, 0), 0))
    halo_b = pl.BlockSpec((HALO, D), lambda i: (jnp.maximum(i * hb - 1, 0), 1))
    return pl.pallas_call(
        body, name="conv_fwd", grid=(t // tm,),
        in_specs=[_rows(tm, D, 0), _rows(tm, D, 1), _rows(tm, D, 2), halo, halo_b,
                  _full((CONV_KP, D)), _full((1, D)), _full((1, D)), _full((1, D))],
        out_specs=[_rows(tm, D), _rows(tm, D)],
        out_shape=[jax.ShapeDtypeStruct((t, D), F32), jax.ShapeDtypeStruct((t, D), BF16)],
        scratch_shapes=[pltpu.VMEM((HALO + tm, D), F32), pltpu.VMEM((8, HALO + tm, 128), F32)],
    )(proj, proj, proj, proj, proj, conv_w, conv_b, ln_g, ln_b)


def _band_masks_t(has_prev):
    key = lax.broadcasted_iota(jnp.int32, (BLK, BLK), 0)
    qry = lax.broadcasted_iota(jnp.int32, (BLK, BLK), 1)
    return jnp.logical_and(key >= qry, has_prev), key <= qry


def _head_lanes(pair, hh):
    lane = lax.broadcasted_iota(jnp.int32, pair.shape, 1)
    return jnp.where((lane >= hh * HEAD) & (lane < (hh + 1) * HEAD), pair, jnp.zeros_like(pair))


def _pair_mask(has_prev):
    mask_p, mask_c = _band_masks_t(has_prev)
    both = jnp.concatenate([mask_p, mask_c], axis=0)
    return jnp.concatenate([both, both], axis=1)


def _query_pair(pair):
    return jnp.concatenate([_head_lanes(pair, 0), _head_lanes(pair, 1)], axis=0)


def _key_pair(ref, prev, cur):
    return jnp.concatenate([ref[pl.ds(prev, BLK), :], ref[pl.ds(cur, BLK), :]], axis=0)


def _own_head(both):
    return jnp.concatenate([both[0:HEAD, 0:BLK], both[HEAD:2 * HEAD, BLK:2 * BLK]], axis=0)


def _store_transposed(dst, base, src):
    for j in range(TILE // BLK):
        dst[base // BLK + j] = src[j * BLK:(j + 1) * BLK, :].T.astype(BF16)


class _Dilated:
    def __init__(self, dil):
        self.dil = dil
        self.per = TILE // dil
        self.nbr = self.per // BLK

    def spread(self, dst, base, src_ref, dtype):
        for r in range(self.dil):
            rows = src_ref[pl.ds(r, self.per, stride=self.dil), :] if self.dil > 1 else src_ref[...]
            dst[pl.ds(pl.multiple_of(base + r * self.per, BLK), self.per), :] = rows.astype(dtype)

    def gather(self, dst_ref, src, base):
        for r in range(self.dil):
            rows = src[pl.ds(pl.multiple_of(base + r * self.per, BLK), self.per), :]
            if self.dil > 1:
                dst_ref[pl.ds(r, self.per, stride=self.dil), :] = rows
            else:
                dst_ref[...] = rows

    def block_rows(self, b, i, cur, prv):
        n = b % self.nbr
        row = pl.multiple_of(b * BLK, BLK)
        has_prev = jnp.logical_or(n > 0, i > 0)
        prev = jnp.where(n > 0, cur + row - BLK, jnp.where(i > 0, prv + row + (self.nbr - 1) * BLK, cur + row))
        return row, pl.multiple_of(prev, BLK), has_prev


def _slots(i):
    return pl.multiple_of((i % 2) * TILE, TILE), pl.multiple_of(((i + 1) % 2) * TILE, TILE)


def _nt(a, b):
    return lax.dot_general(a, b, (((1,), (1,)), ((), ())), preferred_element_type=F32)


def _qkv_specs(gi, clamp_to=None):
    def spec(col0):
        def imap(hp, i):
            return (i if clamp_to is None else jnp.minimum(i, clamp_to), (col0 + gi * ATT) // 128 + hp)
        return pl.BlockSpec((TILE, 128), imap)
    return [spec(C_Q), spec(C_K), spec(C_V)]


def _att_fwd(proj, tables, gi, dil):
    t = proj.shape[0]
    dl = _Dilated(dil)

    def body(q_ref, k_ref, v_ref, c_ref, lo_ref, hi_ref, o_ref, lse_ref, tmp, qd, kd, vt, od, ld):
        i = pl.program_id(1)
        cur, prv = _slots(i)
        cs, lo, hi = c_ref[...], lo_ref[...], hi_ref[...]
        tmp[...] = _rope_apply(q_ref[...], cs, lo, hi) * SM_SCALE
        dl.spread(qd, 0, tmp, BF16)
        tmp[...] = _rope_apply(k_ref[...], cs, lo, hi)
        dl.spread(kd, cur, tmp, BF16)
        dl.spread(tmp, 0, v_ref, F32)
        _store_transposed(vt, cur, tmp)

        def block(b, carry):
            row, prev, has_prev = dl.block_rows(b, i, cur, prv)
            s = jnp.where(_pair_mask(has_prev), _nt(_key_pair(kd, prev, cur + row), _query_pair(qd[pl.ds(row, BLK), :])), NEG_INF)
            mx = jnp.max(s, axis=0, keepdims=True)
            p = jnp.exp(s - mx)
            den = jnp.sum(p, axis=0, keepdims=True)
            v_t = jnp.concatenate([vt[prev // BLK], vt[(cur + row) // BLK]], axis=1)
            acc = jnp.dot(v_t, p.astype(BF16), preferred_element_type=F32) / den
            lse = mx + jnp.log(den)
            od[pl.ds(row, BLK), :] = _own_head(acc).T
            ld[pl.ds(row, BLK), :] = _own_head(jnp.broadcast_to(lse, (2 * HEAD, 2 * BLK))).T
            return carry

        lax.fori_loop(0, TILE // BLK, block, 0, unroll=True)
        dl.gather(o_ref, od, 0)
        dl.gather(lse_ref, ld, 0)

    tab = pl.BlockSpec((TILE, 128), lambda hp, i: (i, 0))
    out_spec = pl.BlockSpec((TILE, 128), lambda hp, i: (i, hp))
    return pl.pallas_call(
        body, name=f"att_fwd_g{gi}", grid=(ATT // 128, t // TILE),
        in_specs=_qkv_specs(gi) + [tab] * 3,
        out_specs=[out_spec] * 2, out_shape=[jax.ShapeDtypeStruct((t, ATT), F32)] * 2,
        scratch_shapes=[pltpu.VMEM((TILE, 128), F32), pltpu.VMEM((TILE, 128), BF16), pltpu.VMEM((2 * TILE, 128), BF16),
                        pltpu.VMEM((2 * TILE // BLK, 128, BLK), BF16), pltpu.VMEM((TILE, 128), F32), pltpu.VMEM((TILE, 128), F32)],
    )(proj, proj, proj, *tables)


def _att_bwd(proj, tables, datt, dsum, lse, gi, dil):
    t = proj.shape[0]
    nt = t // TILE
    dl = _Dilated(dil)

    def body(q_ref, k_ref, v_ref, c_ref, lo_ref, hi_ref, cl_ref, lol_ref, hil_ref, do_ref, ds_ref, lse_ref,
             dq_ref, dk_ref, dv_ref, tmp, qd, kd, vd, dod, dsd, lsd, dqd, dkd, dvd, kt):
        i = pl.program_id(1)
        cur, prv = _slots(i)

        @pl.when(i < nt)
        def _():
            cs, lo, hi = c_ref[...], lo_ref[...], hi_ref[...]
            tmp[...] = _rope_apply(q_ref[...], cs, lo, hi) * SM_SCALE
            dl.spread(qd, 0, tmp, BF16)
            tmp[...] = _rope_apply(k_ref[...], cs, lo, hi)
            dl.spread(kd, cur, tmp, BF16)
            dl.spread(dqd, 0, tmp, F32)
            _store_transposed(kt, cur, dqd)
            dl.spread(vd, cur, v_ref, BF16)
            dl.spread(dod, 0, do_ref, BF16)
            dl.spread(dsd, 0, ds_ref, F32)
            dl.spread(lsd, 0, lse_ref, F32)
            dkd[pl.ds(cur, TILE), :] = jnp.zeros((TILE, 128), F32)
            dvd[pl.ds(cur, TILE), :] = jnp.zeros((TILE, 128), F32)

            def block(b, carry):
                row, prev, has_prev = dl.block_rows(b, i, cur, prv)
                q_pair, do_pair = _query_pair(qd[pl.ds(row, BLK), :]), _query_pair(dod[pl.ds(row, BLK), :])
                k_pair, v_pair = _key_pair(kd, prev, cur + row), _key_pair(vd, prev, cur + row)
                ds_t, ls_t = dsd[pl.ds(row, BLK), :].T, lsd[pl.ds(row, BLK), :].T
                lse = jnp.concatenate([ls_t[0:1, :], ls_t[HEAD:HEAD + 1, :]], axis=1)
                dsm = jnp.concatenate([ds_t[0:1, :], ds_t[HEAD:HEAD + 1, :]], axis=1)
                p = jnp.exp(jnp.where(_pair_mask(has_prev), _nt(k_pair, q_pair), NEG_INF) - lse)
                ds = (p * (_nt(v_pair, do_pair) - dsm)).astype(BF16)
                k_t = jnp.concatenate([kt[prev // BLK], kt[(cur + row) // BLK]], axis=1)
                dqd[pl.ds(row, BLK), :] = _own_head(jnp.dot(k_t, ds, preferred_element_type=F32)).T * SM_SCALE
                dk = jnp.dot(ds, q_pair, preferred_element_type=F32)
                dv = jnp.dot(p.astype(BF16), do_pair, preferred_element_type=F32)
                dkd[pl.ds(cur + row, BLK), :] += dk[BLK:2 * BLK, :]
                dvd[pl.ds(cur + row, BLK), :] += dv[BLK:2 * BLK, :]
                dkd[pl.ds(prev, BLK), :] += dk[0:BLK, :]
                dvd[pl.ds(prev, BLK), :] += dv[0:BLK, :]
                return carry

            lax.fori_loop(0, TILE // BLK, block, 0, unroll=True)
            dl.gather(tmp, dqd, 0)
            dq_ref[...] = _rope_apply(tmp[...], cs, -lo, -hi).astype(BF16)

        @pl.when(i > 0)
        def _():
            dl.gather(tmp, dkd, prv)
            dk_ref[...] = _rope_apply(tmp[...], cl_ref[...], -lol_ref[...], -hil_ref[...]).astype(BF16)
            dl.gather(tmp, dvd, prv)
            dv_ref[...] = tmp[...].astype(BF16)

    now = lambda col: pl.BlockSpec((TILE, 128), lambda hp, i: (jnp.minimum(i, nt - 1), col(hp)))
    lag = lambda col: pl.BlockSpec((TILE, 128), lambda hp, i: (jnp.maximum(i - 1, 0), col(hp)))
    first, pair = (lambda hp: 0), (lambda hp: hp)
    return pl.pallas_call(
        body, name=f"att_bwd_g{gi}", grid=(ATT // 128, nt + 1),
        in_specs=_qkv_specs(gi, nt - 1) + [now(first)] * 3 + [lag(first)] * 3 + [now(pair)] * 3,
        out_specs=[now(pair), lag(pair), lag(pair)],
        out_shape=[jax.ShapeDtypeStruct((t, ATT), BF16)] * 3,
        scratch_shapes=[pltpu.VMEM((TILE, 128), F32), pltpu.VMEM((TILE, 128), BF16), pltpu.VMEM((2 * TILE, 128), BF16),
                        pltpu.VMEM((2 * TILE, 128), BF16), pltpu.VMEM((TILE, 128), BF16), pltpu.VMEM((TILE, 128), F32),
                        pltpu.VMEM((TILE, 128), F32), pltpu.VMEM((TILE, 128), F32), pltpu.VMEM((2 * TILE, 128), F32),
                        pltpu.VMEM((2 * TILE, 128), F32), pltpu.VMEM((2 * TILE // BLK, 128, BLK), BF16)],
    )(proj, proj, proj, *tables, *tables, datt, dsum, lse)


def _merge_head(pc, att_parts, proj, x, mod, final_g, target, w_co, w_ao, w_o, u1, ln_g, ln_b, tm):
    t = x.shape[0]

    def body(pc_ref, o0, l0, o1, l1, o2, l2, gc_ref, ga_ref, x_ref, mod_ref, fg_ref, tg_ref, wco_ref, wao_ref, wo_ref,
             zc_ref, u1_ref, lg_ref, lb_ref, za_ref,
             merged_ref, do_ref, dyc_ref, dya_ref, dout_ref, du1_ref, dzc_ref, datt_ref, ds_ref, dp_ref, pa_ref, lse_ref,
             sq_ref, gfg_ref, dgate_ref, dlg_ref, dlb_ref, dcb_ref):
        i = pl.program_id(0)
        m_all = jnp.maximum(jnp.maximum(l0[...], l1[...]), l2[...])
        w0, w1, w2 = jnp.exp(l0[...] - m_all), jnp.exp(l1[...] - m_all), jnp.exp(l2[...] - m_all)
        den = w0 + w1 + w2
        att_v = (w0 * o0[...] + w1 * o1[...] + w2 * o2[...]) / den
        lse_ref[...] = m_all + jnp.log(den)
        za = za_ref[...]
        sza = _sig(za)
        pa = (att_v * (za * sza)).astype(BF16)
        pa_ref[...] = pa
        yc = jnp.dot(pc_ref[...], wco_ref[...], preferred_element_type=F32)
        ya = jnp.dot(pa, wao_ref[...], preferred_element_type=F32)
        sc, sa = _sig(gc_ref[...]), _sig(ga_ref[...])
        merged = (sc * yc + sa * ya).astype(BF16)
        merged_ref[...] = merged
        ov = jnp.dot(merged, wo_ref[...], preferred_element_type=F32)
        gate = mod_ref[:, 2 * D:3 * D]
        out = x_ref[...] + gate * ov
        r = lax.rsqrt(jnp.mean(out * out, axis=-1, keepdims=True) + EPS)
        yn = out * r
        diff = yn * fg_ref[...] - tg_ref[...]
        dy = diff * (1.0 / D)
        gy = dy * fg_ref[...]
        dout = r * (gy - yn * jnp.mean(gy * yn, axis=-1, keepdims=True))
        dout_ref[...] = dout
        do = (dout * gate).astype(BF16)
        do_ref[...] = do
        _acc_rows(sq_ref, i, diff * diff)
        _acc_rows(gfg_ref, i, dy * yn)
        _acc_rows(dgate_ref, i, dout * ov)
        dm = _nt(do, wo_ref[...])
        dyc = (dm * sc).astype(BF16)
        dya = (dm * sa).astype(BF16)
        dyc_ref[...] = dyc
        dya_ref[...] = dya
        dp_ref[:, ATT:ATT + D] = (dm * yc * sc * (1.0 - sc)).astype(BF16)
        dp_ref[:, ATT + D:ATT + 2 * D] = (dm * ya * sa * (1.0 - sa)).astype(BF16)

        dpc = _nt(dyc, wco_ref[...])
        u1v = u1_ref[...]
        xc = u1v - jnp.mean(u1v, axis=-1, keepdims=True)
        rs = lax.rsqrt(jnp.mean(xc * xc, axis=-1, keepdims=True) + EPS)
        uhat = xc * rs
        u2 = uhat * lg_ref[...] + lb_ref[...]
        s2 = _sig(u2)
        zc = zc_ref[...]
        szc = _sig(zc)
        dzc_ref[...] = (dpc * (u2 * s2) * _dsilu(zc, szc)).astype(BF16)
        du2 = dpc * (zc * szc) * _dsilu(u2, s2)
        duhat = du2 * lg_ref[...]
        du1 = rs * (duhat - jnp.mean(duhat, axis=-1, keepdims=True) - uhat * jnp.mean(duhat * uhat, axis=-1, keepdims=True))
        du1_ref[...] = du1
        _acc_rows(dlg_ref, i, du2 * uhat)
        _acc_rows(dlb_ref, i, du2)
        _acc_rows(dcb_ref, i, du1)

        dpa = _nt(dya, wao_ref[...])
        datt = dpa * (za * sza)
        datt_ref[...] = datt
        dp_ref[:, 0:ATT] = (dpa * att_v * _dsilu(za, sza)).astype(BF16)
        prod = datt * att_v
        for hd in range(ATT // HEAD):
            sl = slice(hd * HEAD, (hd + 1) * HEAD)
            ds_ref[:, sl] = jnp.broadcast_to(jnp.sum(prod[:, sl], axis=-1, keepdims=True), (tm, HEAD))

    vec = _full((1, D))
    bf = lambda w: jax.ShapeDtypeStruct((t, w), BF16)
    f32 = lambda w: jax.ShapeDtypeStruct((t, w), F32)
    tail = ATT + 2 * D
    return pl.pallas_call(
        body, name="merge_head", grid=(t // tm,),
        in_specs=[_rows(tm, D)] + [_rows(tm, ATT)] * 6 + [_rows(tm, D, C_GC // D), _rows(tm, D, C_GA // D), _rows(tm, D),
                  _full((1, 3 * D)), vec, _rows(tm, D), _full((D, D)), _full((ATT, D)), _full((D, D)),
                  _rows(tm, D, C_ZC // D), _rows(tm, D), vec, vec, _rows(tm, ATT, C_ZA // ATT)],
        out_specs=[_rows(tm, D)] * 7 + [_rows(tm, ATT), _rows(tm, ATT), _rows(tm, tail, C_ZA // tail),
                                        _rows(tm, ATT), _rows(tm, ATT)] + [vec] * 6,
        out_shape=[bf(D), bf(D), bf(D), bf(D), f32(D), f32(D), bf(D), f32(ATT), f32(ATT), bf(N_COL), bf(ATT), f32(ATT)]
        + [jax.ShapeDtypeStruct((1, D), F32)] * 6,
    )(pc, *att_parts, proj, proj, x, mod, final_g, target, w_co, w_ao, w_o, proj, u1, ln_g, ln_b, proj)


def _acc_rows(ref, i, val):
    @pl.when(i == 0)
    def _():
        ref[...] = jnp.zeros_like(ref)

    ref[...] += jnp.sum(val, axis=0, keepdims=True)


def _conv_bwd_taps(du1, proj, conv_w, dzc, dqkv, dproj, tm):
    t = proj.shape[0]
    hb = tm // HALO
    last = t // HALO - 1

    def body(du_ref, duh_ref, a_ref, b_ref, ah_ref, bh_ref, w_ref, dzc_ref, *rest):
        qkv_refs, (dp_in, dp_ref, dw_ref, dbuf, ubuf, g0, shd, shu) = rest[:9], rest[9:]
        del dp_in
        dp_ref[:, C_ZC:C_ZC + D] = dzc_ref[...]
        for n, ref in enumerate(qkv_refs):
            dp_ref[:, C_Q + n * ATT:C_Q + (n + 1) * ATT] = ref[...]
        i = pl.program_id(0)
        a, sb = a_ref[...], _sig(b_ref[...])
        ubuf[0:HALO, :] = jnp.where(i > 0, ah_ref[...] * _sig(bh_ref[...]), 0.0)
        ubuf[HALO:HALO + tm, :] = a * sb
        dbuf[0:tm, :] = du_ref[...]
        dbuf[tm:tm + HALO, :] = jnp.where(i < pl.num_programs(0) - 1, duh_ref[...], 0.0)

        @pl.when(i == 0)
        def _():
            dw_ref[...] = jnp.zeros_like(dw_ref)

        def col(ci, carry):
            c0 = pl.multiple_of(ci * 128, 128)
            _shift_copies(shd, dbuf, c0)
            _shift_copies(shu, ubuf, c0)
            for rc in range(tm // 64):
                g0[rc * 64:(rc + 1) * 64, pl.ds(c0, 128)] = _conv_taps(
                    jnp.zeros((64, 128), F32), w_ref, dbuf, shd, rc * 64, c0, lambda j: CONV_K - 1 - j)
            for j in range(CONV_K):
                part = jnp.zeros((8, 128), F32)
                for rc in range(tm // 64):
                    off = rc * 64 + HALO - (CONV_K - 1) + j
                    prod = dbuf[rc * 64:(rc + 1) * 64, pl.ds(c0, 128)] * _window64(ubuf, shu, c0, off)
                    part = part + jnp.sum(prod.reshape(8, 8, 128), axis=0)
                dw_ref[j:j + 1, pl.ds(c0, 128)] += jnp.sum(part, axis=0, keepdims=True)
            return carry

        lax.fori_loop(0, D // 128, col, 0)
        du0 = g0[...]
        dp_ref[:, 0:D] = (du0 * sb).astype(BF16)
        dp_ref[:, D:2 * D] = (du0 * a * sb * (1.0 - sb)).astype(BF16)

    prev = lambda col: pl.BlockSpec((HALO, D), lambda i: (jnp.maximum(i * hb - 1, 0), col))
    nxt = pl.BlockSpec((HALO, D), lambda i: (jnp.minimum((i + 1) * hb, last), 0))
    return pl.pallas_call(
        body, name="conv_bwd_taps", grid=(t // tm,),
        in_specs=[_rows(tm, D), nxt, _rows(tm, D, 0), _rows(tm, D, 1), prev(0), prev(1), _full((CONV_KP, D)),
                  _rows(tm, D)] + [_rows(tm, ATT)] * 9 + [ANY],
        out_specs=[_rows(tm, C_ZA, 0), _full((CONV_KP, D))],
        out_shape=[jax.ShapeDtypeStruct((t, N_COL), BF16), jax.ShapeDtypeStruct((CONV_KP, D), F32)],
        scratch_shapes=[pltpu.VMEM((tm + HALO, D), F32), pltpu.VMEM((HALO + tm, D), F32), pltpu.VMEM((tm, D), F32),
                        pltpu.VMEM((8, HALO + tm, 128), F32), pltpu.VMEM((8, HALO + tm, 128), F32)],
        input_output_aliases={17: 0},
    )(du1, du1, proj, proj, proj, proj, conv_w, dzc, *dqkv, dproj)


def _dh_prenorm_bwd(dproj, w_in_blocks, x, dout, mod, norm_g, tm):
    t = x.shape[0]
    nk, _, tk = w_in_blocks.shape

    def body(dp_ref, w_ref, x_ref, dout_ref, mod_ref, g_ref, gx_ref, dshift_ref, dscale_ref, dg_ref, acc):
        i, kk = pl.program_id(0), pl.program_id(1)
        p = _nt(dp_ref[...], w_ref[...])

        @pl.when(kk == 0)
        def _():
            acc[...] = p

        @pl.when(kk > 0)
        def _():
            acc[...] += p

        @pl.when(kk == nk - 1)
        def _():
            xv, dhv = x_ref[...], acc[...]
            r = lax.rsqrt(jnp.mean(xv * xv, axis=-1, keepdims=True) + EPS)
            xn = xv * r
            one_scale = 1.0 + mod_ref[:, D:2 * D]
            dxn = dhv * (g_ref[...] * one_scale)
            gx_ref[...] = r * (dxn - xn * jnp.mean(dxn * xn, axis=-1, keepdims=True)) + dout_ref[...]
            _acc_rows(dshift_ref, i, dhv)
            _acc_rows(dscale_ref, i, dhv * xn * g_ref[...])
            _acc_rows(dg_ref, i, dhv * xn * one_scale)

    row = pl.BlockSpec((tm, D), lambda i, kk: (i, 0))
    vec = pl.BlockSpec((1, D), lambda i, kk: (0, 0))
    return pl.pallas_call(
        body, name="dh_prenorm_bwd", grid=(t // tm, nk),
        in_specs=[pl.BlockSpec((tm, tk), lambda i, kk: (i, kk)), pl.BlockSpec((None, D, tk), lambda i, kk: (kk, 0, 0)),
                  row, row, pl.BlockSpec((1, 3 * D), lambda i, kk: (0, 0)), vec],
        out_specs=[row, vec, vec, vec],
        out_shape=[jax.ShapeDtypeStruct((t, D), F32)] + [jax.ShapeDtypeStruct((1, D), F32)] * 3,
        scratch_shapes=[pltpu.VMEM((tm, D), F32)],
    )(dproj, w_in_blocks, x, dout, mod, norm_g)


def _sum_devices(gathered):
    w = gathered.shape[-1]

    def body(g_ref, o_ref):
        acc = g_ref[0]
        for j in range(1, N_DEV):
            acc = acc + g_ref[j]
        o_ref[...] = acc

    return pl.pallas_call(body, name="sum_devices", grid=(1,), in_specs=[_full(gathered.shape)], out_specs=_full((1, w)),
                          out_shape=jax.ShapeDtypeStruct((1, w), F32))(gathered)


def _rope_tables(positions):
    half = HEAD // 8
    t = positions.shape[-1]
    inv_freq = ROPE_THETA ** (-(jnp.arange(half, dtype=F32) * 2.0 / (2 * half)))
    ang = positions.reshape(t, 1).astype(F32) * inv_freq
    cos, sin = jnp.cos(ang), jnp.sin(ang)
    zeros = lambda n: jnp.zeros((t, n), F32)
    c64 = jnp.concatenate([cos, cos, jnp.ones((t, HEAD - 2 * half), F32)], axis=1)
    lo64 = jnp.concatenate([-sin, zeros(HEAD - half)], axis=1)
    hi64 = jnp.concatenate([zeros(half), sin, zeros(HEAD - 2 * half)], axis=1)
    return tuple(jnp.tile(a, (1, 2)) for a in (c64, lo64, hi64))


def kernel(x, c, positions, norm_g, w_ada, b_ada, w_in, conv_w, conv_b, conv_ln_g, conv_ln_b, w_conv_out, w_att_out, w_o, final_g, loss_target, m_norm_g, m_w_ada, m_b_ada, m_w_in, m_conv_w, m_conv_b, m_conv_ln_g, m_conv_ln_b, m_w_conv_out, m_w_att_out, m_w_o, m_final_g, v_norm_g, v_w_ada, v_b_ada, v_w_in, v_conv_w, v_conv_b, v_conv_ln_g, v_conv_ln_b, v_w_conv_out, v_w_att_out, v_w_o, v_final_g):
    me = 4 * lax.axis_index("x") + 2 * lax.axis_index("y") + lax.axis_index("c")
    x2, tgt = x[0], loss_target[0]
    t = x2.shape[0]
    te = 512 if t % 512 == 0 else 256
    tcv = 256
    tmh = 256
    tmm = 1024 if t % 1024 == 0 else 256
    n_ada = w_ada.shape[-1]

    pad_taps = lambda a: jnp.pad(a[0], ((0, CONV_KP - CONV_K), (0, 0)))
    shards = (_cast_bf16(w_in[0], "cast_w_in"), _cast_bf16(w_conv_out[0], "cast_w_conv_out"),
              _cast_bf16(w_att_out[0], "cast_w_att_out"), _cast_bf16(w_o[0], "cast_w_o"), pad_taps(conv_w))
    block_of = lambda relations: jnp.bitwise_xor(me, jnp.array(relations, jnp.int32))

    c_all = _allgather_small(c, "gather_c").reshape(N_DEV, D)
    b_ada_l = lax.dynamic_slice(b_ada, (0, me * n_ada), (1, n_ada))
    parts = _allgather_small(_mod_part(c_all, w_ada[0], b_ada_l), "gather_mod")
    mod = lax.dynamic_slice(parts, (0, me, 0), (N_DEV, 1, n_ada)).reshape(1, N_DEV * n_ada)

    h, ht = _prenorm(x2, mod, norm_g, te)
    proj, w_in_f, w_co_f, w_ao_f, w_o_f, conv_w_f = _proj_gather(h, shards, block_of(GATHER_ORDER), tmm)
    u1, pc = _conv_fwd(proj, conv_w_f, conv_b, conv_ln_g, conv_ln_b, tcv)
    tables = _rope_tables(positions)
    parts_att = []
    for gi, dil in GROUPS:
        parts_att += _att_fwd(proj, tables, gi, dil)

    (merged, do, dyc, dya, dout, du1, dzc, datt, dsum, dproj, pa, lse,
     sq_sum, g_final, d_gate, d_ln_g, d_ln_b, d_conv_b) = _merge_head(
        pc, parts_att, proj, x2, mod, final_g.reshape(1, D), tgt, w_co_f, w_ao_f, w_o_f, u1, conv_ln_g, conv_ln_b, tmh)

    dw_o = _matmul(merged, do, ta=True, out_dtype=BF16, tm=D, tn=D, tk=512, name="dw_o")
    dw_co = _matmul(pc, dyc, ta=True, out_dtype=BF16, tm=D, tn=D, tk=512, name="dw_conv_out")
    dw_ao = _matmul(pa, dya, ta=True, out_dtype=BF16, tm=ATT, tn=D, tk=512, name="dw_att_out")
    dqs, dks, dvs = [], [], []
    for gi, dil in GROUPS:
        dq, dk, dv = _att_bwd(proj, tables, datt, dsum, lse, gi, dil)
        dqs.append(dq), dks.append(dk), dvs.append(dv)
    dproj, dconv_w = _conv_bwd_taps(du1, proj, conv_w_f, dzc, dqs + dks + dvs, dproj, tcv)
    grad_x, d_shift, d_scale, d_norm_g = _dh_prenorm_bwd(dproj, w_in_f, x2, dout, mod, norm_g, tmm)

    packed = jnp.concatenate([d_shift, d_scale, d_gate, d_norm_g, d_conv_b, d_ln_g, d_ln_b, g_final, sq_sum], axis=1)
    gathered = _allgather_small(packed, "gather_partials")
    total = _sum_devices(gathered)
    seg = lambda k, n=1: total[:, k * D:(k + n) * D]
    g_b_ada, g_norm_g, g_conv_b, g_ln_g, g_ln_b, g_final_g = seg(0, 3), seg(3), seg(4), seg(5), seg(6), seg(7)
    loss = (0.5 / D) * jnp.sum(seg(8))
    dmod_all = gathered[:, 0, 0:3 * D]
    dmod_cols = lax.dynamic_slice(dmod_all, (0, me * n_ada), (N_DEV, n_ada))
    g_w_ada, d_w_ada, nm_w_ada, nv_w_ada = _w_ada_update(c_all.T, dmod_cols, w_ada[0], m_w_ada[0], v_w_ada[0])

    small = {}
    for name, g, w, m, v in (("norm_g", g_norm_g, norm_g, m_norm_g, v_norm_g), ("b_ada", g_b_ada, b_ada, m_b_ada, v_b_ada),
                             ("conv_b", g_conv_b, conv_b, m_conv_b, v_conv_b), ("conv_ln_g", g_ln_g, conv_ln_g, m_conv_ln_g, v_conv_ln_g),
                             ("conv_ln_b", g_ln_b, conv_ln_b, m_conv_ln_b, v_conv_ln_b),
                             ("final_g", g_final_g, final_g.reshape(1, D), m_final_g.reshape(1, D), v_final_g.reshape(1, D))):
        small[name] = (g,) + tuple(_adamw_small(g, w, m, v, "adamw_" + name))

    slots = _dw_in_scatter(ht, dproj, (dw_co, dw_ao, dw_o, dconv_w), block_of(SCATTER_ORDER), 1024)
    big = {
        "w_in": _sum_adamw(slots[0], w_in[0], m_w_in[0], v_w_in[0], 256, "adamw_w_in"),
        "w_conv_out": _sum_adamw(slots[1], w_conv_out[0], m_w_conv_out[0], v_w_conv_out[0], 128, "adamw_w_conv_out"),
        "w_att_out": _sum_adamw(slots[2], w_att_out[0], m_w_att_out[0], v_w_att_out[0], 512, "adamw_w_att_out"),
        "w_o": _sum_adamw(slots[3], w_o[0], m_w_o[0], v_w_o[0], 128, "adamw_w_o"),
        "conv_w": [r[:CONV_K] for r in _sum_adamw(slots[4], pad_taps(conv_w), pad_taps(m_conv_w), pad_taps(v_conv_w), CONV_KP, "adamw_conv_w")],
    }
    big["w_ada"] = (g_w_ada, d_w_ada, nm_w_ada, nv_w_ada)

    order = ("norm_g", "w_ada", "b_ada", "w_in", "conv_w", "conv_b", "conv_ln_g", "conv_ln_b", "w_conv_out", "w_att_out", "w_o", "final_g")
    lead = lambda name, a: a.reshape(D) if name == "final_g" else (a[None] if name in big else a)
    result = {**small, **big}
    outs = [loss, grad_x[None]]
    for field in range(4):
        outs += [lead(name, result[name][field]) for name in order]
    return tuple(outs)
```

```python
import functools

import jax
import jax.numpy as jnp
from jax import lax
from jax.experimental import pallas as pl
from jax.experimental.pallas import tpu as pltpu

F32 = jnp.float32
BF16 = jnp.bfloat16

N_DEV = 8
D = 1024
N_COL = 10240
C_A, C_B, C_ZC, C_Q, C_K, C_V, C_ZA, C_GC, C_GA = 0, 1024, 2048, 3072, 4608, 6144, 7680, 8192, 9216
QKV = 1536
ATT = 512
HEAD = 64
BLK = 128
TILE = 2048
GROUPS = ((0, 1), (1, 4), (2, 16))
CONV_K = 31
CONV_KP = 32
HALO = 32
EPS = 1e-6
NEG_INF = -1e30
ROPE_THETA = 500000.0
SM_SCALE = HEAD ** -0.5

ADAM_LR, ADAM_B1, ADAM_B2, ADAM_EPS, ADAM_WD, ADAM_STEP = 0.001, 0.9, 0.999, 1e-08, 0.01, 10

MESH = pl.DeviceIdType.MESH
ANY = pl.BlockSpec(memory_space=pl.ANY)


def _sig(v):
    return 1.0 / (1.0 + jnp.exp(-v))


def _dsilu(v, s):
    return s * (1.0 + v * (1.0 - s))


def _full(shape):
    return pl.BlockSpec(shape, lambda *_: (0,) * len(shape))


def _rows(tm, width, col=0):
    return pl.BlockSpec((tm, width), lambda i: (i, col))


def _matmul(a, b, *, ta=False, tb=False, out_dtype=F32, tm, tn, tk, name):
    m, k = (a.shape[1], a.shape[0]) if ta else a.shape
    n = b.shape[0] if tb else b.shape[1]
    assert (b.shape[1] if tb else b.shape[0]) == k
    assert m % tm == 0 and n % tn == 0 and k % tk == 0
    nk = k // tk
    dims = (((0 if ta else 1,), (1 if tb else 0,)), ((), ()))
    use_scratch = out_dtype != F32 and nk > 1

    def body(a_ref, b_ref, o_ref, *scratch):
        p = lax.dot_general(a_ref[...], b_ref[...], dims, preferred_element_type=F32)
        if nk == 1:
            o_ref[...] = p.astype(out_dtype)
            return
        acc = scratch[0] if use_scratch else o_ref
        kk = pl.program_id(2)

        @pl.when(kk == 0)
        def _():
            acc[...] = p

        @pl.when(kk > 0)
        def _():
            acc[...] += p

        if use_scratch:
            @pl.when(kk == nk - 1)
            def _():
                o_ref[...] = acc[...].astype(out_dtype)

    a_spec = pl.BlockSpec((tk, tm), lambda i, j, kk: (kk, i)) if ta else pl.BlockSpec((tm, tk), lambda i, j, kk: (i, kk))
    b_spec = pl.BlockSpec((tn, tk), lambda i, j, kk: (j, kk)) if tb else pl.BlockSpec((tk, tn), lambda i, j, kk: (kk, j))
    return pl.pallas_call(
        body, name=name, grid=(m // tm, n // tn, nk),
        in_specs=[a_spec, b_spec],
        out_specs=pl.BlockSpec((tm, tn), lambda i, j, kk: (i, j)),
        out_shape=jax.ShapeDtypeStruct((m, n), out_dtype),
        scratch_shapes=[pltpu.VMEM((tm, tn), F32)] if use_scratch else [],
    )(a, b)


def _me_and_peers():
    x, y, c = lax.axis_index("x"), lax.axis_index("y"), lax.axis_index("c")
    me = 4 * x + 2 * y + c
    peers = []
    for k in range(1, N_DEV):
        px, py, pc = x ^ (k >> 2), y ^ ((k >> 1) & 1), c ^ (k & 1)
        peers.append(((px, py, pc), 4 * px + 2 * py + pc))
    return me, peers


def _allgather_small(v, name):
    r, c = v.shape

    def body(v_ref, out_ref, send_sems, recv_sems):
        me, peers = _me_and_peers()
        out_ref[me] = v_ref[...]
        copies = []
        for k, (dev, _) in enumerate(peers):
            cp = pltpu.make_async_remote_copy(src_ref=v_ref, dst_ref=out_ref.at[me], send_sem=send_sems.at[k],
                                              recv_sem=recv_sems.at[k], device_id=dev, device_id_type=MESH)
            cp.start()
            copies.append(cp)
        for k, (dev, idx) in enumerate(peers):
            pltpu.make_async_remote_copy(src_ref=v_ref, dst_ref=out_ref.at[idx], send_sem=send_sems.at[k],
                                         recv_sem=recv_sems.at[k], device_id=dev, device_id_type=MESH).wait_recv()
        for cp in copies:
            cp.wait_send()

    return pl.pallas_call(
        body, name=name,
        in_specs=[pl.BlockSpec(memory_space=pltpu.VMEM)],
        out_specs=pl.BlockSpec(memory_space=pltpu.VMEM),
        out_shape=jax.ShapeDtypeStruct((N_DEV, r, c), v.dtype),
        scratch_shapes=[pltpu.SemaphoreType.DMA((N_DEV - 1,)), pltpu.SemaphoreType.DMA((N_DEV - 1,))],
    )(v)


def _window(ref, kind, idx, size):
    if kind == "block":
        return ref.at[idx]
    start = pl.multiple_of(idx * size, size)
    if kind == "rows":
        return ref.at[pl.ds(start, size), :]
    return ref.at[:, pl.ds(start, size)]


_BIG = (("cols", N_COL // N_DEV), ("rows", D // N_DEV), ("cols", D // N_DEV), ("rows", D // N_DEV), ("cols", D // N_DEV))
_GATHERED = (("block", 1),) + _BIG[1:]


GATHER_ORDER = (0, 1, 2, 4, 3, 5, 6, 7)
W_IN_DIRECT = (1, 2, 4, 6)
SCATTER_ORDER = (7, 6, 5, 4, 3, 2, 1, 0)
W_IN_SLOT = {0: 0, 1: 1, 2: 2, 4: 3, 6: 4}


def _proj_gather(h, shards, order, tm):
    t = h.shape[0]
    nt = len(shards)
    n_blk = N_COL // N_DEV
    full_shapes = []
    for s, (kind, size) in zip(shards, _GATHERED):
        full_shapes.append(jax.ShapeDtypeStruct({"block": (N_DEV,) + s.shape, "rows": (s.shape[0] * N_DEV, s.shape[1]),
                                                 "cols": (s.shape[0], s.shape[1] * N_DEV)}[kind], s.dtype))
    last = (N_DEV - 1, t // tm - 1)

    def body(order_ref, h_ref, *refs):
        src, proj_ref, dst = refs[:nt], refs[nt], refs[nt + 1:2 * nt + 1]
        w_all, send_sems, recv_sems, local_sems, keep_sems = refs[2 * nt + 1:]
        j, i = pl.program_id(0), pl.program_id(1)
        me, peers = _me_and_peers()

        def landing(tn, idx):
            kind, size = _GATHERED[tn]
            return w_all.at[idx] if tn == 0 else _window(dst[tn], kind, idx, size)

        def local(tn):
            return pltpu.make_async_copy(src[tn], landing(tn, me), local_sems.at[tn])

        def remote(tn, k, block_of):
            dev, idx = peers[k - 1]
            return pltpu.make_async_remote_copy(src_ref=src[tn], dst_ref=landing(tn, me if block_of == "mine" else idx),
                                                send_sem=send_sems.at[tn, k - 1], recv_sem=recv_sems.at[tn, k - 1],
                                                device_id=dev, device_id_type=MESH)

        def forward(k):
            block = w_all.at[peers[k - 1][1]]
            return pltpu.make_async_remote_copy(src_ref=block, dst_ref=block, send_sem=send_sems.at[0, k], recv_sem=recv_sems.at[0, k],
                                                device_id=peers[0][0], device_id_type=MESH)

        def keep(step):
            blk = order_ref[step]
            return pltpu.make_async_copy(w_all.at[blk], dst[0].at[blk], keep_sems.at[step])

        @pl.when((j == 0) & (i == 0))
        def _():
            for tn in range(nt):
                local(tn).start()
                for k in GATHER_ORDER[1:]:
                    if tn > 0 or k in W_IN_DIRECT:
                        remote(tn, k, "mine").start()

        @pl.when(i == 0)
        def _():
            for step, k in enumerate(GATHER_ORDER):
                @pl.when(j == step)
                def _():
                    if k == 0:
                        local(0).wait()
                    else:
                        remote(0, k, "theirs").wait_recv()
                        if k in W_IN_DIRECT and k > 1:
                            forward(k).start()
                    keep(step).start()

        proj_ref[...] = jnp.dot(h_ref[...], w_all[order_ref[j]], preferred_element_type=F32)

        @pl.when((j == last[0]) & (i == last[1]))
        def _():
            for step in range(N_DEV):
                keep(step).wait()
            for tn in range(1, nt):
                local(tn).wait()
                for k in range(1, N_DEV):
                    remote(tn, k, "theirs").wait_recv()
            for tn in range(nt):
                for k in range(1, N_DEV):
                    if tn > 0 or k in W_IN_DIRECT:
                        remote(tn, k, "mine").wait_send()
                    else:
                        forward(k - 1).wait_send()

    grid_spec = pltpu.PrefetchScalarGridSpec(
        num_scalar_prefetch=1, grid=(N_DEV, t // tm),
        in_specs=[pl.BlockSpec((tm, D), lambda j, i, order_ref: (i, 0))] + [ANY] * nt,
        out_specs=[pl.BlockSpec((tm, n_blk), lambda j, i, order_ref: (i, order_ref[j]))] + [ANY] * nt,
        scratch_shapes=[pltpu.VMEM((N_DEV, D, n_blk), BF16), pltpu.SemaphoreType.DMA((nt, N_DEV - 1)),
                        pltpu.SemaphoreType.DMA((nt, N_DEV - 1)), pltpu.SemaphoreType.DMA((nt,)), pltpu.SemaphoreType.DMA((N_DEV,))],
    )
    return pl.pallas_call(
        body, name="proj_gather", grid_spec=grid_spec,
        out_shape=[jax.ShapeDtypeStruct((t, N_COL), F32)] + full_shapes,
    )(order, h, *shards)


def _dw_in_scatter(ht, dproj, small_grads, order, tk):
    t = ht.shape[1]
    nt = 1 + len(small_grads)
    n_blk = N_COL // N_DEV
    nk = t // tk
    slot_shapes = [jax.ShapeDtypeStruct((len(W_IN_SLOT), D, n_blk), BF16)]
    for g, (kind, size) in zip(small_grads, _BIG[1:]):
        slot_shapes.append(jax.ShapeDtypeStruct((N_DEV,) + ((size, g.shape[1]) if kind == "rows" else (g.shape[0], size)), g.dtype))

    def body(order_ref, h_ref, dp_ref, *refs):
        src, dst = refs[:nt - 1], refs[nt - 1:2 * nt - 1]
        acc, stage, partner, send_sems, recv_sems, local_sems, pair_send, pair_recv = refs[2 * nt - 1:]
        j, kk = pl.program_id(0), pl.program_id(1)
        me, peers = _me_and_peers()

        def small_local(tn):
            kind, size = _BIG[tn]
            return pltpu.make_async_copy(_window(src[tn - 1], kind, me, size), dst[tn].at[me], local_sems.at[tn])

        def small_remote(tn, k, mine):
            kind, size = _BIG[tn]
            dev, idx = peers[k - 1]
            return pltpu.make_async_remote_copy(src_ref=_window(src[tn - 1], kind, idx if mine else me, size),
                                                dst_ref=dst[tn].at[me if mine else idx],
                                                send_sem=send_sems.at[tn, k - 1], recv_sem=recv_sems.at[tn, k - 1],
                                                device_id=dev, device_id_type=MESH)

        def push(step):
            k, slot = SCATTER_ORDER[step], step % 2
            if k == 0:
                return pltpu.make_async_copy(stage.at[slot], dst[0].at[W_IN_SLOT[0]], local_sems.at[0])
            if k not in W_IN_SLOT:
                p = (k - 3) // 2
                return pltpu.make_async_remote_copy(src_ref=stage.at[slot], dst_ref=partner.at[p], send_sem=pair_send.at[p],
                                                    recv_sem=pair_recv.at[p], device_id=peers[0][0], device_id_type=MESH)
            return pltpu.make_async_remote_copy(src_ref=stage.at[slot], dst_ref=dst[0].at[W_IN_SLOT[k]],
                                                send_sem=send_sems.at[0, k - 1], recv_sem=recv_sems.at[0, k - 1],
                                                device_id=peers[k - 1][0], device_id_type=MESH)

        @pl.when((j == 0) & (kk == 0))
        def _():
            for tn in range(1, nt):
                small_local(tn).start()
                for k in range(1, N_DEV):
                    small_remote(tn, k, True).start()

        p = jnp.dot(h_ref[...], dp_ref[...], preferred_element_type=F32)

        @pl.when(kk == 0)
        def _():
            acc[...] = p

        @pl.when(kk > 0)
        def _():
            acc[...] += p

        @pl.when(kk == nk - 1)
        def _():
            for step, k in enumerate(SCATTER_ORDER):
                @pl.when(j == step)
                def _():
                    if step >= 2:
                        push(step - 2).wait_send()
                    total = acc[...]
                    if k in W_IN_SLOT and k >= 2:
                        p = k // 2 - 1
                        push(SCATTER_ORDER.index(k + 1)).wait_recv()
                        total = total + partner[p].astype(F32)
                    stage[step % 2] = total.astype(BF16)
                    push(step).start()

        @pl.when((j == N_DEV - 1) & (kk == nk - 1))
        def _():
            push(N_DEV - 2).wait_send()
            push(N_DEV - 1).wait()
            for k in (1, 2, 4, 6):
                push(SCATTER_ORDER.index(k)).wait_recv()
            for tn in range(1, nt):
                small_local(tn).wait()
                for k in range(1, N_DEV):
                    small_remote(tn, k, False).wait_recv()
                    small_remote(tn, k, True).wait_send()

    grid_spec = pltpu.PrefetchScalarGridSpec(
        num_scalar_prefetch=1, grid=(N_DEV, nk),
        in_specs=[pl.BlockSpec((D, tk), lambda j, kk, order_ref: (0, kk)),
                  pl.BlockSpec((tk, n_blk), lambda j, kk, order_ref: (kk, order_ref[j]))] + [ANY] * (nt - 1),
        out_specs=[ANY] * nt,
        scratch_shapes=[pltpu.VMEM((D, n_blk), F32), pltpu.VMEM((2, D, n_blk), BF16), pltpu.VMEM((3, D, n_blk), BF16),
                        pltpu.SemaphoreType.DMA((nt, N_DEV - 1)), pltpu.SemaphoreType.DMA((nt, N_DEV - 1)),
                        pltpu.SemaphoreType.DMA((nt,)), pltpu.SemaphoreType.DMA((3,)), pltpu.SemaphoreType.DMA((3,))],
    )
    return pl.pallas_call(body, name="dw_in_scatter", grid_spec=grid_spec, out_shape=slot_shapes)(order, ht, dproj, *small_grads)


def _adamw_math(w, g, m, v):
    m = ADAM_B1 * m + (1.0 - ADAM_B1) * g
    v = ADAM_B2 * v + (1.0 - ADAM_B2) * (g * g)
    m_hat = m / (1.0 - ADAM_B1 ** ADAM_STEP)
    v_hat = v / (1.0 - ADAM_B2 ** ADAM_STEP)
    delta = -ADAM_LR * (m_hat / (jnp.sqrt(v_hat) + ADAM_EPS) + ADAM_WD * w)
    return delta, m, v


def _sum_adamw(slots, w, m, v, tr, name):
    n_slots, r, c = slots.shape
    assert r % tr == 0

    def body(s_ref, w_ref, m_ref, v_ref, g_ref, d_ref, nm_ref, nv_ref):
        g = s_ref[0].astype(F32)
        for j in range(1, n_slots):
            g = g + s_ref[j].astype(F32)
        delta, nm, nv = _adamw_math(w_ref[...], g, m_ref[...], v_ref[...])
        g_ref[...] = g
        d_ref[...] = delta
        nm_ref[...] = nm
        nv_ref[...] = nv

    blk = pl.BlockSpec((tr, c), lambda i: (i, 0))
    return pl.pallas_call(
        body, name=name, grid=(r // tr,),
        in_specs=[pl.BlockSpec((n_slots, tr, c), lambda i: (0, i, 0)), blk, blk, blk],
        out_specs=[blk] * 4, out_shape=[jax.ShapeDtypeStruct((r, c), F32)] * 4,
    )(slots, w, m, v)


def _adamw_small(g, w, m, v, name):
    def body(g_ref, w_ref, m_ref, v_ref, d_ref, nm_ref, nv_ref):
        delta, nm, nv = _adamw_math(w_ref[...], g_ref[...], m_ref[...], v_ref[...])
        d_ref[...] = delta
        nm_ref[...] = nm
        nv_ref[...] = nv

    spec = _full(g.shape)
    return pl.pallas_call(body, name=name, grid=(1,), in_specs=[spec] * 4, out_specs=[spec] * 3,
                          out_shape=[jax.ShapeDtypeStruct(g.shape, F32)] * 3)(g, w, m, v)


def _mod_part(c_all, w_ada_l, b_ada_l):
    n = w_ada_l.shape[1]

    def body(c_ref, w_ref, b_ref, o_ref):
        o_ref[...] = jnp.dot(c_ref[...], w_ref[...], preferred_element_type=F32,
                             precision=lax.Precision.HIGHEST) + b_ref[...]

    return pl.pallas_call(body, name="mod_part", grid=(1,),
                          in_specs=[_full(c_all.shape), _full(w_ada_l.shape), _full(b_ada_l.shape)],
                          out_specs=_full((N_DEV, n)), out_shape=jax.ShapeDtypeStruct((N_DEV, n), F32))(c_all, w_ada_l, b_ada_l)


def _w_ada_update(c_all_t, dmod_cols, w, m, v):
    def body(c_ref, dm_ref, w_ref, m_ref, v_ref, g_ref, d_ref, nm_ref, nv_ref):
        g = c_ref[:, 0:1] * dm_ref[0:1, :]
        for b in range(1, N_DEV):
            g = g + c_ref[:, b:b + 1] * dm_ref[b:b + 1, :]
        delta, nm, nv = _adamw_math(w_ref[...], g, m_ref[...], v_ref[...])
        g_ref[...] = g
        d_ref[...] = delta
        nm_ref[...] = nm
        nv_ref[...] = nv

    spec = _full(w.shape)
    return pl.pallas_call(body, name="w_ada_update", grid=(1,),
                          in_specs=[_full(c_all_t.shape), _full(dmod_cols.shape), spec, spec, spec],
                          out_specs=[spec] * 4, out_shape=[jax.ShapeDtypeStruct(w.shape, F32)] * 4)(c_all_t, dmod_cols, w, m, v)


def _cast_bf16(w, name):
    def body(w_ref, o_ref):
        o_ref[...] = w_ref[...].astype(BF16)

    return pl.pallas_call(body, name=name, grid=(1,), in_specs=[_full(w.shape)], out_specs=_full(w.shape),
                          out_shape=jax.ShapeDtypeStruct(w.shape, BF16))(w)


def _prenorm(x, mod, norm_g, tm):
    t = x.shape[0]

    def body(x_ref, mod_ref, g_ref, h_ref, ht_ref):
        xv = x_ref[...]
        r = lax.rsqrt(jnp.mean(xv * xv, axis=-1, keepdims=True) + EPS)
        h = (xv * r) * g_ref[...] * (1.0 + mod_ref[:, D:2 * D]) + mod_ref[:, 0:D]
        h_ref[...] = h.astype(BF16)
        ht_ref[...] = h.T.astype(BF16)

    return pl.pallas_call(body, name="prenorm", grid=(t // tm,),
                          in_specs=[_rows(tm, D), _full((1, 3 * D)), _full((1, D))],
                          out_specs=[_rows(tm, D), pl.BlockSpec((D, tm), lambda i: (0, i))],
                          out_shape=[jax.ShapeDtypeStruct((t, D), BF16), jax.ShapeDtypeStruct((D, t), BF16)])(x, mod, norm_g)


def _rope_apply(t, cos, s_lo, s_hi):
    return t * cos + pltpu.roll(t, 120, 1) * s_lo + pltpu.roll(t, 8, 1) * s_hi


def _shift_copies(sh, buf, c0):
    rows = buf.shape[0] - 8
    for s in range(1, 8):
        sh[s, 0:rows, :] = buf[s:s + rows, pl.ds(c0, 128)]


def _window64(buf, sh, c0, start):
    s = start % 8
    if s == 0:
        return buf[start:start + 64, pl.ds(c0, 128)]
    return sh[s, start - s:start - s + 64, :]


def _conv_taps(acc_init, w_ref, buf, sh, row0, c0, offset_of_tap):
    acc = acc_init
    for j in range(CONV_K):
        acc = acc + w_ref[j:j + 1, pl.ds(c0, 128)] * _window64(buf, sh, c0, row0 + offset_of_tap(j))
    return acc


def _conv_fwd(proj, conv_w, conv_b, ln_g, ln_b, tm):
    t = proj.shape[0]
    hb = tm // HALO

    def body(a_ref, b_ref, z_ref, ah_ref, bh_ref, w_ref, cb_ref, lg_ref, lb_ref, u1_ref, pc_ref, ubuf, sh):
        i = pl.program_id(0)
        u0h = ah_ref[...] * _sig(bh_ref[...])
        ubuf[0:HALO, :] = jnp.where(i > 0, u0h, 0.0)
        ubuf[HALO:HALO + tm, :] = a_ref[...] * _sig(b_ref[...])

        def col(ci, carry):
            c0 = pl.multiple_of(ci * 128, 128)
            _shift_copies(sh, ubuf, c0)
            for rc in range(tm // 64):
                init = jnp.zeros((64, 128), F32)
                acc = _conv_taps(init, w_ref, ubuf, sh, rc * 64, c0, lambda j: HALO - (CONV_K - 1) + j)
                u1_ref[rc * 64:(rc + 1) * 64, pl.ds(c0, 128)] = acc + cb_ref[:, pl.ds(c0, 128)]
            return carry

        lax.fori_loop(0, D // 128, col, 0)
        u1 = u1_ref[...]
        mu = jnp.mean(u1, axis=-1, keepdims=True)
        xc = u1 - mu
        var = jnp.mean(xc * xc, axis=-1, keepdims=True)
        u2 = xc * lax.rsqrt(var + EPS) * lg_ref[...] + lb_ref[...]
        z = z_ref[...]
        pc_ref[...] = (u2 * _sig(u2) * (z * _sig(z))).astype(BF16)

    halo = pl.BlockSpec((HALO, D), lambda i: (jnp.maximum(i * hb - 1, 0), 0))
    halo_b = pl.BlockSpec((HALO, D), lambda i: (jnp.maximum(i * hb - 1, 0), 1))
    return pl.pallas_call(
        body, name="conv_fwd", grid=(t // tm,),
        in_specs=[_rows(tm, D, 0), _rows(tm, D, 1), _rows(tm, D, 2), halo, halo_b,
                  _full((CONV_KP, D)), _full((1, D)), _full((1, D)), _full((1, D))],
        out_specs=[_rows(tm, D), _rows(tm, D)],
        out_shape=[jax.ShapeDtypeStruct((t, D), F32), jax.ShapeDtypeStruct((t, D), BF16)],
        scratch_shapes=[pltpu.VMEM((HALO + tm, D), F32), pltpu.VMEM((8, HALO + tm, 128), F32)],
    )(proj, proj, proj, proj, proj, conv_w, conv_b, ln_g, ln_b)


def _band_masks_t(has_prev):
    key = lax.broadcasted_iota(jnp.int32, (BLK, BLK), 0)
    qry = lax.broadcasted_iota(jnp.int32, (BLK, BLK), 1)
    return jnp.logical_and(key >= qry, has_prev), key <= qry


def _head_lanes(pair, hh):
    lane = lax.broadcasted_iota(jnp.int32, pair.shape, 1)
    return jnp.where((lane >= hh * HEAD) & (lane < (hh + 1) * HEAD), pair, jnp.zeros_like(pair))


def _pair_mask(has_prev):
    mask_p, mask_c = _band_masks_t(has_prev)
    both = jnp.concatenate([mask_p, mask_c], axis=0)
    return jnp.concatenate([both, both], axis=1)


def _query_pair(pair):
    return jnp.concatenate([_head_lanes(pair, 0), _head_lanes(pair, 1)], axis=0)


def _key_pair(ref, prev, cur):
    return jnp.concatenate([ref[pl.ds(prev, BLK), :], ref[pl.ds(cur, BLK), :]], axis=0)


def _own_head(both):
    return jnp.concatenate([both[0:HEAD, 0:BLK], both[HEAD:2 * HEAD, BLK:2 * BLK]], axis=0)


def _store_transposed(dst, base, src):
    for j in range(TILE // BLK):
        dst[base // BLK + j] = src[j * BLK:(j + 1) * BLK, :].T.astype(BF16)


class _Dilated:
    def __init__(self, dil):
        self.dil = dil
        self.per = TILE // dil
        self.nbr = self.per // BLK

    def spread(self, dst, base, src_ref, dtype):
        for r in range(self.dil):
            rows = src_ref[pl.ds(r, self.per, stride=self.dil), :] if self.dil > 1 else src_ref[...]
            dst[pl.ds(pl.multiple_of(base + r * self.per, BLK), self.per), :] = rows.astype(dtype)

    def gather(self, dst_ref, src, base):
        for r in range(self.dil):
            rows = src[pl.ds(pl.multiple_of(base + r * self.per, BLK), self.per), :]
            if self.dil > 1:
                dst_ref[pl.ds(r, self.per, stride=self.dil), :] = rows
            else:
                dst_ref[...] = rows

    def block_rows(self, b, i, cur, prv):
        n = b % self.nbr
        row = pl.multiple_of(b * BLK, BLK)
        has_prev = jnp.logical_or(n > 0, i > 0)
        prev = jnp.where(n > 0, cur + row - BLK, jnp.where(i > 0, prv + row + (self.nbr - 1) * BLK, cur + row))
        return row, pl.multiple_of(prev, BLK), has_prev


def _slots(i):
    return pl.multiple_of((i % 2) * TILE, TILE), pl.multiple_of(((i + 1) % 2) * TILE, TILE)


def _nt(a, b):
    return lax.dot_general(a, b, (((1,), (1,)), ((), ())), preferred_element_type=F32)


def _qkv_specs(gi, clamp_to=None):
    def spec(col0):
        def imap(hp, i):
            return (i if clamp_to is None else jnp.minimum(i, clamp_to), (col0 + gi * ATT) // 128 + hp)
        return pl.BlockSpec((TILE, 128), imap)
    return [spec(C_Q), spec(C_K), spec(C_V)]


def _att_fwd(proj, tables):
    t = proj.shape[0]

    def body(*refs):
        qkv_refs, (c_ref, lo_ref, hi_ref, att_ref, lse_ref, tmp, qd, od, ld), kv_scratch = refs[:9], refs[9:18], refs[18:]
        i = pl.program_id(1)
        cur, prv = _slots(i)
        cs, lo, hi = c_ref[...], lo_ref[...], hi_ref[...]
        for gi, dil in GROUPS:
            dl = _Dilated(dil)
            q_ref, k_ref, v_ref = qkv_refs[3 * gi:3 * gi + 3]
            kd, vt = kv_scratch[2 * gi:2 * gi + 2]
            tmp[...] = _rope_apply(q_ref[...], cs, lo, hi) * SM_SCALE
            dl.spread(qd, 0, tmp, BF16)
            tmp[...] = _rope_apply(k_ref[...], cs, lo, hi)
            dl.spread(kd, cur, tmp, BF16)
            dl.spread(tmp, 0, v_ref, F32)
            _store_transposed(vt, cur, tmp)

            def block(b, carry, dl=dl, kd=kd, vt=vt):
                row, prev, has_prev = dl.block_rows(b, i, cur, prv)
                s = jnp.where(_pair_mask(has_prev), _nt(_key_pair(kd, prev, cur + row), _query_pair(qd[pl.ds(row, BLK), :])), NEG_INF)
                mx = jnp.max(s, axis=0, keepdims=True)
                p = jnp.exp(s - mx)
                den = jnp.sum(p, axis=0, keepdims=True)
                v_t = jnp.concatenate([vt[prev // BLK], vt[(cur + row) // BLK]], axis=1)
                acc = jnp.dot(v_t, p.astype(BF16), preferred_element_type=F32) / den
                lse = mx + jnp.log(den)
                od[pl.ds(row, BLK), :] = _own_head(acc).T
                ld[pl.ds(row, BLK), :] = _own_head(jnp.broadcast_to(lse, (2 * HEAD, 2 * BLK))).T
                return carry

            lax.fori_loop(0, TILE // BLK, block, 0, unroll=True)
            if gi == 0:
                dl.gather(att_ref, od, 0)
                dl.gather(lse_ref, ld, 0)
            else:
                dl.gather(tmp, ld, 0)
                l_run, l_new = lse_ref[...], tmp[...]
                m = jnp.maximum(l_run, l_new)
                w_run, w_new = jnp.exp(l_run - m), jnp.exp(l_new - m)
                lse_ref[...] = m + jnp.log(w_run + w_new)
                ld[...] = w_new / (w_run + w_new)
                dl.gather(tmp, od, 0)
                share = ld[...]
                att_ref[...] = att_ref[...] + share * (tmp[...] - att_ref[...])

    tab = pl.BlockSpec((TILE, 128), lambda hp, i: (i, 0))
    out_spec = pl.BlockSpec((TILE, 128), lambda hp, i: (i, hp))
    kv_shapes = [pltpu.VMEM((2 * TILE, 128), BF16), pltpu.VMEM((2 * TILE // BLK, 128, BLK), BF16)] * len(GROUPS)
    return pl.pallas_call(
        body, name="att_fwd", grid=(ATT // 128, t // TILE),
        in_specs=[spec for gi, _ in GROUPS for spec in _qkv_specs(gi)] + [tab] * 3,
        out_specs=[out_spec] * 2, out_shape=[jax.ShapeDtypeStruct((t, ATT), F32)] * 2,
        scratch_shapes=[pltpu.VMEM((TILE, 128), F32), pltpu.VMEM((TILE, 128), BF16), pltpu.VMEM((TILE, 128), F32),
                        pltpu.VMEM((TILE, 128), F32)] + kv_shapes,
    )(*([proj] * (3 * len(GROUPS))), *tables)


def _att_bwd(proj, tables, datt, dsum, lse, gi, dil):
    t = proj.shape[0]
    nt = t // TILE
    dl = _Dilated(dil)

    def body(q_ref, k_ref, v_ref, c_ref, lo_ref, hi_ref, cl_ref, lol_ref, hil_ref, do_ref, ds_ref, lse_ref,
             dq_ref, dk_ref, dv_ref, tmp, qd, kd, vd, dod, dsd, lsd, dqd, dkd, dvd, kt):
        i = pl.program_id(1)
        cur, prv = _slots(i)

        @pl.when(i < nt)
        def _():
            cs, lo, hi = c_ref[...], lo_ref[...], hi_ref[...]
            tmp[...] = _rope_apply(q_ref[...], cs, lo, hi) * SM_SCALE
            dl.spread(qd, 0, tmp, BF16)
            tmp[...] = _rope_apply(k_ref[...], cs, lo, hi)
            dl.spread(kd, cur, tmp, BF16)
            dl.spread(dqd, 0, tmp, F32)
            _store_transposed(kt, cur, dqd)
            dl.spread(vd, cur, v_ref, BF16)
            dl.spread(dod, 0, do_ref, BF16)
            dl.spread(dsd, 0, ds_ref, F32)
            dl.spread(lsd, 0, lse_ref, F32)
            dkd[pl.ds(cur, TILE), :] = jnp.zeros((TILE, 128), F32)
            dvd[pl.ds(cur, TILE), :] = jnp.zeros((TILE, 128), F32)

            def block(b, carry):
                row, prev, has_prev = dl.block_rows(b, i, cur, prv)
                q_pair, do_pair = _query_pair(qd[pl.ds(row, BLK), :]), _query_pair(dod[pl.ds(row, BLK), :])
                k_pair, v_pair = _key_pair(kd, prev, cur + row), _key_pair(vd, prev, cur + row)
                ds_t, ls_t = dsd[pl.ds(row, BLK), :].T, lsd[pl.ds(row, BLK), :].T
                lse = jnp.concatenate([ls_t[0:1, :], ls_t[HEAD:HEAD + 1, :]], axis=1)
                dsm = jnp.concatenate([ds_t[0:1, :], ds_t[HEAD:HEAD + 1, :]], axis=1)
                p = jnp.exp(jnp.where(_pair_mask(has_prev), _nt(k_pair, q_pair), NEG_INF) - lse)
                ds = (p * (_nt(v_pair, do_pair) - dsm)).astype(BF16)
                k_t = jnp.concatenate([kt[prev // BLK], kt[(cur + row) // BLK]], axis=1)
                dqd[pl.ds(row, BLK), :] = _own_head(jnp.dot(k_t, ds, preferred_element_type=F32)).T * SM_SCALE
                dk = jnp.dot(ds, q_pair, preferred_element_type=F32)
                dv = jnp.dot(p.astype(BF16), do_pair, preferred_element_type=F32)
                dkd[pl.ds(cur + row, BLK), :] += dk[BLK:2 * BLK, :]
                dvd[pl.ds(cur + row, BLK), :] += dv[BLK:2 * BLK, :]
                dkd[pl.ds(prev, BLK), :] += dk[0:BLK, :]
                dvd[pl.ds(prev, BLK), :] += dv[0:BLK, :]
                return carry

            lax.fori_loop(0, TILE // BLK, block, 0, unroll=True)
            dl.gather(tmp, dqd, 0)
            dq_ref[...] = _rope_apply(tmp[...], cs, -lo, -hi).astype(BF16)

        @pl.when(i > 0)
        def _():
            dl.gather(tmp, dkd, prv)
            dk_ref[...] = _rope_apply(tmp[...], cl_ref[...], -lol_ref[...], -hil_ref[...]).astype(BF16)
            dl.gather(tmp, dvd, prv)
            dv_ref[...] = tmp[...].astype(BF16)

    now = lambda col: pl.BlockSpec((TILE, 128), lambda hp, i: (jnp.minimum(i, nt - 1), col(hp)))
    lag = lambda col: pl.BlockSpec((TILE, 128), lambda hp, i: (jnp.maximum(i - 1, 0), col(hp)))
    first, pair = (lambda hp: 0), (lambda hp: hp)
    return pl.pallas_call(
        body, name=f"att_bwd_g{gi}", grid=(ATT // 128, nt + 1),
        in_specs=_qkv_specs(gi, nt - 1) + [now(first)] * 3 + [lag(first)] * 3 + [now(pair)] * 3,
        out_specs=[now(pair), lag(pair), lag(pair)],
        out_shape=[jax.ShapeDtypeStruct((t, ATT), BF16)] * 3,
        scratch_shapes=[pltpu.VMEM((TILE, 128), F32), pltpu.VMEM((TILE, 128), BF16), pltpu.VMEM((2 * TILE, 128), BF16),
                        pltpu.VMEM((2 * TILE, 128), BF16), pltpu.VMEM((TILE, 128), BF16), pltpu.VMEM((TILE, 128), F32),
                        pltpu.VMEM((TILE, 128), F32), pltpu.VMEM((TILE, 128), F32), pltpu.VMEM((2 * TILE, 128), F32),
                        pltpu.VMEM((2 * TILE, 128), F32), pltpu.VMEM((2 * TILE // BLK, 128, BLK), BF16)],
    )(proj, proj, proj, *tables, *tables, datt, dsum, lse)


def _merge_head(pc, att, proj, x, mod, final_g, target, w_co, w_ao, w_o, u1, ln_g, ln_b, tm):
    t = x.shape[0]

    def body(pc_ref, att_ref, gc_ref, ga_ref, x_ref, mod_ref, fg_ref, tg_ref, wco_ref, wao_ref, wo_ref,
             zc_ref, u1_ref, lg_ref, lb_ref, za_ref,
             merged_ref, do_ref, dyc_ref, dya_ref, dout_ref, du1_ref, dzc_ref, datt_ref, ds_ref, dp_ref, pa_ref,
             sq_ref, gfg_ref, dgate_ref, dlg_ref, dlb_ref, dcb_ref):
        i = pl.program_id(0)
        att_v = att_ref[...]
        za = za_ref[...]
        sza = _sig(za)
        pa = (att_v * (za * sza)).astype(BF16)
        pa_ref[...] = pa
        yc = jnp.dot(pc_ref[...], wco_ref[...], preferred_element_type=F32)
        ya = jnp.dot(pa, wao_ref[...], preferred_element_type=F32)
        sc, sa = _sig(gc_ref[...]), _sig(ga_ref[...])
        merged = (sc * yc + sa * ya).astype(BF16)
        merged_ref[...] = merged
        ov = jnp.dot(merged, wo_ref[...], preferred_element_type=F32)
        gate = mod_ref[:, 2 * D:3 * D]
        out = x_ref[...] + gate * ov
        r = lax.rsqrt(jnp.mean(out * out, axis=-1, keepdims=True) + EPS)
        yn = out * r
        diff = yn * fg_ref[...] - tg_ref[...]
        dy = diff * (1.0 / D)
        gy = dy * fg_ref[...]
        dout = r * (gy - yn * jnp.mean(gy * yn, axis=-1, keepdims=True))
        dout_ref[...] = dout
        do = (dout * gate).astype(BF16)
        do_ref[...] = do
        _acc_rows(sq_ref, i, diff * diff)
        _acc_rows(gfg_ref, i, dy * yn)
        _acc_rows(dgate_ref, i, dout * ov)
        dm = _nt(do, wo_ref[...])
        dyc = (dm * sc).astype(BF16)
        dya = (dm * sa).astype(BF16)
        dyc_ref[...] = dyc
        dya_ref[...] = dya
        dp_ref[:, ATT:ATT + D] = (dm * yc * sc * (1.0 - sc)).astype(BF16)
        dp_ref[:, ATT + D:ATT + 2 * D] = (dm * ya * sa * (1.0 - sa)).astype(BF16)

        dpc = _nt(dyc, wco_ref[...])
        u1v = u1_ref[...]
        xc = u1v - jnp.mean(u1v, axis=-1, keepdims=True)
        rs = lax.rsqrt(jnp.mean(xc * xc, axis=-1, keepdims=True) + EPS)
        uhat = xc * rs
        u2 = uhat * lg_ref[...] + lb_ref[...]
        s2 = _sig(u2)
        zc = zc_ref[...]
        szc = _sig(zc)
        dzc_ref[...] = (dpc * (u2 * s2) * _dsilu(zc, szc)).astype(BF16)
        du2 = dpc * (zc * szc) * _dsilu(u2, s2)
        duhat = du2 * lg_ref[...]
        du1 = rs * (duhat - jnp.mean(duhat, axis=-1, keepdims=True) - uhat * jnp.mean(duhat * uhat, axis=-1, keepdims=True))
        du1_ref[...] = du1
        _acc_rows(dlg_ref, i, du2 * uhat)
        _acc_rows(dlb_ref, i, du2)
        _acc_rows(dcb_ref, i, du1)

        dpa = _nt(dya, wao_ref[...])
        datt = dpa * (za * sza)
        datt_ref[...] = datt
        dp_ref[:, 0:ATT] = (dpa * att_v * _dsilu(za, sza)).astype(BF16)
        prod = datt * att_v
        for hd in range(ATT // HEAD):
            sl = slice(hd * HEAD, (hd + 1) * HEAD)
            ds_ref[:, sl] = jnp.broadcast_to(jnp.sum(prod[:, sl], axis=-1, keepdims=True), (tm, HEAD))

    vec = _full((1, D))
    bf = lambda w: jax.ShapeDtypeStruct((t, w), BF16)
    f32 = lambda w: jax.ShapeDtypeStruct((t, w), F32)
    tail = ATT + 2 * D
    return pl.pallas_call(
        body, name="merge_head", grid=(t // tm,),
        in_specs=[_rows(tm, D), _rows(tm, ATT), _rows(tm, D, C_GC // D), _rows(tm, D, C_GA // D), _rows(tm, D),
                  _full((1, 3 * D)), vec, _rows(tm, D), _full((D, D)), _full((ATT, D)), _full((D, D)),
                  _rows(tm, D, C_ZC // D), _rows(tm, D), vec, vec, _rows(tm, ATT, C_ZA // ATT)],
        out_specs=[_rows(tm, D)] * 7 + [_rows(tm, ATT), _rows(tm, ATT), _rows(tm, tail, C_ZA // tail),
                                        _rows(tm, ATT)] + [vec] * 6,
        out_shape=[bf(D), bf(D), bf(D), bf(D), f32(D), f32(D), bf(D), f32(ATT), f32(ATT), bf(N_COL), bf(ATT)]
        + [jax.ShapeDtypeStruct((1, D), F32)] * 6,
    )(pc, att, proj, proj, x, mod, final_g, target, w_co, w_ao, w_o, proj, u1, ln_g, ln_b, proj)


def _acc_rows(ref, i, val):
    @pl.when(i == 0)
    def _():
        ref[...] = jnp.zeros_like(ref)

    ref[...] += jnp.sum(val, axis=0, keepdims=True)


def _conv_bwd_taps(du1, proj, conv_w, dzc, dqkv, dproj, tm):
    t = proj.shape[0]
    hb = tm // HALO
    last = t // HALO - 1

    def body(du_ref, duh_ref, a_ref, b_ref, ah_ref, bh_ref, w_ref, dzc_ref, *rest):
        qkv_refs, (dp_in, dp_ref, dw_ref, dbuf, ubuf, g0, shd, shu) = rest[:9], rest[9:]
        del dp_in
        dp_ref[:, C_ZC:C_ZC + D] = dzc_ref[...]
        for n, ref in enumerate(qkv_refs):
            dp_ref[:, C_Q + n * ATT:C_Q + (n + 1) * ATT] = ref[...]
        i = pl.program_id(0)
        a, sb = a_ref[...], _sig(b_ref[...])
        ubuf[0:HALO, :] = jnp.where(i > 0, ah_ref[...] * _sig(bh_ref[...]), 0.0)
        ubuf[HALO:HALO + tm, :] = a * sb
        dbuf[0:tm, :] = du_ref[...]
        dbuf[tm:tm + HALO, :] = jnp.where(i < pl.num_programs(0) - 1, duh_ref[...], 0.0)

        @pl.when(i == 0)
        def _():
            dw_ref[...] = jnp.zeros_like(dw_ref)

        def col(ci, carry):
            c0 = pl.multiple_of(ci * 128, 128)
            _shift_copies(shd, dbuf, c0)
            _shift_copies(shu, ubuf, c0)
            for rc in range(tm // 64):
                g0[rc * 64:(rc + 1) * 64, pl.ds(c0, 128)] = _conv_taps(
                    jnp.zeros((64, 128), F32), w_ref, dbuf, shd, rc * 64, c0, lambda j: CONV_K - 1 - j)
            for j in range(CONV_K):
                part = jnp.zeros((8, 128), F32)
                for rc in range(tm // 64):
                    off = rc * 64 + HALO - (CONV_K - 1) + j
                    prod = dbuf[rc * 64:(rc + 1) * 64, pl.ds(c0, 128)] * _window64(ubuf, shu, c0, off)
                    part = part + jnp.sum(prod.reshape(8, 8, 128), axis=0)
                dw_ref[j:j + 1, pl.ds(c0, 128)] += jnp.sum(part, axis=0, keepdims=True)
            return carry

        lax.fori_loop(0, D // 128, col, 0)
        du0 = g0[...]
        dp_ref[:, 0:D] = (du0 * sb).astype(BF16)
        dp_ref[:, D:2 * D] = (du0 * a * sb * (1.0 - sb)).astype(BF16)

    prev = lambda col: pl.BlockSpec((HALO, D), lambda i: (jnp.maximum(i * hb - 1, 0), col))
    nxt = pl.BlockSpec((HALO, D), lambda i: (jnp.minimum((i + 1) * hb, last), 0))
    return pl.pallas_call(
        body, name="conv_bwd_taps", grid=(t // tm,),
        in_specs=[_rows(tm, D), nxt, _rows(tm, D, 0), _rows(tm, D, 1), prev(0), prev(1), _full((CONV_KP, D)),
                  _rows(tm, D)] + [_rows(tm, ATT)] * 9 + [ANY],
        out_specs=[_rows(tm, C_ZA, 0), _full((CONV_KP, D))],
        out_shape=[jax.ShapeDtypeStruct((t, N_COL), BF16), jax.ShapeDtypeStruct((CONV_KP, D), F32)],
        scratch_shapes=[pltpu.VMEM((tm + HALO, D), F32), pltpu.VMEM((HALO + tm, D), F32), pltpu.VMEM((tm, D), F32),
                        pltpu.VMEM((8, HALO + tm, 128), F32), pltpu.VMEM((8, HALO + tm, 128), F32)],
        input_output_aliases={17: 0},
    )(du1, du1, proj, proj, proj, proj, conv_w, dzc, *dqkv, dproj)


def _dh_prenorm_bwd(dproj, w_in_blocks, x, dout, mod, norm_g, tm):
    t = x.shape[0]
    nk, _, tk = w_in_blocks.shape

    def body(dp_ref, w_ref, x_ref, dout_ref, mod_ref, g_ref, gx_ref, dshift_ref, dscale_ref, dg_ref, acc):
        i, kk = pl.program_id(0), pl.program_id(1)
        p = _nt(dp_ref[...], w_ref[...])

        @pl.when(kk == 0)
        def _():
            acc[...] = p

        @pl.when(kk > 0)
        def _():
            acc[...] += p

        @pl.when(kk == nk - 1)
        def _():
            xv, dhv = x_ref[...], acc[...]
            r = lax.rsqrt(jnp.mean(xv * xv, axis=-1, keepdims=True) + EPS)
            xn = xv * r
            one_scale = 1.0 + mod_ref[:, D:2 * D]
            dxn = dhv * (g_ref[...] * one_scale)
            gx_ref[...] = r * (dxn - xn * jnp.mean(dxn * xn, axis=-1, keepdims=True)) + dout_ref[...]
            _acc_rows(dshift_ref, i, dhv)
            _acc_rows(dscale_ref, i, dhv * xn * g_ref[...])
            _acc_rows(dg_ref, i, dhv * xn * one_scale)

    row = pl.BlockSpec((tm, D), lambda i, kk: (i, 0))
    vec = pl.BlockSpec((1, D), lambda i, kk: (0, 0))
    return pl.pallas_call(
        body, name="dh_prenorm_bwd", grid=(t // tm, nk),
        in_specs=[pl.BlockSpec((tm, tk), lambda i, kk: (i, kk)), pl.BlockSpec((None, D, tk), lambda i, kk: (kk, 0, 0)),
                  row, row, pl.BlockSpec((1, 3 * D), lambda i, kk: (0, 0)), vec],
        out_specs=[row, vec, vec, vec],
        out_shape=[jax.ShapeDtypeStruct((t, D), F32)] + [jax.ShapeDtypeStruct((1, D), F32)] * 3,
        scratch_shapes=[pltpu.VMEM((tm, D), F32)],
    )(dproj, w_in_blocks, x, dout, mod, norm_g)


def _sum_devices(gathered):
    w = gathered.shape[-1]

    def body(g_ref, o_ref):
        acc = g_ref[0]
        for j in range(1, N_DEV):
            acc = acc + g_ref[j]
        o_ref[...] = acc

    return pl.pallas_call(body, name="sum_devices", grid=(1,), in_specs=[_full(gathered.shape)], out_specs=_full((1, w)),
                          out_shape=jax.ShapeDtypeStruct((1, w), F32))(gathered)


def _rope_tables(positions):
    half = HEAD // 8
    t = positions.shape[-1]
    inv_freq = ROPE_THETA ** (-(jnp.arange(half, dtype=F32) * 2.0 / (2 * half)))
    ang = positions.reshape(t, 1).astype(F32) * inv_freq
    cos, sin = jnp.cos(ang), jnp.sin(ang)
    zeros = lambda n: jnp.zeros((t, n), F32)
    c64 = jnp.concatenate([cos, cos, jnp.ones((t, HEAD - 2 * half), F32)], axis=1)
    lo64 = jnp.concatenate([-sin, zeros(HEAD - half)], axis=1)
    hi64 = jnp.concatenate([zeros(half), sin, zeros(HEAD - 2 * half)], axis=1)
    return tuple(jnp.tile(a, (1, 2)) for a in (c64, lo64, hi64))


def kernel(x, c, positions, norm_g, w_ada, b_ada, w_in, conv_w, conv_b, conv_ln_g, conv_ln_b, w_conv_out, w_att_out, w_o, final_g, loss_target, m_norm_g, m_w_ada, m_b_ada, m_w_in, m_conv_w, m_conv_b, m_conv_ln_g, m_conv_ln_b, m_w_conv_out, m_w_att_out, m_w_o, m_final_g, v_norm_g, v_w_ada, v_b_ada, v_w_in, v_conv_w, v_conv_b, v_conv_ln_g, v_conv_ln_b, v_w_conv_out, v_w_att_out, v_w_o, v_final_g):
    me = 4 * lax.axis_index("x") + 2 * lax.axis_index("y") + lax.axis_index("c")
    x2, tgt = x[0], loss_target[0]
    t = x2.shape[0]
    te = 512 if t % 512 == 0 else 256
    tcv = 256
    tmh = 256
    tmm = 1024 if t % 1024 == 0 else 256
    n_ada = w_ada.shape[-1]

    pad_taps = lambda a: jnp.pad(a[0], ((0, CONV_KP - CONV_K), (0, 0)))
    shards = (_cast_bf16(w_in[0], "cast_w_in"), _cast_bf16(w_conv_out[0], "cast_w_conv_out"),
              _cast_bf16(w_att_out[0], "cast_w_att_out"), _cast_bf16(w_o[0], "cast_w_o"), pad_taps(conv_w))
    block_of = lambda relations: jnp.bitwise_xor(me, jnp.array(relations, jnp.int32))

    c_all = _allgather_small(c, "gather_c").reshape(N_DEV, D)
    b_ada_l = lax.dynamic_slice(b_ada, (0, me * n_ada), (1, n_ada))
    parts = _allgather_small(_mod_part(c_all, w_ada[0], b_ada_l), "gather_mod")
    mod = lax.dynamic_slice(parts, (0, me, 0), (N_DEV, 1, n_ada)).reshape(1, N_DEV * n_ada)

    h, ht = _prenorm(x2, mod, norm_g, te)
    proj, w_in_f, w_co_f, w_ao_f, w_o_f, conv_w_f = _proj_gather(h, shards, block_of(GATHER_ORDER), tmm)
    u1, pc = _conv_fwd(proj, conv_w_f, conv_b, conv_ln_g, conv_ln_b, tcv)
    tables = _rope_tables(positions)
    att, lse = _att_fwd(proj, tables)

    (merged, do, dyc, dya, dout, du1, dzc, datt, dsum, dproj, pa,
     sq_sum, g_final, d_gate, d_ln_g, d_ln_b, d_conv_b) = _merge_head(
        pc, att, proj, x2, mod, final_g.reshape(1, D), tgt, w_co_f, w_ao_f, w_o_f, u1, conv_ln_g, conv_ln_b, tmh)

    dw_o = _matmul(merged, do, ta=True, out_dtype=BF16, tm=D, tn=D, tk=512, name="dw_o")
    dw_co = _matmul(pc, dyc, ta=True, out_dtype=BF16, tm=D, tn=D, tk=512, name="dw_conv_out")
    dw_ao = _matmul(pa, dya, ta=True, out_dtype=BF16, tm=ATT, tn=D, tk=512, name="dw_att_out")
    dqs, dks, dvs = [], [], []
    for gi, dil in GROUPS:
        dq, dk, dv = _att_bwd(proj, tables, datt, dsum, lse, gi, dil)
        dqs.append(dq), dks.append(dk), dvs.append(dv)
    dproj, dconv_w = _conv_bwd_taps(du1, proj, conv_w_f, dzc, dqs + dks + dvs, dproj, tcv)
    grad_x, d_shift, d_scale, d_norm_g = _dh_prenorm_bwd(dproj, w_in_f, x2, dout, mod, norm_g, tmm)

    packed = jnp.concatenate([d_shift, d_scale, d_gate, d_norm_g, d_conv_b, d_ln_g, d_ln_b, g_final, sq_sum], axis=1)
    gathered = _allgather_small(packed, "gather_partials")
    total = _sum_devices(gathered)
    seg = lambda k, n=1: total[:, k * D:(k + n) * D]
    g_b_ada, g_norm_g, g_conv_b, g_ln_g, g_ln_b, g_final_g = seg(0, 3), seg(3), seg(4), seg(5), seg(6), seg(7)
    loss = (0.5 / D) * jnp.sum(seg(8))
    dmod_all = gathered[:, 0, 0:3 * D]
    dmod_cols = lax.dynamic_slice(dmod_all, (0, me * n_ada), (N_DEV, n_ada))
    g_w_ada, d_w_ada, nm_w_ada, nv_w_ada = _w_ada_update(c_all.T, dmod_cols, w_ada[0], m_w_ada[0], v_w_ada[0])

    small = {}
    for name, g, w, m, v in (("norm_g", g_norm_g, norm_g, m_norm_g, v_norm_g), ("b_ada", g_b_ada, b_ada, m_b_ada, v_b_ada),
                             ("conv_b", g_conv_b, conv_b, m_conv_b, v_conv_b), ("conv_ln_g", g_ln_g, conv_ln_g, m_conv_ln_g, v_conv_ln_g),
                             ("conv_ln_b", g_ln_b, conv_ln_b, m_conv_ln_b, v_conv_ln_b),
                             ("final_g", g_final_g, final_g.reshape(1, D), m_final_g.reshape(1, D), v_final_g.reshape(1, D))):
        small[name] = (g,) + tuple(_adamw_small(g, w, m, v, "adamw_" + name))

    slots = _dw_in_scatter(ht, dproj, (dw_co, dw_ao, dw_o, dconv_w), block_of(SCATTER_ORDER), 1024)
    big = {
        "w_in": _sum_adamw(slots[0], w_in[0], m_w_in[0], v_w_in[0], 256, "adamw_w_in"),
        "w_conv_out": _sum_adamw(slots[1], w_conv_out[0], m_w_conv_out[0], v_w_conv_out[0], 128, "adamw_w_conv_out"),
        "w_att_out": _sum_adamw(slots[2], w_att_out[0], m_w_att_out[0], v_w_att_out[0], 512, "adamw_w_att_out"),
        "w_o": _sum_adamw(slots[3], w_o[0], m_w_o[0], v_w_o[0], 128, "adamw_w_o"),
        "conv_w": [r[:CONV_K] for r in _sum_adamw(slots[4], pad_taps(conv_w), pad_taps(m_conv_w), pad_taps(v_conv_w), CONV_KP, "adamw_conv_w")],
    }
    big["w_ada"] = (g_w_ada, d_w_ada, nm_w_ada, nv_w_ada)

    order = ("norm_g", "w_ada", "b_ada", "w_in", "conv_w", "conv_b", "conv_ln_g", "conv_ln_b", "w_conv_out", "w_att_out", "w_o", "final_g")
    lead = lambda name, a: a.reshape(D) if name == "final_g" else (a[None] if name in big else a)
    result = {**small, **big}
    outs = [loss, grad_x[None]]
    for field in range(4):
        outs += [lead(name, result[name][field]) for name in order]
    return tuple(outs)
```

```python
import jax
import jax.numpy as jnp
from jax import lax
from jax.experimental import pallas as pl
from jax.experimental.pallas import tpu as pltpu

F32 = jnp.float32
BF16 = jnp.bfloat16

N_DEV = 8
D = 1024
N_COL = 10240
C_A, C_B, C_ZC, C_Q, C_K, C_V, C_ZA, C_GC, C_GA = 0, 1024, 2048, 3072, 4608, 6144, 7680, 8192, 9216
QKV = 1536
ATT = 512
HEAD = 64
BLK = 128
TILE = 2048
GROUPS = ((0, 1), (1, 4), (2, 16))
CONV_K = 31
CONV_KP = 32
HALO = 32
EPS = 1e-6
NEG_INF = -1e30
ROPE_THETA = 500000.0
SM_SCALE = HEAD ** -0.5

ADAM_LR, ADAM_B1, ADAM_B2, ADAM_EPS, ADAM_WD, ADAM_STEP = 0.001, 0.9, 0.999, 1e-08, 0.01, 10

MESH = pl.DeviceIdType.MESH
ANY = pl.BlockSpec(memory_space=pl.ANY)


def _sig(v):
    return 1.0 / (1.0 + jnp.exp(-v))


def _dsilu(v, s):
    return s * (1.0 + v * (1.0 - s))


def _full(shape):
    return pl.BlockSpec(shape, lambda *_: (0,) * len(shape))


def _rows(tm, width, col=0):
    return pl.BlockSpec((tm, width), lambda i: (i, col))


def _matmul(a, b, *, ta=False, tb=False, out_dtype=F32, tm, tn, tk, name):
    m, k = (a.shape[1], a.shape[0]) if ta else a.shape
    n = b.shape[0] if tb else b.shape[1]
    assert (b.shape[1] if tb else b.shape[0]) == k
    assert m % tm == 0 and n % tn == 0 and k % tk == 0
    nk = k // tk
    dims = (((0 if ta else 1,), (1 if tb else 0,)), ((), ()))
    use_scratch = out_dtype != F32 and nk > 1

    def body(a_ref, b_ref, o_ref, *scratch):
        p = lax.dot_general(a_ref[...], b_ref[...], dims, preferred_element_type=F32)
        if nk == 1:
            o_ref[...] = p.astype(out_dtype)
            return
        acc = scratch[0] if use_scratch else o_ref
        kk = pl.program_id(2)

        @pl.when(kk == 0)
        def _():
            acc[...] = p

        @pl.when(kk > 0)
        def _():
            acc[...] += p

        if use_scratch:
            @pl.when(kk == nk - 1)
            def _():
                o_ref[...] = acc[...].astype(out_dtype)

    a_spec = pl.BlockSpec((tk, tm), lambda i, j, kk: (kk, i)) if ta else pl.BlockSpec((tm, tk), lambda i, j, kk: (i, kk))
    b_spec = pl.BlockSpec((tn, tk), lambda i, j, kk: (j, kk)) if tb else pl.BlockSpec((tk, tn), lambda i, j, kk: (kk, j))
    return pl.pallas_call(
        body, name=name, grid=(m // tm, n // tn, nk),
        in_specs=[a_spec, b_spec],
        out_specs=pl.BlockSpec((tm, tn), lambda i, j, kk: (i, j)),
        out_shape=jax.ShapeDtypeStruct((m, n), out_dtype),
        scratch_shapes=[pltpu.VMEM((tm, tn), F32)] if use_scratch else [],
    )(a, b)


def _me_and_peers():
    x, y, c = lax.axis_index("x"), lax.axis_index("y"), lax.axis_index("c")
    me = 4 * x + 2 * y + c
    peers = []
    for k in range(1, N_DEV):
        px, py, pc = x ^ (k >> 2), y ^ ((k >> 1) & 1), c ^ (k & 1)
        peers.append(((px, py, pc), 4 * px + 2 * py + pc))
    return me, peers


def _allgather_small(v, name):
    r, c = v.shape

    def body(v_ref, out_ref, send_sems, recv_sems):
        me, peers = _me_and_peers()
        out_ref[me] = v_ref[...]
        copies = []
        for k, (dev, _) in enumerate(peers):
            cp = pltpu.make_async_remote_copy(src_ref=v_ref, dst_ref=out_ref.at[me], send_sem=send_sems.at[k],
                                              recv_sem=recv_sems.at[k], device_id=dev, device_id_type=MESH)
            cp.start()
            copies.append(cp)
        for k, (dev, idx) in enumerate(peers):
            pltpu.make_async_remote_copy(src_ref=v_ref, dst_ref=out_ref.at[idx], send_sem=send_sems.at[k],
                                         recv_sem=recv_sems.at[k], device_id=dev, device_id_type=MESH).wait_recv()
        for cp in copies:
            cp.wait_send()

    return pl.pallas_call(
        body, name=name,
        in_specs=[pl.BlockSpec(memory_space=pltpu.VMEM)],
        out_specs=pl.BlockSpec(memory_space=pltpu.VMEM),
        out_shape=jax.ShapeDtypeStruct((N_DEV, r, c), v.dtype),
        scratch_shapes=[pltpu.SemaphoreType.DMA((N_DEV - 1,)), pltpu.SemaphoreType.DMA((N_DEV - 1,))],
    )(v)


def _window(ref, kind, idx, size):
    if kind == "block":
        return ref.at[idx]
    start = pl.multiple_of(idx * size, size)
    if kind == "rows":
        return ref.at[pl.ds(start, size), :]
    return ref.at[:, pl.ds(start, size)]


_BIG = (("cols", N_COL // N_DEV), ("rows", D // N_DEV), ("cols", D // N_DEV), ("rows", D // N_DEV), ("cols", D // N_DEV))
_GATHERED = (("block", 1),) + _BIG[1:]


GATHER_ORDER = (0, 1, 2, 4, 3, 5, 6, 7)
W_IN_DIRECT = (1, 2, 4, 6)
SCATTER_ORDER = (7, 6, 5, 4, 3, 2, 1, 0)
W_IN_SLOT = {0: 0, 1: 1, 2: 2, 4: 3, 6: 4}


def _proj_gather(h, shards, order, tm):
    t = h.shape[0]
    nt = len(shards)
    n_blk = N_COL // N_DEV
    full_shapes = []
    for s, (kind, size) in zip(shards, _GATHERED):
        full_shapes.append(jax.ShapeDtypeStruct({"block": (N_DEV,) + s.shape, "rows": (s.shape[0] * N_DEV, s.shape[1]),
                                                 "cols": (s.shape[0], s.shape[1] * N_DEV)}[kind], s.dtype))
    last = (N_DEV - 1, t // tm - 1)

    def body(order_ref, h_ref, *refs):
        src, proj_ref, dst = refs[:nt], refs[nt], refs[nt + 1:2 * nt + 1]
        w_all, send_sems, recv_sems, local_sems, keep_sems = refs[2 * nt + 1:]
        j, i = pl.program_id(0), pl.program_id(1)
        me, peers = _me_and_peers()

        def landing(tn, idx):
            kind, size = _GATHERED[tn]
            return w_all.at[idx] if tn == 0 else _window(dst[tn], kind, idx, size)

        def local(tn):
            return pltpu.make_async_copy(src[tn], landing(tn, me), local_sems.at[tn])

        def remote(tn, k, block_of):
            dev, idx = peers[k - 1]
            return pltpu.make_async_remote_copy(src_ref=src[tn], dst_ref=landing(tn, me if block_of == "mine" else idx),
                                                send_sem=send_sems.at[tn, k - 1], recv_sem=recv_sems.at[tn, k - 1],
                                                device_id=dev, device_id_type=MESH)

        def forward(k):
            block = w_all.at[peers[k - 1][1]]
            return pltpu.make_async_remote_copy(src_ref=block, dst_ref=block, send_sem=send_sems.at[0, k], recv_sem=recv_sems.at[0, k],
                                                device_id=peers[0][0], device_id_type=MESH)

        def keep(step):
            blk = order_ref[step]
            return pltpu.make_async_copy(w_all.at[blk], dst[0].at[blk], keep_sems.at[step])

        @pl.when((j == 0) & (i == 0))
        def _():
            for tn in range(nt):
                local(tn).start()
                for k in GATHER_ORDER[1:]:
                    if tn > 0 or k in W_IN_DIRECT:
                        remote(tn, k, "mine").start()

        @pl.when(i == 0)
        def _():
            for step, k in enumerate(GATHER_ORDER):
                @pl.when(j == step)
                def _():
                    if k == 0:
                        local(0).wait()
                    else:
                        remote(0, k, "theirs").wait_recv()
                        if k in W_IN_DIRECT and k > 1:
                            forward(k).start()
                    keep(step).start()

        proj_ref[...] = jnp.dot(h_ref[...], w_all[order_ref[j]], preferred_element_type=F32)

        @pl.when((j == last[0]) & (i == last[1]))
        def _():
            for step in range(N_DEV):
                keep(step).wait()
            for tn in range(1, nt):
                local(tn).wait()
                for k in range(1, N_DEV):
                    remote(tn, k, "theirs").wait_recv()
            for tn in range(nt):
                for k in range(1, N_DEV):
                    if tn > 0 or k in W_IN_DIRECT:
                        remote(tn, k, "mine").wait_send()
                    else:
                        forward(k - 1).wait_send()

    grid_spec = pltpu.PrefetchScalarGridSpec(
        num_scalar_prefetch=1, grid=(N_DEV, t // tm),
        in_specs=[pl.BlockSpec((tm, D), lambda j, i, order_ref: (i, 0))] + [ANY] * nt,
        out_specs=[pl.BlockSpec((tm, n_blk), lambda j, i, order_ref: (i, order_ref[j]))] + [ANY] * nt,
        scratch_shapes=[pltpu.VMEM((N_DEV, D, n_blk), BF16), pltpu.SemaphoreType.DMA((nt, N_DEV - 1)),
                        pltpu.SemaphoreType.DMA((nt, N_DEV - 1)), pltpu.SemaphoreType.DMA((nt,)), pltpu.SemaphoreType.DMA((N_DEV,))],
    )
    return pl.pallas_call(
        body, name="proj_gather", grid_spec=grid_spec,
        out_shape=[jax.ShapeDtypeStruct((t, N_COL), F32)] + full_shapes,
    )(order, h, *shards)


def _dw_in_scatter(ht, dproj, small_grads, order, tk):
    t = ht.shape[1]
    nt = 1 + len(small_grads)
    n_blk = N_COL // N_DEV
    nk = t // tk
    slot_shapes = [jax.ShapeDtypeStruct((len(W_IN_SLOT), D, n_blk), BF16)]
    for g, (kind, size) in zip(small_grads, _BIG[1:]):
        slot_shapes.append(jax.ShapeDtypeStruct((N_DEV,) + ((size, g.shape[1]) if kind == "rows" else (g.shape[0], size)), g.dtype))

    def body(order_ref, h_ref, dp_ref, *refs):
        src, dst = refs[:nt - 1], refs[nt - 1:2 * nt - 1]
        acc, stage, partner, send_sems, recv_sems, local_sems, pair_send, pair_recv = refs[2 * nt - 1:]
        j, kk = pl.program_id(0), pl.program_id(1)
        me, peers = _me_and_peers()

        def small_local(tn):
            kind, size = _BIG[tn]
            return pltpu.make_async_copy(_window(src[tn - 1], kind, me, size), dst[tn].at[me], local_sems.at[tn])

        def small_remote(tn, k, mine):
            kind, size = _BIG[tn]
            dev, idx = peers[k - 1]
            return pltpu.make_async_remote_copy(src_ref=_window(src[tn - 1], kind, idx if mine else me, size),
                                                dst_ref=dst[tn].at[me if mine else idx],
                                                send_sem=send_sems.at[tn, k - 1], recv_sem=recv_sems.at[tn, k - 1],
                                                device_id=dev, device_id_type=MESH)

        def push(step):
            k, slot = SCATTER_ORDER[step], step % 2
            if k == 0:
                return pltpu.make_async_copy(stage.at[slot], dst[0].at[W_IN_SLOT[0]], local_sems.at[0])
            if k not in W_IN_SLOT:
                p = (k - 3) // 2
                return pltpu.make_async_remote_copy(src_ref=stage.at[slot], dst_ref=partner.at[p], send_sem=pair_send.at[p],
                                                    recv_sem=pair_recv.at[p], device_id=peers[0][0], device_id_type=MESH)
            return pltpu.make_async_remote_copy(src_ref=stage.at[slot], dst_ref=dst[0].at[W_IN_SLOT[k]],
                                                send_sem=send_sems.at[0, k - 1], recv_sem=recv_sems.at[0, k - 1],
                                                device_id=peers[k - 1][0], device_id_type=MESH)

        @pl.when((j == 0) & (kk == 0))
        def _():
            for tn in range(1, nt):
                small_local(tn).start()
                for k in range(1, N_DEV):
                    small_remote(tn, k, True).start()

        p = jnp.dot(h_ref[...], dp_ref[...], preferred_element_type=F32)

        @pl.when(kk == 0)
        def _():
            acc[...] = p

        @pl.when(kk > 0)
        def _():
            acc[...] += p

        @pl.when(kk == nk - 1)
        def _():
            for step, k in enumerate(SCATTER_ORDER):
                @pl.when(j == step)
                def _():
                    if step >= 2:
                        push(step - 2).wait_send()
                    total = acc[...]
                    if k in W_IN_SLOT and k >= 2:
                        p = k // 2 - 1
                        push(SCATTER_ORDER.index(k + 1)).wait_recv()
                        total = total + partner[p].astype(F32)
                    stage[step % 2] = total.astype(BF16)
                    push(step).start()

        @pl.when((j == N_DEV - 1) & (kk == nk - 1))
        def _():
            push(N_DEV - 2).wait_send()
            push(N_DEV - 1).wait()
            for k in (1, 2, 4, 6):
                push(SCATTER_ORDER.index(k)).wait_recv()
            for tn in range(1, nt):
                small_local(tn).wait()
                for k in range(1, N_DEV):
                    small_remote(tn, k, False).wait_recv()
                    small_remote(tn, k, True).wait_send()

    grid_spec = pltpu.PrefetchScalarGridSpec(
        num_scalar_prefetch=1, grid=(N_DEV, nk),
        in_specs=[pl.BlockSpec((D, tk), lambda j, kk, order_ref: (0, kk)),
                  pl.BlockSpec((tk, n_blk), lambda j, kk, order_ref: (kk, order_ref[j]))] + [ANY] * (nt - 1),
        out_specs=[ANY] * nt,
        scratch_shapes=[pltpu.VMEM((D, n_blk), F32), pltpu.VMEM((2, D, n_blk), BF16), pltpu.VMEM((3, D, n_blk), BF16),
                        pltpu.SemaphoreType.DMA((nt, N_DEV - 1)), pltpu.SemaphoreType.DMA((nt, N_DEV - 1)),
                        pltpu.SemaphoreType.DMA((nt,)), pltpu.SemaphoreType.DMA((3,)), pltpu.SemaphoreType.DMA((3,))],
    )
    return pl.pallas_call(body, name="dw_in_scatter", grid_spec=grid_spec, out_shape=slot_shapes)(order, ht, dproj, *small_grads)


def _adamw_math(w, g, m, v):
    m = ADAM_B1 * m + (1.0 - ADAM_B1) * g
    v = ADAM_B2 * v + (1.0 - ADAM_B2) * (g * g)
    m_hat = m / (1.0 - ADAM_B1 ** ADAM_STEP)
    v_hat = v / (1.0 - ADAM_B2 ** ADAM_STEP)
    delta = -ADAM_LR * (m_hat / (jnp.sqrt(v_hat) + ADAM_EPS) + ADAM_WD * w)
    return delta, m, v


def _sum_adamw(slots, w, m, v, tr, name):
    n_slots, r, c = slots.shape
    assert r % tr == 0

    def body(s_ref, w_ref, m_ref, v_ref, g_ref, d_ref, nm_ref, nv_ref):
        g = s_ref[0].astype(F32)
        for j in range(1, n_slots):
            g = g + s_ref[j].astype(F32)
        delta, nm, nv = _adamw_math(w_ref[...], g, m_ref[...], v_ref[...])
        g_ref[...] = g
        d_ref[...] = delta
        nm_ref[...] = nm
        nv_ref[...] = nv

    blk = pl.BlockSpec((tr, c), lambda i: (i, 0))
    return pl.pallas_call(
        body, name=name, grid=(r // tr,),
        in_specs=[pl.BlockSpec((n_slots, tr, c), lambda i: (0, i, 0)), blk, blk, blk],
        out_specs=[blk] * 4, out_shape=[jax.ShapeDtypeStruct((r, c), F32)] * 4,
    )(slots, w, m, v)


def _adamw_small(g, w, m, v, name):
    def body(g_ref, w_ref, m_ref, v_ref, d_ref, nm_ref, nv_ref):
        delta, nm, nv = _adamw_math(w_ref[...], g_ref[...], m_ref[...], v_ref[...])
        d_ref[...] = delta
        nm_ref[...] = nm
        nv_ref[...] = nv

    spec = _full(g.shape)
    return pl.pallas_call(body, name=name, grid=(1,), in_specs=[spec] * 4, out_specs=[spec] * 3,
                          out_shape=[jax.ShapeDtypeStruct(g.shape, F32)] * 3)(g, w, m, v)


def _mod_part(c_all, w_ada_l, b_ada_l):
    n = w_ada_l.shape[1]

    def body(c_ref, w_ref, b_ref, o_ref):
        o_ref[...] = jnp.dot(c_ref[...], w_ref[...], preferred_element_type=F32,
                             precision=lax.Precision.HIGHEST) + b_ref[...]

    return pl.pallas_call(body, name="mod_part", grid=(1,),
                          in_specs=[_full(c_all.shape), _full(w_ada_l.shape), _full(b_ada_l.shape)],
                          out_specs=_full((N_DEV, n)), out_shape=jax.ShapeDtypeStruct((N_DEV, n), F32))(c_all, w_ada_l, b_ada_l)


def _w_ada_update(c_all_t, dmod_cols, w, m, v):
    def body(c_ref, dm_ref, w_ref, m_ref, v_ref, g_ref, d_ref, nm_ref, nv_ref):
        g = c_ref[:, 0:1] * dm_ref[0:1, :]
        for b in range(1, N_DEV):
            g = g + c_ref[:, b:b + 1] * dm_ref[b:b + 1, :]
        delta, nm, nv = _adamw_math(w_ref[...], g, m_ref[...], v_ref[...])
        g_ref[...] = g
        d_ref[...] = delta
        nm_ref[...] = nm
        nv_ref[...] = nv

    spec = _full(w.shape)
    return pl.pallas_call(body, name="w_ada_update", grid=(1,),
                          in_specs=[_full(c_all_t.shape), _full(dmod_cols.shape), spec, spec, spec],
                          out_specs=[spec] * 4, out_shape=[jax.ShapeDtypeStruct(w.shape, F32)] * 4)(c_all_t, dmod_cols, w, m, v)


def _cast_bf16(w, name):
    def body(w_ref, o_ref):
        o_ref[...] = w_ref[...].astype(BF16)

    return pl.pallas_call(body, name=name, grid=(1,), in_specs=[_full(w.shape)], out_specs=_full(w.shape),
                          out_shape=jax.ShapeDtypeStruct(w.shape, BF16))(w)


def _prenorm(x, mod, norm_g, tm):
    t = x.shape[0]

    def body(x_ref, mod_ref, g_ref, h_ref, ht_ref):
        xv = x_ref[...]
        r = lax.rsqrt(jnp.mean(xv * xv, axis=-1, keepdims=True) + EPS)
        h = (xv * r) * g_ref[...] * (1.0 + mod_ref[:, D:2 * D]) + mod_ref[:, 0:D]
        h_ref[...] = h.astype(BF16)
        ht_ref[...] = h.T.astype(BF16)

    return pl.pallas_call(body, name="prenorm", grid=(t // tm,),
                          in_specs=[_rows(tm, D), _full((1, 3 * D)), _full((1, D))],
                          out_specs=[_rows(tm, D), pl.BlockSpec((D, tm), lambda i: (0, i))],
                          out_shape=[jax.ShapeDtypeStruct((t, D), BF16), jax.ShapeDtypeStruct((D, t), BF16)])(x, mod, norm_g)


def _rope_apply(t, cos, s_lo, s_hi):
    return t * cos + pltpu.roll(t, 120, 1) * s_lo + pltpu.roll(t, 8, 1) * s_hi


def _shift_copies(sh, buf, c0):
    rows = buf.shape[0] - 8
    for s in range(1, 8):
        sh[s, 0:rows, :] = buf[s:s + rows, pl.ds(c0, 128)]


def _window64(buf, sh, c0, start):
    s = start % 8
    if s == 0:
        return buf[start:start + 64, pl.ds(c0, 128)]
    return sh[s, start - s:start - s + 64, :]


def _conv_taps(acc_init, w_ref, buf, sh, row0, c0, offset_of_tap):
    acc = acc_init
    for j in range(CONV_K):
        acc = acc + w_ref[j:j + 1, pl.ds(c0, 128)] * _window64(buf, sh, c0, row0 + offset_of_tap(j))
    return acc


def _conv_fwd(proj, conv_w, conv_b, ln_g, ln_b, tm):
    t = proj.shape[0]
    hb = tm // HALO

    def body(a_ref, b_ref, z_ref, ah_ref, bh_ref, w_ref, cb_ref, lg_ref, lb_ref, u1_ref, pc_ref, ubuf, sh):
        i = pl.program_id(0)
        u0h = ah_ref[...] * _sig(bh_ref[...])
        ubuf[0:HALO, :] = jnp.where(i > 0, u0h, 0.0)
        ubuf[HALO:HALO + tm, :] = a_ref[...] * _sig(b_ref[...])

        def col(ci, carry):
            c0 = pl.multiple_of(ci * 128, 128)
            _shift_copies(sh, ubuf, c0)
            for rc in range(tm // 64):
                init = jnp.zeros((64, 128), F32)
                acc = _conv_taps(init, w_ref, ubuf, sh, rc * 64, c0, lambda j: HALO - (CONV_K - 1) + j)
                u1_ref[rc * 64:(rc + 1) * 64, pl.ds(c0, 128)] = acc + cb_ref[:, pl.ds(c0, 128)]
            return carry

        lax.fori_loop(0, D // 128, col, 0)
        u1 = u1_ref[...]
        mu = jnp.mean(u1, axis=-1, keepdims=True)
        xc = u1 - mu
        var = jnp.mean(xc * xc, axis=-1, keepdims=True)
        u2 = xc * lax.rsqrt(var + EPS) * lg_ref[...] + lb_ref[...]
        z = z_ref[...]
        pc_ref[...] = (u2 * _sig(u2) * (z * _sig(z))).astype(BF16)

    halo = pl.BlockSpec((HALO, D), lambda i: (jnp.maximum(i * hb - 1, 0), 0))
    halo_b = pl.BlockSpec((HALO, D), lambda i: (jnp.maximum(i * hb - 1, 0), 1))
    return pl.pallas_call(
        body, name="conv_fwd", grid=(t // tm,),
        in_specs=[_rows(tm, D, 0), _rows(tm, D, 1), _rows(tm, D, 2), halo, halo_b,
                  _full((CONV_KP, D)), _full((1, D)), _full((1, D)), _full((1, D))],
        out_specs=[_rows(tm, D), _rows(tm, D)],
        out_shape=[jax.ShapeDtypeStruct((t, D), F32), jax.ShapeDtypeStruct((t, D), BF16)],
        scratch_shapes=[pltpu.VMEM((HALO + tm, D), F32), pltpu.VMEM((8, HALO + tm, 128), F32)],
    )(proj, proj, proj, proj, proj, conv_w, conv_b, ln_g, ln_b)


def _band_masks_t(has_prev):
    key = lax.broadcasted_iota(jnp.int32, (BLK, BLK), 0)
    qry = lax.broadcasted_iota(jnp.int32, (BLK, BLK), 1)
    return jnp.logical_and(key >= qry, has_prev), key <= qry


def _head_lanes(pair, hh):
    lane = lax.broadcasted_iota(jnp.int32, pair.shape, 1)
    return jnp.where((lane >= hh * HEAD) & (lane < (hh + 1) * HEAD), pair, jnp.zeros_like(pair))


def _pair_mask(has_prev):
    mask_p, mask_c = _band_masks_t(has_prev)
    both = jnp.concatenate([mask_p, mask_c], axis=0)
    return jnp.concatenate([both, both], axis=1)


def _query_pair(pair):
    return jnp.concatenate([_head_lanes(pair, 0), _head_lanes(pair, 1)], axis=0)


def _key_pair(ref, prev, cur):
    return jnp.concatenate([ref[pl.ds(prev, BLK), :], ref[pl.ds(cur, BLK), :]], axis=0)


def _own_head(both):
    return jnp.concatenate([both[0:HEAD, 0:BLK], both[HEAD:2 * HEAD, BLK:2 * BLK]], axis=0)


def _store_transposed(dst, base, src):
    for j in range(TILE // BLK):
        dst[base // BLK + j] = src[j * BLK:(j + 1) * BLK, :].T.astype(BF16)


class _Dilated:
    def __init__(self, dil):
        self.dil = dil
        self.per = TILE // dil
        self.nbr = self.per // BLK

    def spread(self, dst, base, src_ref, dtype):
        for r in range(self.dil):
            rows = src_ref[pl.ds(r, self.per, stride=self.dil), :] if self.dil > 1 else src_ref[...]
            dst[pl.ds(pl.multiple_of(base + r * self.per, BLK), self.per), :] = rows.astype(dtype)

    def gather(self, dst_ref, src, base):
        for r in range(self.dil):
            rows = src[pl.ds(pl.multiple_of(base + r * self.per, BLK), self.per), :]
            if self.dil > 1:
                dst_ref[pl.ds(r, self.per, stride=self.dil), :] = rows
            else:
                dst_ref[...] = rows

    def block_rows(self, b, i, cur, prv):
        n = b % self.nbr
        row = pl.multiple_of(b * BLK, BLK)
        has_prev = jnp.logical_or(n > 0, i > 0)
        prev = jnp.where(n > 0, cur + row - BLK, jnp.where(i > 0, prv + row + (self.nbr - 1) * BLK, cur + row))
        return row, pl.multiple_of(prev, BLK), has_prev


def _slots(i):
    return pl.multiple_of((i % 2) * TILE, TILE), pl.multiple_of(((i + 1) % 2) * TILE, TILE)


def _nt(a, b):
    return lax.dot_general(a, b, (((1,), (1,)), ((), ())), preferred_element_type=F32)


def _qkv_specs(gi, clamp_to=None):
    def spec(col0):
        def imap(hp, i):
            return (i if clamp_to is None else jnp.minimum(i, clamp_to), (col0 + gi * ATT) // 128 + hp)
        return pl.BlockSpec((TILE, 128), imap)
    return [spec(C_Q), spec(C_K), spec(C_V)]


def _att_fwd(proj, tables):
    t = proj.shape[0]

    def body(*refs):
        qkv_refs, (c_ref, lo_ref, hi_ref, att_ref, lse_ref, tmp, qd, od, ld), kv_scratch = refs[:9], refs[9:18], refs[18:]
        i = pl.program_id(1)
        cur, prv = _slots(i)
        cs, lo, hi = c_ref[...], lo_ref[...], hi_ref[...]
        for gi, dil in GROUPS:
            dl = _Dilated(dil)
            q_ref, k_ref, v_ref = qkv_refs[3 * gi:3 * gi + 3]
            kd, vt = kv_scratch[2 * gi:2 * gi + 2]
            tmp[...] = _rope_apply(q_ref[...], cs, lo, hi) * SM_SCALE
            dl.spread(qd, 0, tmp, BF16)
            tmp[...] = _rope_apply(k_ref[...], cs, lo, hi)
            dl.spread(kd, cur, tmp, BF16)
            dl.spread(tmp, 0, v_ref, F32)
            _store_transposed(vt, cur, tmp)

            def block(b, carry, dl=dl, kd=kd, vt=vt):
                row, prev, has_prev = dl.block_rows(b, i, cur, prv)
                s = jnp.where(_pair_mask(has_prev), _nt(_key_pair(kd, prev, cur + row), _query_pair(qd[pl.ds(row, BLK), :])), NEG_INF)
                mx = jnp.max(s, axis=0, keepdims=True)
                p = jnp.exp(s - mx)
                den = jnp.sum(p, axis=0, keepdims=True)
                v_t = jnp.concatenate([vt[prev // BLK], vt[(cur + row) // BLK]], axis=1)
                acc = jnp.dot(v_t, p.astype(BF16), preferred_element_type=F32) / den
                lse = mx + jnp.log(den)
                od[pl.ds(row, BLK), :] = _own_head(acc).T
                ld[pl.ds(row, BLK), :] = _own_head(jnp.broadcast_to(lse, (2 * HEAD, 2 * BLK))).T
                return carry

            lax.fori_loop(0, TILE // BLK, block, 0, unroll=True)
            if gi == 0:
                dl.gather(att_ref, od, 0)
                dl.gather(lse_ref, ld, 0)
            else:
                dl.gather(tmp, ld, 0)
                l_run, l_new = lse_ref[...], tmp[...]
                m = jnp.maximum(l_run, l_new)
                w_run, w_new = jnp.exp(l_run - m), jnp.exp(l_new - m)
                lse_ref[...] = m + jnp.log(w_run + w_new)
                ld[...] = w_new / (w_run + w_new)
                dl.gather(tmp, od, 0)
                share = ld[...]
                att_ref[...] = att_ref[...] + share * (tmp[...] - att_ref[...])

    tab = pl.BlockSpec((TILE, 128), lambda hp, i: (i, 0))
    out_spec = pl.BlockSpec((TILE, 128), lambda hp, i: (i, hp))
    kv_shapes = [pltpu.VMEM((2 * TILE, 128), BF16), pltpu.VMEM((2 * TILE // BLK, 128, BLK), BF16)] * len(GROUPS)
    return pl.pallas_call(
        body, name="att_fwd", grid=(ATT // 128, t // TILE),
        in_specs=[spec for gi, _ in GROUPS for spec in _qkv_specs(gi)] + [tab] * 3,
        out_specs=[out_spec] * 2, out_shape=[jax.ShapeDtypeStruct((t, ATT), F32)] * 2,
        scratch_shapes=[pltpu.VMEM((TILE, 128), F32), pltpu.VMEM((TILE, 128), BF16), pltpu.VMEM((TILE, 128), F32),
                        pltpu.VMEM((TILE, 128), F32)] + kv_shapes,
    )(*([proj] * (3 * len(GROUPS))), *tables)


def _att_bwd(proj, tables, datt, dsum, lse, gi, dil):
    t = proj.shape[0]
    nt = t // TILE
    dl = _Dilated(dil)

    def body(q_ref, k_ref, v_ref, c_ref, lo_ref, hi_ref, cl_ref, lol_ref, hil_ref, do_ref, ds_ref, lse_ref,
             dq_ref, dk_ref, dv_ref, tmp, qd, kd, vd, dod, dsd, lsd, dqd, dkd, dvd, kt):
        i = pl.program_id(1)
        cur, prv = _slots(i)

        @pl.when(i < nt)
        def _():
            cs, lo, hi = c_ref[...], lo_ref[...], hi_ref[...]
            tmp[...] = _rope_apply(q_ref[...], cs, lo, hi) * SM_SCALE
            dl.spread(qd, 0, tmp, BF16)
            tmp[...] = _rope_apply(k_ref[...], cs, lo, hi)
            dl.spread(kd, cur, tmp, BF16)
            dl.spread(dqd, 0, tmp, F32)
            _store_transposed(kt, cur, dqd)
            dl.spread(vd, cur, v_ref, BF16)
            dl.spread(dod, 0, do_ref, BF16)
            dl.spread(dsd, 0, ds_ref, F32)
            dl.spread(lsd, 0, lse_ref, F32)

            def block(b, carry):
                row, prev, has_prev = dl.block_rows(b, i, cur, prv)
                q_pair, do_pair = _query_pair(qd[pl.ds(row, BLK), :]), _query_pair(dod[pl.ds(row, BLK), :])
                k_pair, v_pair = _key_pair(kd, prev, cur + row), _key_pair(vd, prev, cur + row)
                ds_t, ls_t = dsd[pl.ds(row, BLK), :].T, lsd[pl.ds(row, BLK), :].T
                lse = jnp.concatenate([ls_t[0:1, :], ls_t[HEAD:HEAD + 1, :]], axis=1)
                dsm = jnp.concatenate([ds_t[0:1, :], ds_t[HEAD:HEAD + 1, :]], axis=1)
                p = jnp.exp(jnp.where(_pair_mask(has_prev), _nt(k_pair, q_pair), NEG_INF) - lse)
                ds = (p * (_nt(v_pair, do_pair) - dsm)).astype(BF16)
                k_t = jnp.concatenate([kt[prev // BLK], kt[(cur + row) // BLK]], axis=1)
                dqd[pl.ds(row, BLK), :] = _own_head(jnp.dot(k_t, ds, preferred_element_type=F32)).T * SM_SCALE
                dk = jnp.dot(ds, q_pair, preferred_element_type=F32)
                dv = jnp.dot(p.astype(BF16), do_pair, preferred_element_type=F32)
                dkd[pl.ds(cur + row, BLK), :] = dk[BLK:2 * BLK, :]
                dvd[pl.ds(cur + row, BLK), :] = dv[BLK:2 * BLK, :]
                dkd[pl.ds(prev, BLK), :] += dk[0:BLK, :]
                dvd[pl.ds(prev, BLK), :] += dv[0:BLK, :]
                return carry

            lax.fori_loop(0, TILE // BLK, block, 0, unroll=True)
            dl.gather(tmp, dqd, 0)
            dq_ref[...] = _rope_apply(tmp[...], cs, -lo, -hi).astype(BF16)

        @pl.when(i > 0)
        def _():
            dl.gather(tmp, dkd, prv)
            dk_ref[...] = _rope_apply(tmp[...], cl_ref[...], -lol_ref[...], -hil_ref[...]).astype(BF16)
            dl.gather(tmp, dvd, prv)
            dv_ref[...] = tmp[...].astype(BF16)

    now = lambda col: pl.BlockSpec((TILE, 128), lambda hp, i: (jnp.minimum(i, nt - 1), col(hp)))
    lag = lambda col: pl.BlockSpec((TILE, 128), lambda hp, i: (jnp.maximum(i - 1, 0), col(hp)))
    first, pair = (lambda hp: 0), (lambda hp: hp)
    return pl.pallas_call(
        body, name=f"att_bwd_g{gi}", grid=(ATT // 128, nt + 1),
        in_specs=_qkv_specs(gi, nt - 1) + [now(first)] * 3 + [lag(first)] * 3 + [now(pair)] * 3,
        out_specs=[now(pair), lag(pair), lag(pair)],
        out_shape=[jax.ShapeDtypeStruct((t, ATT), BF16)] * 3,
        scratch_shapes=[pltpu.VMEM((TILE, 128), F32), pltpu.VMEM((TILE, 128), BF16), pltpu.VMEM((2 * TILE, 128), BF16),
                        pltpu.VMEM((2 * TILE, 128), BF16), pltpu.VMEM((TILE, 128), BF16), pltpu.VMEM((TILE, 128), F32),
                        pltpu.VMEM((TILE, 128), F32), pltpu.VMEM((TILE, 128), F32), pltpu.VMEM((2 * TILE, 128), F32),
                        pltpu.VMEM((2 * TILE, 128), F32), pltpu.VMEM((2 * TILE // BLK, 128, BLK), BF16)],
    )(proj, proj, proj, *tables, *tables, datt, dsum, lse)


def _merge_head(pc, att, proj, x, mod, final_g, target, w_co, w_ao, w_o, u1, ln_g, ln_b, tm):
    t = x.shape[0]

    def body(pc_ref, att_ref, gc_ref, ga_ref, x_ref, mod_ref, fg_ref, tg_ref, wco_ref, wao_ref, wo_ref,
             zc_ref, u1_ref, lg_ref, lb_ref, za_ref,
             merged_ref, do_ref, dyc_ref, dya_ref, dout_ref, du1_ref, dzc_ref, datt_ref, ds_ref, dp_ref, pa_ref,
             sq_ref, gfg_ref, dgate_ref, dlg_ref, dlb_ref, dcb_ref):
        i = pl.program_id(0)
        att_v = att_ref[...]
        za = za_ref[...]
        sza = _sig(za)
        pa = (att_v * (za * sza)).astype(BF16)
        pa_ref[...] = pa
        yc = jnp.dot(pc_ref[...], wco_ref[...], preferred_element_type=F32)
        ya = jnp.dot(pa, wao_ref[...], preferred_element_type=F32)
        sc, sa = _sig(gc_ref[...]), _sig(ga_ref[...])
        merged = (sc * yc + sa * ya).astype(BF16)
        merged_ref[...] = merged
        ov = jnp.dot(merged, wo_ref[...], preferred_element_type=F32)
        gate = mod_ref[:, 2 * D:3 * D]
        out = x_ref[...] + gate * ov
        r = lax.rsqrt(jnp.mean(out * out, axis=-1, keepdims=True) + EPS)
        yn = out * r
        diff = yn * fg_ref[...] - tg_ref[...]
        dy = diff * (1.0 / D)
        gy = dy * fg_ref[...]
        dout = r * (gy - yn * jnp.mean(gy * yn, axis=-1, keepdims=True))
        dout_ref[...] = dout
        do = (dout * gate).astype(BF16)
        do_ref[...] = do
        _acc_rows(sq_ref, i, diff * diff)
        _acc_rows(gfg_ref, i, dy * yn)
        _acc_rows(dgate_ref, i, dout * ov)
        dm = _nt(do, wo_ref[...])
        dyc = (dm * sc).astype(BF16)
        dya = (dm * sa).astype(BF16)
        dyc_ref[...] = dyc
        dya_ref[...] = dya
        dp_ref[:, ATT:ATT + D] = (dm * yc * sc * (1.0 - sc)).astype(BF16)
        dp_ref[:, ATT + D:ATT + 2 * D] = (dm * ya * sa * (1.0 - sa)).astype(BF16)

        dpc = _nt(dyc, wco_ref[...])
        u1v = u1_ref[...]
        xc = u1v - jnp.mean(u1v, axis=-1, keepdims=True)
        rs = lax.rsqrt(jnp.mean(xc * xc, axis=-1, keepdims=True) + EPS)
        uhat = xc * rs
        u2 = uhat * lg_ref[...] + lb_ref[...]
        s2 = _sig(u2)
        zc = zc_ref[...]
        szc = _sig(zc)
        dzc_ref[...] = (dpc * (u2 * s2) * _dsilu(zc, szc)).astype(BF16)
        du2 = dpc * (zc * szc) * _dsilu(u2, s2)
        duhat = du2 * lg_ref[...]
        du1 = rs * (duhat - jnp.mean(duhat, axis=-1, keepdims=True) - uhat * jnp.mean(duhat * uhat, axis=-1, keepdims=True))
        du1_ref[...] = du1
        _acc_rows(dlg_ref, i, du2 * uhat)
        _acc_rows(dlb_ref, i, du2)
        _acc_rows(dcb_ref, i, du1)

        dpa = _nt(dya, wao_ref[...])
        datt = dpa * (za * sza)
        datt_ref[...] = datt
        dp_ref[:, 0:ATT] = (dpa * att_v * _dsilu(za, sza)).astype(BF16)
        prod = datt * att_v
        for hd in range(ATT // HEAD):
            sl = slice(hd * HEAD, (hd + 1) * HEAD)
            ds_ref[:, sl] = jnp.broadcast_to(jnp.sum(prod[:, sl], axis=-1, keepdims=True), (tm, HEAD))

    vec = _full((1, D))
    bf = lambda w: jax.ShapeDtypeStruct((t, w), BF16)
    f32 = lambda w: jax.ShapeDtypeStruct((t, w), F32)
    tail = ATT + 2 * D
    return pl.pallas_call(
        body, name="merge_head", grid=(t // tm,),
        in_specs=[_rows(tm, D), _rows(tm, ATT), _rows(tm, D, C_GC // D), _rows(tm, D, C_GA // D), _rows(tm, D),
                  _full((1, 3 * D)), vec, _rows(tm, D), _full((D, D)), _full((ATT, D)), _full((D, D)),
                  _rows(tm, D, C_ZC // D), _rows(tm, D), vec, vec, _rows(tm, ATT, C_ZA // ATT)],
        out_specs=[_rows(tm, D)] * 7 + [_rows(tm, ATT), _rows(tm, ATT), _rows(tm, tail, C_ZA // tail),
                                        _rows(tm, ATT)] + [vec] * 6,
        out_shape=[bf(D), bf(D), bf(D), bf(D), f32(D), f32(D), bf(D), f32(ATT), f32(ATT), bf(N_COL), bf(ATT)]
        + [jax.ShapeDtypeStruct((1, D), F32)] * 6,
    )(pc, att, proj, proj, x, mod, final_g, target, w_co, w_ao, w_o, proj, u1, ln_g, ln_b, proj)


def _acc_rows(ref, i, val):
    @pl.when(i == 0)
    def _():
        ref[...] = jnp.zeros_like(ref)

    ref[...] += jnp.sum(val, axis=0, keepdims=True)


def _conv_bwd_taps(du1, proj, conv_w, dzc, dqkv, dproj, tm):
    t = proj.shape[0]
    hb = tm // HALO
    last = t // HALO - 1

    def body(du_ref, duh_ref, a_ref, b_ref, ah_ref, bh_ref, w_ref, dzc_ref, *rest):
        qkv_refs, (dp_in, dp_ref, dw_ref, dbuf, ubuf, g0, shd, shu) = rest[:9], rest[9:]
        del dp_in
        dp_ref[:, C_ZC:C_ZC + D] = dzc_ref[...]
        for n, ref in enumerate(qkv_refs):
            dp_ref[:, C_Q + n * ATT:C_Q + (n + 1) * ATT] = ref[...]
        i = pl.program_id(0)
        a, sb = a_ref[...], _sig(b_ref[...])
        ubuf[0:HALO, :] = jnp.where(i > 0, ah_ref[...] * _sig(bh_ref[...]), 0.0)
        ubuf[HALO:HALO + tm, :] = a * sb
        dbuf[0:tm, :] = du_ref[...]
        dbuf[tm:tm + HALO, :] = jnp.where(i < pl.num_programs(0) - 1, duh_ref[...], 0.0)

        @pl.when(i == 0)
        def _():
            dw_ref[...] = jnp.zeros_like(dw_ref)

        def col(ci, carry):
            c0 = pl.multiple_of(ci * 128, 128)
            _shift_copies(shd, dbuf, c0)
            _shift_copies(shu, ubuf, c0)
            for rc in range(tm // 64):
                g0[rc * 64:(rc + 1) * 64, pl.ds(c0, 128)] = _conv_taps(
                    jnp.zeros((64, 128), F32), w_ref, dbuf, shd, rc * 64, c0, lambda j: CONV_K - 1 - j)
            for j in range(CONV_K):
                part = jnp.zeros((8, 128), F32)
                for rc in range(tm // 64):
                    off = rc * 64 + HALO - (CONV_K - 1) + j
                    prod = dbuf[rc * 64:(rc + 1) * 64, pl.ds(c0, 128)] * _window64(ubuf, shu, c0, off)
                    part = part + jnp.sum(prod.reshape(8, 8, 128), axis=0)
                dw_ref[j:j + 1, pl.ds(c0, 128)] += jnp.sum(part, axis=0, keepdims=True)
            return carry

        lax.fori_loop(0, D // 128, col, 0)
        du0 = g0[...]
        dp_ref[:, 0:D] = (du0 * sb).astype(BF16)
        dp_ref[:, D:2 * D] = (du0 * a * sb * (1.0 - sb)).astype(BF16)

    prev = lambda col: pl.BlockSpec((HALO, D), lambda i: (jnp.maximum(i * hb - 1, 0), col))
    nxt = pl.BlockSpec((HALO, D), lambda i: (jnp.minimum((i + 1) * hb, last), 0))
    return pl.pallas_call(
        body, name="conv_bwd_taps", grid=(t // tm,),
        in_specs=[_rows(tm, D), nxt, _rows(tm, D, 0), _rows(tm, D, 1), prev(0), prev(1), _full((CONV_KP, D)),
                  _rows(tm, D)] + [_rows(tm, ATT)] * 9 + [ANY],
        out_specs=[_rows(tm, C_ZA, 0), _full((CONV_KP, D))],
        out_shape=[jax.ShapeDtypeStruct((t, N_COL), BF16), jax.ShapeDtypeStruct((CONV_KP, D), F32)],
        scratch_shapes=[pltpu.VMEM((tm + HALO, D), F32), pltpu.VMEM((HALO + tm, D), F32), pltpu.VMEM((tm, D), F32),
                        pltpu.VMEM((8, HALO + tm, 128), F32), pltpu.VMEM((8, HALO + tm, 128), F32)],
        input_output_aliases={17: 0},
    )(du1, du1, proj, proj, proj, proj, conv_w, dzc, *dqkv, dproj)


def _dh_prenorm_bwd(dproj, w_in_blocks, x, dout, mod, norm_g, tm):
    t = x.shape[0]
    nk, _, tk = w_in_blocks.shape

    def body(dp_ref, w_ref, x_ref, dout_ref, mod_ref, g_ref, gx_ref, dshift_ref, dscale_ref, dg_ref, acc):
        i, kk = pl.program_id(0), pl.program_id(1)
        p = _nt(dp_ref[...], w_ref[...])

        @pl.when(kk == 0)
        def _():
            acc[...] = p

        @pl.when(kk > 0)
        def _():
            acc[...] += p

        @pl.when(kk == nk - 1)
        def _():
            xv, dhv = x_ref[...], acc[...]
            r = lax.rsqrt(jnp.mean(xv * xv, axis=-1, keepdims=True) + EPS)
            xn = xv * r
            one_scale = 1.0 + mod_ref[:, D:2 * D]
            dxn = dhv * (g_ref[...] * one_scale)
            gx_ref[...] = r * (dxn - xn * jnp.mean(dxn * xn, axis=-1, keepdims=True)) + dout_ref[...]
            _acc_rows(dshift_ref, i, dhv)
            _acc_rows(dscale_ref, i, dhv * xn * g_ref[...])
            _acc_rows(dg_ref, i, dhv * xn * one_scale)

    row = pl.BlockSpec((tm, D), lambda i, kk: (i, 0))
    vec = pl.BlockSpec((1, D), lambda i, kk: (0, 0))
    return pl.pallas_call(
        body, name="dh_prenorm_bwd", grid=(t // tm, nk),
        in_specs=[pl.BlockSpec((tm, tk), lambda i, kk: (i, kk)), pl.BlockSpec((None, D, tk), lambda i, kk: (kk, 0, 0)),
                  row, row, pl.BlockSpec((1, 3 * D), lambda i, kk: (0, 0)), vec],
        out_specs=[row, vec, vec, vec],
        out_shape=[jax.ShapeDtypeStruct((t, D), F32)] + [jax.ShapeDtypeStruct((1, D), F32)] * 3,
        scratch_shapes=[pltpu.VMEM((tm, D), F32)],
    )(dproj, w_in_blocks, x, dout, mod, norm_g)


def _sum_devices(gathered):
    w = gathered.shape[-1]

    def body(g_ref, o_ref):
        acc = g_ref[0]
        for j in range(1, N_DEV):
            acc = acc + g_ref[j]
        o_ref[...] = acc

    return pl.pallas_call(body, name="sum_devices", grid=(1,), in_specs=[_full(gathered.shape)], out_specs=_full((1, w)),
                          out_shape=jax.ShapeDtypeStruct((1, w), F32))(gathered)


def _rope_tables(positions):
    half = HEAD // 8
    t = positions.shape[-1]
    inv_freq = ROPE_THETA ** (-(jnp.arange(half, dtype=F32) * 2.0 / (2 * half)))
    ang = positions.reshape(t, 1).astype(F32) * inv_freq
    cos, sin = jnp.cos(ang), jnp.sin(ang)
    zeros = lambda n: jnp.zeros((t, n), F32)
    c64 = jnp.concatenate([cos, cos, jnp.ones((t, HEAD - 2 * half), F32)], axis=1)
    lo64 = jnp.concatenate([-sin, zeros(HEAD - half)], axis=1)
    hi64 = jnp.concatenate([zeros(half), sin, zeros(HEAD - 2 * half)], axis=1)
    return tuple(jnp.tile(a, (1, 2)) for a in (c64, lo64, hi64))


def kernel(x, c, positions, norm_g, w_ada, b_ada, w_in, conv_w, conv_b, conv_ln_g, conv_ln_b, w_conv_out, w_att_out, w_o, final_g, loss_target, m_norm_g, m_w_ada, m_b_ada, m_w_in, m_conv_w, m_conv_b, m_conv_ln_g, m_conv_ln_b, m_w_conv_out, m_w_att_out, m_w_o, m_final_g, v_norm_g, v_w_ada, v_b_ada, v_w_in, v_conv_w, v_conv_b, v_conv_ln_g, v_conv_ln_b, v_w_conv_out, v_w_att_out, v_w_o, v_final_g):
    me = 4 * lax.axis_index("x") + 2 * lax.axis_index("y") + lax.axis_index("c")
    x2, tgt = x[0], loss_target[0]
    t = x2.shape[0]
    te = 512 if t % 512 == 0 else 256
    tcv = 256
    tmh = 256
    tmm = 1024 if t % 1024 == 0 else 256
    n_ada = w_ada.shape[-1]

    pad_taps = lambda a: jnp.pad(a[0], ((0, CONV_KP - CONV_K), (0, 0)))
    shards = (_cast_bf16(w_in[0], "cast_w_in"), _cast_bf16(w_conv_out[0], "cast_w_conv_out"),
              _cast_bf16(w_att_out[0], "cast_w_att_out"), _cast_bf16(w_o[0], "cast_w_o"), pad_taps(conv_w))
    block_of = lambda relations: jnp.bitwise_xor(me, jnp.array(relations, jnp.int32))

    c_all = _allgather_small(c, "gather_c").reshape(N_DEV, D)
    b_ada_l = lax.dynamic_slice(b_ada, (0, me * n_ada), (1, n_ada))
    parts = _allgather_small(_mod_part(c_all, w_ada[0], b_ada_l), "gather_mod")
    mod = lax.dynamic_slice(parts, (0, me, 0), (N_DEV, 1, n_ada)).reshape(1, N_DEV * n_ada)

    h, ht = _prenorm(x2, mod, norm_g, te)
    proj, w_in_f, w_co_f, w_ao_f, w_o_f, conv_w_f = _proj_gather(h, shards, block_of(GATHER_ORDER), tmm)
    u1, pc = _conv_fwd(proj, conv_w_f, conv_b, conv_ln_g, conv_ln_b, tcv)
    tables = _rope_tables(positions)
    att, lse = _att_fwd(proj, tables)

    (merged, do, dyc, dya, dout, du1, dzc, datt, dsum, dproj, pa,
     sq_sum, g_final, d_gate, d_ln_g, d_ln_b, d_conv_b) = _merge_head(
        pc, att, proj, x2, mod, final_g.reshape(1, D), tgt, w_co_f, w_ao_f, w_o_f, u1, conv_ln_g, conv_ln_b, tmh)

    dw_o = _matmul(merged, do, ta=True, out_dtype=BF16, tm=D, tn=D, tk=512, name="dw_o")
    dw_co = _matmul(pc, dyc, ta=True, out_dtype=BF16, tm=D, tn=D, tk=512, name="dw_conv_out")
    dw_ao = _matmul(pa, dya, ta=True, out_dtype=BF16, tm=ATT, tn=D, tk=512, name="dw_att_out")
    dqs, dks, dvs = [], [], []
    for gi, dil in GROUPS:
        dq, dk, dv = _att_bwd(proj, tables, datt, dsum, lse, gi, dil)
        dqs.append(dq), dks.append(dk), dvs.append(dv)
    dproj, dconv_w = _conv_bwd_taps(du1, proj, conv_w_f, dzc, dqs + dks + dvs, dproj, tcv)
    grad_x, d_shift, d_scale, d_norm_g = _dh_prenorm_bwd(dproj, w_in_f, x2, dout, mod, norm_g, tmm)

    packed = jnp.concatenate([d_shift, d_scale, d_gate, d_norm_g, d_conv_b, d_ln_g, d_ln_b, g_final, sq_sum], axis=1)
    gathered = _allgather_small(packed, "gather_partials")
    total = _sum_devices(gathered)
    seg = lambda k, n=1: total[:, k * D:(k + n) * D]
    g_b_ada, g_norm_g, g_conv_b, g_ln_g, g_ln_b, g_final_g = seg(0, 3), seg(3), seg(4), seg(5), seg(6), seg(7)
    loss = (0.5 / D) * jnp.sum(seg(8))
    dmod_all = gathered[:, 0, 0:3 * D]
    dmod_cols = lax.dynamic_slice(dmod_all, (0, me * n_ada), (N_DEV, n_ada))
    g_w_ada, d_w_ada, nm_w_ada, nv_w_ada = _w_ada_update(c_all.T, dmod_cols, w_ada[0], m_w_ada[0], v_w_ada[0])

    small = {}
    for name, g, w, m, v in (("norm_g", g_norm_g, norm_g, m_norm_g, v_norm_g), ("b_ada", g_b_ada, b_ada, m_b_ada, v_b_ada),
                             ("conv_b", g_conv_b, conv_b, m_conv_b, v_conv_b), ("conv_ln_g", g_ln_g, conv_ln_g, m_conv_ln_g, v_conv_ln_g),
                             ("conv_ln_b", g_ln_b, conv_ln_b, m_conv_ln_b, v_conv_ln_b),
                             ("final_g", g_final_g, final_g.reshape(1, D), m_final_g.reshape(1, D), v_final_g.reshape(1, D))):
        small[name] = (g,) + tuple(_adamw_small(g, w, m, v, "adamw_" + name))

    slots = _dw_in_scatter(ht, dproj, (dw_co, dw_ao, dw_o, dconv_w), block_of(SCATTER_ORDER), 1024)
    big = {
        "w_in": _sum_adamw(slots[0], w_in[0], m_w_in[0], v_w_in[0], 256, "adamw_w_in"),
        "w_conv_out": _sum_adamw(slots[1], w_conv_out[0], m_w_conv_out[0], v_w_conv_out[0], 128, "adamw_w_conv_out"),
        "w_att_out": _sum_adamw(slots[2], w_att_out[0], m_w_att_out[0], v_w_att_out[0], 512, "adamw_w_att_out"),
        "w_o": _sum_adamw(slots[3], w_o[0], m_w_o[0], v_w_o[0], 128, "adamw_w_o"),
        "conv_w": [r[:CONV_K] for r in _sum_adamw(slots[4], pad_taps(conv_w), pad_taps(m_conv_w), pad_taps(v_conv_w), CONV_KP, "adamw_conv_w")],
    }
    big["w_ada"] = (g_w_ada, d_w_ada, nm_w_ada, nv_w_ada)

    order = ("norm_g", "w_ada", "b_ada", "w_in", "conv_w", "conv_b", "conv_ln_g", "conv_ln_b", "w_conv_out", "w_att_out", "w_o", "final_g")
    lead = lambda name, a: a.reshape(D) if name == "final_g" else (a[None] if name in big else a)
    result = {**small, **big}
    outs = [loss, grad_x[None]]
    for field in range(4):
        outs += [lead(name, result[name][field]) for name in order]
    return tuple(outs)
```

```python
import jax
import jax.numpy as jnp
from jax import lax
from jax.experimental import pallas as pl
from jax.experimental.pallas import tpu as pltpu

F32 = jnp.float32
BF16 = jnp.bfloat16

N_DEV = 8
D = 1024
N_COL = 10240
C_A, C_B, C_ZC, C_Q, C_K, C_V, C_ZA, C_GC, C_GA = 0, 1024, 2048, 3072, 4608, 6144, 7680, 8192, 9216
QKV = 1536
ATT = 512
HEAD = 64
BLK = 128
TILE = 2048
GROUPS = ((0, 1), (1, 4), (2, 16))
CONV_K = 31
CONV_KP = 32
HALO = 32
EPS = 1e-6
NEG_INF = -1e30
ROPE_THETA = 500000.0
SM_SCALE = HEAD ** -0.5

ADAM_LR, ADAM_B1, ADAM_B2, ADAM_EPS, ADAM_WD, ADAM_STEP = 0.001, 0.9, 0.999, 1e-08, 0.01, 10

MESH = pl.DeviceIdType.MESH
ANY = pl.BlockSpec(memory_space=pl.ANY)


def _sig(v):
    return 1.0 / (1.0 + jnp.exp(-v))


def _dsilu(v, s):
    return s * (1.0 + v * (1.0 - s))


def _full(shape):
    return pl.BlockSpec(shape, lambda *_: (0,) * len(shape))


def _rows(tm, width, col=0):
    return pl.BlockSpec((tm, width), lambda i: (i, col))


def _matmul(a, b, *, ta=False, tb=False, out_dtype=F32, tm, tn, tk, name):
    m, k = (a.shape[1], a.shape[0]) if ta else a.shape
    n = b.shape[0] if tb else b.shape[1]
    assert (b.shape[1] if tb else b.shape[0]) == k
    assert m % tm == 0 and n % tn == 0 and k % tk == 0
    nk = k // tk
    dims = (((0 if ta else 1,), (1 if tb else 0,)), ((), ()))
    use_scratch = out_dtype != F32 and nk > 1

    def body(a_ref, b_ref, o_ref, *scratch):
        p = lax.dot_general(a_ref[...], b_ref[...], dims, preferred_element_type=F32)
        if nk == 1:
            o_ref[...] = p.astype(out_dtype)
            return
        acc = scratch[0] if use_scratch else o_ref
        kk = pl.program_id(2)

        @pl.when(kk == 0)
        def _():
            acc[...] = p

        @pl.when(kk > 0)
        def _():
            acc[...] += p

        if use_scratch:
            @pl.when(kk == nk - 1)
            def _():
                o_ref[...] = acc[...].astype(out_dtype)

    a_spec = pl.BlockSpec((tk, tm), lambda i, j, kk: (kk, i)) if ta else pl.BlockSpec((tm, tk), lambda i, j, kk: (i, kk))
    b_spec = pl.BlockSpec((tn, tk), lambda i, j, kk: (j, kk)) if tb else pl.BlockSpec((tk, tn), lambda i, j, kk: (kk, j))
    return pl.pallas_call(
        body, name=name, grid=(m // tm, n // tn, nk),
        in_specs=[a_spec, b_spec],
        out_specs=pl.BlockSpec((tm, tn), lambda i, j, kk: (i, j)),
        out_shape=jax.ShapeDtypeStruct((m, n), out_dtype),
        scratch_shapes=[pltpu.VMEM((tm, tn), F32)] if use_scratch else [],
    )(a, b)


def _me_and_peers():
    x, y, c = lax.axis_index("x"), lax.axis_index("y"), lax.axis_index("c")
    me = 4 * x + 2 * y + c
    peers = []
    for k in range(1, N_DEV):
        px, py, pc = x ^ (k >> 2), y ^ ((k >> 1) & 1), c ^ (k & 1)
        peers.append(((px, py, pc), 4 * px + 2 * py + pc))
    return me, peers


def _allgather_small(v, name):
    r, c = v.shape

    def body(v_ref, out_ref, send_sems, recv_sems):
        me, peers = _me_and_peers()
        out_ref[me] = v_ref[...]
        copies = []
        for k, (dev, _) in enumerate(peers):
            cp = pltpu.make_async_remote_copy(src_ref=v_ref, dst_ref=out_ref.at[me], send_sem=send_sems.at[k],
                                              recv_sem=recv_sems.at[k], device_id=dev, device_id_type=MESH)
            cp.start()
            copies.append(cp)
        for k, (dev, idx) in enumerate(peers):
            pltpu.make_async_remote_copy(src_ref=v_ref, dst_ref=out_ref.at[idx], send_sem=send_sems.at[k],
                                         recv_sem=recv_sems.at[k], device_id=dev, device_id_type=MESH).wait_recv()
        for cp in copies:
            cp.wait_send()

    return pl.pallas_call(
        body, name=name,
        in_specs=[pl.BlockSpec(memory_space=pltpu.VMEM)],
        out_specs=pl.BlockSpec(memory_space=pltpu.VMEM),
        out_shape=jax.ShapeDtypeStruct((N_DEV, r, c), v.dtype),
        scratch_shapes=[pltpu.SemaphoreType.DMA((N_DEV - 1,)), pltpu.SemaphoreType.DMA((N_DEV - 1,))],
    )(v)


def _window(ref, kind, idx, size):
    if kind == "block":
        return ref.at[idx]
    start = pl.multiple_of(idx * size, size)
    if kind == "rows":
        return ref.at[pl.ds(start, size), :]
    return ref.at[:, pl.ds(start, size)]


_BIG = (("cols", N_COL // N_DEV), ("rows", D // N_DEV), ("cols", D // N_DEV), ("rows", D // N_DEV), ("cols", D // N_DEV))
_GATHERED = (("block", 1),) + _BIG[1:]


GATHER_ORDER = (0, 1, 2, 4, 3, 5, 6, 7)
W_IN_DIRECT = (1, 2, 4, 6)
SCATTER_ORDER = (7, 6, 5, 4, 3, 2, 1, 0)
W_IN_SLOT = {0: 0, 1: 1, 2: 2, 4: 3, 6: 4}


def _proj_gather(h, shards, order, tm):
    t = h.shape[0]
    nt = len(shards)
    n_blk = N_COL // N_DEV
    full_shapes = []
    for s, (kind, size) in zip(shards, _GATHERED):
        full_shapes.append(jax.ShapeDtypeStruct({"block": (N_DEV,) + s.shape, "rows": (s.shape[0] * N_DEV, s.shape[1]),
                                                 "cols": (s.shape[0], s.shape[1] * N_DEV)}[kind], s.dtype))
    last = (N_DEV - 1, t // tm - 1)

    def body(order_ref, h_ref, *refs):
        src, proj_ref, dst = refs[:nt], refs[nt], refs[nt + 1:2 * nt + 1]
        w_all, send_sems, recv_sems, local_sems, keep_sems = refs[2 * nt + 1:]
        j, i = pl.program_id(0), pl.program_id(1)
        me, peers = _me_and_peers()

        def landing(tn, idx):
            kind, size = _GATHERED[tn]
            return w_all.at[idx] if tn == 0 else _window(dst[tn], kind, idx, size)

        def local(tn):
            return pltpu.make_async_copy(src[tn], landing(tn, me), local_sems.at[tn])

        def remote(tn, k, block_of):
            dev, idx = peers[k - 1]
            return pltpu.make_async_remote_copy(src_ref=src[tn], dst_ref=landing(tn, me if block_of == "mine" else idx),
                                                send_sem=send_sems.at[tn, k - 1], recv_sem=recv_sems.at[tn, k - 1],
                                                device_id=dev, device_id_type=MESH)

        def forward(k):
            block = w_all.at[peers[k - 1][1]]
            return pltpu.make_async_remote_copy(src_ref=block, dst_ref=block, send_sem=send_sems.at[0, k], recv_sem=recv_sems.at[0, k],
                                                device_id=peers[0][0], device_id_type=MESH)

        def keep(step):
            blk = order_ref[step]
            return pltpu.make_async_copy(w_all.at[blk], dst[0].at[blk], keep_sems.at[step])

        @pl.when((j == 0) & (i == 0))
        def _():
            for tn in range(nt):
                local(tn).start()
                for k in GATHER_ORDER[1:]:
                    if tn > 0 or k in W_IN_DIRECT:
                        remote(tn, k, "mine").start()

        @pl.when(i == 0)
        def _():
            for step, k in enumerate(GATHER_ORDER):
                @pl.when(j == step)
                def _():
                    if k == 0:
                        local(0).wait()
                    else:
                        remote(0, k, "theirs").wait_recv()
                        if k in W_IN_DIRECT and k > 1:
                            forward(k).start()
                    keep(step).start()

        proj_ref[...] = jnp.dot(h_ref[...], w_all[order_ref[j]], preferred_element_type=F32)

        @pl.when((j == last[0]) & (i == last[1]))
        def _():
            for step in range(N_DEV):
                keep(step).wait()
            for tn in range(1, nt):
                local(tn).wait()
                for k in range(1, N_DEV):
                    remote(tn, k, "theirs").wait_recv()
            for tn in range(nt):
                for k in range(1, N_DEV):
                    if tn > 0 or k in W_IN_DIRECT:
                        remote(tn, k, "mine").wait_send()
                    else:
                        forward(k - 1).wait_send()

    grid_spec = pltpu.PrefetchScalarGridSpec(
        num_scalar_prefetch=1, grid=(N_DEV, t // tm),
        in_specs=[pl.BlockSpec((tm, D), lambda j, i, order_ref: (i, 0))] + [ANY] * nt,
        out_specs=[pl.BlockSpec((tm, n_blk), lambda j, i, order_ref: (i, order_ref[j]))] + [ANY] * nt,
        scratch_shapes=[pltpu.VMEM((N_DEV, D, n_blk), BF16), pltpu.SemaphoreType.DMA((nt, N_DEV - 1)),
                        pltpu.SemaphoreType.DMA((nt, N_DEV - 1)), pltpu.SemaphoreType.DMA((nt,)), pltpu.SemaphoreType.DMA((N_DEV,))],
    )
    return pl.pallas_call(
        body, name="proj_gather", grid_spec=grid_spec,
        out_shape=[jax.ShapeDtypeStruct((t, N_COL), F32)] + full_shapes,
    )(order, h, *shards)


def _dw_in_scatter(ht, dproj, small_grads, order, tk):
    t = ht.shape[1]
    nt = 1 + len(small_grads)
    n_blk = N_COL // N_DEV
    nk = t // tk
    slot_shapes = [jax.ShapeDtypeStruct((len(W_IN_SLOT), D, n_blk), BF16)]
    for g, (kind, size) in zip(small_grads, _BIG[1:]):
        slot_shapes.append(jax.ShapeDtypeStruct((N_DEV,) + ((size, g.shape[1]) if kind == "rows" else (g.shape[0], size)), g.dtype))

    def body(order_ref, h_ref, dp_ref, *refs):
        src, dst = refs[:nt - 1], refs[nt - 1:2 * nt - 1]
        acc, stage, partner, send_sems, recv_sems, local_sems, pair_send, pair_recv = refs[2 * nt - 1:]
        j, kk = pl.program_id(0), pl.program_id(1)
        me, peers = _me_and_peers()

        def small_local(tn):
            kind, size = _BIG[tn]
            return pltpu.make_async_copy(_window(src[tn - 1], kind, me, size), dst[tn].at[me], local_sems.at[tn])

        def small_remote(tn, k, mine):
            kind, size = _BIG[tn]
            dev, idx = peers[k - 1]
            return pltpu.make_async_remote_copy(src_ref=_window(src[tn - 1], kind, idx if mine else me, size),
                                                dst_ref=dst[tn].at[me if mine else idx],
                                                send_sem=send_sems.at[tn, k - 1], recv_sem=recv_sems.at[tn, k - 1],
                                                device_id=dev, device_id_type=MESH)

        def push(step):
            k, slot = SCATTER_ORDER[step], step % 2
            if k == 0:
                return pltpu.make_async_copy(stage.at[slot], dst[0].at[W_IN_SLOT[0]], local_sems.at[0])
            if k not in W_IN_SLOT:
                p = (k - 3) // 2
                return pltpu.make_async_remote_copy(src_ref=stage.at[slot], dst_ref=partner.at[p], send_sem=pair_send.at[p],
                                                    recv_sem=pair_recv.at[p], device_id=peers[0][0], device_id_type=MESH)
            return pltpu.make_async_remote_copy(src_ref=stage.at[slot], dst_ref=dst[0].at[W_IN_SLOT[k]],
                                                send_sem=send_sems.at[0, k - 1], recv_sem=recv_sems.at[0, k - 1],
                                                device_id=peers[k - 1][0], device_id_type=MESH)

        @pl.when((j == 0) & (kk == 0))
        def _():
            for tn in range(1, nt):
                small_local(tn).start()
                for k in range(1, N_DEV):
                    small_remote(tn, k, True).start()

        p = jnp.dot(h_ref[...], dp_ref[...], preferred_element_type=F32)

        @pl.when(kk == 0)
        def _():
            acc[...] = p

        @pl.when(kk > 0)
        def _():
            acc[...] += p

        @pl.when(kk == nk - 1)
        def _():
            for step, k in enumerate(SCATTER_ORDER):
                @pl.when(j == step)
                def _():
                    if step >= 2:
                        push(step - 2).wait_send()
                    total = acc[...]
                    if k in W_IN_SLOT and k >= 2:
                        p = k // 2 - 1
                        push(SCATTER_ORDER.index(k + 1)).wait_recv()
                        total = total + partner[p].astype(F32)
                    stage[step % 2] = total.astype(BF16)
                    push(step).start()

        @pl.when((j == N_DEV - 1) & (kk == nk - 1))
        def _():
            push(N_DEV - 2).wait_send()
            push(N_DEV - 1).wait()
            for k in (1, 2, 4, 6):
                push(SCATTER_ORDER.index(k)).wait_recv()
            for tn in range(1, nt):
                small_local(tn).wait()
                for k in range(1, N_DEV):
                    small_remote(tn, k, False).wait_recv()
                    small_remote(tn, k, True).wait_send()

    grid_spec = pltpu.PrefetchScalarGridSpec(
        num_scalar_prefetch=1, grid=(N_DEV, nk),
        in_specs=[pl.BlockSpec((D, tk), lambda j, kk, order_ref: (0, kk)),
                  pl.BlockSpec((tk, n_blk), lambda j, kk, order_ref: (kk, order_ref[j]))] + [ANY] * (nt - 1),
        out_specs=[ANY] * nt,
        scratch_shapes=[pltpu.VMEM((D, n_blk), F32), pltpu.VMEM((2, D, n_blk), BF16), pltpu.VMEM((3, D, n_blk), BF16),
                        pltpu.SemaphoreType.DMA((nt, N_DEV - 1)), pltpu.SemaphoreType.DMA((nt, N_DEV - 1)),
                        pltpu.SemaphoreType.DMA((nt,)), pltpu.SemaphoreType.DMA((3,)), pltpu.SemaphoreType.DMA((3,))],
    )
    return pl.pallas_call(body, name="dw_in_scatter", grid_spec=grid_spec, out_shape=slot_shapes)(order, ht, dproj, *small_grads)


def _adamw_math(w, g, m, v):
    m = ADAM_B1 * m + (1.0 - ADAM_B1) * g
    v = ADAM_B2 * v + (1.0 - ADAM_B2) * (g * g)
    m_hat = m / (1.0 - ADAM_B1 ** ADAM_STEP)
    v_hat = v / (1.0 - ADAM_B2 ** ADAM_STEP)
    delta = -ADAM_LR * (m_hat / (jnp.sqrt(v_hat) + ADAM_EPS) + ADAM_WD * w)
    return delta, m, v


def _sum_adamw(slots, w, m, v, tr, name):
    n_slots, r, c = slots.shape
    assert r % tr == 0

    def body(s_ref, w_ref, m_ref, v_ref, g_ref, d_ref, nm_ref, nv_ref):
        g = s_ref[0].astype(F32)
        for j in range(1, n_slots):
            g = g + s_ref[j].astype(F32)
        delta, nm, nv = _adamw_math(w_ref[...], g, m_ref[...], v_ref[...])
        g_ref[...] = g
        d_ref[...] = delta
        nm_ref[...] = nm
        nv_ref[...] = nv

    blk = pl.BlockSpec((tr, c), lambda i: (i, 0))
    return pl.pallas_call(
        body, name=name, grid=(r // tr,),
        in_specs=[pl.BlockSpec((n_slots, tr, c), lambda i: (0, i, 0)), blk, blk, blk],
        out_specs=[blk] * 4, out_shape=[jax.ShapeDtypeStruct((r, c), F32)] * 4,
    )(slots, w, m, v)


def _adamw_small(g, w, m, v, name):
    def body(g_ref, w_ref, m_ref, v_ref, d_ref, nm_ref, nv_ref):
        delta, nm, nv = _adamw_math(w_ref[...], g_ref[...], m_ref[...], v_ref[...])
        d_ref[...] = delta
        nm_ref[...] = nm
        nv_ref[...] = nv

    spec = _full(g.shape)
    return pl.pallas_call(body, name=name, grid=(1,), in_specs=[spec] * 4, out_specs=[spec] * 3,
                          out_shape=[jax.ShapeDtypeStruct(g.shape, F32)] * 3)(g, w, m, v)


def _mod_part(c_all, w_ada_l, b_ada_l):
    n = w_ada_l.shape[1]

    def body(c_ref, w_ref, b_ref, o_ref):
        o_ref[...] = jnp.dot(c_ref[...], w_ref[...], preferred_element_type=F32,
                             precision=lax.Precision.HIGHEST) + b_ref[...]

    return pl.pallas_call(body, name="mod_part", grid=(1,),
                          in_specs=[_full(c_all.shape), _full(w_ada_l.shape), _full(b_ada_l.shape)],
                          out_specs=_full((N_DEV, n)), out_shape=jax.ShapeDtypeStruct((N_DEV, n), F32))(c_all, w_ada_l, b_ada_l)


def _w_ada_update(c_all_t, dmod_cols, w, m, v):
    def body(c_ref, dm_ref, w_ref, m_ref, v_ref, g_ref, d_ref, nm_ref, nv_ref):
        g = c_ref[:, 0:1] * dm_ref[0:1, :]
        for b in range(1, N_DEV):
            g = g + c_ref[:, b:b + 1] * dm_ref[b:b + 1, :]
        delta, nm, nv = _adamw_math(w_ref[...], g, m_ref[...], v_ref[...])
        g_ref[...] = g
        d_ref[...] = delta
        nm_ref[...] = nm
        nv_ref[...] = nv

    spec = _full(w.shape)
    return pl.pallas_call(body, name="w_ada_update", grid=(1,),
                          in_specs=[_full(c_all_t.shape), _full(dmod_cols.shape), spec, spec, spec],
                          out_specs=[spec] * 4, out_shape=[jax.ShapeDtypeStruct(w.shape, F32)] * 4)(c_all_t, dmod_cols, w, m, v)


def _cast_bf16(w, name):
    def body(w_ref, o_ref):
        o_ref[...] = w_ref[...].astype(BF16)

    return pl.pallas_call(body, name=name, grid=(1,), in_specs=[_full(w.shape)], out_specs=_full(w.shape),
                          out_shape=jax.ShapeDtypeStruct(w.shape, BF16))(w)


def _prenorm(x, mod, norm_g, tm):
    t = x.shape[0]

    def body(x_ref, mod_ref, g_ref, h_ref, ht_ref):
        xv = x_ref[...]
        r = lax.rsqrt(jnp.mean(xv * xv, axis=-1, keepdims=True) + EPS)
        h = (xv * r) * g_ref[...] * (1.0 + mod_ref[:, D:2 * D]) + mod_ref[:, 0:D]
        h_ref[...] = h.astype(BF16)
        ht_ref[...] = h.T.astype(BF16)

    return pl.pallas_call(body, name="prenorm", grid=(t // tm,),
                          in_specs=[_rows(tm, D), _full((1, 3 * D)), _full((1, D))],
                          out_specs=[_rows(tm, D), pl.BlockSpec((D, tm), lambda i: (0, i))],
                          out_shape=[jax.ShapeDtypeStruct((t, D), BF16), jax.ShapeDtypeStruct((D, t), BF16)])(x, mod, norm_g)


def _rope_apply(t, cos, s_lo, s_hi):
    return t * cos + pltpu.roll(t, 120, 1) * s_lo + pltpu.roll(t, 8, 1) * s_hi


def _shift_copies(sh, buf, c0):
    rows = buf.shape[0] - 8
    for s in range(1, 8):
        sh[s, 0:rows, :] = buf[s:s + rows, pl.ds(c0, 128)]


def _window64(buf, sh, c0, start):
    s = start % 8
    if s == 0:
        return buf[start:start + 64, pl.ds(c0, 128)]
    return sh[s, start - s:start - s + 64, :]


def _conv_taps(acc_init, w_ref, buf, sh, row0, c0, offset_of_tap):
    acc = acc_init
    for j in range(CONV_K):
        acc = acc + w_ref[j:j + 1, pl.ds(c0, 128)] * _window64(buf, sh, c0, row0 + offset_of_tap(j))
    return acc


def _conv_fwd(proj, conv_w, conv_b, ln_g, ln_b, tm):
    t = proj.shape[0]
    hb = tm // HALO

    def body(a_ref, b_ref, z_ref, ah_ref, bh_ref, w_ref, cb_ref, lg_ref, lb_ref, u1_ref, pc_ref, ubuf, sh):
        i = pl.program_id(0)
        u0h = ah_ref[...] * _sig(bh_ref[...])
        ubuf[0:HALO, :] = jnp.where(i > 0, u0h, 0.0)
        ubuf[HALO:HALO + tm, :] = a_ref[...] * _sig(b_ref[...])

        def col(ci, carry):
            c0 = pl.multiple_of(ci * 128, 128)
            _shift_copies(sh, ubuf, c0)
            for rc in range(tm // 64):
                init = jnp.zeros((64, 128), F32)
                acc = _conv_taps(init, w_ref, ubuf, sh, rc * 64, c0, lambda j: HALO - (CONV_K - 1) + j)
                u1_ref[rc * 64:(rc + 1) * 64, pl.ds(c0, 128)] = acc + cb_ref[:, pl.ds(c0, 128)]
            return carry

        lax.fori_loop(0, D // 128, col, 0)
        u1 = u1_ref[...]
        mu = jnp.mean(u1, axis=-1, keepdims=True)
        xc = u1 - mu
        var = jnp.mean(xc * xc, axis=-1, keepdims=True)
        u2 = xc * lax.rsqrt(var + EPS) * lg_ref[...] + lb_ref[...]
        z = z_ref[...]
        pc_ref[...] = (u2 * _sig(u2) * (z * _sig(z))).astype(BF16)

    halo = pl.BlockSpec((HALO, D), lambda i: (jnp.maximum(i * hb - 1, 0), 0))
    halo_b = pl.BlockSpec((HALO, D), lambda i: (jnp.maximum(i * hb - 1, 0), 1))
    return pl.pallas_call(
        body, name="conv_fwd", grid=(t // tm,),
        in_specs=[_rows(tm, D, 0), _rows(tm, D, 1), _rows(tm, D, 2), halo, halo_b,
                  _full((CONV_KP, D)), _full((1, D)), _full((1, D)), _full((1, D))],
        out_specs=[_rows(tm, D), _rows(tm, D)],
        out_shape=[jax.ShapeDtypeStruct((t, D), F32), jax.ShapeDtypeStruct((t, D), BF16)],
        scratch_shapes=[pltpu.VMEM((HALO + tm, D), F32), pltpu.VMEM((8, HALO + tm, 128), F32)],
    )(proj, proj, proj, proj, proj, conv_w, conv_b, ln_g, ln_b)


def _band_masks_t(has_prev):
    key = lax.broadcasted_iota(jnp.int32, (BLK, BLK), 0)
    qry = lax.broadcasted_iota(jnp.int32, (BLK, BLK), 1)
    return jnp.logical_and(key >= qry, has_prev), key <= qry


def _head_lanes(pair, hh):
    lane = lax.broadcasted_iota(jnp.int32, pair.shape, 1)
    return jnp.where((lane >= hh * HEAD) & (lane < (hh + 1) * HEAD), pair, jnp.zeros_like(pair))


def _pair_mask(has_prev):
    mask_p, mask_c = _band_masks_t(has_prev)
    both = jnp.concatenate([mask_p, mask_c], axis=0)
    return jnp.concatenate([both, both], axis=1)


def _query_pair(pair):
    return jnp.concatenate([_head_lanes(pair, 0), _head_lanes(pair, 1)], axis=0)


def _key_pair(ref, prev, cur):
    return jnp.concatenate([ref[pl.ds(prev, BLK), :], ref[pl.ds(cur, BLK), :]], axis=0)


def _own_head(both):
    return jnp.concatenate([both[0:HEAD, 0:BLK], both[HEAD:2 * HEAD, BLK:2 * BLK]], axis=0)


def _store_transposed(dst, base, src):
    for j in range(TILE // BLK):
        dst[base // BLK + j] = src[j * BLK:(j + 1) * BLK, :].T.astype(BF16)


class _Dilated:
    def __init__(self, dil):
        self.dil = dil
        self.per = TILE // dil
        self.nbr = self.per // BLK

    def spread(self, dst, base, src_ref, dtype):
        for r in range(self.dil):
            rows = src_ref[pl.ds(r, self.per, stride=self.dil), :] if self.dil > 1 else src_ref[...]
            dst[pl.ds(pl.multiple_of(base + r * self.per, BLK), self.per), :] = rows.astype(dtype)

    def gather(self, dst_ref, src, base):
        for r in range(self.dil):
            rows = src[pl.ds(pl.multiple_of(base + r * self.per, BLK), self.per), :]
            if self.dil > 1:
                dst_ref[pl.ds(r, self.per, stride=self.dil), :] = rows
            else:
                dst_ref[...] = rows

    def block_rows(self, b, i, cur, prv):
        n = b % self.nbr
        row = pl.multiple_of(b * BLK, BLK)
        has_prev = jnp.logical_or(n > 0, i > 0)
        prev = jnp.where(n > 0, cur + row - BLK, jnp.where(i > 0, prv + row + (self.nbr - 1) * BLK, cur + row))
        return row, pl.multiple_of(prev, BLK), has_prev


def _slots(i):
    return pl.multiple_of((i % 2) * TILE, TILE), pl.multiple_of(((i + 1) % 2) * TILE, TILE)


def _nt(a, b):
    return lax.dot_general(a, b, (((1,), (1,)), ((), ())), preferred_element_type=F32)


def _qkv_specs(gi, clamp_to=None):
    def spec(col0):
        def imap(hp, i):
            return (i if clamp_to is None else jnp.minimum(i, clamp_to), (col0 + gi * ATT) // 128 + hp)
        return pl.BlockSpec((TILE, 128), imap)
    return [spec(C_Q), spec(C_K), spec(C_V)]


def _att_fwd(proj, tables):
    t = proj.shape[0]

    def body(*refs):
        qkv_refs, (c_ref, lo_ref, hi_ref, att_ref, lse_ref, tmp, qd, od, ld), kv_scratch = refs[:9], refs[9:18], refs[18:]
        i = pl.program_id(1)
        cur, prv = _slots(i)
        cs, lo, hi = c_ref[...], lo_ref[...], hi_ref[...]
        for gi, dil in GROUPS:
            dl = _Dilated(dil)
            q_ref, k_ref, v_ref = qkv_refs[3 * gi:3 * gi + 3]
            kd, vt = kv_scratch[2 * gi:2 * gi + 2]
            tmp[...] = _rope_apply(q_ref[...], cs, lo, hi) * SM_SCALE
            dl.spread(qd, 0, tmp, BF16)
            tmp[...] = _rope_apply(k_ref[...], cs, lo, hi)
            dl.spread(kd, cur, tmp, BF16)
            dl.spread(tmp, 0, v_ref, F32)
            _store_transposed(vt, cur, tmp)

            def block(b, carry, dl=dl, kd=kd, vt=vt):
                row, prev, has_prev = dl.block_rows(b, i, cur, prv)
                s = jnp.where(_pair_mask(has_prev), _nt(_key_pair(kd, prev, cur + row), _query_pair(qd[pl.ds(row, BLK), :])), NEG_INF)
                mx = jnp.max(s, axis=0, keepdims=True)
                p = jnp.exp(s - mx)
                den = jnp.sum(p, axis=0, keepdims=True)
                v_t = jnp.concatenate([vt[prev // BLK], vt[(cur + row) // BLK]], axis=1)
                acc = jnp.dot(v_t, p.astype(BF16), preferred_element_type=F32) / den
                lse = mx + jnp.log(den)
                od[pl.ds(row, BLK), :] = _own_head(acc).T
                ld[pl.ds(row, BLK), :] = _own_head(jnp.broadcast_to(lse, (2 * HEAD, 2 * BLK))).T
                return carry

            lax.fori_loop(0, TILE // BLK, block, 0, unroll=True)
            if gi == 0:
                dl.gather(att_ref, od, 0)
                dl.gather(lse_ref, ld, 0)
            else:
                dl.gather(tmp, ld, 0)
                l_run, l_new = lse_ref[...], tmp[...]
                m = jnp.maximum(l_run, l_new)
                w_run, w_new = jnp.exp(l_run - m), jnp.exp(l_new - m)
                lse_ref[...] = m + jnp.log(w_run + w_new)
                ld[...] = w_new / (w_run + w_new)
                dl.gather(tmp, od, 0)
                share = ld[...]
                att_ref[...] = att_ref[...] + share * (tmp[...] - att_ref[...])

    tab = pl.BlockSpec((TILE, 128), lambda hp, i: (i, 0))
    out_spec = pl.BlockSpec((TILE, 128), lambda hp, i: (i, hp))
    kv_shapes = [pltpu.VMEM((2 * TILE, 128), BF16), pltpu.VMEM((2 * TILE // BLK, 128, BLK), BF16)] * len(GROUPS)
    return pl.pallas_call(
        body, name="att_fwd", grid=(ATT // 128, t // TILE),
        in_specs=[spec for gi, _ in GROUPS for spec in _qkv_specs(gi)] + [tab] * 3,
        out_specs=[out_spec] * 2, out_shape=[jax.ShapeDtypeStruct((t, ATT), F32)] * 2,
        scratch_shapes=[pltpu.VMEM((TILE, 128), F32), pltpu.VMEM((TILE, 128), BF16), pltpu.VMEM((TILE, 128), F32),
                        pltpu.VMEM((TILE, 128), F32)] + kv_shapes,
    )(*([proj] * (3 * len(GROUPS))), *tables)


def _att_bwd(proj, tables, datt, dsum, lse, gi, dil):
    t = proj.shape[0]
    nt = t // TILE
    dl = _Dilated(dil)

    def body(q_ref, k_ref, v_ref, c_ref, lo_ref, hi_ref, cl_ref, lol_ref, hil_ref, do_ref, ds_ref, lse_ref,
             dq_ref, dk_ref, dv_ref, tmp, qd, kd, vd, dod, dsd, lsd, dqd, dkd, dvd, kt):
        i = pl.program_id(1)
        cur, prv = _slots(i)

        @pl.when(i < nt)
        def _():
            cs, lo, hi = c_ref[...], lo_ref[...], hi_ref[...]
            tmp[...] = _rope_apply(q_ref[...], cs, lo, hi) * SM_SCALE
            dl.spread(qd, 0, tmp, BF16)
            tmp[...] = _rope_apply(k_ref[...], cs, lo, hi)
            dl.spread(kd, cur, tmp, BF16)
            dl.spread(dqd, 0, tmp, F32)
            _store_transposed(kt, cur, dqd)
            dl.spread(vd, cur, v_ref, BF16)
            dl.spread(dod, 0, do_ref, BF16)
            dl.spread(dsd, 0, ds_ref, F32)
            dl.spread(lsd, 0, lse_ref, F32)

            def block(b, carry):
                row, prev, has_prev = dl.block_rows(b, i, cur, prv)
                q_pair, do_pair = _query_pair(qd[pl.ds(row, BLK), :]), _query_pair(dod[pl.ds(row, BLK), :])
                k_pair, v_pair = _key_pair(kd, prev, cur + row), _key_pair(vd, prev, cur + row)
                ds_t, ls_t = dsd[pl.ds(row, BLK), :].T, lsd[pl.ds(row, BLK), :].T
                lse = jnp.concatenate([ls_t[0:1, :], ls_t[HEAD:HEAD + 1, :]], axis=1)
                dsm = jnp.concatenate([ds_t[0:1, :], ds_t[HEAD:HEAD + 1, :]], axis=1)
                p = jnp.exp(jnp.where(_pair_mask(has_prev), _nt(k_pair, q_pair), NEG_INF) - lse)
                ds = (p * (_nt(v_pair, do_pair) - dsm)).astype(BF16)
                k_t = jnp.concatenate([kt[prev // BLK], kt[(cur + row) // BLK]], axis=1)
                dqd[pl.ds(row, BLK), :] = _own_head(jnp.dot(k_t, ds, preferred_element_type=F32)).T * SM_SCALE
                dk = jnp.dot(ds, q_pair, preferred_element_type=F32)
                dv = jnp.dot(p.astype(BF16), do_pair, preferred_element_type=F32)
                dkd[pl.ds(cur + row, BLK), :] = dk[BLK:2 * BLK, :]
                dvd[pl.ds(cur + row, BLK), :] = dv[BLK:2 * BLK, :]
                dkd[pl.ds(prev, BLK), :] += dk[0:BLK, :]
                dvd[pl.ds(prev, BLK), :] += dv[0:BLK, :]
                return carry

            lax.fori_loop(0, TILE // BLK, block, 0, unroll=True)
            dl.gather(tmp, dqd, 0)
            dq_ref[...] = _rope_apply(tmp[...], cs, -lo, -hi).astype(BF16)

        @pl.when(i > 0)
        def _():
            dl.gather(tmp, dkd, prv)
            dk_ref[...] = _rope_apply(tmp[...], cl_ref[...], -lol_ref[...], -hil_ref[...]).astype(BF16)
            dl.gather(tmp, dvd, prv)
            dv_ref[...] = tmp[...].astype(BF16)

    now = lambda col: pl.BlockSpec((TILE, 128), lambda hp, i: (jnp.minimum(i, nt - 1), col(hp)))
    lag = lambda col: pl.BlockSpec((TILE, 128), lambda hp, i: (jnp.maximum(i - 1, 0), col(hp)))
    first, pair = (lambda hp: 0), (lambda hp: hp)
    return pl.pallas_call(
        body, name=f"att_bwd_g{gi}", grid=(ATT // 128, nt + 1),
        in_specs=_qkv_specs(gi, nt - 1) + [now(first)] * 3 + [lag(first)] * 3 + [now(pair)] * 3,
        out_specs=[now(pair), lag(pair), lag(pair)],
        out_shape=[jax.ShapeDtypeStruct((t, ATT), BF16)] * 3,
        scratch_shapes=[pltpu.VMEM((TILE, 128), F32), pltpu.VMEM((TILE, 128), BF16), pltpu.VMEM((2 * TILE, 128), BF16),
                        pltpu.VMEM((2 * TILE, 128), BF16), pltpu.VMEM((TILE, 128), BF16), pltpu.VMEM((TILE, 128), F32),
                        pltpu.VMEM((TILE, 128), F32), pltpu.VMEM((TILE, 128), F32), pltpu.VMEM((2 * TILE, 128), F32),
                        pltpu.VMEM((2 * TILE, 128), F32), pltpu.VMEM((2 * TILE // BLK, 128, BLK), BF16)],
    )(proj, proj, proj, *tables, *tables, datt, dsum, lse)


def _merge_head(pc, att, proj, x, mod, final_g, target, w_co, w_ao, w_o, u1, ln_g, ln_b, tm):
    t = x.shape[0]

    def body(pc_ref, att_ref, gc_ref, ga_ref, x_ref, mod_ref, fg_ref, tg_ref, wco_ref, wao_ref, wo_ref,
             zc_ref, u1_ref, lg_ref, lb_ref, za_ref,
             merged_ref, do_ref, dyc_ref, dya_ref, dout_ref, du1_ref, dzc_ref, datt_ref, ds_ref, dp_ref, pa_ref,
             sq_ref, gfg_ref, dgate_ref, dlg_ref, dlb_ref, dcb_ref):
        i = pl.program_id(0)
        att_v = att_ref[...]
        za = za_ref[...]
        sza = _sig(za)
        pa = (att_v * (za * sza)).astype(BF16)
        pa_ref[...] = pa
        yc = jnp.dot(pc_ref[...], wco_ref[...], preferred_element_type=F32)
        ya = jnp.dot(pa, wao_ref[...], preferred_element_type=F32)
        sc, sa = _sig(gc_ref[...]), _sig(ga_ref[...])
        merged = (sc * yc + sa * ya).astype(BF16)
        merged_ref[...] = merged
        ov = jnp.dot(merged, wo_ref[...], preferred_element_type=F32)
        gate = mod_ref[:, 2 * D:3 * D]
        out = x_ref[...] + gate * ov
        r = lax.rsqrt(jnp.mean(out * out, axis=-1, keepdims=True) + EPS)
        yn = out * r
        diff = yn * fg_ref[...] - tg_ref[...]
        dy = diff * (1.0 / D)
        gy = dy * fg_ref[...]
        dout = r * (gy - yn * jnp.mean(gy * yn, axis=-1, keepdims=True))
        dout_ref[...] = dout
        do = (dout * gate).astype(BF16)
        do_ref[...] = do
        _acc_rows(sq_ref, i, diff * diff)
        _acc_rows(gfg_ref, i, dy * yn)
        _acc_rows(dgate_ref, i, dout * ov)
        dm = _nt(do, wo_ref[...])
        dyc = (dm * sc).astype(BF16)
        dya = (dm * sa).astype(BF16)
        dyc_ref[...] = dyc
        dya_ref[...] = dya
        dp_ref[:, ATT:ATT + D] = (dm * yc * sc * (1.0 - sc)).astype(BF16)
        dp_ref[:, ATT + D:ATT + 2 * D] = (dm * ya * sa * (1.0 - sa)).astype(BF16)

        dpc = _nt(dyc, wco_ref[...])
        u1v = u1_ref[...]
        xc = u1v - jnp.mean(u1v, axis=-1, keepdims=True)
        rs = lax.rsqrt(jnp.mean(xc * xc, axis=-1, keepdims=True) + EPS)
        uhat = xc * rs
        u2 = uhat * lg_ref[...] + lb_ref[...]
        s2 = _sig(u2)
        zc = zc_ref[...]
        szc = _sig(zc)
        dzc_ref[...] = (dpc * (u2 * s2) * _dsilu(zc, szc)).astype(BF16)
        du2 = dpc * (zc * szc) * _dsilu(u2, s2)
        duhat = du2 * lg_ref[...]
        du1 = rs * (duhat - jnp.mean(duhat, axis=-1, keepdims=True) - uhat * jnp.mean(duhat * uhat, axis=-1, keepdims=True))
        du1_ref[...] = du1
        _acc_rows(dlg_ref, i, du2 * uhat)
        _acc_rows(dlb_ref, i, du2)
        _acc_rows(dcb_ref, i, du1)

        dpa = _nt(dya, wao_ref[...])
        datt = dpa * (za * sza)
        datt_ref[...] = datt
        dp_ref[:, 0:ATT] = (dpa * att_v * _dsilu(za, sza)).astype(BF16)
        prod = datt * att_v
        for hd in range(ATT // HEAD):
            sl = slice(hd * HEAD, (hd + 1) * HEAD)
            ds_ref[:, sl] = jnp.broadcast_to(jnp.sum(prod[:, sl], axis=-1, keepdims=True), (tm, HEAD))

    vec = _full((1, D))
    bf = lambda w: jax.ShapeDtypeStruct((t, w), BF16)
    f32 = lambda w: jax.ShapeDtypeStruct((t, w), F32)
    tail = ATT + 2 * D
    return pl.pallas_call(
        body, name="merge_head", grid=(t // tm,),
        in_specs=[_rows(tm, D), _rows(tm, ATT), _rows(tm, D, C_GC // D), _rows(tm, D, C_GA // D), _rows(tm, D),
                  _full((1, 3 * D)), vec, _rows(tm, D), _full((D, D)), _full((ATT, D)), _full((D, D)),
                  _rows(tm, D, C_ZC // D), _rows(tm, D), vec, vec, _rows(tm, ATT, C_ZA // ATT)],
        out_specs=[_rows(tm, D)] * 7 + [_rows(tm, ATT), _rows(tm, ATT), _rows(tm, tail, C_ZA // tail),
                                        _rows(tm, ATT)] + [vec] * 6,
        out_shape=[bf(D), bf(D), bf(D), bf(D), f32(D), f32(D), bf(D), f32(ATT), f32(ATT), bf(N_COL), bf(ATT)]
        + [jax.ShapeDtypeStruct((1, D), F32)] * 6,
    )(pc, att, proj, proj, x, mod, final_g, target, w_co, w_ao, w_o, proj, u1, ln_g, ln_b, proj)


def _acc_rows(ref, i, val):
    @pl.when(i == 0)
    def _():
        ref[...] = jnp.zeros_like(ref)

    ref[...] += jnp.sum(val, axis=0, keepdims=True)


def _conv_bwd_taps(du1, proj, conv_w, dzc, dqkv, dproj, tm):
    t = proj.shape[0]
    hb = tm // HALO
    last = t // HALO - 1

    def body(du_ref, duh_ref, a_ref, b_ref, ah_ref, bh_ref, w_ref, dzc_ref, *rest):
        qkv_refs, (dp_in, dp_ref, dw_ref, dbuf, ubuf, g0, shd, shu) = rest[:9], rest[9:]
        del dp_in
        dp_ref[:, C_ZC:C_ZC + D] = dzc_ref[...]
        for n, ref in enumerate(qkv_refs):
            dp_ref[:, C_Q + n * ATT:C_Q + (n + 1) * ATT] = ref[...]
        i = pl.program_id(0)
        a, sb = a_ref[...], _sig(b_ref[...])
        ubuf[0:HALO, :] = jnp.where(i > 0, ah_ref[...] * _sig(bh_ref[...]), 0.0)
        ubuf[HALO:HALO + tm, :] = a * sb
        dbuf[0:tm, :] = du_ref[...]
        dbuf[tm:tm + HALO, :] = jnp.where(i < pl.num_programs(0) - 1, duh_ref[...], 0.0)

        @pl.when(i == 0)
        def _():
            dw_ref[...] = jnp.zeros_like(dw_ref)

        def col(ci, carry):
            c0 = pl.multiple_of(ci * 128, 128)
            _shift_copies(shd, dbuf, c0)
            _shift_copies(shu, ubuf, c0)
            for rc in range(tm // 64):
                g0[rc * 64:(rc + 1) * 64, pl.ds(c0, 128)] = _conv_taps(
                    jnp.zeros((64, 128), F32), w_ref, dbuf, shd, rc * 64, c0, lambda j: CONV_K - 1 - j)
            for j in range(CONV_K):
                part = jnp.zeros((8, 128), F32)
                for rc in range(tm // 64):
                    off = rc * 64 + HALO - (CONV_K - 1) + j
                    prod = dbuf[rc * 64:(rc + 1) * 64, pl.ds(c0, 128)] * _window64(ubuf, shu, c0, off)
                    part = part + jnp.sum(prod.reshape(8, 8, 128), axis=0)
                dw_ref[j:j + 1, pl.ds(c0, 128)] += jnp.sum(part, axis=0, keepdims=True)
            return carry

        lax.fori_loop(0, D // 128, col, 0)
        du0 = g0[...]
        dp_ref[:, 0:D] = (du0 * sb).astype(BF16)
        dp_ref[:, D:2 * D] = (du0 * a * sb * (1.0 - sb)).astype(BF16)

    prev = lambda col: pl.BlockSpec((HALO, D), lambda i: (jnp.maximum(i * hb - 1, 0), col))
    nxt = pl.BlockSpec((HALO, D), lambda i: (jnp.minimum((i + 1) * hb, last), 0))
    return pl.pallas_call(
        body, name="conv_bwd_taps", grid=(t // tm,),
        in_specs=[_rows(tm, D), nxt, _rows(tm, D, 0), _rows(tm, D, 1), prev(0), prev(1), _full((CONV_KP, D)),
                  _rows(tm, D)] + [_rows(tm, ATT)] * 9 + [ANY],
        out_specs=[_rows(tm, C_ZA, 0), _full((CONV_KP, D))],
        out_shape=[jax.ShapeDtypeStruct((t, N_COL), BF16), jax.ShapeDtypeStruct((CONV_KP, D), F32)],
        scratch_shapes=[pltpu.VMEM((tm + HALO, D), F32), pltpu.VMEM((HALO + tm, D), F32), pltpu.VMEM((tm, D), F32),
                        pltpu.VMEM((8, HALO + tm, 128), F32), pltpu.VMEM((8, HALO + tm, 128), F32)],
        input_output_aliases={17: 0},
    )(du1, du1, proj, proj, proj, proj, conv_w, dzc, *dqkv, dproj)


def _dh_prenorm_bwd(dproj, w_in_blocks, x, dout, mod, norm_g, tm):
    t = x.shape[0]
    nk, _, tk = w_in_blocks.shape

    def body(dp_ref, w_ref, x_ref, dout_ref, mod_ref, g_ref, gx_ref, dshift_ref, dscale_ref, dg_ref, acc):
        i, kk = pl.program_id(0), pl.program_id(1)
        p = _nt(dp_ref[...], w_ref[...])

        @pl.when(kk == 0)
        def _():
            acc[...] = p

        @pl.when(kk > 0)
        def _():
            acc[...] += p

        @pl.when(kk == nk - 1)
        def _():
            xv, dhv = x_ref[...], acc[...]
            r = lax.rsqrt(jnp.mean(xv * xv, axis=-1, keepdims=True) + EPS)
            xn = xv * r
            one_scale = 1.0 + mod_ref[:, D:2 * D]
            dxn = dhv * (g_ref[...] * one_scale)
            gx_ref[...] = r * (dxn - xn * jnp.mean(dxn * xn, axis=-1, keepdims=True)) + dout_ref[...]
            _acc_rows(dshift_ref, i, dhv)
            _acc_rows(dscale_ref, i, dhv * xn * g_ref[...])
            _acc_rows(dg_ref, i, dhv * xn * one_scale)

    row = pl.BlockSpec((tm, D), lambda i, kk: (i, 0))
    vec = pl.BlockSpec((1, D), lambda i, kk: (0, 0))
    return pl.pallas_call(
        body, name="dh_prenorm_bwd", grid=(t // tm, nk),
        in_specs=[pl.BlockSpec((tm, tk), lambda i, kk: (i, kk)), pl.BlockSpec((None, D, tk), lambda i, kk: (kk, 0, 0)),
                  row, row, pl.BlockSpec((1, 3 * D), lambda i, kk: (0, 0)), vec],
        out_specs=[row, vec, vec, vec],
        out_shape=[jax.ShapeDtypeStruct((t, D), F32)] + [jax.ShapeDtypeStruct((1, D), F32)] * 3,
        scratch_shapes=[pltpu.VMEM((tm, D), F32)],
    )(dproj, w_in_blocks, x, dout, mod, norm_g)


def _sum_devices(gathered):
    w = gathered.shape[-1]

    def body(g_ref, o_ref):
        acc = g_ref[0]
        for j in range(1, N_DEV):
            acc = acc + g_ref[j]
        o_ref[...] = acc

    return pl.pallas_call(body, name="sum_devices", grid=(1,), in_specs=[_full(gathered.shape)], out_specs=_full((1, w)),
                          out_shape=jax.ShapeDtypeStruct((1, w), F32))(gathered)


def _rope_tables(positions):
    half = HEAD // 8
    t = positions.shape[-1]
    inv_freq = ROPE_THETA ** (-(jnp.arange(half, dtype=F32) * 2.0 / (2 * half)))
    ang = positions.reshape(t, 1).astype(F32) * inv_freq
    cos, sin = jnp.cos(ang), jnp.sin(ang)
    zeros = lambda n: jnp.zeros((t, n), F32)
    c64 = jnp.concatenate([cos, cos, jnp.ones((t, HEAD - 2 * half), F32)], axis=1)
    lo64 = jnp.concatenate([-sin, zeros(HEAD - half)], axis=1)
    hi64 = jnp.concatenate([zeros(half), sin, zeros(HEAD - 2 * half)], axis=1)
    return tuple(jnp.tile(a, (1, 2)) for a in (c64, lo64, hi64))


def kernel(x, c, positions, norm_g, w_ada, b_ada, w_in, conv_w, conv_b, conv_ln_g, conv_ln_b, w_conv_out, w_att_out, w_o, final_g, loss_target, m_norm_g, m_w_ada, m_b_ada, m_w_in, m_conv_w, m_conv_b, m_conv_ln_g, m_conv_ln_b, m_w_conv_out, m_w_att_out, m_w_o, m_final_g, v_norm_g, v_w_ada, v_b_ada, v_w_in, v_conv_w, v_conv_b, v_conv_ln_g, v_conv_ln_b, v_w_conv_out, v_w_att_out, v_w_o, v_final_g):
    me = 4 * lax.axis_index("x") + 2 * lax.axis_index("y") + lax.axis_index("c")
    x2, tgt = x[0], loss_target[0]
    t = x2.shape[0]
    te = 512 if t % 512 == 0 else 256
    tcv = 256
    tmh = 256
    tmm = 1024 if t % 1024 == 0 else 256
    n_ada = w_ada.shape[-1]

    pad_taps = lambda a: jnp.pad(a[0], ((0, CONV_KP - CONV_K), (0, 0)))
    shards = (_cast_bf16(w_in[0], "cast_w_in"), _cast_bf16(w_conv_out[0], "cast_w_conv_out"),
              _cast_bf16(w_att_out[0], "cast_w_att_out"), _cast_bf16(w_o[0], "cast_w_o"), pad_taps(conv_w))
    block_of = lambda relations: jnp.bitwise_xor(me, jnp.array(relations, jnp.int32))

    c_all = _allgather_small(c, "gather_c").reshape(N_DEV, D)
    b_ada_l = lax.dynamic_slice(b_ada, (0, me * n_ada), (1, n_ada))
    parts = _allgather_small(_mod_part(c_all, w_ada[0], b_ada_l), "gather_mod")
    mod = lax.dynamic_slice(parts, (0, me, 0), (N_DEV, 1, n_ada)).reshape(1, N_DEV * n_ada)

    h, ht = _prenorm(x2, mod, norm_g, te)
    proj, w_in_f, w_co_f, w_ao_f, w_o_f, conv_w_f = _proj_gather(h, shards, block_of(GATHER_ORDER), tmm)
    u1, pc = _conv_fwd(proj, conv_w_f, conv_b, conv_ln_g, conv_ln_b, tcv)
    tables = _rope_tables(positions)
    att, lse = _att_fwd(proj, tables)

    (merged, do, dyc, dya, dout, du1, dzc, datt, dsum, dproj, pa,
     sq_sum, g_final, d_gate, d_ln_g, d_ln_b, d_conv_b) = _merge_head(
        pc, att, proj, x2, mod, final_g.reshape(1, D), tgt, w_co_f, w_ao_f, w_o_f, u1, conv_ln_g, conv_ln_b, tmh)

    tkw = 2048 if t % 2048 == 0 else 256
    dw_o = _matmul(merged, do, ta=True, out_dtype=BF16, tm=D, tn=D, tk=tkw, name="dw_o")
    dw_co = _matmul(pc, dyc, ta=True, out_dtype=BF16, tm=D, tn=D, tk=tkw, name="dw_conv_out")
    dw_ao = _matmul(pa, dya, ta=True, out_dtype=BF16, tm=ATT, tn=D, tk=tkw, name="dw_att_out")
    dqs, dks, dvs = [], [], []
    for gi, dil in GROUPS:
        dq, dk, dv = _att_bwd(proj, tables, datt, dsum, lse, gi, dil)
        dqs.append(dq), dks.append(dk), dvs.append(dv)
    dproj, dconv_w = _conv_bwd_taps(du1, proj, conv_w_f, dzc, dqs + dks + dvs, dproj, tcv)
    grad_x, d_shift, d_scale, d_norm_g = _dh_prenorm_bwd(dproj, w_in_f, x2, dout, mod, norm_g, tmm)

    packed = jnp.concatenate([d_shift, d_scale, d_gate, d_norm_g, d_conv_b, d_ln_g, d_ln_b, g_final, sq_sum], axis=1)
    gathered = _allgather_small(packed, "gather_partials")
    total = _sum_devices(gathered)
    seg = lambda k, n=1: total[:, k * D:(k + n) * D]
    g_b_ada, g_norm_g, g_conv_b, g_ln_g, g_ln_b, g_final_g = seg(0, 3), seg(3), seg(4), seg(5), seg(6), seg(7)
    loss = (0.5 / D) * jnp.sum(seg(8))
    dmod_all = gathered[:, 0, 0:3 * D]
    dmod_cols = lax.dynamic_slice(dmod_all, (0, me * n_ada), (N_DEV, n_ada))
    g_w_ada, d_w_ada, nm_w_ada, nv_w_ada = _w_ada_update(c_all.T, dmod_cols, w_ada[0], m_w_ada[0], v_w_ada[0])

    small = {}
    for name, g, w, m, v in (("norm_g", g_norm_g, norm_g, m_norm_g, v_norm_g), ("b_ada", g_b_ada, b_ada, m_b_ada, v_b_ada),
                             ("conv_b", g_conv_b, conv_b, m_conv_b, v_conv_b), ("conv_ln_g", g_ln_g, conv_ln_g, m_conv_ln_g, v_conv_ln_g),
                             ("conv_ln_b", g_ln_b, conv_ln_b, m_conv_ln_b, v_conv_ln_b),
                             ("final_g", g_final_g, final_g.reshape(1, D), m_final_g.reshape(1, D), v_final_g.reshape(1, D))):
        small[name] = (g,) + tuple(_adamw_small(g, w, m, v, "adamw_" + name))

    slots = _dw_in_scatter(ht, dproj, (dw_co, dw_ao, dw_o, dconv_w), block_of(SCATTER_ORDER), tkw)
    big = {
        "w_in": _sum_adamw(slots[0], w_in[0], m_w_in[0], v_w_in[0], 256, "adamw_w_in"),
        "w_conv_out": _sum_adamw(slots[1], w_conv_out[0], m_w_conv_out[0], v_w_conv_out[0], 128, "adamw_w_conv_out"),
        "w_att_out": _sum_adamw(slots[2], w_att_out[0], m_w_att_out[0], v_w_att_out[0], 512, "adamw_w_att_out"),
        "w_o": _sum_adamw(slots[3], w_o[0], m_w_o[0], v_w_o[0], 128, "adamw_w_o"),
        "conv_w": [r[:CONV_K] for r in _sum_adamw(slots[4], pad_taps(conv_w), pad_taps(m_conv_w), pad_taps(v_conv_w), CONV_KP, "adamw_conv_w")],
    }
    big["w_ada"] = (g_w_ada, d_w_ada, nm_w_ada, nv_w_ada)

    order = ("norm_g", "w_ada", "b_ada", "w_in", "conv_w", "conv_b", "conv_ln_g", "conv_ln_b", "w_conv_out", "w_att_out", "w_o", "final_g")
    lead = lambda name, a: a.reshape(D) if name == "final_g" else (a[None] if name in big else a)
    result = {**small, **big}
    outs = [loss, grad_x[None]]
    for field in range(4):
        outs += [lead(name, result[name][field]) for name in order]
    return tuple(outs)
```

```python
import jax
import jax.numpy as jnp
from jax import lax
from jax.experimental import pallas as pl
from jax.experimental.pallas import tpu as pltpu

F32 = jnp.float32
BF16 = jnp.bfloat16

N_DEV = 8
D = 1024
N_COL = 10240
C_A, C_B, C_ZC, C_Q, C_K, C_V, C_ZA, C_GC, C_GA = 0, 1024, 2048, 3072, 4608, 6144, 7680, 8192, 9216
QKV = 1536
ATT = 512
HEAD = 64
BLK = 128
TILE = 2048
GROUPS = ((0, 1), (1, 4), (2, 16))
CONV_K = 31
CONV_KP = 32
HALO = 32
EPS = 1e-6
NEG_INF = -1e30
ROPE_THETA = 500000.0
SM_SCALE = HEAD ** -0.5

ADAM_LR, ADAM_B1, ADAM_B2, ADAM_EPS, ADAM_WD, ADAM_STEP = 0.001, 0.9, 0.999, 1e-08, 0.01, 10

MESH = pl.DeviceIdType.MESH
ANY = pl.BlockSpec(memory_space=pl.ANY)


def _sig(v):
    return 1.0 / (1.0 + jnp.exp(-v))


def _dsilu(v, s):
    return s * (1.0 + v * (1.0 - s))


def _full(shape):
    return pl.BlockSpec(shape, lambda *_: (0,) * len(shape))


def _rows(tm, width, col=0):
    return pl.BlockSpec((tm, width), lambda i: (i, col))


def _matmul(a, b, *, ta=False, tb=False, out_dtype=F32, tm, tn, tk, name):
    m, k = (a.shape[1], a.shape[0]) if ta else a.shape
    n = b.shape[0] if tb else b.shape[1]
    assert (b.shape[1] if tb else b.shape[0]) == k
    assert m % tm == 0 and n % tn == 0 and k % tk == 0
    nk = k // tk
    dims = (((0 if ta else 1,), (1 if tb else 0,)), ((), ()))
    use_scratch = out_dtype != F32 and nk > 1

    def body(a_ref, b_ref, o_ref, *scratch):
        p = lax.dot_general(a_ref[...], b_ref[...], dims, preferred_element_type=F32)
        if nk == 1:
            o_ref[...] = p.astype(out_dtype)
            return
        acc = scratch[0] if use_scratch else o_ref
        kk = pl.program_id(2)

        @pl.when(kk == 0)
        def _():
            acc[...] = p

        @pl.when(kk > 0)
        def _():
            acc[...] += p

        if use_scratch:
            @pl.when(kk == nk - 1)
            def _():
                o_ref[...] = acc[...].astype(out_dtype)

    a_spec = pl.BlockSpec((tk, tm), lambda i, j, kk: (kk, i)) if ta else pl.BlockSpec((tm, tk), lambda i, j, kk: (i, kk))
    b_spec = pl.BlockSpec((tn, tk), lambda i, j, kk: (j, kk)) if tb else pl.BlockSpec((tk, tn), lambda i, j, kk: (kk, j))
    return pl.pallas_call(
        body, name=name, grid=(m // tm, n // tn, nk),
        in_specs=[a_spec, b_spec],
        out_specs=pl.BlockSpec((tm, tn), lambda i, j, kk: (i, j)),
        out_shape=jax.ShapeDtypeStruct((m, n), out_dtype),
        scratch_shapes=[pltpu.VMEM((tm, tn), F32)] if use_scratch else [],
    )(a, b)


def _me_and_peers():
    x, y, c = lax.axis_index("x"), lax.axis_index("y"), lax.axis_index("c")
    me = 4 * x + 2 * y + c
    peers = []
    for k in range(1, N_DEV):
        px, py, pc = x ^ (k >> 2), y ^ ((k >> 1) & 1), c ^ (k & 1)
        peers.append(((px, py, pc), 4 * px + 2 * py + pc))
    return me, peers


def _allgather_small(v, name):
    r, c = v.shape

    def body(v_ref, out_ref, send_sems, recv_sems):
        me, peers = _me_and_peers()
        out_ref[me] = v_ref[...]
        copies = []
        for k, (dev, _) in enumerate(peers):
            cp = pltpu.make_async_remote_copy(src_ref=v_ref, dst_ref=out_ref.at[me], send_sem=send_sems.at[k],
                                              recv_sem=recv_sems.at[k], device_id=dev, device_id_type=MESH)
            cp.start()
            copies.append(cp)
        for k, (dev, idx) in enumerate(peers):
            pltpu.make_async_remote_copy(src_ref=v_ref, dst_ref=out_ref.at[idx], send_sem=send_sems.at[k],
                                         recv_sem=recv_sems.at[k], device_id=dev, device_id_type=MESH).wait_recv()
        for cp in copies:
            cp.wait_send()

    return pl.pallas_call(
        body, name=name,
        in_specs=[pl.BlockSpec(memory_space=pltpu.VMEM)],
        out_specs=pl.BlockSpec(memory_space=pltpu.VMEM),
        out_shape=jax.ShapeDtypeStruct((N_DEV, r, c), v.dtype),
        scratch_shapes=[pltpu.SemaphoreType.DMA((N_DEV - 1,)), pltpu.SemaphoreType.DMA((N_DEV - 1,))],
    )(v)


def _window(ref, kind, idx, size):
    if kind == "block":
        return ref.at[idx]
    start = pl.multiple_of(idx * size, size)
    if kind == "rows":
        return ref.at[pl.ds(start, size), :]
    return ref.at[:, pl.ds(start, size)]


_BIG = (("cols", N_COL // N_DEV), ("rows", D // N_DEV), ("cols", D // N_DEV), ("rows", D // N_DEV), ("cols", D // N_DEV))
_GATHERED = (("block", 1),) + _BIG[1:]


GATHER_ORDER = (0, 1, 2, 4, 3, 5, 6, 7)
W_IN_DIRECT = (1, 2, 4, 6)
SCATTER_ORDER = (7, 6, 5, 4, 3, 2, 1, 0)
W_IN_SLOT = {0: 0, 1: 1, 2: 2, 4: 3, 6: 4}


def _proj_gather(h, shards, order, tm):
    t = h.shape[0]
    nt = len(shards)
    n_blk = N_COL // N_DEV
    full_shapes = []
    for s, (kind, size) in zip(shards, _GATHERED):
        full_shapes.append(jax.ShapeDtypeStruct({"block": (N_DEV,) + s.shape, "rows": (s.shape[0] * N_DEV, s.shape[1]),
                                                 "cols": (s.shape[0], s.shape[1] * N_DEV)}[kind], s.dtype))
    last = (N_DEV - 1, t // tm - 1)

    def body(order_ref, h_ref, *refs):
        src, proj_ref, dst = refs[:nt], refs[nt], refs[nt + 1:2 * nt + 1]
        w_all, send_sems, recv_sems, local_sems, keep_sems, h_all, h_sems = refs[2 * nt + 1:]
        j, i = pl.program_id(0), pl.program_id(1)
        me, peers = _me_and_peers()

        def landing(tn, idx):
            kind, size = _GATHERED[tn]
            return w_all.at[idx] if tn == 0 else _window(dst[tn], kind, idx, size)

        def local(tn):
            return pltpu.make_async_copy(src[tn], landing(tn, me), local_sems.at[tn])

        def remote(tn, k, block_of):
            dev, idx = peers[k - 1]
            return pltpu.make_async_remote_copy(src_ref=src[tn], dst_ref=landing(tn, me if block_of == "mine" else idx),
                                                send_sem=send_sems.at[tn, k - 1], recv_sem=recv_sems.at[tn, k - 1],
                                                device_id=dev, device_id_type=MESH)

        def forward(k):
            block = w_all.at[peers[k - 1][1]]
            return pltpu.make_async_remote_copy(src_ref=block, dst_ref=block, send_sem=send_sems.at[0, k], recv_sem=recv_sems.at[0, k],
                                                device_id=peers[0][0], device_id_type=MESH)

        def keep(step):
            blk = order_ref[step]
            return pltpu.make_async_copy(w_all.at[blk], dst[0].at[blk], keep_sems.at[step])

        def load_h(ii):
            rows = pl.ds(ii * tm, tm)
            return pltpu.make_async_copy(h_ref.at[rows, :], h_all.at[rows, :], h_sems.at[ii])

        @pl.when((j == 0) & (i == 0))
        def _():
            for ii in range(t // tm):
                load_h(ii).start()
            for tn in range(nt):
                local(tn).start()
                for k in GATHER_ORDER[1:]:
                    if tn > 0 or k in W_IN_DIRECT:
                        remote(tn, k, "mine").start()

        @pl.when(i == 0)
        def _():
            for step, k in enumerate(GATHER_ORDER):
                @pl.when(j == step)
                def _():
                    if k == 0:
                        local(0).wait()
                    else:
                        remote(0, k, "theirs").wait_recv()
                        if k in W_IN_DIRECT and k > 1:
                            forward(k).start()
                    keep(step).start()

        @pl.when(j == 0)
        def _():
            for ii in range(t // tm):
                @pl.when(i == ii)
                def _():
                    load_h(ii).wait()

        proj_ref[...] = jnp.dot(h_all[pl.ds(pl.multiple_of(i * tm, tm), tm), :], w_all[order_ref[j]], preferred_element_type=F32)

        @pl.when((j == last[0]) & (i == last[1]))
        def _():
            for step in range(N_DEV):
                keep(step).wait()
            for tn in range(1, nt):
                local(tn).wait()
                for k in range(1, N_DEV):
                    remote(tn, k, "theirs").wait_recv()
            for tn in range(nt):
                for k in range(1, N_DEV):
                    if tn > 0 or k in W_IN_DIRECT:
                        remote(tn, k, "mine").wait_send()
                    else:
                        forward(k - 1).wait_send()

    grid_spec = pltpu.PrefetchScalarGridSpec(
        num_scalar_prefetch=1, grid=(N_DEV, t // tm),
        in_specs=[ANY] + [ANY] * nt,
        out_specs=[pl.BlockSpec((tm, n_blk), lambda j, i, order_ref: (i, order_ref[j]))] + [ANY] * nt,
        scratch_shapes=[pltpu.VMEM((N_DEV, D, n_blk), BF16), pltpu.SemaphoreType.DMA((nt, N_DEV - 1)),
                        pltpu.SemaphoreType.DMA((nt, N_DEV - 1)), pltpu.SemaphoreType.DMA((nt,)), pltpu.SemaphoreType.DMA((N_DEV,)),
                        pltpu.VMEM((t, D), BF16), pltpu.SemaphoreType.DMA((t // tm,))],
    )
    return pl.pallas_call(
        body, name="proj_gather", grid_spec=grid_spec,
        out_shape=[jax.ShapeDtypeStruct((t, N_COL), F32)] + full_shapes,
    )(order, h, *shards)


def _dw_in_scatter(ht, dproj, small_grads, order, tk):
    t = ht.shape[1]
    nt = 1 + len(small_grads)
    n_blk = N_COL // N_DEV
    nk = t // tk
    slot_shapes = [jax.ShapeDtypeStruct((len(W_IN_SLOT), D, n_blk), BF16)]
    for g, (kind, size) in zip(small_grads, _BIG[1:]):
        slot_shapes.append(jax.ShapeDtypeStruct((N_DEV,) + ((size, g.shape[1]) if kind == "rows" else (g.shape[0], size)), g.dtype))

    def body(order_ref, h_ref, dp_ref, *refs):
        src, dst = refs[:nt - 1], refs[nt - 1:2 * nt - 1]
        acc, stage, partner, send_sems, recv_sems, local_sems, pair_send, pair_recv = refs[2 * nt - 1:]
        j, kk = pl.program_id(0), pl.program_id(1)
        me, peers = _me_and_peers()

        def small_local(tn):
            kind, size = _BIG[tn]
            return pltpu.make_async_copy(_window(src[tn - 1], kind, me, size), dst[tn].at[me], local_sems.at[tn])

        def small_remote(tn, k, mine):
            kind, size = _BIG[tn]
            dev, idx = peers[k - 1]
            return pltpu.make_async_remote_copy(src_ref=_window(src[tn - 1], kind, idx if mine else me, size),
                                                dst_ref=dst[tn].at[me if mine else idx],
                                                send_sem=send_sems.at[tn, k - 1], recv_sem=recv_sems.at[tn, k - 1],
                                                device_id=dev, device_id_type=MESH)

        def push(step):
            k, slot = SCATTER_ORDER[step], step % 2
            if k == 0:
                return pltpu.make_async_copy(stage.at[slot], dst[0].at[W_IN_SLOT[0]], local_sems.at[0])
            if k not in W_IN_SLOT:
                p = (k - 3) // 2
                return pltpu.make_async_remote_copy(src_ref=stage.at[slot], dst_ref=partner.at[p], send_sem=pair_send.at[p],
                                                    recv_sem=pair_recv.at[p], device_id=peers[0][0], device_id_type=MESH)
            return pltpu.make_async_remote_copy(src_ref=stage.at[slot], dst_ref=dst[0].at[W_IN_SLOT[k]],
                                                send_sem=send_sems.at[0, k - 1], recv_sem=recv_sems.at[0, k - 1],
                                                device_id=peers[k - 1][0], device_id_type=MESH)

        @pl.when((j == 0) & (kk == 0))
        def _():
            for tn in range(1, nt):
                small_local(tn).start()
                for k in range(1, N_DEV):
                    small_remote(tn, k, True).start()

        p = jnp.dot(h_ref[...], dp_ref[...], preferred_element_type=F32)

        @pl.when(kk == 0)
        def _():
            acc[...] = p

        @pl.when(kk > 0)
        def _():
            acc[...] += p

        @pl.when(kk == nk - 1)
        def _():
            for step, k in enumerate(SCATTER_ORDER):
                @pl.when(j == step)
                def _():
                    if step >= 2:
                        push(step - 2).wait_send()
                    total = acc[...]
                    if k in W_IN_SLOT and k >= 2:
                        p = k // 2 - 1
                        push(SCATTER_ORDER.index(k + 1)).wait_recv()
                        total = total + partner[p].astype(F32)
                    stage[step % 2] = total.astype(BF16)
                    push(step).start()

        @pl.when((j == N_DEV - 1) & (kk == nk - 1))
        def _():
            push(N_DEV - 2).wait_send()
            push(N_DEV - 1).wait()
            for k in (1, 2, 4, 6):
                push(SCATTER_ORDER.index(k)).wait_recv()
            for tn in range(1, nt):
                small_local(tn).wait()
                for k in range(1, N_DEV):
                    small_remote(tn, k, False).wait_recv()
                    small_remote(tn, k, True).wait_send()

    grid_spec = pltpu.PrefetchScalarGridSpec(
        num_scalar_prefetch=1, grid=(N_DEV, nk),
        in_specs=[pl.BlockSpec((D, tk), lambda j, kk, order_ref: (0, kk)),
                  pl.BlockSpec((tk, n_blk), lambda j, kk, order_ref: (kk, order_ref[j]))] + [ANY] * (nt - 1),
        out_specs=[ANY] * nt,
        scratch_shapes=[pltpu.VMEM((D, n_blk), F32), pltpu.VMEM((2, D, n_blk), BF16), pltpu.VMEM((3, D, n_blk), BF16),
                        pltpu.SemaphoreType.DMA((nt, N_DEV - 1)), pltpu.SemaphoreType.DMA((nt, N_DEV - 1)),
                        pltpu.SemaphoreType.DMA((nt,)), pltpu.SemaphoreType.DMA((3,)), pltpu.SemaphoreType.DMA((3,))],
    )
    return pl.pallas_call(body, name="dw_in_scatter", grid_spec=grid_spec, out_shape=slot_shapes)(order, ht, dproj, *small_grads)


def _adamw_math(w, g, m, v):
    m = ADAM_B1 * m + (1.0 - ADAM_B1) * g
    v = ADAM_B2 * v + (1.0 - ADAM_B2) * (g * g)
    m_hat = m / (1.0 - ADAM_B1 ** ADAM_STEP)
    v_hat = v / (1.0 - ADAM_B2 ** ADAM_STEP)
    delta = -ADAM_LR * (m_hat / (jnp.sqrt(v_hat) + ADAM_EPS) + ADAM_WD * w)
    return delta, m, v


def _sum_adamw(slots, w, m, v, tr, name):
    n_slots, r, c = slots.shape
    assert r % tr == 0

    def body(s_ref, w_ref, m_ref, v_ref, g_ref, d_ref, nm_ref, nv_ref):
        g = s_ref[0].astype(F32)
        for j in range(1, n_slots):
            g = g + s_ref[j].astype(F32)
        delta, nm, nv = _adamw_math(w_ref[...], g, m_ref[...], v_ref[...])
        g_ref[...] = g
        d_ref[...] = delta
        nm_ref[...] = nm
        nv_ref[...] = nv

    blk = pl.BlockSpec((tr, c), lambda i: (i, 0))
    return pl.pallas_call(
        body, name=name, grid=(r // tr,),
        in_specs=[pl.BlockSpec((n_slots, tr, c), lambda i: (0, i, 0)), blk, blk, blk],
        out_specs=[blk] * 4, out_shape=[jax.ShapeDtypeStruct((r, c), F32)] * 4,
    )(slots, w, m, v)


def _adamw_small(g, w, m, v, name):
    def body(g_ref, w_ref, m_ref, v_ref, d_ref, nm_ref, nv_ref):
        delta, nm, nv = _adamw_math(w_ref[...], g_ref[...], m_ref[...], v_ref[...])
        d_ref[...] = delta
        nm_ref[...] = nm
        nv_ref[...] = nv

    spec = _full(g.shape)
    return pl.pallas_call(body, name=name, grid=(1,), in_specs=[spec] * 4, out_specs=[spec] * 3,
                          out_shape=[jax.ShapeDtypeStruct(g.shape, F32)] * 3)(g, w, m, v)


def _mod_part(c_all, w_ada_l, b_ada_l):
    n = w_ada_l.shape[1]

    def body(c_ref, w_ref, b_ref, o_ref):
        o_ref[...] = jnp.dot(c_ref[...], w_ref[...], preferred_element_type=F32,
                             precision=lax.Precision.HIGHEST) + b_ref[...]

    return pl.pallas_call(body, name="mod_part", grid=(1,),
                          in_specs=[_full(c_all.shape), _full(w_ada_l.shape), _full(b_ada_l.shape)],
                          out_specs=_full((N_DEV, n)), out_shape=jax.ShapeDtypeStruct((N_DEV, n), F32))(c_all, w_ada_l, b_ada_l)


def _w_ada_update(c_all_t, dmod_cols, w, m, v):
    def body(c_ref, dm_ref, w_ref, m_ref, v_ref, g_ref, d_ref, nm_ref, nv_ref):
        g = c_ref[:, 0:1] * dm_ref[0:1, :]
        for b in range(1, N_DEV):
            g = g + c_ref[:, b:b + 1] * dm_ref[b:b + 1, :]
        delta, nm, nv = _adamw_math(w_ref[...], g, m_ref[...], v_ref[...])
        g_ref[...] = g
        d_ref[...] = delta
        nm_ref[...] = nm
        nv_ref[...] = nv

    spec = _full(w.shape)
    return pl.pallas_call(body, name="w_ada_update", grid=(1,),
                          in_specs=[_full(c_all_t.shape), _full(dmod_cols.shape), spec, spec, spec],
                          out_specs=[spec] * 4, out_shape=[jax.ShapeDtypeStruct(w.shape, F32)] * 4)(c_all_t, dmod_cols, w, m, v)


def _cast_bf16(w, name):
    def body(w_ref, o_ref):
        o_ref[...] = w_ref[...].astype(BF16)

    return pl.pallas_call(body, name=name, grid=(1,), in_specs=[_full(w.shape)], out_specs=_full(w.shape),
                          out_shape=jax.ShapeDtypeStruct(w.shape, BF16))(w)


def _prenorm(x, mod, norm_g, tm):
    t = x.shape[0]

    def body(x_ref, mod_ref, g_ref, h_ref, ht_ref):
        xv = x_ref[...]
        r = lax.rsqrt(jnp.mean(xv * xv, axis=-1, keepdims=True) + EPS)
        h = (xv * r) * g_ref[...] * (1.0 + mod_ref[:, D:2 * D]) + mod_ref[:, 0:D]
        h_ref[...] = h.astype(BF16)
        ht_ref[...] = h.T.astype(BF16)

    return pl.pallas_call(body, name="prenorm", grid=(t // tm,),
                          in_specs=[_rows(tm, D), _full((1, 3 * D)), _full((1, D))],
                          out_specs=[_rows(tm, D), pl.BlockSpec((D, tm), lambda i: (0, i))],
                          out_shape=[jax.ShapeDtypeStruct((t, D), BF16), jax.ShapeDtypeStruct((D, t), BF16)])(x, mod, norm_g)


def _rope_apply(t, cos, s_lo, s_hi):
    return t * cos + pltpu.roll(t, 120, 1) * s_lo + pltpu.roll(t, 8, 1) * s_hi


def _shift_copies(sh, buf, c0):
    rows = buf.shape[0] - 8
    for s in range(1, 8):
        sh[s, 0:rows, :] = buf[s:s + rows, pl.ds(c0, 128)]


def _window64(buf, sh, c0, start):
    s = start % 8
    if s == 0:
        return buf[start:start + 64, pl.ds(c0, 128)]
    return sh[s, start - s:start - s + 64, :]


def _conv_taps(acc_init, w_ref, buf, sh, row0, c0, offset_of_tap):
    acc = acc_init
    for j in range(CONV_K):
        acc = acc + w_ref[j:j + 1, pl.ds(c0, 128)] * _window64(buf, sh, c0, row0 + offset_of_tap(j))
    return acc


def _conv_fwd(proj, conv_w, conv_b, ln_g, ln_b, tm):
    t = proj.shape[0]
    hb = tm // HALO

    def body(a_ref, b_ref, z_ref, ah_ref, bh_ref, w_ref, cb_ref, lg_ref, lb_ref, u1_ref, pc_ref, ubuf, sh):
        i = pl.program_id(0)
        u0h = ah_ref[...] * _sig(bh_ref[...])
        ubuf[0:HALO, :] = jnp.where(i > 0, u0h, 0.0)
        ubuf[HALO:HALO + tm, :] = a_ref[...] * _sig(b_ref[...])

        def col(ci, carry):
            c0 = pl.multiple_of(ci * 128, 128)
            _shift_copies(sh, ubuf, c0)
            for rc in range(tm // 64):
                init = jnp.zeros((64, 128), F32)
                acc = _conv_taps(init, w_ref, ubuf, sh, rc * 64, c0, lambda j: HALO - (CONV_K - 1) + j)
                u1_ref[rc * 64:(rc + 1) * 64, pl.ds(c0, 128)] = acc + cb_ref[:, pl.ds(c0, 128)]
            return carry

        lax.fori_loop(0, D // 128, col, 0)
        u1 = u1_ref[...]
        mu = jnp.mean(u1, axis=-1, keepdims=True)
        xc = u1 - mu
        var = jnp.mean(xc * xc, axis=-1, keepdims=True)
        u2 = xc * lax.rsqrt(var + EPS) * lg_ref[...] + lb_ref[...]
        z = z_ref[...]
        pc_ref[...] = (u2 * _sig(u2) * (z * _sig(z))).astype(BF16)

    halo = pl.BlockSpec((HALO, D), lambda i: (jnp.maximum(i * hb - 1, 0), 0))
    halo_b = pl.BlockSpec((HALO, D), lambda i: (jnp.maximum(i * hb - 1, 0), 1))
    return pl.pallas_call(
        body, name="conv_fwd", grid=(t // tm,),
        in_specs=[_rows(tm, D, 0), _rows(tm, D, 1), _rows(tm, D, 2), halo, halo_b,
                  _full((CONV_KP, D)), _full((1, D)), _full((1, D)), _full((1, D))],
        out_specs=[_rows(tm, D), _rows(tm, D)],
        out_shape=[jax.ShapeDtypeStruct((t, D), F32), jax.ShapeDtypeStruct((t, D), BF16)],
        scratch_shapes=[pltpu.VMEM((HALO + tm, D), F32), pltpu.VMEM((8, HALO + tm, 128), F32)],
    )(proj, proj, proj, proj, proj, conv_w, conv_b, ln_g, ln_b)


def _band_masks_t(has_prev):
    key = lax.broadcasted_iota(jnp.int32, (BLK, BLK), 0)
    qry = lax.broadcasted_iota(jnp.int32, (BLK, BLK), 1)
    return jnp.logical_and(key >= qry, has_prev), key <= qry


def _head_lanes(pair, hh):
    lane = lax.broadcasted_iota(jnp.int32, pair.shape, 1)
    return jnp.where((lane >= hh * HEAD) & (lane < (hh + 1) * HEAD), pair, jnp.zeros_like(pair))


def _pair_mask(has_prev):
    mask_p, mask_c = _band_masks_t(has_prev)
    both = jnp.concatenate([mask_p, mask_c], axis=0)
    return jnp.concatenate([both, both], axis=1)


def _query_pair(pair):
    return jnp.concatenate([_head_lanes(pair, 0), _head_lanes(pair, 1)], axis=0)


def _key_pair(ref, prev, cur):
    return jnp.concatenate([ref[pl.ds(prev, BLK), :], ref[pl.ds(cur, BLK), :]], axis=0)


def _own_head(both):
    return jnp.concatenate([both[0:HEAD, 0:BLK], both[HEAD:2 * HEAD, BLK:2 * BLK]], axis=0)


def _store_transposed(dst, base, src):
    for j in range(TILE // BLK):
        dst[base // BLK + j] = src[j * BLK:(j + 1) * BLK, :].T.astype(BF16)


class _Dilated:
    def __init__(self, dil):
        self.dil = dil
        self.per = TILE // dil
        self.nbr = self.per // BLK

    def spread(self, dst, base, src_ref, dtype):
        for r in range(self.dil):
            rows = src_ref[pl.ds(r, self.per, stride=self.dil), :] if self.dil > 1 else src_ref[...]
            dst[pl.ds(pl.multiple_of(base + r * self.per, BLK), self.per), :] = rows.astype(dtype)

    def gather(self, dst_ref, src, base):
        for r in range(self.dil):
            rows = src[pl.ds(pl.multiple_of(base + r * self.per, BLK), self.per), :]
            if self.dil > 1:
                dst_ref[pl.ds(r, self.per, stride=self.dil), :] = rows
            else:
                dst_ref[...] = rows

    def block_rows(self, b, i, cur, prv):
        n = b % self.nbr
        row = pl.multiple_of(b * BLK, BLK)
        has_prev = jnp.logical_or(n > 0, i > 0)
        prev = jnp.where(n > 0, cur + row - BLK, jnp.where(i > 0, prv + row + (self.nbr - 1) * BLK, cur + row))
        return row, pl.multiple_of(prev, BLK), has_prev


def _slots(i):
    return pl.multiple_of((i % 2) * TILE, TILE), pl.multiple_of(((i + 1) % 2) * TILE, TILE)


def _nt(a, b):
    return lax.dot_general(a, b, (((1,), (1,)), ((), ())), preferred_element_type=F32)


def _qkv_specs(gi, clamp_to=None):
    def spec(col0):
        def imap(hp, i):
            return (i if clamp_to is None else jnp.minimum(i, clamp_to), (col0 + gi * ATT) // 128 + hp)
        return pl.BlockSpec((TILE, 128), imap)
    return [spec(C_Q), spec(C_K), spec(C_V)]


def _att_fwd(proj, tables):
    t = proj.shape[0]

    def body(*refs):
        qkv_refs, (c_ref, lo_ref, hi_ref, att_ref, lse_ref, tmp, qd, od, ld), kv_scratch = refs[:9], refs[9:18], refs[18:]
        i = pl.program_id(1)
        cur, prv = _slots(i)
        cs, lo, hi = c_ref[...], lo_ref[...], hi_ref[...]
        for gi, dil in GROUPS:
            dl = _Dilated(dil)
            q_ref, k_ref, v_ref = qkv_refs[3 * gi:3 * gi + 3]
            kd, vt = kv_scratch[2 * gi:2 * gi + 2]
            tmp[...] = _rope_apply(q_ref[...], cs, lo, hi) * SM_SCALE
            dl.spread(qd, 0, tmp, BF16)
            tmp[...] = _rope_apply(k_ref[...], cs, lo, hi)
            dl.spread(kd, cur, tmp, BF16)
            dl.spread(tmp, 0, v_ref, F32)
            _store_transposed(vt, cur, tmp)

            def block(b, carry, dl=dl, kd=kd, vt=vt):
                row, prev, has_prev = dl.block_rows(b, i, cur, prv)
                s = jnp.where(_pair_mask(has_prev), _nt(_key_pair(kd, prev, cur + row), _query_pair(qd[pl.ds(row, BLK), :])), NEG_INF)
                mx = jnp.max(s, axis=0, keepdims=True)
                p = jnp.exp(s - mx)
                den = jnp.sum(p, axis=0, keepdims=True)
                v_t = jnp.concatenate([vt[prev // BLK], vt[(cur + row) // BLK]], axis=1)
                acc = jnp.dot(v_t, p.astype(BF16), preferred_element_type=F32) / den
                lse = mx + jnp.log(den)
                od[pl.ds(row, BLK), :] = _own_head(acc).T
                ld[pl.ds(row, BLK), :] = _own_head(jnp.broadcast_to(lse, (2 * HEAD, 2 * BLK))).T
                return carry

            lax.fori_loop(0, TILE // BLK, block, 0, unroll=True)
            if gi == 0:
                dl.gather(att_ref, od, 0)
                dl.gather(lse_ref, ld, 0)
            else:
                dl.gather(tmp, ld, 0)
                l_run, l_new = lse_ref[...], tmp[...]
                m = jnp.maximum(l_run, l_new)
                w_run, w_new = jnp.exp(l_run - m), jnp.exp(l_new - m)
                lse_ref[...] = m + jnp.log(w_run + w_new)
                ld[...] = w_new / (w_run + w_new)
                dl.gather(tmp, od, 0)
                share = ld[...]
                att_ref[...] = att_ref[...] + share * (tmp[...] - att_ref[...])

    tab = pl.BlockSpec((TILE, 128), lambda hp, i: (i, 0))
    out_spec = pl.BlockSpec((TILE, 128), lambda hp, i: (i, hp))
    kv_shapes = [pltpu.VMEM((2 * TILE, 128), BF16), pltpu.VMEM((2 * TILE // BLK, 128, BLK), BF16)] * len(GROUPS)
    return pl.pallas_call(
        body, name="att_fwd", grid=(ATT // 128, t // TILE),
        in_specs=[spec for gi, _ in GROUPS for spec in _qkv_specs(gi)] + [tab] * 3,
        out_specs=[out_spec] * 2, out_shape=[jax.ShapeDtypeStruct((t, ATT), F32)] * 2,
        scratch_shapes=[pltpu.VMEM((TILE, 128), F32), pltpu.VMEM((TILE, 128), BF16), pltpu.VMEM((TILE, 128), F32),
                        pltpu.VMEM((TILE, 128), F32)] + kv_shapes,
    )(*([proj] * (3 * len(GROUPS))), *tables)


def _att_bwd(proj, tables, datt, dsum, lse, gi, dil):
    t = proj.shape[0]
    nt = t // TILE
    dl = _Dilated(dil)

    def body(q_ref, k_ref, v_ref, c_ref, lo_ref, hi_ref, cl_ref, lol_ref, hil_ref, do_ref, ds_ref, lse_ref,
             dq_ref, dk_ref, dv_ref, tmp, qd, kd, vd, dod, dsd, lsd, dqd, dkd, dvd, kt):
        i = pl.program_id(1)
        cur, prv = _slots(i)

        @pl.when(i < nt)
        def _():
            cs, lo, hi = c_ref[...], lo_ref[...], hi_ref[...]
            tmp[...] = _rope_apply(q_ref[...], cs, lo, hi) * SM_SCALE
            dl.spread(qd, 0, tmp, BF16)
            tmp[...] = _rope_apply(k_ref[...], cs, lo, hi)
            dl.spread(kd, cur, tmp, BF16)
            dl.spread(dqd, 0, tmp, F32)
            _store_transposed(kt, cur, dqd)
            dl.spread(vd, cur, v_ref, BF16)
            dl.spread(dod, 0, do_ref, BF16)
            dl.spread(dsd, 0, ds_ref, F32)
            dl.spread(lsd, 0, lse_ref, F32)

            def block(b, carry):
                row, prev, has_prev = dl.block_rows(b, i, cur, prv)
                q_pair, do_pair = _query_pair(qd[pl.ds(row, BLK), :]), _query_pair(dod[pl.ds(row, BLK), :])
                k_pair, v_pair = _key_pair(kd, prev, cur + row), _key_pair(vd, prev, cur + row)
                ds_t, ls_t = dsd[pl.ds(row, BLK), :].T, lsd[pl.ds(row, BLK), :].T
                lse = jnp.concatenate([ls_t[0:1, :], ls_t[HEAD:HEAD + 1, :]], axis=1)
                dsm = jnp.concatenate([ds_t[0:1, :], ds_t[HEAD:HEAD + 1, :]], axis=1)
                p = jnp.exp(jnp.where(_pair_mask(has_prev), _nt(k_pair, q_pair), NEG_INF) - lse)
                ds = (p * (_nt(v_pair, do_pair) - dsm)).astype(BF16)
                k_t = jnp.concatenate([kt[prev // BLK], kt[(cur + row) // BLK]], axis=1)
                dqd[pl.ds(row, BLK), :] = _own_head(jnp.dot(k_t, ds, preferred_element_type=F32)).T * SM_SCALE
                dk = jnp.dot(ds, q_pair, preferred_element_type=F32)
                dv = jnp.dot(p.astype(BF16), do_pair, preferred_element_type=F32)
                dkd[pl.ds(cur + row, BLK), :] = dk[BLK:2 * BLK, :]
                dvd[pl.ds(cur + row, BLK), :] = dv[BLK:2 * BLK, :]
                dkd[pl.ds(prev, BLK), :] += dk[0:BLK, :]
                dvd[pl.ds(prev, BLK), :] += dv[0:BLK, :]
                return carry

            lax.fori_loop(0, TILE // BLK, block, 0, unroll=True)
            dl.gather(tmp, dqd, 0)
            dq_ref[...] = _rope_apply(tmp[...], cs, -lo, -hi).astype(BF16)

        @pl.when(i > 0)
        def _():
            dl.gather(tmp, dkd, prv)
            dk_ref[...] = _rope_apply(tmp[...], cl_ref[...], -lol_ref[...], -hil_ref[...]).astype(BF16)
            dl.gather(tmp, dvd, prv)
            dv_ref[...] = tmp[...].astype(BF16)

    now = lambda col: pl.BlockSpec((TILE, 128), lambda hp, i: (jnp.minimum(i, nt - 1), col(hp)))
    lag = lambda col: pl.BlockSpec((TILE, 128), lambda hp, i: (jnp.maximum(i - 1, 0), col(hp)))
    first, pair = (lambda hp: 0), (lambda hp: hp)
    return pl.pallas_call(
        body, name=f"att_bwd_g{gi}", grid=(ATT // 128, nt + 1),
        in_specs=_qkv_specs(gi, nt - 1) + [now(first)] * 3 + [lag(first)] * 3 + [now(pair)] * 3,
        out_specs=[now(pair), lag(pair), lag(pair)],
        out_shape=[jax.ShapeDtypeStruct((t, ATT), BF16)] * 3,
        scratch_shapes=[pltpu.VMEM((TILE, 128), F32), pltpu.VMEM((TILE, 128), BF16), pltpu.VMEM((2 * TILE, 128), BF16),
                        pltpu.VMEM((2 * TILE, 128), BF16), pltpu.VMEM((TILE, 128), BF16), pltpu.VMEM((TILE, 128), F32),
                        pltpu.VMEM((TILE, 128), F32), pltpu.VMEM((TILE, 128), F32), pltpu.VMEM((2 * TILE, 128), F32),
                        pltpu.VMEM((2 * TILE, 128), F32), pltpu.VMEM((2 * TILE // BLK, 128, BLK), BF16)],
    )(proj, proj, proj, *tables, *tables, datt, dsum, lse)


def _merge_head(pc, att, proj, x, mod, final_g, target, w_co, w_ao, w_o, u1, ln_g, ln_b, tm):
    t = x.shape[0]

    def body(pc_ref, att_ref, gc_ref, ga_ref, x_ref, mod_ref, fg_ref, tg_ref, wco_ref, wao_ref, wo_ref,
             zc_ref, u1_ref, lg_ref, lb_ref, za_ref,
             merged_ref, do_ref, dyc_ref, dya_ref, dout_ref, du1_ref, dzc_ref, datt_ref, ds_ref, dp_ref, pa_ref,
             sq_ref, gfg_ref, dgate_ref, dlg_ref, dlb_ref, dcb_ref):
        i = pl.program_id(0)
        att_v = att_ref[...]
        za = za_ref[...]
        sza = _sig(za)
        pa = (att_v * (za * sza)).astype(BF16)
        pa_ref[...] = pa
        yc = jnp.dot(pc_ref[...], wco_ref[...], preferred_element_type=F32)
        ya = jnp.dot(pa, wao_ref[...], preferred_element_type=F32)
        sc, sa = _sig(gc_ref[...]), _sig(ga_ref[...])
        merged = (sc * yc + sa * ya).astype(BF16)
        merged_ref[...] = merged
        ov = jnp.dot(merged, wo_ref[...], preferred_element_type=F32)
        gate = mod_ref[:, 2 * D:3 * D]
        out = x_ref[...] + gate * ov
        r = lax.rsqrt(jnp.mean(out * out, axis=-1, keepdims=True) + EPS)
        yn = out * r
        diff = yn * fg_ref[...] - tg_ref[...]
        dy = diff * (1.0 / D)
        gy = dy * fg_ref[...]
        dout = r * (gy - yn * jnp.mean(gy * yn, axis=-1, keepdims=True))
        dout_ref[...] = dout
        do = (dout * gate).astype(BF16)
        do_ref[...] = do
        _acc_rows(sq_ref, i, diff * diff)
        _acc_rows(gfg_ref, i, dy * yn)
        _acc_rows(dgate_ref, i, dout * ov)
        dm = _nt(do, wo_ref[...])
        dyc = (dm * sc).astype(BF16)
        dya = (dm * sa).astype(BF16)
        dyc_ref[...] = dyc
        dya_ref[...] = dya
        dp_ref[:, ATT:ATT + D] = (dm * yc * sc * (1.0 - sc)).astype(BF16)
        dp_ref[:, ATT + D:ATT + 2 * D] = (dm * ya * sa * (1.0 - sa)).astype(BF16)

        dpc = _nt(dyc, wco_ref[...])
        u1v = u1_ref[...]
        xc = u1v - jnp.mean(u1v, axis=-1, keepdims=True)
        rs = lax.rsqrt(jnp.mean(xc * xc, axis=-1, keepdims=True) + EPS)
        uhat = xc * rs
        u2 = uhat * lg_ref[...] + lb_ref[...]
        s2 = _sig(u2)
        zc = zc_ref[...]
        szc = _sig(zc)
        dzc_ref[...] = (dpc * (u2 * s2) * _dsilu(zc, szc)).astype(BF16)
        du2 = dpc * (zc * szc) * _dsilu(u2, s2)
        duhat = du2 * lg_ref[...]
        du1 = rs * (duhat - jnp.mean(duhat, axis=-1, keepdims=True) - uhat * jnp.mean(duhat * uhat, axis=-1, keepdims=True))
        du1_ref[...] = du1
        _acc_rows(dlg_ref, i, du2 * uhat)
        _acc_rows(dlb_ref, i, du2)
        _acc_rows(dcb_ref, i, du1)

        dpa = _nt(dya, wao_ref[...])
        datt = dpa * (za * sza)
        datt_ref[...] = datt
        dp_ref[:, 0:ATT] = (dpa * att_v * _dsilu(za, sza)).astype(BF16)
        prod = datt * att_v
        for hd in range(ATT // HEAD):
            sl = slice(hd * HEAD, (hd + 1) * HEAD)
            ds_ref[:, sl] = jnp.broadcast_to(jnp.sum(prod[:, sl], axis=-1, keepdims=True), (tm, HEAD))

    vec = _full((1, D))
    bf = lambda w: jax.ShapeDtypeStruct((t, w), BF16)
    f32 = lambda w: jax.ShapeDtypeStruct((t, w), F32)
    tail = ATT + 2 * D
    return pl.pallas_call(
        body, name="merge_head", grid=(t // tm,),
        in_specs=[_rows(tm, D), _rows(tm, ATT), _rows(tm, D, C_GC // D), _rows(tm, D, C_GA // D), _rows(tm, D),
                  _full((1, 3 * D)), vec, _rows(tm, D), _full((D, D)), _full((ATT, D)), _full((D, D)),
                  _rows(tm, D, C_ZC // D), _rows(tm, D), vec, vec, _rows(tm, ATT, C_ZA // ATT)],
        out_specs=[_rows(tm, D)] * 7 + [_rows(tm, ATT), _rows(tm, ATT), _rows(tm, tail, C_ZA // tail),
                                        _rows(tm, ATT)] + [vec] * 6,
        out_shape=[bf(D), bf(D), bf(D), bf(D), f32(D), f32(D), bf(D), f32(ATT), f32(ATT), bf(N_COL), bf(ATT)]
        + [jax.ShapeDtypeStruct((1, D), F32)] * 6,
    )(pc, att, proj, proj, x, mod, final_g, target, w_co, w_ao, w_o, proj, u1, ln_g, ln_b, proj)


def _acc_rows(ref, i, val):
    @pl.when(i == 0)
    def _():
        ref[...] = jnp.zeros_like(ref)

    ref[...] += jnp.sum(val, axis=0, keepdims=True)


def _conv_bwd_taps(du1, proj, conv_w, dzc, dqkv, dproj, tm):
    t = proj.shape[0]
    hb = tm // HALO
    last = t // HALO - 1

    def body(du_ref, duh_ref, a_ref, b_ref, ah_ref, bh_ref, w_ref, dzc_ref, *rest):
        qkv_refs, (dp_in, dp_ref, dw_ref, dbuf, ubuf, g0, shd, shu) = rest[:9], rest[9:]
        del dp_in
        dp_ref[:, C_ZC:C_ZC + D] = dzc_ref[...]
        for n, ref in enumerate(qkv_refs):
            dp_ref[:, C_Q + n * ATT:C_Q + (n + 1) * ATT] = ref[...]
        i = pl.program_id(0)
        a, sb = a_ref[...], _sig(b_ref[...])
        ubuf[0:HALO, :] = jnp.where(i > 0, ah_ref[...] * _sig(bh_ref[...]), 0.0)
        ubuf[HALO:HALO + tm, :] = a * sb
        dbuf[0:tm, :] = du_ref[...]
        dbuf[tm:tm + HALO, :] = jnp.where(i < pl.num_programs(0) - 1, duh_ref[...], 0.0)

        @pl.when(i == 0)
        def _():
            dw_ref[...] = jnp.zeros_like(dw_ref)

        def col(ci, carry):
            c0 = pl.multiple_of(ci * 128, 128)
            _shift_copies(shd, dbuf, c0)
            _shift_copies(shu, ubuf, c0)
            for rc in range(tm // 64):
                g0[rc * 64:(rc + 1) * 64, pl.ds(c0, 128)] = _conv_taps(
                    jnp.zeros((64, 128), F32), w_ref, dbuf, shd, rc * 64, c0, lambda j: CONV_K - 1 - j)
            for j in range(CONV_K):
                part = jnp.zeros((8, 128), F32)
                for rc in range(tm // 64):
                    off = rc * 64 + HALO - (CONV_K - 1) + j
                    prod = dbuf[rc * 64:(rc + 1) * 64, pl.ds(c0, 128)] * _window64(ubuf, shu, c0, off)
                    part = part + jnp.sum(prod.reshape(8, 8, 128), axis=0)
                dw_ref[j:j + 1, pl.ds(c0, 128)] += jnp.sum(part, axis=0, keepdims=True)
            return carry

        lax.fori_loop(0, D // 128, col, 0)
        du0 = g0[...]
        dp_ref[:, 0:D] = (du0 * sb).astype(BF16)
        dp_ref[:, D:2 * D] = (du0 * a * sb * (1.0 - sb)).astype(BF16)

    prev = lambda col: pl.BlockSpec((HALO, D), lambda i: (jnp.maximum(i * hb - 1, 0), col))
    nxt = pl.BlockSpec((HALO, D), lambda i: (jnp.minimum((i + 1) * hb, last), 0))
    return pl.pallas_call(
        body, name="conv_bwd_taps", grid=(t // tm,),
        in_specs=[_rows(tm, D), nxt, _rows(tm, D, 0), _rows(tm, D, 1), prev(0), prev(1), _full((CONV_KP, D)),
                  _rows(tm, D)] + [_rows(tm, ATT)] * 9 + [ANY],
        out_specs=[_rows(tm, C_ZA, 0), _full((CONV_KP, D))],
        out_shape=[jax.ShapeDtypeStruct((t, N_COL), BF16), jax.ShapeDtypeStruct((CONV_KP, D), F32)],
        scratch_shapes=[pltpu.VMEM((tm + HALO, D), F32), pltpu.VMEM((HALO + tm, D), F32), pltpu.VMEM((tm, D), F32),
                        pltpu.VMEM((8, HALO + tm, 128), F32), pltpu.VMEM((8, HALO + tm, 128), F32)],
        input_output_aliases={17: 0},
    )(du1, du1, proj, proj, proj, proj, conv_w, dzc, *dqkv, dproj)


def _dh_prenorm_bwd(dproj, w_in_blocks, x, dout, mod, norm_g, tm):
    t = x.shape[0]
    nk, _, tk = w_in_blocks.shape

    def body(dp_ref, w_ref, x_ref, dout_ref, mod_ref, g_ref, gx_ref, dshift_ref, dscale_ref, dg_ref, acc):
        i, kk = pl.program_id(0), pl.program_id(1)
        p = _nt(dp_ref[...], w_ref[...])

        @pl.when(kk == 0)
        def _():
            acc[...] = p

        @pl.when(kk > 0)
        def _():
            acc[...] += p

        @pl.when(kk == nk - 1)
        def _():
            xv, dhv = x_ref[...], acc[...]
            r = lax.rsqrt(jnp.mean(xv * xv, axis=-1, keepdims=True) + EPS)
            xn = xv * r
            one_scale = 1.0 + mod_ref[:, D:2 * D]
            dxn = dhv * (g_ref[...] * one_scale)
            gx_ref[...] = r * (dxn - xn * jnp.mean(dxn * xn, axis=-1, keepdims=True)) + dout_ref[...]
            _acc_rows(dshift_ref, i, dhv)
            _acc_rows(dscale_ref, i, dhv * xn * g_ref[...])
            _acc_rows(dg_ref, i, dhv * xn * one_scale)

    row = pl.BlockSpec((tm, D), lambda i, kk: (i, 0))
    vec = pl.BlockSpec((1, D), lambda i, kk: (0, 0))
    return pl.pallas_call(
        body, name="dh_prenorm_bwd", grid=(t // tm, nk),
        in_specs=[pl.BlockSpec((tm, tk), lambda i, kk: (i, kk)), pl.BlockSpec((None, D, tk), lambda i, kk: (kk, 0, 0)),
                  row, row, pl.BlockSpec((1, 3 * D), lambda i, kk: (0, 0)), vec],
        out_specs=[row, vec, vec, vec],
        out_shape=[jax.ShapeDtypeStruct((t, D), F32)] + [jax.ShapeDtypeStruct((1, D), F32)] * 3,
        scratch_shapes=[pltpu.VMEM((tm, D), F32)],
    )(dproj, w_in_blocks, x, dout, mod, norm_g)


def _sum_devices(gathered):
    w = gathered.shape[-1]

    def body(g_ref, o_ref):
        acc = g_ref[0]
        for j in range(1, N_DEV):
            acc = acc + g_ref[j]
        o_ref[...] = acc

    return pl.pallas_call(body, name="sum_devices", grid=(1,), in_specs=[_full(gathered.shape)], out_specs=_full((1, w)),
                          out_shape=jax.ShapeDtypeStruct((1, w), F32))(gathered)


def _rope_tables(positions):
    half = HEAD // 8
    t = positions.shape[-1]
    inv_freq = ROPE_THETA ** (-(jnp.arange(half, dtype=F32) * 2.0 / (2 * half)))
    ang = positions.reshape(t, 1).astype(F32) * inv_freq
    cos, sin = jnp.cos(ang), jnp.sin(ang)
    zeros = lambda n: jnp.zeros((t, n), F32)
    c64 = jnp.concatenate([cos, cos, jnp.ones((t, HEAD - 2 * half), F32)], axis=1)
    lo64 = jnp.concatenate([-sin, zeros(HEAD - half)], axis=1)
    hi64 = jnp.concatenate([zeros(half), sin, zeros(HEAD - 2 * half)], axis=1)
    return tuple(jnp.tile(a, (1, 2)) for a in (c64, lo64, hi64))


def kernel(x, c, positions, norm_g, w_ada, b_ada, w_in, conv_w, conv_b, conv_ln_g, conv_ln_b, w_conv_out, w_att_out, w_o, final_g, loss_target, m_norm_g, m_w_ada, m_b_ada, m_w_in, m_conv_w, m_conv_b, m_conv_ln_g, m_conv_ln_b, m_w_conv_out, m_w_att_out, m_w_o, m_final_g, v_norm_g, v_w_ada, v_b_ada, v_w_in, v_conv_w, v_conv_b, v_conv_ln_g, v_conv_ln_b, v_w_conv_out, v_w_att_out, v_w_o, v_final_g):
    me = 4 * lax.axis_index("x") + 2 * lax.axis_index("y") + lax.axis_index("c")
    x2, tgt = x[0], loss_target[0]
    t = x2.shape[0]
    te = 512 if t % 512 == 0 else 256
    tcv = 256
    tmh = 256
    tmm = 1024 if t % 1024 == 0 else 256
    n_ada = w_ada.shape[-1]

    pad_taps = lambda a: jnp.pad(a[0], ((0, CONV_KP - CONV_K), (0, 0)))
    shards = (_cast_bf16(w_in[0], "cast_w_in"), _cast_bf16(w_conv_out[0], "cast_w_conv_out"),
              _cast_bf16(w_att_out[0], "cast_w_att_out"), _cast_bf16(w_o[0], "cast_w_o"), pad_taps(conv_w))
    block_of = lambda relations: jnp.bitwise_xor(me, jnp.array(relations, jnp.int32))

    c_all = _allgather_small(c, "gather_c").reshape(N_DEV, D)
    b_ada_l = lax.dynamic_slice(b_ada, (0, me * n_ada), (1, n_ada))
    parts = _allgather_small(_mod_part(c_all, w_ada[0], b_ada_l), "gather_mod")
    mod = lax.dynamic_slice(parts, (0, me, 0), (N_DEV, 1, n_ada)).reshape(1, N_DEV * n_ada)

    h, ht = _prenorm(x2, mod, norm_g, te)
    proj, w_in_f, w_co_f, w_ao_f, w_o_f, conv_w_f = _proj_gather(h, shards, block_of(GATHER_ORDER), tmm)
    u1, pc = _conv_fwd(proj, conv_w_f, conv_b, conv_ln_g, conv_ln_b, tcv)
    tables = _rope_tables(positions)
    att, lse = _att_fwd(proj, tables)

    (merged, do, dyc, dya, dout, du1, dzc, datt, dsum, dproj, pa,
     sq_sum, g_final, d_gate, d_ln_g, d_ln_b, d_conv_b) = _merge_head(
        pc, att, proj, x2, mod, final_g.reshape(1, D), tgt, w_co_f, w_ao_f, w_o_f, u1, conv_ln_g, conv_ln_b, tmh)

    tkw = 2048 if t % 2048 == 0 else 256
    dw_o = _matmul(merged, do, ta=True, out_dtype=BF16, tm=D, tn=D, tk=tkw, name="dw_o")
    dw_co = _matmul(pc, dyc, ta=True, out_dtype=BF16, tm=D, tn=D, tk=tkw, name="dw_conv_out")
    dw_ao = _matmul(pa, dya, ta=True, out_dtype=BF16, tm=ATT, tn=D, tk=tkw, name="dw_att_out")
    dqs, dks, dvs = [], [], []
    for gi, dil in GROUPS:
        dq, dk, dv = _att_bwd(proj, tables, datt, dsum, lse, gi, dil)
        dqs.append(dq), dks.append(dk), dvs.append(dv)
    dproj, dconv_w = _conv_bwd_taps(du1, proj, conv_w_f, dzc, dqs + dks + dvs, dproj, tcv)
    grad_x, d_shift, d_scale, d_norm_g = _dh_prenorm_bwd(dproj, w_in_f, x2, dout, mod, norm_g, tmm)

    packed = jnp.concatenate([d_shift, d_scale, d_gate, d_norm_g, d_conv_b, d_ln_g, d_ln_b, g_final, sq_sum], axis=1)
    gathered = _allgather_small(packed, "gather_partials")
    total = _sum_devices(gathered)
    seg = lambda k, n=1: total[:, k * D:(k + n) * D]
    g_b_ada, g_norm_g, g_conv_b, g_ln_g, g_ln_b, g_final_g = seg(0, 3), seg(3), seg(4), seg(5), seg(6), seg(7)
    loss = (0.5 / D) * jnp.sum(seg(8))
    dmod_all = gathered[:, 0, 0:3 * D]
    dmod_cols = lax.dynamic_slice(dmod_all, (0, me * n_ada), (N_DEV, n_ada))
    g_w_ada, d_w_ada, nm_w_ada, nv_w_ada = _w_ada_update(c_all.T, dmod_cols, w_ada[0], m_w_ada[0], v_w_ada[0])

    small = {}
    for name, g, w, m, v in (("norm_g", g_norm_g, norm_g, m_norm_g, v_norm_g), ("b_ada", g_b_ada, b_ada, m_b_ada, v_b_ada),
                             ("conv_b", g_conv_b, conv_b, m_conv_b, v_conv_b), ("conv_ln_g", g_ln_g, conv_ln_g, m_conv_ln_g, v_conv_ln_g),
                             ("conv_ln_b", g_ln_b, conv_ln_b, m_conv_ln_b, v_conv_ln_b),
                             ("final_g", g_final_g, final_g.reshape(1, D), m_final_g.reshape(1, D), v_final_g.reshape(1, D))):
        small[name] = (g,) + tuple(_adamw_small(g, w, m, v, "adamw_" + name))

    slots = _dw_in_scatter(ht, dproj, (dw_co, dw_ao, dw_o, dconv_w), block_of(SCATTER_ORDER), tkw)
    big = {
        "w_in": _sum_adamw(slots[0], w_in[0], m_w_in[0], v_w_in[0], 256, "adamw_w_in"),
        "w_conv_out": _sum_adamw(slots[1], w_conv_out[0], m_w_conv_out[0], v_w_conv_out[0], 128, "adamw_w_conv_out"),
        "w_att_out": _sum_adamw(slots[2], w_att_out[0], m_w_att_out[0], v_w_att_out[0], 512, "adamw_w_att_out"),
        "w_o": _sum_adamw(slots[3], w_o[0], m_w_o[0], v_w_o[0], 128, "adamw_w_o"),
        "conv_w": [r[:CONV_K] for r in _sum_adamw(slots[4], pad_taps(conv_w), pad_taps(m_conv_w), pad_taps(v_conv_w), CONV_KP, "adamw_conv_w")],
    }
    big["w_ada"] = (g_w_ada, d_w_ada, nm_w_ada, nv_w_ada)

    order = ("norm_g", "w_ada", "b_ada", "w_in", "conv_w", "conv_b", "conv_ln_g", "conv_ln_b", "w_conv_out", "w_att_out", "w_o", "final_g")
    lead = lambda name, a: a.reshape(D) if name == "final_g" else (a[None] if name in big else a)
    result = {**small, **big}
    outs = [loss, grad_x[None]]
    for field in range(4):
        outs += [lead(name, result[name][field]) for name in order]
    return tuple(outs)
```

```python
import jax
import jax.numpy as jnp
from jax import lax
from jax.experimental import pallas as pl
from jax.experimental.pallas import tpu as pltpu

F32 = jnp.float32
BF16 = jnp.bfloat16

N_DEV = 8
D = 1024
N_COL = 10240
C_A, C_B, C_ZC, C_Q, C_K, C_V, C_ZA, C_GC, C_GA = 0, 1024, 2048, 3072, 4608, 6144, 7680, 8192, 9216
QKV = 1536
ATT = 512
HEAD = 64
BLK = 128
TILE = 2048
GROUPS = ((0, 1), (1, 4), (2, 16))
CONV_K = 31
CONV_KP = 32
HALO = 32
EPS = 1e-6
NEG_INF = -1e30
ROPE_THETA = 500000.0
SM_SCALE = HEAD ** -0.5

ADAM_LR, ADAM_B1, ADAM_B2, ADAM_EPS, ADAM_WD, ADAM_STEP = 0.001, 0.9, 0.999, 1e-08, 0.01, 10

MESH = pl.DeviceIdType.MESH
ANY = pl.BlockSpec(memory_space=pl.ANY)


def _sig(v):
    return 1.0 / (1.0 + jnp.exp(-v))


def _dsilu(v, s):
    return s * (1.0 + v * (1.0 - s))


def _full(shape):
    return pl.BlockSpec(shape, lambda *_: (0,) * len(shape))


def _rows(tm, width, col=0):
    return pl.BlockSpec((tm, width), lambda i: (i, col))


def _matmul(a, b, *, ta=False, tb=False, out_dtype=F32, tm, tn, tk, name):
    m, k = (a.shape[1], a.shape[0]) if ta else a.shape
    n = b.shape[0] if tb else b.shape[1]
    assert (b.shape[1] if tb else b.shape[0]) == k
    assert m % tm == 0 and n % tn == 0 and k % tk == 0
    nk = k // tk
    dims = (((0 if ta else 1,), (1 if tb else 0,)), ((), ()))
    use_scratch = out_dtype != F32 and nk > 1

    def body(a_ref, b_ref, o_ref, *scratch):
        p = lax.dot_general(a_ref[...], b_ref[...], dims, preferred_element_type=F32)
        if nk == 1:
            o_ref[...] = p.astype(out_dtype)
            return
        acc = scratch[0] if use_scratch else o_ref
        kk = pl.program_id(2)

        @pl.when(kk == 0)
        def _():
            acc[...] = p

        @pl.when(kk > 0)
        def _():
            acc[...] += p

        if use_scratch:
            @pl.when(kk == nk - 1)
            def _():
                o_ref[...] = acc[...].astype(out_dtype)

    a_spec = pl.BlockSpec((tk, tm), lambda i, j, kk: (kk, i)) if ta else pl.BlockSpec((tm, tk), lambda i, j, kk: (i, kk))
    b_spec = pl.BlockSpec((tn, tk), lambda i, j, kk: (j, kk)) if tb else pl.BlockSpec((tk, tn), lambda i, j, kk: (kk, j))
    return pl.pallas_call(
        body, name=name, grid=(m // tm, n // tn, nk),
        in_specs=[a_spec, b_spec],
        out_specs=pl.BlockSpec((tm, tn), lambda i, j, kk: (i, j)),
        out_shape=jax.ShapeDtypeStruct((m, n), out_dtype),
        scratch_shapes=[pltpu.VMEM((tm, tn), F32)] if use_scratch else [],
    )(a, b)


def _me_and_peers():
    x, y, c = lax.axis_index("x"), lax.axis_index("y"), lax.axis_index("c")
    me = 4 * x + 2 * y + c
    peers = []
    for k in range(1, N_DEV):
        px, py, pc = x ^ (k >> 2), y ^ ((k >> 1) & 1), c ^ (k & 1)
        peers.append(((px, py, pc), 4 * px + 2 * py + pc))
    return me, peers


def _allgather_small(v, name):
    r, c = v.shape

    def body(v_ref, out_ref, send_sems, recv_sems):
        me, peers = _me_and_peers()
        out_ref[me] = v_ref[...]
        copies = []
        for k, (dev, _) in enumerate(peers):
            cp = pltpu.make_async_remote_copy(src_ref=v_ref, dst_ref=out_ref.at[me], send_sem=send_sems.at[k],
                                              recv_sem=recv_sems.at[k], device_id=dev, device_id_type=MESH)
            cp.start()
            copies.append(cp)
        for k, (dev, idx) in enumerate(peers):
            pltpu.make_async_remote_copy(src_ref=v_ref, dst_ref=out_ref.at[idx], send_sem=send_sems.at[k],
                                         recv_sem=recv_sems.at[k], device_id=dev, device_id_type=MESH).wait_recv()
        for cp in copies:
            cp.wait_send()

    return pl.pallas_call(
        body, name=name,
        in_specs=[pl.BlockSpec(memory_space=pltpu.VMEM)],
        out_specs=pl.BlockSpec(memory_space=pltpu.VMEM),
        out_shape=jax.ShapeDtypeStruct((N_DEV, r, c), v.dtype),
        scratch_shapes=[pltpu.SemaphoreType.DMA((N_DEV - 1,)), pltpu.SemaphoreType.DMA((N_DEV - 1,))],
    )(v)


def _window(ref, kind, idx, size):
    if kind == "block":
        return ref.at[idx]
    start = pl.multiple_of(idx * size, size)
    if kind == "rows":
        return ref.at[pl.ds(start, size), :]
    return ref.at[:, pl.ds(start, size)]


_BIG = (("cols", N_COL // N_DEV), ("rows", D // N_DEV), ("cols", D // N_DEV), ("rows", D // N_DEV), ("cols", D // N_DEV))
_GATHERED = (("block", 1),) + _BIG[1:]


GATHER_ORDER = (0, 1, 2, 4, 3, 5, 6, 7)
W_IN_DIRECT = (1, 2, 4, 6)
SCATTER_ORDER = (7, 6, 5, 4, 3, 2, 1, 0)
W_IN_SLOT = {0: 0, 1: 1, 2: 2, 4: 3, 6: 4}


def _proj_gather(h, shards, order, tm):
    t = h.shape[0]
    nt = len(shards)
    n_blk = N_COL // N_DEV
    full_shapes = []
    for s, (kind, size) in zip(shards, _GATHERED):
        full_shapes.append(jax.ShapeDtypeStruct({"block": (N_DEV,) + s.shape, "rows": (s.shape[0] * N_DEV, s.shape[1]),
                                                 "cols": (s.shape[0], s.shape[1] * N_DEV)}[kind], s.dtype))
    last = (N_DEV - 1, t // tm - 1)

    def body(order_ref, h_ref, *refs):
        src, proj_ref, dst = refs[:nt], refs[nt], refs[nt + 1:2 * nt + 1]
        w_all, send_sems, recv_sems, local_sems, keep_sems, h_all, h_sems = refs[2 * nt + 1:]
        j, i = pl.program_id(0), pl.program_id(1)
        me, peers = _me_and_peers()

        def landing(tn, idx):
            kind, size = _GATHERED[tn]
            return w_all.at[idx] if tn == 0 else _window(dst[tn], kind, idx, size)

        def local(tn):
            return pltpu.make_async_copy(src[tn], landing(tn, me), local_sems.at[tn])

        def remote(tn, k, block_of):
            dev, idx = peers[k - 1]
            return pltpu.make_async_remote_copy(src_ref=src[tn], dst_ref=landing(tn, me if block_of == "mine" else idx),
                                                send_sem=send_sems.at[tn, k - 1], recv_sem=recv_sems.at[tn, k - 1],
                                                device_id=dev, device_id_type=MESH)

        def forward(k):
            block = w_all.at[peers[k - 1][1]]
            return pltpu.make_async_remote_copy(src_ref=block, dst_ref=block, send_sem=send_sems.at[0, k], recv_sem=recv_sems.at[0, k],
                                                device_id=peers[0][0], device_id_type=MESH)

        def keep(step):
            blk = order_ref[step]
            return pltpu.make_async_copy(w_all.at[blk], dst[0].at[blk], keep_sems.at[step])

        def load_h(ii):
            rows = pl.ds(ii * tm, tm)
            return pltpu.make_async_copy(h_ref.at[rows, :], h_all.at[rows, :], h_sems.at[ii])

        @pl.when((j == 0) & (i == 0))
        def _():
            for ii in range(t // tm):
                load_h(ii).start()
            for tn in range(nt):
                local(tn).start()
                for k in GATHER_ORDER[1:]:
                    if tn > 0 or k in W_IN_DIRECT:
                        remote(tn, k, "mine").start()

        @pl.when(i == 0)
        def _():
            for step, k in enumerate(GATHER_ORDER):
                @pl.when(j == step)
                def _():
                    if k == 0:
                        local(0).wait()
                    else:
                        remote(0, k, "theirs").wait_recv()
                        if k in W_IN_DIRECT and k > 1:
                            forward(k).start()
                    keep(step).start()

        @pl.when(j == 0)
        def _():
            for ii in range(t // tm):
                @pl.when(i == ii)
                def _():
                    load_h(ii).wait()

        proj_ref[...] = jnp.dot(h_all[pl.ds(pl.multiple_of(i * tm, tm), tm), :], w_all[order_ref[j]], preferred_element_type=F32)

        @pl.when((j == last[0]) & (i == last[1]))
        def _():
            for step in range(N_DEV):
                keep(step).wait()
            for tn in range(1, nt):
                local(tn).wait()
                for k in range(1, N_DEV):
                    remote(tn, k, "theirs").wait_recv()
            for tn in range(nt):
                for k in range(1, N_DEV):
                    if tn > 0 or k in W_IN_DIRECT:
                        remote(tn, k, "mine").wait_send()
                    else:
                        forward(k - 1).wait_send()

    grid_spec = pltpu.PrefetchScalarGridSpec(
        num_scalar_prefetch=1, grid=(N_DEV, t // tm),
        in_specs=[ANY] + [ANY] * nt,
        out_specs=[pl.BlockSpec((tm, n_blk), lambda j, i, order_ref: (i, order_ref[j]))] + [ANY] * nt,
        scratch_shapes=[pltpu.VMEM((N_DEV, D, n_blk), BF16), pltpu.SemaphoreType.DMA((nt, N_DEV - 1)),
                        pltpu.SemaphoreType.DMA((nt, N_DEV - 1)), pltpu.SemaphoreType.DMA((nt,)), pltpu.SemaphoreType.DMA((N_DEV,)),
                        pltpu.VMEM((t, D), BF16), pltpu.SemaphoreType.DMA((t // tm,))],
    )
    return pl.pallas_call(
        body, name="proj_gather", grid_spec=grid_spec,
        out_shape=[jax.ShapeDtypeStruct((t, N_COL), F32)] + full_shapes,
    )(order, h, *shards)


def _dw_in_scatter(ht, dproj, small_grads, order, tk):
    t = ht.shape[1]
    nt = 1 + len(small_grads)
    n_blk = N_COL // N_DEV
    nk = t // tk
    slot_shapes = [jax.ShapeDtypeStruct((len(W_IN_SLOT), D, n_blk), BF16)]
    for g, (kind, size) in zip(small_grads, _BIG[1:]):
        slot_shapes.append(jax.ShapeDtypeStruct((N_DEV,) + ((size, g.shape[1]) if kind == "rows" else (g.shape[0], size)), g.dtype))

    def body(order_ref, h_ref, dp_ref, *refs):
        src, dst = refs[:nt - 1], refs[nt - 1:2 * nt - 1]
        acc, stage, partner, send_sems, recv_sems, local_sems, pair_send, pair_recv = refs[2 * nt - 1:]
        j, kk = pl.program_id(0), pl.program_id(1)
        me, peers = _me_and_peers()

        def small_local(tn):
            kind, size = _BIG[tn]
            return pltpu.make_async_copy(_window(src[tn - 1], kind, me, size), dst[tn].at[me], local_sems.at[tn])

        def small_remote(tn, k, mine):
            kind, size = _BIG[tn]
            dev, idx = peers[k - 1]
            return pltpu.make_async_remote_copy(src_ref=_window(src[tn - 1], kind, idx if mine else me, size),
                                                dst_ref=dst[tn].at[me if mine else idx],
                                                send_sem=send_sems.at[tn, k - 1], recv_sem=recv_sems.at[tn, k - 1],
                                                device_id=dev, device_id_type=MESH)

        def push(step):
            k, slot = SCATTER_ORDER[step], step % 2
            if k == 0:
                return pltpu.make_async_copy(stage.at[slot], dst[0].at[W_IN_SLOT[0]], local_sems.at[0])
            if k not in W_IN_SLOT:
                p = (k - 3) // 2
                return pltpu.make_async_remote_copy(src_ref=stage.at[slot], dst_ref=partner.at[p], send_sem=pair_send.at[p],
                                                    recv_sem=pair_recv.at[p], device_id=peers[0][0], device_id_type=MESH)
            return pltpu.make_async_remote_copy(src_ref=stage.at[slot], dst_ref=dst[0].at[W_IN_SLOT[k]],
                                                send_sem=send_sems.at[0, k - 1], recv_sem=recv_sems.at[0, k - 1],
                                                device_id=peers[k - 1][0], device_id_type=MESH)

        @pl.when((j == 0) & (kk == 0))
        def _():
            for tn in range(1, nt):
                small_local(tn).start()
                for k in range(1, N_DEV):
                    small_remote(tn, k, True).start()

        p = jnp.dot(h_ref[...], dp_ref[...], preferred_element_type=F32)

        @pl.when(kk == 0)
        def _():
            acc[...] = p

        @pl.when(kk > 0)
        def _():
            acc[...] += p

        @pl.when(kk == nk - 1)
        def _():
            for step, k in enumerate(SCATTER_ORDER):
                @pl.when(j == step)
                def _():
                    if step >= 2:
                        push(step - 2).wait_send()
                    total = acc[...]
                    if k in W_IN_SLOT and k >= 2:
                        p = k // 2 - 1
                        push(SCATTER_ORDER.index(k + 1)).wait_recv()
                        total = total + partner[p].astype(F32)
                    stage[step % 2] = total.astype(BF16)
                    push(step).start()

        @pl.when((j == N_DEV - 1) & (kk == nk - 1))
        def _():
            push(N_DEV - 2).wait_send()
            push(N_DEV - 1).wait()
            for k in (1, 2, 4, 6):
                push(SCATTER_ORDER.index(k)).wait_recv()
            for tn in range(1, nt):
                small_local(tn).wait()
                for k in range(1, N_DEV):
                    small_remote(tn, k, False).wait_recv()
                    small_remote(tn, k, True).wait_send()

    grid_spec = pltpu.PrefetchScalarGridSpec(
        num_scalar_prefetch=1, grid=(N_DEV, nk),
        in_specs=[pl.BlockSpec((D, tk), lambda j, kk, order_ref: (0, kk)),
                  pl.BlockSpec((tk, n_blk), lambda j, kk, order_ref: (kk, order_ref[j]))] + [ANY] * (nt - 1),
        out_specs=[ANY] * nt,
        scratch_shapes=[pltpu.VMEM((D, n_blk), F32), pltpu.VMEM((2, D, n_blk), BF16), pltpu.VMEM((3, D, n_blk), BF16),
                        pltpu.SemaphoreType.DMA((nt, N_DEV - 1)), pltpu.SemaphoreType.DMA((nt, N_DEV - 1)),
                        pltpu.SemaphoreType.DMA((nt,)), pltpu.SemaphoreType.DMA((3,)), pltpu.SemaphoreType.DMA((3,))],
    )
    return pl.pallas_call(body, name="dw_in_scatter", grid_spec=grid_spec, out_shape=slot_shapes)(order, ht, dproj, *small_grads)


def _adamw_math(w, g, m, v):
    m = ADAM_B1 * m + (1.0 - ADAM_B1) * g
    v = ADAM_B2 * v + (1.0 - ADAM_B2) * (g * g)
    m_hat = m / (1.0 - ADAM_B1 ** ADAM_STEP)
    v_hat = v / (1.0 - ADAM_B2 ** ADAM_STEP)
    delta = -ADAM_LR * (m_hat / (jnp.sqrt(v_hat) + ADAM_EPS) + ADAM_WD * w)
    return delta, m, v


def _sum_adamw(slots, w, m, v, tr, name):
    n_slots, r, c = slots.shape
    assert r % tr == 0

    def body(s_ref, w_ref, m_ref, v_ref, g_ref, d_ref, nm_ref, nv_ref):
        g = s_ref[0].astype(F32)
        for j in range(1, n_slots):
            g = g + s_ref[j].astype(F32)
        delta, nm, nv = _adamw_math(w_ref[...], g, m_ref[...], v_ref[...])
        g_ref[...] = g
        d_ref[...] = delta
        nm_ref[...] = nm
        nv_ref[...] = nv

    blk = pl.BlockSpec((tr, c), lambda i: (i, 0))
    return pl.pallas_call(
        body, name=name, grid=(r // tr,),
        in_specs=[pl.BlockSpec((n_slots, tr, c), lambda i: (0, i, 0)), blk, blk, blk],
        out_specs=[blk] * 4, out_shape=[jax.ShapeDtypeStruct((r, c), F32)] * 4,
    )(slots, w, m, v)


def _adamw_small(g, w, m, v, name):
    def body(g_ref, w_ref, m_ref, v_ref, d_ref, nm_ref, nv_ref):
        delta, nm, nv = _adamw_math(w_ref[...], g_ref[...], m_ref[...], v_ref[...])
        d_ref[...] = delta
        nm_ref[...] = nm
        nv_ref[...] = nv

    spec = _full(g.shape)
    return pl.pallas_call(body, name=name, grid=(1,), in_specs=[spec] * 4, out_specs=[spec] * 3,
                          out_shape=[jax.ShapeDtypeStruct(g.shape, F32)] * 3)(g, w, m, v)


def _mod_part(c_all, w_ada_l, b_ada_l):
    n = w_ada_l.shape[1]

    def body(c_ref, w_ref, b_ref, o_ref):
        o_ref[...] = jnp.dot(c_ref[...], w_ref[...], preferred_element_type=F32,
                             precision=lax.Precision.HIGHEST) + b_ref[...]

    return pl.pallas_call(body, name="mod_part", grid=(1,),
                          in_specs=[_full(c_all.shape), _full(w_ada_l.shape), _full(b_ada_l.shape)],
                          out_specs=_full((N_DEV, n)), out_shape=jax.ShapeDtypeStruct((N_DEV, n), F32))(c_all, w_ada_l, b_ada_l)


def _w_ada_update(c_all_t, dmod_cols, w, m, v):
    def body(c_ref, dm_ref, w_ref, m_ref, v_ref, g_ref, d_ref, nm_ref, nv_ref):
        g = c_ref[:, 0:1] * dm_ref[0:1, :]
        for b in range(1, N_DEV):
            g = g + c_ref[:, b:b + 1] * dm_ref[b:b + 1, :]
        delta, nm, nv = _adamw_math(w_ref[...], g, m_ref[...], v_ref[...])
        g_ref[...] = g
        d_ref[...] = delta
        nm_ref[...] = nm
        nv_ref[...] = nv

    spec = _full(w.shape)
    return pl.pallas_call(body, name="w_ada_update", grid=(1,),
                          in_specs=[_full(c_all_t.shape), _full(dmod_cols.shape), spec, spec, spec],
                          out_specs=[spec] * 4, out_shape=[jax.ShapeDtypeStruct(w.shape, F32)] * 4)(c_all_t, dmod_cols, w, m, v)


def _cast_bf16(w, name):
    def body(w_ref, o_ref):
        o_ref[...] = w_ref[...].astype(BF16)

    return pl.pallas_call(body, name=name, grid=(1,), in_specs=[_full(w.shape)], out_specs=_full(w.shape),
                          out_shape=jax.ShapeDtypeStruct(w.shape, BF16))(w)


def _prenorm(x, mod, norm_g, tm):
    t = x.shape[0]

    def body(x_ref, mod_ref, g_ref, h_ref, ht_ref):
        xv = x_ref[...]
        r = lax.rsqrt(jnp.mean(xv * xv, axis=-1, keepdims=True) + EPS)
        h = (xv * r) * g_ref[...] * (1.0 + mod_ref[:, D:2 * D]) + mod_ref[:, 0:D]
        h_ref[...] = h.astype(BF16)
        ht_ref[...] = h.T.astype(BF16)

    return pl.pallas_call(body, name="prenorm", grid=(t // tm,),
                          in_specs=[_rows(tm, D), _full((1, 3 * D)), _full((1, D))],
                          out_specs=[_rows(tm, D), pl.BlockSpec((D, tm), lambda i: (0, i))],
                          out_shape=[jax.ShapeDtypeStruct((t, D), BF16), jax.ShapeDtypeStruct((D, t), BF16)])(x, mod, norm_g)


def _rope_apply(t, cos, s_lo, s_hi):
    return t * cos + pltpu.roll(t, 120, 1) * s_lo + pltpu.roll(t, 8, 1) * s_hi


def _shift_copies(sh, buf, c0):
    rows = buf.shape[0] - 8
    for s in range(1, 8):
        sh[s, 0:rows, :] = buf[s:s + rows, pl.ds(c0, 128)]


def _window64(buf, sh, c0, start):
    s = start % 8
    if s == 0:
        return buf[start:start + 64, pl.ds(c0, 128)]
    return sh[s, start - s:start - s + 64, :]


def _conv_taps(acc_init, w_ref, buf, sh, row0, c0, offset_of_tap):
    acc = acc_init
    for j in range(CONV_K):
        acc = acc + w_ref[j:j + 1, pl.ds(c0, 128)] * _window64(buf, sh, c0, row0 + offset_of_tap(j))
    return acc


def _conv_fwd(proj, conv_w, conv_b, ln_g, ln_b, tm):
    t = proj.shape[0]
    hb = tm // HALO

    def body(a_ref, b_ref, z_ref, ah_ref, bh_ref, w_ref, cb_ref, lg_ref, lb_ref, u1_ref, pc_ref, ubuf, sh):
        i = pl.program_id(0)
        u0h = ah_ref[...] * _sig(bh_ref[...])
        ubuf[0:HALO, :] = jnp.where(i > 0, u0h, 0.0)
        ubuf[HALO:HALO + tm, :] = a_ref[...] * _sig(b_ref[...])

        def col(ci, carry):
            c0 = pl.multiple_of(ci * 128, 128)
            _shift_copies(sh, ubuf, c0)
            for rc in range(tm // 64):
                init = jnp.zeros((64, 128), F32)
                acc = _conv_taps(init, w_ref, ubuf, sh, rc * 64, c0, lambda j: HALO - (CONV_K - 1) + j)
                u1_ref[rc * 64:(rc + 1) * 64, pl.ds(c0, 128)] = acc + cb_ref[:, pl.ds(c0, 128)]
            return carry

        lax.fori_loop(0, D // 128, col, 0)
        u1 = u1_ref[...]
        mu = jnp.mean(u1, axis=-1, keepdims=True)
        xc = u1 - mu
        var = jnp.mean(xc * xc, axis=-1, keepdims=True)
        u2 = xc * lax.rsqrt(var + EPS) * lg_ref[...] + lb_ref[...]
        z = z_ref[...]
        pc_ref[...] = (u2 * _sig(u2) * (z * _sig(z))).astype(BF16)

    halo = pl.BlockSpec((HALO, D), lambda i: (jnp.maximum(i * hb - 1, 0), 0))
    halo_b = pl.BlockSpec((HALO, D), lambda i: (jnp.maximum(i * hb - 1, 0), 1))
    return pl.pallas_call(
        body, name="conv_fwd", grid=(t // tm,),
        in_specs=[_rows(tm, D, 0), _rows(tm, D, 1), _rows(tm, D, 2), halo, halo_b,
                  _full((CONV_KP, D)), _full((1, D)), _full((1, D)), _full((1, D))],
        out_specs=[_rows(tm, D), _rows(tm, D)],
        out_shape=[jax.ShapeDtypeStruct((t, D), F32), jax.ShapeDtypeStruct((t, D), BF16)],
        scratch_shapes=[pltpu.VMEM((HALO + tm, D), F32), pltpu.VMEM((8, HALO + tm, 128), F32)],
    )(proj, proj, proj, proj, proj, conv_w, conv_b, ln_g, ln_b)


def _band_masks_t(has_prev):
    key = lax.broadcasted_iota(jnp.int32, (BLK, BLK), 0)
    qry = lax.broadcasted_iota(jnp.int32, (BLK, BLK), 1)
    return jnp.logical_and(key >= qry, has_prev), key <= qry


def _head_lanes(pair, hh):
    lane = lax.broadcasted_iota(jnp.int32, pair.shape, 1)
    return jnp.where((lane >= hh * HEAD) & (lane < (hh + 1) * HEAD), pair, jnp.zeros_like(pair))


def _pair_mask(has_prev):
    mask_p, mask_c = _band_masks_t(has_prev)
    both = jnp.concatenate([mask_p, mask_c], axis=0)
    return jnp.concatenate([both, both], axis=1)


def _query_pair(pair):
    return jnp.concatenate([_head_lanes(pair, 0), _head_lanes(pair, 1)], axis=0)


def _key_pair(ref, prev, cur):
    return jnp.concatenate([ref[pl.ds(prev, BLK), :], ref[pl.ds(cur, BLK), :]], axis=0)


def _own_head(both):
    return jnp.concatenate([both[0:HEAD, 0:BLK], both[HEAD:2 * HEAD, BLK:2 * BLK]], axis=0)


def _store_transposed(dst, base, src):
    for j in range(TILE // BLK):
        dst[base // BLK + j] = src[j * BLK:(j + 1) * BLK, :].T.astype(BF16)


class _Dilated:
    def __init__(self, dil):
        self.dil = dil
        self.per = TILE // dil
        self.nbr = self.per // BLK

    def spread(self, dst, base, src_ref, dtype):
        for r in range(self.dil):
            rows = src_ref[pl.ds(r, self.per, stride=self.dil), :] if self.dil > 1 else src_ref[...]
            dst[pl.ds(pl.multiple_of(base + r * self.per, BLK), self.per), :] = rows.astype(dtype)

    def gather(self, dst_ref, src, base):
        for r in range(self.dil):
            rows = src[pl.ds(pl.multiple_of(base + r * self.per, BLK), self.per), :]
            if self.dil > 1:
                dst_ref[pl.ds(r, self.per, stride=self.dil), :] = rows
            else:
                dst_ref[...] = rows

    def block_rows(self, b, i, cur, prv):
        n = b % self.nbr
        row = pl.multiple_of(b * BLK, BLK)
        has_prev = jnp.logical_or(n > 0, i > 0)
        prev = jnp.where(n > 0, cur + row - BLK, jnp.where(i > 0, prv + row + (self.nbr - 1) * BLK, cur + row))
        return row, pl.multiple_of(prev, BLK), has_prev


def _slots(i):
    return pl.multiple_of((i % 2) * TILE, TILE), pl.multiple_of(((i + 1) % 2) * TILE, TILE)


def _nt(a, b):
    return lax.dot_general(a, b, (((1,), (1,)), ((), ())), preferred_element_type=F32)


def _qkv_specs(gi, clamp_to=None):
    def spec(col0):
        def imap(hp, i):
            return (i if clamp_to is None else jnp.minimum(i, clamp_to), (col0 + gi * ATT) // 128 + hp)
        return pl.BlockSpec((TILE, 128), imap)
    return [spec(C_Q), spec(C_K), spec(C_V)]


def _att_fwd(proj, tables):
    t = proj.shape[0]

    def body(*refs):
        qkv_refs, (c_ref, lo_ref, hi_ref, att_ref, lse_ref, tmp, qd, od, ld), kv_scratch = refs[:9], refs[9:18], refs[18:]
        i = pl.program_id(1)
        cur, prv = _slots(i)
        cs, lo, hi = c_ref[...], lo_ref[...], hi_ref[...]
        for gi, dil in GROUPS:
            dl = _Dilated(dil)
            q_ref, k_ref, v_ref = qkv_refs[3 * gi:3 * gi + 3]
            kd, vt = kv_scratch[2 * gi:2 * gi + 2]
            tmp[...] = _rope_apply(q_ref[...], cs, lo, hi) * SM_SCALE
            dl.spread(qd, 0, tmp, BF16)
            tmp[...] = _rope_apply(k_ref[...], cs, lo, hi)
            dl.spread(kd, cur, tmp, BF16)
            dl.spread(tmp, 0, v_ref, F32)
            _store_transposed(vt, cur, tmp)

            def block(b, carry, dl=dl, kd=kd, vt=vt):
                row, prev, has_prev = dl.block_rows(b, i, cur, prv)
                s = jnp.where(_pair_mask(has_prev), _nt(_key_pair(kd, prev, cur + row), _query_pair(qd[pl.ds(row, BLK), :])), NEG_INF)
                mx = jnp.max(s, axis=0, keepdims=True)
                p = jnp.exp(s - mx)
                den = jnp.sum(p, axis=0, keepdims=True)
                v_t = jnp.concatenate([vt[prev // BLK], vt[(cur + row) // BLK]], axis=1)
                acc = jnp.dot(v_t, p.astype(BF16), preferred_element_type=F32) / den
                lse = mx + jnp.log(den)
                od[pl.ds(row, BLK), :] = _own_head(acc).T
                ld[pl.ds(row, BLK), :] = _own_head(jnp.broadcast_to(lse, (2 * HEAD, 2 * BLK))).T
                return carry

            lax.fori_loop(0, TILE // BLK, block, 0, unroll=True)
            if gi == 0:
                dl.gather(att_ref, od, 0)
                dl.gather(lse_ref, ld, 0)
            else:
                dl.gather(tmp, ld, 0)
                l_run, l_new = lse_ref[...], tmp[...]
                m = jnp.maximum(l_run, l_new)
                w_run, w_new = jnp.exp(l_run - m), jnp.exp(l_new - m)
                lse_ref[...] = m + jnp.log(w_run + w_new)
                ld[...] = w_new / (w_run + w_new)
                dl.gather(tmp, od, 0)
                share = ld[...]
                att_ref[...] = att_ref[...] + share * (tmp[...] - att_ref[...])

    tab = pl.BlockSpec((TILE, 128), lambda hp, i: (i, 0))
    out_spec = pl.BlockSpec((TILE, 128), lambda hp, i: (i, hp))
    kv_shapes = [pltpu.VMEM((2 * TILE, 128), BF16), pltpu.VMEM((2 * TILE // BLK, 128, BLK), BF16)] * len(GROUPS)
    return pl.pallas_call(
        body, name="att_fwd", grid=(ATT // 128, t // TILE),
        in_specs=[spec for gi, _ in GROUPS for spec in _qkv_specs(gi)] + [tab] * 3,
        out_specs=[out_spec] * 2, out_shape=[jax.ShapeDtypeStruct((t, ATT), F32)] * 2,
        scratch_shapes=[pltpu.VMEM((TILE, 128), F32), pltpu.VMEM((TILE, 128), BF16), pltpu.VMEM((TILE, 128), F32),
                        pltpu.VMEM((TILE, 128), F32)] + kv_shapes,
    )(*([proj] * (3 * len(GROUPS))), *tables)


def _att_bwd(proj, tables, datt, dsum, lse, gi, dil):
    t = proj.shape[0]
    nt = t // TILE
    dl = _Dilated(dil)

    def body(q_ref, k_ref, v_ref, c_ref, lo_ref, hi_ref, cl_ref, lol_ref, hil_ref, do_ref, ds_ref, lse_ref,
             dq_ref, dk_ref, dv_ref, tmp, qd, kd, vd, dod, dsd, lsd, dqd, dkd, dvd, kt):
        i = pl.program_id(1)
        cur, prv = _slots(i)

        @pl.when(i < nt)
        def _():
            cs, lo, hi = c_ref[...], lo_ref[...], hi_ref[...]
            tmp[...] = _rope_apply(q_ref[...], cs, lo, hi) * SM_SCALE
            dl.spread(qd, 0, tmp, BF16)
            tmp[...] = _rope_apply(k_ref[...], cs, lo, hi)
            dl.spread(kd, cur, tmp, BF16)
            dl.spread(dqd, 0, tmp, F32)
            _store_transposed(kt, cur, dqd)
            dl.spread(vd, cur, v_ref, BF16)
            dl.spread(dod, 0, do_ref, BF16)
            dl.spread(dsd, 0, ds_ref, F32)
            dl.spread(lsd, 0, lse_ref, F32)

            def block(b, carry):
                row, prev, has_prev = dl.block_rows(b, i, cur, prv)
                q_pair, do_pair = _query_pair(qd[pl.ds(row, BLK), :]), _query_pair(dod[pl.ds(row, BLK), :])
                k_pair, v_pair = _key_pair(kd, prev, cur + row), _key_pair(vd, prev, cur + row)
                ds_t, ls_t = dsd[pl.ds(row, BLK), :].T, lsd[pl.ds(row, BLK), :].T
                lse = jnp.concatenate([ls_t[0:1, :], ls_t[HEAD:HEAD + 1, :]], axis=1)
                dsm = jnp.concatenate([ds_t[0:1, :], ds_t[HEAD:HEAD + 1, :]], axis=1)
                p = jnp.exp(jnp.where(_pair_mask(has_prev), _nt(k_pair, q_pair), NEG_INF) - lse)
                ds = (p * (_nt(v_pair, do_pair) - dsm)).astype(BF16)
                k_t = jnp.concatenate([kt[prev // BLK], kt[(cur + row) // BLK]], axis=1)
                dqd[pl.ds(row, BLK), :] = _own_head(jnp.dot(k_t, ds, preferred_element_type=F32)).T * SM_SCALE
                dk = jnp.dot(ds, q_pair, preferred_element_type=F32)
                dv = jnp.dot(p.astype(BF16), do_pair, preferred_element_type=F32)
                dkd[pl.ds(cur + row, BLK), :] = dk[BLK:2 * BLK, :]
                dvd[pl.ds(cur + row, BLK), :] = dv[BLK:2 * BLK, :]
                dkd[pl.ds(prev, BLK), :] += dk[0:BLK, :]
                dvd[pl.ds(prev, BLK), :] += dv[0:BLK, :]
                return carry

            lax.fori_loop(0, TILE // BLK, block, 0, unroll=True)
            dl.gather(tmp, dqd, 0)
            dq_ref[...] = _rope_apply(tmp[...], cs, -lo, -hi).astype(BF16)

        @pl.when(i > 0)
        def _():
            dl.gather(tmp, dkd, prv)
            dk_ref[...] = _rope_apply(tmp[...], cl_ref[...], -lol_ref[...], -hil_ref[...]).astype(BF16)
            dl.gather(tmp, dvd, prv)
            dv_ref[...] = tmp[...].astype(BF16)

    now = lambda col: pl.BlockSpec((TILE, 128), lambda hp, i: (jnp.minimum(i, nt - 1), col(hp)))
    lag = lambda col: pl.BlockSpec((TILE, 128), lambda hp, i: (jnp.maximum(i - 1, 0), col(hp)))
    first, pair = (lambda hp: 0), (lambda hp: hp)
    return pl.pallas_call(
        body, name=f"att_bwd_g{gi}", grid=(ATT // 128, nt + 1),
        in_specs=_qkv_specs(gi, nt - 1) + [now(first)] * 3 + [lag(first)] * 3 + [now(pair)] * 3,
        out_specs=[now(pair), lag(pair), lag(pair)],
        out_shape=[jax.ShapeDtypeStruct((t, ATT), BF16)] * 3,
        scratch_shapes=[pltpu.VMEM((TILE, 128), F32), pltpu.VMEM((TILE, 128), BF16), pltpu.VMEM((2 * TILE, 128), BF16),
                        pltpu.VMEM((2 * TILE, 128), BF16), pltpu.VMEM((TILE, 128), BF16), pltpu.VMEM((TILE, 128), F32),
                        pltpu.VMEM((TILE, 128), F32), pltpu.VMEM((TILE, 128), F32), pltpu.VMEM((2 * TILE, 128), F32),
                        pltpu.VMEM((2 * TILE, 128), F32), pltpu.VMEM((2 * TILE // BLK, 128, BLK), BF16)],
    )(proj, proj, proj, *tables, *tables, datt, dsum, lse)


def _merge_head(pc, att, proj, x, mod, final_g, target, w_co, w_ao, w_o, u1, ln_g, ln_b, tm):
    t = x.shape[0]

    def body(pc_ref, att_ref, gc_ref, ga_ref, x_ref, mod_ref, fg_ref, tg_ref, wco_ref, wao_ref, wo_ref,
             zc_ref, u1_ref, lg_ref, lb_ref, za_ref,
             merged_ref, do_ref, dyc_ref, dya_ref, dout_ref, du1_ref, dzc_ref, datt_ref, ds_ref, dp_ref, pa_ref,
             sq_ref, gfg_ref, dgate_ref, dlg_ref, dlb_ref, dcb_ref):
        i = pl.program_id(0)
        att_v = att_ref[...]
        za = za_ref[...]
        sza = _sig(za)
        pa = (att_v * (za * sza)).astype(BF16)
        pa_ref[...] = pa
        yc = jnp.dot(pc_ref[...], wco_ref[...], preferred_element_type=F32)
        ya = jnp.dot(pa, wao_ref[...], preferred_element_type=F32)
        sc, sa = _sig(gc_ref[...]), _sig(ga_ref[...])
        merged = (sc * yc + sa * ya).astype(BF16)
        merged_ref[...] = merged
        ov = jnp.dot(merged, wo_ref[...], preferred_element_type=F32)
        gate = mod_ref[:, 2 * D:3 * D]
        out = x_ref[...] + gate * ov
        r = lax.rsqrt(jnp.mean(out * out, axis=-1, keepdims=True) + EPS)
        yn = out * r
        diff = yn * fg_ref[...] - tg_ref[...]
        dy = diff * (1.0 / D)
        gy = dy * fg_ref[...]
        dout = r * (gy - yn * jnp.mean(gy * yn, axis=-1, keepdims=True))
        dout_ref[...] = dout
        do = (dout * gate).astype(BF16)
        do_ref[...] = do
        _acc_rows(sq_ref, i, diff * diff)
        _acc_rows(gfg_ref, i, dy * yn)
        _acc_rows(dgate_ref, i, dout * ov)
        dm = _nt(do, wo_ref[...])
        dyc = (dm * sc).astype(BF16)
        dya = (dm * sa).astype(BF16)
        dyc_ref[...] = dyc
        dya_ref[...] = dya
        dp_ref[:, ATT:ATT + D] = (dm * yc * sc * (1.0 - sc)).astype(BF16)
        dp_ref[:, ATT + D:ATT + 2 * D] = (dm * ya * sa * (1.0 - sa)).astype(BF16)

        dpc = _nt(dyc, wco_ref[...])
        u1v = u1_ref[...]
        xc = u1v - jnp.mean(u1v, axis=-1, keepdims=True)
        rs = lax.rsqrt(jnp.mean(xc * xc, axis=-1, keepdims=True) + EPS)
        uhat = xc * rs
        u2 = uhat * lg_ref[...] + lb_ref[...]
        s2 = _sig(u2)
        zc = zc_ref[...]
        szc = _sig(zc)
        dzc_ref[...] = (dpc * (u2 * s2) * _dsilu(zc, szc)).astype(BF16)
        du2 = dpc * (zc * szc) * _dsilu(u2, s2)
        duhat = du2 * lg_ref[...]
        du1 = rs * (duhat - jnp.mean(duhat, axis=-1, keepdims=True) - uhat * jnp.mean(duhat * uhat, axis=-1, keepdims=True))
        du1_ref[...] = du1
        _acc_rows(dlg_ref, i, du2 * uhat)
        _acc_rows(dlb_ref, i, du2)
        _acc_rows(dcb_ref, i, du1)

        dpa = _nt(dya, wao_ref[...])
        datt = dpa * (za * sza)
        datt_ref[...] = datt
        dp_ref[:, 0:ATT] = (dpa * att_v * _dsilu(za, sza)).astype(BF16)
        prod = datt * att_v
        for hd in range(ATT // HEAD):
            sl = slice(hd * HEAD, (hd + 1) * HEAD)
            ds_ref[:, sl] = jnp.broadcast_to(jnp.sum(prod[:, sl], axis=-1, keepdims=True), (tm, HEAD))

    vec = _full((1, D))
    bf = lambda w: jax.ShapeDtypeStruct((t, w), BF16)
    f32 = lambda w: jax.ShapeDtypeStruct((t, w), F32)
    tail = ATT + 2 * D
    return pl.pallas_call(
        body, name="merge_head", grid=(t // tm,),
        in_specs=[_rows(tm, D), _rows(tm, ATT), _rows(tm, D, C_GC // D), _rows(tm, D, C_GA // D), _rows(tm, D),
                  _full((1, 3 * D)), vec, _rows(tm, D), _full((D, D)), _full((ATT, D)), _full((D, D)),
                  _rows(tm, D, C_ZC // D), _rows(tm, D), vec, vec, _rows(tm, ATT, C_ZA // ATT)],
        out_specs=[_rows(tm, D)] * 7 + [_rows(tm, ATT), _rows(tm, ATT), _rows(tm, tail, C_ZA // tail),
                                        _rows(tm, ATT)] + [vec] * 6,
        out_shape=[bf(D), bf(D), bf(D), bf(D), f32(D), f32(D), bf(D), f32(ATT), f32(ATT), bf(N_COL), bf(ATT)]
        + [jax.ShapeDtypeStruct((1, D), F32)] * 6,
    )(pc, att, proj, proj, x, mod, final_g, target, w_co, w_ao, w_o, proj, u1, ln_g, ln_b, proj)


def _acc_rows(ref, i, val):
    @pl.when(i == 0)
    def _():
        ref[...] = jnp.zeros_like(ref)

    ref[...] += jnp.sum(val, axis=0, keepdims=True)


def _conv_bwd_taps(du1, proj, conv_w, dzc, dqkv, dproj, tm):
    t = proj.shape[0]
    hb = tm // HALO
    last = t // HALO - 1

    def body(du_ref, duh_ref, a_ref, b_ref, ah_ref, bh_ref, w_ref, dzc_ref, *rest):
        qkv_refs, (dp_in, dp_ref, dw_ref, dbuf, ubuf, g0, shd, shu) = rest[:9], rest[9:]
        del dp_in
        dp_ref[:, C_ZC:C_ZC + D] = dzc_ref[...]
        for n, ref in enumerate(qkv_refs):
            dp_ref[:, C_Q + n * ATT:C_Q + (n + 1) * ATT] = ref[...]
        i = pl.program_id(0)
        a, sb = a_ref[...], _sig(b_ref[...])
        ubuf[0:HALO, :] = jnp.where(i > 0, ah_ref[...] * _sig(bh_ref[...]), 0.0)
        ubuf[HALO:HALO + tm, :] = a * sb
        dbuf[0:tm, :] = du_ref[...]
        dbuf[tm:tm + HALO, :] = jnp.where(i < pl.num_programs(0) - 1, duh_ref[...], 0.0)

        @pl.when(i == 0)
        def _():
            dw_ref[...] = jnp.zeros_like(dw_ref)

        def col(ci, carry):
            c0 = pl.multiple_of(ci * 128, 128)
            _shift_copies(shd, dbuf, c0)
            _shift_copies(shu, ubuf, c0)
            for rc in range(tm // 64):
                g0[rc * 64:(rc + 1) * 64, pl.ds(c0, 128)] = _conv_taps(
                    jnp.zeros((64, 128), F32), w_ref, dbuf, shd, rc * 64, c0, lambda j: CONV_K - 1 - j)
            for j in range(CONV_K):
                part = jnp.zeros((8, 128), F32)
                for rc in range(tm // 64):
                    off = rc * 64 + HALO - (CONV_K - 1) + j
                    prod = dbuf[rc * 64:(rc + 1) * 64, pl.ds(c0, 128)] * _window64(ubuf, shu, c0, off)
                    part = part + jnp.sum(prod.reshape(8, 8, 128), axis=0)
                dw_ref[j:j + 1, pl.ds(c0, 128)] += jnp.sum(part, axis=0, keepdims=True)
            return carry

        lax.fori_loop(0, D // 128, col, 0)
        du0 = g0[...]
        dp_ref[:, 0:D] = (du0 * sb).astype(BF16)
        dp_ref[:, D:2 * D] = (du0 * a * sb * (1.0 - sb)).astype(BF16)

    prev = lambda col: pl.BlockSpec((HALO, D), lambda i: (jnp.maximum(i * hb - 1, 0), col))
    nxt = pl.BlockSpec((HALO, D), lambda i: (jnp.minimum((i + 1) * hb, last), 0))
    return pl.pallas_call(
        body, name="conv_bwd_taps", grid=(t // tm,),
        in_specs=[_rows(tm, D), nxt, _rows(tm, D, 0), _rows(tm, D, 1), prev(0), prev(1), _full((CONV_KP, D)),
                  _rows(tm, D)] + [_rows(tm, ATT)] * 9 + [ANY],
        out_specs=[_rows(tm, C_ZA, 0), _full((CONV_KP, D))],
        out_shape=[jax.ShapeDtypeStruct((t, N_COL), BF16), jax.ShapeDtypeStruct((CONV_KP, D), F32)],
        scratch_shapes=[pltpu.VMEM((tm + HALO, D), F32), pltpu.VMEM((HALO + tm, D), F32), pltpu.VMEM((tm, D), F32),
                        pltpu.VMEM((8, HALO + tm, 128), F32), pltpu.VMEM((8, HALO + tm, 128), F32)],
        input_output_aliases={17: 0},
    )(du1, du1, proj, proj, proj, proj, conv_w, dzc, *dqkv, dproj)


def _dh_prenorm_bwd(dproj, w_in_blocks, x, dout, mod, norm_g, tm):
    t = x.shape[0]
    nk, _, tk = w_in_blocks.shape

    def body(dp_ref, w_hbm, x_ref, dout_ref, mod_ref, g_ref, gx_ref, dshift_ref, dscale_ref, dg_ref, acc, wbuf, w_sems):
        i, kk = pl.program_id(0), pl.program_id(1)
        step = i * nk + kk
        last_step = (t // tm) * nk - 1

        def fetch(s, slot):
            return pltpu.make_async_copy(w_hbm.at[s % nk], wbuf.at[slot], w_sems.at[slot])

        @pl.when(step == 0)
        def _():
            fetch(0, 0).start()
            fetch(1, 1).start()

        for slot in range(3):
            @pl.when(step % 3 == slot)
            def _():
                fetch(step, slot).wait()

                @pl.when(step + 2 <= last_step)
                def _():
                    fetch(step + 2, (slot + 2) % 3).start()

        p = _nt(dp_ref[...], wbuf[step % 3])

        @pl.when(kk == 0)
        def _():
            acc[...] = p

        @pl.when(kk > 0)
        def _():
            acc[...] += p

        @pl.when(kk == nk - 1)
        def _():
            xv, dhv = x_ref[...], acc[...]
            r = lax.rsqrt(jnp.mean(xv * xv, axis=-1, keepdims=True) + EPS)
            xn = xv * r
            one_scale = 1.0 + mod_ref[:, D:2 * D]
            dxn = dhv * (g_ref[...] * one_scale)
            gx_ref[...] = r * (dxn - xn * jnp.mean(dxn * xn, axis=-1, keepdims=True)) + dout_ref[...]
            _acc_rows(dshift_ref, i, dhv)
            _acc_rows(dscale_ref, i, dhv * xn * g_ref[...])
            _acc_rows(dg_ref, i, dhv * xn * one_scale)

    row = pl.BlockSpec((tm, D), lambda i, kk: (i, 0))
    vec = pl.BlockSpec((1, D), lambda i, kk: (0, 0))
    return pl.pallas_call(
        body, name="dh_prenorm_bwd", grid=(t // tm, nk),
        in_specs=[pl.BlockSpec((tm, tk), lambda i, kk: (i, kk)), ANY,
                  row, row, pl.BlockSpec((1, 3 * D), lambda i, kk: (0, 0)), vec],
        out_specs=[row, vec, vec, vec],
        out_shape=[jax.ShapeDtypeStruct((t, D), F32)] + [jax.ShapeDtypeStruct((1, D), F32)] * 3,
        scratch_shapes=[pltpu.VMEM((tm, D), F32), pltpu.VMEM((3, D, tk), BF16), pltpu.SemaphoreType.DMA((3,))],
    )(dproj, w_in_blocks, x, dout, mod, norm_g)


def _sum_devices(gathered):
    w = gathered.shape[-1]

    def body(g_ref, o_ref):
        acc = g_ref[0]
        for j in range(1, N_DEV):
            acc = acc + g_ref[j]
        o_ref[...] = acc

    return pl.pallas_call(body, name="sum_devices", grid=(1,), in_specs=[_full(gathered.shape)], out_specs=_full((1, w)),
                          out_shape=jax.ShapeDtypeStruct((1, w), F32))(gathered)


def _rope_tables(positions):
    half = HEAD // 8
    t = positions.shape[-1]
    inv_freq = ROPE_THETA ** (-(jnp.arange(half, dtype=F32) * 2.0 / (2 * half)))
    ang = positions.reshape(t, 1).astype(F32) * inv_freq
    cos, sin = jnp.cos(ang), jnp.sin(ang)
    zeros = lambda n: jnp.zeros((t, n), F32)
    c64 = jnp.concatenate([cos, cos, jnp.ones((t, HEAD - 2 * half), F32)], axis=1)
    lo64 = jnp.concatenate([-sin, zeros(HEAD - half)], axis=1)
    hi64 = jnp.concatenate([zeros(half), sin, zeros(HEAD - 2 * half)], axis=1)
    return tuple(jnp.tile(a, (1, 2)) for a in (c64, lo64, hi64))


def kernel(x, c, positions, norm_g, w_ada, b_ada, w_in, conv_w, conv_b, conv_ln_g, conv_ln_b, w_conv_out, w_att_out, w_o, final_g, loss_target, m_norm_g, m_w_ada, m_b_ada, m_w_in, m_conv_w, m_conv_b, m_conv_ln_g, m_conv_ln_b, m_w_conv_out, m_w_att_out, m_w_o, m_final_g, v_norm_g, v_w_ada, v_b_ada, v_w_in, v_conv_w, v_conv_b, v_conv_ln_g, v_conv_ln_b, v_w_conv_out, v_w_att_out, v_w_o, v_final_g):
    me = 4 * lax.axis_index("x") + 2 * lax.axis_index("y") + lax.axis_index("c")
    x2, tgt = x[0], loss_target[0]
    t = x2.shape[0]
    te = 512 if t % 512 == 0 else 256
    tcv = 256
    tmh = 256
    tmm = 1024 if t % 1024 == 0 else 256
    n_ada = w_ada.shape[-1]

    pad_taps = lambda a: jnp.pad(a[0], ((0, CONV_KP - CONV_K), (0, 0)))
    shards = (_cast_bf16(w_in[0], "cast_w_in"), _cast_bf16(w_conv_out[0], "cast_w_conv_out"),
              _cast_bf16(w_att_out[0], "cast_w_att_out"), _cast_bf16(w_o[0], "cast_w_o"), pad_taps(conv_w))
    block_of = lambda relations: jnp.bitwise_xor(me, jnp.array(relations, jnp.int32))

    c_all = _allgather_small(c, "gather_c").reshape(N_DEV, D)
    b_ada_l = lax.dynamic_slice(b_ada, (0, me * n_ada), (1, n_ada))
    parts = _allgather_small(_mod_part(c_all, w_ada[0], b_ada_l), "gather_mod")
    mod = lax.dynamic_slice(parts, (0, me, 0), (N_DEV, 1, n_ada)).reshape(1, N_DEV * n_ada)

    h, ht = _prenorm(x2, mod, norm_g, te)
    proj, w_in_f, w_co_f, w_ao_f, w_o_f, conv_w_f = _proj_gather(h, shards, block_of(GATHER_ORDER), tmm)
    u1, pc = _conv_fwd(proj, conv_w_f, conv_b, conv_ln_g, conv_ln_b, tcv)
    tables = _rope_tables(positions)
    att, lse = _att_fwd(proj, tables)

    (merged, do, dyc, dya, dout, du1, dzc, datt, dsum, dproj, pa,
     sq_sum, g_final, d_gate, d_ln_g, d_ln_b, d_conv_b) = _merge_head(
        pc, att, proj, x2, mod, final_g.reshape(1, D), tgt, w_co_f, w_ao_f, w_o_f, u1, conv_ln_g, conv_ln_b, tmh)

    tkw = 2048 if t % 2048 == 0 else 256
    dw_o = _matmul(merged, do, ta=True, out_dtype=BF16, tm=D, tn=D, tk=tkw, name="dw_o")
    dw_co = _matmul(pc, dyc, ta=True, out_dtype=BF16, tm=D, tn=D, tk=tkw, name="dw_conv_out")
    dw_ao = _matmul(pa, dya, ta=True, out_dtype=BF16, tm=ATT, tn=D, tk=tkw, name="dw_att_out")
    dqs, dks, dvs = [], [], []
    for gi, dil in GROUPS:
        dq, dk, dv = _att_bwd(proj, tables, datt, dsum, lse, gi, dil)
        dqs.append(dq), dks.append(dk), dvs.append(dv)
    dproj, dconv_w = _conv_bwd_taps(du1, proj, conv_w_f, dzc, dqs + dks + dvs, dproj, tcv)
    grad_x, d_shift, d_scale, d_norm_g = _dh_prenorm_bwd(dproj, w_in_f, x2, dout, mod, norm_g, tmm)

    packed = jnp.concatenate([d_shift, d_scale, d_gate, d_norm_g, d_conv_b, d_ln_g, d_ln_b, g_final, sq_sum], axis=1)
    gathered = _allgather_small(packed, "gather_partials")
    total = _sum_devices(gathered)
    seg = lambda k, n=1: total[:, k * D:(k + n) * D]
    g_b_ada, g_norm_g, g_conv_b, g_ln_g, g_ln_b, g_final_g = seg(0, 3), seg(3), seg(4), seg(5), seg(6), seg(7)
    loss = (0.5 / D) * jnp.sum(seg(8))
    dmod_all = gathered[:, 0, 0:3 * D]
    dmod_cols = lax.dynamic_slice(dmod_all, (0, me * n_ada), (N_DEV, n_ada))
    g_w_ada, d_w_ada, nm_w_ada, nv_w_ada = _w_ada_update(c_all.T, dmod_cols, w_ada[0], m_w_ada[0], v_w_ada[0])

    small = {}
    for name, g, w, m, v in (("norm_g", g_norm_g, norm_g, m_norm_g, v_norm_g), ("b_ada", g_b_ada, b_ada, m_b_ada, v_b_ada),
                             ("conv_b", g_conv_b, conv_b, m_conv_b, v_conv_b), ("conv_ln_g", g_ln_g, conv_ln_g, m_conv_ln_g, v_conv_ln_g),
                             ("conv_ln_b", g_ln_b, conv_ln_b, m_conv_ln_b, v_conv_ln_b),
                             ("final_g", g_final_g, final_g.reshape(1, D), m_final_g.reshape(1, D), v_final_g.reshape(1, D))):
        small[name] = (g,) + tuple(_adamw_small(g, w, m, v, "adamw_" + name))

    slots = _dw_in_scatter(ht, dproj, (dw_co, dw_ao, dw_o, dconv_w), block_of(SCATTER_ORDER), tkw)
    big = {
        "w_in": _sum_adamw(slots[0], w_in[0], m_w_in[0], v_w_in[0], 256, "adamw_w_in"),
        "w_conv_out": _sum_adamw(slots[1], w_conv_out[0], m_w_conv_out[0], v_w_conv_out[0], 128, "adamw_w_conv_out"),
        "w_att_out": _sum_adamw(slots[2], w_att_out[0], m_w_att_out[0], v_w_att_out[0], 512, "adamw_w_att_out"),
        "w_o": _sum_adamw(slots[3], w_o[0], m_w_o[0], v_w_o[0], 128, "adamw_w_o"),
        "conv_w": [r[:CONV_K] for r in _sum_adamw(slots[4], pad_taps(conv_w), pad_taps(m_conv_w), pad_taps(v_conv_w), CONV_KP, "adamw_conv_w")],
    }
    big["w_ada"] = (g_w_ada, d_w_ada, nm_w_ada, nv_w_ada)

    order = ("norm_g", "w_ada", "b_ada", "w_in", "conv_w", "conv_b", "conv_ln_g", "conv_ln_b", "w_conv_out", "w_att_out", "w_o", "final_g")
    lead = lambda name, a: a.reshape(D) if name == "final_g" else (a[None] if name in big else a)
    result = {**small, **big}
    outs = [loss, grad_x[None]]
    for field in range(4):
        outs += [lead(name, result[name][field]) for name in order]
    return tuple(outs)
```
